```python
import math
import jax, jax.numpy as jnp
from jax import lax
import numpy as np

D_MODEL = 1024
BATCH = 2
SEQ = 8192
DEPTH = 2

N_META = 16
D_MIX = D_MODEL
EPS = 1e-6
N_ATTN_HEADS = 8
ATTN_HEAD_DIM = 64
D_ATTN = N_ATTN_HEADS * ATTN_HEAD_DIM
Q_RANK = 256
KV_RANK = 128
N_IDX_HEADS = 4
IDX_DIM = 64
TOPK_MAX = 256
Q_BLOCK = 128
N_REL_BUCKETS = 32
REL_MAX_DIST = 128
N_GDN_HEADS = 4
GDN_HEAD_DIM = 128
D_GDN = N_GDN_HEADS * GDN_HEAD_DIM
CONV_WIDTH = 4
CHUNK = 64
IN_PROJ_SPLITS = (Q_RANK, KV_RANK, IDX_DIM, N_IDX_HEADS, D_ATTN, D_GDN, D_GDN, D_GDN, D_GDN, N_GDN_HEADS, N_GDN_HEADS)
D_IN_PROJ = Q_RANK + KV_RANK + IDX_DIM + N_IDX_HEADS + D_ATTN + 4 * D_GDN + 2 * N_GDN_HEADS

kernel_name = "hybrid_dsa_gdn_parallel_heads"


def _split_last(a, sizes):
    outs, start = [], 0
    for s in sizes:
        outs.append(a[..., start:start + s])
        start += s
    return outs


def _rms_norm(x, gain):
    xf = x.astype(jnp.float32)
    y = xf * lax.rsqrt(jnp.mean(xf * xf, axis=-1, keepdims=True) + EPS)
    return (y * gain.astype(jnp.float32)).astype(x.dtype)


def _l2_norm(x):
    return x * lax.rsqrt(jnp.sum(x * x, axis=-1, keepdims=True) + EPS)


def _pad_seq(a, before, after):
    widths = [(0, 0)] * a.ndim
    widths[1] = (before, after)
    return jnp.pad(a, widths)


def _t5_bucket(dist):
    n = jnp.maximum(dist, 0)
    max_exact = N_REL_BUCKETS // 2
    nf = jnp.maximum(n, 1).astype(jnp.float32)
    large = max_exact + (jnp.log(nf / max_exact) / math.log(REL_MAX_DIST / max_exact)
                         * (N_REL_BUCKETS - max_exact)).astype(jnp.int32)
    large = jnp.minimum(large, N_REL_BUCKETS - 1)
    return jnp.where(n < max_exact, n, large)


def _dsa_mixer(c_q, c_kv, k_idx, w_idx, g_cq, g_ckv, w_uq, w_ukv, w_q_idx, g_qn, g_kn, rel_table, topk):
    B, T, _ = c_q.shape
    cq = _rms_norm(c_q, g_cq)
    ckv = _rms_norm(c_kv, g_ckv)
    q = _rms_norm((cq @ w_uq).reshape(B, T, N_ATTN_HEADS, ATTN_HEAD_DIM), g_qn)
    k_lin, v = _split_last(ckv @ w_ukv, (D_ATTN, D_ATTN))
    k = _rms_norm(k_lin.reshape(B, T, N_ATTN_HEADS, ATTN_HEAD_DIM), g_kn)
    v = v.reshape(B, T, N_ATTN_HEADS, ATTN_HEAD_DIM)
    q_idx = (cq @ w_q_idx).reshape(B, T, N_IDX_HEADS, IDX_DIM)
    w_idx = w_idx * (N_IDX_HEADS ** -0.5)

    n_blk = -(-T // Q_BLOCK)
    Tp = n_blk * Q_BLOCK
    pad = Tp - T
    q_p, qi_p, wi_p = _pad_seq(q, 0, pad), _pad_seq(q_idx, 0, pad), _pad_seq(w_idx, 0, pad)
    k_p, v_p, ki_p = _pad_seq(k, 0, pad), _pad_seq(v, 0, pad), _pad_seq(k_idx, 0, pad)
    key_pos = jnp.arange(Tp, dtype=jnp.int32)
    scale = ATTN_HEAD_DIM ** -0.5

    def to_blocks(a):
        return jnp.moveaxis(a.reshape((B, n_blk, Q_BLOCK) + a.shape[2:]), 1, 0)

    def one_block(args):
        qb, qib, wb, t0 = args
        q_pos = t0 + jnp.arange(Q_BLOCK, dtype=jnp.int32)
        causal = key_pos[None, :] <= q_pos[:, None]
        idx_logits = jnp.einsum('bqhd,bsd->bqhs', qib, ki_p) * (IDX_DIM ** -0.5)
        idx_score = jnp.einsum('bqh,bqhs->bqs', wb, jax.nn.relu(idx_logits)).astype(jnp.float32)
        idx_score = jnp.where(causal[None], idx_score, -jnp.inf)
        _, sel = lax.top_k(idx_score, topk)
        valid = sel <= q_pos[None, :, None]
        kg = jax.vmap(lambda a, i: a[i])(k_p, sel)
        vg = jax.vmap(lambda a, i: a[i])(v_p, sel)
        bias = rel_table[_t5_bucket(q_pos[None, :, None] - sel)]
        logits = (jnp.einsum('bqhd,bqkhd->bqhk', qb, kg).astype(jnp.float32) * scale
                  + jnp.moveaxis(bias, -1, 2).astype(jnp.float32))
        logits = jnp.where(valid[:, :, None, :], logits, -jnp.inf)
        p = jax.nn.softmax(logits, axis=-1).astype(vg.dtype)
        return jnp.einsum('bqhk,bqkhd->bqhd', p, vg)

    t0s = jnp.arange(n_blk, dtype=jnp.int32) * Q_BLOCK
    out = lax.map(one_block, (to_blocks(q_p), to_blocks(qi_p), to_blocks(wi_p), t0s))
    return jnp.moveaxis(out, 0, 1).reshape(B, Tp, D_ATTN)[:, :T]


def _gdn_mixer(q, k, v, z, b, a, conv_w, a_log, dt_bias, g_out, n_front):
    B, T, _ = q.shape
    f32 = jnp.float32
    H, d = N_GDN_HEADS, GDN_HEAD_DIM
    qkv = jnp.concatenate([q, k, v], axis=-1).astype(f32)
    qkv = lax.conv_general_dilated(qkv, conv_w.astype(f32)[:, None, :], window_strides=(1,),
                                   padding=[(CONV_WIDTH - 1, 0)], dimension_numbers=('NWC', 'WIO', 'NWC'),
                                   feature_group_count=3 * D_GDN)
    qkv = jax.nn.silu(qkv)
    qc, kc, vc = _split_last(qkv, (D_GDN, D_GDN, D_GDN))
    qc = _l2_norm(qc.reshape(B, T, H, d)) * (d ** -0.5)
    kc = _l2_norm(kc.reshape(B, T, H, d))
    vc = vc.reshape(B, T, H, d)
    beta = jax.nn.sigmoid(b.astype(f32))
    g = -jnp.exp(a_log.astype(f32)) * jax.nn.softplus(a.astype(f32) + dt_bias.astype(f32))

    qc, kc, vc, beta, g = (_pad_seq(t, n_front, 0) for t in (qc, kc, vc, beta, g))
    Tc = T + n_front
    N = Tc // CHUNK

    def chunks4(t):
        return t.reshape(B, N, CHUNK, H, d).transpose(0, 3, 1, 2, 4)

    def chunks3(t):
        return t.reshape(B, N, CHUNK, H).transpose(0, 3, 1, 2)

    qc, kc, vc = chunks4(qc), chunks4(kc), chunks4(vc)
    beta, g = chunks3(beta), chunks3(g)
    decay = jnp.cumsum(g, axis=-1)
    tri = jnp.tril(jnp.ones((CHUNK, CHUNK), dtype=bool))
    strict = jnp.tril(jnp.ones((CHUNK, CHUNK), dtype=bool), -1)
    gamma = jnp.exp(jnp.where(tri, decay[..., :, None] - decay[..., None, :], -jnp.inf))
    kb = kc * beta[..., None]
    amat = jnp.eye(CHUNK, dtype=f32) + jnp.where(strict, jnp.einsum('bhncd,bhnsd->bhncs', kb, kc) * gamma, 0.0)
    u = lax.linalg.triangular_solve(amat, vc * beta[..., None], left_side=True, lower=True, unit_diagonal=True)
    w = lax.linalg.triangular_solve(amat, kb * jnp.exp(decay)[..., None], left_side=True, lower=True,
                                    unit_diagonal=True)
    aqk = jnp.where(tri, jnp.einsum('bhncd,bhnsd->bhncs', qc, kc) * gamma, 0.0)
    q_dec = qc * jnp.exp(decay)[..., None]
    k_dec = kc * jnp.exp(decay[..., -1:] - decay)[..., None]
    chunk_decay = jnp.exp(decay[..., -1])

    def step(S, xs):
        u_i, w_i, qd_i, kd_i, aqk_i, cd_i = xs
        v_new = u_i - jnp.einsum('bhcd,bhde->bhce', w_i, S)
        o = jnp.einsum('bhcd,bhde->bhce', qd_i, S) + jnp.einsum('bhcs,bhse->bhce', aqk_i, v_new)
        S = S * cd_i[..., None, None] + jnp.einsum('bhcd,bhce->bhde', kd_i, v_new)
        return S, o

    xs = tuple(jnp.moveaxis(t, 2, 0) for t in (u, w, q_dec, k_dec, aqk, chunk_decay))
    S0 = jnp.zeros((B, H, d, d), f32)
    _, o = lax.scan(step, S0, xs)
    o = o.transpose(1, 0, 3, 2, 4).reshape(B, Tc, H, d)[:, n_front:]
    o = _rms_norm(o, g_out) * jax.nn.silu(z.astype(f32).reshape(B, T, H, d))
    return o.reshape(B, T, D_GDN)


def _hybrid_layer(h, norm_gain, w_in, cq_norm_gain, ckv_norm_gain, w_uq, w_ukv, w_q_idx, q_norm_gain,
                  k_norm_gain, conv_w, a_log, dt_bias, gdn_norm_gain, w_out, rel_table, topk):
    hn = _rms_norm(h, norm_gain)
    proj = hn @ w_in
    (c_q, c_kv, k_idx, w_idx, z_attn, q_g, k_g, v_g, z_g, b_g, a_g) = _split_last(proj, IN_PROJ_SPLITS)
    o_attn = _dsa_mixer(c_q, c_kv, k_idx, w_idx, cq_norm_gain, ckv_norm_gain, w_uq, w_ukv, w_q_idx,
                        q_norm_gain, k_norm_gain, rel_table, topk) * jax.nn.silu(z_attn)
    o_gdn = _gdn_mixer(q_g, k_g, v_g, z_g, b_g, a_g, conv_w, a_log, dt_bias, gdn_norm_gain,
                       (-N_META) % CHUNK).astype(h.dtype)
    y = jnp.concatenate([o_attn.astype(h.dtype), o_gdn], axis=-1) @ w_out
    return h + y


def setup_inputs(seed: int = 0) -> dict:
    key = jax.random.key(seed)
    ks = jax.random.split(key, 20)
    nrm = jax.random.normal
    x = nrm(ks[0], (BATCH, SEQ, D_MODEL), jnp.float32)
    meta_tokens = nrm(ks[1], (N_META, D_MODEL), jnp.float32)
    rel_bias_table = 0.5 * nrm(ks[2], (N_REL_BUCKETS, N_ATTN_HEADS), jnp.float32)
    norm_gain = 1.0 + 0.02 * nrm(ks[3], (DEPTH, D_MODEL), jnp.float32)
    w_in = nrm(ks[4], (DEPTH, D_MODEL, D_IN_PROJ), jnp.float32) * D_MODEL ** -0.5
    cq_norm_gain = 1.0 + 0.02 * nrm(ks[5], (DEPTH, Q_RANK), jnp.float32)
    ckv_norm_gain = 1.0 + 0.02 * nrm(ks[6], (DEPTH, KV_RANK), jnp.float32)
    w_uq = nrm(ks[7], (DEPTH, Q_RANK, D_ATTN), jnp.float32) * Q_RANK ** -0.5
    w_ukv = nrm(ks[8], (DEPTH, KV_RANK, 2 * D_ATTN), jnp.float32) * KV_RANK ** -0.5
    w_q_idx = nrm(ks[9], (DEPTH, Q_RANK, N_IDX_HEADS * IDX_DIM), jnp.float32) * Q_RANK ** -0.5
    q_norm_gain = 1.0 + 0.02 * nrm(ks[10], (DEPTH, ATTN_HEAD_DIM), jnp.float32)
    k_norm_gain = 1.0 + 0.02 * nrm(ks[11], (DEPTH, ATTN_HEAD_DIM), jnp.float32)
    conv_w = nrm(ks[12], (DEPTH, CONV_WIDTH, 3 * D_GDN), jnp.float32) * CONV_WIDTH ** -0.5
    a_log = jnp.log(jax.random.uniform(ks[13], (DEPTH, N_GDN_HEADS), jnp.float32, minval=1.0, maxval=16.0))
    dt = jnp.exp(jax.random.uniform(ks[14], (DEPTH, N_GDN_HEADS), jnp.float32,
                                    minval=math.log(1e-3), maxval=math.log(1e-1)))
    dt_bias = dt + jnp.log(-jnp.expm1(-dt))
    gdn_norm_gain = 1.0 + 0.02 * nrm(ks[15], (DEPTH, GDN_HEAD_DIM), jnp.float32)
    w_out = nrm(ks[16], (DEPTH, D_MIX, D_MODEL), jnp.float32) * D_MIX ** -0.5
    return {"x": x, "meta_tokens": meta_tokens, "rel_bias_table": rel_bias_table, "norm_gain": norm_gain,
            "w_in": w_in, "cq_norm_gain": cq_norm_gain, "ckv_norm_gain": ckv_norm_gain, "w_uq": w_uq,
            "w_ukv": w_ukv, "w_q_idx": w_q_idx, "q_norm_gain": q_norm_gain, "k_norm_gain": k_norm_gain,
            "conv_w": conv_w, "a_log": a_log, "dt_bias": dt_bias, "gdn_norm_gain": gdn_norm_gain,
            "w_out": w_out}


def reference(x, meta_tokens, rel_bias_table, norm_gain, w_in, cq_norm_gain, ckv_norm_gain, w_uq, w_ukv,
              w_q_idx, q_norm_gain, k_norm_gain, conv_w, a_log, dt_bias, gdn_norm_gain, w_out):
    B, S, _ = x.shape
    topk = min(TOPK_MAX, S // 4)
    meta = jnp.broadcast_to(meta_tokens[None].astype(x.dtype), (B, N_META, D_MODEL))
    h = jnp.concatenate([meta, x], axis=1)
    for l in range(DEPTH):
        h = _hybrid_layer(h, norm_gain[l], w_in[l], cq_norm_gain[l], ckv_norm_gain[l], w_uq[l], w_ukv[l],
                          w_q_idx[l], q_norm_gain[l], k_norm_gain[l], conv_w[l], a_log[l], dt_bias[l],
                          gdn_norm_gain[l], w_out[l], rel_bias_table, topk)
    return h[:, N_META:]
```

```python
import functools
import math

import jax
import jax.numpy as jnp
from jax import lax
from jax.experimental import pallas as pl
from jax.experimental.pallas import tpu as pltpu

F32 = jnp.float32
BF16 = jnp.bfloat16
I32 = jnp.int32
HIGHEST = lax.Precision.HIGHEST

D_MODEL = 1024
N_META = 16
EPS = 1e-6
N_ATTN_HEADS = 8
ATTN_HEAD_DIM = 64
D_ATTN = N_ATTN_HEADS * ATTN_HEAD_DIM
Q_RANK = 256
KV_RANK = 128
N_IDX_HEADS = 4
IDX_DIM = 64
TOPK_MAX = 256
N_REL_BUCKETS = 32
REL_MAX_DIST = 128
N_GDN_HEADS = 4
GDN_HEAD_DIM = 128
D_GDN = N_GDN_HEADS * GDN_HEAD_DIM
CONV_WIDTH = 4

LANES = 128
ROW_TILES = (5, 4, 3, 2, 1)
HALO_ROWS = 8
D_PACKED = 3 * D_GDN + 512 + D_ATTN + D_GDN
COL_SMALL = 3
COL_Z_ATTN = 4
COL_Z_GDN = 5
MASKED_LOGIT = -1e30
KEY_MIN = -2 ** 31
VMEM_LIMIT = 56 * 1024 * 1024

NT_DIMS = (((1,), (1,)), ((), ()))


def _dot(a, b):
    return jnp.dot(a, b, preferred_element_type=F32)


def _dot_nt(a, b):
    return lax.dot_general(a, b, NT_DIMS, preferred_element_type=F32)


def _dot_hi(a, b):
    return jnp.dot(a, b, preferred_element_type=F32, precision=HIGHEST)


def _sigmoid(x):
    return 1.0 / (1.0 + jnp.exp(-x))


def _silu(x):
    return x * _sigmoid(x)


def _row_block(tp):
    tiles = tp // LANES
    return LANES * next(d for d in ROW_TILES if tiles % d == 0)


def _params(*sem):
    return pltpu.CompilerParams(dimension_semantics=sem, vmem_limit_bytes=VMEM_LIMIT)


def _bias_kernel(table_ref, out_ref):
    row = lax.broadcasted_iota(I32, (LANES, LANES), 0)
    col = lax.broadcasted_iota(I32, (LANES, LANES), 1)
    max_exact = N_REL_BUCKETS // 2
    for kind in range(3):
        dist = col - row + (2 - kind) * LANES
        n = jnp.maximum(dist, 0)
        nf = jnp.maximum(n, 1).astype(F32)
        large = max_exact + (jnp.log(nf / max_exact) / math.log(REL_MAX_DIST / max_exact)
                             * (N_REL_BUCKETS - max_exact)).astype(I32)
        large = jnp.minimum(large, N_REL_BUCKETS - 1)
        bucket = jnp.where(n < max_exact, n, large)
        for h in range(N_ATTN_HEADS):
            tile = jnp.zeros((LANES, LANES), F32)
            for b in range(N_REL_BUCKETS):
                tile = jnp.where(bucket == b, table_ref[b, h], tile)
            out_ref[h, kind] = tile


def _bias_tiles(rel_table):
    return pl.pallas_call(
        _bias_kernel,
        out_shape=jax.ShapeDtypeStruct((N_ATTN_HEADS, 3, LANES, LANES), F32),
        in_specs=[pl.BlockSpec(memory_space=pltpu.SMEM)],
        out_specs=pl.BlockSpec(memory_space=pltpu.VMEM),
        name="rel_bias_tiles",
    )(rel_table)


def _in_proj_kernel(h_ref, gain_ref, w_ref, wrows_ref, proj_ref, rows_ref):
    x = h_ref[...]
    y = x * lax.rsqrt(jnp.mean(x * x, axis=-1, keepdims=True) + EPS)
    hn = (y * gain_ref[...]).astype(BF16)
    proj_ref[...] = _dot(hn, w_ref[...])
    rows_ref[...] = _dot_nt(wrows_ref[...], hn)


def _in_proj(h2, gain, w_packed, w_rows, rb):
    n_rows = h2.shape[0]
    grid = (n_rows // rb,)
    return pl.pallas_call(
        _in_proj_kernel,
        out_shape=(jax.ShapeDtypeStruct((n_rows, D_PACKED), F32),
                   jax.ShapeDtypeStruct((16, n_rows), F32)),
        grid=grid,
        in_specs=[pl.BlockSpec((rb, D_MODEL), lambda i: (i, 0)),
                  pl.BlockSpec((1, D_MODEL), lambda i: (0, 0)),
                  pl.BlockSpec((D_MODEL, D_PACKED), lambda i: (0, 0)),
                  pl.BlockSpec((16, D_MODEL), lambda i: (0, 0))],
        out_specs=(pl.BlockSpec((rb, D_PACKED), lambda i: (i, 0)),
                   pl.BlockSpec((16, rb), lambda i: (0, i))),
        compiler_params=_params("parallel"),
        name="in_proj",
    )(h2, gain, w_packed, w_rows)


def _rms_rows(x, gain):
    return x * lax.rsqrt(jnp.mean(x * x, axis=-1, keepdims=True) + EPS) * gain


def _head_rms(x, gain, sel, sel_t):
    ms = _dot_hi(x * x, sel) * (1.0 / ATTN_HEAD_DIM)
    inv = _dot_hi(lax.rsqrt(ms + EPS), sel_t)
    return x * inv * gain


def _zero_other_head(x_pair, which):
    lane = lax.broadcasted_iota(I32, x_pair.shape, 1)
    keep = (lane >= which * ATTN_HEAD_DIM) & (lane < (which + 1) * ATTN_HEAD_DIM)
    return jnp.where(keep, x_pair, 0.0)


def _dsa_prep_kernel(sm_ref, gcq_ref, gckv_ref, wuq_ref, wqi_ref, wuk_ref, wuvt_ref, gq_ref, gk_ref,
                     sel_ref, selt_ref, qpad_ref, k_ref, vt_ref, qipad_ref, kidx_ref):
    sm = sm_ref[...]
    cq = _rms_rows(sm[:, :Q_RANK], gcq_ref[...]).astype(BF16)
    ckv = _rms_rows(sm[:, Q_RANK:Q_RANK + KV_RANK], gckv_ref[...]).astype(BF16)
    kidx_ref[...] = sm[:, Q_RANK + KV_RANK:].astype(BF16)
    sel, sel_t = sel_ref[...], selt_ref[...]
    q = _head_rms(_dot(cq, wuq_ref[...]), gq_ref[...], sel, sel_t) * (ATTN_HEAD_DIM ** -0.5)
    for h in range(N_ATTN_HEADS):
        pair = q[:, (h // 2) * LANES:(h // 2 + 1) * LANES]
        qpad_ref[:, h * LANES:(h + 1) * LANES] = _zero_other_head(pair, h % 2).astype(BF16)
    k_ref[...] = _head_rms(_dot(ckv, wuk_ref[...]), gk_ref[...], sel, sel_t).astype(BF16)
    vt_ref[...] = _dot_nt(wuvt_ref[...], ckv).astype(BF16)
    qi = _dot(cq, wqi_ref[...])
    for h in range(N_IDX_HEADS):
        pair = qi[:, (h // 2) * LANES:(h // 2 + 1) * LANES]
        qipad_ref[:, h * LANES:(h + 1) * LANES] = _zero_other_head(pair, h % 2).astype(BF16)


def _dsa_prep(proj, lw, batch, tp, rb):
    n_rows = proj.shape[0]
    nb = tp // rb
    const = lambda shape: pl.BlockSpec(shape, lambda b, i: (0, 0))
    row_spec = lambda width: pl.BlockSpec((rb, width), lambda b, i: (b * nb + i, 0))
    return pl.pallas_call(
        _dsa_prep_kernel,
        out_shape=(jax.ShapeDtypeStruct((n_rows, N_ATTN_HEADS * LANES), BF16),
                   jax.ShapeDtypeStruct((n_rows, D_ATTN), BF16),
                   jax.ShapeDtypeStruct((batch, D_ATTN, tp), BF16),
                   jax.ShapeDtypeStruct((n_rows, N_IDX_HEADS * LANES), BF16),
                   jax.ShapeDtypeStruct((n_rows, LANES), BF16)),
        grid=(batch, nb),
        in_specs=[pl.BlockSpec((rb, 512), lambda b, i: (b * nb + i, COL_SMALL)),
                  const((1, Q_RANK)), const((1, KV_RANK)),
                  const((Q_RANK, D_ATTN)), const((Q_RANK, N_IDX_HEADS * IDX_DIM)),
                  const((KV_RANK, D_ATTN)), const((D_ATTN, KV_RANK)),
                  const((1, D_ATTN)), const((1, D_ATTN)),
                  const((D_ATTN, LANES)), const((LANES, D_ATTN))],
        out_specs=(row_spec(N_ATTN_HEADS * LANES), row_spec(D_ATTN),
                   pl.BlockSpec((None, D_ATTN, rb), lambda b, i: (b, 0, i)),
                   row_spec(N_IDX_HEADS * LANES), row_spec(LANES)),
        compiler_params=_params("parallel", "parallel"),
        name="dsa_prep",
    )(proj, lw["g_cq"], lw["g_ckv"], lw["w_uq"], lw["w_qi"], lw["w_uk"], lw["w_uvt"], lw["g_q"], lw["g_k"],
      lw["head_sel"], lw["head_sel_t"])


def _float_order_key(x):
    bits = lax.bitcast_convert_type(x, I32)
    bits = jnp.where(bits == KEY_MIN, 0, bits)
    return bits ^ ((bits >> 31) & 0x7FFFFFFF)


def _dsa_kernel(qipad_ref, rows_ref, qpad_ref, kidx_ref, k_ref, vt_ref, bias_ref, o_ref,
                key_scr, m_scr, l_scr, acc_scr, *, topk):
    i = pl.program_id(1)
    n_kb = i + 1
    t0 = i * LANES
    row = lax.broadcasted_iota(I32, (LANES, LANES), 0)
    col = lax.broadcasted_iota(I32, (LANES, LANES), 1)
    w_idx = rows_ref[0:N_IDX_HEADS, :] * (N_IDX_HEADS ** -0.5 * IDX_DIM ** -0.5)

    def key_block(j):
        return pl.multiple_of(j * LANES, LANES)

    def causal(j):
        return (j * LANES + row) <= (t0 + col)

    def score_body(j, carry):
        s0 = key_block(j)
        kb = kidx_ref[pl.ds(s0, LANES), :]
        score = jnp.zeros((LANES, LANES), F32)
        for h in range(N_IDX_HEADS):
            logits = _dot_nt(kb, qipad_ref[:, h * LANES:(h + 1) * LANES])
            score = score + jnp.maximum(logits, 0.0) * w_idx[h:h + 1, :]
        key_scr[pl.ds(s0, LANES), :] = jnp.where(causal(j), _float_order_key(score), KEY_MIN)
        return carry

    lax.fori_loop(0, n_kb, score_body, 0)

    def count_ge(cand):
        def body(j, acc):
            kb = key_scr[pl.ds(key_block(j), LANES), :]
            ind = jnp.where(kb >= cand, 1, 0)
            return acc + jnp.sum(ind.reshape(LANES // 8, 8, LANES), axis=0)
        acc = lax.fori_loop(0, n_kb, body, jnp.zeros((8, LANES), I32))
        return jnp.sum(acc, axis=0, keepdims=True)

    zero_row = jnp.zeros((1, LANES), I32)
    prefix = jnp.where(count_ge(zero_row) >= topk, 0, KEY_MIN)

    def bit_body(b, prefix):
        cand = prefix | (1 << (30 - b))
        return jnp.where(count_ge(cand) >= topk, cand, prefix)

    tau = lax.fori_loop(0, 31, bit_body, prefix)
    n_above = count_ge(jnp.where(tau == 2 ** 31 - 1, tau, tau + 1))
    n_above = jnp.where(tau == 2 ** 31 - 1, 0, n_above)
    need = (topk - n_above).astype(F32)

    m_scr[...] = jnp.full(m_scr.shape, MASKED_LOGIT, F32)
    l_scr[...] = jnp.zeros(l_scr.shape, F32)
    acc_scr[...] = jnp.zeros(acc_scr.shape, F32)
    lower_tri = jnp.where(row >= col, 1.0, 0.0).astype(BF16)

    def attn_body(j, tie_carry):
        s0 = key_block(j)
        kb = key_scr[pl.ds(s0, LANES), :]
        tie = kb == tau
        tie_rank = _dot(lower_tri, jnp.where(tie, 1.0, 0.0).astype(BF16)) + tie_carry
        take = (tie & (tie_rank <= need)) | (kb > tau)
        mask_add = jnp.where(take & causal(j), 0.0, MASKED_LOGIT)
        kind = jnp.clip(j - i + 2, 0, 2)
        for h in range(N_ATTN_HEADS):
            k_pair = k_ref[pl.ds(s0, LANES), (h // 2) * LANES:(h // 2 + 1) * LANES]
            logits = _dot_nt(k_pair, qpad_ref[:, h * LANES:(h + 1) * LANES])
            logits = logits + bias_ref[h, kind] + mask_add
            m_old = m_scr[h:h + 1, :]
            m_new = jnp.maximum(m_old, jnp.max(logits, axis=0, keepdims=True))
            alpha = jnp.exp(m_old - m_new)
            p = jnp.exp(logits - m_new)
            l_scr[h:h + 1, :] = alpha * l_scr[h:h + 1, :] + jnp.sum(p, axis=0, keepdims=True)
            m_scr[h:h + 1, :] = m_new
            v_t = vt_ref[h * ATTN_HEAD_DIM:(h + 1) * ATTN_HEAD_DIM, pl.ds(s0, LANES)]
            rows_h = slice(h * ATTN_HEAD_DIM, (h + 1) * ATTN_HEAD_DIM)
            acc_scr[rows_h, :] = acc_scr[rows_h, :] * alpha + _dot(v_t, p.astype(BF16))
        return tie_rank[LANES - 1:LANES, :]

    lax.fori_loop(0, n_kb, attn_body, jnp.zeros((1, LANES), F32))

    for h in range(N_ATTN_HEADS):
        rows_h = slice(h * ATTN_HEAD_DIM, (h + 1) * ATTN_HEAD_DIM)
        acc_scr[rows_h, :] = acc_scr[rows_h, :] / l_scr[h:h + 1, :]
    o_ref[...] = acc_scr[...].T


def _dsa_attention(qipad, rows, qpad, kidx, k, vt, bias, batch, tp, topk):
    n_rows = qpad.shape[0]
    nqb = tp // LANES
    q_spec = lambda width: pl.BlockSpec((LANES, width), lambda b, i: (b * nqb + i, 0))
    return pl.pallas_call(
        functools.partial(_dsa_kernel, topk=topk),
        out_shape=jax.ShapeDtypeStruct((n_rows, D_ATTN), F32),
        grid=(batch, nqb),
        in_specs=[q_spec(N_IDX_HEADS * LANES),
                  pl.BlockSpec((16, LANES), lambda b, i: (0, b * nqb + i)),
                  q_spec(N_ATTN_HEADS * LANES),
                  pl.BlockSpec((tp, LANES), lambda b, i: (b, 0)),
                  pl.BlockSpec((tp, D_ATTN), lambda b, i: (b, 0)),
                  pl.BlockSpec((None, D_ATTN, tp), lambda b, i: (b, 0, 0)),
                  pl.BlockSpec((N_ATTN_HEADS, 3, LANES, LANES), lambda b, i: (0, 0, 0, 0))],
        out_specs=q_spec(D_ATTN),
        scratch_shapes=[pltpu.VMEM((tp, LANES), I32),
                        pltpu.VMEM((N_ATTN_HEADS, LANES), F32),
                        pltpu.VMEM((N_ATTN_HEADS, LANES), F32),
                        pltpu.VMEM((D_ATTN, LANES), F32)],
        compiler_params=_params("parallel", "parallel"),
        name="dsa_attention",
    )(qipad, rows, qpad, kidx, k, vt, bias)


def _gdn_prep_kernel(x_ref, halo_ref, cw_ref, rows_ref, alog_ref, dtb_ref, q_ref, k_ref, v_ref, bg_ref, buf):
    first = pl.program_id(1) == 0
    rb = x_ref.shape[0]
    buf[0:HALO_ROWS, :] = jnp.where(first, 0.0, halo_ref[...])
    buf[HALO_ROWS:, :] = x_ref[...]
    acc = jnp.zeros((rb, 3 * D_GDN), F32)
    for tap in range(CONV_WIDTH):
        start = HALO_ROWS - (CONV_WIDTH - 1) + tap
        acc = acc + cw_ref[tap:tap + 1, :] * buf[start:start + rb, :]
    y = _silu(acc)
    for h in range(N_GDN_HEADS):
        cols = slice(h * GDN_HEAD_DIM, (h + 1) * GDN_HEAD_DIM)
        qh = y[:, cols]
        kh = y[:, D_GDN + h * GDN_HEAD_DIM:D_GDN + (h + 1) * GDN_HEAD_DIM]
        q_ref[:, cols] = (qh * lax.rsqrt(jnp.sum(qh * qh, axis=-1, keepdims=True) + EPS)
                          * (GDN_HEAD_DIM ** -0.5))
        k_ref[:, cols] = kh * lax.rsqrt(jnp.sum(kh * kh, axis=-1, keepdims=True) + EPS)
    v_ref[...] = y[:, 2 * D_GDN:]
    rows = rows_ref[...]
    beta = _sigmoid(rows[4:8, :])
    a = rows[8:12, :] + dtb_ref[...]
    softplus = jnp.maximum(a, 0.0) + jnp.log1p(jnp.exp(-jnp.abs(a)))
    bg_ref[0:4, :] = beta
    bg_ref[4:8, :] = -jnp.exp(alog_ref[...]) * softplus


def _gdn_prep(proj, rows, lw, batch, tp, rb):
    n_rows = proj.shape[0]
    nb = tp // rb
    halo_per_block = rb // HALO_ROWS
    row_spec = pl.BlockSpec((rb, D_GDN), lambda b, i: (b * nb + i, 0))
    return pl.pallas_call(
        _gdn_prep_kernel,
        out_shape=(jax.ShapeDtypeStruct((n_rows, D_GDN), F32),) * 3
        + (jax.ShapeDtypeStruct((8, n_rows), F32),),
        grid=(batch, nb),
        in_specs=[pl.BlockSpec((rb, 3 * D_GDN), lambda b, i: (b * nb + i, 0)),
                  pl.BlockSpec((HALO_ROWS, 3 * D_GDN),
                               lambda b, i: (jnp.maximum((b * nb + i) * halo_per_block - 1, 0), 0)),
                  pl.BlockSpec((CONV_WIDTH, 3 * D_GDN), lambda b, i: (0, 0)),
                  pl.BlockSpec((16, rb), lambda b, i: (0, b * nb + i)),
                  pl.BlockSpec((N_GDN_HEADS, 1), lambda b, i: (0, 0)),
                  pl.BlockSpec((N_GDN_HEADS, 1), lambda b, i: (0, 0))],
        out_specs=(row_spec, row_spec, row_spec,
                   pl.BlockSpec((8, rb), lambda b, i: (0, b * nb + i))),
        scratch_shapes=[pltpu.VMEM((HALO_ROWS + rb, 3 * D_GDN), F32)],
        compiler_params=_params("parallel", "parallel"),
        name="gdn_prep",
    )(proj, proj, lw["conv_w"], rows, lw["a_log"], lw["dt_bias"])


def _gdn_chunk_kernel(q_ref, k_ref, v_ref, bg_ref, m_ref, n_ref, p_ref, r_ref, cd_ref):
    c = LANES
    bg = bg_ref[...]
    lane8 = lax.broadcasted_iota(I32, (8, c), 1)
    dec = bg
    shift = 1
    while shift < c:
        dec = dec + jnp.where(lane8 >= shift, pltpu.roll(dec, shift, 1), 0.0)
        shift *= 2
    row = lax.broadcasted_iota(I32, (c, c), 0)
    col = lax.broadcasted_iota(I32, (c, c), 1)
    tri = row >= col
    strict = row > col
    eye = jnp.where(row == col, 1.0, 0.0)
    for h in range(N_GDN_HEADS):
        cols = slice(h * GDN_HEAD_DIM, (h + 1) * GDN_HEAD_DIM)
        d_row = jnp.broadcast_to(dec[4 + h:5 + h, :], (c, c))
        d_col = d_row.T
        beta_col = jnp.broadcast_to(bg[h:h + 1, :], (c, c)).T
        d_last = d_row[:, c - 1:c]
        q, k, v = q_ref[:, cols], k_ref[:, cols], v_ref[:, cols]
        gamma = jnp.exp(jnp.where(tri, d_col - d_row, MASKED_LOGIT))
        kb = k * beta_col
        k16 = k.astype(BF16)
        nil = jnp.where(strict, _dot_nt(kb.astype(BF16), k16) * gamma, 0.0)
        inv = eye - nil
        power = _dot_hi(nil, nil)
        steps = int(math.log2(c)) - 1
        for it in range(steps):
            inv = inv + _dot_hi(inv, power)
            if it + 1 < steps:
                power = _dot_hi(power, power)
        exp_d = jnp.exp(d_col)
        u = _dot_hi(inv, v * beta_col).astype(BF16)
        w = _dot_hi(inv, kb * exp_d).astype(BF16)
        aqk = jnp.where(tri, _dot_nt(q.astype(BF16), k16) * gamma, 0.0).astype(BF16)
        kd_t = (k * jnp.exp(d_last - d_col)).T.astype(BF16)
        m_ref[h] = (-_dot(kd_t, w)).astype(BF16)
        n_ref[h] = _dot(kd_t, u)
        p_ref[h] = (q * exp_d - _dot(aqk, w)).astype(BF16)
        r_ref[h] = _dot(aqk, u)
        cd_ref[h:h + 1, :] = jnp.exp(d_last[0:1, :] + jnp.zeros((1, c), F32))


def _gdn_chunks(qn, kn, vv, bg):
    n_rows = qn.shape[0]
    nc = n_rows // LANES
    tok = pl.BlockSpec((LANES, D_GDN), lambda c: (c, 0))
    mat = pl.BlockSpec((None, N_GDN_HEADS, LANES, LANES), lambda c: (c, 0, 0, 0))
    mat_shape = lambda dt: jax.ShapeDtypeStruct((nc, N_GDN_HEADS, LANES, LANES), dt)
    return pl.pallas_call(
        _gdn_chunk_kernel,
        out_shape=(mat_shape(BF16), mat_shape(F32), mat_shape(BF16), mat_shape(F32),
                   jax.ShapeDtypeStruct((nc, N_GDN_HEADS, LANES), F32)),
        grid=(nc,),
        in_specs=[tok, tok, tok, pl.BlockSpec((8, LANES), lambda c: (0, c))],
        out_specs=(mat, mat, mat, mat, pl.BlockSpec((None, N_GDN_HEADS, LANES), lambda c: (c, 0, 0))),
        compiler_params=_params("parallel"),
        name="gdn_chunks",
    )(qn, kn, vv, bg)


def _gdn_scan_kernel(m_ref, n_ref, p_ref, r_ref, cd_ref, o_ref, s_scr, *, batch):
    @pl.when(pl.program_id(0) == 0)
    def _():
        s_scr[...] = jnp.zeros(s_scr.shape, F32)

    for b in range(batch):
        for h in range(N_GDN_HEADS):
            s = s_scr[b, h]
            s16 = s.astype(BF16)
            o_ref[b, :, h * GDN_HEAD_DIM:(h + 1) * GDN_HEAD_DIM] = _dot(p_ref[b, h], s16) + r_ref[b, h]
            s_scr[b, h] = s * cd_ref[b, h:h + 1, :] + _dot(m_ref[b, h], s16) + n_ref[b, h]


def _gdn_scan(m_mat, n_mat, p_mat, r_mat, cd, batch, tp):
    nc = tp // LANES
    shape5 = lambda a: a.reshape(batch, nc, N_GDN_HEADS, LANES, LANES)
    mat = pl.BlockSpec((batch, None, N_GDN_HEADS, LANES, LANES), lambda c: (0, c, 0, 0, 0))
    return pl.pallas_call(
        functools.partial(_gdn_scan_kernel, batch=batch),
        out_shape=jax.ShapeDtypeStruct((batch, tp, D_GDN), F32),
        grid=(nc,),
        in_specs=[mat, mat, mat, mat,
                  pl.BlockSpec((batch, None, N_GDN_HEADS, LANES), lambda c: (0, c, 0, 0))],
        out_specs=pl.BlockSpec((batch, LANES, D_GDN), lambda c: (0, c, 0)),
        scratch_shapes=[pltpu.VMEM((batch, N_GDN_HEADS, LANES, LANES), F32)],
        compiler_params=_params("arbitrary"),
        name="gdn_scan",
    )(shape5(m_mat), shape5(n_mat), shape5(p_mat), shape5(r_mat), cd.reshape(batch, nc, N_GDN_HEADS, LANES))


def _out_proj_kernel(h_ref, oa_ref, za_ref, og_ref, zg_ref, gain_ref, w_ref, out_ref):
    attn = (oa_ref[...] * _silu(za_ref[...])).astype(BF16)
    y = _dot(attn, w_ref[0:D_ATTN, :])
    og, zg = og_ref[...], zg_ref[...]
    for h in range(N_GDN_HEADS):
        cols = slice(h * GDN_HEAD_DIM, (h + 1) * GDN_HEAD_DIM)
        gated = (_rms_rows(og[:, cols], gain_ref[...]) * _silu(zg[:, cols])).astype(BF16)
        y = y + _dot(gated, w_ref[D_ATTN + h * GDN_HEAD_DIM:D_ATTN + (h + 1) * GDN_HEAD_DIM, :])
    out_ref[...] = h_ref[...] + y


def _out_proj(h2, o_attn, proj, o_gdn, gain, w_out, rb):
    n_rows = h2.shape[0]
    blk = lambda width, col: pl.BlockSpec((rb, width), lambda i: (i, col))
    return pl.pallas_call(
        _out_proj_kernel,
        out_shape=jax.ShapeDtypeStruct((n_rows, D_MODEL), F32),
        grid=(n_rows // rb,),
        in_specs=[blk(D_MODEL, 0), blk(D_ATTN, 0), blk(D_ATTN, COL_Z_ATTN), blk(D_GDN, 0),
                  blk(D_GDN, COL_Z_GDN),
                  pl.BlockSpec((1, GDN_HEAD_DIM), lambda i: (0, 0)),
                  pl.BlockSpec((D_MODEL, D_MODEL), lambda i: (0, 0))],
        out_specs=blk(D_MODEL, 0),
        compiler_params=_params("parallel"),
        name="out_proj",
    )(h2, o_attn, proj, o_gdn, proj, gain, w_out)


def _pack_layer(norm_gain, w_in, cq_gain, ckv_gain, w_uq, w_ukv, w_q_idx, q_gain, k_gain, conv_w, a_log,
                dt_bias, gdn_gain, w_out):
    o = 0
    parts = {}
    for name, size in (("c_q", Q_RANK), ("c_kv", KV_RANK), ("k_idx", IDX_DIM), ("w_idx", N_IDX_HEADS),
                       ("z_attn", D_ATTN), ("qkv_g", 3 * D_GDN), ("z_g", D_GDN), ("b", N_GDN_HEADS),
                       ("a", N_GDN_HEADS)):
        parts[name] = w_in[:, o:o + size]
        o += size
    w_packed = jnp.concatenate([parts["qkv_g"], parts["c_q"], parts["c_kv"], parts["k_idx"], parts["k_idx"],
                                parts["z_attn"], parts["z_g"]], axis=1).astype(BF16)
    w_rows = jnp.concatenate([parts["w_idx"], parts["b"], parts["a"],
                              jnp.zeros((D_MODEL, 4), w_in.dtype)], axis=1).T.astype(BF16)
    head_of_col = jnp.arange(D_ATTN) // ATTN_HEAD_DIM
    head_sel = (head_of_col[:, None] == jnp.arange(LANES)[None, :]).astype(F32)
    return dict(
        gain=norm_gain[None, :], w_packed=w_packed, w_rows=w_rows,
        g_cq=cq_gain[None, :], g_ckv=ckv_gain[None, :],
        w_uq=w_uq.astype(BF16), w_qi=w_q_idx.astype(BF16),
        w_uk=w_ukv[:, :D_ATTN].astype(BF16), w_uvt=w_ukv[:, D_ATTN:].T.astype(BF16),
        g_q=jnp.tile(q_gain, N_ATTN_HEADS)[None, :], g_k=jnp.tile(k_gain, N_ATTN_HEADS)[None, :],
        head_sel=head_sel, head_sel_t=head_sel.T,
        conv_w=conv_w, a_log=a_log[:, None], dt_bias=dt_bias[:, None],
        gdn_gain=gdn_gain[None, :], w_out=w_out.astype(BF16))


def _layer(h2, lw, bias, batch, tp, topk):
    rb = _row_block(tp)
    proj, rows = _in_proj(h2, lw["gain"], lw["w_packed"], lw["w_rows"], rb)
    qpad, k, vt, qipad, kidx = _dsa_prep(proj, lw, batch, tp, rb)
    o_attn = _dsa_attention(qipad, rows, qpad, kidx, k, vt, bias, batch, tp, topk)
    qn, kn, vv, bg = _gdn_prep(proj, rows, lw, batch, tp, rb)
    m_mat, n_mat, p_mat, r_mat, cd = _gdn_chunks(qn, kn, vv, bg)
    o_gdn = _gdn_scan(m_mat, n_mat, p_mat, r_mat, cd, batch, tp).reshape(batch * tp, D_GDN)
    return _out_proj(h2, o_attn, proj, o_gdn, lw["gdn_gain"], lw["w_out"], rb)


def _forward(x, meta_tokens, rel_bias_table, layer_weights, topk):
    batch, seq, _ = x.shape
    t = seq + N_META
    tp = -(-t // LANES) * LANES
    meta = jnp.broadcast_to(meta_tokens[None].astype(x.dtype), (batch, N_META, D_MODEL))
    h = jnp.concatenate([meta, x, jnp.zeros((batch, tp - t, D_MODEL), x.dtype)], axis=1)
    h2 = h.reshape(batch * tp, D_MODEL)
    bias = _bias_tiles(rel_bias_table)
    for lw in layer_weights:
        h2 = _layer(h2, lw, bias, batch, tp, topk)
    return h2.reshape(batch, tp, D_MODEL)[:, N_META:t]


def kernel(x, meta_tokens, rel_bias_table, norm_gain, w_in, cq_norm_gain, ckv_norm_gain, w_uq, w_ukv, w_q_idx,
           q_norm_gain, k_norm_gain, conv_w, a_log, dt_bias, gdn_norm_gain, w_out):
    depth = norm_gain.shape[0]
    topk = min(TOPK_MAX, x.shape[1] // 4)
    layers = [_pack_layer(norm_gain[l], w_in[l], cq_norm_gain[l], ckv_norm_gain[l], w_uq[l], w_ukv[l],
                          w_q_idx[l], q_norm_gain[l], k_norm_gain[l], conv_w[l], a_log[l], dt_bias[l],
                          gdn_norm_gain[l], w_out[l]) for l in range(depth)]
    return _forward(x, meta_tokens, rel_bias_table, layers, topk)
```

```python
import functools
import math

import jax
import jax.numpy as jnp
from jax import lax
from jax.experimental import pallas as pl
from jax.experimental.pallas import tpu as pltpu

F32 = jnp.float32
BF16 = jnp.bfloat16
I32 = jnp.int32
HIGHEST = lax.Precision.HIGHEST

D_MODEL = 1024
N_META = 16
EPS = 1e-6
N_ATTN_HEADS = 8
ATTN_HEAD_DIM = 64
D_ATTN = N_ATTN_HEADS * ATTN_HEAD_DIM
Q_RANK = 256
KV_RANK = 128
N_IDX_HEADS = 4
IDX_DIM = 64
TOPK_MAX = 256
N_REL_BUCKETS = 32
REL_MAX_DIST = 128
N_GDN_HEADS = 4
GDN_HEAD_DIM = 128
D_GDN = N_GDN_HEADS * GDN_HEAD_DIM
CONV_WIDTH = 4

LANES = 128
KEY_TILE = 256
COUNT_TILE = 512
ROW_TILES = (5, 4, 3, 2, 1)
HALO_ROWS = 8
D_PACKED = 3 * D_GDN + 512 + D_ATTN + D_GDN
COL_SMALL = 3
COL_Z_ATTN = 4
COL_Z_GDN = 5
MASKED_LOGIT = -1e30
KEY_MIN = -2 ** 31
VMEM_LIMIT = 56 * 1024 * 1024

NT_DIMS = (((1,), (1,)), ((), ()))


def _dot(a, b):
    return jnp.dot(a, b, preferred_element_type=F32)


def _dot_nt(a, b):
    return lax.dot_general(a, b, NT_DIMS, preferred_element_type=F32)


def _dot_hi(a, b):
    return jnp.dot(a, b, preferred_element_type=F32, precision=HIGHEST)


def _sigmoid(x):
    return 1.0 / (1.0 + jnp.exp(-x))


def _silu(x):
    return x * _sigmoid(x)


def _row_block(tp):
    tiles = tp // LANES
    return LANES * next(d for d in ROW_TILES if tiles % d == 0)


def _params(*sem):
    return pltpu.CompilerParams(dimension_semantics=sem, vmem_limit_bytes=VMEM_LIMIT)


def _bias_kernel(table_ref, out_ref):
    row = lax.broadcasted_iota(I32, (LANES, LANES), 0)
    col = lax.broadcasted_iota(I32, (LANES, LANES), 1)
    max_exact = N_REL_BUCKETS // 2
    for kind in range(3):
        dist = col - row + (2 - kind) * LANES
        n = jnp.maximum(dist, 0)
        nf = jnp.maximum(n, 1).astype(F32)
        large = max_exact + (jnp.log(nf / max_exact) / math.log(REL_MAX_DIST / max_exact)
                             * (N_REL_BUCKETS - max_exact)).astype(I32)
        large = jnp.minimum(large, N_REL_BUCKETS - 1)
        bucket = jnp.where(n < max_exact, n, large)
        for h in range(N_ATTN_HEADS):
            tile = jnp.zeros((LANES, LANES), F32)
            for b in range(N_REL_BUCKETS):
                tile = jnp.where(bucket == b, table_ref[b, h], tile)
            out_ref[kind, :, h * LANES:(h + 1) * LANES] = tile


def _bias_tiles(rel_table):
    return pl.pallas_call(
        _bias_kernel,
        out_shape=jax.ShapeDtypeStruct((3, LANES, N_ATTN_HEADS * LANES), F32),
        in_specs=[pl.BlockSpec(memory_space=pltpu.SMEM)],
        out_specs=pl.BlockSpec(memory_space=pltpu.VMEM),
        name="rel_bias_tiles",
    )(rel_table)


def _in_proj_kernel(h_ref, gain_ref, w_ref, wrows_ref, proj_ref, rows_ref):
    x = h_ref[...]
    y = x * lax.rsqrt(jnp.mean(x * x, axis=-1, keepdims=True) + EPS)
    hn = (y * gain_ref[...]).astype(BF16)
    proj_ref[...] = _dot(hn, w_ref[...])
    rows_ref[...] = _dot_nt(wrows_ref[...], hn)


def _in_proj(h2, gain, w_packed, w_rows, rb):
    n_rows = h2.shape[0]
    grid = (n_rows // rb,)
    return pl.pallas_call(
        _in_proj_kernel,
        out_shape=(jax.ShapeDtypeStruct((n_rows, D_PACKED), F32),
                   jax.ShapeDtypeStruct((16, n_rows), F32)),
        grid=grid,
        in_specs=[pl.BlockSpec((rb, D_MODEL), lambda i: (i, 0)),
                  pl.BlockSpec((1, D_MODEL), lambda i: (0, 0)),
                  pl.BlockSpec((D_MODEL, D_PACKED), lambda i: (0, 0)),
                  pl.BlockSpec((16, D_MODEL), lambda i: (0, 0))],
        out_specs=(pl.BlockSpec((rb, D_PACKED), lambda i: (i, 0)),
                   pl.BlockSpec((16, rb), lambda i: (0, i))),
        compiler_params=_params("parallel"),
        name="in_proj",
    )(h2, gain, w_packed, w_rows)


def _rms_rows(x, gain):
    return x * lax.rsqrt(jnp.mean(x * x, axis=-1, keepdims=True) + EPS) * gain


def _head_rms(x, gain, sel, sel_t):
    ms = _dot_hi(x * x, sel) * (1.0 / ATTN_HEAD_DIM)
    inv = _dot_hi(lax.rsqrt(ms + EPS), sel_t)
    return x * inv * gain


def _dsa_prep_kernel(sm_ref, gcq_ref, gckv_ref, wuqt_ref, wqit_ref, wuk_ref, wuvt_ref, gq_ref, gk_ref,
                     sel_ref, selt_ref, qt_ref, k_ref, vt_ref, qit_ref, kidx_ref):
    sm = sm_ref[...]
    rb = sm.shape[0]
    cq = _rms_rows(sm[:, :Q_RANK], gcq_ref[...]).astype(BF16)
    ckv = _rms_rows(sm[:, Q_RANK:Q_RANK + KV_RANK], gckv_ref[...]).astype(BF16)
    kidx_ref[...] = sm[:, Q_RANK + KV_RANK:].astype(BF16)
    q3 = _dot_nt(wuqt_ref[...], cq).reshape(N_ATTN_HEADS, ATTN_HEAD_DIM, rb)
    q3 = q3 * lax.rsqrt(jnp.mean(q3 * q3, axis=1, keepdims=True) + EPS) * gq_ref[...][None]
    qt_ref[...] = (q3 * (ATTN_HEAD_DIM ** -0.5)).reshape(D_ATTN, rb).astype(BF16)
    k_ref[...] = _head_rms(_dot(ckv, wuk_ref[...]), gk_ref[...], sel_ref[...], selt_ref[...]).astype(BF16)
    vt_ref[...] = _dot_nt(wuvt_ref[...], ckv).astype(BF16)
    qit_ref[...] = _dot_nt(wqit_ref[...], cq).astype(BF16)


def _dsa_prep(proj, lw, batch, tp, rb):
    n_rows = proj.shape[0]
    nb = tp // rb
    d_idx = N_IDX_HEADS * IDX_DIM
    const = lambda shape: pl.BlockSpec(shape, lambda b, i: (0, 0))
    row_spec = lambda width: pl.BlockSpec((rb, width), lambda b, i: (b * nb + i, 0))
    col_spec = lambda height: pl.BlockSpec((None, height, rb), lambda b, i: (b, 0, i))
    return pl.pallas_call(
        _dsa_prep_kernel,
        out_shape=(jax.ShapeDtypeStruct((batch, D_ATTN, tp), BF16),
                   jax.ShapeDtypeStruct((n_rows, D_ATTN), BF16),
                   jax.ShapeDtypeStruct((batch, D_ATTN, tp), BF16),
                   jax.ShapeDtypeStruct((batch, d_idx, tp), BF16),
                   jax.ShapeDtypeStruct((n_rows, LANES), BF16)),
        grid=(batch, nb),
        in_specs=[pl.BlockSpec((rb, 512), lambda b, i: (b * nb + i, COL_SMALL)),
                  const((1, Q_RANK)), const((1, KV_RANK)),
                  const((D_ATTN, Q_RANK)), const((d_idx, Q_RANK)),
                  const((KV_RANK, D_ATTN)), const((D_ATTN, KV_RANK)),
                  const((ATTN_HEAD_DIM, 1)), const((1, D_ATTN)),
                  const((D_ATTN, LANES)), const((LANES, D_ATTN))],
        out_specs=(col_spec(D_ATTN), row_spec(D_ATTN), col_spec(D_ATTN), col_spec(d_idx), row_spec(LANES)),
        compiler_params=_params("parallel", "parallel"),
        name="dsa_prep",
    )(proj, lw["g_cq"], lw["g_ckv"], lw["w_uqt"], lw["w_qit"], lw["w_uk"], lw["w_uvt"], lw["g_q_col"], lw["g_k"],
      lw["head_sel"], lw["head_sel_t"])


def _float_order_key(x):
    bits = lax.bitcast_convert_type(x, I32)
    bits = jnp.where(bits == KEY_MIN, 0, bits)
    return bits ^ ((bits >> 31) & 0x7FFFFFFF)


def _dsa_block(i, qit_ref, rows_ref, qt_ref, kidx_ref, k_ref, vt_ref, bias_ref, tri_ref, o_ref,
               key_scr, wi_scr, wq_scr, s_scr, p_scr, m_scr, l_scr, acc_scr, *, topk):
    t0 = i * LANES
    n_kt = i // 2 + 1
    n_ct = (n_kt + 1) // 2
    hd = ATTN_HEAD_DIM
    pair_w = 2 * LANES

    zeros_hd = jnp.zeros((hd, LANES), BF16)
    for h in range(N_IDX_HEADS):
        wi_scr[0:IDX_DIM, h * LANES:(h + 1) * LANES] = qit_ref[h * IDX_DIM:(h + 1) * IDX_DIM, :]
    wi_scr[IDX_DIM:, :] = jnp.zeros((LANES - IDX_DIM, N_IDX_HEADS * LANES), BF16)
    for p in range(N_ATTN_HEADS // 2):
        wq_scr[p, 0:hd, 0:LANES] = qt_ref[2 * p * hd:(2 * p + 1) * hd, :]
        wq_scr[p, 0:hd, LANES:] = zeros_hd
        wq_scr[p, hd:, 0:LANES] = zeros_hd
        wq_scr[p, hd:, LANES:] = qt_ref[(2 * p + 1) * hd:(2 * p + 2) * hd, :]

    row = lax.broadcasted_iota(I32, (KEY_TILE, LANES), 0)
    col = lax.broadcasted_iota(I32, (KEY_TILE, LANES), 1)
    w_idx = rows_ref[0:N_IDX_HEADS, :] * (N_IDX_HEADS ** -0.5 * IDX_DIM ** -0.5)

    def key_tile(j):
        return pl.multiple_of(j * KEY_TILE, KEY_TILE)

    def causal(j):
        return (j * KEY_TILE + row) <= (t0 + col)

    def score_body(j, carry):
        s0 = key_tile(j)
        logits = _dot(kidx_ref[pl.ds(s0, KEY_TILE), :], wi_scr[...])
        score = jnp.zeros((KEY_TILE, LANES), F32)
        for h in range(N_IDX_HEADS):
            score = score + jnp.maximum(logits[:, h * LANES:(h + 1) * LANES], 0.0) * w_idx[h:h + 1, :]
        key_scr[pl.ds(s0, KEY_TILE), :] = jnp.where(causal(j), _float_order_key(score), KEY_MIN)
        return carry

    lax.fori_loop(0, n_kt, score_body, 0)

    @pl.when(n_kt % 2 == 1)
    def _():
        key_scr[pl.ds(key_tile(n_kt), KEY_TILE), :] = jnp.full((KEY_TILE, LANES), KEY_MIN, I32)

    def count_ge(cand):
        def body(j, acc):
            kb = key_scr[pl.ds(pl.multiple_of(j * COUNT_TILE, COUNT_TILE), COUNT_TILE), :]
            ind = jnp.where(kb >= cand, 1, 0)
            return acc + jnp.sum(ind.reshape(COUNT_TILE // 8, 8, LANES), axis=0)
        acc = lax.fori_loop(0, n_ct, body, jnp.zeros((8, LANES), I32))
        return jnp.sum(acc, axis=0, keepdims=True)

    prefix = jnp.where(count_ge(jnp.zeros((1, LANES), I32)) >= topk, 0, KEY_MIN)

    def bit_body(b, prefix):
        cand = prefix | (1 << (30 - b))
        return jnp.where(count_ge(cand) >= topk, cand, prefix)

    tau = lax.fori_loop(0, 31, bit_body, prefix)
    key_max = 2 ** 31 - 1
    n_above = jnp.where(tau == key_max, 0, count_ge(jnp.where(tau == key_max, tau, tau + 1)))
    need = (topk - n_above).astype(F32)

    m_scr[...] = jnp.full(m_scr.shape, MASKED_LOGIT, F32)
    l_scr[...] = jnp.zeros(l_scr.shape, F32)
    acc_scr[...] = jnp.zeros(acc_scr.shape, F32)

    def attn_body(j, tie_carry):
        s0 = key_tile(j)
        kb = key_scr[pl.ds(s0, KEY_TILE), :]
        tie = kb == tau
        tie_rank = _dot(tri_ref[...], jnp.where(tie, 1.0, 0.0).astype(BF16)) + tie_carry
        take = (tie & (tie_rank <= need)) | (kb > tau)
        mask_add = jnp.where(take & causal(j), 0.0, MASKED_LOGIT)
        for p in range(N_ATTN_HEADS // 2):
            s_scr[:, p * pair_w:(p + 1) * pair_w] = _dot(
                k_ref[pl.ds(s0, KEY_TILE), p * LANES:(p + 1) * LANES], wq_scr[p])
        kind_top = jnp.clip(2 * j - i + 2, 0, 2)
        kind_bot = jnp.clip(2 * j - i + 3, 0, 2)
        alphas = []
        for h in range(N_ATTN_HEADS):
            cols = slice(h * LANES, (h + 1) * LANES)
            bias = jnp.concatenate([bias_ref[kind_top, :, cols], bias_ref[kind_bot, :, cols]], axis=0)
            logits = s_scr[:, cols] + bias + mask_add
            m_old = m_scr[h:h + 1, :]
            m_new = jnp.maximum(m_old, jnp.max(logits, axis=0, keepdims=True))
            alpha = jnp.exp(m_old - m_new)
            prob = jnp.exp(logits - m_new)
            l_scr[h:h + 1, :] = alpha * l_scr[h:h + 1, :] + jnp.sum(prob, axis=0, keepdims=True)
            m_scr[h:h + 1, :] = m_new
            p_scr[:, cols] = prob.astype(BF16)
            alphas.append(alpha)
        for p in range(N_ATTN_HEADS // 2):
            pv = _dot(vt_ref[p * 2 * hd:(p + 1) * 2 * hd, pl.ds(s0, KEY_TILE)],
                      p_scr[:, p * pair_w:(p + 1) * pair_w])
            for half in range(2):
                h = 2 * p + half
                rows_h = slice(h * hd, (h + 1) * hd)
                acc_scr[rows_h, :] = (acc_scr[rows_h, :] * alphas[h]
                                      + pv[half * hd:(half + 1) * hd, half * LANES:(half + 1) * LANES])
        return tie_rank[KEY_TILE - 1:KEY_TILE, :]

    lax.fori_loop(0, n_kt, attn_body, jnp.zeros((1, LANES), F32))

    for h in range(N_ATTN_HEADS):
        rows_h = slice(h * hd, (h + 1) * hd)
        acc_scr[rows_h, :] = acc_scr[rows_h, :] / l_scr[h:h + 1, :]
    o_ref[...] = acc_scr[...].T


def _dsa_kernel(*refs, topk, t_valid):
    o_ref = refs[8]
    i = pl.program_id(1)
    is_real = i * LANES < t_valid

    @pl.when(is_real)
    def _():
        _dsa_block(i, *refs, topk=topk)

    @pl.when(jnp.logical_not(is_real))
    def _():
        o_ref[...] = jnp.zeros(o_ref.shape, F32)


def _dsa_attention(qit, rows, qt, kidx, k, vt, bias, tri, batch, tp, t_valid, topk):
    n_rows = k.shape[0]
    nqb = tp // LANES
    d_idx = N_IDX_HEADS * IDX_DIM
    n_pairs = N_ATTN_HEADS // 2
    key_rows = -(-tp // COUNT_TILE) * COUNT_TILE
    q_cols = lambda height: pl.BlockSpec((None, height, LANES), lambda b, i: (b, 0, i))
    return pl.pallas_call(
        functools.partial(_dsa_kernel, topk=topk, t_valid=t_valid),
        out_shape=jax.ShapeDtypeStruct((n_rows, D_ATTN), F32),
        grid=(batch, nqb),
        in_specs=[q_cols(d_idx),
                  pl.BlockSpec((16, LANES), lambda b, i: (0, b * nqb + i)),
                  q_cols(D_ATTN),
                  pl.BlockSpec((tp, LANES), lambda b, i: (b, 0)),
                  pl.BlockSpec((tp, D_ATTN), lambda b, i: (b, 0)),
                  pl.BlockSpec((None, D_ATTN, tp), lambda b, i: (b, 0, 0)),
                  pl.BlockSpec((3, LANES, N_ATTN_HEADS * LANES), lambda b, i: (0, 0, 0)),
                  pl.BlockSpec((KEY_TILE, KEY_TILE), lambda b, i: (0, 0))],
        out_specs=pl.BlockSpec((LANES, D_ATTN), lambda b, i: (b * nqb + i, 0)),
        scratch_shapes=[pltpu.VMEM((key_rows, LANES), I32),
                        pltpu.VMEM((LANES, N_IDX_HEADS * LANES), BF16),
                        pltpu.VMEM((n_pairs, LANES, 2 * LANES), BF16),
                        pltpu.VMEM((KEY_TILE, N_ATTN_HEADS * LANES), F32),
                        pltpu.VMEM((KEY_TILE, N_ATTN_HEADS * LANES), BF16),
                        pltpu.VMEM((N_ATTN_HEADS, LANES), F32),
                        pltpu.VMEM((N_ATTN_HEADS, LANES), F32),
                        pltpu.VMEM((D_ATTN, LANES), F32)],
        compiler_params=_params("parallel", "parallel"),
        name="dsa_attention",
    )(qit, rows, qt, kidx, k, vt, bias, tri)


def _gdn_prep_kernel(x_ref, halo_ref, cw_ref, rows_ref, alog_ref, dtb_ref, q_ref, k_ref, v_ref, bg_ref, buf):
    first = pl.program_id(1) == 0
    rb = x_ref.shape[0]
    buf[0:HALO_ROWS, :] = jnp.where(first, 0.0, halo_ref[...])
    buf[HALO_ROWS:, :] = x_ref[...]
    acc = jnp.zeros((rb, 3 * D_GDN), F32)
    for tap in range(CONV_WIDTH):
        start = HALO_ROWS - (CONV_WIDTH - 1) + tap
        acc = acc + cw_ref[tap:tap + 1, :] * buf[start:start + rb, :]
    y = _silu(acc)
    for h in range(N_GDN_HEADS):
        cols = slice(h * GDN_HEAD_DIM, (h + 1) * GDN_HEAD_DIM)
        qh = y[:, cols]
        kh = y[:, D_GDN + h * GDN_HEAD_DIM:D_GDN + (h + 1) * GDN_HEAD_DIM]
        q_ref[:, cols] = (qh * lax.rsqrt(jnp.sum(qh * qh, axis=-1, keepdims=True) + EPS)
                          * (GDN_HEAD_DIM ** -0.5))
        k_ref[:, cols] = kh * lax.rsqrt(jnp.sum(kh * kh, axis=-1, keepdims=True) + EPS)
    v_ref[...] = y[:, 2 * D_GDN:]
    rows = rows_ref[...]
    beta = _sigmoid(rows[4:8, :])
    a = rows[8:12, :] + dtb_ref[...]
    softplus = jnp.maximum(a, 0.0) + jnp.log1p(jnp.exp(-jnp.abs(a)))
    bg_ref[0:4, :] = beta
    bg_ref[4:8, :] = -jnp.exp(alog_ref[...]) * softplus


def _gdn_prep(proj, rows, lw, batch, tp, rb):
    n_rows = proj.shape[0]
    nb = tp // rb
    halo_per_block = rb // HALO_ROWS
    row_spec = pl.BlockSpec((rb, D_GDN), lambda b, i: (b * nb + i, 0))
    return pl.pallas_call(
        _gdn_prep_kernel,
        out_shape=(jax.ShapeDtypeStruct((n_rows, D_GDN), F32),) * 3
        + (jax.ShapeDtypeStruct((8, n_rows), F32),),
        grid=(batch, nb),
        in_specs=[pl.BlockSpec((rb, 3 * D_GDN), lambda b, i: (b * nb + i, 0)),
                  pl.BlockSpec((HALO_ROWS, 3 * D_GDN),
                               lambda b, i: (jnp.maximum((b * nb + i) * halo_per_block - 1, 0), 0)),
                  pl.BlockSpec((CONV_WIDTH, 3 * D_GDN), lambda b, i: (0, 0)),
                  pl.BlockSpec((16, rb), lambda b, i: (0, b * nb + i)),
                  pl.BlockSpec((N_GDN_HEADS, 1), lambda b, i: (0, 0)),
                  pl.BlockSpec((N_GDN_HEADS, 1), lambda b, i: (0, 0))],
        out_specs=(row_spec, row_spec, row_spec,
                   pl.BlockSpec((8, rb), lambda b, i: (0, b * nb + i))),
        scratch_shapes=[pltpu.VMEM((HALO_ROWS + rb, 3 * D_GDN), F32)],
        compiler_params=_params("parallel", "parallel"),
        name="gdn_prep",
    )(proj, proj, lw["conv_w"], rows, lw["a_log"], lw["dt_bias"])


def _gdn_chunk_kernel(q_ref, k_ref, v_ref, bg_ref, m_ref, n_ref, p_ref, r_ref, cd_ref):
    c = LANES
    bg = bg_ref[...]
    lane8 = lax.broadcasted_iota(I32, (8, c), 1)
    dec = bg
    shift = 1
    while shift < c:
        dec = dec + jnp.where(lane8 >= shift, pltpu.roll(dec, shift, 1), 0.0)
        shift *= 2
    row = lax.broadcasted_iota(I32, (c, c), 0)
    col = lax.broadcasted_iota(I32, (c, c), 1)
    tri = row >= col
    strict = row > col
    eye = jnp.where(row == col, 1.0, 0.0)
    for h in range(N_GDN_HEADS):
        cols = slice(h * GDN_HEAD_DIM, (h + 1) * GDN_HEAD_DIM)
        d_row = jnp.broadcast_to(dec[4 + h:5 + h, :], (c, c))
        d_col = d_row.T
        beta_col = jnp.broadcast_to(bg[h:h + 1, :], (c, c)).T
        d_last = d_row[:, c - 1:c]
        q, k, v = q_ref[:, cols], k_ref[:, cols], v_ref[:, cols]
        gamma = jnp.exp(jnp.where(tri, d_col - d_row, MASKED_LOGIT))
        kb = k * beta_col
        k16 = k.astype(BF16)
        nil = jnp.where(strict, _dot_nt(kb.astype(BF16), k16) * gamma, 0.0)
        inv = eye - nil
        power = _dot_hi(nil, nil)
        steps = int(math.log2(c)) - 1
        for it in range(steps):
            inv = inv + _dot_hi(inv, power)
            if it + 1 < steps:
                power = _dot_hi(power, power)
        exp_d = jnp.exp(d_col)
        u = _dot_hi(inv, v * beta_col).astype(BF16)
        w = _dot_hi(inv, kb * exp_d).astype(BF16)
        aqk = jnp.where(tri, _dot_nt(q.astype(BF16), k16) * gamma, 0.0).astype(BF16)
        kd_t = (k * jnp.exp(d_last - d_col)).T.astype(BF16)
        m_ref[h] = (-_dot(kd_t, w)).astype(BF16)
        n_ref[h] = _dot(kd_t, u)
        p_ref[h] = (q * exp_d - _dot(aqk, w)).astype(BF16)
        r_ref[h] = _dot(aqk, u)
        cd_ref[h:h + 1, :] = jnp.exp(d_last[0:1, :] + jnp.zeros((1, c), F32))


def _gdn_chunks(qn, kn, vv, bg):
    n_rows = qn.shape[0]
    nc = n_rows // LANES
    tok = pl.BlockSpec((LANES, D_GDN), lambda c: (c, 0))
    mat = pl.BlockSpec((None, N_GDN_HEADS, LANES, LANES), lambda c: (c, 0, 0, 0))
    mat_shape = lambda dt: jax.ShapeDtypeStruct((nc, N_GDN_HEADS, LANES, LANES), dt)
    return pl.pallas_call(
        _gdn_chunk_kernel,
        out_shape=(mat_shape(BF16), mat_shape(F32), mat_shape(BF16), mat_shape(F32),
                   jax.ShapeDtypeStruct((nc, N_GDN_HEADS, LANES), F32)),
        grid=(nc,),
        in_specs=[tok, tok, tok, pl.BlockSpec((8, LANES), lambda c: (0, c))],
        out_specs=(mat, mat, mat, mat, pl.BlockSpec((None, N_GDN_HEADS, LANES), lambda c: (c, 0, 0))),
        compiler_params=_params("parallel"),
        name="gdn_chunks",
    )(qn, kn, vv, bg)


def _gdn_scan_kernel(m_ref, n_ref, p_ref, r_ref, cd_ref, o_ref, s_scr, *, batch):
    @pl.when(pl.program_id(0) == 0)
    def _():
        s_scr[...] = jnp.zeros(s_scr.shape, F32)

    for b in range(batch):
        for h in range(N_GDN_HEADS):
            s = s_scr[b, h]
            s16 = s.astype(BF16)
            o_ref[b, :, h * GDN_HEAD_DIM:(h + 1) * GDN_HEAD_DIM] = _dot(p_ref[b, h], s16) + r_ref[b, h]
            s_scr[b, h] = s * cd_ref[b, h:h + 1, :] + _dot(m_ref[b, h], s16) + n_ref[b, h]


def _gdn_scan(m_mat, n_mat, p_mat, r_mat, cd, batch, tp):
    nc = tp // LANES
    shape5 = lambda a: a.reshape(batch, nc, N_GDN_HEADS, LANES, LANES)
    mat = pl.BlockSpec((batch, None, N_GDN_HEADS, LANES, LANES), lambda c: (0, c, 0, 0, 0))
    return pl.pallas_call(
        functools.partial(_gdn_scan_kernel, batch=batch),
        out_shape=jax.ShapeDtypeStruct((batch, tp, D_GDN), F32),
        grid=(nc,),
        in_specs=[mat, mat, mat, mat,
                  pl.BlockSpec((batch, None, N_GDN_HEADS, LANES), lambda c: (0, c, 0, 0))],
        out_specs=pl.BlockSpec((batch, LANES, D_GDN), lambda c: (0, c, 0)),
        scratch_shapes=[pltpu.VMEM((batch, N_GDN_HEADS, LANES, LANES), F32)],
        compiler_params=_params("arbitrary"),
        name="gdn_scan",
    )(shape5(m_mat), shape5(n_mat), shape5(p_mat), shape5(r_mat), cd.reshape(batch, nc, N_GDN_HEADS, LANES))


def _out_proj_kernel(h_ref, oa_ref, za_ref, og_ref, zg_ref, gain_ref, w_ref, out_ref):
    attn = (oa_ref[...] * _silu(za_ref[...])).astype(BF16)
    y = _dot(attn, w_ref[0:D_ATTN, :])
    og, zg = og_ref[...], zg_ref[...]
    for h in range(N_GDN_HEADS):
        cols = slice(h * GDN_HEAD_DIM, (h + 1) * GDN_HEAD_DIM)
        gated = (_rms_rows(og[:, cols], gain_ref[...]) * _silu(zg[:, cols])).astype(BF16)
        y = y + _dot(gated, w_ref[D_ATTN + h * GDN_HEAD_DIM:D_ATTN + (h + 1) * GDN_HEAD_DIM, :])
    out_ref[...] = h_ref[...] + y


def _out_proj(h2, o_attn, proj, o_gdn, gain, w_out, rb):
    n_rows = h2.shape[0]
    blk = lambda width, col: pl.BlockSpec((rb, width), lambda i: (i, col))
    return pl.pallas_call(
        _out_proj_kernel,
        out_shape=jax.ShapeDtypeStruct((n_rows, D_MODEL), F32),
        grid=(n_rows // rb,),
        in_specs=[blk(D_MODEL, 0), blk(D_ATTN, 0), blk(D_ATTN, COL_Z_ATTN), blk(D_GDN, 0),
                  blk(D_GDN, COL_Z_GDN),
                  pl.BlockSpec((1, GDN_HEAD_DIM), lambda i: (0, 0)),
                  pl.BlockSpec((D_MODEL, D_MODEL), lambda i: (0, 0))],
        out_specs=blk(D_MODEL, 0),
        compiler_params=_params("parallel"),
        name="out_proj",
    )(h2, o_attn, proj, o_gdn, proj, gain, w_out)


def _pack_layer(norm_gain, w_in, cq_gain, ckv_gain, w_uq, w_ukv, w_q_idx, q_gain, k_gain, conv_w, a_log,
                dt_bias, gdn_gain, w_out):
    o = 0
    parts = {}
    for name, size in (("c_q", Q_RANK), ("c_kv", KV_RANK), ("k_idx", IDX_DIM), ("w_idx", N_IDX_HEADS),
                       ("z_attn", D_ATTN), ("qkv_g", 3 * D_GDN), ("z_g", D_GDN), ("b", N_GDN_HEADS),
                       ("a", N_GDN_HEADS)):
        parts[name] = w_in[:, o:o + size]
        o += size
    w_packed = jnp.concatenate([parts["qkv_g"], parts["c_q"], parts["c_kv"], parts["k_idx"], parts["k_idx"],
                                parts["z_attn"], parts["z_g"]], axis=1).astype(BF16)
    w_rows = jnp.concatenate([parts["w_idx"], parts["b"], parts["a"],
                              jnp.zeros((D_MODEL, 4), w_in.dtype)], axis=1).T.astype(BF16)
    head_of_col = jnp.arange(D_ATTN) // ATTN_HEAD_DIM
    head_sel = (head_of_col[:, None] == jnp.arange(LANES)[None, :]).astype(F32)
    return dict(
        gain=norm_gain[None, :], w_packed=w_packed, w_rows=w_rows,
        g_cq=cq_gain[None, :], g_ckv=ckv_gain[None, :],
        w_uqt=w_uq.T.astype(BF16), w_qit=w_q_idx.T.astype(BF16),
        w_uk=w_ukv[:, :D_ATTN].astype(BF16), w_uvt=w_ukv[:, D_ATTN:].T.astype(BF16),
        g_q_col=q_gain[:, None], g_k=jnp.tile(k_gain, N_ATTN_HEADS)[None, :],
        head_sel=head_sel, head_sel_t=head_sel.T,
        conv_w=conv_w, a_log=a_log[:, None], dt_bias=dt_bias[:, None],
        gdn_gain=gdn_gain[None, :], w_out=w_out.astype(BF16))


def _layer(h2, lw, bias, tri, batch, tp, t_valid, topk):
    rb = _row_block(tp)
    proj, rows = _in_proj(h2, lw["gain"], lw["w_packed"], lw["w_rows"], rb)
    qt, k, vt, qit, kidx = _dsa_prep(proj, lw, batch, tp, rb)
    o_attn = _dsa_attention(qit, rows, qt, kidx, k, vt, bias, tri, batch, tp, t_valid, topk)
    qn, kn, vv, bg = _gdn_prep(proj, rows, lw, batch, tp, rb)
    m_mat, n_mat, p_mat, r_mat, cd = _gdn_chunks(qn, kn, vv, bg)
    o_gdn = _gdn_scan(m_mat, n_mat, p_mat, r_mat, cd, batch, tp).reshape(batch * tp, D_GDN)
    return _out_proj(h2, o_attn, proj, o_gdn, lw["gdn_gain"], lw["w_out"], rb)


def _forward(x, meta_tokens, rel_bias_table, layer_weights, topk):
    batch, seq, _ = x.shape
    t = seq + N_META
    tp = -(-t // KEY_TILE) * KEY_TILE
    meta = jnp.broadcast_to(meta_tokens[None].astype(x.dtype), (batch, N_META, D_MODEL))
    h = jnp.concatenate([meta, x, jnp.zeros((batch, tp - t, D_MODEL), x.dtype)], axis=1)
    h2 = h.reshape(batch * tp, D_MODEL)
    bias = _bias_tiles(rel_bias_table)
    tri = jnp.tril(jnp.ones((KEY_TILE, KEY_TILE), BF16))
    for lw in layer_weights:
        h2 = _layer(h2, lw, bias, tri, batch, tp, t, topk)
    return h2.reshape(batch, tp, D_MODEL)[:, N_META:t]


def kernel(x, meta_tokens, rel_bias_table, norm_gain, w_in, cq_norm_gain, ckv_norm_gain, w_uq, w_ukv, w_q_idx,
           q_norm_gain, k_norm_gain, conv_w, a_log, dt_bias, gdn_norm_gain, w_out):
    depth = norm_gain.shape[0]
    topk = min(TOPK_MAX, x.shape[1] // 4)
    layers = [_pack_layer(norm_gain[l], w_in[l], cq_norm_gain[l], ckv_norm_gain[l], w_uq[l], w_ukv[l],
                          w_q_idx[l], q_norm_gain[l], k_norm_gain[l], conv_w[l], a_log[l], dt_bias[l],
                          gdn_norm_gain[l], w_out[l]) for l in range(depth)]
    return _forward(x, meta_tokens, rel_bias_table, layers, topk)
```

```python
import functools
import math

import jax
import jax.numpy as jnp
from jax import lax
from jax.experimental import pallas as pl
from jax.experimental.pallas import tpu as pltpu

F32 = jnp.float32
BF16 = jnp.bfloat16
I32 = jnp.int32
HIGHEST = lax.Precision.HIGHEST

D_MODEL = 1024
N_META = 16
EPS = 1e-6
N_ATTN_HEADS = 8
ATTN_HEAD_DIM = 64
D_ATTN = N_ATTN_HEADS * ATTN_HEAD_DIM
Q_RANK = 256
KV_RANK = 128
N_IDX_HEADS = 4
IDX_DIM = 64
TOPK_MAX = 256
N_REL_BUCKETS = 32
REL_MAX_DIST = 128
N_GDN_HEADS = 4
GDN_HEAD_DIM = 128
D_GDN = N_GDN_HEADS * GDN_HEAD_DIM
CONV_WIDTH = 4

LANES = 128
KEY_TILE = 256
COUNT_TILE = 512
BITS_PER_CHECK = 4
ROW_TILES = (5, 4, 3, 2, 1)
HALO_ROWS = 8
D_PACKED = 3 * D_GDN + 512 + D_ATTN + D_GDN
COL_SMALL = 3
COL_Z_ATTN = 4
COL_Z_GDN = 5
MASKED_LOGIT = -1e30
LOG2_E = math.log2(math.e)
KEY_MIN = -2 ** 31
VMEM_LIMIT = 56 * 1024 * 1024

NT_DIMS = (((1,), (1,)), ((), ()))


def _dot(a, b):
    return jnp.dot(a, b, preferred_element_type=F32)


def _dot_nt(a, b):
    return lax.dot_general(a, b, NT_DIMS, preferred_element_type=F32)


def _dot_hi(a, b):
    return jnp.dot(a, b, preferred_element_type=F32, precision=HIGHEST)


def _split_bf16(x):
    hi = x.astype(BF16)
    return hi, (x - hi.astype(F32)).astype(BF16)


def _dot_split(a_parts, b_parts):
    (a_hi, a_lo), (b_hi, b_lo) = a_parts, b_parts
    return _dot(a_hi, b_hi) + (_dot(a_hi, b_lo) + _dot(a_lo, b_hi))


def _sigmoid(x):
    return 1.0 / (1.0 + jnp.exp(-x))


def _silu(x):
    return x * _sigmoid(x)


def _row_block(tp):
    tiles = tp // LANES
    return LANES * next(d for d in ROW_TILES if tiles % d == 0)


def _params(*sem):
    return pltpu.CompilerParams(dimension_semantics=sem, vmem_limit_bytes=VMEM_LIMIT)


def _bias_kernel(table_ref, out_ref):
    row = lax.broadcasted_iota(I32, (LANES, LANES), 0)
    col = lax.broadcasted_iota(I32, (LANES, LANES), 1)
    max_exact = N_REL_BUCKETS // 2
    for kind in range(3):
        dist = col - row + (2 - kind) * LANES
        n = jnp.maximum(dist, 0)
        nf = jnp.maximum(n, 1).astype(F32)
        large = max_exact + (jnp.log(nf / max_exact) / math.log(REL_MAX_DIST / max_exact)
                             * (N_REL_BUCKETS - max_exact)).astype(I32)
        large = jnp.minimum(large, N_REL_BUCKETS - 1)
        bucket = jnp.where(n < max_exact, n, large)
        for h in range(N_ATTN_HEADS):
            tile = jnp.zeros((LANES, LANES), F32)
            for b in range(N_REL_BUCKETS):
                tile = jnp.where(bucket == b, table_ref[b, h], tile)
            far = table_ref[N_REL_BUCKETS - 1, h]
            out_ref[kind, :, h * LANES:(h + 1) * LANES] = (tile - far) * LOG2_E


def _bias_tiles(rel_table):
    return pl.pallas_call(
        _bias_kernel,
        out_shape=jax.ShapeDtypeStruct((3, LANES, N_ATTN_HEADS * LANES), F32),
        in_specs=[pl.BlockSpec(memory_space=pltpu.SMEM)],
        out_specs=pl.BlockSpec(memory_space=pltpu.VMEM),
        name="rel_bias_tiles",
    )(rel_table)


def _in_proj_kernel(h_ref, gain_ref, w_ref, wrows_ref, proj_ref, rows_ref):
    x = h_ref[...]
    y = x * lax.rsqrt(jnp.mean(x * x, axis=-1, keepdims=True) + EPS)
    hn = (y * gain_ref[...]).astype(BF16)
    proj_ref[...] = _dot(hn, w_ref[...])
    rows_ref[...] = _dot_nt(wrows_ref[...], hn)


def _in_proj(h2, gain, w_packed, w_rows, rb):
    n_rows = h2.shape[0]
    grid = (n_rows // rb,)
    return pl.pallas_call(
        _in_proj_kernel,
        out_shape=(jax.ShapeDtypeStruct((n_rows, D_PACKED), F32),
                   jax.ShapeDtypeStruct((16, n_rows), F32)),
        grid=grid,
        in_specs=[pl.BlockSpec((rb, D_MODEL), lambda i: (i, 0)),
                  pl.BlockSpec((1, D_MODEL), lambda i: (0, 0)),
                  pl.BlockSpec((D_MODEL, D_PACKED), lambda i: (0, 0)),
                  pl.BlockSpec((16, D_MODEL), lambda i: (0, 0))],
        out_specs=(pl.BlockSpec((rb, D_PACKED), lambda i: (i, 0)),
                   pl.BlockSpec((16, rb), lambda i: (0, i))),
        compiler_params=_params("parallel"),
        name="in_proj",
    )(h2, gain, w_packed, w_rows)


def _rms_rows(x, gain):
    return x * lax.rsqrt(jnp.mean(x * x, axis=-1, keepdims=True) + EPS) * gain


def _head_rms(x, gain, sel, sel_t):
    ms = _dot_hi(x * x, sel) * (1.0 / ATTN_HEAD_DIM)
    inv = _dot_hi(lax.rsqrt(ms + EPS), sel_t)
    return x * inv * gain


def _dsa_prep_kernel(sm_ref, gcq_ref, gckv_ref, wuqt_ref, wqit_ref, wuk_ref, wuvt_ref, gq_ref, gk_ref,
                     sel_ref, selt_ref, qt_ref, k_ref, vt_ref, qit_ref, kidx_ref):
    sm = sm_ref[...]
    rb = sm.shape[0]
    cq = _rms_rows(sm[:, :Q_RANK], gcq_ref[...]).astype(BF16)
    ckv = _rms_rows(sm[:, Q_RANK:Q_RANK + KV_RANK], gckv_ref[...]).astype(BF16)
    kidx_ref[...] = sm[:, Q_RANK + KV_RANK:].astype(BF16)
    q3 = _dot_nt(wuqt_ref[...], cq).reshape(N_ATTN_HEADS, ATTN_HEAD_DIM, rb)
    q3 = q3 * lax.rsqrt(jnp.mean(q3 * q3, axis=1, keepdims=True) + EPS) * gq_ref[...][None]
    qt_ref[...] = (q3 * (ATTN_HEAD_DIM ** -0.5 * LOG2_E)).reshape(D_ATTN, rb).astype(BF16)
    k_ref[...] = _head_rms(_dot(ckv, wuk_ref[...]), gk_ref[...], sel_ref[...], selt_ref[...]).astype(BF16)
    vt_ref[...] = _dot_nt(wuvt_ref[...], ckv).astype(BF16)
    qit_ref[...] = _dot_nt(wqit_ref[...], cq).astype(BF16)


def _dsa_prep(proj, lw, batch, tp, rb):
    n_rows = proj.shape[0]
    nb = tp // rb
    d_idx = N_IDX_HEADS * IDX_DIM
    const = lambda shape: pl.BlockSpec(shape, lambda b, i: (0, 0))
    row_spec = lambda width: pl.BlockSpec((rb, width), lambda b, i: (b * nb + i, 0))
    col_spec = lambda height: pl.BlockSpec((None, height, rb), lambda b, i: (b, 0, i))
    return pl.pallas_call(
        _dsa_prep_kernel,
        out_shape=(jax.ShapeDtypeStruct((batch, D_ATTN, tp), BF16),
                   jax.ShapeDtypeStruct((n_rows, D_ATTN), BF16),
                   jax.ShapeDtypeStruct((batch, D_ATTN, tp), BF16),
                   jax.ShapeDtypeStruct((batch, d_idx, tp), BF16),
                   jax.ShapeDtypeStruct((n_rows, LANES), BF16)),
        grid=(batch, nb),
        in_specs=[pl.BlockSpec((rb, 512), lambda b, i: (b * nb + i, COL_SMALL)),
                  const((1, Q_RANK)), const((1, KV_RANK)),
                  const((D_ATTN, Q_RANK)), const((d_idx, Q_RANK)),
                  const((KV_RANK, D_ATTN)), const((D_ATTN, KV_RANK)),
                  const((ATTN_HEAD_DIM, 1)), const((1, D_ATTN)),
                  const((D_ATTN, LANES)), const((LANES, D_ATTN))],
        out_specs=(col_spec(D_ATTN), row_spec(D_ATTN), col_spec(D_ATTN), col_spec(d_idx), row_spec(LANES)),
        compiler_params=_params("parallel", "parallel"),
        name="dsa_prep",
    )(proj, lw["g_cq"], lw["g_ckv"], lw["w_uqt"], lw["w_qit"], lw["w_uk"], lw["w_uvt"], lw["g_q_col"], lw["g_k"],
      lw["head_sel"], lw["head_sel_t"])


def _float_order_key(x):
    bits = lax.bitcast_convert_type(x, I32)
    bits = jnp.where(bits == KEY_MIN, 0, bits)
    return bits ^ ((bits >> 31) & 0x7FFFFFFF)


def _dsa_block(i, qit_ref, rows_ref, qt_ref, kidx_ref, k_ref, vt_ref, bias_ref, tri_ref, o_ref,
               key_scr, wi_scr, wq_scr, s_scr, p_scr, m_scr, l_scr, acc_scr, *, topk):
    t0 = i * LANES
    n_kt = i // 2 + 1
    n_ct = (n_kt + 1) // 2
    hd = ATTN_HEAD_DIM
    pair_w = 2 * LANES
    n_pairs = N_ATTN_HEADS // 2

    zeros_hd = jnp.zeros((hd, LANES), BF16)
    for h in range(N_IDX_HEADS):
        wi_scr[0:IDX_DIM, h * LANES:(h + 1) * LANES] = qit_ref[h * IDX_DIM:(h + 1) * IDX_DIM, :]
    wi_scr[IDX_DIM:, :] = jnp.zeros((LANES - IDX_DIM, N_IDX_HEADS * LANES), BF16)
    for p in range(n_pairs):
        wq_scr[p, 0:hd, 0:LANES] = qt_ref[2 * p * hd:(2 * p + 1) * hd, :]
        wq_scr[p, 0:hd, LANES:] = zeros_hd
        wq_scr[p, hd:, 0:LANES] = zeros_hd
        wq_scr[p, hd:, LANES:] = qt_ref[(2 * p + 1) * hd:(2 * p + 2) * hd, :]

    row = lax.broadcasted_iota(I32, (KEY_TILE, LANES), 0)
    col = lax.broadcasted_iota(I32, (KEY_TILE, LANES), 1)
    w_idx = rows_ref[0:N_IDX_HEADS, :] * (N_IDX_HEADS ** -0.5 * IDX_DIM ** -0.5)

    def key_tile(j):
        return pl.multiple_of(j * KEY_TILE, KEY_TILE)

    def causal(j):
        return (j * KEY_TILE + row) <= (t0 + col)

    def score_body(jc, carry):
        subs = [jc * (COUNT_TILE // KEY_TILE) + sub for sub in range(COUNT_TILE // KEY_TILE)]
        logits = [_dot(kidx_ref[pl.ds(key_tile(jnp.minimum(j, n_kt - 1)), KEY_TILE), :], wi_scr[...])
                  for j in subs]
        for j, lg in zip(subs, logits):
            score = jnp.zeros((KEY_TILE, LANES), F32)
            for h in range(N_IDX_HEADS):
                score = score + jnp.maximum(lg[:, h * LANES:(h + 1) * LANES], 0.0) * w_idx[h:h + 1, :]
            key_scr[pl.ds(key_tile(j), KEY_TILE), :] = jnp.where(causal(j), _float_order_key(score), KEY_MIN)
        return carry

    lax.fori_loop(0, n_ct, score_body, 0)

    def count_ge(cand):
        def body(j, acc):
            kb = key_scr[pl.ds(pl.multiple_of(j * COUNT_TILE, COUNT_TILE), COUNT_TILE), :]
            ind = jnp.where(kb >= cand, 1, 0)
            return acc + jnp.sum(ind.reshape(COUNT_TILE // 8, 8, LANES), axis=0)
        acc = lax.fori_loop(0, n_ct, body, jnp.zeros((8, LANES), I32))
        return jnp.sum(acc, axis=0, keepdims=True)

    count0 = count_ge(jnp.zeros((1, LANES), I32))
    prefix = jnp.where(count0 >= topk, 0, KEY_MIN)
    count_prefix = jnp.where(count0 >= topk, count0, -1)
    open_q = jnp.where((count_prefix == topk) | (count_ge(prefix + 1) < topk), 0, 1)

    def bits_left(state):
        b, _, _, _, n_open = state
        return (b < 31) & (n_open > 0)

    def bit_group(state):
        b0, prefix, count_prefix, open_q, _ = state

        def bit_body(step, carry):
            prefix, count_prefix, open_q = carry
            bit = 30 - (b0 + step)
            cand = prefix | jnp.where(bit >= 0, 1 << jnp.maximum(bit, 0), 0)
            count = count_ge(cand)
            accept = (count >= topk) & (open_q == 1)
            prefix = jnp.where(accept, cand, prefix)
            count_prefix = jnp.where(accept, count, count_prefix)
            return prefix, count_prefix, jnp.where(count_prefix == topk, 0, open_q)

        prefix, count_prefix, open_q = lax.fori_loop(0, BITS_PER_CHECK, bit_body, (prefix, count_prefix, open_q))
        return b0 + BITS_PER_CHECK, prefix, count_prefix, open_q, jnp.sum(open_q)

    state = (jnp.int32(0), prefix, count_prefix, open_q, jnp.sum(open_q))
    _, tau, _, _, _ = lax.while_loop(bits_left, bit_group, state)
    tau = jnp.maximum(tau, KEY_MIN + 1)
    key_max = 2 ** 31 - 1
    n_above = jnp.where(tau == key_max, 0, count_ge(jnp.where(tau == key_max, tau, tau + 1)))
    need = (topk - n_above).astype(F32)

    def mask_body(jc, tie_carry):
        for sub in range(COUNT_TILE // KEY_TILE):
            rows_j = pl.ds(key_tile(jc * (COUNT_TILE // KEY_TILE) + sub), KEY_TILE)
            kb = key_scr[rows_j, :]
            tie = kb == tau
            tie_rank = _dot(tri_ref[...], jnp.where(tie, 1.0, 0.0).astype(BF16)) + tie_carry
            tie_carry = tie_rank[KEY_TILE - 1:KEY_TILE, :]
            take = (tie & (tie_rank <= need)) | (kb > tau)
            key_scr[rows_j, :] = lax.bitcast_convert_type(jnp.where(take, 0.0, MASKED_LOGIT), I32)
        return tie_carry

    lax.fori_loop(0, n_ct, mask_body, jnp.zeros((1, LANES), F32))

    m_scr[...] = jnp.full(m_scr.shape, MASKED_LOGIT, F32)
    l_scr[...] = jnp.zeros(l_scr.shape, F32)
    acc_scr[...] = jnp.zeros(acc_scr.shape, F32)
    ones_rows = jnp.ones((16, KEY_TILE), BF16)

    def tile_group(j_first, n_tiles, near):
        tiles = [j_first + t for t in range(n_tiles)]
        for slot, j in enumerate(tiles):
            s0 = key_tile(j)
            for p in range(n_pairs):
                s_scr[slot, :, p * pair_w:(p + 1) * pair_w] = _dot(
                    k_ref[pl.ds(s0, KEY_TILE), p * LANES:(p + 1) * LANES], wq_scr[p])
        for slot, j in enumerate(tiles):
            s0 = key_tile(j)
            mask_add = lax.bitcast_convert_type(key_scr[pl.ds(s0, KEY_TILE), :], F32)
            if near:
                kind_top = jnp.clip(2 * j - i + 2, 0, 2)
                kind_bot = jnp.clip(2 * j - i + 3, 0, 2)
            alphas = []
            for h in range(N_ATTN_HEADS):
                cols = slice(h * LANES, (h + 1) * LANES)
                logits = s_scr[slot, :, cols] + mask_add
                if near:
                    logits = logits + jnp.concatenate(
                        [bias_ref[kind_top, :, cols], bias_ref[kind_bot, :, cols]], axis=0)
                m_old = m_scr[h:h + 1, :]
                m_new = jnp.maximum(m_old, jnp.max(logits, axis=0, keepdims=True))
                m_scr[h:h + 1, :] = m_new
                p_scr[slot, :, cols] = jnp.exp2(logits - m_new).astype(BF16)
                alphas.append(jnp.exp2(m_old - m_new))
            for p in range(n_pairs):
                lhs = jnp.concatenate([vt_ref[p * 2 * hd:(p + 1) * 2 * hd, pl.ds(s0, KEY_TILE)], ones_rows], axis=0)
                pv = _dot(lhs, p_scr[slot, :, p * pair_w:(p + 1) * pair_w])
                for half in range(2):
                    h = 2 * p + half
                    rows_h = slice(h * hd, (h + 1) * hd)
                    q_cols = slice(half * LANES, (half + 1) * LANES)
                    acc_scr[rows_h, :] = acc_scr[rows_h, :] * alphas[h] + pv[half * hd:(half + 1) * hd, q_cols]
                    l_scr[h:h + 1, :] = l_scr[h:h + 1, :] * alphas[h] + pv[2 * hd:2 * hd + 1, q_cols]

    n_far = 2 * (jnp.maximum(n_kt - 2, 0) // 2)

    def far_body(jp, carry):
        tile_group(2 * jp, 2, near=False)
        return carry

    lax.fori_loop(0, n_far // 2, far_body, 0)
    n_near = n_kt - n_far

    @pl.when(n_near >= 2)
    def _():
        tile_group(n_far, 2, near=True)

    @pl.when(n_near % 2 == 1)
    def _():
        tile_group(n_kt - 1, 1, near=True)

    for h in range(N_ATTN_HEADS):
        rows_h = slice(h * hd, (h + 1) * hd)
        acc_scr[rows_h, :] = acc_scr[rows_h, :] / l_scr[h:h + 1, :]
    o_ref[...] = acc_scr[...].T


def _dsa_kernel(*refs, topk, t_valid):
    o_ref = refs[8]
    i = pl.program_id(1)
    is_real = i * LANES < t_valid

    @pl.when(is_real)
    def _():
        _dsa_block(i, *refs, topk=topk)

    @pl.when(jnp.logical_not(is_real))
    def _():
        o_ref[...] = jnp.zeros(o_ref.shape, F32)


def _dsa_attention(qit, rows, qt, kidx, k, vt, bias, tri, batch, tp, t_valid, topk):
    n_rows = k.shape[0]
    nqb = tp // LANES
    d_idx = N_IDX_HEADS * IDX_DIM
    n_pairs = N_ATTN_HEADS // 2
    key_rows = -(-tp // COUNT_TILE) * COUNT_TILE
    q_cols = lambda height: pl.BlockSpec((None, height, LANES), lambda b, i: (b, 0, i))
    return pl.pallas_call(
        functools.partial(_dsa_kernel, topk=topk, t_valid=t_valid),
        out_shape=jax.ShapeDtypeStruct((n_rows, D_ATTN), F32),
        grid=(batch, nqb),
        in_specs=[q_cols(d_idx),
                  pl.BlockSpec((16, LANES), lambda b, i: (0, b * nqb + i)),
                  q_cols(D_ATTN),
                  pl.BlockSpec((tp, LANES), lambda b, i: (b, 0)),
                  pl.BlockSpec((tp, D_ATTN), lambda b, i: (b, 0)),
                  pl.BlockSpec((None, D_ATTN, tp), lambda b, i: (b, 0, 0)),
                  pl.BlockSpec((3, LANES, N_ATTN_HEADS * LANES), lambda b, i: (0, 0, 0)),
                  pl.BlockSpec((KEY_TILE, KEY_TILE), lambda b, i: (0, 0))],
        out_specs=pl.BlockSpec((LANES, D_ATTN), lambda b, i: (b * nqb + i, 0)),
        scratch_shapes=[pltpu.VMEM((key_rows, LANES), I32),
                        pltpu.VMEM((LANES, N_IDX_HEADS * LANES), BF16),
                        pltpu.VMEM((n_pairs, LANES, 2 * LANES), BF16),
                        pltpu.VMEM((2, KEY_TILE, N_ATTN_HEADS * LANES), F32),
                        pltpu.VMEM((2, KEY_TILE, N_ATTN_HEADS * LANES), BF16),
                        pltpu.VMEM((N_ATTN_HEADS, LANES), F32),
                        pltpu.VMEM((N_ATTN_HEADS, LANES), F32),
                        pltpu.VMEM((D_ATTN, LANES), F32)],
        compiler_params=_params("parallel", "parallel"),
        name="dsa_attention",
    )(qit, rows, qt, kidx, k, vt, bias, tri)


def _gdn_prep_kernel(x_ref, halo_ref, cw_ref, rows_ref, alog_ref, dtb_ref, q_ref, k_ref, v_ref, bg_ref, buf):
    first = pl.program_id(1) == 0
    rb = x_ref.shape[0]
    buf[0:HALO_ROWS, :] = jnp.where(first, 0.0, halo_ref[...])
    buf[HALO_ROWS:, :] = x_ref[...]
    acc = jnp.zeros((rb, 3 * D_GDN), F32)
    for tap in range(CONV_WIDTH):
        start = HALO_ROWS - (CONV_WIDTH - 1) + tap
        acc = acc + cw_ref[tap:tap + 1, :] * buf[start:start + rb, :]
    y = _silu(acc)
    for h in range(N_GDN_HEADS):
        cols = slice(h * GDN_HEAD_DIM, (h + 1) * GDN_HEAD_DIM)
        qh = y[:, cols]
        kh = y[:, D_GDN + h * GDN_HEAD_DIM:D_GDN + (h + 1) * GDN_HEAD_DIM]
        q_ref[:, cols] = (qh * lax.rsqrt(jnp.sum(qh * qh, axis=-1, keepdims=True) + EPS)
                          * (GDN_HEAD_DIM ** -0.5))
        k_ref[:, cols] = kh * lax.rsqrt(jnp.sum(kh * kh, axis=-1, keepdims=True) + EPS)
    v_ref[...] = y[:, 2 * D_GDN:]
    rows = rows_ref[...]
    beta = _sigmoid(rows[4:8, :])
    a = rows[8:12, :] + dtb_ref[...]
    softplus = jnp.maximum(a, 0.0) + jnp.log1p(jnp.exp(-jnp.abs(a)))
    bg_ref[0:4, :] = beta
    bg_ref[4:8, :] = -jnp.exp(alog_ref[...]) * softplus


def _gdn_prep(proj, rows, lw, batch, tp, rb):
    n_rows = proj.shape[0]
    nb = tp // rb
    halo_per_block = rb // HALO_ROWS
    row_spec = pl.BlockSpec((rb, D_GDN), lambda b, i: (b * nb + i, 0))
    return pl.pallas_call(
        _gdn_prep_kernel,
        out_shape=(jax.ShapeDtypeStruct((n_rows, D_GDN), F32),) * 3
        + (jax.ShapeDtypeStruct((8, n_rows), F32),),
        grid=(batch, nb),
        in_specs=[pl.BlockSpec((rb, 3 * D_GDN), lambda b, i: (b * nb + i, 0)),
                  pl.BlockSpec((HALO_ROWS, 3 * D_GDN),
                               lambda b, i: (jnp.maximum((b * nb + i) * halo_per_block - 1, 0), 0)),
                  pl.BlockSpec((CONV_WIDTH, 3 * D_GDN), lambda b, i: (0, 0)),
                  pl.BlockSpec((16, rb), lambda b, i: (0, b * nb + i)),
                  pl.BlockSpec((N_GDN_HEADS, 1), lambda b, i: (0, 0)),
                  pl.BlockSpec((N_GDN_HEADS, 1), lambda b, i: (0, 0))],
        out_specs=(row_spec, row_spec, row_spec,
                   pl.BlockSpec((8, rb), lambda b, i: (0, b * nb + i))),
        scratch_shapes=[pltpu.VMEM((HALO_ROWS + rb, 3 * D_GDN), F32)],
        compiler_params=_params("parallel", "parallel"),
        name="gdn_prep",
    )(proj, proj, lw["conv_w"], rows, lw["a_log"], lw["dt_bias"])


def _gdn_chunk_kernel(q_ref, k_ref, v_ref, bg_ref, m_ref, n_ref, p_ref, r_ref, cd_ref):
    c = LANES
    heads = range(N_GDN_HEADS)
    bg = bg_ref[...]
    lane8 = lax.broadcasted_iota(I32, (8, c), 1)
    dec = bg
    shift = 1
    while shift < c:
        dec = dec + jnp.where(lane8 >= shift, pltpu.roll(dec, shift, 1), 0.0)
        shift *= 2
    row = lax.broadcasted_iota(I32, (c, c), 0)
    col = lax.broadcasted_iota(I32, (c, c), 1)
    tri = row >= col
    strict = row > col
    eye = jnp.where(row == col, 1.0, 0.0)

    cols = [slice(h * GDN_HEAD_DIM, (h + 1) * GDN_HEAD_DIM) for h in heads]
    d_row = [jnp.broadcast_to(dec[4 + h:5 + h, :], (c, c)) for h in heads]
    d_col = [d.T for d in d_row]
    beta_col = [jnp.broadcast_to(bg[h:h + 1, :], (c, c)).T for h in heads]
    d_last = [d[:, c - 1:c] for d in d_row]
    gamma = [jnp.exp(jnp.where(tri, d_col[h] - d_row[h], MASKED_LOGIT)) for h in heads]
    exp_d = [jnp.exp(d_col[h]) for h in heads]
    k16 = [k_ref[:, cols[h]].astype(BF16) for h in heads]
    kb = [k_ref[:, cols[h]] * beta_col[h] for h in heads]
    nil = [jnp.where(strict, _dot_nt(kb[h].astype(BF16), k16[h]) * gamma[h], 0.0) for h in heads]
    nil_parts = [_split_bf16(x) for x in nil]
    inv = [eye - x for x in nil]
    power = [_dot_split(x, x) for x in nil_parts]
    steps = int(math.log2(c)) - 1
    for it in range(steps):
        power_parts = [_split_bf16(x) for x in power]
        inv = [inv[h] + _dot_split(_split_bf16(inv[h]), power_parts[h]) for h in heads]
        if it + 1 < steps:
            power = [_dot_split(x, x) for x in power_parts]
    rhs = [jnp.concatenate([kb[h] * exp_d[h], v_ref[:, cols[h]] * beta_col[h]], axis=1) for h in heads]
    wu = [_dot_split(_split_bf16(inv[h]), _split_bf16(rhs[h])).astype(BF16) for h in heads]
    aqk = [jnp.where(tri, _dot_nt(q_ref[:, cols[h]].astype(BF16), k16[h]) * gamma[h], 0.0).astype(BF16)
           for h in heads]
    kd_t = [(k_ref[:, cols[h]] * jnp.exp(d_last[h] - d_col[h])).T.astype(BF16) for h in heads]
    state_wu = [_dot(kd_t[h], wu[h]) for h in heads]
    out_wu = [_dot(aqk[h], wu[h]) for h in heads]
    for h in heads:
        m_ref[h] = (-state_wu[h][:, :c]).astype(BF16)
        n_ref[h] = state_wu[h][:, c:]
        p_ref[h] = (q_ref[:, cols[h]] * exp_d[h] - out_wu[h][:, :c]).astype(BF16)
        r_ref[h] = out_wu[h][:, c:]
        cd_ref[h:h + 1, :] = jnp.exp(d_last[h][0:1, :] + jnp.zeros((1, c), F32))


def _gdn_chunks(qn, kn, vv, bg):
    n_rows = qn.shape[0]
    nc = n_rows // LANES
    tok = pl.BlockSpec((LANES, D_GDN), lambda c: (c, 0))
    mat = pl.BlockSpec((None, N_GDN_HEADS, LANES, LANES), lambda c: (c, 0, 0, 0))
    mat_shape = lambda dt: jax.ShapeDtypeStruct((nc, N_GDN_HEADS, LANES, LANES), dt)
    return pl.pallas_call(
        _gdn_chunk_kernel,
        out_shape=(mat_shape(BF16), mat_shape(F32), mat_shape(BF16), mat_shape(F32),
                   jax.ShapeDtypeStruct((nc, N_GDN_HEADS, LANES), F32)),
        grid=(nc,),
        in_specs=[tok, tok, tok, pl.BlockSpec((8, LANES), lambda c: (0, c))],
        out_specs=(mat, mat, mat, mat, pl.BlockSpec((None, N_GDN_HEADS, LANES), lambda c: (c, 0, 0))),
        compiler_params=_params("parallel"),
        name="gdn_chunks",
    )(qn, kn, vv, bg)


def _gdn_scan_kernel(m_ref, n_ref, p_ref, r_ref, cd_ref, o_ref, s_scr, *, batch):
    @pl.when(pl.program_id(0) == 0)
    def _():
        s_scr[...] = jnp.zeros(s_scr.shape, F32)

    for b in range(batch):
        for h in range(N_GDN_HEADS):
            s = s_scr[b, h]
            s16 = s.astype(BF16)
            o_ref[b, :, h * GDN_HEAD_DIM:(h + 1) * GDN_HEAD_DIM] = _dot(p_ref[b, h], s16) + r_ref[b, h]
            s_scr[b, h] = s * cd_ref[b, h:h + 1, :] + _dot(m_ref[b, h], s16) + n_ref[b, h]


def _gdn_scan(m_mat, n_mat, p_mat, r_mat, cd, batch, tp):
    nc = tp // LANES
    shape5 = lambda a: a.reshape(batch, nc, N_GDN_HEADS, LANES, LANES)
    mat = pl.BlockSpec((batch, None, N_GDN_HEADS, LANES, LANES), lambda c: (0, c, 0, 0, 0))
    return pl.pallas_call(
        functools.partial(_gdn_scan_kernel, batch=batch),
        out_shape=jax.ShapeDtypeStruct((batch, tp, D_GDN), F32),
        grid=(nc,),
        in_specs=[mat, mat, mat, mat,
                  pl.BlockSpec((batch, None, N_GDN_HEADS, LANES), lambda c: (0, c, 0, 0))],
        out_specs=pl.BlockSpec((batch, LANES, D_GDN), lambda c: (0, c, 0)),
        scratch_shapes=[pltpu.VMEM((batch, N_GDN_HEADS, LANES, LANES), F32)],
        compiler_params=_params("arbitrary"),
        name="gdn_scan",
    )(shape5(m_mat), shape5(n_mat), shape5(p_mat), shape5(r_mat), cd.reshape(batch, nc, N_GDN_HEADS, LANES))


def _out_proj_kernel(h_ref, oa_ref, za_ref, og_ref, zg_ref, gain_ref, w_ref, out_ref):
    attn = (oa_ref[...] * _silu(za_ref[...])).astype(BF16)
    y = _dot(attn, w_ref[0:D_ATTN, :])
    og, zg = og_ref[...], zg_ref[...]
    for h in range(N_GDN_HEADS):
        cols = slice(h * GDN_HEAD_DIM, (h + 1) * GDN_HEAD_DIM)
        gated = (_rms_rows(og[:, cols], gain_ref[...]) * _silu(zg[:, cols])).astype(BF16)
        y = y + _dot(gated, w_ref[D_ATTN + h * GDN_HEAD_DIM:D_ATTN + (h + 1) * GDN_HEAD_DIM, :])
    out_ref[...] = h_ref[...] + y


def _out_proj(h2, o_attn, proj, o_gdn, gain, w_out, rb):
    n_rows = h2.shape[0]
    blk = lambda width, col: pl.BlockSpec((rb, width), lambda i: (i, col))
    return pl.pallas_call(
        _out_proj_kernel,
        out_shape=jax.ShapeDtypeStruct((n_rows, D_MODEL), F32),
        grid=(n_rows // rb,),
        in_specs=[blk(D_MODEL, 0), blk(D_ATTN, 0), blk(D_ATTN, COL_Z_ATTN), blk(D_GDN, 0),
                  blk(D_GDN, COL_Z_GDN),
                  pl.BlockSpec((1, GDN_HEAD_DIM), lambda i: (0, 0)),
                  pl.BlockSpec((D_MODEL, D_MODEL), lambda i: (0, 0))],
        out_specs=blk(D_MODEL, 0),
        compiler_params=_params("parallel"),
        name="out_proj",
    )(h2, o_attn, proj, o_gdn, proj, gain, w_out)


def _pack_layer(norm_gain, w_in, cq_gain, ckv_gain, w_uq, w_ukv, w_q_idx, q_gain, k_gain, conv_w, a_log,
                dt_bias, gdn_gain, w_out):
    o = 0
    parts = {}
    for name, size in (("c_q", Q_RANK), ("c_kv", KV_RANK), ("k_idx", IDX_DIM), ("w_idx", N_IDX_HEADS),
                       ("z_attn", D_ATTN), ("qkv_g", 3 * D_GDN), ("z_g", D_GDN), ("b", N_GDN_HEADS),
                       ("a", N_GDN_HEADS)):
        parts[name] = w_in[:, o:o + size]
        o += size
    w_packed = jnp.concatenate([parts["qkv_g"], parts["c_q"], parts["c_kv"], parts["k_idx"], parts["k_idx"],
                                parts["z_attn"], parts["z_g"]], axis=1).astype(BF16)
    w_rows = jnp.concatenate([parts["w_idx"], parts["b"], parts["a"],
                              jnp.zeros((D_MODEL, 4), w_in.dtype)], axis=1).T.astype(BF16)
    head_of_col = jnp.arange(D_ATTN) // ATTN_HEAD_DIM
    head_sel = (head_of_col[:, None] == jnp.arange(LANES)[None, :]).astype(F32)
    return dict(
        gain=norm_gain[None, :], w_packed=w_packed, w_rows=w_rows,
        g_cq=cq_gain[None, :], g_ckv=ckv_gain[None, :],
        w_uqt=w_uq.T.astype(BF16), w_qit=w_q_idx.T.astype(BF16),
        w_uk=w_ukv[:, :D_ATTN].astype(BF16), w_uvt=w_ukv[:, D_ATTN:].T.astype(BF16),
        g_q_col=q_gain[:, None], g_k=jnp.tile(k_gain, N_ATTN_HEADS)[None, :],
        head_sel=head_sel, head_sel_t=head_sel.T,
        conv_w=conv_w, a_log=a_log[:, None], dt_bias=dt_bias[:, None],
        gdn_gain=gdn_gain[None, :], w_out=w_out.astype(BF16))


def _layer(h2, lw, bias, tri, batch, tp, t_valid, topk):
    rb = _row_block(tp)
    proj, rows = _in_proj(h2, lw["gain"], lw["w_packed"], lw["w_rows"], rb)
    qt, k, vt, qit, kidx = _dsa_prep(proj, lw, batch, tp, rb)
    o_attn = _dsa_attention(qit, rows, qt, kidx, k, vt, bias, tri, batch, tp, t_valid, topk)
    qn, kn, vv, bg = _gdn_prep(proj, rows, lw, batch, tp, rb)
    m_mat, n_mat, p_mat, r_mat, cd = _gdn_chunks(qn, kn, vv, bg)
    o_gdn = _gdn_scan(m_mat, n_mat, p_mat, r_mat, cd, batch, tp).reshape(batch * tp, D_GDN)
    return _out_proj(h2, o_attn, proj, o_gdn, lw["gdn_gain"], lw["w_out"], rb)


def _forward(x, meta_tokens, rel_bias_table, layer_weights, topk):
    batch, seq, _ = x.shape
    t = seq + N_META
    tp = -(-t // KEY_TILE) * KEY_TILE
    meta = jnp.broadcast_to(meta_tokens[None].astype(x.dtype), (batch, N_META, D_MODEL))
    h = jnp.concatenate([meta, x, jnp.zeros((batch, tp - t, D_MODEL), x.dtype)], axis=1)
    h2 = h.reshape(batch * tp, D_MODEL)
    bias = _bias_tiles(rel_bias_table)
    tri = jnp.tril(jnp.ones((KEY_TILE, KEY_TILE), BF16))
    for lw in layer_weights:
        h2 = _layer(h2, lw, bias, tri, batch, tp, t, topk)
    return h2.reshape(batch, tp, D_MODEL)[:, N_META:t]


def kernel(x, meta_tokens, rel_bias_table, norm_gain, w_in, cq_norm_gain, ckv_norm_gain, w_uq, w_ukv, w_q_idx,
           q_norm_gain, k_norm_gain, conv_w, a_log, dt_bias, gdn_norm_gain, w_out):
    depth = norm_gain.shape[0]
    topk = min(TOPK_MAX, x.shape[1] // 4)
    layers = [_pack_layer(norm_gain[l], w_in[l], cq_norm_gain[l], ckv_norm_gain[l], w_uq[l], w_ukv[l],
                          w_q_idx[l], q_norm_gain[l], k_norm_gain[l], conv_w[l], a_log[l], dt_bias[l],
                          gdn_norm_gain[l], w_out[l]) for l in range(depth)]
    return _forward(x, meta_tokens, rel_bias_table, layers, topk)
```

```python
import functools
import math

import jax
import jax.numpy as jnp
from jax import lax
from jax.experimental import pallas as pl
from jax.experimental.pallas import tpu as pltpu

F32 = jnp.float32
BF16 = jnp.bfloat16
I32 = jnp.int32
HIGHEST = lax.Precision.HIGHEST

D_MODEL = 1024
N_META = 16
EPS = 1e-6
N_ATTN_HEADS = 8
ATTN_HEAD_DIM = 64
D_ATTN = N_ATTN_HEADS * ATTN_HEAD_DIM
Q_RANK = 256
KV_RANK = 128
N_IDX_HEADS = 4
IDX_DIM = 64
TOPK_MAX = 256
N_REL_BUCKETS = 32
REL_MAX_DIST = 128
N_GDN_HEADS = 4
GDN_HEAD_DIM = 128
D_GDN = N_GDN_HEADS * GDN_HEAD_DIM
CONV_WIDTH = 4

LANES = 128
KEY_TILE = 256
COUNT_TILE = 512
SCAN_TILE = 1024
HI_BITS = 15
LO_BITS_PER_CHECK = 4
ROW_TILES = (5, 4, 3, 2, 1)
HALO_ROWS = 8
D_PACKED = 3 * D_GDN + 512 + D_ATTN + D_GDN
COL_SMALL = 3
COL_Z_ATTN = 4
COL_Z_GDN = 5
MASKED_LOGIT = -1e30
LOG2_E = math.log2(math.e)
KEY_MIN = -2 ** 31
PATTERN_NEG_FLT_MAX = KEY_MIN + (1 << 23)
VMEM_LIMIT = 56 * 1024 * 1024

NT_DIMS = (((1,), (1,)), ((), ()))


def _dot(a, b):
    return jnp.dot(a, b, preferred_element_type=F32)


def _dot_nt(a, b):
    return lax.dot_general(a, b, NT_DIMS, preferred_element_type=F32)


def _dot_hi(a, b):
    return jnp.dot(a, b, preferred_element_type=F32, precision=HIGHEST)


def _split_bf16(x):
    hi = x.astype(BF16)
    return hi, (x - hi.astype(F32)).astype(BF16)


def _dot_split(a_parts, b_parts):
    (a_hi, a_lo), (b_hi, b_lo) = a_parts, b_parts
    return _dot(a_hi, b_hi) + (_dot(a_hi, b_lo) + _dot(a_lo, b_hi))


def _sigmoid(x):
    return 1.0 / (1.0 + jnp.exp(-x))


def _silu(x):
    return x * _sigmoid(x)


def _row_block(tp):
    tiles = tp // LANES
    return LANES * next(d for d in ROW_TILES if tiles % d == 0)


def _params(*sem):
    return pltpu.CompilerParams(dimension_semantics=sem, vmem_limit_bytes=VMEM_LIMIT)


def _bias_kernel(table_ref, out_ref):
    row = lax.broadcasted_iota(I32, (LANES, LANES), 0)
    col = lax.broadcasted_iota(I32, (LANES, LANES), 1)
    max_exact = N_REL_BUCKETS // 2
    for kind in range(3):
        dist = col - row + (2 - kind) * LANES
        n = jnp.maximum(dist, 0)
        nf = jnp.maximum(n, 1).astype(F32)
        large = max_exact + (jnp.log(nf / max_exact) / math.log(REL_MAX_DIST / max_exact)
                             * (N_REL_BUCKETS - max_exact)).astype(I32)
        large = jnp.minimum(large, N_REL_BUCKETS - 1)
        bucket = jnp.where(n < max_exact, n, large)
        for h in range(N_ATTN_HEADS):
            tile = jnp.zeros((LANES, LANES), F32)
            for b in range(N_REL_BUCKETS):
                tile = jnp.where(bucket == b, table_ref[b, h], tile)
            far = table_ref[N_REL_BUCKETS - 1, h]
            out_ref[kind, :, h * LANES:(h + 1) * LANES] = (tile - far) * LOG2_E


def _bias_tiles(rel_table):
    return pl.pallas_call(
        _bias_kernel,
        out_shape=jax.ShapeDtypeStruct((3, LANES, N_ATTN_HEADS * LANES), F32),
        in_specs=[pl.BlockSpec(memory_space=pltpu.SMEM)],
        out_specs=pl.BlockSpec(memory_space=pltpu.VMEM),
        name="rel_bias_tiles",
    )(rel_table)


def _in_proj_kernel(h_ref, gain_ref, w_ref, wrows_ref, proj_ref, rows_ref):
    x = h_ref[...]
    y = x * lax.rsqrt(jnp.mean(x * x, axis=-1, keepdims=True) + EPS)
    hn = (y * gain_ref[...]).astype(BF16)
    proj_ref[...] = _dot(hn, w_ref[...])
    rows_ref[...] = _dot_nt(wrows_ref[...], hn)


def _in_proj(h2, gain, w_packed, w_rows, rb):
    n_rows = h2.shape[0]
    grid = (n_rows // rb,)
    return pl.pallas_call(
        _in_proj_kernel,
        out_shape=(jax.ShapeDtypeStruct((n_rows, D_PACKED), F32),
                   jax.ShapeDtypeStruct((16, n_rows), F32)),
        grid=grid,
        in_specs=[pl.BlockSpec((rb, D_MODEL), lambda i: (i, 0)),
                  pl.BlockSpec((1, D_MODEL), lambda i: (0, 0)),
                  pl.BlockSpec((D_MODEL, D_PACKED), lambda i: (0, 0)),
                  pl.BlockSpec((16, D_MODEL), lambda i: (0, 0))],
        out_specs=(pl.BlockSpec((rb, D_PACKED), lambda i: (i, 0)),
                   pl.BlockSpec((16, rb), lambda i: (0, i))),
        compiler_params=_params("parallel"),
        name="in_proj",
    )(h2, gain, w_packed, w_rows)


def _rms_rows(x, gain):
    return x * lax.rsqrt(jnp.mean(x * x, axis=-1, keepdims=True) + EPS) * gain


def _head_rms(x, gain, sel, sel_t):
    ms = _dot_hi(x * x, sel) * (1.0 / ATTN_HEAD_DIM)
    inv = _dot_hi(lax.rsqrt(ms + EPS), sel_t)
    return x * inv * gain


def _dsa_prep_kernel(sm_ref, gcq_ref, gckv_ref, wuqt_ref, wqit_ref, wuk_ref, wuvt_ref, gq_ref, gk_ref,
                     sel_ref, selt_ref, qt_ref, k_ref, vt_ref, qit_ref, kidx_ref):
    sm = sm_ref[...]
    rb = sm.shape[0]
    cq = _rms_rows(sm[:, :Q_RANK], gcq_ref[...]).astype(BF16)
    ckv = _rms_rows(sm[:, Q_RANK:Q_RANK + KV_RANK], gckv_ref[...]).astype(BF16)
    kidx_ref[...] = sm[:, Q_RANK + KV_RANK:].astype(BF16)
    q3 = _dot_nt(wuqt_ref[...], cq).reshape(N_ATTN_HEADS, ATTN_HEAD_DIM, rb)
    q3 = q3 * lax.rsqrt(jnp.mean(q3 * q3, axis=1, keepdims=True) + EPS) * gq_ref[...][None]
    qt_ref[...] = (q3 * (ATTN_HEAD_DIM ** -0.5 * LOG2_E)).reshape(D_ATTN, rb).astype(BF16)
    k_ref[...] = _head_rms(_dot(ckv, wuk_ref[...]), gk_ref[...], sel_ref[...], selt_ref[...]).astype(BF16)
    vt_ref[...] = _dot_nt(wuvt_ref[...], ckv).astype(BF16)
    qit_ref[...] = _dot_nt(wqit_ref[...], cq).astype(BF16)


def _dsa_prep(proj, lw, batch, tp, rb):
    n_rows = proj.shape[0]
    nb = tp // rb
    d_idx = N_IDX_HEADS * IDX_DIM
    const = lambda shape: pl.BlockSpec(shape, lambda b, i: (0, 0))
    row_spec = lambda width: pl.BlockSpec((rb, width), lambda b, i: (b * nb + i, 0))
    col_spec = lambda height: pl.BlockSpec((None, height, rb), lambda b, i: (b, 0, i))
    return pl.pallas_call(
        _dsa_prep_kernel,
        out_shape=(jax.ShapeDtypeStruct((batch, D_ATTN, tp), BF16),
                   jax.ShapeDtypeStruct((n_rows, D_ATTN), BF16),
                   jax.ShapeDtypeStruct((batch, D_ATTN, tp), BF16),
                   jax.ShapeDtypeStruct((batch, d_idx, tp), BF16),
                   jax.ShapeDtypeStruct((n_rows, LANES), BF16)),
        grid=(batch, nb),
        in_specs=[pl.BlockSpec((rb, 512), lambda b, i: (b * nb + i, COL_SMALL)),
                  const((1, Q_RANK)), const((1, KV_RANK)),
                  const((D_ATTN, Q_RANK)), const((d_idx, Q_RANK)),
                  const((KV_RANK, D_ATTN)), const((D_ATTN, KV_RANK)),
                  const((ATTN_HEAD_DIM, 1)), const((1, D_ATTN)),
                  const((D_ATTN, LANES)), const((LANES, D_ATTN))],
        out_specs=(col_spec(D_ATTN), row_spec(D_ATTN), col_spec(D_ATTN), col_spec(d_idx), row_spec(LANES)),
        compiler_params=_params("parallel", "parallel"),
        name="dsa_prep",
    )(proj, lw["g_cq"], lw["g_ckv"], lw["w_uqt"], lw["w_qit"], lw["w_uk"], lw["w_uvt"], lw["g_q_col"], lw["g_k"],
      lw["head_sel"], lw["head_sel_t"])


def _truncate_to_bf16(x):
    return lax.bitcast_convert_type(lax.bitcast_convert_type(x, I32) & -65536, F32).astype(BF16)


def _dsa_block(i, qit_ref, rows_ref, qt_ref, kidx_ref, k_ref, vt_ref, bias_ref, tri_ref, o_ref,
               score_scr, hi_scr, wi_scr, wq_scr, s_scr, p_scr, m_scr, l_scr, acc_scr, alpha_scr, *, topk):
    t0 = i * LANES
    n_kt = i // 2 + 1
    n_ct = (n_kt + 1) // 2
    n_st = (n_ct + 1) // 2
    hd = ATTN_HEAD_DIM
    pair_w = 2 * LANES
    n_pairs = N_ATTN_HEADS // 2

    zeros_hd = jnp.zeros((hd, LANES), BF16)
    for h in range(N_IDX_HEADS):
        wi_scr[0:IDX_DIM, h * LANES:(h + 1) * LANES] = qit_ref[h * IDX_DIM:(h + 1) * IDX_DIM, :]
    wi_scr[IDX_DIM:, :] = jnp.zeros((LANES - IDX_DIM, N_IDX_HEADS * LANES), BF16)
    for p in range(n_pairs):
        wq_scr[p, 0:hd, 0:LANES] = qt_ref[2 * p * hd:(2 * p + 1) * hd, :]
        wq_scr[p, 0:hd, LANES:] = zeros_hd
        wq_scr[p, hd:, 0:LANES] = zeros_hd
        wq_scr[p, hd:, LANES:] = qt_ref[(2 * p + 1) * hd:(2 * p + 2) * hd, :]

    row = lax.broadcasted_iota(I32, (KEY_TILE, LANES), 0)
    col = lax.broadcasted_iota(I32, (KEY_TILE, LANES), 1)
    w_idx = rows_ref[0:N_IDX_HEADS, :] * (N_IDX_HEADS ** -0.5 * IDX_DIM ** -0.5)

    def key_tile(j):
        return pl.multiple_of(j * KEY_TILE, KEY_TILE)

    def causal(j):
        return (j * KEY_TILE + row) <= (t0 + col)

    def score_body(jc, carry):
        subs = [jc * (COUNT_TILE // KEY_TILE) + sub for sub in range(COUNT_TILE // KEY_TILE)]
        logits = [_dot(kidx_ref[pl.ds(key_tile(jnp.minimum(j, n_kt - 1)), KEY_TILE), :], wi_scr[...])
                  for j in subs]
        for j, lg in zip(subs, logits):
            score = jnp.zeros((KEY_TILE, LANES), F32)
            for h in range(N_IDX_HEADS):
                score = score + jnp.maximum(lg[:, h * LANES:(h + 1) * LANES], 0.0) * w_idx[h:h + 1, :]
            score = jnp.where(causal(j), score, -jnp.inf)
            rows_j = pl.ds(key_tile(j), KEY_TILE)
            score_scr[rows_j, :] = score
            hi_scr[rows_j, :] = _truncate_to_bf16(score)
        return carry

    lax.fori_loop(0, n_ct, score_body, 0)

    @pl.when(n_ct % 2 == 1)
    def _():
        pad_rows = pl.ds(pl.multiple_of(n_ct * COUNT_TILE, COUNT_TILE), COUNT_TILE)
        score_scr[pad_rows, :] = jnp.full((COUNT_TILE, LANES), -jnp.inf, F32)
        hi_scr[pad_rows, :] = jnp.full((COUNT_TILE, LANES), -jnp.inf, BF16)

    def count_f32(cand, strict):
        def body(j, acc):
            for part in range(SCAN_TILE // COUNT_TILE):
                start = pl.multiple_of(j * SCAN_TILE + part * COUNT_TILE, COUNT_TILE)
                x = score_scr[pl.ds(start, COUNT_TILE), :]
                ind = jnp.where((x > cand) if strict else (x >= cand), 1, 0)
                acc = acc + jnp.sum(ind.reshape(COUNT_TILE // 8, 8, LANES), axis=0)
            return acc
        acc = lax.fori_loop(0, n_st, body, jnp.zeros((8, LANES), I32))
        return jnp.sum(acc, axis=0, keepdims=True)

    pack = 16
    one_bf, zero_bf = jnp.ones((), BF16), jnp.zeros((), BF16)

    def count_hi(cand):
        def body(j, acc):
            x = hi_scr[pl.ds(pl.multiple_of(j * SCAN_TILE, SCAN_TILE), SCAN_TILE), :]
            ind = jnp.where(x >= cand, one_bf, zero_bf)
            parts = [ind[r * pack:(r + 1) * pack, :] for r in range(SCAN_TILE // pack)]
            while len(parts) > 1:
                parts = [parts[n] + parts[n + 1] for n in range(0, len(parts), 2)]
            return acc + parts[0].astype(F32)
        acc = lax.fori_loop(0, n_st, body, jnp.zeros((pack, LANES), F32))
        return jnp.sum(acc, axis=0, keepdims=True).astype(I32)

    def pattern_to_f32(c):
        return lax.bitcast_convert_type(c ^ ((c >> 31) & 0x7FFFFFFF), F32)

    def pattern_to_hi(c):
        return _truncate_to_bf16(pattern_to_f32(c))

    def search_bits(state, hi_bit, n_bits, per_check, count_fn):
        def bits_left(st):
            done_bits, _, _, _, n_open = st
            return (done_bits < n_bits) & (n_open > 0)

        def bit_group(st):
            b0, prefix, count_prefix, open_q, _ = st

            def bit_body(step, carry):
                prefix, count_prefix, open_q = carry
                cand = prefix | (1 << (hi_bit - (b0 + step)))
                count = count_fn(cand)
                accept = (count >= topk) & (open_q == 1)
                prefix = jnp.where(accept, cand, prefix)
                count_prefix = jnp.where(accept, count, count_prefix)
                return prefix, count_prefix, jnp.where(count_prefix == topk, 0, open_q)

            prefix, count_prefix, open_q = lax.fori_loop(0, per_check, bit_body, (prefix, count_prefix, open_q))
            return b0 + per_check, prefix, count_prefix, open_q, jnp.sum(open_q)

        _, prefix, count_prefix, open_q, n_open = lax.while_loop(bits_left, bit_group, (jnp.int32(0),) + state)
        return prefix, count_prefix, open_q, n_open

    zero_row = jnp.zeros((1, LANES), F32)
    count0 = count_hi(zero_row.astype(BF16))
    nonneg = count0 >= topk
    prefix = jnp.where(nonneg, 0, KEY_MIN)
    count_prefix = jnp.where(nonneg, count0, -1)
    zero_tied = nonneg & (count_f32(zero_row, strict=True) < topk)
    open_q = jnp.where((count_prefix == topk) | zero_tied, 0, 1)
    state = (prefix, count_prefix, open_q, jnp.sum(open_q))
    state = search_bits(state, 30, HI_BITS, HI_BITS, lambda c: count_hi(pattern_to_hi(c)))
    state = search_bits(state, 30 - HI_BITS, 31 - HI_BITS, LO_BITS_PER_CHECK,
                        lambda c: count_f32(pattern_to_f32(c), strict=False))
    tau = pattern_to_f32(jnp.maximum(state[0], PATTERN_NEG_FLT_MAX))
    need = (topk - count_f32(tau, strict=True)).astype(F32)

    def mask_body(jc, tie_carry):
        rows = [pl.ds(key_tile(jc * (SCAN_TILE // KEY_TILE) + sub), KEY_TILE)
                for sub in range(SCAN_TILE // KEY_TILE)]
        xs = [score_scr[r, :] for r in rows]
        ties = [x == tau for x in xs]
        tie_cols = jnp.concatenate([jnp.where(tie, 1.0, 0.0).astype(BF16) for tie in ties], axis=1)
        all_ranks = _dot(tri_ref[...], tie_cols)
        ranks = [all_ranks[:, n * LANES:(n + 1) * LANES] for n in range(len(ties))]
        for r, x, tie, rank in zip(rows, xs, ties, ranks):
            rank = rank + tie_carry
            tie_carry = rank[KEY_TILE - 1:KEY_TILE, :]
            take = (tie & (rank <= need)) | (x > tau)
            score_scr[r, :] = jnp.where(take, 0.0, MASKED_LOGIT)
        return tie_carry

    lax.fori_loop(0, n_st, mask_body, jnp.zeros((1, LANES), F32))

    m_scr[...] = jnp.full(m_scr.shape, MASKED_LOGIT, F32)
    l_scr[...] = jnp.zeros(l_scr.shape, F32)
    acc_scr[...] = jnp.zeros(acc_scr.shape, F32)
    ones_rows = jnp.ones((16, KEY_TILE), BF16)

    def qk(slot, j):
        s0 = key_tile(j)
        for p in range(n_pairs):
            s_scr[slot, :, p * pair_w:(p + 1) * pair_w] = _dot(
                k_ref[pl.ds(s0, KEY_TILE), p * LANES:(p + 1) * LANES], wq_scr[p])

    def softmax(slot, j, near):
        mask_add = score_scr[pl.ds(key_tile(j), KEY_TILE), :]
        if near:
            kind_top = jnp.clip(2 * j - i + 2, 0, 2)
            kind_bot = jnp.clip(2 * j - i + 3, 0, 2)
        alphas = []
        for h in range(N_ATTN_HEADS):
            cols = slice(h * LANES, (h + 1) * LANES)
            logits = s_scr[slot, :, cols] + mask_add
            if near:
                logits = logits + jnp.concatenate(
                    [bias_ref[kind_top, :, cols], bias_ref[kind_bot, :, cols]], axis=0)
            m_old = m_scr[h:h + 1, :]
            m_new = jnp.maximum(m_old, jnp.max(logits, axis=0, keepdims=True))
            m_scr[h:h + 1, :] = m_new
            p_scr[slot, :, cols] = jnp.exp2(logits - m_new).astype(BF16)
            alphas.append(jnp.exp2(m_old - m_new))
        return alphas

    def pv(slot, j, alphas):
        s0 = key_tile(j)
        for p in range(n_pairs):
            lhs = jnp.concatenate([vt_ref[p * 2 * hd:(p + 1) * 2 * hd, pl.ds(s0, KEY_TILE)], ones_rows], axis=0)
            out = _dot(lhs, p_scr[slot, :, p * pair_w:(p + 1) * pair_w])
            for half in range(2):
                h = 2 * p + half
                rows_h = slice(h * hd, (h + 1) * hd)
                q_cols = slice(half * LANES, (half + 1) * LANES)
                acc_scr[rows_h, :] = acc_scr[rows_h, :] * alphas[h] + out[half * hd:(half + 1) * hd, q_cols]
                l_scr[h:h + 1, :] = l_scr[h:h + 1, :] * alphas[h] + out[2 * hd:2 * hd + 1, q_cols]

    def pending_alphas():
        return [alpha_scr[h:h + 1, :] for h in range(N_ATTN_HEADS)]

    def clear_pending():
        p_scr[1] = jnp.zeros(p_scr.shape[1:], BF16)
        alpha_scr[...] = jnp.ones(alpha_scr.shape, F32)

    def pair_step(ja, j_pending, j_next, near):
        pv(1, j_pending, pending_alphas())
        qk(1, ja + 1)
        alphas_a = softmax(0, ja, near)
        pv(0, ja, alphas_a)
        qk(0, j_next)
        alphas_b = softmax(1, ja + 1, near)
        for h in range(N_ATTN_HEADS):
            alpha_scr[h:h + 1, :] = alphas_b[h]

    def single_step(ja, j_pending, near):
        pv(1, j_pending, pending_alphas())
        pv(0, ja, softmax(0, ja, near))
        clear_pending()

    n_far = 2 * (jnp.maximum(n_kt - 2, 0) // 2)
    n_near = n_kt - n_far
    last = n_kt - 1
    clear_pending()
    qk(0, 0)

    def far_body(jp, carry):
        pair_step(2 * jp, jnp.maximum(2 * jp - 1, 0), 2 * jp + 2, near=False)
        return carry

    lax.fori_loop(0, n_far // 2, far_body, 0)

    @pl.when(n_near >= 2)
    def _():
        pair_step(n_far, jnp.maximum(n_far - 1, 0), jnp.minimum(n_far + 2, last), near=True)

    @pl.when(n_near % 2 == 1)
    def _():
        single_step(last, jnp.where(n_near == 3, n_far + 1, jnp.maximum(n_far - 1, 0)), near=True)

    pv(1, last, pending_alphas())

    for h in range(N_ATTN_HEADS):
        rows_h = slice(h * hd, (h + 1) * hd)
        acc_scr[rows_h, :] = acc_scr[rows_h, :] / l_scr[h:h + 1, :]
    o_ref[...] = acc_scr[...].T


def _dsa_kernel(*refs, topk, t_valid):
    o_ref = refs[8]
    i = pl.program_id(1)
    is_real = i * LANES < t_valid

    @pl.when(is_real)
    def _():
        _dsa_block(i, *refs, topk=topk)

    @pl.when(jnp.logical_not(is_real))
    def _():
        o_ref[...] = jnp.zeros(o_ref.shape, F32)


def _dsa_attention(qit, rows, qt, kidx, k, vt, bias, tri, batch, tp, t_valid, topk):
    n_rows = k.shape[0]
    nqb = tp // LANES
    d_idx = N_IDX_HEADS * IDX_DIM
    n_pairs = N_ATTN_HEADS // 2
    key_rows = -(-tp // SCAN_TILE) * SCAN_TILE
    q_cols = lambda height: pl.BlockSpec((None, height, LANES), lambda b, i: (b, 0, i))
    return pl.pallas_call(
        functools.partial(_dsa_kernel, topk=topk, t_valid=t_valid),
        out_shape=jax.ShapeDtypeStruct((n_rows, D_ATTN), F32),
        grid=(batch, nqb),
        in_specs=[q_cols(d_idx),
                  pl.BlockSpec((16, LANES), lambda b, i: (0, b * nqb + i)),
                  q_cols(D_ATTN),
                  pl.BlockSpec((tp, LANES), lambda b, i: (b, 0)),
                  pl.BlockSpec((tp, D_ATTN), lambda b, i: (b, 0)),
                  pl.BlockSpec((None, D_ATTN, tp), lambda b, i: (b, 0, 0)),
                  pl.BlockSpec((3, LANES, N_ATTN_HEADS * LANES), lambda b, i: (0, 0, 0)),
                  pl.BlockSpec((KEY_TILE, KEY_TILE), lambda b, i: (0, 0))],
        out_specs=pl.BlockSpec((LANES, D_ATTN), lambda b, i: (b * nqb + i, 0)),
        scratch_shapes=[pltpu.VMEM((key_rows, LANES), F32),
                        pltpu.VMEM((key_rows, LANES), BF16),
                        pltpu.VMEM((LANES, N_IDX_HEADS * LANES), BF16),
                        pltpu.VMEM((n_pairs, LANES, 2 * LANES), BF16),
                        pltpu.VMEM((2, KEY_TILE, N_ATTN_HEADS * LANES), F32),
                        pltpu.VMEM((2, KEY_TILE, N_ATTN_HEADS * LANES), BF16),
                        pltpu.VMEM((N_ATTN_HEADS, LANES), F32),
                        pltpu.VMEM((N_ATTN_HEADS, LANES), F32),
                        pltpu.VMEM((D_ATTN, LANES), F32),
                        pltpu.VMEM((N_ATTN_HEADS, LANES), F32)],
        compiler_params=_params("parallel", "parallel"),
        name="dsa_attention",
    )(qit, rows, qt, kidx, k, vt, bias, tri)


def _gdn_prep_kernel(x_ref, halo_ref, cw_ref, rows_ref, alog_ref, dtb_ref, q_ref, k_ref, v_ref, bg_ref, buf):
    first = pl.program_id(1) == 0
    rb = x_ref.shape[0]
    buf[0:HALO_ROWS, :] = jnp.where(first, 0.0, halo_ref[...])
    buf[HALO_ROWS:, :] = x_ref[...]
    acc = jnp.zeros((rb, 3 * D_GDN), F32)
    for tap in range(CONV_WIDTH):
        start = HALO_ROWS - (CONV_WIDTH - 1) + tap
        acc = acc + cw_ref[tap:tap + 1, :] * buf[start:start + rb, :]
    y = _silu(acc)
    for h in range(N_GDN_HEADS):
        cols = slice(h * GDN_HEAD_DIM, (h + 1) * GDN_HEAD_DIM)
        qh = y[:, cols]
        kh = y[:, D_GDN + h * GDN_HEAD_DIM:D_GDN + (h + 1) * GDN_HEAD_DIM]
        q_ref[:, cols] = (qh * lax.rsqrt(jnp.sum(qh * qh, axis=-1, keepdims=True) + EPS)
                          * (GDN_HEAD_DIM ** -0.5))
        k_ref[:, cols] = kh * lax.rsqrt(jnp.sum(kh * kh, axis=-1, keepdims=True) + EPS)
    v_ref[...] = y[:, 2 * D_GDN:]
    rows = rows_ref[...]
    beta = _sigmoid(rows[4:8, :])
    a = rows[8:12, :] + dtb_ref[...]
    softplus = jnp.maximum(a, 0.0) + jnp.log1p(jnp.exp(-jnp.abs(a)))
    bg_ref[0:4, :] = beta
    bg_ref[4:8, :] = -jnp.exp(alog_ref[...]) * softplus


def _gdn_prep(proj, rows, lw, batch, tp, rb):
    n_rows = proj.shape[0]
    nb = tp // rb
    halo_per_block = rb // HALO_ROWS
    row_spec = pl.BlockSpec((rb, D_GDN), lambda b, i: (b * nb + i, 0))
    return pl.pallas_call(
        _gdn_prep_kernel,
        out_shape=(jax.ShapeDtypeStruct((n_rows, D_GDN), F32),) * 3
        + (jax.ShapeDtypeStruct((8, n_rows), F32),),
        grid=(batch, nb),
        in_specs=[pl.BlockSpec((rb, 3 * D_GDN), lambda b, i: (b * nb + i, 0)),
                  pl.BlockSpec((HALO_ROWS, 3 * D_GDN),
                               lambda b, i: (jnp.maximum((b * nb + i) * halo_per_block - 1, 0), 0)),
                  pl.BlockSpec((CONV_WIDTH, 3 * D_GDN), lambda b, i: (0, 0)),
                  pl.BlockSpec((16, rb), lambda b, i: (0, b * nb + i)),
                  pl.BlockSpec((N_GDN_HEADS, 1), lambda b, i: (0, 0)),
                  pl.BlockSpec((N_GDN_HEADS, 1), lambda b, i: (0, 0))],
        out_specs=(row_spec, row_spec, row_spec,
                   pl.BlockSpec((8, rb), lambda b, i: (0, b * nb + i))),
        scratch_shapes=[pltpu.VMEM((HALO_ROWS + rb, 3 * D_GDN), F32)],
        compiler_params=_params("parallel", "parallel"),
        name="gdn_prep",
    )(proj, proj, lw["conv_w"], rows, lw["a_log"], lw["dt_bias"])


def _gdn_chunk_kernel(q_ref, k_ref, v_ref, bg_ref, m_ref, n_ref, p_ref, r_ref, cd_ref):
    c = LANES
    heads = range(N_GDN_HEADS)
    bg = bg_ref[...]
    lane8 = lax.broadcasted_iota(I32, (8, c), 1)
    dec = bg
    shift = 1
    while shift < c:
        dec = dec + jnp.where(lane8 >= shift, pltpu.roll(dec, shift, 1), 0.0)
        shift *= 2
    row = lax.broadcasted_iota(I32, (c, c), 0)
    col = lax.broadcasted_iota(I32, (c, c), 1)
    tri = row >= col
    strict = row > col
    eye = jnp.where(row == col, 1.0, 0.0)

    cols = [slice(h * GDN_HEAD_DIM, (h + 1) * GDN_HEAD_DIM) for h in heads]
    d_row = [jnp.broadcast_to(dec[4 + h:5 + h, :], (c, c)) for h in heads]
    d_col = [d.T for d in d_row]
    beta_col = [jnp.broadcast_to(bg[h:h + 1, :], (c, c)).T for h in heads]
    d_last = [d[:, c - 1:c] for d in d_row]
    gamma = [jnp.exp(jnp.where(tri, d_col[h] - d_row[h], MASKED_LOGIT)) for h in heads]
    exp_d = [jnp.exp(d_col[h]) for h in heads]
    k16 = [k_ref[:, cols[h]].astype(BF16) for h in heads]
    kb = [k_ref[:, cols[h]] * beta_col[h] for h in heads]
    nil = [jnp.where(strict, _dot_nt(kb[h].astype(BF16), k16[h]) * gamma[h], 0.0) for h in heads]
    nil_parts = [_split_bf16(x) for x in nil]
    inv = [eye - x for x in nil]
    power = [_dot_split(x, x) for x in nil_parts]
    steps = int(math.log2(c)) - 1
    for it in range(steps):
        power_parts = [_split_bf16(x) for x in power]
        inv = [inv[h] + _dot_split(_split_bf16(inv[h]), power_parts[h]) for h in heads]
        if it + 1 < steps:
            power = [_dot_split(x, x) for x in power_parts]
    rhs = [jnp.concatenate([kb[h] * exp_d[h], v_ref[:, cols[h]] * beta_col[h]], axis=1) for h in heads]
    wu = [_dot_split(_split_bf16(inv[h]), _split_bf16(rhs[h])).astype(BF16) for h in heads]
    aqk = [jnp.where(tri, _dot_nt(q_ref[:, cols[h]].astype(BF16), k16[h]) * gamma[h], 0.0).astype(BF16)
           for h in heads]
    kd_t = [(k_ref[:, cols[h]] * jnp.exp(d_last[h] - d_col[h])).T.astype(BF16) for h in heads]
    state_wu = [_dot(kd_t[h], wu[h]) for h in heads]
    out_wu = [_dot(aqk[h], wu[h]) for h in heads]
    for h in heads:
        m_ref[h] = (-state_wu[h][:, :c]).astype(BF16)
        n_ref[h] = state_wu[h][:, c:]
        p_ref[h] = (q_ref[:, cols[h]] * exp_d[h] - out_wu[h][:, :c]).astype(BF16)
        r_ref[h] = out_wu[h][:, c:]
        cd_ref[h:h + 1, :] = jnp.exp(d_last[h][0:1, :] + jnp.zeros((1, c), F32))


def _gdn_chunks(qn, kn, vv, bg):
    n_rows = qn.shape[0]
    nc = n_rows // LANES
    tok = pl.BlockSpec((LANES, D_GDN), lambda c: (c, 0))
    mat = pl.BlockSpec((None, N_GDN_HEADS, LANES, LANES), lambda c: (c, 0, 0, 0))
    mat_shape = lambda dt: jax.ShapeDtypeStruct((nc, N_GDN_HEADS, LANES, LANES), dt)
    return pl.pallas_call(
        _gdn_chunk_kernel,
        out_shape=(mat_shape(BF16), mat_shape(F32), mat_shape(BF16), mat_shape(F32),
                   jax.ShapeDtypeStruct((nc, N_GDN_HEADS, LANES), F32)),
        grid=(nc,),
        in_specs=[tok, tok, tok, pl.BlockSpec((8, LANES), lambda c: (0, c))],
        out_specs=(mat, mat, mat, mat, pl.BlockSpec((None, N_GDN_HEADS, LANES), lambda c: (c, 0, 0))),
        compiler_params=_params("parallel"),
        name="gdn_chunks",
    )(qn, kn, vv, bg)


def _gdn_scan_kernel(m_ref, n_ref, p_ref, r_ref, cd_ref, o_ref, s_scr, *, batch):
    @pl.when(pl.program_id(0) == 0)
    def _():
        s_scr[...] = jnp.zeros(s_scr.shape, F32)

    for b in range(batch):
        for h in range(N_GDN_HEADS):
            s = s_scr[b, h]
            s16 = s.astype(BF16)
            o_ref[b, :, h * GDN_HEAD_DIM:(h + 1) * GDN_HEAD_DIM] = _dot(p_ref[b, h], s16) + r_ref[b, h]
            s_scr[b, h] = s * cd_ref[b, h:h + 1, :] + _dot(m_ref[b, h], s16) + n_ref[b, h]


def _gdn_scan(m_mat, n_mat, p_mat, r_mat, cd, batch, tp):
    nc = tp // LANES
    shape5 = lambda a: a.reshape(batch, nc, N_GDN_HEADS, LANES, LANES)
    mat = pl.BlockSpec((batch, None, N_GDN_HEADS, LANES, LANES), lambda c: (0, c, 0, 0, 0))
    return pl.pallas_call(
        functools.partial(_gdn_scan_kernel, batch=batch),
        out_shape=jax.ShapeDtypeStruct((batch, tp, D_GDN), F32),
        grid=(nc,),
        in_specs=[mat, mat, mat, mat,
                  pl.BlockSpec((batch, None, N_GDN_HEADS, LANES), lambda c: (0, c, 0, 0))],
        out_specs=pl.BlockSpec((batch, LANES, D_GDN), lambda c: (0, c, 0)),
        scratch_shapes=[pltpu.VMEM((batch, N_GDN_HEADS, LANES, LANES), F32)],
        compiler_params=_params("arbitrary"),
        name="gdn_scan",
    )(shape5(m_mat), shape5(n_mat), shape5(p_mat), shape5(r_mat), cd.reshape(batch, nc, N_GDN_HEADS, LANES))


def _out_proj_kernel(h_ref, oa_ref, za_ref, og_ref, zg_ref, gain_ref, w_ref, out_ref):
    attn = (oa_ref[...] * _silu(za_ref[...])).astype(BF16)
    y = _dot(attn, w_ref[0:D_ATTN, :])
    og, zg = og_ref[...], zg_ref[...]
    for h in range(N_GDN_HEADS):
        cols = slice(h * GDN_HEAD_DIM, (h + 1) * GDN_HEAD_DIM)
        gated = (_rms_rows(og[:, cols], gain_ref[...]) * _silu(zg[:, cols])).astype(BF16)
        y = y + _dot(gated, w_ref[D_ATTN + h * GDN_HEAD_DIM:D_ATTN + (h + 1) * GDN_HEAD_DIM, :])
    out_ref[...] = h_ref[...] + y


def _out_proj(h2, o_attn, proj, o_gdn, gain, w_out, rb):
    n_rows = h2.shape[0]
    blk = lambda width, col: pl.BlockSpec((rb, width), lambda i: (i, col))
    return pl.pallas_call(
        _out_proj_kernel,
        out_shape=jax.ShapeDtypeStruct((n_rows, D_MODEL), F32),
        grid=(n_rows // rb,),
        in_specs=[blk(D_MODEL, 0), blk(D_ATTN, 0), blk(D_ATTN, COL_Z_ATTN), blk(D_GDN, 0),
                  blk(D_GDN, COL_Z_GDN),
                  pl.BlockSpec((1, GDN_HEAD_DIM), lambda i: (0, 0)),
                  pl.BlockSpec((D_MODEL, D_MODEL), lambda i: (0, 0))],
        out_specs=blk(D_MODEL, 0),
        compiler_params=_params("parallel"),
        name="out_proj",
    )(h2, o_attn, proj, o_gdn, proj, gain, w_out)


def _pack_layer(norm_gain, w_in, cq_gain, ckv_gain, w_uq, w_ukv, w_q_idx, q_gain, k_gain, conv_w, a_log,
                dt_bias, gdn_gain, w_out):
    o = 0
    parts = {}
    for name, size in (("c_q", Q_RANK), ("c_kv", KV_RANK), ("k_idx", IDX_DIM), ("w_idx", N_IDX_HEADS),
                       ("z_attn", D_ATTN), ("qkv_g", 3 * D_GDN), ("z_g", D_GDN), ("b", N_GDN_HEADS),
                       ("a", N_GDN_HEADS)):
        parts[name] = w_in[:, o:o + size]
        o += size
    w_packed = jnp.concatenate([parts["qkv_g"], parts["c_q"], parts["c_kv"], parts["k_idx"], parts["k_idx"],
                                parts["z_attn"], parts["z_g"]], axis=1).astype(BF16)
    w_rows = jnp.concatenate([parts["w_idx"], parts["b"], parts["a"],
                              jnp.zeros((D_MODEL, 4), w_in.dtype)], axis=1).T.astype(BF16)
    head_of_col = jnp.arange(D_ATTN) // ATTN_HEAD_DIM
    head_sel = (head_of_col[:, None] == jnp.arange(LANES)[None, :]).astype(F32)
    return dict(
        gain=norm_gain[None, :], w_packed=w_packed, w_rows=w_rows,
        g_cq=cq_gain[None, :], g_ckv=ckv_gain[None, :],
        w_uqt=w_uq.T.astype(BF16), w_qit=w_q_idx.T.astype(BF16),
        w_uk=w_ukv[:, :D_ATTN].astype(BF16), w_uvt=w_ukv[:, D_ATTN:].T.astype(BF16),
        g_q_col=q_gain[:, None], g_k=jnp.tile(k_gain, N_ATTN_HEADS)[None, :],
        head_sel=head_sel, head_sel_t=head_sel.T,
        conv_w=conv_w, a_log=a_log[:, None], dt_bias=dt_bias[:, None],
        gdn_gain=gdn_gain[None, :], w_out=w_out.astype(BF16))


def _layer(h2, lw, bias, tri, batch, tp, t_valid, topk):
    rb = _row_block(tp)
    proj, rows = _in_proj(h2, lw["gain"], lw["w_packed"], lw["w_rows"], rb)
    qt, k, vt, qit, kidx = _dsa_prep(proj, lw, batch, tp, rb)
    o_attn = _dsa_attention(qit, rows, qt, kidx, k, vt, bias, tri, batch, tp, t_valid, topk)
    qn, kn, vv, bg = _gdn_prep(proj, rows, lw, batch, tp, rb)
    m_mat, n_mat, p_mat, r_mat, cd = _gdn_chunks(qn, kn, vv, bg)
    o_gdn = _gdn_scan(m_mat, n_mat, p_mat, r_mat, cd, batch, tp).reshape(batch * tp, D_GDN)
    return _out_proj(h2, o_attn, proj, o_gdn, lw["gdn_gain"], lw["w_out"], rb)


def _forward(x, meta_tokens, rel_bias_table, layer_weights, topk):
    batch, seq, _ = x.shape
    t = seq + N_META
    tp = -(-t // KEY_TILE) * KEY_TILE
    meta = jnp.broadcast_to(meta_tokens[None].astype(x.dtype), (batch, N_META, D_MODEL))
    h = jnp.concatenate([meta, x, jnp.zeros((batch, tp - t, D_MODEL), x.dtype)], axis=1)
    h2 = h.reshape(batch * tp, D_MODEL)
    bias = _bias_tiles(rel_bias_table)
    tri = jnp.tril(jnp.ones((KEY_TILE, KEY_TILE), BF16))
    for lw in layer_weights:
        h2 = _layer(h2, lw, bias, tri, batch, tp, t, topk)
    return h2.reshape(batch, tp, D_MODEL)[:, N_META:t]


def kernel(x, meta_tokens, rel_bias_table, norm_gain, w_in, cq_norm_gain, ckv_norm_gain, w_uq, w_ukv, w_q_idx,
           q_norm_gain, k_norm_gain, conv_w, a_log, dt_bias, gdn_norm_gain, w_out):
    depth = norm_gain.shape[0]
    topk = min(TOPK_MAX, x.shape[1] // 4)
    layers = [_pack_layer(norm_gain[l], w_in[l], cq_norm_gain[l], ckv_norm_gain[l], w_uq[l], w_ukv[l],
                          w_q_idx[l], q_norm_gain[l], k_norm_gain[l], conv_w[l], a_log[l], dt_bias[l],
                          gdn_norm_gain[l], w_out[l]) for l in range(depth)]
    return _forward(x, meta_tokens, rel_bias_table, layers, topk)
```

```python
import functools
import math

import jax
import jax.numpy as jnp
from jax import lax
from jax.experimental import pallas as pl
from jax.experimental.pallas import tpu as pltpu

F32 = jnp.float32
BF16 = jnp.bfloat16
I32 = jnp.int32
HIGHEST = lax.Precision.HIGHEST

D_MODEL = 1024
N_META = 16
EPS = 1e-6
N_ATTN_HEADS = 8
ATTN_HEAD_DIM = 64
D_ATTN = N_ATTN_HEADS * ATTN_HEAD_DIM
Q_RANK = 256
KV_RANK = 128
N_IDX_HEADS = 4
IDX_DIM = 64
TOPK_MAX = 256
N_REL_BUCKETS = 32
REL_MAX_DIST = 128
N_GDN_HEADS = 4
GDN_HEAD_DIM = 128
D_GDN = N_GDN_HEADS * GDN_HEAD_DIM
CONV_WIDTH = 4

LANES = 128
KEY_TILE = 256
COUNT_TILE = 512
SCAN_TILE = 1024
PASSES_PER_CHECK = 2
INTERPOLATED_PASSES = 24
MAX_SEARCH_PASSES = INTERPOLATED_PASSES + 32
ROW_TILES = (5, 4, 3, 2, 1)
HALO_ROWS = 8
D_PACKED = 3 * D_GDN + 512 + D_ATTN + D_GDN
COL_SMALL = 3
COL_Z_ATTN = 4
COL_Z_GDN = 5
MASKED_LOGIT = -1e30
LOG2_E = math.log2(math.e)
KEY_MIN = -2 ** 31
PATTERN_NEG_FLT_MAX = KEY_MIN + (1 << 23)
VMEM_LIMIT = 56 * 1024 * 1024

NT_DIMS = (((1,), (1,)), ((), ()))


def _dot(a, b):
    return jnp.dot(a, b, preferred_element_type=F32)


def _dot_nt(a, b):
    return lax.dot_general(a, b, NT_DIMS, preferred_element_type=F32)


def _dot_hi(a, b):
    return jnp.dot(a, b, preferred_element_type=F32, precision=HIGHEST)


def _split_bf16(x):
    hi = x.astype(BF16)
    return hi, (x - hi.astype(F32)).astype(BF16)


def _dot_split(a_parts, b_parts):
    (a_hi, a_lo), (b_hi, b_lo) = a_parts, b_parts
    return _dot(a_hi, b_hi) + (_dot(a_hi, b_lo) + _dot(a_lo, b_hi))


def _sigmoid(x):
    return 1.0 / (1.0 + jnp.exp(-x))


def _silu(x):
    return x * _sigmoid(x)


def _row_block(tp):
    tiles = tp // LANES
    return LANES * next(d for d in ROW_TILES if tiles % d == 0)


def _params(*sem):
    return pltpu.CompilerParams(dimension_semantics=sem, vmem_limit_bytes=VMEM_LIMIT)


def _bias_kernel(table_ref, out_ref):
    row = lax.broadcasted_iota(I32, (LANES, LANES), 0)
    col = lax.broadcasted_iota(I32, (LANES, LANES), 1)
    max_exact = N_REL_BUCKETS // 2
    for kind in range(3):
        dist = col - row + (2 - kind) * LANES
        n = jnp.maximum(dist, 0)
        nf = jnp.maximum(n, 1).astype(F32)
        large = max_exact + (jnp.log(nf / max_exact) / math.log(REL_MAX_DIST / max_exact)
                             * (N_REL_BUCKETS - max_exact)).astype(I32)
        large = jnp.minimum(large, N_REL_BUCKETS - 1)
        bucket = jnp.where(n < max_exact, n, large)
        for h in range(N_ATTN_HEADS):
            tile = jnp.zeros((LANES, LANES), F32)
            for b in range(N_REL_BUCKETS):
                tile = jnp.where(bucket == b, table_ref[b, h], tile)
            far = table_ref[N_REL_BUCKETS - 1, h]
            out_ref[kind, :, h * LANES:(h + 1) * LANES] = (tile - far) * LOG2_E


def _bias_tiles(rel_table):
    return pl.pallas_call(
        _bias_kernel,
        out_shape=jax.ShapeDtypeStruct((3, LANES, N_ATTN_HEADS * LANES), F32),
        in_specs=[pl.BlockSpec(memory_space=pltpu.SMEM)],
        out_specs=pl.BlockSpec(memory_space=pltpu.VMEM),
        name="rel_bias_tiles",
    )(rel_table)


def _in_proj_kernel(h_ref, gain_ref, w_ref, wrows_ref, proj_ref, rows_ref):
    x = h_ref[...]
    y = x * lax.rsqrt(jnp.mean(x * x, axis=-1, keepdims=True) + EPS)
    hn = (y * gain_ref[...]).astype(BF16)
    proj_ref[...] = _dot(hn, w_ref[...])
    rows_ref[...] = _dot_nt(wrows_ref[...], hn)


def _in_proj(h2, gain, w_packed, w_rows, rb):
    n_rows = h2.shape[0]
    grid = (n_rows // rb,)
    return pl.pallas_call(
        _in_proj_kernel,
        out_shape=(jax.ShapeDtypeStruct((n_rows, D_PACKED), F32),
                   jax.ShapeDtypeStruct((16, n_rows), F32)),
        grid=grid,
        in_specs=[pl.BlockSpec((rb, D_MODEL), lambda i: (i, 0)),
                  pl.BlockSpec((1, D_MODEL), lambda i: (0, 0)),
                  pl.BlockSpec((D_MODEL, D_PACKED), lambda i: (0, 0)),
                  pl.BlockSpec((16, D_MODEL), lambda i: (0, 0))],
        out_specs=(pl.BlockSpec((rb, D_PACKED), lambda i: (i, 0)),
                   pl.BlockSpec((16, rb), lambda i: (0, i))),
        compiler_params=_params("parallel"),
        name="in_proj",
    )(h2, gain, w_packed, w_rows)


def _rms_rows(x, gain):
    return x * lax.rsqrt(jnp.mean(x * x, axis=-1, keepdims=True) + EPS) * gain


def _head_rms(x, gain, sel, sel_t):
    ms = _dot_hi(x * x, sel) * (1.0 / ATTN_HEAD_DIM)
    inv = _dot_hi(lax.rsqrt(ms + EPS), sel_t)
    return x * inv * gain


def _dsa_prep_kernel(sm_ref, gcq_ref, gckv_ref, wuqt_ref, wqit_ref, wuk_ref, wuvt_ref, gq_ref, gk_ref,
                     sel_ref, selt_ref, qt_ref, k_ref, vt_ref, qit_ref, kidx_ref):
    sm = sm_ref[...]
    rb = sm.shape[0]
    cq = _rms_rows(sm[:, :Q_RANK], gcq_ref[...]).astype(BF16)
    ckv = _rms_rows(sm[:, Q_RANK:Q_RANK + KV_RANK], gckv_ref[...]).astype(BF16)
    kidx_ref[...] = sm[:, Q_RANK + KV_RANK:].astype(BF16)
    q3 = _dot_nt(wuqt_ref[...], cq).reshape(N_ATTN_HEADS, ATTN_HEAD_DIM, rb)
    q3 = q3 * lax.rsqrt(jnp.mean(q3 * q3, axis=1, keepdims=True) + EPS) * gq_ref[...][None]
    qt_ref[...] = (q3 * (ATTN_HEAD_DIM ** -0.5 * LOG2_E)).reshape(D_ATTN, rb).astype(BF16)
    k_ref[...] = _head_rms(_dot(ckv, wuk_ref[...]), gk_ref[...], sel_ref[...], selt_ref[...]).astype(BF16)
    vt_ref[...] = _dot_nt(wuvt_ref[...], ckv).astype(BF16)
    qit_ref[...] = _dot_nt(wqit_ref[...], cq).astype(BF16)


def _dsa_prep(proj, lw, batch, tp, rb):
    n_rows = proj.shape[0]
    nb = tp // rb
    d_idx = N_IDX_HEADS * IDX_DIM
    const = lambda shape: pl.BlockSpec(shape, lambda b, i: (0, 0))
    row_spec = lambda width: pl.BlockSpec((rb, width), lambda b, i: (b * nb + i, 0))
    col_spec = lambda height: pl.BlockSpec((None, height, rb), lambda b, i: (b, 0, i))
    return pl.pallas_call(
        _dsa_prep_kernel,
        out_shape=(jax.ShapeDtypeStruct((batch, D_ATTN, tp), BF16),
                   jax.ShapeDtypeStruct((n_rows, D_ATTN), BF16),
                   jax.ShapeDtypeStruct((batch, D_ATTN, tp), BF16),
                   jax.ShapeDtypeStruct((batch, d_idx, tp), BF16),
                   jax.ShapeDtypeStruct((n_rows, LANES), BF16)),
        grid=(batch, nb),
        in_specs=[pl.BlockSpec((rb, 512), lambda b, i: (b * nb + i, COL_SMALL)),
                  const((1, Q_RANK)), const((1, KV_RANK)),
                  const((D_ATTN, Q_RANK)), const((d_idx, Q_RANK)),
                  const((KV_RANK, D_ATTN)), const((D_ATTN, KV_RANK)),
                  const((ATTN_HEAD_DIM, 1)), const((1, D_ATTN)),
                  const((D_ATTN, LANES)), const((LANES, D_ATTN))],
        out_specs=(col_spec(D_ATTN), row_spec(D_ATTN), col_spec(D_ATTN), col_spec(d_idx), row_spec(LANES)),
        compiler_params=_params("parallel", "parallel"),
        name="dsa_prep",
    )(proj, lw["g_cq"], lw["g_ckv"], lw["w_uqt"], lw["w_qit"], lw["w_uk"], lw["w_uvt"], lw["g_q_col"], lw["g_k"],
      lw["head_sel"], lw["head_sel_t"])


def _dsa_block(i, qit_ref, rows_ref, qt_ref, kidx_ref, k_ref, vt_ref, bias_ref, tri_ref, o_ref,
               score_scr, wi_scr, wq_scr, s_scr, p_scr, m_scr, l_scr, acc_scr, alpha_scr, *, topk):
    t0 = i * LANES
    n_kt = i // 2 + 1
    n_ct = (n_kt + 1) // 2
    n_st = (n_ct + 1) // 2
    hd = ATTN_HEAD_DIM
    pair_w = 2 * LANES
    n_pairs = N_ATTN_HEADS // 2

    zeros_hd = jnp.zeros((hd, LANES), BF16)
    for h in range(N_IDX_HEADS):
        wi_scr[0:IDX_DIM, h * LANES:(h + 1) * LANES] = qit_ref[h * IDX_DIM:(h + 1) * IDX_DIM, :]
    wi_scr[IDX_DIM:, :] = jnp.zeros((LANES - IDX_DIM, N_IDX_HEADS * LANES), BF16)
    for p in range(n_pairs):
        wq_scr[p, 0:hd, 0:LANES] = qt_ref[2 * p * hd:(2 * p + 1) * hd, :]
        wq_scr[p, 0:hd, LANES:] = zeros_hd
        wq_scr[p, hd:, 0:LANES] = zeros_hd
        wq_scr[p, hd:, LANES:] = qt_ref[(2 * p + 1) * hd:(2 * p + 2) * hd, :]

    row = lax.broadcasted_iota(I32, (KEY_TILE, LANES), 0)
    col = lax.broadcasted_iota(I32, (KEY_TILE, LANES), 1)
    w_idx = rows_ref[0:N_IDX_HEADS, :] * (N_IDX_HEADS ** -0.5 * IDX_DIM ** -0.5)

    def key_tile(j):
        return pl.multiple_of(j * KEY_TILE, KEY_TILE)

    def causal(j):
        return (j * KEY_TILE + row) <= (t0 + col)

    def fold8(x, op):
        return op(x.reshape(KEY_TILE // 8, 8, LANES), axis=0)

    def score_body(jc, carry):
        top, bottom = carry
        subs = [jc * (COUNT_TILE // KEY_TILE) + sub for sub in range(COUNT_TILE // KEY_TILE)]
        logits = [_dot(kidx_ref[pl.ds(key_tile(jnp.minimum(j, n_kt - 1)), KEY_TILE), :], wi_scr[...])
                  for j in subs]
        for j, lg in zip(subs, logits):
            score = jnp.zeros((KEY_TILE, LANES), F32)
            for h in range(N_IDX_HEADS):
                score = score + jnp.maximum(lg[:, h * LANES:(h + 1) * LANES], 0.0) * w_idx[h:h + 1, :]
            visible = causal(j)
            top = jnp.maximum(top, fold8(jnp.where(visible, score, -jnp.inf), jnp.max))
            bottom = jnp.minimum(bottom, fold8(jnp.where(visible, score, jnp.inf), jnp.min))
            score_scr[pl.ds(key_tile(j), KEY_TILE), :] = jnp.where(visible, score, -jnp.inf)
        return top, bottom

    top, bottom = lax.fori_loop(0, n_ct, score_body,
                                (jnp.full((8, LANES), -jnp.inf, F32), jnp.full((8, LANES), jnp.inf, F32)))
    top = jnp.max(top, axis=0, keepdims=True)
    bottom = jnp.min(bottom, axis=0, keepdims=True)

    @pl.when(n_ct % 2 == 1)
    def _():
        pad_rows = pl.ds(pl.multiple_of(n_ct * COUNT_TILE, COUNT_TILE), COUNT_TILE)
        score_scr[pad_rows, :] = jnp.full((COUNT_TILE, LANES), -jnp.inf, F32)

    def count_f32(cand, strict):
        def body(j, acc):
            for part in range(SCAN_TILE // COUNT_TILE):
                start = pl.multiple_of(j * SCAN_TILE + part * COUNT_TILE, COUNT_TILE)
                x = score_scr[pl.ds(start, COUNT_TILE), :]
                ind = jnp.where((x > cand) if strict else (x >= cand), 1, 0)
                acc = acc + jnp.sum(ind.reshape(COUNT_TILE // 8, 8, LANES), axis=0)
            return acc
        acc = lax.fori_loop(0, n_st, body, jnp.zeros((8, LANES), I32))
        return jnp.sum(acc, axis=0, keepdims=True)

    def to_pattern(v):
        bits = lax.bitcast_convert_type(v, I32)
        return bits ^ ((bits >> 31) & 0x7FFFFFFF)

    def to_f32(c):
        return lax.bitcast_convert_type(c ^ ((c >> 31) & 0x7FFFFFFF), F32)

    zero_row = jnp.zeros((1, LANES), F32)
    n_visible = t0 + 1 + lax.broadcasted_iota(I32, (1, LANES), 1)
    count0 = count_f32(zero_row, strict=False)
    count_pos = count_f32(zero_row, strict=True)
    nonneg = count0 >= topk
    lo = jnp.where(nonneg, 0, to_pattern(bottom))
    hi = jnp.where(nonneg, to_pattern(top) + 1, 0)
    count_lo = jnp.where(nonneg, count0, n_visible)
    count_hi = jnp.where(nonneg, 0, count0)
    few = n_visible < topk
    lo = jnp.where(few, PATTERN_NEG_FLT_MAX, lo)
    zero_tied = nonneg & (count_pos < topk)
    open_q = jnp.where(few | zero_tied | (count_lo == topk), 0, 1)

    log_topk = math.log(topk)

    def count_error(count):
        return jnp.log(count.astype(F32) + 0.5) - log_topk

    def probes_left(st):
        n_pass, n_open = st[0], st[-1]
        return (n_pass < MAX_SEARCH_PASSES) & (n_open > 0)

    def probe_group(st):
        n_pass = st[0]

        def probe(step, carry):
            lo, hi, count_lo, count_hi, err_lo, err_hi, last_side, open_q = carry
            v_lo, v_hi = to_f32(lo), to_f32(hi)
            frac = err_lo / (err_lo - err_hi)
            frac = jnp.where(count_lo - count_hi <= 4, 0.5, frac)
            guess = to_pattern(v_lo + (v_hi - v_lo) * frac)
            middle = lo + lax.shift_right_logical(hi - lo, 1)
            cand = jnp.where(n_pass + step >= INTERPOLATED_PASSES, middle, guess)
            cand = jnp.minimum(jnp.maximum(cand, lo + 1), hi - 1)
            count = count_f32(to_f32(cand), strict=False)
            raise_lo = (open_q == 1) & (count >= topk)
            lower_hi = (open_q == 1) & (count < topk)
            err = count_error(count)
            err_hi = jnp.where(raise_lo & (last_side == 1), err_hi * 0.5, err_hi)
            err_lo = jnp.where(lower_hi & (last_side == -1), err_lo * 0.5, err_lo)
            err_lo = jnp.where(raise_lo, err, err_lo)
            err_hi = jnp.where(lower_hi, err, err_hi)
            lo = jnp.where(raise_lo, cand, lo)
            count_lo = jnp.where(raise_lo, count, count_lo)
            hi = jnp.where(lower_hi, cand, hi)
            count_hi = jnp.where(lower_hi, count, count_hi)
            last_side = jnp.where(raise_lo, 1, jnp.where(lower_hi, -1, last_side))
            closed = (count_lo == topk) | (hi - lo == 1)
            return lo, hi, count_lo, count_hi, err_lo, err_hi, last_side, jnp.where(closed, 0, open_q)

        carry = lax.fori_loop(0, PASSES_PER_CHECK, probe, st[1:-1])
        return (n_pass + PASSES_PER_CHECK,) + carry + (jnp.sum(carry[-1]),)

    state = (jnp.int32(0), lo, hi, count_lo, count_hi, count_error(count_lo), count_error(count_hi),
             jnp.zeros((1, LANES), I32), open_q, jnp.sum(open_q))
    state = lax.while_loop(probes_left, probe_group, state)
    tau = to_f32(state[1])
    need = (topk - count_f32(tau, strict=True)).astype(F32)

    def mask_body(jc, tie_carry):
        rows = [pl.ds(key_tile(jc * (SCAN_TILE // KEY_TILE) + sub), KEY_TILE)
                for sub in range(SCAN_TILE // KEY_TILE)]
        xs = [score_scr[r, :] for r in rows]
        ties = [x == tau for x in xs]
        tie_cols = jnp.concatenate([jnp.where(tie, 1.0, 0.0).astype(BF16) for tie in ties], axis=1)
        all_ranks = _dot(tri_ref[...], tie_cols)
        ranks = [all_ranks[:, n * LANES:(n + 1) * LANES] for n in range(len(ties))]
        for r, x, tie, rank in zip(rows, xs, ties, ranks):
            rank = rank + tie_carry
            tie_carry = rank[KEY_TILE - 1:KEY_TILE, :]
            take = (tie & (rank <= need)) | (x > tau)
            score_scr[r, :] = jnp.where(take, 0.0, MASKED_LOGIT)
        return tie_carry

    lax.fori_loop(0, n_st, mask_body, jnp.zeros((1, LANES), F32))

    m_scr[...] = jnp.full(m_scr.shape, MASKED_LOGIT, F32)
    l_scr[...] = jnp.zeros(l_scr.shape, F32)
    acc_scr[...] = jnp.zeros(acc_scr.shape, F32)
    ones_rows = jnp.ones((16, KEY_TILE), BF16)

    def qk(slot, j):
        s0 = key_tile(j)
        for p in range(n_pairs):
            s_scr[slot, :, p * pair_w:(p + 1) * pair_w] = _dot(
                k_ref[pl.ds(s0, KEY_TILE), p * LANES:(p + 1) * LANES], wq_scr[p])

    def softmax(slot, j, near):
        mask_add = score_scr[pl.ds(key_tile(j), KEY_TILE), :]
        if near:
            kind_top = jnp.clip(2 * j - i + 2, 0, 2)
            kind_bot = jnp.clip(2 * j - i + 3, 0, 2)
        alphas = []
        for h in range(N_ATTN_HEADS):
            cols = slice(h * LANES, (h + 1) * LANES)
            logits = s_scr[slot, :, cols] + mask_add
            if near:
                logits = logits + jnp.concatenate(
                    [bias_ref[kind_top, :, cols], bias_ref[kind_bot, :, cols]], axis=0)
            m_old = m_scr[h:h + 1, :]
            m_new = jnp.maximum(m_old, jnp.max(logits, axis=0, keepdims=True))
            m_scr[h:h + 1, :] = m_new
            p_scr[slot, :, cols] = jnp.exp2(logits - m_new).astype(BF16)
            alphas.append(jnp.exp2(m_old - m_new))
        return alphas

    def pv(slot, j, alphas):
        s0 = key_tile(j)
        for p in range(n_pairs):
            lhs = jnp.concatenate([vt_ref[p * 2 * hd:(p + 1) * 2 * hd, pl.ds(s0, KEY_TILE)], ones_rows], axis=0)
            out = _dot(lhs, p_scr[slot, :, p * pair_w:(p + 1) * pair_w])
            for half in range(2):
                h = 2 * p + half
                rows_h = slice(h * hd, (h + 1) * hd)
                q_cols = slice(half * LANES, (half + 1) * LANES)
                acc_scr[rows_h, :] = acc_scr[rows_h, :] * alphas[h] + out[half * hd:(half + 1) * hd, q_cols]
                l_scr[h:h + 1, :] = l_scr[h:h + 1, :] * alphas[h] + out[2 * hd:2 * hd + 1, q_cols]

    def pending_alphas():
        return [alpha_scr[h:h + 1, :] for h in range(N_ATTN_HEADS)]

    def clear_pending():
        p_scr[1] = jnp.zeros(p_scr.shape[1:], BF16)
        alpha_scr[...] = jnp.ones(alpha_scr.shape, F32)

    def pair_step(ja, j_pending, j_next, near):
        pv(1, j_pending, pending_alphas())
        qk(1, ja + 1)
        alphas_a = softmax(0, ja, near)
        pv(0, ja, alphas_a)
        qk(0, j_next)
        alphas_b = softmax(1, ja + 1, near)
        for h in range(N_ATTN_HEADS):
            alpha_scr[h:h + 1, :] = alphas_b[h]

    def single_step(ja, j_pending, near):
        pv(1, j_pending, pending_alphas())
        pv(0, ja, softmax(0, ja, near))
        clear_pending()

    n_far = 2 * (jnp.maximum(n_kt - 2, 0) // 2)
    n_near = n_kt - n_far
    last = n_kt - 1
    clear_pending()
    qk(0, 0)

    def far_body(jp, carry):
        pair_step(2 * jp, jnp.maximum(2 * jp - 1, 0), 2 * jp + 2, near=False)
        return carry

    lax.fori_loop(0, n_far // 2, far_body, 0)

    @pl.when(n_near >= 2)
    def _():
        pair_step(n_far, jnp.maximum(n_far - 1, 0), jnp.minimum(n_far + 2, last), near=True)

    @pl.when(n_near % 2 == 1)
    def _():
        single_step(last, jnp.where(n_near == 3, n_far + 1, jnp.maximum(n_far - 1, 0)), near=True)

    pv(1, last, pending_alphas())

    for h in range(N_ATTN_HEADS):
        rows_h = slice(h * hd, (h + 1) * hd)
        acc_scr[rows_h, :] = acc_scr[rows_h, :] / l_scr[h:h + 1, :]
    o_ref[...] = acc_scr[...].T


def _dsa_kernel(*refs, topk, t_valid):
    o_ref = refs[8]
    i = pl.program_id(1)
    is_real = i * LANES < t_valid

    @pl.when(is_real)
    def _():
        _dsa_block(i, *refs, topk=topk)

    @pl.when(jnp.logical_not(is_real))
    def _():
        o_ref[...] = jnp.zeros(o_ref.shape, F32)


def _dsa_attention(qit, rows, qt, kidx, k, vt, bias, tri, batch, tp, t_valid, topk):
    n_rows = k.shape[0]
    nqb = tp // LANES
    d_idx = N_IDX_HEADS * IDX_DIM
    n_pairs = N_ATTN_HEADS // 2
    key_rows = -(-tp // SCAN_TILE) * SCAN_TILE
    q_cols = lambda height: pl.BlockSpec((None, height, LANES), lambda b, i: (b, 0, i))
    return pl.pallas_call(
        functools.partial(_dsa_kernel, topk=topk, t_valid=t_valid),
        out_shape=jax.ShapeDtypeStruct((n_rows, D_ATTN), F32),
        grid=(batch, nqb),
        in_specs=[q_cols(d_idx),
                  pl.BlockSpec((16, LANES), lambda b, i: (0, b * nqb + i)),
                  q_cols(D_ATTN),
                  pl.BlockSpec((tp, LANES), lambda b, i: (b, 0)),
                  pl.BlockSpec((tp, D_ATTN), lambda b, i: (b, 0)),
                  pl.BlockSpec((None, D_ATTN, tp), lambda b, i: (b, 0, 0)),
                  pl.BlockSpec((3, LANES, N_ATTN_HEADS * LANES), lambda b, i: (0, 0, 0)),
                  pl.BlockSpec((KEY_TILE, KEY_TILE), lambda b, i: (0, 0))],
        out_specs=pl.BlockSpec((LANES, D_ATTN), lambda b, i: (b * nqb + i, 0)),
        scratch_shapes=[pltpu.VMEM((key_rows, LANES), F32),
                        pltpu.VMEM((LANES, N_IDX_HEADS * LANES), BF16),
                        pltpu.VMEM((n_pairs, LANES, 2 * LANES), BF16),
                        pltpu.VMEM((2, KEY_TILE, N_ATTN_HEADS * LANES), F32),
                        pltpu.VMEM((2, KEY_TILE, N_ATTN_HEADS * LANES), BF16),
                        pltpu.VMEM((N_ATTN_HEADS, LANES), F32),
                        pltpu.VMEM((N_ATTN_HEADS, LANES), F32),
                        pltpu.VMEM((D_ATTN, LANES), F32),
                        pltpu.VMEM((N_ATTN_HEADS, LANES), F32)],
        compiler_params=_params("parallel", "parallel"),
        name="dsa_attention",
    )(qit, rows, qt, kidx, k, vt, bias, tri)


def _gdn_prep_kernel(x_ref, halo_ref, cw_ref, rows_ref, alog_ref, dtb_ref, q_ref, k_ref, v_ref, bg_ref, buf):
    first = pl.program_id(1) == 0
    rb = x_ref.shape[0]
    buf[0:HALO_ROWS, :] = jnp.where(first, 0.0, halo_ref[...])
    buf[HALO_ROWS:, :] = x_ref[...]
    acc = jnp.zeros((rb, 3 * D_GDN), F32)
    for tap in range(CONV_WIDTH):
        start = HALO_ROWS - (CONV_WIDTH - 1) + tap
        acc = acc + cw_ref[tap:tap + 1, :] * buf[start:start + rb, :]
    y = _silu(acc)
    for h in range(N_GDN_HEADS):
        cols = slice(h * GDN_HEAD_DIM, (h + 1) * GDN_HEAD_DIM)
        qh = y[:, cols]
        kh = y[:, D_GDN + h * GDN_HEAD_DIM:D_GDN + (h + 1) * GDN_HEAD_DIM]
        q_ref[:, cols] = (qh * lax.rsqrt(jnp.sum(qh * qh, axis=-1, keepdims=True) + EPS)
                          * (GDN_HEAD_DIM ** -0.5))
        k_ref[:, cols] = kh * lax.rsqrt(jnp.sum(kh * kh, axis=-1, keepdims=True) + EPS)
    v_ref[...] = y[:, 2 * D_GDN:]
    rows = rows_ref[...]
    beta = _sigmoid(rows[4:8, :])
    a = rows[8:12, :] + dtb_ref[...]
    softplus = jnp.maximum(a, 0.0) + jnp.log1p(jnp.exp(-jnp.abs(a)))
    bg_ref[0:4, :] = beta
    bg_ref[4:8, :] = -jnp.exp(alog_ref[...]) * softplus


def _gdn_prep(proj, rows, lw, batch, tp, rb):
    n_rows = proj.shape[0]
    nb = tp // rb
    halo_per_block = rb // HALO_ROWS
    row_spec = pl.BlockSpec((rb, D_GDN), lambda b, i: (b * nb + i, 0))
    return pl.pallas_call(
        _gdn_prep_kernel,
        out_shape=(jax.ShapeDtypeStruct((n_rows, D_GDN), F32),) * 3
        + (jax.ShapeDtypeStruct((8, n_rows), F32),),
        grid=(batch, nb),
        in_specs=[pl.BlockSpec((rb, 3 * D_GDN), lambda b, i: (b * nb + i, 0)),
                  pl.BlockSpec((HALO_ROWS, 3 * D_GDN),
                               lambda b, i: (jnp.maximum((b * nb + i) * halo_per_block - 1, 0), 0)),
                  pl.BlockSpec((CONV_WIDTH, 3 * D_GDN), lambda b, i: (0, 0)),
                  pl.BlockSpec((16, rb), lambda b, i: (0, b * nb + i)),
                  pl.BlockSpec((N_GDN_HEADS, 1), lambda b, i: (0, 0)),
                  pl.BlockSpec((N_GDN_HEADS, 1), lambda b, i: (0, 0))],
        out_specs=(row_spec, row_spec, row_spec,
                   pl.BlockSpec((8, rb), lambda b, i: (0, b * nb + i))),
        scratch_shapes=[pltpu.VMEM((HALO_ROWS + rb, 3 * D_GDN), F32)],
        compiler_params=_params("parallel", "parallel"),
        name="gdn_prep",
    )(proj, proj, lw["conv_w"], rows, lw["a_log"], lw["dt_bias"])


def _gdn_chunk_kernel(q_ref, k_ref, v_ref, bg_ref, m_ref, n_ref, p_ref, r_ref, cd_ref):
    c = LANES
    heads = range(N_GDN_HEADS)
    bg = bg_ref[...]
    lane8 = lax.broadcasted_iota(I32, (8, c), 1)
    dec = bg
    shift = 1
    while shift < c:
        dec = dec + jnp.where(lane8 >= shift, pltpu.roll(dec, shift, 1), 0.0)
        shift *= 2
    row = lax.broadcasted_iota(I32, (c, c), 0)
    col = lax.broadcasted_iota(I32, (c, c), 1)
    tri = row >= col
    strict = row > col
    eye = jnp.where(row == col, 1.0, 0.0)

    cols = [slice(h * GDN_HEAD_DIM, (h + 1) * GDN_HEAD_DIM) for h in heads]
    d_row = [jnp.broadcast_to(dec[4 + h:5 + h, :], (c, c)) for h in heads]
    d_col = [d.T for d in d_row]
    beta_col = [jnp.broadcast_to(bg[h:h + 1, :], (c, c)).T for h in heads]
    d_last = [d[:, c - 1:c] for d in d_row]
    gamma = [jnp.exp(jnp.where(tri, d_col[h] - d_row[h], MASKED_LOGIT)) for h in heads]
    exp_d = [jnp.exp(d_col[h]) for h in heads]
    k16 = [k_ref[:, cols[h]].astype(BF16) for h in heads]
    kb = [k_ref[:, cols[h]] * beta_col[h] for h in heads]
    nil = [jnp.where(strict, _dot_nt(kb[h].astype(BF16), k16[h]) * gamma[h], 0.0) for h in heads]
    nil_parts = [_split_bf16(x) for x in nil]
    inv = [eye - x for x in nil]
    power = [_dot_split(x, x) for x in nil_parts]
    steps = int(math.log2(c)) - 1
    for it in range(steps):
        power_parts = [_split_bf16(x) for x in power]
        inv = [inv[h] + _dot_split(_split_bf16(inv[h]), power_parts[h]) for h in heads]
        if it + 1 < steps:
            power = [_dot_split(x, x) for x in power_parts]
    rhs = [jnp.concatenate([kb[h] * exp_d[h], v_ref[:, cols[h]] * beta_col[h]], axis=1) for h in heads]
    wu = [_dot_split(_split_bf16(inv[h]), _split_bf16(rhs[h])).astype(BF16) for h in heads]
    aqk = [jnp.where(tri, _dot_nt(q_ref[:, cols[h]].astype(BF16), k16[h]) * gamma[h], 0.0).astype(BF16)
           for h in heads]
    kd_t = [(k_ref[:, cols[h]] * jnp.exp(d_last[h] - d_col[h])).T.astype(BF16) for h in heads]
    state_wu = [_dot(kd_t[h], wu[h]) for h in heads]
    out_wu = [_dot(aqk[h], wu[h]) for h in heads]
    for h in heads:
        m_ref[h] = (-state_wu[h][:, :c]).astype(BF16)
        n_ref[h] = state_wu[h][:, c:]
        p_ref[h] = (q_ref[:, cols[h]] * exp_d[h] - out_wu[h][:, :c]).astype(BF16)
        r_ref[h] = out_wu[h][:, c:]
        cd_ref[h:h + 1, :] = jnp.exp(d_last[h][0:1, :] + jnp.zeros((1, c), F32))


def _gdn_chunks(qn, kn, vv, bg):
    n_rows = qn.shape[0]
    nc = n_rows // LANES
    tok = pl.BlockSpec((LANES, D_GDN), lambda c: (c, 0))
    mat = pl.BlockSpec((None, N_GDN_HEADS, LANES, LANES), lambda c: (c, 0, 0, 0))
    mat_shape = lambda dt: jax.ShapeDtypeStruct((nc, N_GDN_HEADS, LANES, LANES), dt)
    return pl.pallas_call(
        _gdn_chunk_kernel,
        out_shape=(mat_shape(BF16), mat_shape(F32), mat_shape(BF16), mat_shape(F32),
                   jax.ShapeDtypeStruct((nc, N_GDN_HEADS, LANES), F32)),
        grid=(nc,),
        in_specs=[tok, tok, tok, pl.BlockSpec((8, LANES), lambda c: (0, c))],
        out_specs=(mat, mat, mat, mat, pl.BlockSpec((None, N_GDN_HEADS, LANES), lambda c: (c, 0, 0))),
        compiler_params=_params("parallel"),
        name="gdn_chunks",
    )(qn, kn, vv, bg)


def _gdn_scan_kernel(m_ref, n_ref, p_ref, r_ref, cd_ref, o_ref, s_scr, *, batch):
    @pl.when(pl.program_id(0) == 0)
    def _():
        s_scr[...] = jnp.zeros(s_scr.shape, F32)

    for b in range(batch):
        for h in range(N_GDN_HEADS):
            s = s_scr[b, h]
            s16 = s.astype(BF16)
            o_ref[b, :, h * GDN_HEAD_DIM:(h + 1) * GDN_HEAD_DIM] = _dot(p_ref[b, h], s16) + r_ref[b, h]
            s_scr[b, h] = s * cd_ref[b, h:h + 1, :] + _dot(m_ref[b, h], s16) + n_ref[b, h]


def _gdn_scan(m_mat, n_mat, p_mat, r_mat, cd, batch, tp):
    nc = tp // LANES
    shape5 = lambda a: a.reshape(batch, nc, N_GDN_HEADS, LANES, LANES)
    mat = pl.BlockSpec((batch, None, N_GDN_HEADS, LANES, LANES), lambda c: (0, c, 0, 0, 0))
    return pl.pallas_call(
        functools.partial(_gdn_scan_kernel, batch=batch),
        out_shape=jax.ShapeDtypeStruct((batch, tp, D_GDN), F32),
        grid=(nc,),
        in_specs=[mat, mat, mat, mat,
                  pl.BlockSpec((batch, None, N_GDN_HEADS, LANES), lambda c: (0, c, 0, 0))],
        out_specs=pl.BlockSpec((batch, LANES, D_GDN), lambda c: (0, c, 0)),
        scratch_shapes=[pltpu.VMEM((batch, N_GDN_HEADS, LANES, LANES), F32)],
        compiler_params=_params("arbitrary"),
        name="gdn_scan",
    )(shape5(m_mat), shape5(n_mat), shape5(p_mat), shape5(r_mat), cd.reshape(batch, nc, N_GDN_HEADS, LANES))


def _out_proj_kernel(h_ref, oa_ref, za_ref, og_ref, zg_ref, gain_ref, w_ref, out_ref):
    attn = (oa_ref[...] * _silu(za_ref[...])).astype(BF16)
    y = _dot(attn, w_ref[0:D_ATTN, :])
    og, zg = og_ref[...], zg_ref[...]
    for h in range(N_GDN_HEADS):
        cols = slice(h * GDN_HEAD_DIM, (h + 1) * GDN_HEAD_DIM)
        gated = (_rms_rows(og[:, cols], gain_ref[...]) * _silu(zg[:, cols])).astype(BF16)
        y = y + _dot(gated, w_ref[D_ATTN + h * GDN_HEAD_DIM:D_ATTN + (h + 1) * GDN_HEAD_DIM, :])
    out_ref[...] = h_ref[...] + y


def _out_proj(h2, o_attn, proj, o_gdn, gain, w_out, rb):
    n_rows = h2.shape[0]
    blk = lambda width, col: pl.BlockSpec((rb, width), lambda i: (i, col))
    return pl.pallas_call(
        _out_proj_kernel,
        out_shape=jax.ShapeDtypeStruct((n_rows, D_MODEL), F32),
        grid=(n_rows // rb,),
        in_specs=[blk(D_MODEL, 0), blk(D_ATTN, 0), blk(D_ATTN, COL_Z_ATTN), blk(D_GDN, 0),
                  blk(D_GDN, COL_Z_GDN),
                  pl.BlockSpec((1, GDN_HEAD_DIM), lambda i: (0, 0)),
                  pl.BlockSpec((D_MODEL, D_MODEL), lambda i: (0, 0))],
        out_specs=blk(D_MODEL, 0),
        compiler_params=_params("parallel"),
        name="out_proj",
    )(h2, o_attn, proj, o_gdn, proj, gain, w_out)


def _pack_layer(norm_gain, w_in, cq_gain, ckv_gain, w_uq, w_ukv, w_q_idx, q_gain, k_gain, conv_w, a_log,
                dt_bias, gdn_gain, w_out):
    o = 0
    parts = {}
    for name, size in (("c_q", Q_RANK), ("c_kv", KV_RANK), ("k_idx", IDX_DIM), ("w_idx", N_IDX_HEADS),
                       ("z_attn", D_ATTN), ("qkv_g", 3 * D_GDN), ("z_g", D_GDN), ("b", N_GDN_HEADS),
                       ("a", N_GDN_HEADS)):
        parts[name] = w_in[:, o:o + size]
        o += size
    w_packed = jnp.concatenate([parts["qkv_g"], parts["c_q"], parts["c_kv"], parts["k_idx"], parts["k_idx"],
                                parts["z_attn"], parts["z_g"]], axis=1).astype(BF16)
    w_rows = jnp.concatenate([parts["w_idx"], parts["b"], parts["a"],
                              jnp.zeros((D_MODEL, 4), w_in.dtype)], axis=1).T.astype(BF16)
    head_of_col = jnp.arange(D_ATTN) // ATTN_HEAD_DIM
    head_sel = (head_of_col[:, None] == jnp.arange(LANES)[None, :]).astype(F32)
    return dict(
        gain=norm_gain[None, :], w_packed=w_packed, w_rows=w_rows,
        g_cq=cq_gain[None, :], g_ckv=ckv_gain[None, :],
        w_uqt=w_uq.T.astype(BF16), w_qit=w_q_idx.T.astype(BF16),
        w_uk=w_ukv[:, :D_ATTN].astype(BF16), w_uvt=w_ukv[:, D_ATTN:].T.astype(BF16),
        g_q_col=q_gain[:, None], g_k=jnp.tile(k_gain, N_ATTN_HEADS)[None, :],
        head_sel=head_sel, head_sel_t=head_sel.T,
        conv_w=conv_w, a_log=a_log[:, None], dt_bias=dt_bias[:, None],
        gdn_gain=gdn_gain[None, :], w_out=w_out.astype(BF16))


def _layer(h2, lw, bias, tri, batch, tp, t_valid, topk):
    rb = _row_block(tp)
    proj, rows = _in_proj(h2, lw["gain"], lw["w_packed"], lw["w_rows"], rb)
    qt, k, vt, qit, kidx = _dsa_prep(proj, lw, batch, tp, rb)
    o_attn = _dsa_attention(qit, rows, qt, kidx, k, vt, bias, tri, batch, tp, t_valid, topk)
    qn, kn, vv, bg = _gdn_prep(proj, rows, lw, batch, tp, rb)
    m_mat, n_mat, p_mat, r_mat, cd = _gdn_chunks(qn, kn, vv, bg)
    o_gdn = _gdn_scan(m_mat, n_mat, p_mat, r_mat, cd, batch, tp).reshape(batch * tp, D_GDN)
    return _out_proj(h2, o_attn, proj, o_gdn, lw["gdn_gain"], lw["w_out"], rb)


def _forward(x, meta_tokens, rel_bias_table, layer_weights, topk):
    batch, seq, _ = x.shape
    t = seq + N_META
    tp = -(-t // KEY_TILE) * KEY_TILE
    meta = jnp.broadcast_to(meta_tokens[None].astype(x.dtype), (batch, N_META, D_MODEL))
    h = jnp.concatenate([meta, x, jnp.zeros((batch, tp - t, D_MODEL), x.dtype)], axis=1)
    h2 = h.reshape(batch * tp, D_MODEL)
    bias = _bias_tiles(rel_bias_table)
    tri = jnp.tril(jnp.ones((KEY_TILE, KEY_TILE), BF16))
    for lw in layer_weights:
        h2 = _layer(h2, lw, bias, tri, batch, tp, t, topk)
    return h2.reshape(batch, tp, D_MODEL)[:, N_META:t]


def kernel(x, meta_tokens, rel_bias_table, norm_gain, w_in, cq_norm_gain, ckv_norm_gain, w_uq, w_ukv, w_q_idx,
           q_norm_gain, k_norm_gain, conv_w, a_log, dt_bias, gdn_norm_gain, w_out):
    depth = norm_gain.shape[0]
    topk = min(TOPK_MAX, x.shape[1] // 4)
    layers = [_pack_layer(norm_gain[l], w_in[l], cq_norm_gain[l], ckv_norm_gain[l], w_uq[l], w_ukv[l],
                          w_q_idx[l], q_norm_gain[l], k_norm_gain[l], conv_w[l], a_log[l], dt_bias[l],
                          gdn_norm_gain[l], w_out[l]) for l in range(depth)]
    return _forward(x, meta_tokens, rel_bias_table, layers, topk)
```

```python
import functools
import math

import jax
import jax.numpy as jnp
from jax import lax
from jax.experimental import pallas as pl
from jax.experimental.pallas import tpu as pltpu

F32 = jnp.float32
BF16 = jnp.bfloat16
I32 = jnp.int32

D_MODEL = 1024
N_META = 16
EPS = 1e-6
N_ATTN_HEADS = 8
ATTN_HEAD_DIM = 64
D_ATTN = N_ATTN_HEADS * ATTN_HEAD_DIM
Q_RANK = 256
KV_RANK = 128
N_IDX_HEADS = 4
IDX_DIM = 64
TOPK_MAX = 256
N_REL_BUCKETS = 32
REL_MAX_DIST = 128
N_GDN_HEADS = 4
GDN_HEAD_DIM = 128
D_GDN = N_GDN_HEADS * GDN_HEAD_DIM
CONV_WIDTH = 4

LANES = 128
KEY_TILE = 256
COUNT_TILE = 512
SCAN_TILE = 1024
PASSES_PER_CHECK = 2
INTERPOLATED_PASSES = 24
MAX_SEARCH_PASSES = INTERPOLATED_PASSES + 32
GDN_CHUNKS_PER_STEP = 2
ROW_TILES = (5, 4, 3, 2, 1)
HALO_ROWS = 8
D_PACKED = 3 * D_GDN + 512 + D_ATTN + D_GDN
COL_SMALL = 3
COL_Z_ATTN = 4
COL_Z_GDN = 5
MASKED_LOGIT = -1e30
LOG2_E = math.log2(math.e)
KEY_MIN = -2 ** 31
PATTERN_NEG_FLT_MAX = KEY_MIN + (1 << 23)
VMEM_LIMIT = 56 * 1024 * 1024

NT_DIMS = (((1,), (1,)), ((), ()))


def _dot(a, b):
    return jnp.dot(a, b, preferred_element_type=F32)


def _dot_nt(a, b):
    return lax.dot_general(a, b, NT_DIMS, preferred_element_type=F32)


def _split_bf16(x):
    hi = x.astype(BF16)
    return hi, (x - hi.astype(F32)).astype(BF16)


def _dot_split(a_parts, b_parts):
    (a_hi, a_lo), (b_hi, b_lo) = a_parts, b_parts
    return _dot(a_hi, b_hi) + (_dot(a_hi, b_lo) + _dot(a_lo, b_hi))


def _sigmoid(x):
    return 1.0 / (1.0 + jnp.exp(-x))


def _silu(x):
    return x * _sigmoid(x)


def _row_block(tp):
    tiles = tp // LANES
    return LANES * next(d for d in ROW_TILES if tiles % d == 0)


def _params(*sem):
    return pltpu.CompilerParams(dimension_semantics=sem, vmem_limit_bytes=VMEM_LIMIT)


def _bias_kernel(table_ref, out_ref):
    row = lax.broadcasted_iota(I32, (LANES, LANES), 0)
    col = lax.broadcasted_iota(I32, (LANES, LANES), 1)
    max_exact = N_REL_BUCKETS // 2
    for kind in range(3):
        dist = col - row + (2 - kind) * LANES
        n = jnp.maximum(dist, 0)
        nf = jnp.maximum(n, 1).astype(F32)
        large = max_exact + (jnp.log(nf / max_exact) / math.log(REL_MAX_DIST / max_exact)
                             * (N_REL_BUCKETS - max_exact)).astype(I32)
        large = jnp.minimum(large, N_REL_BUCKETS - 1)
        bucket = jnp.where(n < max_exact, n, large)
        for h in range(N_ATTN_HEADS):
            tile = jnp.zeros((LANES, LANES), F32)
            for b in range(N_REL_BUCKETS):
                tile = jnp.where(bucket == b, table_ref[b, h], tile)
            far = table_ref[N_REL_BUCKETS - 1, h]
            out_ref[kind, :, h * LANES:(h + 1) * LANES] = (tile - far) * LOG2_E


def _bias_tiles(rel_table):
    return pl.pallas_call(
        _bias_kernel,
        out_shape=jax.ShapeDtypeStruct((3, LANES, N_ATTN_HEADS * LANES), F32),
        in_specs=[pl.BlockSpec(memory_space=pltpu.SMEM)],
        out_specs=pl.BlockSpec(memory_space=pltpu.VMEM),
        name="rel_bias_tiles",
    )(rel_table)


def _in_proj_kernel(h_ref, gain_ref, w_ref, wrows_ref, proj_ref, rows_ref):
    x = h_ref[...]
    y = x * lax.rsqrt(jnp.mean(x * x, axis=-1, keepdims=True) + EPS)
    hn = (y * gain_ref[...]).astype(BF16)
    proj_ref[...] = _dot(hn, w_ref[...])
    rows_ref[...] = _dot_nt(wrows_ref[...], hn)


def _in_proj(h2, gain, w_packed, w_rows, rb):
    n_rows = h2.shape[0]
    grid = (n_rows // rb,)
    return pl.pallas_call(
        _in_proj_kernel,
        out_shape=(jax.ShapeDtypeStruct((n_rows, D_PACKED), F32),
                   jax.ShapeDtypeStruct((16, n_rows), F32)),
        grid=grid,
        in_specs=[pl.BlockSpec((rb, D_MODEL), lambda i: (i, 0)),
                  pl.BlockSpec((1, D_MODEL), lambda i: (0, 0)),
                  pl.BlockSpec((D_MODEL, D_PACKED), lambda i: (0, 0)),
                  pl.BlockSpec((16, D_MODEL), lambda i: (0, 0))],
        out_specs=(pl.BlockSpec((rb, D_PACKED), lambda i: (i, 0)),
                   pl.BlockSpec((16, rb), lambda i: (0, i))),
        compiler_params=_params("parallel"),
        name="in_proj",
    )(h2, gain, w_packed, w_rows)


def _rms_rows(x, gain):
    return x * lax.rsqrt(jnp.mean(x * x, axis=-1, keepdims=True) + EPS) * gain


def _dsa_prep_kernel(sm_ref, gcq_ref, gckv_ref, wuqt_ref, wqit_ref, wukt_ref, wuvt_ref, gq_ref, gk_ref,
                     qt_ref, k_ref, vt_ref, qit_ref, kidx_ref):
    sm = sm_ref[...]
    rb = sm.shape[0]
    cq = _rms_rows(sm[:, :Q_RANK], gcq_ref[...]).astype(BF16)
    ckv = _rms_rows(sm[:, Q_RANK:Q_RANK + KV_RANK], gckv_ref[...]).astype(BF16)
    kidx_ref[...] = sm[:, Q_RANK + KV_RANK:].astype(BF16)
    q3 = _dot_nt(wuqt_ref[...], cq).reshape(N_ATTN_HEADS, ATTN_HEAD_DIM, rb)
    q3 = q3 * lax.rsqrt(jnp.mean(q3 * q3, axis=1, keepdims=True) + EPS) * gq_ref[...][None]
    qt_ref[...] = (q3 * (ATTN_HEAD_DIM ** -0.5 * LOG2_E)).reshape(D_ATTN, rb).astype(BF16)
    k3 = _dot_nt(wukt_ref[...], ckv).reshape(N_ATTN_HEADS, ATTN_HEAD_DIM, rb)
    k3 = k3 * lax.rsqrt(jnp.mean(k3 * k3, axis=1, keepdims=True) + EPS) * gk_ref[...][None]
    k_ref[...] = k3.reshape(D_ATTN, rb).T.astype(BF16)
    vt_ref[...] = _dot_nt(wuvt_ref[...], ckv).astype(BF16)
    qit_ref[...] = _dot_nt(wqit_ref[...], cq).astype(BF16)


def _dsa_prep(proj, lw, batch, tp, rb):
    n_rows = proj.shape[0]
    nb = tp // rb
    d_idx = N_IDX_HEADS * IDX_DIM
    const = lambda shape: pl.BlockSpec(shape, lambda b, i: (0, 0))
    row_spec = lambda width: pl.BlockSpec((rb, width), lambda b, i: (b * nb + i, 0))
    col_spec = lambda height: pl.BlockSpec((None, height, rb), lambda b, i: (b, 0, i))
    return pl.pallas_call(
        _dsa_prep_kernel,
        out_shape=(jax.ShapeDtypeStruct((batch, D_ATTN, tp), BF16),
                   jax.ShapeDtypeStruct((n_rows, D_ATTN), BF16),
                   jax.ShapeDtypeStruct((batch, D_ATTN, tp), BF16),
                   jax.ShapeDtypeStruct((batch, d_idx, tp), BF16),
                   jax.ShapeDtypeStruct((n_rows, LANES), BF16)),
        grid=(batch, nb),
        in_specs=[pl.BlockSpec((rb, 512), lambda b, i: (b * nb + i, COL_SMALL)),
                  const((1, Q_RANK)), const((1, KV_RANK)),
                  const((D_ATTN, Q_RANK)), const((d_idx, Q_RANK)),
                  const((D_ATTN, KV_RANK)), const((D_ATTN, KV_RANK)),
                  const((ATTN_HEAD_DIM, 1)), const((ATTN_HEAD_DIM, 1))],
        out_specs=(col_spec(D_ATTN), row_spec(D_ATTN), col_spec(D_ATTN), col_spec(d_idx), row_spec(LANES)),
        compiler_params=_params("parallel", "parallel"),
        name="dsa_prep",
    )(proj, lw["g_cq"], lw["g_ckv"], lw["w_uqt"], lw["w_qit"], lw["w_ukt"], lw["w_uvt"], lw["g_q_col"],
      lw["g_k_col"])


def _dsa_block(i, qit_ref, rows_ref, qt_ref, kidx_ref, k_ref, vt_ref, bias_ref, tri_ref, o_ref,
               score_scr, wi_scr, wq_scr, s_scr, p_scr, m_scr, l_scr, acc_scr, alpha_scr, *, topk):
    t0 = i * LANES
    n_kt = i // 2 + 1
    n_ct = (n_kt + 1) // 2
    n_st = (n_ct + 1) // 2
    hd = ATTN_HEAD_DIM
    pair_w = 2 * LANES
    n_pairs = N_ATTN_HEADS // 2

    zeros_hd = jnp.zeros((hd, LANES), BF16)
    for h in range(N_IDX_HEADS):
        wi_scr[0:IDX_DIM, h * LANES:(h + 1) * LANES] = qit_ref[h * IDX_DIM:(h + 1) * IDX_DIM, :]
    wi_scr[IDX_DIM:, :] = jnp.zeros((LANES - IDX_DIM, N_IDX_HEADS * LANES), BF16)
    for p in range(n_pairs):
        wq_scr[p, 0:hd, 0:LANES] = qt_ref[2 * p * hd:(2 * p + 1) * hd, :]
        wq_scr[p, 0:hd, LANES:] = zeros_hd
        wq_scr[p, hd:, 0:LANES] = zeros_hd
        wq_scr[p, hd:, LANES:] = qt_ref[(2 * p + 1) * hd:(2 * p + 2) * hd, :]

    row = lax.broadcasted_iota(I32, (KEY_TILE, LANES), 0)
    col = lax.broadcasted_iota(I32, (KEY_TILE, LANES), 1)
    w_idx = rows_ref[0:N_IDX_HEADS, :] * (N_IDX_HEADS ** -0.5 * IDX_DIM ** -0.5)

    def key_tile(j):
        return pl.multiple_of(j * KEY_TILE, KEY_TILE)

    def causal(j):
        return (j * KEY_TILE + row) <= (t0 + col)

    def fold8(x, op):
        return op(x.reshape(KEY_TILE // 8, 8, LANES), axis=0)

    def score_body(jc, carry):
        top, bottom = carry
        subs = [jc * (COUNT_TILE // KEY_TILE) + sub for sub in range(COUNT_TILE // KEY_TILE)]
        logits = [_dot(kidx_ref[pl.ds(key_tile(jnp.minimum(j, n_kt - 1)), KEY_TILE), :], wi_scr[...])
                  for j in subs]
        for j, lg in zip(subs, logits):
            score = jnp.zeros((KEY_TILE, LANES), F32)
            for h in range(N_IDX_HEADS):
                score = score + jnp.maximum(lg[:, h * LANES:(h + 1) * LANES], 0.0) * w_idx[h:h + 1, :]
            visible = causal(j)
            top = jnp.maximum(top, fold8(jnp.where(visible, score, -jnp.inf), jnp.max))
            bottom = jnp.minimum(bottom, fold8(jnp.where(visible, score, jnp.inf), jnp.min))
            score_scr[pl.ds(key_tile(j), KEY_TILE), :] = jnp.where(visible, score, -jnp.inf)
        return top, bottom

    top, bottom = lax.fori_loop(0, n_ct, score_body,
                                (jnp.full((8, LANES), -jnp.inf, F32), jnp.full((8, LANES), jnp.inf, F32)))
    top = jnp.max(top, axis=0, keepdims=True)
    bottom = jnp.min(bottom, axis=0, keepdims=True)

    @pl.when(n_ct % 2 == 1)
    def _():
        pad_rows = pl.ds(pl.multiple_of(n_ct * COUNT_TILE, COUNT_TILE), COUNT_TILE)
        score_scr[pad_rows, :] = jnp.full((COUNT_TILE, LANES), -jnp.inf, F32)

    def count_f32(cand, strict):
        def body(j, acc):
            for part in range(SCAN_TILE // COUNT_TILE):
                start = pl.multiple_of(j * SCAN_TILE + part * COUNT_TILE, COUNT_TILE)
                x = score_scr[pl.ds(start, COUNT_TILE), :]
                ind = jnp.where((x > cand) if strict else (x >= cand), 1, 0)
                acc = acc + jnp.sum(ind.reshape(COUNT_TILE // 8, 8, LANES), axis=0)
            return acc
        acc = lax.fori_loop(0, n_st, body, jnp.zeros((8, LANES), I32))
        return jnp.sum(acc, axis=0, keepdims=True)

    def to_pattern(v):
        bits = lax.bitcast_convert_type(v, I32)
        return bits ^ ((bits >> 31) & 0x7FFFFFFF)

    def to_f32(c):
        return lax.bitcast_convert_type(c ^ ((c >> 31) & 0x7FFFFFFF), F32)

    zero_row = jnp.zeros((1, LANES), F32)
    n_visible = t0 + 1 + lax.broadcasted_iota(I32, (1, LANES), 1)
    count0 = count_f32(zero_row, strict=False)
    count_pos = count_f32(zero_row, strict=True)
    nonneg = count0 >= topk
    lo = jnp.where(nonneg, 0, to_pattern(bottom))
    hi = jnp.where(nonneg, to_pattern(top) + 1, 0)
    count_lo = jnp.where(nonneg, count0, n_visible)
    count_hi = jnp.where(nonneg, 0, count0)
    few = n_visible < topk
    lo = jnp.where(few, PATTERN_NEG_FLT_MAX, lo)
    zero_tied = nonneg & (count_pos < topk)
    open_q = jnp.where(few | zero_tied | (count_lo == topk), 0, 1)

    log_topk = math.log(topk)

    def count_error(count):
        return jnp.log(count.astype(F32) + 0.5) - log_topk

    def probes_left(st):
        n_pass, n_open = st[0], st[-1]
        return (n_pass < MAX_SEARCH_PASSES) & (n_open > 0)

    def probe_group(st):
        n_pass = st[0]

        def probe(step, carry):
            lo, hi, count_lo, count_hi, err_lo, err_hi, last_side, open_q = carry
            v_lo, v_hi = to_f32(lo), to_f32(hi)
            frac = err_lo / (err_lo - err_hi)
            frac = jnp.where(count_lo - count_hi <= 4, 0.5, frac)
            guess = to_pattern(v_lo + (v_hi - v_lo) * frac)
            middle = lo + lax.shift_right_logical(hi - lo, 1)
            cand = jnp.where(n_pass + step >= INTERPOLATED_PASSES, middle, guess)
            cand = jnp.minimum(jnp.maximum(cand, lo + 1), hi - 1)
            count = count_f32(to_f32(cand), strict=False)
            raise_lo = (open_q == 1) & (count >= topk)
            lower_hi = (open_q == 1) & (count < topk)
            err = count_error(count)
            err_hi = jnp.where(raise_lo & (last_side == 1), err_hi * 0.5, err_hi)
            err_lo = jnp.where(lower_hi & (last_side == -1), err_lo * 0.5, err_lo)
            err_lo = jnp.where(raise_lo, err, err_lo)
            err_hi = jnp.where(lower_hi, err, err_hi)
            lo = jnp.where(raise_lo, cand, lo)
            count_lo = jnp.where(raise_lo, count, count_lo)
            hi = jnp.where(lower_hi, cand, hi)
            count_hi = jnp.where(lower_hi, count, count_hi)
            last_side = jnp.where(raise_lo, 1, jnp.where(lower_hi, -1, last_side))
            closed = (count_lo == topk) | (hi - lo == 1)
            return lo, hi, count_lo, count_hi, err_lo, err_hi, last_side, jnp.where(closed, 0, open_q)

        carry = lax.fori_loop(0, PASSES_PER_CHECK, probe, st[1:-1])
        return (n_pass + PASSES_PER_CHECK,) + carry + (jnp.sum(carry[-1]),)

    state = (jnp.int32(0), lo, hi, count_lo, count_hi, count_error(count_lo), count_error(count_hi),
             jnp.zeros((1, LANES), I32), open_q, jnp.sum(open_q))
    state = lax.while_loop(probes_left, probe_group, state)
    tau = to_f32(state[1])
    need = (topk - count_f32(tau, strict=True)).astype(F32)

    def mask_body(jc, tie_carry):
        rows = [pl.ds(key_tile(jc * (SCAN_TILE // KEY_TILE) + sub), KEY_TILE)
                for sub in range(SCAN_TILE // KEY_TILE)]
        xs = [score_scr[r, :] for r in rows]
        ties = [x == tau for x in xs]
        tie_cols = jnp.concatenate([jnp.where(tie, 1.0, 0.0).astype(BF16) for tie in ties], axis=1)
        all_ranks = _dot(tri_ref[...], tie_cols)
        ranks = [all_ranks[:, n * LANES:(n + 1) * LANES] for n in range(len(ties))]
        for r, x, tie, rank in zip(rows, xs, ties, ranks):
            rank = rank + tie_carry
            tie_carry = rank[KEY_TILE - 1:KEY_TILE, :]
            take = (tie & (rank <= need)) | (x > tau)
            score_scr[r, :] = jnp.where(take, 0.0, MASKED_LOGIT)
        return tie_carry

    lax.fori_loop(0, n_st, mask_body, jnp.zeros((1, LANES), F32))

    m_scr[...] = jnp.full(m_scr.shape, MASKED_LOGIT, F32)
    l_scr[...] = jnp.zeros(l_scr.shape, F32)
    acc_scr[...] = jnp.zeros(acc_scr.shape, F32)
    ones_rows = jnp.ones((16, KEY_TILE), BF16)

    def qk_pair(slot, j, p):
        s_scr[slot, :, p * pair_w:(p + 1) * pair_w] = _dot(
            k_ref[pl.ds(key_tile(j), KEY_TILE), p * LANES:(p + 1) * LANES], wq_scr[p])

    def softmax_pair(slot, j, p, near):
        mask_add = score_scr[pl.ds(key_tile(j), KEY_TILE), :]
        alphas = []
        for h in (2 * p, 2 * p + 1):
            cols = slice(h * LANES, (h + 1) * LANES)
            logits = s_scr[slot, :, cols] + mask_add
            if near:
                kind_top = jnp.clip(2 * j - i + 2, 0, 2)
                kind_bot = jnp.clip(2 * j - i + 3, 0, 2)
                logits = logits + jnp.concatenate(
                    [bias_ref[kind_top, :, cols], bias_ref[kind_bot, :, cols]], axis=0)
            m_old = m_scr[h:h + 1, :]
            m_new = jnp.maximum(m_old, jnp.max(logits, axis=0, keepdims=True))
            m_scr[h:h + 1, :] = m_new
            p_scr[slot, :, cols] = jnp.exp2(logits - m_new).astype(BF16)
            alphas.append(jnp.exp2(m_old - m_new))
        return alphas

    def pv_pair(slot, j, p, alphas):
        lhs = jnp.concatenate([vt_ref[p * 2 * hd:(p + 1) * 2 * hd, pl.ds(key_tile(j), KEY_TILE)], ones_rows],
                              axis=0)
        out = _dot(lhs, p_scr[slot, :, p * pair_w:(p + 1) * pair_w])
        for half in range(2):
            h = 2 * p + half
            rows_h = slice(h * hd, (h + 1) * hd)
            q_cols = slice(half * LANES, (half + 1) * LANES)
            acc_scr[rows_h, :] = acc_scr[rows_h, :] * alphas[half] + out[half * hd:(half + 1) * hd, q_cols]
            l_scr[h:h + 1, :] = l_scr[h:h + 1, :] * alphas[half] + out[2 * hd:2 * hd + 1, q_cols]

    def pending_alphas(p):
        return [alpha_scr[h:h + 1, :] for h in (2 * p, 2 * p + 1)]

    def clear_pending():
        p_scr[1] = jnp.zeros(p_scr.shape[1:], BF16)
        alpha_scr[...] = jnp.ones(alpha_scr.shape, F32)

    def pair_step(ja, j_pending, j_next, near):
        alphas_a = []
        for p in range(n_pairs):
            pv_pair(1, j_pending, p, pending_alphas(p))
            qk_pair(1, ja + 1, p)
            alphas_a.append(softmax_pair(0, ja, p, near))
        for p in range(n_pairs):
            pv_pair(0, ja, p, alphas_a[p])
            qk_pair(0, j_next, p)
            alphas_b = softmax_pair(1, ja + 1, p, near)
            for half in range(2):
                alpha_scr[2 * p + half:2 * p + half + 1, :] = alphas_b[half]

    def single_step(ja, j_pending, near):
        alphas_a = []
        for p in range(n_pairs):
            pv_pair(1, j_pending, p, pending_alphas(p))
            alphas_a.append(softmax_pair(0, ja, p, near))
        for p in range(n_pairs):
            pv_pair(0, ja, p, alphas_a[p])
        clear_pending()

    n_far = 2 * (jnp.maximum(n_kt - 2, 0) // 2)
    n_near = n_kt - n_far
    last = n_kt - 1
    clear_pending()
    for p in range(n_pairs):
        qk_pair(0, 0, p)

    def far_body(jp, carry):
        pair_step(2 * jp, jnp.maximum(2 * jp - 1, 0), 2 * jp + 2, near=False)
        return carry

    lax.fori_loop(0, n_far // 2, far_body, 0)

    @pl.when(n_near >= 2)
    def _():
        pair_step(n_far, jnp.maximum(n_far - 1, 0), jnp.minimum(n_far + 2, last), near=True)

    @pl.when(n_near % 2 == 1)
    def _():
        single_step(last, jnp.where(n_near == 3, n_far + 1, jnp.maximum(n_far - 1, 0)), near=True)

    for p in range(n_pairs):
        pv_pair(1, last, p, pending_alphas(p))

    for h in range(N_ATTN_HEADS):
        rows_h = slice(h * hd, (h + 1) * hd)
        acc_scr[rows_h, :] = acc_scr[rows_h, :] / l_scr[h:h + 1, :]
    o_ref[...] = acc_scr[...].T


def _dsa_kernel(*refs, topk, t_valid):
    o_ref = refs[8]
    i = pl.program_id(1)
    is_real = i * LANES < t_valid

    @pl.when(is_real)
    def _():
        _dsa_block(i, *refs, topk=topk)

    @pl.when(jnp.logical_not(is_real))
    def _():
        o_ref[...] = jnp.zeros(o_ref.shape, F32)


def _dsa_attention(qit, rows, qt, kidx, k, vt, bias, tri, batch, tp, t_valid, topk):
    n_rows = k.shape[0]
    nqb = tp // LANES
    d_idx = N_IDX_HEADS * IDX_DIM
    n_pairs = N_ATTN_HEADS // 2
    key_rows = -(-tp // SCAN_TILE) * SCAN_TILE
    q_cols = lambda height: pl.BlockSpec((None, height, LANES), lambda b, i: (b, 0, i))
    return pl.pallas_call(
        functools.partial(_dsa_kernel, topk=topk, t_valid=t_valid),
        out_shape=jax.ShapeDtypeStruct((n_rows, D_ATTN), F32),
        grid=(batch, nqb),
        in_specs=[q_cols(d_idx),
                  pl.BlockSpec((16, LANES), lambda b, i: (0, b * nqb + i)),
                  q_cols(D_ATTN),
                  pl.BlockSpec((tp, LANES), lambda b, i: (b, 0)),
                  pl.BlockSpec((tp, D_ATTN), lambda b, i: (b, 0)),
                  pl.BlockSpec((None, D_ATTN, tp), lambda b, i: (b, 0, 0)),
                  pl.BlockSpec((3, LANES, N_ATTN_HEADS * LANES), lambda b, i: (0, 0, 0)),
                  pl.BlockSpec((KEY_TILE, KEY_TILE), lambda b, i: (0, 0))],
        out_specs=pl.BlockSpec((LANES, D_ATTN), lambda b, i: (b * nqb + i, 0)),
        scratch_shapes=[pltpu.VMEM((key_rows, LANES), F32),
                        pltpu.VMEM((LANES, N_IDX_HEADS * LANES), BF16),
                        pltpu.VMEM((n_pairs, LANES, 2 * LANES), BF16),
                        pltpu.VMEM((2, KEY_TILE, N_ATTN_HEADS * LANES), F32),
                        pltpu.VMEM((2, KEY_TILE, N_ATTN_HEADS * LANES), BF16),
                        pltpu.VMEM((N_ATTN_HEADS, LANES), F32),
                        pltpu.VMEM((N_ATTN_HEADS, LANES), F32),
                        pltpu.VMEM((D_ATTN, LANES), F32),
                        pltpu.VMEM((N_ATTN_HEADS, LANES), F32)],
        compiler_params=_params("parallel", "parallel"),
        name="dsa_attention",
    )(qit, rows, qt, kidx, k, vt, bias, tri)


def _gdn_prep_kernel(x_ref, halo_ref, cw_ref, rows_ref, alog_ref, dtb_ref, q_ref, k_ref, v_ref, bg_ref, buf):
    first = pl.program_id(1) == 0
    rb = x_ref.shape[0]
    buf[0:HALO_ROWS, :] = jnp.where(first, 0.0, halo_ref[...])
    buf[HALO_ROWS:, :] = x_ref[...]
    acc = jnp.zeros((rb, 3 * D_GDN), F32)
    for tap in range(CONV_WIDTH):
        start = HALO_ROWS - (CONV_WIDTH - 1) + tap
        acc = acc + cw_ref[tap:tap + 1, :] * buf[start:start + rb, :]
    y = _silu(acc)
    for h in range(N_GDN_HEADS):
        cols = slice(h * GDN_HEAD_DIM, (h + 1) * GDN_HEAD_DIM)
        qh = y[:, cols]
        kh = y[:, D_GDN + h * GDN_HEAD_DIM:D_GDN + (h + 1) * GDN_HEAD_DIM]
        q_ref[:, cols] = (qh * lax.rsqrt(jnp.sum(qh * qh, axis=-1, keepdims=True) + EPS)
                          * (GDN_HEAD_DIM ** -0.5))
        k_ref[:, cols] = kh * lax.rsqrt(jnp.sum(kh * kh, axis=-1, keepdims=True) + EPS)
    v_ref[...] = y[:, 2 * D_GDN:]
    rows = rows_ref[...]
    beta = _sigmoid(rows[4:8, :])
    a = rows[8:12, :] + dtb_ref[...]
    softplus = jnp.maximum(a, 0.0) + jnp.log1p(jnp.exp(-jnp.abs(a)))
    bg_ref[0:4, :] = beta
    bg_ref[4:8, :] = -jnp.exp(alog_ref[...]) * softplus


def _gdn_prep(proj, rows, lw, batch, tp, rb):
    n_rows = proj.shape[0]
    nb = tp // rb
    halo_per_block = rb // HALO_ROWS
    row_spec = pl.BlockSpec((rb, D_GDN), lambda b, i: (b * nb + i, 0))
    return pl.pallas_call(
        _gdn_prep_kernel,
        out_shape=(jax.ShapeDtypeStruct((n_rows, D_GDN), F32),) * 3
        + (jax.ShapeDtypeStruct((8, n_rows), F32),),
        grid=(batch, nb),
        in_specs=[pl.BlockSpec((rb, 3 * D_GDN), lambda b, i: (b * nb + i, 0)),
                  pl.BlockSpec((HALO_ROWS, 3 * D_GDN),
                               lambda b, i: (jnp.maximum((b * nb + i) * halo_per_block - 1, 0), 0)),
                  pl.BlockSpec((CONV_WIDTH, 3 * D_GDN), lambda b, i: (0, 0)),
                  pl.BlockSpec((16, rb), lambda b, i: (0, b * nb + i)),
                  pl.BlockSpec((N_GDN_HEADS, 1), lambda b, i: (0, 0)),
                  pl.BlockSpec((N_GDN_HEADS, 1), lambda b, i: (0, 0))],
        out_specs=(row_spec, row_spec, row_spec,
                   pl.BlockSpec((8, rb), lambda b, i: (0, b * nb + i))),
        scratch_shapes=[pltpu.VMEM((HALO_ROWS + rb, 3 * D_GDN), F32)],
        compiler_params=_params("parallel", "parallel"),
        name="gdn_prep",
    )(proj, proj, lw["conv_w"], rows, lw["a_log"], lw["dt_bias"])


def _gdn_chunk_kernel(q_ref, k_ref, v_ref, bg_ref, m_ref, n_ref, p_ref, r_ref, cd_ref):
    c = LANES
    n_chunks = q_ref.shape[0] // c
    items = [(ch, h) for ch in range(n_chunks) for h in range(N_GDN_HEADS)]
    idx = range(len(items))
    row = lax.broadcasted_iota(I32, (c, c), 0)
    col = lax.broadcasted_iota(I32, (c, c), 1)
    tri = row >= col
    strict = row > col
    eye = jnp.where(row == col, 1.0, 0.0)
    lane8 = lax.broadcasted_iota(I32, (8, c), 1)
    gates, decays = [], []
    for ch in range(n_chunks):
        bg = bg_ref[:, ch * c:(ch + 1) * c]
        dec = bg
        shift = 1
        while shift < c:
            dec = dec + jnp.where(lane8 >= shift, pltpu.roll(dec, shift, 1), 0.0)
            shift *= 2
        gates.append(bg)
        decays.append(dec)

    def tokens(ref, n):
        ch, h = items[n]
        return ref[ch * c:(ch + 1) * c, h * GDN_HEAD_DIM:(h + 1) * GDN_HEAD_DIM]

    d_row = [jnp.broadcast_to(decays[ch][4 + h:5 + h, :], (c, c)) for ch, h in items]
    d_col = [d.T for d in d_row]
    beta_col = [jnp.broadcast_to(gates[ch][h:h + 1, :], (c, c)).T for ch, h in items]
    d_last = [d[:, c - 1:c] for d in d_row]
    gamma = [jnp.exp(jnp.where(tri, d_col[n] - d_row[n], MASKED_LOGIT)) for n in idx]
    exp_d = [jnp.exp(d_col[n]) for n in idx]
    k16 = [tokens(k_ref, n).astype(BF16) for n in idx]
    kb = [tokens(k_ref, n) * beta_col[n] for n in idx]
    nil = [jnp.where(strict, _dot_nt(kb[n].astype(BF16), k16[n]) * gamma[n], 0.0) for n in idx]
    nil_parts = [_split_bf16(x) for x in nil]
    inv = [eye - x for x in nil]
    power = [_dot_split(x, x) for x in nil_parts]
    steps = int(math.log2(c)) - 1
    for it in range(steps):
        power_parts = [_split_bf16(x) for x in power]
        inv = [inv[n] + _dot_split(_split_bf16(inv[n]), power_parts[n]) for n in idx]
        if it + 1 < steps:
            power = [_dot_split(x, x) for x in power_parts]
    rhs = [jnp.concatenate([kb[n] * exp_d[n], tokens(v_ref, n) * beta_col[n]], axis=1) for n in idx]
    wu = [_dot_split(_split_bf16(inv[n]), _split_bf16(rhs[n])).astype(BF16) for n in idx]
    aqk = [jnp.where(tri, _dot_nt(tokens(q_ref, n).astype(BF16), k16[n]) * gamma[n], 0.0).astype(BF16)
           for n in idx]
    kd_t = [(tokens(k_ref, n) * jnp.exp(d_last[n] - d_col[n])).T.astype(BF16) for n in idx]
    state_wu = [_dot(kd_t[n], wu[n]) for n in idx]
    out_wu = [_dot(aqk[n], wu[n]) for n in idx]
    for n, (ch, h) in enumerate(items):
        m_ref[ch, h] = (-state_wu[n][:, :c]).astype(BF16)
        n_ref[ch, h] = state_wu[n][:, c:]
        p_ref[ch, h] = (tokens(q_ref, n) * exp_d[n] - out_wu[n][:, :c]).astype(BF16)
        r_ref[ch, h] = out_wu[n][:, c:]
        cd_ref[ch, h:h + 1, :] = jnp.exp(d_last[n][0:1, :] + jnp.zeros((1, c), F32))


def _gdn_chunks(qn, kn, vv, bg):
    n_rows = qn.shape[0]
    nc = n_rows // LANES
    per_step = GDN_CHUNKS_PER_STEP
    tok = pl.BlockSpec((per_step * LANES, D_GDN), lambda c: (c, 0))
    mat = pl.BlockSpec((per_step, N_GDN_HEADS, LANES, LANES), lambda c: (c, 0, 0, 0))
    mat_shape = lambda dt: jax.ShapeDtypeStruct((nc, N_GDN_HEADS, LANES, LANES), dt)
    return pl.pallas_call(
        _gdn_chunk_kernel,
        out_shape=(mat_shape(BF16), mat_shape(F32), mat_shape(BF16), mat_shape(F32),
                   jax.ShapeDtypeStruct((nc, N_GDN_HEADS, LANES), F32)),
        grid=(nc // per_step,),
        in_specs=[tok, tok, tok, pl.BlockSpec((8, per_step * LANES), lambda c: (0, c))],
        out_specs=(mat, mat, mat, mat, pl.BlockSpec((per_step, N_GDN_HEADS, LANES), lambda c: (c, 0, 0))),
        compiler_params=_params("parallel"),
        name="gdn_chunks",
    )(qn, kn, vv, bg)


def _gdn_scan_kernel(m_ref, n_ref, p_ref, r_ref, cd_ref, o_ref, s_scr, *, batch):
    @pl.when(pl.program_id(0) == 0)
    def _():
        s_scr[...] = jnp.zeros(s_scr.shape, F32)

    for b in range(batch):
        for h in range(N_GDN_HEADS):
            s = s_scr[b, h]
            s16 = s.astype(BF16)
            o_ref[b, :, h * GDN_HEAD_DIM:(h + 1) * GDN_HEAD_DIM] = _dot(p_ref[b, h], s16) + r_ref[b, h]
            s_scr[b, h] = s * cd_ref[b, h:h + 1, :] + _dot(m_ref[b, h], s16) + n_ref[b, h]


def _gdn_scan(m_mat, n_mat, p_mat, r_mat, cd, batch, tp):
    nc = tp // LANES
    shape5 = lambda a: a.reshape(batch, nc, N_GDN_HEADS, LANES, LANES)
    mat = pl.BlockSpec((batch, None, N_GDN_HEADS, LANES, LANES), lambda c: (0, c, 0, 0, 0))
    return pl.pallas_call(
        functools.partial(_gdn_scan_kernel, batch=batch),
        out_shape=jax.ShapeDtypeStruct((batch, tp, D_GDN), F32),
        grid=(nc,),
        in_specs=[mat, mat, mat, mat,
                  pl.BlockSpec((batch, None, N_GDN_HEADS, LANES), lambda c: (0, c, 0, 0))],
        out_specs=pl.BlockSpec((batch, LANES, D_GDN), lambda c: (0, c, 0)),
        scratch_shapes=[pltpu.VMEM((batch, N_GDN_HEADS, LANES, LANES), F32)],
        compiler_params=_params("arbitrary"),
        name="gdn_scan",
    )(shape5(m_mat), shape5(n_mat), shape5(p_mat), shape5(r_mat), cd.reshape(batch, nc, N_GDN_HEADS, LANES))


def _out_proj_kernel(h_ref, oa_ref, za_ref, og_ref, zg_ref, gain_ref, w_ref, out_ref):
    attn = (oa_ref[...] * _silu(za_ref[...])).astype(BF16)
    y = _dot(attn, w_ref[0:D_ATTN, :])
    og, zg = og_ref[...], zg_ref[...]
    for h in range(N_GDN_HEADS):
        cols = slice(h * GDN_HEAD_DIM, (h + 1) * GDN_HEAD_DIM)
        gated = (_rms_rows(og[:, cols], gain_ref[...]) * _silu(zg[:, cols])).astype(BF16)
        y = y + _dot(gated, w_ref[D_ATTN + h * GDN_HEAD_DIM:D_ATTN + (h + 1) * GDN_HEAD_DIM, :])
    out_ref[...] = h_ref[...] + y


def _out_proj(h2, o_attn, proj, o_gdn, gain, w_out, rb):
    n_rows = h2.shape[0]
    blk = lambda width, col: pl.BlockSpec((rb, width), lambda i: (i, col))
    return pl.pallas_call(
        _out_proj_kernel,
        out_shape=jax.ShapeDtypeStruct((n_rows, D_MODEL), F32),
        grid=(n_rows // rb,),
        in_specs=[blk(D_MODEL, 0), blk(D_ATTN, 0), blk(D_ATTN, COL_Z_ATTN), blk(D_GDN, 0),
                  blk(D_GDN, COL_Z_GDN),
                  pl.BlockSpec((1, GDN_HEAD_DIM), lambda i: (0, 0)),
                  pl.BlockSpec((D_MODEL, D_MODEL), lambda i: (0, 0))],
        out_specs=blk(D_MODEL, 0),
        compiler_params=_params("parallel"),
        name="out_proj",
    )(h2, o_attn, proj, o_gdn, proj, gain, w_out)


def _pack_layer(norm_gain, w_in, cq_gain, ckv_gain, w_uq, w_ukv, w_q_idx, q_gain, k_gain, conv_w, a_log,
                dt_bias, gdn_gain, w_out):
    o = 0
    parts = {}
    for name, size in (("c_q", Q_RANK), ("c_kv", KV_RANK), ("k_idx", IDX_DIM), ("w_idx", N_IDX_HEADS),
                       ("z_attn", D_ATTN), ("qkv_g", 3 * D_GDN), ("z_g", D_GDN), ("b", N_GDN_HEADS),
                       ("a", N_GDN_HEADS)):
        parts[name] = w_in[:, o:o + size]
        o += size
    w_packed = jnp.concatenate([parts["qkv_g"], parts["c_q"], parts["c_kv"], parts["k_idx"], parts["k_idx"],
                                parts["z_attn"], parts["z_g"]], axis=1).astype(BF16)
    w_rows = jnp.concatenate([parts["w_idx"], parts["b"], parts["a"],
                              jnp.zeros((D_MODEL, 4), w_in.dtype)], axis=1).T.astype(BF16)
    return dict(
        gain=norm_gain[None, :], w_packed=w_packed, w_rows=w_rows,
        g_cq=cq_gain[None, :], g_ckv=ckv_gain[None, :],
        w_uqt=w_uq.T.astype(BF16), w_qit=w_q_idx.T.astype(BF16),
        w_ukt=w_ukv[:, :D_ATTN].T.astype(BF16), w_uvt=w_ukv[:, D_ATTN:].T.astype(BF16),
        g_q_col=q_gain[:, None], g_k_col=k_gain[:, None],
        conv_w=conv_w, a_log=a_log[:, None], dt_bias=dt_bias[:, None],
        gdn_gain=gdn_gain[None, :], w_out=w_out.astype(BF16))


def _layer(h2, lw, bias, tri, batch, tp, t_valid, topk):
    rb = _row_block(tp)
    proj, rows = _in_proj(h2, lw["gain"], lw["w_packed"], lw["w_rows"], rb)
    qt, k, vt, qit, kidx = _dsa_prep(proj, lw, batch, tp, rb)
    o_attn = _dsa_attention(qit, rows, qt, kidx, k, vt, bias, tri, batch, tp, t_valid, topk)
    qn, kn, vv, bg = _gdn_prep(proj, rows, lw, batch, tp, rb)
    m_mat, n_mat, p_mat, r_mat, cd = _gdn_chunks(qn, kn, vv, bg)
    o_gdn = _gdn_scan(m_mat, n_mat, p_mat, r_mat, cd, batch, tp).reshape(batch * tp, D_GDN)
    return _out_proj(h2, o_attn, proj, o_gdn, lw["gdn_gain"], lw["w_out"], rb)


def _forward(x, meta_tokens, rel_bias_table, layer_weights, topk):
    batch, seq, _ = x.shape
    t = seq + N_META
    tp = -(-t // KEY_TILE) * KEY_TILE
    meta = jnp.broadcast_to(meta_tokens[None].astype(x.dtype), (batch, N_META, D_MODEL))
    h = jnp.concatenate([meta, x, jnp.zeros((batch, tp - t, D_MODEL), x.dtype)], axis=1)
    h2 = h.reshape(batch * tp, D_MODEL)
    bias = _bias_tiles(rel_bias_table)
    tri = jnp.tril(jnp.ones((KEY_TILE, KEY_TILE), BF16))
    for lw in layer_weights:
        h2 = _layer(h2, lw, bias, tri, batch, tp, t, topk)
    return h2.reshape(batch, tp, D_MODEL)[:, N_META:t]


def kernel(x, meta_tokens, rel_bias_table, norm_gain, w_in, cq_norm_gain, ckv_norm_gain, w_uq, w_ukv, w_q_idx,
           q_norm_gain, k_norm_gain, conv_w, a_log, dt_bias, gdn_norm_gain, w_out):
    depth = norm_gain.shape[0]
    topk = min(TOPK_MAX, x.shape[1] // 4)
    layers = [_pack_layer(norm_gain[l], w_in[l], cq_norm_gain[l], ckv_norm_gain[l], w_uq[l], w_ukv[l],
                          w_q_idx[l], q_norm_gain[l], k_norm_gain[l], conv_w[l], a_log[l], dt_bias[l],
                          gdn_norm_gain[l], w_out[l]) for l in range(depth)]
    return _forward(x, meta_tokens, rel_bias_table, layers, topk)
```

```python
import functools
import math

import jax
import jax.numpy as jnp
from jax import lax
from jax.experimental import pallas as pl
from jax.experimental.pallas import tpu as pltpu

F32 = jnp.float32
BF16 = jnp.bfloat16
I32 = jnp.int32

D_MODEL = 1024
N_META = 16
EPS = 1e-6
N_ATTN_HEADS = 8
ATTN_HEAD_DIM = 64
D_ATTN = N_ATTN_HEADS * ATTN_HEAD_DIM
Q_RANK = 256
KV_RANK = 128
N_IDX_HEADS = 4
IDX_DIM = 64
TOPK_MAX = 256
N_REL_BUCKETS = 32
REL_MAX_DIST = 128
N_GDN_HEADS = 4
GDN_HEAD_DIM = 128
D_GDN = N_GDN_HEADS * GDN_HEAD_DIM
CONV_WIDTH = 4

LANES = 128
KEY_TILE = 256
COUNT_TILE = 512
SCAN_TILE = 1024
PASSES_PER_CHECK = 2
INTERPOLATED_PASSES = 24
MAX_SEARCH_PASSES = INTERPOLATED_PASSES + 32
GDN_CHUNKS_PER_STEP = 2
ROW_TILES = (5, 4, 3, 2, 1)
HALO_ROWS = 8
D_PACKED = 3 * D_GDN + 512 + D_ATTN + D_GDN
COL_SMALL = 3
COL_Z_ATTN = 4
COL_Z_GDN = 5
MASKED_LOGIT = -1e30
LOG2_E = math.log2(math.e)
KEY_MIN = -2 ** 31
PATTERN_NEG_FLT_MAX = KEY_MIN + (1 << 23)
VMEM_LIMIT = 56 * 1024 * 1024

NT_DIMS = (((1,), (1,)), ((), ()))


def _dot(a, b):
    return jnp.dot(a, b, preferred_element_type=F32)


def _dot_nt(a, b):
    return lax.dot_general(a, b, NT_DIMS, preferred_element_type=F32)


def _split_bf16(x):
    hi = x.astype(BF16)
    return hi, (x - hi.astype(F32)).astype(BF16)


def _dot_split(a_parts, b_parts):
    (a_hi, a_lo), (b_hi, b_lo) = a_parts, b_parts
    return _dot(a_hi, b_hi) + (_dot(a_hi, b_lo) + _dot(a_lo, b_hi))


def _sigmoid(x):
    return 1.0 / (1.0 + jnp.exp(-x))


def _silu(x):
    return x * _sigmoid(x)


def _row_block(tp):
    tiles = tp // LANES
    return LANES * next(d for d in ROW_TILES if tiles % d == 0)


def _params(*sem):
    return pltpu.CompilerParams(dimension_semantics=sem, vmem_limit_bytes=VMEM_LIMIT)


def _bias_kernel(table_ref, out_ref):
    row = lax.broadcasted_iota(I32, (LANES, LANES), 0)
    col = lax.broadcasted_iota(I32, (LANES, LANES), 1)
    max_exact = N_REL_BUCKETS // 2
    for kind in range(3):
        dist = col - row + (2 - kind) * LANES
        n = jnp.maximum(dist, 0)
        nf = jnp.maximum(n, 1).astype(F32)
        large = max_exact + (jnp.log(nf / max_exact) / math.log(REL_MAX_DIST / max_exact)
                             * (N_REL_BUCKETS - max_exact)).astype(I32)
        large = jnp.minimum(large, N_REL_BUCKETS - 1)
        bucket = jnp.where(n < max_exact, n, large)
        for h in range(N_ATTN_HEADS):
            tile = jnp.zeros((LANES, LANES), F32)
            for b in range(N_REL_BUCKETS):
                tile = jnp.where(bucket == b, table_ref[b, h], tile)
            far = table_ref[N_REL_BUCKETS - 1, h]
            out_ref[kind, :, h * LANES:(h + 1) * LANES] = (tile - far) * LOG2_E


def _bias_tiles(rel_table):
    return pl.pallas_call(
        _bias_kernel,
        out_shape=jax.ShapeDtypeStruct((3, LANES, N_ATTN_HEADS * LANES), F32),
        in_specs=[pl.BlockSpec(memory_space=pltpu.SMEM)],
        out_specs=pl.BlockSpec(memory_space=pltpu.VMEM),
        name="rel_bias_tiles",
    )(rel_table)


def _in_proj_kernel(h_ref, gain_ref, w_ref, wrows_ref, proj_ref, rows_ref):
    x = h_ref[...]
    y = x * lax.rsqrt(jnp.mean(x * x, axis=-1, keepdims=True) + EPS)
    hn = (y * gain_ref[...]).astype(BF16)
    proj_ref[...] = _dot(hn, w_ref[...])
    rows_ref[...] = _dot_nt(wrows_ref[...], hn)


def _in_proj(h2, gain, w_packed, w_rows, rb):
    n_rows = h2.shape[0]
    grid = (n_rows // rb,)
    return pl.pallas_call(
        _in_proj_kernel,
        out_shape=(jax.ShapeDtypeStruct((n_rows, D_PACKED), F32),
                   jax.ShapeDtypeStruct((16, n_rows), F32)),
        grid=grid,
        in_specs=[pl.BlockSpec((rb, D_MODEL), lambda i: (i, 0)),
                  pl.BlockSpec((1, D_MODEL), lambda i: (0, 0)),
                  pl.BlockSpec((D_MODEL, D_PACKED), lambda i: (0, 0)),
                  pl.BlockSpec((16, D_MODEL), lambda i: (0, 0))],
        out_specs=(pl.BlockSpec((rb, D_PACKED), lambda i: (i, 0)),
                   pl.BlockSpec((16, rb), lambda i: (0, i))),
        compiler_params=_params("parallel"),
        name="in_proj",
    )(h2, gain, w_packed, w_rows)


def _rms_rows(x, gain):
    return x * lax.rsqrt(jnp.mean(x * x, axis=-1, keepdims=True) + EPS) * gain


def _dsa_prep_kernel(sm_ref, gcq_ref, gckv_ref, wuqt_ref, wqit_ref, wukt_ref, wuvt_ref, gq_ref, gk_ref,
                     qt_ref, k_ref, vt_ref, qit_ref, kidx_ref):
    sm = sm_ref[...]
    rb = sm.shape[0]
    cq = _rms_rows(sm[:, :Q_RANK], gcq_ref[...]).astype(BF16)
    ckv = _rms_rows(sm[:, Q_RANK:Q_RANK + KV_RANK], gckv_ref[...]).astype(BF16)
    kidx_ref[...] = sm[:, Q_RANK + KV_RANK:].astype(BF16)
    q3 = _dot_nt(wuqt_ref[...], cq).reshape(N_ATTN_HEADS, ATTN_HEAD_DIM, rb)
    q3 = q3 * lax.rsqrt(jnp.mean(q3 * q3, axis=1, keepdims=True) + EPS) * gq_ref[...][None]
    qt_ref[...] = (q3 * (ATTN_HEAD_DIM ** -0.5 * LOG2_E)).reshape(D_ATTN, rb).astype(BF16)
    k3 = _dot_nt(wukt_ref[...], ckv).reshape(N_ATTN_HEADS, ATTN_HEAD_DIM, rb)
    k3 = k3 * lax.rsqrt(jnp.mean(k3 * k3, axis=1, keepdims=True) + EPS) * gk_ref[...][None]
    k_ref[...] = k3.reshape(D_ATTN, rb).T.astype(BF16)
    vt_ref[...] = _dot_nt(wuvt_ref[...], ckv).astype(BF16)
    qit_ref[...] = _dot_nt(wqit_ref[...], cq).astype(BF16)


def _dsa_prep(proj, lw, batch, tp, rb):
    n_rows = proj.shape[0]
    nb = tp // rb
    d_idx = N_IDX_HEADS * IDX_DIM
    const = lambda shape: pl.BlockSpec(shape, lambda b, i: (0, 0))
    row_spec = lambda width: pl.BlockSpec((rb, width), lambda b, i: (b * nb + i, 0))
    col_spec = lambda height: pl.BlockSpec((None, height, rb), lambda b, i: (b, 0, i))
    return pl.pallas_call(
        _dsa_prep_kernel,
        out_shape=(jax.ShapeDtypeStruct((batch, D_ATTN, tp), BF16),
                   jax.ShapeDtypeStruct((n_rows, D_ATTN), BF16),
                   jax.ShapeDtypeStruct((batch, D_ATTN, tp), BF16),
                   jax.ShapeDtypeStruct((batch, d_idx, tp), BF16),
                   jax.ShapeDtypeStruct((n_rows, LANES), BF16)),
        grid=(batch, nb),
        in_specs=[pl.BlockSpec((rb, 512), lambda b, i: (b * nb + i, COL_SMALL)),
                  const((1, Q_RANK)), const((1, KV_RANK)),
                  const((D_ATTN, Q_RANK)), const((d_idx, Q_RANK)),
                  const((D_ATTN, KV_RANK)), const((D_ATTN, KV_RANK)),
                  const((ATTN_HEAD_DIM, 1)), const((ATTN_HEAD_DIM, 1))],
        out_specs=(col_spec(D_ATTN), row_spec(D_ATTN), col_spec(D_ATTN), col_spec(d_idx), row_spec(LANES)),
        compiler_params=_params("parallel", "parallel"),
        name="dsa_prep",
    )(proj, lw["g_cq"], lw["g_ckv"], lw["w_uqt"], lw["w_qit"], lw["w_ukt"], lw["w_uvt"], lw["g_q_col"],
      lw["g_k_col"])


def _dsa_block(i, qit_ref, rows_ref, qt_ref, kidx_ref, k_ref, vt_ref, bias_ref, tri_ref, o_ref,
               score_scr, wi_scr, wq_scr, s_scr, p_scr, m_scr, l_scr, acc_scr, alpha_scr, mask_scr, tie_scr, *, topk):
    t0 = i * LANES
    n_kt = i // 2 + 1
    n_ct = (n_kt + 1) // 2
    n_st = (n_ct + 1) // 2
    hd = ATTN_HEAD_DIM
    pair_w = 2 * LANES
    n_pairs = N_ATTN_HEADS // 2

    zeros_hd = jnp.zeros((hd, LANES), BF16)
    for h in range(N_IDX_HEADS):
        wi_scr[0:IDX_DIM, h * LANES:(h + 1) * LANES] = qit_ref[h * IDX_DIM:(h + 1) * IDX_DIM, :]
    wi_scr[IDX_DIM:, :] = jnp.zeros((LANES - IDX_DIM, N_IDX_HEADS * LANES), BF16)
    for p in range(n_pairs):
        wq_scr[p, 0:hd, 0:LANES] = qt_ref[2 * p * hd:(2 * p + 1) * hd, :]
        wq_scr[p, 0:hd, LANES:] = zeros_hd
        wq_scr[p, hd:, 0:LANES] = zeros_hd
        wq_scr[p, hd:, LANES:] = qt_ref[(2 * p + 1) * hd:(2 * p + 2) * hd, :]

    row = lax.broadcasted_iota(I32, (KEY_TILE, LANES), 0)
    col = lax.broadcasted_iota(I32, (KEY_TILE, LANES), 1)
    w_idx = rows_ref[0:N_IDX_HEADS, :] * (N_IDX_HEADS ** -0.5 * IDX_DIM ** -0.5)

    def key_tile(j):
        return pl.multiple_of(j * KEY_TILE, KEY_TILE)

    def causal(j):
        return (j * KEY_TILE + row) <= (t0 + col)

    def fold8(x, op):
        return op(x.reshape(KEY_TILE // 8, 8, LANES), axis=0)

    def score_body(jc, carry):
        top, bottom, n_nonneg, n_pos = carry
        subs = [jc * (COUNT_TILE // KEY_TILE) + sub for sub in range(COUNT_TILE // KEY_TILE)]
        logits = [_dot(kidx_ref[pl.ds(key_tile(jnp.minimum(j, n_kt - 1)), KEY_TILE), :], wi_scr[...])
                  for j in subs]
        for j, lg in zip(subs, logits):
            score = jnp.zeros((KEY_TILE, LANES), F32)
            for h in range(N_IDX_HEADS):
                score = score + jnp.maximum(lg[:, h * LANES:(h + 1) * LANES], 0.0) * w_idx[h:h + 1, :]
            visible = causal(j)
            seen = jnp.where(visible, score, -jnp.inf)
            top = jnp.maximum(top, fold8(seen, jnp.max))
            bottom = jnp.minimum(bottom, fold8(jnp.where(visible, score, jnp.inf), jnp.min))
            n_nonneg = n_nonneg + fold8(jnp.where(seen >= 0.0, 1, 0), jnp.sum)
            n_pos = n_pos + fold8(jnp.where(seen > 0.0, 1, 0), jnp.sum)
            score_scr[pl.ds(key_tile(j), KEY_TILE), :] = seen
        return top, bottom, n_nonneg, n_pos

    zeros8 = jnp.zeros((8, LANES), I32)
    top, bottom, n_nonneg, n_pos = lax.fori_loop(
        0, n_ct, score_body,
        (jnp.full((8, LANES), -jnp.inf, F32), jnp.full((8, LANES), jnp.inf, F32), zeros8, zeros8))
    top = jnp.max(top, axis=0, keepdims=True)
    bottom = jnp.min(bottom, axis=0, keepdims=True)
    count0 = jnp.sum(n_nonneg, axis=0, keepdims=True)
    count_pos = jnp.sum(n_pos, axis=0, keepdims=True)

    @pl.when(n_ct % 2 == 1)
    def _():
        pad_rows = pl.ds(pl.multiple_of(n_ct * COUNT_TILE, COUNT_TILE), COUNT_TILE)
        score_scr[pad_rows, :] = jnp.full((COUNT_TILE, LANES), -jnp.inf, F32)

    def count_f32(cand, strict):
        def body(j, acc):
            for part in range(SCAN_TILE // COUNT_TILE):
                start = pl.multiple_of(j * SCAN_TILE + part * COUNT_TILE, COUNT_TILE)
                x = score_scr[pl.ds(start, COUNT_TILE), :]
                ind = jnp.where((x > cand) if strict else (x >= cand), 1, 0)
                acc = acc + jnp.sum(ind.reshape(COUNT_TILE // 8, 8, LANES), axis=0)
            return acc
        acc = lax.fori_loop(0, n_st, body, jnp.zeros((8, LANES), I32))
        return jnp.sum(acc, axis=0, keepdims=True)

    def to_pattern(v):
        bits = lax.bitcast_convert_type(v, I32)
        return bits ^ ((bits >> 31) & 0x7FFFFFFF)

    def to_f32(c):
        return lax.bitcast_convert_type(c ^ ((c >> 31) & 0x7FFFFFFF), F32)

    n_visible = t0 + 1 + lax.broadcasted_iota(I32, (1, LANES), 1)
    nonneg = count0 >= topk
    lo = jnp.where(nonneg, 0, to_pattern(bottom))
    hi = jnp.where(nonneg, to_pattern(top) + 1, 0)
    count_lo = jnp.where(nonneg, count0, n_visible)
    count_hi = jnp.where(nonneg, 0, count0)
    few = n_visible < topk
    lo = jnp.where(few, PATTERN_NEG_FLT_MAX, lo)
    zero_tied = nonneg & (count_pos < topk)
    hi = jnp.where(zero_tied, 1, hi)
    count_hi = jnp.where(zero_tied, count_pos, count_hi)
    open_q = jnp.where(few | zero_tied | (count_lo == topk), 0, 1)

    log_topk = math.log(topk)

    def count_error(count):
        return jnp.log(count.astype(F32) + 0.5) - log_topk

    def probes_left(st):
        n_pass, n_open = st[0], st[-1]
        return (n_pass < MAX_SEARCH_PASSES) & (n_open > 0)

    def probe_group(st):
        n_pass = st[0]

        def probe(step, carry):
            lo, hi, count_lo, count_hi, err_lo, err_hi, last_side, open_q = carry
            v_lo, v_hi = to_f32(lo), to_f32(hi)
            frac = err_lo / (err_lo - err_hi)
            frac = jnp.where(count_lo - count_hi <= 4, 0.5, frac)
            guess = to_pattern(v_lo + (v_hi - v_lo) * frac)
            middle = lo + lax.shift_right_logical(hi - lo, 1)
            cand = jnp.where(n_pass + step >= INTERPOLATED_PASSES, middle, guess)
            cand = jnp.minimum(jnp.maximum(cand, lo + 1), hi - 1)
            count = count_f32(to_f32(cand), strict=False)
            raise_lo = (open_q == 1) & (count >= topk)
            lower_hi = (open_q == 1) & (count < topk)
            err = count_error(count)
            err_hi = jnp.where(raise_lo & (last_side == 1), err_hi * 0.5, err_hi)
            err_lo = jnp.where(lower_hi & (last_side == -1), err_lo * 0.5, err_lo)
            err_lo = jnp.where(raise_lo, err, err_lo)
            err_hi = jnp.where(lower_hi, err, err_hi)
            lo = jnp.where(raise_lo, cand, lo)
            count_lo = jnp.where(raise_lo, count, count_lo)
            hi = jnp.where(lower_hi, cand, hi)
            count_hi = jnp.where(lower_hi, count, count_hi)
            last_side = jnp.where(raise_lo, 1, jnp.where(lower_hi, -1, last_side))
            closed = (count_lo == topk) | (hi - lo == 1)
            return lo, hi, count_lo, count_hi, err_lo, err_hi, last_side, jnp.where(closed, 0, open_q)

        carry = lax.fori_loop(0, PASSES_PER_CHECK, probe, st[1:-1])
        return (n_pass + PASSES_PER_CHECK,) + carry + (jnp.sum(carry[-1]),)

    state = (jnp.int32(0), lo, hi, count_lo, count_hi, count_error(count_lo), count_error(count_hi),
             jnp.zeros((1, LANES), I32), open_q, jnp.sum(open_q))
    state = lax.while_loop(probes_left, probe_group, state)
    lo, count_lo, count_hi = state[1], state[3], state[4]
    tau = to_f32(lo)
    need = jnp.where((count_lo == topk) | few, topk, topk - count_hi).astype(F32)

    m_scr[...] = jnp.full(m_scr.shape, MASKED_LOGIT, F32)
    l_scr[...] = jnp.zeros(l_scr.shape, F32)
    acc_scr[...] = jnp.zeros(acc_scr.shape, F32)
    ones_rows = jnp.ones((16, KEY_TILE), BF16)
    last = n_kt - 1

    def mask_pair(j_first, parity):
        xs = [score_scr[pl.ds(key_tile(jnp.minimum(j_first + slot, last)), KEY_TILE), :] for slot in range(2)]
        ties = [x == tau for x in xs]
        tie_cols = jnp.concatenate([jnp.where(tie, 1.0, 0.0).astype(BF16) for tie in ties], axis=1)
        ranks = _dot(tri_ref[...], tie_cols)
        tie_carry = tie_scr[0:1, :]
        for slot, (x, tie) in enumerate(zip(xs, ties)):
            rank = ranks[:, slot * LANES:(slot + 1) * LANES] + tie_carry
            tie_carry = rank[KEY_TILE - 1:KEY_TILE, :]
            take = (tie & (rank <= need)) | (x > tau)
            mask_scr[parity, slot] = jnp.where(take, 0.0, MASKED_LOGIT)
        tie_scr[0:1, :] = tie_carry

    def qk_pair(slot, j, p):
        s_scr[slot, :, p * pair_w:(p + 1) * pair_w] = _dot(
            k_ref[pl.ds(key_tile(j), KEY_TILE), p * LANES:(p + 1) * LANES], wq_scr[p])

    def softmax_pair(slot, j, p, near, parity):
        mask_add = mask_scr[parity, slot]
        alphas = []
        for h in (2 * p, 2 * p + 1):
            cols = slice(h * LANES, (h + 1) * LANES)
            logits = s_scr[slot, :, cols] + mask_add
            if near:
                kind_top = jnp.clip(2 * j - i + 2, 0, 2)
                kind_bot = jnp.clip(2 * j - i + 3, 0, 2)
                logits = logits + jnp.concatenate(
                    [bias_ref[kind_top, :, cols], bias_ref[kind_bot, :, cols]], axis=0)
            m_old = m_scr[h:h + 1, :]
            m_new = jnp.maximum(m_old, jnp.max(logits, axis=0, keepdims=True))
            m_scr[h:h + 1, :] = m_new
            p_scr[slot, :, cols] = jnp.exp2(logits - m_new).astype(BF16)
            alphas.append(jnp.exp2(m_old - m_new))
        return alphas

    def pv_pair(slot, j, p, alphas):
        lhs = jnp.concatenate([vt_ref[p * 2 * hd:(p + 1) * 2 * hd, pl.ds(key_tile(j), KEY_TILE)], ones_rows],
                              axis=0)
        out = _dot(lhs, p_scr[slot, :, p * pair_w:(p + 1) * pair_w])
        for half in range(2):
            h = 2 * p + half
            rows_h = slice(h * hd, (h + 1) * hd)
            q_cols = slice(half * LANES, (half + 1) * LANES)
            acc_scr[rows_h, :] = acc_scr[rows_h, :] * alphas[half] + out[half * hd:(half + 1) * hd, q_cols]
            l_scr[h:h + 1, :] = l_scr[h:h + 1, :] * alphas[half] + out[2 * hd:2 * hd + 1, q_cols]

    def pending_alphas(p):
        return [alpha_scr[h:h + 1, :] for h in (2 * p, 2 * p + 1)]

    def clear_pending():
        p_scr[1] = jnp.zeros(p_scr.shape[1:], BF16)
        alpha_scr[...] = jnp.ones(alpha_scr.shape, F32)

    def pair_step(ja, j_pending, j_next, near, parity):
        alphas_a = []
        for p in range(n_pairs):
            pv_pair(1, j_pending, p, pending_alphas(p))
            qk_pair(1, ja + 1, p)
            alphas_a.append(softmax_pair(0, ja, p, near, parity))
        for p in range(n_pairs):
            pv_pair(0, ja, p, alphas_a[p])
            qk_pair(0, j_next, p)
            alphas_b = softmax_pair(1, ja + 1, p, near, parity)
            for half in range(2):
                alpha_scr[2 * p + half:2 * p + half + 1, :] = alphas_b[half]
        mask_pair(ja + 2, 1 - parity)

    def single_step(ja, j_pending, near, parity):
        alphas_a = []
        for p in range(n_pairs):
            pv_pair(1, j_pending, p, pending_alphas(p))
            alphas_a.append(softmax_pair(0, ja, p, near, parity))
        for p in range(n_pairs):
            pv_pair(0, ja, p, alphas_a[p])
        clear_pending()

    n_far = 2 * (jnp.maximum(n_kt - 2, 0) // 2)
    n_near = n_kt - n_far
    first_near_parity = (n_far // 2) % 2
    clear_pending()
    tie_scr[...] = jnp.zeros(tie_scr.shape, F32)
    mask_pair(0, 0)
    for p in range(n_pairs):
        qk_pair(0, 0, p)

    def far_body(jp, carry):
        pair_step(2 * jp, jnp.maximum(2 * jp - 1, 0), 2 * jp + 2, near=False, parity=jp % 2)
        return carry

    lax.fori_loop(0, n_far // 2, far_body, 0)

    @pl.when(n_near >= 2)
    def _():
        pair_step(n_far, jnp.maximum(n_far - 1, 0), jnp.minimum(n_far + 2, last), near=True,
                  parity=first_near_parity)

    @pl.when(n_near % 2 == 1)
    def _():
        single_step(last, jnp.where(n_near == 3, n_far + 1, jnp.maximum(n_far - 1, 0)), near=True,
                    parity=jnp.where(n_near == 3, 1 - first_near_parity, first_near_parity))

    for p in range(n_pairs):
        pv_pair(1, last, p, pending_alphas(p))

    for h in range(N_ATTN_HEADS):
        rows_h = slice(h * hd, (h + 1) * hd)
        acc_scr[rows_h, :] = acc_scr[rows_h, :] / l_scr[h:h + 1, :]
    o_ref[...] = acc_scr[...].T


def _dsa_kernel(*refs, topk, t_valid):
    o_ref = refs[8]
    i = pl.program_id(1)
    is_real = i * LANES < t_valid

    @pl.when(is_real)
    def _():
        _dsa_block(i, *refs, topk=topk)

    @pl.when(jnp.logical_not(is_real))
    def _():
        o_ref[...] = jnp.zeros(o_ref.shape, F32)


def _dsa_attention(qit, rows, qt, kidx, k, vt, bias, tri, batch, tp, t_valid, topk):
    n_rows = k.shape[0]
    nqb = tp // LANES
    d_idx = N_IDX_HEADS * IDX_DIM
    n_pairs = N_ATTN_HEADS // 2
    key_rows = -(-tp // SCAN_TILE) * SCAN_TILE
    q_cols = lambda height: pl.BlockSpec((None, height, LANES), lambda b, i: (b, 0, i))
    return pl.pallas_call(
        functools.partial(_dsa_kernel, topk=topk, t_valid=t_valid),
        out_shape=jax.ShapeDtypeStruct((n_rows, D_ATTN), F32),
        grid=(batch, nqb),
        in_specs=[q_cols(d_idx),
                  pl.BlockSpec((16, LANES), lambda b, i: (0, b * nqb + i)),
                  q_cols(D_ATTN),
                  pl.BlockSpec((tp, LANES), lambda b, i: (b, 0)),
                  pl.BlockSpec((tp, D_ATTN), lambda b, i: (b, 0)),
                  pl.BlockSpec((None, D_ATTN, tp), lambda b, i: (b, 0, 0)),
                  pl.BlockSpec((3, LANES, N_ATTN_HEADS * LANES), lambda b, i: (0, 0, 0)),
                  pl.BlockSpec((KEY_TILE, KEY_TILE), lambda b, i: (0, 0))],
        out_specs=pl.BlockSpec((LANES, D_ATTN), lambda b, i: (b * nqb + i, 0)),
        scratch_shapes=[pltpu.VMEM((key_rows, LANES), F32),
                        pltpu.VMEM((LANES, N_IDX_HEADS * LANES), BF16),
                        pltpu.VMEM((n_pairs, LANES, 2 * LANES), BF16),
                        pltpu.VMEM((2, KEY_TILE, N_ATTN_HEADS * LANES), F32),
                        pltpu.VMEM((2, KEY_TILE, N_ATTN_HEADS * LANES), BF16),
                        pltpu.VMEM((N_ATTN_HEADS, LANES), F32),
                        pltpu.VMEM((N_ATTN_HEADS, LANES), F32),
                        pltpu.VMEM((D_ATTN, LANES), F32),
                        pltpu.VMEM((N_ATTN_HEADS, LANES), F32),
                        pltpu.VMEM((2, 2, KEY_TILE, LANES), F32),
                        pltpu.VMEM((8, LANES), F32)],
        compiler_params=_params("parallel", "parallel"),
        name="dsa_attention",
    )(qit, rows, qt, kidx, k, vt, bias, tri)


def _gdn_prep_kernel(x_ref, halo_ref, cw_ref, rows_ref, alog_ref, dtb_ref, q_ref, k_ref, v_ref, bg_ref, buf):
    first = pl.program_id(1) == 0
    rb = x_ref.shape[0]
    buf[0:HALO_ROWS, :] = jnp.where(first, 0.0, halo_ref[...])
    buf[HALO_ROWS:, :] = x_ref[...]
    acc = jnp.zeros((rb, 3 * D_GDN), F32)
    for tap in range(CONV_WIDTH):
        start = HALO_ROWS - (CONV_WIDTH - 1) + tap
        acc = acc + cw_ref[tap:tap + 1, :] * buf[start:start + rb, :]
    y = _silu(acc)
    for h in range(N_GDN_HEADS):
        cols = slice(h * GDN_HEAD_DIM, (h + 1) * GDN_HEAD_DIM)
        qh = y[:, cols]
        kh = y[:, D_GDN + h * GDN_HEAD_DIM:D_GDN + (h + 1) * GDN_HEAD_DIM]
        q_ref[:, cols] = (qh * lax.rsqrt(jnp.sum(qh * qh, axis=-1, keepdims=True) + EPS)
                          * (GDN_HEAD_DIM ** -0.5))
        k_ref[:, cols] = kh * lax.rsqrt(jnp.sum(kh * kh, axis=-1, keepdims=True) + EPS)
    v_ref[...] = y[:, 2 * D_GDN:]
    rows = rows_ref[...]
    beta = _sigmoid(rows[4:8, :])
    a = rows[8:12, :] + dtb_ref[...]
    softplus = jnp.maximum(a, 0.0) + jnp.log1p(jnp.exp(-jnp.abs(a)))
    bg_ref[0:4, :] = beta
    bg_ref[4:8, :] = -jnp.exp(alog_ref[...]) * softplus


def _gdn_prep(proj, rows, lw, batch, tp, rb):
    n_rows = proj.shape[0]
    nb = tp // rb
    halo_per_block = rb // HALO_ROWS
    row_spec = pl.BlockSpec((rb, D_GDN), lambda b, i: (b * nb + i, 0))
    return pl.pallas_call(
        _gdn_prep_kernel,
        out_shape=(jax.ShapeDtypeStruct((n_rows, D_GDN), F32),) * 3
        + (jax.ShapeDtypeStruct((8, n_rows), F32),),
        grid=(batch, nb),
        in_specs=[pl.BlockSpec((rb, 3 * D_GDN), lambda b, i: (b * nb + i, 0)),
                  pl.BlockSpec((HALO_ROWS, 3 * D_GDN),
                               lambda b, i: (jnp.maximum((b * nb + i) * halo_per_block - 1, 0), 0)),
                  pl.BlockSpec((CONV_WIDTH, 3 * D_GDN), lambda b, i: (0, 0)),
                  pl.BlockSpec((16, rb), lambda b, i: (0, b * nb + i)),
                  pl.BlockSpec((N_GDN_HEADS, 1), lambda b, i: (0, 0)),
                  pl.BlockSpec((N_GDN_HEADS, 1), lambda b, i: (0, 0))],
        out_specs=(row_spec, row_spec, row_spec,
                   pl.BlockSpec((8, rb), lambda b, i: (0, b * nb + i))),
        scratch_shapes=[pltpu.VMEM((HALO_ROWS + rb, 3 * D_GDN), F32)],
        compiler_params=_params("parallel", "parallel"),
        name="gdn_prep",
    )(proj, proj, lw["conv_w"], rows, lw["a_log"], lw["dt_bias"])


def _gdn_chunk_kernel(q_ref, k_ref, v_ref, bg_ref, m_ref, n_ref, p_ref, r_ref, cd_ref):
    c = LANES
    n_chunks = q_ref.shape[0] // c
    items = [(ch, h) for ch in range(n_chunks) for h in range(N_GDN_HEADS)]
    idx = range(len(items))
    row = lax.broadcasted_iota(I32, (c, c), 0)
    col = lax.broadcasted_iota(I32, (c, c), 1)
    tri = row >= col
    strict = row > col
    eye = jnp.where(row == col, 1.0, 0.0)
    lane8 = lax.broadcasted_iota(I32, (8, c), 1)
    gates, decays = [], []
    for ch in range(n_chunks):
        bg = bg_ref[:, ch * c:(ch + 1) * c]
        dec = bg
        shift = 1
        while shift < c:
            dec = dec + jnp.where(lane8 >= shift, pltpu.roll(dec, shift, 1), 0.0)
            shift *= 2
        gates.append(bg)
        decays.append(dec)

    def tokens(ref, n):
        ch, h = items[n]
        return ref[ch * c:(ch + 1) * c, h * GDN_HEAD_DIM:(h + 1) * GDN_HEAD_DIM]

    d_row = [jnp.broadcast_to(decays[ch][4 + h:5 + h, :], (c, c)) for ch, h in items]
    d_col = [d.T for d in d_row]
    beta_col = [jnp.broadcast_to(gates[ch][h:h + 1, :], (c, c)).T for ch, h in items]
    d_last = [d[:, c - 1:c] for d in d_row]
    gamma = [jnp.exp(jnp.where(tri, d_col[n] - d_row[n], MASKED_LOGIT)) for n in idx]
    exp_d = [jnp.exp(d_col[n]) for n in idx]
    k16 = [tokens(k_ref, n).astype(BF16) for n in idx]
    kb = [tokens(k_ref, n) * beta_col[n] for n in idx]
    nil = [jnp.where(strict, _dot_nt(kb[n].astype(BF16), k16[n]) * gamma[n], 0.0) for n in idx]
    nil_parts = [_split_bf16(x) for x in nil]
    inv = [eye - x for x in nil]
    power = [_dot_split(x, x) for x in nil_parts]
    steps = int(math.log2(c)) - 1
    for it in range(steps):
        power_parts = [_split_bf16(x) for x in power]
        inv = [inv[n] + _dot_split(_split_bf16(inv[n]), power_parts[n]) for n in idx]
        if it + 1 < steps:
            power = [_dot_split(x, x) for x in power_parts]
    rhs = [jnp.concatenate([kb[n] * exp_d[n], tokens(v_ref, n) * beta_col[n]], axis=1) for n in idx]
    wu = [_dot_split(_split_bf16(inv[n]), _split_bf16(rhs[n])).astype(BF16) for n in idx]
    aqk = [jnp.where(tri, _dot_nt(tokens(q_ref, n).astype(BF16), k16[n]) * gamma[n], 0.0).astype(BF16)
           for n in idx]
    kd_t = [(tokens(k_ref, n) * jnp.exp(d_last[n] - d_col[n])).T.astype(BF16) for n in idx]
    state_wu = [_dot(kd_t[n], wu[n]) for n in idx]
    out_wu = [_dot(aqk[n], wu[n]) for n in idx]
    for n, (ch, h) in enumerate(items):
        m_ref[ch, h] = (-state_wu[n][:, :c]).astype(BF16)
        n_ref[ch, h] = state_wu[n][:, c:]
        p_ref[ch, h] = (tokens(q_ref, n) * exp_d[n] - out_wu[n][:, :c]).astype(BF16)
        r_ref[ch, h] = out_wu[n][:, c:]
        cd_ref[ch, h:h + 1, :] = jnp.exp(d_last[n][0:1, :] + jnp.zeros((1, c), F32))


def _gdn_chunks(qn, kn, vv, bg):
    n_rows = qn.shape[0]
    nc = n_rows // LANES
    per_step = GDN_CHUNKS_PER_STEP
    tok = pl.BlockSpec((per_step * LANES, D_GDN), lambda c: (c, 0))
    mat = pl.BlockSpec((per_step, N_GDN_HEADS, LANES, LANES), lambda c: (c, 0, 0, 0))
    mat_shape = lambda dt: jax.ShapeDtypeStruct((nc, N_GDN_HEADS, LANES, LANES), dt)
    return pl.pallas_call(
        _gdn_chunk_kernel,
        out_shape=(mat_shape(BF16), mat_shape(F32), mat_shape(BF16), mat_shape(F32),
                   jax.ShapeDtypeStruct((nc, N_GDN_HEADS, LANES), F32)),
        grid=(nc // per_step,),
        in_specs=[tok, tok, tok, pl.BlockSpec((8, per_step * LANES), lambda c: (0, c))],
        out_specs=(mat, mat, mat, mat, pl.BlockSpec((per_step, N_GDN_HEADS, LANES), lambda c: (c, 0, 0))),
        compiler_params=_params("parallel"),
        name="gdn_chunks",
    )(qn, kn, vv, bg)


def _gdn_scan_kernel(m_ref, n_ref, p_ref, r_ref, cd_ref, o_ref, s_scr, *, batch):
    @pl.when(pl.program_id(0) == 0)
    def _():
        s_scr[...] = jnp.zeros(s_scr.shape, F32)

    for b in range(batch):
        for h in range(N_GDN_HEADS):
            s = s_scr[b, h]
            s16 = s.astype(BF16)
            o_ref[b, :, h * GDN_HEAD_DIM:(h + 1) * GDN_HEAD_DIM] = _dot(p_ref[b, h], s16) + r_ref[b, h]
            s_scr[b, h] = s * cd_ref[b, h:h + 1, :] + _dot(m_ref[b, h], s16) + n_ref[b, h]


def _gdn_scan(m_mat, n_mat, p_mat, r_mat, cd, batch, tp):
    nc = tp // LANES
    shape5 = lambda a: a.reshape(batch, nc, N_GDN_HEADS, LANES, LANES)
    mat = pl.BlockSpec((batch, None, N_GDN_HEADS, LANES, LANES), lambda c: (0, c, 0, 0, 0))
    return pl.pallas_call(
        functools.partial(_gdn_scan_kernel, batch=batch),
        out_shape=jax.ShapeDtypeStruct((batch, tp, D_GDN), F32),
        grid=(nc,),
        in_specs=[mat, mat, mat, mat,
                  pl.BlockSpec((batch, None, N_GDN_HEADS, LANES), lambda c: (0, c, 0, 0))],
        out_specs=pl.BlockSpec((batch, LANES, D_GDN), lambda c: (0, c, 0)),
        scratch_shapes=[pltpu.VMEM((batch, N_GDN_HEADS, LANES, LANES), F32)],
        compiler_params=_params("arbitrary"),
        name="gdn_scan",
    )(shape5(m_mat), shape5(n_mat), shape5(p_mat), shape5(r_mat), cd.reshape(batch, nc, N_GDN_HEADS, LANES))


def _out_proj_kernel(h_ref, oa_ref, za_ref, og_ref, zg_ref, gain_ref, w_ref, out_ref):
    attn = (oa_ref[...] * _silu(za_ref[...])).astype(BF16)
    y = _dot(attn, w_ref[0:D_ATTN, :])
    og, zg = og_ref[...], zg_ref[...]
    for h in range(N_GDN_HEADS):
        cols = slice(h * GDN_HEAD_DIM, (h + 1) * GDN_HEAD_DIM)
        gated = (_rms_rows(og[:, cols], gain_ref[...]) * _silu(zg[:, cols])).astype(BF16)
        y = y + _dot(gated, w_ref[D_ATTN + h * GDN_HEAD_DIM:D_ATTN + (h + 1) * GDN_HEAD_DIM, :])
    out_ref[...] = h_ref[...] + y


def _out_proj(h2, o_attn, proj, o_gdn, gain, w_out, rb):
    n_rows = h2.shape[0]
    blk = lambda width, col: pl.BlockSpec((rb, width), lambda i: (i, col))
    return pl.pallas_call(
        _out_proj_kernel,
        out_shape=jax.ShapeDtypeStruct((n_rows, D_MODEL), F32),
        grid=(n_rows // rb,),
        in_specs=[blk(D_MODEL, 0), blk(D_ATTN, 0), blk(D_ATTN, COL_Z_ATTN), blk(D_GDN, 0),
                  blk(D_GDN, COL_Z_GDN),
                  pl.BlockSpec((1, GDN_HEAD_DIM), lambda i: (0, 0)),
                  pl.BlockSpec((D_MODEL, D_MODEL), lambda i: (0, 0))],
        out_specs=blk(D_MODEL, 0),
        compiler_params=_params("parallel"),
        name="out_proj",
    )(h2, o_attn, proj, o_gdn, proj, gain, w_out)


def _pack_layer(norm_gain, w_in, cq_gain, ckv_gain, w_uq, w_ukv, w_q_idx, q_gain, k_gain, conv_w, a_log,
                dt_bias, gdn_gain, w_out):
    o = 0
    parts = {}
    for name, size in (("c_q", Q_RANK), ("c_kv", KV_RANK), ("k_idx", IDX_DIM), ("w_idx", N_IDX_HEADS),
                       ("z_attn", D_ATTN), ("qkv_g", 3 * D_GDN), ("z_g", D_GDN), ("b", N_GDN_HEADS),
                       ("a", N_GDN_HEADS)):
        parts[name] = w_in[:, o:o + size]
        o += size
    w_packed = jnp.concatenate([parts["qkv_g"], parts["c_q"], parts["c_kv"], parts["k_idx"], parts["k_idx"],
                                parts["z_attn"], parts["z_g"]], axis=1).astype(BF16)
    w_rows = jnp.concatenate([parts["w_idx"], parts["b"], parts["a"],
                              jnp.zeros((D_MODEL, 4), w_in.dtype)], axis=1).T.astype(BF16)
    return dict(
        gain=norm_gain[None, :], w_packed=w_packed, w_rows=w_rows,
        g_cq=cq_gain[None, :], g_ckv=ckv_gain[None, :],
        w_uqt=w_uq.T.astype(BF16), w_qit=w_q_idx.T.astype(BF16),
        w_ukt=w_ukv[:, :D_ATTN].T.astype(BF16), w_uvt=w_ukv[:, D_ATTN:].T.astype(BF16),
        g_q_col=q_gain[:, None], g_k_col=k_gain[:, None],
        conv_w=conv_w, a_log=a_log[:, None], dt_bias=dt_bias[:, None],
        gdn_gain=gdn_gain[None, :], w_out=w_out.astype(BF16))


def _layer(h2, lw, bias, tri, batch, tp, t_valid, topk):
    rb = _row_block(tp)
    proj, rows = _in_proj(h2, lw["gain"], lw["w_packed"], lw["w_rows"], rb)
    qt, k, vt, qit, kidx = _dsa_prep(proj, lw, batch, tp, rb)
    o_attn = _dsa_attention(qit, rows, qt, kidx, k, vt, bias, tri, batch, tp, t_valid, topk)
    qn, kn, vv, bg = _gdn_prep(proj, rows, lw, batch, tp, rb)
    m_mat, n_mat, p_mat, r_mat, cd = _gdn_chunks(qn, kn, vv, bg)
    o_gdn = _gdn_scan(m_mat, n_mat, p_mat, r_mat, cd, batch, tp).reshape(batch * tp, D_GDN)
    return _out_proj(h2, o_attn, proj, o_gdn, lw["gdn_gain"], lw["w_out"], rb)


def _forward(x, meta_tokens, rel_bias_table, layer_weights, topk):
    batch, seq, _ = x.shape
    t = seq + N_META
    tp = -(-t // KEY_TILE) * KEY_TILE
    meta = jnp.broadcast_to(meta_tokens[None].astype(x.dtype), (batch, N_META, D_MODEL))
    h = jnp.concatenate([meta, x, jnp.zeros((batch, tp - t, D_MODEL), x.dtype)], axis=1)
    h2 = h.reshape(batch * tp, D_MODEL)
    bias = _bias_tiles(rel_bias_table)
    tri = jnp.tril(jnp.ones((KEY_TILE, KEY_TILE), BF16))
    for lw in layer_weights:
        h2 = _layer(h2, lw, bias, tri, batch, tp, t, topk)
    return h2.reshape(batch, tp, D_MODEL)[:, N_META:t]


def kernel(x, meta_tokens, rel_bias_table, norm_gain, w_in, cq_norm_gain, ckv_norm_gain, w_uq, w_ukv, w_q_idx,
           q_norm_gain, k_norm_gain, conv_w, a_log, dt_bias, gdn_norm_gain, w_out):
    depth = norm_gain.shape[0]
    topk = min(TOPK_MAX, x.shape[1] // 4)
    layers = [_pack_layer(norm_gain[l], w_in[l], cq_norm_gain[l], ckv_norm_gain[l], w_uq[l], w_ukv[l],
                          w_q_idx[l], q_norm_gain[l], k_norm_gain[l], conv_w[l], a_log[l], dt_bias[l],
                          gdn_norm_gain[l], w_out[l]) for l in range(depth)]
    return _forward(x, meta_tokens, rel_bias_table, layers, topk)
```

```python
import functools
import math

import jax
import jax.numpy as jnp
from jax import lax
from jax.experimental import pallas as pl
from jax.experimental.pallas import tpu as pltpu

F32 = jnp.float32
BF16 = jnp.bfloat16
I32 = jnp.int32

D_MODEL = 1024
N_META = 16
EPS = 1e-6
N_ATTN_HEADS = 8
ATTN_HEAD_DIM = 64
D_ATTN = N_ATTN_HEADS * ATTN_HEAD_DIM
Q_RANK = 256
KV_RANK = 128
N_IDX_HEADS = 4
IDX_DIM = 64
TOPK_MAX = 256
N_REL_BUCKETS = 32
REL_MAX_DIST = 128
N_GDN_HEADS = 4
GDN_HEAD_DIM = 128
D_GDN = N_GDN_HEADS * GDN_HEAD_DIM
CONV_WIDTH = 4

LANES = 128
KEY_TILE = 256
COUNT_TILE = 512
SCAN_TILE = 1024
PASSES_PER_CHECK = 2
INTERPOLATED_PASSES = 24
MAX_SEARCH_PASSES = INTERPOLATED_PASSES + 32
GDN_CHUNKS_PER_STEP = 2
ROW_TILES = (5, 4, 3, 2, 1)
HALO_ROWS = 8
D_PACKED = 3 * D_GDN + 512 + D_ATTN + D_GDN
COL_SMALL = 3
COL_Z_ATTN = 4
COL_Z_GDN = 5
MASKED_LOGIT = -1e30
LOG2_E = math.log2(math.e)
KEY_MIN = -2 ** 31
PATTERN_NEG_FLT_MAX = KEY_MIN + (1 << 23)
VMEM_LIMIT = 56 * 1024 * 1024

NT_DIMS = (((1,), (1,)), ((), ()))


def _dot(a, b):
    return jnp.dot(a, b, preferred_element_type=F32)


def _dot_nt(a, b):
    return lax.dot_general(a, b, NT_DIMS, preferred_element_type=F32)


def _split_bf16(x):
    hi = x.astype(BF16)
    return hi, (x - hi.astype(F32)).astype(BF16)


def _dot_split(a_parts, b_parts):
    (a_hi, a_lo), (b_hi, b_lo) = a_parts, b_parts
    return _dot(a_hi, b_hi) + (_dot(a_hi, b_lo) + _dot(a_lo, b_hi))


def _sigmoid(x):
    return 1.0 / (1.0 + jnp.exp(-x))


def _silu(x):
    return x * _sigmoid(x)


def _row_block(tp):
    tiles = tp // LANES
    return LANES * next(d for d in ROW_TILES if tiles % d == 0)


def _params(*sem):
    return pltpu.CompilerParams(dimension_semantics=sem, vmem_limit_bytes=VMEM_LIMIT)


def _bias_kernel(table_ref, out_ref):
    row = lax.broadcasted_iota(I32, (LANES, LANES), 0)
    col = lax.broadcasted_iota(I32, (LANES, LANES), 1)
    max_exact = N_REL_BUCKETS // 2
    for kind in range(3):
        dist = col - row + (2 - kind) * LANES
        n = jnp.maximum(dist, 0)
        nf = jnp.maximum(n, 1).astype(F32)
        large = max_exact + (jnp.log(nf / max_exact) / math.log(REL_MAX_DIST / max_exact)
                             * (N_REL_BUCKETS - max_exact)).astype(I32)
        large = jnp.minimum(large, N_REL_BUCKETS - 1)
        bucket = jnp.where(n < max_exact, n, large)
        for h in range(N_ATTN_HEADS):
            tile = jnp.zeros((LANES, LANES), F32)
            for b in range(N_REL_BUCKETS):
                tile = jnp.where(bucket == b, table_ref[b, h], tile)
            far = table_ref[N_REL_BUCKETS - 1, h]
            out_ref[kind, :, h * LANES:(h + 1) * LANES] = (tile - far) * LOG2_E


def _bias_tiles(rel_table):
    return pl.pallas_call(
        _bias_kernel,
        out_shape=jax.ShapeDtypeStruct((3, LANES, N_ATTN_HEADS * LANES), F32),
        in_specs=[pl.BlockSpec(memory_space=pltpu.SMEM)],
        out_specs=pl.BlockSpec(memory_space=pltpu.VMEM),
        name="rel_bias_tiles",
    )(rel_table)


def _in_proj_kernel(h_ref, gain_ref, w_ref, wrows_ref, proj_ref, rows_ref):
    x = h_ref[...]
    y = x * lax.rsqrt(jnp.mean(x * x, axis=-1, keepdims=True) + EPS)
    hn = (y * gain_ref[...]).astype(BF16)
    proj_ref[...] = _dot(hn, w_ref[...])
    rows_ref[...] = _dot_nt(wrows_ref[...], hn)


def _in_proj(h2, gain, w_packed, w_rows, rb):
    n_rows = h2.shape[0]
    grid = (n_rows // rb,)
    return pl.pallas_call(
        _in_proj_kernel,
        out_shape=(jax.ShapeDtypeStruct((n_rows, D_PACKED), F32),
                   jax.ShapeDtypeStruct((16, n_rows), F32)),
        grid=grid,
        in_specs=[pl.BlockSpec((rb, D_MODEL), lambda i: (i, 0)),
                  pl.BlockSpec((1, D_MODEL), lambda i: (0, 0)),
                  pl.BlockSpec((D_MODEL, D_PACKED), lambda i: (0, 0)),
                  pl.BlockSpec((16, D_MODEL), lambda i: (0, 0))],
        out_specs=(pl.BlockSpec((rb, D_PACKED), lambda i: (i, 0)),
                   pl.BlockSpec((16, rb), lambda i: (0, i))),
        compiler_params=_params("parallel"),
        name="in_proj",
    )(h2, gain, w_packed, w_rows)


def _rms_rows(x, gain):
    return x * lax.rsqrt(jnp.mean(x * x, axis=-1, keepdims=True) + EPS) * gain


def _dsa_prep_kernel(sm_ref, gcq_ref, gckv_ref, wuqt_ref, wqit_ref, wukt_ref, wuvt_ref, gq_ref, gk_ref,
                     qt_ref, k_ref, vt_ref, qit_ref, kidx_ref):
    sm = sm_ref[...]
    rb = sm.shape[0]
    cq = _rms_rows(sm[:, :Q_RANK], gcq_ref[...]).astype(BF16)
    ckv = _rms_rows(sm[:, Q_RANK:Q_RANK + KV_RANK], gckv_ref[...]).astype(BF16)
    kidx_ref[...] = sm[:, Q_RANK + KV_RANK:].astype(BF16)
    q3 = _dot_nt(wuqt_ref[...], cq).reshape(N_ATTN_HEADS, ATTN_HEAD_DIM, rb)
    q3 = q3 * lax.rsqrt(jnp.mean(q3 * q3, axis=1, keepdims=True) + EPS) * gq_ref[...][None]
    qt_ref[...] = (q3 * (ATTN_HEAD_DIM ** -0.5 * LOG2_E)).reshape(D_ATTN, rb).astype(BF16)
    k3 = _dot_nt(wukt_ref[...], ckv).reshape(N_ATTN_HEADS, ATTN_HEAD_DIM, rb)
    k3 = k3 * lax.rsqrt(jnp.mean(k3 * k3, axis=1, keepdims=True) + EPS) * gk_ref[...][None]
    k_ref[...] = k3.reshape(D_ATTN, rb).T.astype(BF16)
    vt_ref[...] = _dot_nt(wuvt_ref[...], ckv).astype(BF16)
    qit_ref[...] = _dot_nt(wqit_ref[...], cq).astype(BF16)


def _dsa_prep(proj, lw, batch, tp, rb):
    n_rows = proj.shape[0]
    nb = tp // rb
    d_idx = N_IDX_HEADS * IDX_DIM
    const = lambda shape: pl.BlockSpec(shape, lambda b, i: (0, 0))
    row_spec = lambda width: pl.BlockSpec((rb, width), lambda b, i: (b * nb + i, 0))
    col_spec = lambda height: pl.BlockSpec((None, height, rb), lambda b, i: (b, 0, i))
    return pl.pallas_call(
        _dsa_prep_kernel,
        out_shape=(jax.ShapeDtypeStruct((batch, D_ATTN, tp), BF16),
                   jax.ShapeDtypeStruct((n_rows, D_ATTN), BF16),
                   jax.ShapeDtypeStruct((batch, D_ATTN, tp), BF16),
                   jax.ShapeDtypeStruct((batch, d_idx, tp), BF16),
                   jax.ShapeDtypeStruct((n_rows, LANES), BF16)),
        grid=(batch, nb),
        in_specs=[pl.BlockSpec((rb, 512), lambda b, i: (b * nb + i, COL_SMALL)),
                  const((1, Q_RANK)), const((1, KV_RANK)),
                  const((D_ATTN, Q_RANK)), const((d_idx, Q_RANK)),
                  const((D_ATTN, KV_RANK)), const((D_ATTN, KV_RANK)),
                  const((ATTN_HEAD_DIM, 1)), const((ATTN_HEAD_DIM, 1))],
        out_specs=(col_spec(D_ATTN), row_spec(D_ATTN), col_spec(D_ATTN), col_spec(d_idx), row_spec(LANES)),
        compiler_params=_params("parallel", "parallel"),
        name="dsa_prep",
    )(proj, lw["g_cq"], lw["g_ckv"], lw["w_uqt"], lw["w_qit"], lw["w_ukt"], lw["w_uvt"], lw["g_q_col"],
      lw["g_k_col"])


def _dsa_block(i, qit_ref, rows_ref, qt_ref, kidx_ref, k_ref, vt_ref, bias_ref, tri_ref, o_ref,
               score_scr, wi_scr, wq_scr, s_scr, p_scr, m_scr, l_scr, acc_scr, alpha_scr, mask_scr, tie_scr, *, topk):
    t0 = i * LANES
    n_kt = i // 2 + 1
    n_ct = (n_kt + 1) // 2
    n_st = (n_ct + 1) // 2
    hd = ATTN_HEAD_DIM
    pair_w = 2 * LANES
    n_pairs = N_ATTN_HEADS // 2

    zeros_hd = jnp.zeros((hd, LANES), BF16)
    for h in range(N_IDX_HEADS):
        wi_scr[0:IDX_DIM, h * LANES:(h + 1) * LANES] = qit_ref[h * IDX_DIM:(h + 1) * IDX_DIM, :]
    wi_scr[IDX_DIM:, :] = jnp.zeros((LANES - IDX_DIM, N_IDX_HEADS * LANES), BF16)
    for p in range(n_pairs):
        wq_scr[p, 0:hd, 0:LANES] = qt_ref[2 * p * hd:(2 * p + 1) * hd, :]
        wq_scr[p, 0:hd, LANES:] = zeros_hd
        wq_scr[p, hd:, 0:LANES] = zeros_hd
        wq_scr[p, hd:, LANES:] = qt_ref[(2 * p + 1) * hd:(2 * p + 2) * hd, :]

    row = lax.broadcasted_iota(I32, (KEY_TILE, LANES), 0)
    col = lax.broadcasted_iota(I32, (KEY_TILE, LANES), 1)
    w_idx = rows_ref[0:N_IDX_HEADS, :] * (N_IDX_HEADS ** -0.5 * IDX_DIM ** -0.5)

    def key_tile(j):
        return pl.multiple_of(j * KEY_TILE, KEY_TILE)

    def causal(j):
        return (j * KEY_TILE + row) <= (t0 + col)

    def fold8(x, op):
        return op(x.reshape(KEY_TILE // 8, 8, LANES), axis=0)

    def score_body(jc, carry):
        top, bottom, n_nonneg, n_pos = carry
        subs = [jc * (COUNT_TILE // KEY_TILE) + sub for sub in range(COUNT_TILE // KEY_TILE)]
        logits = [_dot(kidx_ref[pl.ds(key_tile(jnp.minimum(j, n_kt - 1)), KEY_TILE), :], wi_scr[...])
                  for j in subs]
        for j, lg in zip(subs, logits):
            score = jnp.zeros((KEY_TILE, LANES), F32)
            for h in range(N_IDX_HEADS):
                score = score + jnp.maximum(lg[:, h * LANES:(h + 1) * LANES], 0.0) * w_idx[h:h + 1, :]
            visible = causal(j)
            seen = jnp.where(visible, score, -jnp.inf)
            top = jnp.maximum(top, fold8(seen, jnp.max))
            bottom = jnp.minimum(bottom, fold8(jnp.where(visible, score, jnp.inf), jnp.min))
            n_nonneg = n_nonneg + fold8(jnp.where(seen >= 0.0, 1, 0), jnp.sum)
            n_pos = n_pos + fold8(jnp.where(seen > 0.0, 1, 0), jnp.sum)
            score_scr[pl.ds(key_tile(j), KEY_TILE), :] = seen
        return top, bottom, n_nonneg, n_pos

    zeros8 = jnp.zeros((8, LANES), I32)
    top, bottom, n_nonneg, n_pos = lax.fori_loop(
        0, n_ct, score_body,
        (jnp.full((8, LANES), -jnp.inf, F32), jnp.full((8, LANES), jnp.inf, F32), zeros8, zeros8))
    top = jnp.max(top, axis=0, keepdims=True)
    bottom = jnp.min(bottom, axis=0, keepdims=True)
    count0 = jnp.sum(n_nonneg, axis=0, keepdims=True)
    count_pos = jnp.sum(n_pos, axis=0, keepdims=True)

    @pl.when(n_ct % 2 == 1)
    def _():
        pad_rows = pl.ds(pl.multiple_of(n_ct * COUNT_TILE, COUNT_TILE), COUNT_TILE)
        score_scr[pad_rows, :] = jnp.full((COUNT_TILE, LANES), -jnp.inf, F32)

    def count_f32(cand, strict):
        def body(j, acc):
            for part in range(SCAN_TILE // COUNT_TILE):
                start = pl.multiple_of(j * SCAN_TILE + part * COUNT_TILE, COUNT_TILE)
                x = score_scr[pl.ds(start, COUNT_TILE), :]
                ind = jnp.where((x > cand) if strict else (x >= cand), 1, 0)
                acc = acc + jnp.sum(ind.reshape(COUNT_TILE // 8, 8, LANES), axis=0)
            return acc
        acc = lax.fori_loop(0, n_st, body, jnp.zeros((8, LANES), I32))
        return jnp.sum(acc, axis=0, keepdims=True)

    def to_pattern(v):
        bits = lax.bitcast_convert_type(v, I32)
        return bits ^ ((bits >> 31) & 0x7FFFFFFF)

    def to_f32(c):
        return lax.bitcast_convert_type(c ^ ((c >> 31) & 0x7FFFFFFF), F32)

    n_visible = t0 + 1 + lax.broadcasted_iota(I32, (1, LANES), 1)
    nonneg = count0 >= topk
    lo = jnp.where(nonneg, 0, to_pattern(bottom))
    hi = jnp.where(nonneg, to_pattern(top) + 1, 0)
    count_lo = jnp.where(nonneg, count0, n_visible)
    count_hi = jnp.where(nonneg, 0, count0)
    few = n_visible < topk
    lo = jnp.where(few, PATTERN_NEG_FLT_MAX, lo)
    zero_tied = nonneg & (count_pos < topk)
    hi = jnp.where(zero_tied, 1, hi)
    count_hi = jnp.where(zero_tied, count_pos, count_hi)
    open_q = jnp.where(few | zero_tied | (count_lo == topk), 0, 1)

    log_topk = math.log(topk)

    def count_error(count):
        return jnp.log(count.astype(F32) + 0.5) - log_topk

    def probes_left(st):
        n_pass, n_open = st[0], st[-1]
        return (n_pass < MAX_SEARCH_PASSES) & (n_open > 0)

    def probe_group(st):
        n_pass = st[0]

        def probe(step, carry):
            lo, hi, count_lo, count_hi, err_lo, err_hi, last_side, open_q = carry
            v_lo, v_hi = to_f32(lo), to_f32(hi)
            frac = err_lo / (err_lo - err_hi)
            frac = jnp.where(count_lo - count_hi <= 4, 0.5, frac)
            guess = to_pattern(v_lo + (v_hi - v_lo) * frac)
            middle = lo + lax.shift_right_logical(hi - lo, 1)
            cand = jnp.where(n_pass + step >= INTERPOLATED_PASSES, middle, guess)
            cand = jnp.minimum(jnp.maximum(cand, lo + 1), hi - 1)
            count = count_f32(to_f32(cand), strict=False)
            raise_lo = (open_q == 1) & (count >= topk)
            lower_hi = (open_q == 1) & (count < topk)
            err = count_error(count)
            err_hi = jnp.where(raise_lo & (last_side == 1), err_hi * 0.5, err_hi)
            err_lo = jnp.where(lower_hi & (last_side == -1), err_lo * 0.5, err_lo)
            err_lo = jnp.where(raise_lo, err, err_lo)
            err_hi = jnp.where(lower_hi, err, err_hi)
            lo = jnp.where(raise_lo, cand, lo)
            count_lo = jnp.where(raise_lo, count, count_lo)
            hi = jnp.where(lower_hi, cand, hi)
            count_hi = jnp.where(lower_hi, count, count_hi)
            last_side = jnp.where(raise_lo, 1, jnp.where(lower_hi, -1, last_side))
            closed = (count_lo == topk) | (hi - lo == 1)
            return lo, hi, count_lo, count_hi, err_lo, err_hi, last_side, jnp.where(closed, 0, open_q)

        carry = lax.fori_loop(0, PASSES_PER_CHECK, probe, st[1:-1])
        return (n_pass + PASSES_PER_CHECK,) + carry + (jnp.sum(carry[-1]),)

    state = (jnp.int32(0), lo, hi, count_lo, count_hi, count_error(count_lo), count_error(count_hi),
             jnp.zeros((1, LANES), I32), open_q, jnp.sum(open_q))
    state = lax.while_loop(probes_left, probe_group, state)
    lo, count_lo, count_hi = state[1], state[3], state[4]
    tau = to_f32(lo)
    need = jnp.where((count_lo == topk) | few, topk, topk - count_hi).astype(F32)

    m_scr[...] = jnp.full(m_scr.shape, MASKED_LOGIT, F32)
    l_scr[...] = jnp.zeros(l_scr.shape, F32)
    acc_scr[...] = jnp.zeros(acc_scr.shape, F32)
    ones_rows = jnp.ones((16, KEY_TILE), BF16)
    last = n_kt - 1

    def mask_pair(j_first, parity):
        xs = [score_scr[pl.ds(key_tile(jnp.minimum(j_first + slot, last)), KEY_TILE), :] for slot in range(2)]
        ties = [x == tau for x in xs]
        tie_cols = jnp.concatenate([jnp.where(tie, 1.0, 0.0).astype(BF16) for tie in ties], axis=1)
        ranks = _dot(tri_ref[...], tie_cols)
        tie_carry = tie_scr[0:1, :]
        for slot, (x, tie) in enumerate(zip(xs, ties)):
            rank = ranks[:, slot * LANES:(slot + 1) * LANES] + tie_carry
            tie_carry = rank[KEY_TILE - 1:KEY_TILE, :]
            take = (tie & (rank <= need)) | (x > tau)
            mask_scr[parity, slot] = jnp.where(take, 0.0, MASKED_LOGIT)
        tie_scr[0:1, :] = tie_carry

    def qk_pair(slot, j, p):
        s_scr[slot, :, p * pair_w:(p + 1) * pair_w] = _dot(
            k_ref[pl.ds(key_tile(j), KEY_TILE), p * LANES:(p + 1) * LANES], wq_scr[p])

    def softmax_pair(slot, j, p, near, parity):
        mask_add = mask_scr[parity, slot]
        alphas = []
        for h in (2 * p, 2 * p + 1):
            cols = slice(h * LANES, (h + 1) * LANES)
            logits = s_scr[slot, :, cols] + mask_add
            if near:
                kind_top = jnp.clip(2 * j - i + 2, 0, 2)
                kind_bot = jnp.clip(2 * j - i + 3, 0, 2)
                logits = logits + jnp.concatenate(
                    [bias_ref[kind_top, :, cols], bias_ref[kind_bot, :, cols]], axis=0)
            m_old = m_scr[h:h + 1, :]
            m_new = jnp.maximum(m_old, jnp.max(logits, axis=0, keepdims=True))
            m_scr[h:h + 1, :] = m_new
            p_scr[slot, :, cols] = jnp.exp2(logits - m_new).astype(BF16)
            alphas.append(jnp.exp2(m_old - m_new))
        return alphas

    def pv_pair(slot, j, p, alphas):
        lhs = jnp.concatenate([vt_ref[p * 2 * hd:(p + 1) * 2 * hd, pl.ds(key_tile(j), KEY_TILE)], ones_rows],
                              axis=0)
        out = _dot(lhs, p_scr[slot, :, p * pair_w:(p + 1) * pair_w])
        for half in range(2):
            h = 2 * p + half
            rows_h = slice(h * hd, (h + 1) * hd)
            q_cols = slice(half * LANES, (half + 1) * LANES)
            acc_scr[rows_h, :] = acc_scr[rows_h, :] * alphas[half] + out[half * hd:(half + 1) * hd, q_cols]
            l_scr[h:h + 1, :] = l_scr[h:h + 1, :] * alphas[half] + out[2 * hd:2 * hd + 1, q_cols]

    def pending_alphas(p):
        return [alpha_scr[h:h + 1, :] for h in (2 * p, 2 * p + 1)]

    def clear_pending():
        p_scr[1] = jnp.zeros(p_scr.shape[1:], BF16)
        alpha_scr[...] = jnp.ones(alpha_scr.shape, F32)

    def pair_step(ja, j_pending, j_next, near, parity):
        mask_pair(ja + 2, 1 - parity)
        alphas_a = []
        for p in range(n_pairs):
            pv_pair(1, j_pending, p, pending_alphas(p))
            qk_pair(1, ja + 1, p)
            alphas_a.append(softmax_pair(0, ja, p, near, parity))
        for p in range(n_pairs):
            pv_pair(0, ja, p, alphas_a[p])
            qk_pair(0, j_next, p)
            alphas_b = softmax_pair(1, ja + 1, p, near, parity)
            for half in range(2):
                alpha_scr[2 * p + half:2 * p + half + 1, :] = alphas_b[half]

    def single_step(ja, j_pending, near, parity):
        alphas_a = []
        for p in range(n_pairs):
            pv_pair(1, j_pending, p, pending_alphas(p))
            alphas_a.append(softmax_pair(0, ja, p, near, parity))
        for p in range(n_pairs):
            pv_pair(0, ja, p, alphas_a[p])
        clear_pending()

    n_far = 4 * (jnp.maximum(n_kt - 2, 0) // 4)
    n_near = n_kt - n_far
    clear_pending()
    tie_scr[...] = jnp.zeros(tie_scr.shape, F32)
    mask_pair(0, 0)
    for p in range(n_pairs):
        qk_pair(0, 0, p)

    def far_body(jq, carry):
        ja = 4 * jq
        pair_step(ja, jnp.maximum(ja - 1, 0), ja + 2, near=False, parity=0)
        pair_step(ja + 2, ja + 1, ja + 4, near=False, parity=1)
        return carry

    lax.fori_loop(0, n_far // 4, far_body, 0)

    @pl.when(n_near >= 2)
    def _():
        pair_step(n_far, jnp.maximum(n_far - 1, 0), jnp.minimum(n_far + 2, last), near=True, parity=0)

    @pl.when(n_near >= 4)
    def _():
        pair_step(n_far + 2, n_far + 1, jnp.minimum(n_far + 4, last), near=True, parity=1)

    @pl.when(n_near % 2 == 1)
    def _():
        single_step(last, jnp.where(n_near >= 3, last - 1, jnp.maximum(n_far - 1, 0)), near=True,
                    parity=((n_near - 1) // 2) % 2)

    for p in range(n_pairs):
        pv_pair(1, last, p, pending_alphas(p))

    for h in range(N_ATTN_HEADS):
        rows_h = slice(h * hd, (h + 1) * hd)
        acc_scr[rows_h, :] = acc_scr[rows_h, :] / l_scr[h:h + 1, :]
    o_ref[...] = acc_scr[...].T


def _dsa_kernel(*refs, topk, t_valid):
    o_ref = refs[8]
    i = pl.program_id(1)
    is_real = i * LANES < t_valid

    @pl.when(is_real)
    def _():
        _dsa_block(i, *refs, topk=topk)

    @pl.when(jnp.logical_not(is_real))
    def _():
        o_ref[...] = jnp.zeros(o_ref.shape, F32)


def _dsa_attention(qit, rows, qt, kidx, k, vt, bias, tri, batch, tp, t_valid, topk):
    n_rows = k.shape[0]
    nqb = tp // LANES
    d_idx = N_IDX_HEADS * IDX_DIM
    n_pairs = N_ATTN_HEADS // 2
    key_rows = -(-tp // SCAN_TILE) * SCAN_TILE
    q_cols = lambda height: pl.BlockSpec((None, height, LANES), lambda b, i: (b, 0, i))
    return pl.pallas_call(
        functools.partial(_dsa_kernel, topk=topk, t_valid=t_valid),
        out_shape=jax.ShapeDtypeStruct((n_rows, D_ATTN), F32),
        grid=(batch, nqb),
        in_specs=[q_cols(d_idx),
                  pl.BlockSpec((16, LANES), lambda b, i: (0, b * nqb + i)),
                  q_cols(D_ATTN),
                  pl.BlockSpec((tp, LANES), lambda b, i: (b, 0)),
                  pl.BlockSpec((tp, D_ATTN), lambda b, i: (b, 0)),
                  pl.BlockSpec((None, D_ATTN, tp), lambda b, i: (b, 0, 0)),
                  pl.BlockSpec((3, LANES, N_ATTN_HEADS * LANES), lambda b, i: (0, 0, 0)),
                  pl.BlockSpec((KEY_TILE, KEY_TILE), lambda b, i: (0, 0))],
        out_specs=pl.BlockSpec((LANES, D_ATTN), lambda b, i: (b * nqb + i, 0)),
        scratch_shapes=[pltpu.VMEM((key_rows, LANES), F32),
                        pltpu.VMEM((LANES, N_IDX_HEADS * LANES), BF16),
                        pltpu.VMEM((n_pairs, LANES, 2 * LANES), BF16),
                        pltpu.VMEM((2, KEY_TILE, N_ATTN_HEADS * LANES), F32),
                        pltpu.VMEM((2, KEY_TILE, N_ATTN_HEADS * LANES), BF16),
                        pltpu.VMEM((N_ATTN_HEADS, LANES), F32),
                        pltpu.VMEM((N_ATTN_HEADS, LANES), F32),
                        pltpu.VMEM((D_ATTN, LANES), F32),
                        pltpu.VMEM((N_ATTN_HEADS, LANES), F32),
                        pltpu.VMEM((2, 2, KEY_TILE, LANES), F32),
                        pltpu.VMEM((8, LANES), F32)],
        compiler_params=_params("parallel", "parallel"),
        name="dsa_attention",
    )(qit, rows, qt, kidx, k, vt, bias, tri)


def _gdn_prep_kernel(x_ref, halo_ref, cw_ref, rows_ref, alog_ref, dtb_ref, q_ref, k_ref, v_ref, bg_ref, buf):
    first = pl.program_id(1) == 0
    rb = x_ref.shape[0]
    buf[0:HALO_ROWS, :] = jnp.where(first, 0.0, halo_ref[...])
    buf[HALO_ROWS:, :] = x_ref[...]
    acc = jnp.zeros((rb, 3 * D_GDN), F32)
    for tap in range(CONV_WIDTH):
        start = HALO_ROWS - (CONV_WIDTH - 1) + tap
        acc = acc + cw_ref[tap:tap + 1, :] * buf[start:start + rb, :]
    y = _silu(acc)
    for h in range(N_GDN_HEADS):
        cols = slice(h * GDN_HEAD_DIM, (h + 1) * GDN_HEAD_DIM)
        qh = y[:, cols]
        kh = y[:, D_GDN + h * GDN_HEAD_DIM:D_GDN + (h + 1) * GDN_HEAD_DIM]
        q_ref[:, cols] = (qh * lax.rsqrt(jnp.sum(qh * qh, axis=-1, keepdims=True) + EPS)
                          * (GDN_HEAD_DIM ** -0.5))
        k_ref[:, cols] = kh * lax.rsqrt(jnp.sum(kh * kh, axis=-1, keepdims=True) + EPS)
    v_ref[...] = y[:, 2 * D_GDN:]
    rows = rows_ref[...]
    beta = _sigmoid(rows[4:8, :])
    a = rows[8:12, :] + dtb_ref[...]
    softplus = jnp.maximum(a, 0.0) + jnp.log1p(jnp.exp(-jnp.abs(a)))
    bg_ref[0:4, :] = beta
    bg_ref[4:8, :] = -jnp.exp(alog_ref[...]) * softplus


def _gdn_prep(proj, rows, lw, batch, tp, rb):
    n_rows = proj.shape[0]
    nb = tp // rb
    halo_per_block = rb // HALO_ROWS
    row_spec = pl.BlockSpec((rb, D_GDN), lambda b, i: (b * nb + i, 0))
    return pl.pallas_call(
        _gdn_prep_kernel,
        out_shape=(jax.ShapeDtypeStruct((n_rows, D_GDN), F32),) * 3
        + (jax.ShapeDtypeStruct((8, n_rows), F32),),
        grid=(batch, nb),
        in_specs=[pl.BlockSpec((rb, 3 * D_GDN), lambda b, i: (b * nb + i, 0)),
                  pl.BlockSpec((HALO_ROWS, 3 * D_GDN),
                               lambda b, i: (jnp.maximum((b * nb + i) * halo_per_block - 1, 0), 0)),
                  pl.BlockSpec((CONV_WIDTH, 3 * D_GDN), lambda b, i: (0, 0)),
                  pl.BlockSpec((16, rb), lambda b, i: (0, b * nb + i)),
                  pl.BlockSpec((N_GDN_HEADS, 1), lambda b, i: (0, 0)),
                  pl.BlockSpec((N_GDN_HEADS, 1), lambda b, i: (0, 0))],
        out_specs=(row_spec, row_spec, row_spec,
                   pl.BlockSpec((8, rb), lambda b, i: (0, b * nb + i))),
        scratch_shapes=[pltpu.VMEM((HALO_ROWS + rb, 3 * D_GDN), F32)],
        compiler_params=_params("parallel", "parallel"),
        name="gdn_prep",
    )(proj, proj, lw["conv_w"], rows, lw["a_log"], lw["dt_bias"])


def _gdn_chunk_kernel(q_ref, k_ref, v_ref, bg_ref, m_ref, n_ref, p_ref, r_ref, cd_ref):
    c = LANES
    n_chunks = q_ref.shape[0] // c
    items = [(ch, h) for ch in range(n_chunks) for h in range(N_GDN_HEADS)]
    idx = range(len(items))
    row = lax.broadcasted_iota(I32, (c, c), 0)
    col = lax.broadcasted_iota(I32, (c, c), 1)
    tri = row >= col
    strict = row > col
    eye = jnp.where(row == col, 1.0, 0.0)
    lane8 = lax.broadcasted_iota(I32, (8, c), 1)
    gates, decays = [], []
    for ch in range(n_chunks):
        bg = bg_ref[:, ch * c:(ch + 1) * c]
        dec = bg
        shift = 1
        while shift < c:
            dec = dec + jnp.where(lane8 >= shift, pltpu.roll(dec, shift, 1), 0.0)
            shift *= 2
        gates.append(bg)
        decays.append(dec)

    def tokens(ref, n):
        ch, h = items[n]
        return ref[ch * c:(ch + 1) * c, h * GDN_HEAD_DIM:(h + 1) * GDN_HEAD_DIM]

    d_row = [jnp.broadcast_to(decays[ch][4 + h:5 + h, :], (c, c)) for ch, h in items]
    d_col = [d.T for d in d_row]
    beta_col = [jnp.broadcast_to(gates[ch][h:h + 1, :], (c, c)).T for ch, h in items]
    d_last = [d[:, c - 1:c] for d in d_row]
    gamma = [jnp.exp(jnp.where(tri, d_col[n] - d_row[n], MASKED_LOGIT)) for n in idx]
    exp_d = [jnp.exp(d_col[n]) for n in idx]
    k16 = [tokens(k_ref, n).astype(BF16) for n in idx]
    kb = [tokens(k_ref, n) * beta_col[n] for n in idx]
    nil = [jnp.where(strict, _dot_nt(kb[n].astype(BF16), k16[n]) * gamma[n], 0.0) for n in idx]
    nil_parts = [_split_bf16(x) for x in nil]
    inv = [eye - x for x in nil]
    power = [_dot_split(x, x) for x in nil_parts]
    steps = int(math.log2(c)) - 1
    for it in range(steps):
        power_parts = [_split_bf16(x) for x in power]
        inv = [inv[n] + _dot_split(_split_bf16(inv[n]), power_parts[n]) for n in idx]
        if it + 1 < steps:
            power = [_dot_split(x, x) for x in power_parts]
    rhs = [jnp.concatenate([kb[n] * exp_d[n], tokens(v_ref, n) * beta_col[n]], axis=1) for n in idx]
    wu = [_dot_split(_split_bf16(inv[n]), _split_bf16(rhs[n])).astype(BF16) for n in idx]
    aqk = [jnp.where(tri, _dot_nt(tokens(q_ref, n).astype(BF16), k16[n]) * gamma[n], 0.0).astype(BF16)
           for n in idx]
    kd_t = [(tokens(k_ref, n) * jnp.exp(d_last[n] - d_col[n])).T.astype(BF16) for n in idx]
    state_wu = [_dot(kd_t[n], wu[n]) for n in idx]
    out_wu = [_dot(aqk[n], wu[n]) for n in idx]
    for n, (ch, h) in enumerate(items):
        m_ref[ch, h] = (-state_wu[n][:, :c]).astype(BF16)
        n_ref[ch, h] = state_wu[n][:, c:]
        p_ref[ch, h] = (tokens(q_ref, n) * exp_d[n] - out_wu[n][:, :c]).astype(BF16)
        r_ref[ch, h] = out_wu[n][:, c:]
        cd_ref[ch, h:h + 1, :] = jnp.exp(d_last[n][0:1, :] + jnp.zeros((1, c), F32))


def _gdn_chunks(qn, kn, vv, bg):
    n_rows = qn.shape[0]
    nc = n_rows // LANES
    per_step = GDN_CHUNKS_PER_STEP
    tok = pl.BlockSpec((per_step * LANES, D_GDN), lambda c: (c, 0))
    mat = pl.BlockSpec((per_step, N_GDN_HEADS, LANES, LANES), lambda c: (c, 0, 0, 0))
    mat_shape = lambda dt: jax.ShapeDtypeStruct((nc, N_GDN_HEADS, LANES, LANES), dt)
    return pl.pallas_call(
        _gdn_chunk_kernel,
        out_shape=(mat_shape(BF16), mat_shape(F32), mat_shape(BF16), mat_shape(F32),
                   jax.ShapeDtypeStruct((nc, N_GDN_HEADS, LANES), F32)),
        grid=(nc // per_step,),
        in_specs=[tok, tok, tok, pl.BlockSpec((8, per_step * LANES), lambda c: (0, c))],
        out_specs=(mat, mat, mat, mat, pl.BlockSpec((per_step, N_GDN_HEADS, LANES), lambda c: (c, 0, 0))),
        compiler_params=_params("parallel"),
        name="gdn_chunks",
    )(qn, kn, vv, bg)


def _gdn_scan_kernel(m_ref, n_ref, p_ref, r_ref, cd_ref, o_ref, s_scr, *, batch):
    @pl.when(pl.program_id(0) == 0)
    def _():
        s_scr[...] = jnp.zeros(s_scr.shape, F32)

    for b in range(batch):
        for h in range(N_GDN_HEADS):
            s = s_scr[b, h]
            s16 = s.astype(BF16)
            o_ref[b, :, h * GDN_HEAD_DIM:(h + 1) * GDN_HEAD_DIM] = _dot(p_ref[b, h], s16) + r_ref[b, h]
            s_scr[b, h] = s * cd_ref[b, h:h + 1, :] + _dot(m_ref[b, h], s16) + n_ref[b, h]


def _gdn_scan(m_mat, n_mat, p_mat, r_mat, cd, batch, tp):
    nc = tp // LANES
    shape5 = lambda a: a.reshape(batch, nc, N_GDN_HEADS, LANES, LANES)
    mat = pl.BlockSpec((batch, None, N_GDN_HEADS, LANES, LANES), lambda c: (0, c, 0, 0, 0))
    return pl.pallas_call(
        functools.partial(_gdn_scan_kernel, batch=batch),
        out_shape=jax.ShapeDtypeStruct((batch, tp, D_GDN), F32),
        grid=(nc,),
        in_specs=[mat, mat, mat, mat,
                  pl.BlockSpec((batch, None, N_GDN_HEADS, LANES), lambda c: (0, c, 0, 0))],
        out_specs=pl.BlockSpec((batch, LANES, D_GDN), lambda c: (0, c, 0)),
        scratch_shapes=[pltpu.VMEM((batch, N_GDN_HEADS, LANES, LANES), F32)],
        compiler_params=_params("arbitrary"),
        name="gdn_scan",
    )(shape5(m_mat), shape5(n_mat), shape5(p_mat), shape5(r_mat), cd.reshape(batch, nc, N_GDN_HEADS, LANES))


def _out_proj_kernel(h_ref, oa_ref, za_ref, og_ref, zg_ref, gain_ref, w_ref, out_ref):
    attn = (oa_ref[...] * _silu(za_ref[...])).astype(BF16)
    y = _dot(attn, w_ref[0:D_ATTN, :])
    og, zg = og_ref[...], zg_ref[...]
    for h in range(N_GDN_HEADS):
        cols = slice(h * GDN_HEAD_DIM, (h + 1) * GDN_HEAD_DIM)
        gated = (_rms_rows(og[:, cols], gain_ref[...]) * _silu(zg[:, cols])).astype(BF16)
        y = y + _dot(gated, w_ref[D_ATTN + h * GDN_HEAD_DIM:D_ATTN + (h + 1) * GDN_HEAD_DIM, :])
    out_ref[...] = h_ref[...] + y


def _out_proj(h2, o_attn, proj, o_gdn, gain, w_out, rb):
    n_rows = h2.shape[0]
    blk = lambda width, col: pl.BlockSpec((rb, width), lambda i: (i, col))
    return pl.pallas_call(
        _out_proj_kernel,
        out_shape=jax.ShapeDtypeStruct((n_rows, D_MODEL), F32),
        grid=(n_rows // rb,),
        in_specs=[blk(D_MODEL, 0), blk(D_ATTN, 0), blk(D_ATTN, COL_Z_ATTN), blk(D_GDN, 0),
                  blk(D_GDN, COL_Z_GDN),
                  pl.BlockSpec((1, GDN_HEAD_DIM), lambda i: (0, 0)),
                  pl.BlockSpec((D_MODEL, D_MODEL), lambda i: (0, 0))],
        out_specs=blk(D_MODEL, 0),
        compiler_params=_params("parallel"),
        name="out_proj",
    )(h2, o_attn, proj, o_gdn, proj, gain, w_out)


def _pack_layer(norm_gain, w_in, cq_gain, ckv_gain, w_uq, w_ukv, w_q_idx, q_gain, k_gain, conv_w, a_log,
                dt_bias, gdn_gain, w_out):
    o = 0
    parts = {}
    for name, size in (("c_q", Q_RANK), ("c_kv", KV_RANK), ("k_idx", IDX_DIM), ("w_idx", N_IDX_HEADS),
                       ("z_attn", D_ATTN), ("qkv_g", 3 * D_GDN), ("z_g", D_GDN), ("b", N_GDN_HEADS),
                       ("a", N_GDN_HEADS)):
        parts[name] = w_in[:, o:o + size]
        o += size
    w_packed = jnp.concatenate([parts["qkv_g"], parts["c_q"], parts["c_kv"], parts["k_idx"], parts["k_idx"],
                                parts["z_attn"], parts["z_g"]], axis=1).astype(BF16)
    w_rows = jnp.concatenate([parts["w_idx"], parts["b"], parts["a"],
                              jnp.zeros((D_MODEL, 4), w_in.dtype)], axis=1).T.astype(BF16)
    return dict(
        gain=norm_gain[None, :], w_packed=w_packed, w_rows=w_rows,
        g_cq=cq_gain[None, :], g_ckv=ckv_gain[None, :],
        w_uqt=w_uq.T.astype(BF16), w_qit=w_q_idx.T.astype(BF16),
        w_ukt=w_ukv[:, :D_ATTN].T.astype(BF16), w_uvt=w_ukv[:, D_ATTN:].T.astype(BF16),
        g_q_col=q_gain[:, None], g_k_col=k_gain[:, None],
        conv_w=conv_w, a_log=a_log[:, None], dt_bias=dt_bias[:, None],
        gdn_gain=gdn_gain[None, :], w_out=w_out.astype(BF16))


def _layer(h2, lw, bias, tri, batch, tp, t_valid, topk):
    rb = _row_block(tp)
    proj, rows = _in_proj(h2, lw["gain"], lw["w_packed"], lw["w_rows"], rb)
    qt, k, vt, qit, kidx = _dsa_prep(proj, lw, batch, tp, rb)
    o_attn = _dsa_attention(qit, rows, qt, kidx, k, vt, bias, tri, batch, tp, t_valid, topk)
    qn, kn, vv, bg = _gdn_prep(proj, rows, lw, batch, tp, rb)
    m_mat, n_mat, p_mat, r_mat, cd = _gdn_chunks(qn, kn, vv, bg)
    o_gdn = _gdn_scan(m_mat, n_mat, p_mat, r_mat, cd, batch, tp).reshape(batch * tp, D_GDN)
    return _out_proj(h2, o_attn, proj, o_gdn, lw["gdn_gain"], lw["w_out"], rb)


def _forward(x, meta_tokens, rel_bias_table, layer_weights, topk):
    batch, seq, _ = x.shape
    t = seq + N_META
    tp = -(-t // KEY_TILE) * KEY_TILE
    meta = jnp.broadcast_to(meta_tokens[None].astype(x.dtype), (batch, N_META, D_MODEL))
    h = jnp.concatenate([meta, x, jnp.zeros((batch, tp - t, D_MODEL), x.dtype)], axis=1)
    h2 = h.reshape(batch * tp, D_MODEL)
    bias = _bias_tiles(rel_bias_table)
    tri = jnp.tril(jnp.ones((KEY_TILE, KEY_TILE), BF16))
    for lw in layer_weights:
        h2 = _layer(h2, lw, bias, tri, batch, tp, t, topk)
    return h2.reshape(batch, tp, D_MODEL)[:, N_META:t]


def kernel(x, meta_tokens, rel_bias_table, norm_gain, w_in, cq_norm_gain, ckv_norm_gain, w_uq, w_ukv, w_q_idx,
           q_norm_gain, k_norm_gain, conv_w, a_log, dt_bias, gdn_norm_gain, w_out):
    depth = norm_gain.shape[0]
    topk = min(TOPK_MAX, x.shape[1] // 4)
    layers = [_pack_layer(norm_gain[l], w_in[l], cq_norm_gain[l], ckv_norm_gain[l], w_uq[l], w_ukv[l],
                          w_q_idx[l], q_norm_gain[l], k_norm_gain[l], conv_w[l], a_log[l], dt_bias[l],
                          gdn_norm_gain[l], w_out[l]) for l in range(depth)]
    return _forward(x, meta_tokens, rel_bias_table, layers, topk)
```

```python
import functools
import math

import jax
import jax.numpy as jnp
from jax import lax
from jax.experimental import pallas as pl
from jax.experimental.pallas import tpu as pltpu

F32 = jnp.float32
BF16 = jnp.bfloat16
I32 = jnp.int32

D_MODEL = 1024
N_META = 16
EPS = 1e-6
N_ATTN_HEADS = 8
ATTN_HEAD_DIM = 64
D_ATTN = N_ATTN_HEADS * ATTN_HEAD_DIM
Q_RANK = 256
KV_RANK = 128
N_IDX_HEADS = 4
IDX_DIM = 64
TOPK_MAX = 256
N_REL_BUCKETS = 32
REL_MAX_DIST = 128
N_GDN_HEADS = 4
GDN_HEAD_DIM = 128
D_GDN = N_GDN_HEADS * GDN_HEAD_DIM
CONV_WIDTH = 4

LANES = 128
KEY_TILE = 256
COUNT_TILE = 512
SCAN_TILE = 1024
PASSES_PER_CHECK = 2
INTERPOLATED_PASSES = 24
MAX_SEARCH_PASSES = INTERPOLATED_PASSES + 32
GDN_CHUNKS_PER_STEP = 2
ROW_TILES = (5, 4, 3, 2, 1)
HALO_ROWS = 8
D_PACKED = 3 * D_GDN + 512 + D_ATTN + D_GDN
COL_SMALL = 3
COL_Z_ATTN = 4
COL_Z_GDN = 5
MASKED_LOGIT = -1e30
LOG2_E = math.log2(math.e)
KEY_MIN = -2 ** 31
PATTERN_NEG_FLT_MAX = KEY_MIN + (1 << 23)
VMEM_LIMIT = 56 * 1024 * 1024

NT_DIMS = (((1,), (1,)), ((), ()))


def _dot(a, b):
    return jnp.dot(a, b, preferred_element_type=F32)


def _dot_nt(a, b):
    return lax.dot_general(a, b, NT_DIMS, preferred_element_type=F32)


def _split_bf16(x):
    hi = x.astype(BF16)
    return hi, (x - hi.astype(F32)).astype(BF16)


def _dot_split(a_parts, b_parts):
    (a_hi, a_lo), (b_hi, b_lo) = a_parts, b_parts
    return _dot(a_hi, b_hi) + (_dot(a_hi, b_lo) + _dot(a_lo, b_hi))


def _sigmoid(x):
    return 1.0 / (1.0 + jnp.exp(-x))


def _silu(x):
    return x * _sigmoid(x)


def _row_block(tp):
    tiles = tp // LANES
    return LANES * next(d for d in ROW_TILES if tiles % d == 0)


def _params(*sem):
    return pltpu.CompilerParams(dimension_semantics=sem, vmem_limit_bytes=VMEM_LIMIT)


def _bias_kernel(table_ref, out_ref):
    row = lax.broadcasted_iota(I32, (LANES, LANES), 0)
    col = lax.broadcasted_iota(I32, (LANES, LANES), 1)
    max_exact = N_REL_BUCKETS // 2
    for kind in range(3):
        dist = col - row + (2 - kind) * LANES
        n = jnp.maximum(dist, 0)
        nf = jnp.maximum(n, 1).astype(F32)
        large = max_exact + (jnp.log(nf / max_exact) / math.log(REL_MAX_DIST / max_exact)
                             * (N_REL_BUCKETS - max_exact)).astype(I32)
        large = jnp.minimum(large, N_REL_BUCKETS - 1)
        bucket = jnp.where(n < max_exact, n, large)
        for h in range(N_ATTN_HEADS):
            tile = jnp.zeros((LANES, LANES), F32)
            for b in range(N_REL_BUCKETS):
                tile = jnp.where(bucket == b, table_ref[b, h], tile)
            far = table_ref[N_REL_BUCKETS - 1, h]
            out_ref[kind, :, h * LANES:(h + 1) * LANES] = (tile - far) * LOG2_E


def _bias_tiles(rel_table):
    return pl.pallas_call(
        _bias_kernel,
        out_shape=jax.ShapeDtypeStruct((3, LANES, N_ATTN_HEADS * LANES), F32),
        in_specs=[pl.BlockSpec(memory_space=pltpu.SMEM)],
        out_specs=pl.BlockSpec(memory_space=pltpu.VMEM),
        name="rel_bias_tiles",
    )(rel_table)


def _in_proj_kernel(h_ref, gain_ref, w_ref, wrows_ref, proj_ref, rows_ref):
    x = h_ref[...]
    y = x * lax.rsqrt(jnp.mean(x * x, axis=-1, keepdims=True) + EPS)
    hn = (y * gain_ref[...]).astype(BF16)
    proj_ref[...] = _dot(hn, w_ref[...])
    rows_ref[...] = _dot_nt(wrows_ref[...], hn)


def _in_proj(h2, gain, w_packed, w_rows, rb):
    n_rows = h2.shape[0]
    grid = (n_rows // rb,)
    return pl.pallas_call(
        _in_proj_kernel,
        out_shape=(jax.ShapeDtypeStruct((n_rows, D_PACKED), F32),
                   jax.ShapeDtypeStruct((16, n_rows), F32)),
        grid=grid,
        in_specs=[pl.BlockSpec((rb, D_MODEL), lambda i: (i, 0)),
                  pl.BlockSpec((1, D_MODEL), lambda i: (0, 0)),
                  pl.BlockSpec((D_MODEL, D_PACKED), lambda i: (0, 0)),
                  pl.BlockSpec((16, D_MODEL), lambda i: (0, 0))],
        out_specs=(pl.BlockSpec((rb, D_PACKED), lambda i: (i, 0)),
                   pl.BlockSpec((16, rb), lambda i: (0, i))),
        compiler_params=_params("parallel"),
        name="in_proj",
    )(h2, gain, w_packed, w_rows)


def _rms_rows(x, gain):
    return x * lax.rsqrt(jnp.mean(x * x, axis=-1, keepdims=True) + EPS) * gain


def _dsa_prep_kernel(sm_ref, gcq_ref, gckv_ref, wuqt_ref, wqit_ref, wukt_ref, wuvt_ref, gq_ref, gk_ref,
                     qt_ref, k_ref, vt_ref, qit_ref, kidx_ref):
    sm = sm_ref[...]
    rb = sm.shape[0]
    cq = _rms_rows(sm[:, :Q_RANK], gcq_ref[...]).astype(BF16)
    ckv = _rms_rows(sm[:, Q_RANK:Q_RANK + KV_RANK], gckv_ref[...]).astype(BF16)
    kidx_ref[...] = sm[:, Q_RANK + KV_RANK:].astype(BF16)
    q3 = _dot_nt(wuqt_ref[...], cq).reshape(N_ATTN_HEADS, ATTN_HEAD_DIM, rb)
    q3 = q3 * lax.rsqrt(jnp.mean(q3 * q3, axis=1, keepdims=True) + EPS) * gq_ref[...][None]
    qt_ref[...] = (q3 * (ATTN_HEAD_DIM ** -0.5 * LOG2_E)).reshape(D_ATTN, rb).astype(BF16)
    k3 = _dot_nt(wukt_ref[...], ckv).reshape(N_ATTN_HEADS, ATTN_HEAD_DIM, rb)
    k3 = k3 * lax.rsqrt(jnp.mean(k3 * k3, axis=1, keepdims=True) + EPS) * gk_ref[...][None]
    k_ref[...] = k3.reshape(D_ATTN, rb).T.astype(BF16)
    vt_ref[...] = _dot_nt(wuvt_ref[...], ckv).astype(BF16)
    qit_ref[...] = _dot_nt(wqit_ref[...], cq).astype(BF16)


def _dsa_prep(proj, lw, batch, tp, rb):
    n_rows = proj.shape[0]
    nb = tp // rb
    d_idx = N_IDX_HEADS * IDX_DIM
    const = lambda shape: pl.BlockSpec(shape, lambda b, i: (0, 0))
    row_spec = lambda width: pl.BlockSpec((rb, width), lambda b, i: (b * nb + i, 0))
    col_spec = lambda height: pl.BlockSpec((None, height, rb), lambda b, i: (b, 0, i))
    return pl.pallas_call(
        _dsa_prep_kernel,
        out_shape=(jax.ShapeDtypeStruct((batch, D_ATTN, tp), BF16),
                   jax.ShapeDtypeStruct((n_rows, D_ATTN), BF16),
                   jax.ShapeDtypeStruct((batch, D_ATTN, tp), BF16),
                   jax.ShapeDtypeStruct((batch, d_idx, tp), BF16),
                   jax.ShapeDtypeStruct((n_rows, LANES), BF16)),
        grid=(batch, nb),
        in_specs=[pl.BlockSpec((rb, 512), lambda b, i: (b * nb + i, COL_SMALL)),
                  const((1, Q_RANK)), const((1, KV_RANK)),
                  const((D_ATTN, Q_RANK)), const((d_idx, Q_RANK)),
                  const((D_ATTN, KV_RANK)), const((D_ATTN, KV_RANK)),
                  const((ATTN_HEAD_DIM, 1)), const((ATTN_HEAD_DIM, 1))],
        out_specs=(col_spec(D_ATTN), row_spec(D_ATTN), col_spec(D_ATTN), col_spec(d_idx), row_spec(LANES)),
        compiler_params=_params("parallel", "parallel"),
        name="dsa_prep",
    )(proj, lw["g_cq"], lw["g_ckv"], lw["w_uqt"], lw["w_qit"], lw["w_ukt"], lw["w_uvt"], lw["g_q_col"],
      lw["g_k_col"])


def _dsa_block(i, qit_ref, rows_ref, qt_ref, kidx_ref, k_ref, vt_ref, bias_ref, tri_ref, o_ref,
               score_scr, wi_scr, wq_scr, s_scr, p_scr, m_scr, l_scr, acc_scr, alpha_scr, mask_scr, tie_scr, *, topk):
    t0 = i * LANES
    n_kt = i // 2 + 1
    n_ct = (n_kt + 1) // 2
    n_st = (n_ct + 1) // 2
    hd = ATTN_HEAD_DIM
    pair_w = 2 * LANES
    n_pairs = N_ATTN_HEADS // 2

    zeros_hd = jnp.zeros((hd, LANES), BF16)
    for h in range(N_IDX_HEADS):
        wi_scr[0:IDX_DIM, h * LANES:(h + 1) * LANES] = qit_ref[h * IDX_DIM:(h + 1) * IDX_DIM, :]
    wi_scr[IDX_DIM:, :] = jnp.zeros((LANES - IDX_DIM, N_IDX_HEADS * LANES), BF16)
    for p in range(n_pairs):
        wq_scr[p, 0:hd, 0:LANES] = qt_ref[2 * p * hd:(2 * p + 1) * hd, :]
        wq_scr[p, 0:hd, LANES:] = zeros_hd
        wq_scr[p, hd:, 0:LANES] = zeros_hd
        wq_scr[p, hd:, LANES:] = qt_ref[(2 * p + 1) * hd:(2 * p + 2) * hd, :]

    row = lax.broadcasted_iota(I32, (KEY_TILE, LANES), 0)
    col = lax.broadcasted_iota(I32, (KEY_TILE, LANES), 1)
    w_idx = rows_ref[0:N_IDX_HEADS, :] * (N_IDX_HEADS ** -0.5 * IDX_DIM ** -0.5)

    def key_tile(j):
        return pl.multiple_of(j * KEY_TILE, KEY_TILE)

    def causal(j):
        return (j * KEY_TILE + row) <= (t0 + col)

    def fold8(x, op):
        return op(x.reshape(KEY_TILE // 8, 8, LANES), axis=0)

    def score_body(jc, carry):
        top, bottom, n_nonneg, n_pos = carry
        subs = [jc * (COUNT_TILE // KEY_TILE) + sub for sub in range(COUNT_TILE // KEY_TILE)]
        logits = [_dot(kidx_ref[pl.ds(key_tile(jnp.minimum(j, n_kt - 1)), KEY_TILE), :], wi_scr[...])
                  for j in subs]
        for j, lg in zip(subs, logits):
            score = jnp.zeros((KEY_TILE, LANES), F32)
            for h in range(N_IDX_HEADS):
                score = score + jnp.maximum(lg[:, h * LANES:(h + 1) * LANES], 0.0) * w_idx[h:h + 1, :]
            visible = causal(j)
            seen = jnp.where(visible, score, -jnp.inf)
            top = jnp.maximum(top, fold8(seen, jnp.max))
            bottom = jnp.minimum(bottom, fold8(jnp.where(visible, score, jnp.inf), jnp.min))
            n_nonneg = n_nonneg + fold8(jnp.where(seen >= 0.0, 1, 0), jnp.sum)
            n_pos = n_pos + fold8(jnp.where(seen > 0.0, 1, 0), jnp.sum)
            score_scr[pl.ds(key_tile(j), KEY_TILE), :] = seen
        return top, bottom, n_nonneg, n_pos

    zeros8 = jnp.zeros((8, LANES), I32)
    top, bottom, n_nonneg, n_pos = lax.fori_loop(
        0, n_ct, score_body,
        (jnp.full((8, LANES), -jnp.inf, F32), jnp.full((8, LANES), jnp.inf, F32), zeros8, zeros8))
    top = jnp.max(top, axis=0, keepdims=True)
    bottom = jnp.min(bottom, axis=0, keepdims=True)
    count0 = jnp.sum(n_nonneg, axis=0, keepdims=True)
    count_pos = jnp.sum(n_pos, axis=0, keepdims=True)

    @pl.when(n_ct % 2 == 1)
    def _():
        pad_rows = pl.ds(pl.multiple_of(n_ct * COUNT_TILE, COUNT_TILE), COUNT_TILE)
        score_scr[pad_rows, :] = jnp.full((COUNT_TILE, LANES), -jnp.inf, F32)

    def count_f32(cand, strict):
        def body(j, acc):
            for part in range(SCAN_TILE // COUNT_TILE):
                start = pl.multiple_of(j * SCAN_TILE + part * COUNT_TILE, COUNT_TILE)
                x = score_scr[pl.ds(start, COUNT_TILE), :]
                ind = jnp.where((x > cand) if strict else (x >= cand), 1, 0)
                acc = acc + jnp.sum(ind.reshape(COUNT_TILE // 8, 8, LANES), axis=0)
            return acc
        acc = lax.fori_loop(0, n_st, body, jnp.zeros((8, LANES), I32))
        return jnp.sum(acc, axis=0, keepdims=True)

    def to_pattern(v):
        bits = lax.bitcast_convert_type(v, I32)
        return bits ^ ((bits >> 31) & 0x7FFFFFFF)

    def to_f32(c):
        return lax.bitcast_convert_type(c ^ ((c >> 31) & 0x7FFFFFFF), F32)

    n_visible = t0 + 1 + lax.broadcasted_iota(I32, (1, LANES), 1)
    nonneg = count0 >= topk
    lo = jnp.where(nonneg, 0, to_pattern(bottom))
    hi = jnp.where(nonneg, to_pattern(top) + 1, 0)
    count_lo = jnp.where(nonneg, count0, n_visible)
    count_hi = jnp.where(nonneg, 0, count0)
    few = n_visible < topk
    lo = jnp.where(few, PATTERN_NEG_FLT_MAX, lo)
    zero_tied = nonneg & (count_pos < topk)
    hi = jnp.where(zero_tied, 1, hi)
    count_hi = jnp.where(zero_tied, count_pos, count_hi)
    open_q = jnp.where(few | zero_tied | (count_lo == topk), 0, 1)

    log_topk = math.log(topk)

    def count_error(count):
        return jnp.log(count.astype(F32) + 0.5) - log_topk

    def probes_left(st):
        n_pass, n_open = st[0], st[-1]
        return (n_pass < MAX_SEARCH_PASSES) & (n_open > 0)

    def probe_group(st):
        n_pass = st[0]

        def probe(step, carry):
            lo, hi, count_lo, count_hi, err_lo, err_hi, last_side, open_q = carry
            v_lo, v_hi = to_f32(lo), to_f32(hi)
            frac = err_lo / (err_lo - err_hi)
            frac = jnp.where(count_lo - count_hi <= 4, 0.5, frac)
            guess = to_pattern(v_lo + (v_hi - v_lo) * frac)
            middle = lo + lax.shift_right_logical(hi - lo, 1)
            cand = jnp.where(n_pass + step >= INTERPOLATED_PASSES, middle, guess)
            cand = jnp.minimum(jnp.maximum(cand, lo + 1), hi - 1)
            count = count_f32(to_f32(cand), strict=False)
            raise_lo = (open_q == 1) & (count >= topk)
            lower_hi = (open_q == 1) & (count < topk)
            err = count_error(count)
            err_hi = jnp.where(raise_lo & (last_side == 1), err_hi * 0.5, err_hi)
            err_lo = jnp.where(lower_hi & (last_side == -1), err_lo * 0.5, err_lo)
            err_lo = jnp.where(raise_lo, err, err_lo)
            err_hi = jnp.where(lower_hi, err, err_hi)
            lo = jnp.where(raise_lo, cand, lo)
            count_lo = jnp.where(raise_lo, count, count_lo)
            hi = jnp.where(lower_hi, cand, hi)
            count_hi = jnp.where(lower_hi, count, count_hi)
            last_side = jnp.where(raise_lo, 1, jnp.where(lower_hi, -1, last_side))
            closed = (count_lo == topk) | (hi - lo == 1)
            return lo, hi, count_lo, count_hi, err_lo, err_hi, last_side, jnp.where(closed, 0, open_q)

        carry = lax.fori_loop(0, PASSES_PER_CHECK, probe, st[1:-1])
        return (n_pass + PASSES_PER_CHECK,) + carry + (jnp.sum(carry[-1]),)

    state = (jnp.int32(0), lo, hi, count_lo, count_hi, count_error(count_lo), count_error(count_hi),
             jnp.zeros((1, LANES), I32), open_q, jnp.sum(open_q))
    state = lax.while_loop(probes_left, probe_group, state)
    lo, count_lo, count_hi = state[1], state[3], state[4]
    tau = to_f32(lo)
    need = jnp.where((count_lo == topk) | few, topk, topk - count_hi).astype(F32)

    m_scr[...] = jnp.full(m_scr.shape, MASKED_LOGIT, F32)
    l_scr[...] = jnp.zeros(l_scr.shape, F32)
    acc_scr[...] = jnp.zeros(acc_scr.shape, F32)
    ones_rows = jnp.ones((16, KEY_TILE), BF16)
    last = n_kt - 1

    def mask_pair(j_first, parity):
        xs = [score_scr[pl.ds(key_tile(jnp.minimum(j_first + slot, last)), KEY_TILE), :] for slot in range(2)]
        ties = [x == tau for x in xs]
        tie_cols = jnp.concatenate([jnp.where(tie, 1.0, 0.0).astype(BF16) for tie in ties], axis=1)
        ranks = _dot(tri_ref[...], tie_cols)
        tie_carry = tie_scr[0:1, :]
        for slot, (x, tie) in enumerate(zip(xs, ties)):
            rank = ranks[:, slot * LANES:(slot + 1) * LANES] + tie_carry
            tie_carry = rank[KEY_TILE - 1:KEY_TILE, :]
            take = (tie & (rank <= need)) | (x > tau)
            mask_scr[parity, slot] = jnp.where(take, 0.0, MASKED_LOGIT)
        tie_scr[0:1, :] = tie_carry

    def qk_pair(slot, j, p):
        s_scr[slot, :, p * pair_w:(p + 1) * pair_w] = _dot(
            k_ref[pl.ds(key_tile(j), KEY_TILE), p * LANES:(p + 1) * LANES], wq_scr[p])

    def softmax_pair(slot, j, p, near, parity):
        alphas = []
        for h in (2 * p, 2 * p + 1):
            cols = slice(h * LANES, (h + 1) * LANES)

            def masked_logits():
                logits = s_scr[slot, :, cols] + mask_scr[parity, slot]
                if near:
                    kind_top = jnp.clip(2 * j - i + 2, 0, 2)
                    kind_bot = jnp.clip(2 * j - i + 3, 0, 2)
                    logits = logits + jnp.concatenate(
                        [bias_ref[kind_top, :, cols], bias_ref[kind_bot, :, cols]], axis=0)
                return logits

            m_old = m_scr[h:h + 1, :]
            m_new = jnp.maximum(m_old, jnp.max(masked_logits(), axis=0, keepdims=True))
            m_scr[h:h + 1, :] = m_new
            p_scr[slot, :, cols] = jnp.exp2(masked_logits() - m_new).astype(BF16)
            alphas.append(jnp.exp2(m_old - m_new))
        return alphas

    def pv_pair(slot, j, p, alphas):
        lhs = jnp.concatenate([vt_ref[p * 2 * hd:(p + 1) * 2 * hd, pl.ds(key_tile(j), KEY_TILE)], ones_rows],
                              axis=0)
        out = _dot(lhs, p_scr[slot, :, p * pair_w:(p + 1) * pair_w])
        for half in range(2):
            h = 2 * p + half
            rows_h = slice(h * hd, (h + 1) * hd)
            q_cols = slice(half * LANES, (half + 1) * LANES)
            acc_scr[rows_h, :] = acc_scr[rows_h, :] * alphas[half] + out[half * hd:(half + 1) * hd, q_cols]
            l_scr[h:h + 1, :] = l_scr[h:h + 1, :] * alphas[half] + out[2 * hd:2 * hd + 1, q_cols]

    def pending_alphas(p):
        return [alpha_scr[h:h + 1, :] for h in (2 * p, 2 * p + 1)]

    def clear_pending():
        p_scr[1] = jnp.zeros(p_scr.shape[1:], BF16)
        alpha_scr[...] = jnp.ones(alpha_scr.shape, F32)

    def pair_step(ja, j_pending, j_next, near, parity):
        alphas_a = []
        for p in range(n_pairs):
            pv_pair(1, j_pending, p, pending_alphas(p))
            qk_pair(1, ja + 1, p)
            alphas_a.append(softmax_pair(0, ja, p, near, parity))
        for p in range(n_pairs):
            pv_pair(0, ja, p, alphas_a[p])
            qk_pair(0, j_next, p)
            alphas_b = softmax_pair(1, ja + 1, p, near, parity)
            for half in range(2):
                alpha_scr[2 * p + half:2 * p + half + 1, :] = alphas_b[half]
        mask_pair(ja + 2, 1 - parity)

    def single_step(ja, j_pending, near, parity):
        alphas_a = []
        for p in range(n_pairs):
            pv_pair(1, j_pending, p, pending_alphas(p))
            alphas_a.append(softmax_pair(0, ja, p, near, parity))
        for p in range(n_pairs):
            pv_pair(0, ja, p, alphas_a[p])
        clear_pending()

    n_far = 2 * (jnp.maximum(n_kt - 2, 0) // 2)
    n_near = n_kt - n_far
    first_near_parity = (n_far // 2) % 2
    clear_pending()
    tie_scr[...] = jnp.zeros(tie_scr.shape, F32)
    mask_pair(0, 0)
    for p in range(n_pairs):
        qk_pair(0, 0, p)

    def far_body(jp, carry):
        pair_step(2 * jp, jnp.maximum(2 * jp - 1, 0), 2 * jp + 2, near=False, parity=jp % 2)
        return carry

    lax.fori_loop(0, n_far // 2, far_body, 0)

    @pl.when(n_near >= 2)
    def _():
        pair_step(n_far, jnp.maximum(n_far - 1, 0), jnp.minimum(n_far + 2, last), near=True,
                  parity=first_near_parity)

    @pl.when(n_near % 2 == 1)
    def _():
        single_step(last, jnp.where(n_near == 3, n_far + 1, jnp.maximum(n_far - 1, 0)), near=True,
                    parity=jnp.where(n_near == 3, 1 - first_near_parity, first_near_parity))

    for p in range(n_pairs):
        pv_pair(1, last, p, pending_alphas(p))

    for h in range(N_ATTN_HEADS):
        rows_h = slice(h * hd, (h + 1) * hd)
        acc_scr[rows_h, :] = acc_scr[rows_h, :] / l_scr[h:h + 1, :]
    o_ref[...] = acc_scr[...].T


def _dsa_kernel(*refs, topk, t_valid):
    o_ref = refs[8]
    i = pl.program_id(1)
    is_real = i * LANES < t_valid

    @pl.when(is_real)
    def _():
        _dsa_block(i, *refs, topk=topk)

    @pl.when(jnp.logical_not(is_real))
    def _():
        o_ref[...] = jnp.zeros(o_ref.shape, F32)


def _dsa_attention(qit, rows, qt, kidx, k, vt, bias, tri, batch, tp, t_valid, topk):
    n_rows = k.shape[0]
    nqb = tp // LANES
    d_idx = N_IDX_HEADS * IDX_DIM
    n_pairs = N_ATTN_HEADS // 2
    key_rows = -(-tp // SCAN_TILE) * SCAN_TILE
    q_cols = lambda height: pl.BlockSpec((None, height, LANES), lambda b, i: (b, 0, i))
    return pl.pallas_call(
        functools.partial(_dsa_kernel, topk=topk, t_valid=t_valid),
        out_shape=jax.ShapeDtypeStruct((n_rows, D_ATTN), F32),
        grid=(batch, nqb),
        in_specs=[q_cols(d_idx),
                  pl.BlockSpec((16, LANES), lambda b, i: (0, b * nqb + i)),
                  q_cols(D_ATTN),
                  pl.BlockSpec((tp, LANES), lambda b, i: (b, 0)),
                  pl.BlockSpec((tp, D_ATTN), lambda b, i: (b, 0)),
                  pl.BlockSpec((None, D_ATTN, tp), lambda b, i: (b, 0, 0)),
                  pl.BlockSpec((3, LANES, N_ATTN_HEADS * LANES), lambda b, i: (0, 0, 0)),
                  pl.BlockSpec((KEY_TILE, KEY_TILE), lambda b, i: (0, 0))],
        out_specs=pl.BlockSpec((LANES, D_ATTN), lambda b, i: (b * nqb + i, 0)),
        scratch_shapes=[pltpu.VMEM((key_rows, LANES), F32),
                        pltpu.VMEM((LANES, N_IDX_HEADS * LANES), BF16),
                        pltpu.VMEM((n_pairs, LANES, 2 * LANES), BF16),
                        pltpu.VMEM((2, KEY_TILE, N_ATTN_HEADS * LANES), F32),
                        pltpu.VMEM((2, KEY_TILE, N_ATTN_HEADS * LANES), BF16),
                        pltpu.VMEM((N_ATTN_HEADS, LANES), F32),
                        pltpu.VMEM((N_ATTN_HEADS, LANES), F32),
                        pltpu.VMEM((D_ATTN, LANES), F32),
                        pltpu.VMEM((N_ATTN_HEADS, LANES), F32),
                        pltpu.VMEM((2, 2, KEY_TILE, LANES), F32),
                        pltpu.VMEM((8, LANES), F32)],
        compiler_params=_params("parallel", "parallel"),
        name="dsa_attention",
    )(qit, rows, qt, kidx, k, vt, bias, tri)


def _gdn_prep_kernel(x_ref, halo_ref, cw_ref, rows_ref, alog_ref, dtb_ref, q_ref, k_ref, v_ref, bg_ref, buf):
    first = pl.program_id(1) == 0
    rb = x_ref.shape[0]
    buf[0:HALO_ROWS, :] = jnp.where(first, 0.0, halo_ref[...])
    buf[HALO_ROWS:, :] = x_ref[...]
    acc = jnp.zeros((rb, 3 * D_GDN), F32)
    for tap in range(CONV_WIDTH):
        start = HALO_ROWS - (CONV_WIDTH - 1) + tap
        acc = acc + cw_ref[tap:tap + 1, :] * buf[start:start + rb, :]
    y = _silu(acc)
    for h in range(N_GDN_HEADS):
        cols = slice(h * GDN_HEAD_DIM, (h + 1) * GDN_HEAD_DIM)
        qh = y[:, cols]
        kh = y[:, D_GDN + h * GDN_HEAD_DIM:D_GDN + (h + 1) * GDN_HEAD_DIM]
        q_ref[:, cols] = (qh * lax.rsqrt(jnp.sum(qh * qh, axis=-1, keepdims=True) + EPS)
                          * (GDN_HEAD_DIM ** -0.5))
        k_ref[:, cols] = kh * lax.rsqrt(jnp.sum(kh * kh, axis=-1, keepdims=True) + EPS)
    v_ref[...] = y[:, 2 * D_GDN:]
    rows = rows_ref[...]
    beta = _sigmoid(rows[4:8, :])
    a = rows[8:12, :] + dtb_ref[...]
    softplus = jnp.maximum(a, 0.0) + jnp.log1p(jnp.exp(-jnp.abs(a)))
    bg_ref[0:4, :] = beta
    bg_ref[4:8, :] = -jnp.exp(alog_ref[...]) * softplus


def _gdn_prep(proj, rows, lw, batch, tp, rb):
    n_rows = proj.shape[0]
    nb = tp // rb
    halo_per_block = rb // HALO_ROWS
    row_spec = pl.BlockSpec((rb, D_GDN), lambda b, i: (b * nb + i, 0))
    return pl.pallas_call(
        _gdn_prep_kernel,
        out_shape=(jax.ShapeDtypeStruct((n_rows, D_GDN), F32),) * 3
        + (jax.ShapeDtypeStruct((8, n_rows), F32),),
        grid=(batch, nb),
        in_specs=[pl.BlockSpec((rb, 3 * D_GDN), lambda b, i: (b * nb + i, 0)),
                  pl.BlockSpec((HALO_ROWS, 3 * D_GDN),
                               lambda b, i: (jnp.maximum((b * nb + i) * halo_per_block - 1, 0), 0)),
                  pl.BlockSpec((CONV_WIDTH, 3 * D_GDN), lambda b, i: (0, 0)),
                  pl.BlockSpec((16, rb), lambda b, i: (0, b * nb + i)),
                  pl.BlockSpec((N_GDN_HEADS, 1), lambda b, i: (0, 0)),
                  pl.BlockSpec((N_GDN_HEADS, 1), lambda b, i: (0, 0))],
        out_specs=(row_spec, row_spec, row_spec,
                   pl.BlockSpec((8, rb), lambda b, i: (0, b * nb + i))),
        scratch_shapes=[pltpu.VMEM((HALO_ROWS + rb, 3 * D_GDN), F32)],
        compiler_params=_params("parallel", "parallel"),
        name="gdn_prep",
    )(proj, proj, lw["conv_w"], rows, lw["a_log"], lw["dt_bias"])


def _gdn_chunk_kernel(q_ref, k_ref, v_ref, bg_ref, m_ref, n_ref, p_ref, r_ref, cd_ref):
    c = LANES
    n_chunks = q_ref.shape[0] // c
    items = [(ch, h) for ch in range(n_chunks) for h in range(N_GDN_HEADS)]
    idx = range(len(items))
    row = lax.broadcasted_iota(I32, (c, c), 0)
    col = lax.broadcasted_iota(I32, (c, c), 1)
    tri = row >= col
    strict = row > col
    eye = jnp.where(row == col, 1.0, 0.0)
    lane8 = lax.broadcasted_iota(I32, (8, c), 1)
    gates, decays = [], []
    for ch in range(n_chunks):
        bg = bg_ref[:, ch * c:(ch + 1) * c]
        dec = bg
        shift = 1
        while shift < c:
            dec = dec + jnp.where(lane8 >= shift, pltpu.roll(dec, shift, 1), 0.0)
            shift *= 2
        gates.append(bg)
        decays.append(dec)

    def tokens(ref, n):
        ch, h = items[n]
        return ref[ch * c:(ch + 1) * c, h * GDN_HEAD_DIM:(h + 1) * GDN_HEAD_DIM]

    d_row = [jnp.broadcast_to(decays[ch][4 + h:5 + h, :], (c, c)) for ch, h in items]
    d_col = [d.T for d in d_row]
    beta_col = [jnp.broadcast_to(gates[ch][h:h + 1, :], (c, c)).T for ch, h in items]
    d_last = [d[:, c - 1:c] for d in d_row]
    gamma = [jnp.exp(jnp.where(tri, d_col[n] - d_row[n], MASKED_LOGIT)) for n in idx]
    exp_d = [jnp.exp(d_col[n]) for n in idx]
    k16 = [tokens(k_ref, n).astype(BF16) for n in idx]
    kb = [tokens(k_ref, n) * beta_col[n] for n in idx]
    nil = [jnp.where(strict, _dot_nt(kb[n].astype(BF16), k16[n]) * gamma[n], 0.0) for n in idx]
    nil_parts = [_split_bf16(x) for x in nil]
    inv = [eye - x for x in nil]
    power = [_dot_split(x, x) for x in nil_parts]
    steps = int(math.log2(c)) - 1
    for it in range(steps):
        power_parts = [_split_bf16(x) for x in power]
        inv = [inv[n] + _dot_split(_split_bf16(inv[n]), power_parts[n]) for n in idx]
        if it + 1 < steps:
            power = [_dot_split(x, x) for x in power_parts]
    rhs = [jnp.concatenate([kb[n] * exp_d[n], tokens(v_ref, n) * beta_col[n]], axis=1) for n in idx]
    wu = [_dot_split(_split_bf16(inv[n]), _split_bf16(rhs[n])).astype(BF16) for n in idx]
    aqk = [jnp.where(tri, _dot_nt(tokens(q_ref, n).astype(BF16), k16[n]) * gamma[n], 0.0).astype(BF16)
           for n in idx]
    kd_t = [(tokens(k_ref, n) * jnp.exp(d_last[n] - d_col[n])).T.astype(BF16) for n in idx]
    state_wu = [_dot(kd_t[n], wu[n]) for n in idx]
    out_wu = [_dot(aqk[n], wu[n]) for n in idx]
    for n, (ch, h) in enumerate(items):
        m_ref[ch, h] = (-state_wu[n][:, :c]).astype(BF16)
        n_ref[ch, h] = state_wu[n][:, c:]
        p_ref[ch, h] = (tokens(q_ref, n) * exp_d[n] - out_wu[n][:, :c]).astype(BF16)
        r_ref[ch, h] = out_wu[n][:, c:]
        cd_ref[ch, h:h + 1, :] = jnp.exp(d_last[n][0:1, :] + jnp.zeros((1, c), F32))


def _gdn_chunks(qn, kn, vv, bg):
    n_rows = qn.shape[0]
    nc = n_rows // LANES
    per_step = GDN_CHUNKS_PER_STEP
    tok = pl.BlockSpec((per_step * LANES, D_GDN), lambda c: (c, 0))
    mat = pl.BlockSpec((per_step, N_GDN_HEADS, LANES, LANES), lambda c: (c, 0, 0, 0))
    mat_shape = lambda dt: jax.ShapeDtypeStruct((nc, N_GDN_HEADS, LANES, LANES), dt)
    return pl.pallas_call(
        _gdn_chunk_kernel,
        out_shape=(mat_shape(BF16), mat_shape(F32), mat_shape(BF16), mat_shape(F32),
                   jax.ShapeDtypeStruct((nc, N_GDN_HEADS, LANES), F32)),
        grid=(nc // per_step,),
        in_specs=[tok, tok, tok, pl.BlockSpec((8, per_step * LANES), lambda c: (0, c))],
        out_specs=(mat, mat, mat, mat, pl.BlockSpec((per_step, N_GDN_HEADS, LANES), lambda c: (c, 0, 0))),
        compiler_params=_params("parallel"),
        name="gdn_chunks",
    )(qn, kn, vv, bg)


def _gdn_scan_kernel(m_ref, n_ref, p_ref, r_ref, cd_ref, o_ref, s_scr, *, batch):
    @pl.when(pl.program_id(0) == 0)
    def _():
        s_scr[...] = jnp.zeros(s_scr.shape, F32)

    for b in range(batch):
        for h in range(N_GDN_HEADS):
            s = s_scr[b, h]
            s16 = s.astype(BF16)
            o_ref[b, :, h * GDN_HEAD_DIM:(h + 1) * GDN_HEAD_DIM] = _dot(p_ref[b, h], s16) + r_ref[b, h]
            s_scr[b, h] = s * cd_ref[b, h:h + 1, :] + _dot(m_ref[b, h], s16) + n_ref[b, h]


def _gdn_scan(m_mat, n_mat, p_mat, r_mat, cd, batch, tp):
    nc = tp // LANES
    shape5 = lambda a: a.reshape(batch, nc, N_GDN_HEADS, LANES, LANES)
    mat = pl.BlockSpec((batch, None, N_GDN_HEADS, LANES, LANES), lambda c: (0, c, 0, 0, 0))
    return pl.pallas_call(
        functools.partial(_gdn_scan_kernel, batch=batch),
        out_shape=jax.ShapeDtypeStruct((batch, tp, D_GDN), F32),
        grid=(nc,),
        in_specs=[mat, mat, mat, mat,
                  pl.BlockSpec((batch, None, N_GDN_HEADS, LANES), lambda c: (0, c, 0, 0))],
        out_specs=pl.BlockSpec((batch, LANES, D_GDN), lambda c: (0, c, 0)),
        scratch_shapes=[pltpu.VMEM((batch, N_GDN_HEADS, LANES, LANES), F32)],
        compiler_params=_params("arbitrary"),
        name="gdn_scan",
    )(shape5(m_mat), shape5(n_mat), shape5(p_mat), shape5(r_mat), cd.reshape(batch, nc, N_GDN_HEADS, LANES))


def _out_proj_kernel(h_ref, oa_ref, za_ref, og_ref, zg_ref, gain_ref, w_ref, out_ref):
    attn = (oa_ref[...] * _silu(za_ref[...])).astype(BF16)
    y = _dot(attn, w_ref[0:D_ATTN, :])
    og, zg = og_ref[...], zg_ref[...]
    for h in range(N_GDN_HEADS):
        cols = slice(h * GDN_HEAD_DIM, (h + 1) * GDN_HEAD_DIM)
        gated = (_rms_rows(og[:, cols], gain_ref[...]) * _silu(zg[:, cols])).astype(BF16)
        y = y + _dot(gated, w_ref[D_ATTN + h * GDN_HEAD_DIM:D_ATTN + (h + 1) * GDN_HEAD_DIM, :])
    out_ref[...] = h_ref[...] + y


def _out_proj(h2, o_attn, proj, o_gdn, gain, w_out, rb):
    n_rows = h2.shape[0]
    blk = lambda width, col: pl.BlockSpec((rb, width), lambda i: (i, col))
    return pl.pallas_call(
        _out_proj_kernel,
        out_shape=jax.ShapeDtypeStruct((n_rows, D_MODEL), F32),
        grid=(n_rows // rb,),
        in_specs=[blk(D_MODEL, 0), blk(D_ATTN, 0), blk(D_ATTN, COL_Z_ATTN), blk(D_GDN, 0),
                  blk(D_GDN, COL_Z_GDN),
                  pl.BlockSpec((1, GDN_HEAD_DIM), lambda i: (0, 0)),
                  pl.BlockSpec((D_MODEL, D_MODEL), lambda i: (0, 0))],
        out_specs=blk(D_MODEL, 0),
        compiler_params=_params("parallel"),
        name="out_proj",
    )(h2, o_attn, proj, o_gdn, proj, gain, w_out)


def _pack_layer(norm_gain, w_in, cq_gain, ckv_gain, w_uq, w_ukv, w_q_idx, q_gain, k_gain, conv_w, a_log,
                dt_bias, gdn_gain, w_out):
    o = 0
    parts = {}
    for name, size in (("c_q", Q_RANK), ("c_kv", KV_RANK), ("k_idx", IDX_DIM), ("w_idx", N_IDX_HEADS),
                       ("z_attn", D_ATTN), ("qkv_g", 3 * D_GDN), ("z_g", D_GDN), ("b", N_GDN_HEADS),
                       ("a", N_GDN_HEADS)):
        parts[name] = w_in[:, o:o + size]
        o += size
    w_packed = jnp.concatenate([parts["qkv_g"], parts["c_q"], parts["c_kv"], parts["k_idx"], parts["k_idx"],
                                parts["z_attn"], parts["z_g"]], axis=1).astype(BF16)
    w_rows = jnp.concatenate([parts["w_idx"], parts["b"], parts["a"],
                              jnp.zeros((D_MODEL, 4), w_in.dtype)], axis=1).T.astype(BF16)
    return dict(
        gain=norm_gain[None, :], w_packed=w_packed, w_rows=w_rows,
        g_cq=cq_gain[None, :], g_ckv=ckv_gain[None, :],
        w_uqt=w_uq.T.astype(BF16), w_qit=w_q_idx.T.astype(BF16),
        w_ukt=w_ukv[:, :D_ATTN].T.astype(BF16), w_uvt=w_ukv[:, D_ATTN:].T.astype(BF16),
        g_q_col=q_gain[:, None], g_k_col=k_gain[:, None],
        conv_w=conv_w, a_log=a_log[:, None], dt_bias=dt_bias[:, None],
        gdn_gain=gdn_gain[None, :], w_out=w_out.astype(BF16))


def _layer(h2, lw, bias, tri, batch, tp, t_valid, topk):
    rb = _row_block(tp)
    proj, rows = _in_proj(h2, lw["gain"], lw["w_packed"], lw["w_rows"], rb)
    qt, k, vt, qit, kidx = _dsa_prep(proj, lw, batch, tp, rb)
    o_attn = _dsa_attention(qit, rows, qt, kidx, k, vt, bias, tri, batch, tp, t_valid, topk)
    qn, kn, vv, bg = _gdn_prep(proj, rows, lw, batch, tp, rb)
    m_mat, n_mat, p_mat, r_mat, cd = _gdn_chunks(qn, kn, vv, bg)
    o_gdn = _gdn_scan(m_mat, n_mat, p_mat, r_mat, cd, batch, tp).reshape(batch * tp, D_GDN)
    return _out_proj(h2, o_attn, proj, o_gdn, lw["gdn_gain"], lw["w_out"], rb)


def _forward(x, meta_tokens, rel_bias_table, layer_weights, topk):
    batch, seq, _ = x.shape
    t = seq + N_META
    tp = -(-t // KEY_TILE) * KEY_TILE
    meta = jnp.broadcast_to(meta_tokens[None].astype(x.dtype), (batch, N_META, D_MODEL))
    h = jnp.concatenate([meta, x, jnp.zeros((batch, tp - t, D_MODEL), x.dtype)], axis=1)
    h2 = h.reshape(batch * tp, D_MODEL)
    bias = _bias_tiles(rel_bias_table)
    tri = jnp.tril(jnp.ones((KEY_TILE, KEY_TILE), BF16))
    for lw in layer_weights:
        h2 = _layer(h2, lw, bias, tri, batch, tp, t, topk)
    return h2.reshape(batch, tp, D_MODEL)[:, N_META:t]


def kernel(x, meta_tokens, rel_bias_table, norm_gain, w_in, cq_norm_gain, ckv_norm_gain, w_uq, w_ukv, w_q_idx,
           q_norm_gain, k_norm_gain, conv_w, a_log, dt_bias, gdn_norm_gain, w_out):
    depth = norm_gain.shape[0]
    topk = min(TOPK_MAX, x.shape[1] // 4)
    layers = [_pack_layer(norm_gain[l], w_in[l], cq_norm_gain[l], ckv_norm_gain[l], w_uq[l], w_ukv[l],
                          w_q_idx[l], q_norm_gain[l], k_norm_gain[l], conv_w[l], a_log[l], dt_bias[l],
                          gdn_norm_gain[l], w_out[l]) for l in range(depth)]
    return _forward(x, meta_tokens, rel_bias_table, layers, topk)
```

```python
import functools
import math

import jax
import jax.numpy as jnp
from jax import lax
from jax.experimental import pallas as pl
from jax.experimental.pallas import tpu as pltpu

F32 = jnp.float32
BF16 = jnp.bfloat16
I32 = jnp.int32

D_MODEL = 1024
N_META = 16
EPS = 1e-6
N_ATTN_HEADS = 8
ATTN_HEAD_DIM = 64
D_ATTN = N_ATTN_HEADS * ATTN_HEAD_DIM
Q_RANK = 256
KV_RANK = 128
N_IDX_HEADS = 4
IDX_DIM = 64
TOPK_MAX = 256
N_REL_BUCKETS = 32
REL_MAX_DIST = 128
N_GDN_HEADS = 4
GDN_HEAD_DIM = 128
D_GDN = N_GDN_HEADS * GDN_HEAD_DIM
CONV_WIDTH = 4

LANES = 128
KEY_TILE = 256
COUNT_TILE = 512
SCAN_TILE = 1024
FIRST_CHECK_PASSES = 8
PASSES_PER_CHECK = 2
INTERPOLATED_PASSES = 24
MAX_SEARCH_PASSES = INTERPOLATED_PASSES + 32
GDN_CHUNKS_PER_STEP = 2
ROW_TILES = (5, 4, 3, 2, 1)
HALO_ROWS = 8
D_PACKED = 3 * D_GDN + 512 + D_ATTN + D_GDN
COL_SMALL = 3
COL_Z_ATTN = 4
COL_Z_GDN = 5
MASKED_LOGIT = -1e30
LOG2_E = math.log2(math.e)
KEY_MIN = -2 ** 31
PATTERN_NEG_FLT_MAX = KEY_MIN + (1 << 23)
VMEM_LIMIT = 56 * 1024 * 1024

NT_DIMS = (((1,), (1,)), ((), ()))


def _dot(a, b):
    return jnp.dot(a, b, preferred_element_type=F32)


def _dot_nt(a, b):
    return lax.dot_general(a, b, NT_DIMS, preferred_element_type=F32)


def _split_bf16(x):
    hi = x.astype(BF16)
    return hi, (x - hi.astype(F32)).astype(BF16)


def _dot_split(a_parts, b_parts):
    (a_hi, a_lo), (b_hi, b_lo) = a_parts, b_parts
    return _dot(a_hi, b_hi) + (_dot(a_hi, b_lo) + _dot(a_lo, b_hi))


def _sigmoid(x):
    return 1.0 / (1.0 + jnp.exp(-x))


def _silu(x):
    return x * _sigmoid(x)


def _row_block(tp):
    tiles = tp // LANES
    return LANES * next(d for d in ROW_TILES if tiles % d == 0)


def _params(*sem):
    return pltpu.CompilerParams(dimension_semantics=sem, vmem_limit_bytes=VMEM_LIMIT)


def _bias_kernel(table_ref, out_ref):
    row = lax.broadcasted_iota(I32, (LANES, LANES), 0)
    col = lax.broadcasted_iota(I32, (LANES, LANES), 1)
    max_exact = N_REL_BUCKETS // 2
    for kind in range(3):
        dist = col - row + (2 - kind) * LANES
        n = jnp.maximum(dist, 0)
        nf = jnp.maximum(n, 1).astype(F32)
        large = max_exact + (jnp.log(nf / max_exact) / math.log(REL_MAX_DIST / max_exact)
                             * (N_REL_BUCKETS - max_exact)).astype(I32)
        large = jnp.minimum(large, N_REL_BUCKETS - 1)
        bucket = jnp.where(n < max_exact, n, large)
        for h in range(N_ATTN_HEADS):
            tile = jnp.zeros((LANES, LANES), F32)
            for b in range(N_REL_BUCKETS):
                tile = jnp.where(bucket == b, table_ref[b, h], tile)
            far = table_ref[N_REL_BUCKETS - 1, h]
            out_ref[kind, :, h * LANES:(h + 1) * LANES] = (tile - far) * LOG2_E


def _bias_tiles(rel_table):
    return pl.pallas_call(
        _bias_kernel,
        out_shape=jax.ShapeDtypeStruct((3, LANES, N_ATTN_HEADS * LANES), F32),
        in_specs=[pl.BlockSpec(memory_space=pltpu.SMEM)],
        out_specs=pl.BlockSpec(memory_space=pltpu.VMEM),
        name="rel_bias_tiles",
    )(rel_table)


def _in_proj_kernel(h_ref, gain_ref, w_ref, wrows_ref, proj_ref, rows_ref):
    x = h_ref[...]
    y = x * lax.rsqrt(jnp.mean(x * x, axis=-1, keepdims=True) + EPS)
    hn = (y * gain_ref[...]).astype(BF16)
    proj_ref[...] = _dot(hn, w_ref[...])
    rows_ref[...] = _dot_nt(wrows_ref[...], hn)


def _in_proj(h2, gain, w_packed, w_rows, rb):
    n_rows = h2.shape[0]
    grid = (n_rows // rb,)
    return pl.pallas_call(
        _in_proj_kernel,
        out_shape=(jax.ShapeDtypeStruct((n_rows, D_PACKED), F32),
                   jax.ShapeDtypeStruct((16, n_rows), F32)),
        grid=grid,
        in_specs=[pl.BlockSpec((rb, D_MODEL), lambda i: (i, 0)),
                  pl.BlockSpec((1, D_MODEL), lambda i: (0, 0)),
                  pl.BlockSpec((D_MODEL, D_PACKED), lambda i: (0, 0)),
                  pl.BlockSpec((16, D_MODEL), lambda i: (0, 0))],
        out_specs=(pl.BlockSpec((rb, D_PACKED), lambda i: (i, 0)),
                   pl.BlockSpec((16, rb), lambda i: (0, i))),
        compiler_params=_params("parallel"),
        name="in_proj",
    )(h2, gain, w_packed, w_rows)


def _rms_rows(x, gain):
    return x * lax.rsqrt(jnp.mean(x * x, axis=-1, keepdims=True) + EPS) * gain


def _dsa_prep_kernel(sm_ref, gcq_ref, gckv_ref, wuqt_ref, wqit_ref, wukt_ref, wuvt_ref, gq_ref, gk_ref,
                     qt_ref, k_ref, vt_ref, qit_ref, kidx_ref):
    sm = sm_ref[...]
    rb = sm.shape[0]
    cq = _rms_rows(sm[:, :Q_RANK], gcq_ref[...]).astype(BF16)
    ckv = _rms_rows(sm[:, Q_RANK:Q_RANK + KV_RANK], gckv_ref[...]).astype(BF16)
    kidx_ref[...] = sm[:, Q_RANK + KV_RANK:].astype(BF16)
    q3 = _dot_nt(wuqt_ref[...], cq).reshape(N_ATTN_HEADS, ATTN_HEAD_DIM, rb)
    q3 = q3 * lax.rsqrt(jnp.mean(q3 * q3, axis=1, keepdims=True) + EPS) * gq_ref[...][None]
    qt_ref[...] = (q3 * (ATTN_HEAD_DIM ** -0.5 * LOG2_E)).reshape(D_ATTN, rb).astype(BF16)
    k3 = _dot_nt(wukt_ref[...], ckv).reshape(N_ATTN_HEADS, ATTN_HEAD_DIM, rb)
    k3 = k3 * lax.rsqrt(jnp.mean(k3 * k3, axis=1, keepdims=True) + EPS) * gk_ref[...][None]
    k_ref[...] = k3.reshape(D_ATTN, rb).T.astype(BF16)
    vt_ref[...] = _dot_nt(wuvt_ref[...], ckv).astype(BF16)
    qit_ref[...] = _dot_nt(wqit_ref[...], cq).astype(BF16)


def _dsa_prep(proj, lw, batch, tp, rb):
    n_rows = proj.shape[0]
    nb = tp // rb
    d_idx = N_IDX_HEADS * IDX_DIM
    const = lambda shape: pl.BlockSpec(shape, lambda b, i: (0, 0))
    row_spec = lambda width: pl.BlockSpec((rb, width), lambda b, i: (b * nb + i, 0))
    col_spec = lambda height: pl.BlockSpec((None, height, rb), lambda b, i: (b, 0, i))
    return pl.pallas_call(
        _dsa_prep_kernel,
        out_shape=(jax.ShapeDtypeStruct((batch, D_ATTN, tp), BF16),
                   jax.ShapeDtypeStruct((n_rows, D_ATTN), BF16),
                   jax.ShapeDtypeStruct((batch, D_ATTN, tp), BF16),
                   jax.ShapeDtypeStruct((batch, d_idx, tp), BF16),
                   jax.ShapeDtypeStruct((n_rows, LANES), BF16)),
        grid=(batch, nb),
        in_specs=[pl.BlockSpec((rb, 512), lambda b, i: (b * nb + i, COL_SMALL)),
                  const((1, Q_RANK)), const((1, KV_RANK)),
                  const((D_ATTN, Q_RANK)), const((d_idx, Q_RANK)),
                  const((D_ATTN, KV_RANK)), const((D_ATTN, KV_RANK)),
                  const((ATTN_HEAD_DIM, 1)), const((ATTN_HEAD_DIM, 1))],
        out_specs=(col_spec(D_ATTN), row_spec(D_ATTN), col_spec(D_ATTN), col_spec(d_idx), row_spec(LANES)),
        compiler_params=_params("parallel", "parallel"),
        name="dsa_prep",
    )(proj, lw["g_cq"], lw["g_ckv"], lw["w_uqt"], lw["w_qit"], lw["w_ukt"], lw["w_uvt"], lw["g_q_col"],
      lw["g_k_col"])


def _dsa_block(i, qit_ref, rows_ref, qt_ref, kidx_ref, k_ref, vt_ref, bias_ref, tri_ref, o_ref,
               score_scr, wi_scr, wq_scr, s_scr, p_scr, m_scr, l_scr, acc_scr, alpha_scr, mask_scr, tie_scr, *, topk):
    t0 = i * LANES
    n_kt = i // 2 + 1
    n_ct = (n_kt + 1) // 2
    n_st = (n_ct + 1) // 2
    hd = ATTN_HEAD_DIM
    pair_w = 2 * LANES
    n_pairs = N_ATTN_HEADS // 2

    zeros_hd = jnp.zeros((hd, LANES), BF16)
    for h in range(N_IDX_HEADS):
        wi_scr[0:IDX_DIM, h * LANES:(h + 1) * LANES] = qit_ref[h * IDX_DIM:(h + 1) * IDX_DIM, :]
    wi_scr[IDX_DIM:, :] = jnp.zeros((LANES - IDX_DIM, N_IDX_HEADS * LANES), BF16)
    for p in range(n_pairs):
        wq_scr[p, 0:hd, 0:LANES] = qt_ref[2 * p * hd:(2 * p + 1) * hd, :]
        wq_scr[p, 0:hd, LANES:] = zeros_hd
        wq_scr[p, hd:, 0:LANES] = zeros_hd
        wq_scr[p, hd:, LANES:] = qt_ref[(2 * p + 1) * hd:(2 * p + 2) * hd, :]

    row = lax.broadcasted_iota(I32, (KEY_TILE, LANES), 0)
    col = lax.broadcasted_iota(I32, (KEY_TILE, LANES), 1)
    w_idx = rows_ref[0:N_IDX_HEADS, :] * (N_IDX_HEADS ** -0.5 * IDX_DIM ** -0.5)

    def key_tile(j):
        return pl.multiple_of(j * KEY_TILE, KEY_TILE)

    def causal(j):
        return (j * KEY_TILE + row) <= (t0 + col)

    def fold8(x, op):
        return op(x.reshape(KEY_TILE // 8, 8, LANES), axis=0)

    def score_step(jc, carry, masked):
        top, bottom, n_nonneg, n_pos = carry
        subs = [jc * (COUNT_TILE // KEY_TILE) + sub for sub in range(COUNT_TILE // KEY_TILE)]
        logits = [_dot(kidx_ref[pl.ds(key_tile(jnp.minimum(j, n_kt - 1)), KEY_TILE), :], wi_scr[...])
                  for j in subs]
        for j, lg in zip(subs, logits):
            score = jnp.zeros((KEY_TILE, LANES), F32)
            for h in range(N_IDX_HEADS):
                score = score + jnp.maximum(lg[:, h * LANES:(h + 1) * LANES], 0.0) * w_idx[h:h + 1, :]
            if masked:
                visible = causal(j)
                seen = jnp.where(visible, score, -jnp.inf)
                bottom = jnp.minimum(bottom, fold8(jnp.where(visible, score, jnp.inf), jnp.min))
            else:
                seen = score
                bottom = jnp.minimum(bottom, fold8(score, jnp.min))
            top = jnp.maximum(top, fold8(seen, jnp.max))
            n_nonneg = n_nonneg + fold8(jnp.where(seen >= 0.0, 1, 0), jnp.sum)
            n_pos = n_pos + fold8(jnp.where(seen > 0.0, 1, 0), jnp.sum)
            score_scr[pl.ds(key_tile(j), KEY_TILE), :] = seen
        return top, bottom, n_nonneg, n_pos

    zeros8 = jnp.zeros((8, LANES), I32)
    carry = (jnp.full((8, LANES), -jnp.inf, F32), jnp.full((8, LANES), jnp.inf, F32), zeros8, zeros8)
    carry = lax.fori_loop(0, n_ct - 1, functools.partial(score_step, masked=False), carry)
    top, bottom, n_nonneg, n_pos = score_step(n_ct - 1, carry, masked=True)
    top = jnp.max(top, axis=0, keepdims=True)
    bottom = jnp.min(bottom, axis=0, keepdims=True)
    count0 = jnp.sum(n_nonneg, axis=0, keepdims=True)
    count_pos = jnp.sum(n_pos, axis=0, keepdims=True)

    @pl.when(n_ct % 2 == 1)
    def _():
        pad_rows = pl.ds(pl.multiple_of(n_ct * COUNT_TILE, COUNT_TILE), COUNT_TILE)
        score_scr[pad_rows, :] = jnp.full((COUNT_TILE, LANES), -jnp.inf, F32)

    def count_f32(cand, strict):
        def body(j, acc):
            for part in range(SCAN_TILE // COUNT_TILE):
                start = pl.multiple_of(j * SCAN_TILE + part * COUNT_TILE, COUNT_TILE)
                x = score_scr[pl.ds(start, COUNT_TILE), :]
                ind = jnp.where((x > cand) if strict else (x >= cand), 1, 0)
                acc = acc + jnp.sum(ind.reshape(COUNT_TILE // 8, 8, LANES), axis=0)
            return acc
        acc = lax.fori_loop(0, n_st, body, jnp.zeros((8, LANES), I32))
        return jnp.sum(acc, axis=0, keepdims=True)

    def to_pattern(v):
        bits = lax.bitcast_convert_type(v, I32)
        return bits ^ ((bits >> 31) & 0x7FFFFFFF)

    def to_f32(c):
        return lax.bitcast_convert_type(c ^ ((c >> 31) & 0x7FFFFFFF), F32)

    n_visible = t0 + 1 + lax.broadcasted_iota(I32, (1, LANES), 1)
    nonneg = count0 >= topk
    lo = jnp.where(nonneg, 0, to_pattern(bottom))
    hi = jnp.where(nonneg, to_pattern(top) + 1, 0)
    count_lo = jnp.where(nonneg, count0, n_visible)
    count_hi = jnp.where(nonneg, 0, count0)
    few = n_visible < topk
    lo = jnp.where(few, PATTERN_NEG_FLT_MAX, lo)
    zero_tied = nonneg & (count_pos < topk)
    hi = jnp.where(zero_tied, 1, hi)
    count_hi = jnp.where(zero_tied, count_pos, count_hi)
    open_q = jnp.where(few | zero_tied | (count_lo == topk), 0, 1)

    log_topk = math.log(topk)

    def count_error(count):
        return jnp.log(count.astype(F32) + 0.5) - log_topk

    def probes_left(st):
        n_pass, n_open = st[0], st[-1]
        return (n_pass < MAX_SEARCH_PASSES) & (n_open > 0)

    def probe_group(st):
        n_pass = st[0]
        n_probes = jnp.where(n_pass == 0, FIRST_CHECK_PASSES, PASSES_PER_CHECK)

        def probe(step, carry):
            lo, hi, count_lo, count_hi, err_lo, err_hi, last_side, open_q = carry
            v_lo, v_hi = to_f32(lo), to_f32(hi)
            frac = err_lo / (err_lo - err_hi)
            frac = jnp.where(count_lo - count_hi <= 4, 0.5, frac)
            guess = to_pattern(v_lo + (v_hi - v_lo) * frac)
            middle = lo + lax.shift_right_logical(hi - lo, 1)
            cand = jnp.where(n_pass + step >= INTERPOLATED_PASSES, middle, guess)
            cand = jnp.minimum(jnp.maximum(cand, lo + 1), hi - 1)
            count = count_f32(to_f32(cand), strict=False)
            raise_lo = (open_q == 1) & (count >= topk)
            lower_hi = (open_q == 1) & (count < topk)
            err = count_error(count)
            err_hi = jnp.where(raise_lo & (last_side == 1), err_hi * 0.5, err_hi)
            err_lo = jnp.where(lower_hi & (last_side == -1), err_lo * 0.5, err_lo)
            err_lo = jnp.where(raise_lo, err, err_lo)
            err_hi = jnp.where(lower_hi, err, err_hi)
            lo = jnp.where(raise_lo, cand, lo)
            count_lo = jnp.where(raise_lo, count, count_lo)
            hi = jnp.where(lower_hi, cand, hi)
            count_hi = jnp.where(lower_hi, count, count_hi)
            last_side = jnp.where(raise_lo, 1, jnp.where(lower_hi, -1, last_side))
            closed = (count_lo == topk) | (hi - lo == 1)
            return lo, hi, count_lo, count_hi, err_lo, err_hi, last_side, jnp.where(closed, 0, open_q)

        carry = lax.fori_loop(0, n_probes, probe, st[1:-1])
        return (n_pass + n_probes,) + carry + (jnp.sum(carry[-1]),)

    state = (jnp.int32(0), lo, hi, count_lo, count_hi, count_error(count_lo), count_error(count_hi),
             jnp.zeros((1, LANES), I32), open_q, jnp.sum(open_q))
    state = lax.while_loop(probes_left, probe_group, state)
    lo, count_lo, count_hi = state[1], state[3], state[4]
    tau = to_f32(lo)
    need = jnp.where((count_lo == topk) | few, topk, topk - count_hi).astype(F32)

    m_scr[...] = jnp.full(m_scr.shape, MASKED_LOGIT, F32)
    l_scr[...] = jnp.zeros(l_scr.shape, F32)
    acc_scr[...] = jnp.zeros(acc_scr.shape, F32)
    ones_rows = jnp.ones((16, KEY_TILE), BF16)
    last = n_kt - 1

    def mask_pair(j_first, parity):
        xs = [score_scr[pl.ds(key_tile(jnp.minimum(j_first + slot, last)), KEY_TILE), :] for slot in range(2)]
        ties = [x == tau for x in xs]
        tie_cols = jnp.concatenate([jnp.where(tie, 1.0, 0.0).astype(BF16) for tie in ties], axis=1)
        ranks = _dot(tri_ref[...], tie_cols)
        tie_carry = tie_scr[0:1, :]
        for slot, (x, tie) in enumerate(zip(xs, ties)):
            rank = ranks[:, slot * LANES:(slot + 1) * LANES] + tie_carry
            tie_carry = rank[KEY_TILE - 1:KEY_TILE, :]
            take = (tie & (rank <= need)) | (x > tau)
            mask_scr[parity, slot] = jnp.where(take, 0.0, MASKED_LOGIT)
        tie_scr[0:1, :] = tie_carry

    def qk_pair(slot, j, p):
        s_scr[slot, :, p * pair_w:(p + 1) * pair_w] = _dot(
            k_ref[pl.ds(key_tile(j), KEY_TILE), p * LANES:(p + 1) * LANES], wq_scr[p])

    def softmax_pair(slot, j, p, near, parity):
        mask_add = mask_scr[parity, slot]
        alphas = []
        for h in (2 * p, 2 * p + 1):
            cols = slice(h * LANES, (h + 1) * LANES)
            logits = s_scr[slot, :, cols] + mask_add
            if near:
                kind_top = jnp.clip(2 * j - i + 2, 0, 2)
                kind_bot = jnp.clip(2 * j - i + 3, 0, 2)
                logits = logits + jnp.concatenate(
                    [bias_ref[kind_top, :, cols], bias_ref[kind_bot, :, cols]], axis=0)
            m_old = m_scr[h:h + 1, :]
            m_new = jnp.maximum(m_old, jnp.max(logits, axis=0, keepdims=True))
            m_scr[h:h + 1, :] = m_new
            p_scr[slot, :, cols] = jnp.exp2(logits - m_new).astype(BF16)
            alphas.append(jnp.exp2(m_old - m_new))
        return alphas

    def pv_pair(slot, j, p, alphas):
        lhs = jnp.concatenate([vt_ref[p * 2 * hd:(p + 1) * 2 * hd, pl.ds(key_tile(j), KEY_TILE)], ones_rows],
                              axis=0)
        out = _dot(lhs, p_scr[slot, :, p * pair_w:(p + 1) * pair_w])
        for half in range(2):
            h = 2 * p + half
            rows_h = slice(h * hd, (h + 1) * hd)
            q_cols = slice(half * LANES, (half + 1) * LANES)
            acc_scr[rows_h, :] = acc_scr[rows_h, :] * alphas[half] + out[half * hd:(half + 1) * hd, q_cols]
            l_scr[h:h + 1, :] = l_scr[h:h + 1, :] * alphas[half] + out[2 * hd:2 * hd + 1, q_cols]

    def pending_alphas(p):
        return [alpha_scr[h:h + 1, :] for h in (2 * p, 2 * p + 1)]

    def clear_pending():
        p_scr[1] = jnp.zeros(p_scr.shape[1:], BF16)
        alpha_scr[...] = jnp.ones(alpha_scr.shape, F32)

    def pair_step(ja, j_pending, j_next, near, parity):
        alphas_a = []
        for p in range(n_pairs):
            pv_pair(1, j_pending, p, pending_alphas(p))
            qk_pair(1, ja + 1, p)
            alphas_a.append(softmax_pair(0, ja, p, near, parity))
        for p in range(n_pairs):
            pv_pair(0, ja, p, alphas_a[p])
            qk_pair(0, j_next, p)
            alphas_b = softmax_pair(1, ja + 1, p, near, parity)
            for half in range(2):
                alpha_scr[2 * p + half:2 * p + half + 1, :] = alphas_b[half]
        mask_pair(ja + 2, 1 - parity)

    def single_step(ja, j_pending, near, parity):
        alphas_a = []
        for p in range(n_pairs):
            pv_pair(1, j_pending, p, pending_alphas(p))
            alphas_a.append(softmax_pair(0, ja, p, near, parity))
        for p in range(n_pairs):
            pv_pair(0, ja, p, alphas_a[p])
        clear_pending()

    n_far = 2 * (jnp.maximum(n_kt - 2, 0) // 2)
    n_near = n_kt - n_far
    first_near_parity = (n_far // 2) % 2
    clear_pending()
    tie_scr[...] = jnp.zeros(tie_scr.shape, F32)
    mask_pair(0, 0)
    for p in range(n_pairs):
        qk_pair(0, 0, p)

    def far_body(jp, carry):
        pair_step(2 * jp, jnp.maximum(2 * jp - 1, 0), 2 * jp + 2, near=False, parity=jp % 2)
        return carry

    lax.fori_loop(0, n_far // 2, far_body, 0)

    @pl.when(n_near >= 2)
    def _():
        pair_step(n_far, jnp.maximum(n_far - 1, 0), jnp.minimum(n_far + 2, last), near=True,
                  parity=first_near_parity)

    @pl.when(n_near % 2 == 1)
    def _():
        single_step(last, jnp.where(n_near == 3, n_far + 1, jnp.maximum(n_far - 1, 0)), near=True,
                    parity=jnp.where(n_near == 3, 1 - first_near_parity, first_near_parity))

    for p in range(n_pairs):
        pv_pair(1, last, p, pending_alphas(p))

    for h in range(N_ATTN_HEADS):
        rows_h = slice(h * hd, (h + 1) * hd)
        acc_scr[rows_h, :] = acc_scr[rows_h, :] / l_scr[h:h + 1, :]
    o_ref[...] = acc_scr[...].T


def _dsa_kernel(*refs, topk, t_valid):
    o_ref = refs[8]
    i = pl.program_id(1)
    is_real = i * LANES < t_valid

    @pl.when(is_real)
    def _():
        _dsa_block(i, *refs, topk=topk)

    @pl.when(jnp.logical_not(is_real))
    def _():
        o_ref[...] = jnp.zeros(o_ref.shape, F32)


def _dsa_attention(qit, rows, qt, kidx, k, vt, bias, tri, batch, tp, t_valid, topk):
    n_rows = k.shape[0]
    nqb = tp // LANES
    d_idx = N_IDX_HEADS * IDX_DIM
    n_pairs = N_ATTN_HEADS // 2
    key_rows = -(-tp // SCAN_TILE) * SCAN_TILE
    q_cols = lambda height: pl.BlockSpec((None, height, LANES), lambda b, i: (b, 0, i))
    return pl.pallas_call(
        functools.partial(_dsa_kernel, topk=topk, t_valid=t_valid),
        out_shape=jax.ShapeDtypeStruct((n_rows, D_ATTN), F32),
        grid=(batch, nqb),
        in_specs=[q_cols(d_idx),
                  pl.BlockSpec((16, LANES), lambda b, i: (0, b * nqb + i)),
                  q_cols(D_ATTN),
                  pl.BlockSpec((tp, LANES), lambda b, i: (b, 0)),
                  pl.BlockSpec((tp, D_ATTN), lambda b, i: (b, 0)),
                  pl.BlockSpec((None, D_ATTN, tp), lambda b, i: (b, 0, 0)),
                  pl.BlockSpec((3, LANES, N_ATTN_HEADS * LANES), lambda b, i: (0, 0, 0)),
                  pl.BlockSpec((KEY_TILE, KEY_TILE), lambda b, i: (0, 0))],
        out_specs=pl.BlockSpec((LANES, D_ATTN), lambda b, i: (b * nqb + i, 0)),
        scratch_shapes=[pltpu.VMEM((key_rows, LANES), F32),
                        pltpu.VMEM((LANES, N_IDX_HEADS * LANES), BF16),
                        pltpu.VMEM((n_pairs, LANES, 2 * LANES), BF16),
                        pltpu.VMEM((2, KEY_TILE, N_ATTN_HEADS * LANES), F32),
                        pltpu.VMEM((2, KEY_TILE, N_ATTN_HEADS * LANES), BF16),
                        pltpu.VMEM((N_ATTN_HEADS, LANES), F32),
                        pltpu.VMEM((N_ATTN_HEADS, LANES), F32),
                        pltpu.VMEM((D_ATTN, LANES), F32),
                        pltpu.VMEM((N_ATTN_HEADS, LANES), F32),
                        pltpu.VMEM((2, 2, KEY_TILE, LANES), F32),
                        pltpu.VMEM((8, LANES), F32)],
        compiler_params=_params("parallel", "parallel"),
        name="dsa_attention",
    )(qit, rows, qt, kidx, k, vt, bias, tri)


def _gdn_prep_kernel(x_ref, halo_ref, cw_ref, rows_ref, alog_ref, dtb_ref, q_ref, k_ref, v_ref, bg_ref, buf):
    first = pl.program_id(1) == 0
    rb = x_ref.shape[0]
    buf[0:HALO_ROWS, :] = jnp.where(first, 0.0, halo_ref[...])
    buf[HALO_ROWS:, :] = x_ref[...]
    acc = jnp.zeros((rb, 3 * D_GDN), F32)
    for tap in range(CONV_WIDTH):
        start = HALO_ROWS - (CONV_WIDTH - 1) + tap
        acc = acc + cw_ref[tap:tap + 1, :] * buf[start:start + rb, :]
    y = _silu(acc)
    for h in range(N_GDN_HEADS):
        cols = slice(h * GDN_HEAD_DIM, (h + 1) * GDN_HEAD_DIM)
        qh = y[:, cols]
        kh = y[:, D_GDN + h * GDN_HEAD_DIM:D_GDN + (h + 1) * GDN_HEAD_DIM]
        q_ref[:, cols] = (qh * lax.rsqrt(jnp.sum(qh * qh, axis=-1, keepdims=True) + EPS)
                          * (GDN_HEAD_DIM ** -0.5))
        k_ref[:, cols] = kh * lax.rsqrt(jnp.sum(kh * kh, axis=-1, keepdims=True) + EPS)
    v_ref[...] = y[:, 2 * D_GDN:]
    rows = rows_ref[...]
    beta = _sigmoid(rows[4:8, :])
    a = rows[8:12, :] + dtb_ref[...]
    softplus = jnp.maximum(a, 0.0) + jnp.log1p(jnp.exp(-jnp.abs(a)))
    bg_ref[0:4, :] = beta
    bg_ref[4:8, :] = -jnp.exp(alog_ref[...]) * softplus


def _gdn_prep(proj, rows, lw, batch, tp, rb):
    n_rows = proj.shape[0]
    nb = tp // rb
    halo_per_block = rb // HALO_ROWS
    row_spec = pl.BlockSpec((rb, D_GDN), lambda b, i: (b * nb + i, 0))
    return pl.pallas_call(
        _gdn_prep_kernel,
        out_shape=(jax.ShapeDtypeStruct((n_rows, D_GDN), F32),) * 3
        + (jax.ShapeDtypeStruct((8, n_rows), F32),),
        grid=(batch, nb),
        in_specs=[pl.BlockSpec((rb, 3 * D_GDN), lambda b, i: (b * nb + i, 0)),
                  pl.BlockSpec((HALO_ROWS, 3 * D_GDN),
                               lambda b, i: (jnp.maximum((b * nb + i) * halo_per_block - 1, 0), 0)),
                  pl.BlockSpec((CONV_WIDTH, 3 * D_GDN), lambda b, i: (0, 0)),
                  pl.BlockSpec((16, rb), lambda b, i: (0, b * nb + i)),
                  pl.BlockSpec((N_GDN_HEADS, 1), lambda b, i: (0, 0)),
                  pl.BlockSpec((N_GDN_HEADS, 1), lambda b, i: (0, 0))],
        out_specs=(row_spec, row_spec, row_spec,
                   pl.BlockSpec((8, rb), lambda b, i: (0, b * nb + i))),
        scratch_shapes=[pltpu.VMEM((HALO_ROWS + rb, 3 * D_GDN), F32)],
        compiler_params=_params("parallel", "parallel"),
        name="gdn_prep",
    )(proj, proj, lw["conv_w"], rows, lw["a_log"], lw["dt_bias"])


def _gdn_chunk_kernel(q_ref, k_ref, v_ref, bg_ref, m_ref, n_ref, p_ref, r_ref, cd_ref):
    c = LANES
    n_chunks = q_ref.shape[0] // c
    items = [(ch, h) for ch in range(n_chunks) for h in range(N_GDN_HEADS)]
    idx = range(len(items))
    row = lax.broadcasted_iota(I32, (c, c), 0)
    col = lax.broadcasted_iota(I32, (c, c), 1)
    tri = row >= col
    strict = row > col
    eye = jnp.where(row == col, 1.0, 0.0)
    lane8 = lax.broadcasted_iota(I32, (8, c), 1)
    gates, decays = [], []
    for ch in range(n_chunks):
        bg = bg_ref[:, ch * c:(ch + 1) * c]
        dec = bg
        shift = 1
        while shift < c:
            dec = dec + jnp.where(lane8 >= shift, pltpu.roll(dec, shift, 1), 0.0)
            shift *= 2
        gates.append(bg)
        decays.append(dec)

    def tokens(ref, n):
        ch, h = items[n]
        return ref[ch * c:(ch + 1) * c, h * GDN_HEAD_DIM:(h + 1) * GDN_HEAD_DIM]

    d_row = [jnp.broadcast_to(decays[ch][4 + h:5 + h, :], (c, c)) for ch, h in items]
    d_col = [d.T for d in d_row]
    beta_col = [jnp.broadcast_to(gates[ch][h:h + 1, :], (c, c)).T for ch, h in items]
    d_last = [d[:, c - 1:c] for d in d_row]
    gamma = [jnp.exp(jnp.where(tri, d_col[n] - d_row[n], MASKED_LOGIT)) for n in idx]
    exp_d = [jnp.exp(d_col[n]) for n in idx]
    k16 = [tokens(k_ref, n).astype(BF16) for n in idx]
    kb = [tokens(k_ref, n) * beta_col[n] for n in idx]
    nil = [jnp.where(strict, _dot_nt(kb[n].astype(BF16), k16[n]) * gamma[n], 0.0) for n in idx]
    nil_parts = [_split_bf16(x) for x in nil]
    inv = [eye - x for x in nil]
    power = [_dot_split(x, x) for x in nil_parts]
    steps = int(math.log2(c)) - 1
    for it in range(steps):
        power_parts = [_split_bf16(x) for x in power]
        inv = [inv[n] + _dot_split(_split_bf16(inv[n]), power_parts[n]) for n in idx]
        if it + 1 < steps:
            power = [_dot_split(x, x) for x in power_parts]
    rhs = [jnp.concatenate([kb[n] * exp_d[n], tokens(v_ref, n) * beta_col[n]], axis=1) for n in idx]
    wu = [_dot_split(_split_bf16(inv[n]), _split_bf16(rhs[n])).astype(BF16) for n in idx]
    aqk = [jnp.where(tri, _dot_nt(tokens(q_ref, n).astype(BF16), k16[n]) * gamma[n], 0.0).astype(BF16)
           for n in idx]
    kd_t = [(tokens(k_ref, n) * jnp.exp(d_last[n] - d_col[n])).T.astype(BF16) for n in idx]
    state_wu = [_dot(kd_t[n], wu[n]) for n in idx]
    out_wu = [_dot(aqk[n], wu[n]) for n in idx]
    for n, (ch, h) in enumerate(items):
        m_ref[ch, h] = (-state_wu[n][:, :c]).astype(BF16)
        n_ref[ch, h] = state_wu[n][:, c:]
        p_ref[ch, h] = (tokens(q_ref, n) * exp_d[n] - out_wu[n][:, :c]).astype(BF16)
        r_ref[ch, h] = out_wu[n][:, c:]
        cd_ref[ch, h:h + 1, :] = jnp.exp(d_last[n][0:1, :] + jnp.zeros((1, c), F32))


def _gdn_chunks(qn, kn, vv, bg):
    n_rows = qn.shape[0]
    nc = n_rows // LANES
    per_step = GDN_CHUNKS_PER_STEP
    tok = pl.BlockSpec((per_step * LANES, D_GDN), lambda c: (c, 0))
    mat = pl.BlockSpec((per_step, N_GDN_HEADS, LANES, LANES), lambda c: (c, 0, 0, 0))
    mat_shape = lambda dt: jax.ShapeDtypeStruct((nc, N_GDN_HEADS, LANES, LANES), dt)
    return pl.pallas_call(
        _gdn_chunk_kernel,
        out_shape=(mat_shape(BF16), mat_shape(F32), mat_shape(BF16), mat_shape(F32),
                   jax.ShapeDtypeStruct((nc, N_GDN_HEADS, LANES), F32)),
        grid=(nc // per_step,),
        in_specs=[tok, tok, tok, pl.BlockSpec((8, per_step * LANES), lambda c: (0, c))],
        out_specs=(mat, mat, mat, mat, pl.BlockSpec((per_step, N_GDN_HEADS, LANES), lambda c: (c, 0, 0))),
        compiler_params=_params("parallel"),
        name="gdn_chunks",
    )(qn, kn, vv, bg)


def _gdn_scan_kernel(m_ref, n_ref, p_ref, r_ref, cd_ref, o_ref, s_scr, *, batch):
    @pl.when(pl.program_id(0) == 0)
    def _():
        s_scr[...] = jnp.zeros(s_scr.shape, F32)

    for b in range(batch):
        for h in range(N_GDN_HEADS):
            s = s_scr[b, h]
            s16 = s.astype(BF16)
            o_ref[b, :, h * GDN_HEAD_DIM:(h + 1) * GDN_HEAD_DIM] = _dot(p_ref[b, h], s16) + r_ref[b, h]
            s_scr[b, h] = s * cd_ref[b, h:h + 1, :] + _dot(m_ref[b, h], s16) + n_ref[b, h]


def _gdn_scan(m_mat, n_mat, p_mat, r_mat, cd, batch, tp):
    nc = tp // LANES
    shape5 = lambda a: a.reshape(batch, nc, N_GDN_HEADS, LANES, LANES)
    mat = pl.BlockSpec((batch, None, N_GDN_HEADS, LANES, LANES), lambda c: (0, c, 0, 0, 0))
    return pl.pallas_call(
        functools.partial(_gdn_scan_kernel, batch=batch),
        out_shape=jax.ShapeDtypeStruct((batch, tp, D_GDN), F32),
        grid=(nc,),
        in_specs=[mat, mat, mat, mat,
                  pl.BlockSpec((batch, None, N_GDN_HEADS, LANES), lambda c: (0, c, 0, 0))],
        out_specs=pl.BlockSpec((batch, LANES, D_GDN), lambda c: (0, c, 0)),
        scratch_shapes=[pltpu.VMEM((batch, N_GDN_HEADS, LANES, LANES), F32)],
        compiler_params=_params("arbitrary"),
        name="gdn_scan",
    )(shape5(m_mat), shape5(n_mat), shape5(p_mat), shape5(r_mat), cd.reshape(batch, nc, N_GDN_HEADS, LANES))


def _out_proj_kernel(h_ref, oa_ref, za_ref, og_ref, zg_ref, gain_ref, w_ref, out_ref):
    attn = (oa_ref[...] * _silu(za_ref[...])).astype(BF16)
    y = _dot(attn, w_ref[0:D_ATTN, :])
    og, zg = og_ref[...], zg_ref[...]
    for h in range(N_GDN_HEADS):
        cols = slice(h * GDN_HEAD_DIM, (h + 1) * GDN_HEAD_DIM)
        gated = (_rms_rows(og[:, cols], gain_ref[...]) * _silu(zg[:, cols])).astype(BF16)
        y = y + _dot(gated, w_ref[D_ATTN + h * GDN_HEAD_DIM:D_ATTN + (h + 1) * GDN_HEAD_DIM, :])
    out_ref[...] = h_ref[...] + y


def _out_proj(h2, o_attn, proj, o_gdn, gain, w_out, rb):
    n_rows = h2.shape[0]
    blk = lambda width, col: pl.BlockSpec((rb, width), lambda i: (i, col))
    return pl.pallas_call(
        _out_proj_kernel,
        out_shape=jax.ShapeDtypeStruct((n_rows, D_MODEL), F32),
        grid=(n_rows // rb,),
        in_specs=[blk(D_MODEL, 0), blk(D_ATTN, 0), blk(D_ATTN, COL_Z_ATTN), blk(D_GDN, 0),
                  blk(D_GDN, COL_Z_GDN),
                  pl.BlockSpec((1, GDN_HEAD_DIM), lambda i: (0, 0)),
                  pl.BlockSpec((D_MODEL, D_MODEL), lambda i: (0, 0))],
        out_specs=blk(D_MODEL, 0),
        compiler_params=_params("parallel"),
        name="out_proj",
    )(h2, o_attn, proj, o_gdn, proj, gain, w_out)


def _pack_layer(norm_gain, w_in, cq_gain, ckv_gain, w_uq, w_ukv, w_q_idx, q_gain, k_gain, conv_w, a_log,
                dt_bias, gdn_gain, w_out):
    o = 0
    parts = {}
    for name, size in (("c_q", Q_RANK), ("c_kv", KV_RANK), ("k_idx", IDX_DIM), ("w_idx", N_IDX_HEADS),
                       ("z_attn", D_ATTN), ("qkv_g", 3 * D_GDN), ("z_g", D_GDN), ("b", N_GDN_HEADS),
                       ("a", N_GDN_HEADS)):
        parts[name] = w_in[:, o:o + size]
        o += size
    w_packed = jnp.concatenate([parts["qkv_g"], parts["c_q"], parts["c_kv"], parts["k_idx"], parts["k_idx"],
                                parts["z_attn"], parts["z_g"]], axis=1).astype(BF16)
    w_rows = jnp.concatenate([parts["w_idx"], parts["b"], parts["a"],
                              jnp.zeros((D_MODEL, 4), w_in.dtype)], axis=1).T.astype(BF16)
    return dict(
        gain=norm_gain[None, :], w_packed=w_packed, w_rows=w_rows,
        g_cq=cq_gain[None, :], g_ckv=ckv_gain[None, :],
        w_uqt=w_uq.T.astype(BF16), w_qit=w_q_idx.T.astype(BF16),
        w_ukt=w_ukv[:, :D_ATTN].T.astype(BF16), w_uvt=w_ukv[:, D_ATTN:].T.astype(BF16),
        g_q_col=q_gain[:, None], g_k_col=k_gain[:, None],
        conv_w=conv_w, a_log=a_log[:, None], dt_bias=dt_bias[:, None],
        gdn_gain=gdn_gain[None, :], w_out=w_out.astype(BF16))


def _layer(h2, lw, bias, tri, batch, tp, t_valid, topk):
    rb = _row_block(tp)
    proj, rows = _in_proj(h2, lw["gain"], lw["w_packed"], lw["w_rows"], rb)
    qt, k, vt, qit, kidx = _dsa_prep(proj, lw, batch, tp, rb)
    o_attn = _dsa_attention(qit, rows, qt, kidx, k, vt, bias, tri, batch, tp, t_valid, topk)
    qn, kn, vv, bg = _gdn_prep(proj, rows, lw, batch, tp, rb)
    m_mat, n_mat, p_mat, r_mat, cd = _gdn_chunks(qn, kn, vv, bg)
    o_gdn = _gdn_scan(m_mat, n_mat, p_mat, r_mat, cd, batch, tp).reshape(batch * tp, D_GDN)
    return _out_proj(h2, o_attn, proj, o_gdn, lw["gdn_gain"], lw["w_out"], rb)


def _forward(x, meta_tokens, rel_bias_table, layer_weights, topk):
    batch, seq, _ = x.shape
    t = seq + N_META
    tp = -(-t // KEY_TILE) * KEY_TILE
    meta = jnp.broadcast_to(meta_tokens[None].astype(x.dtype), (batch, N_META, D_MODEL))
    h = jnp.concatenate([meta, x, jnp.zeros((batch, tp - t, D_MODEL), x.dtype)], axis=1)
    h2 = h.reshape(batch * tp, D_MODEL)
    bias = _bias_tiles(rel_bias_table)
    tri = jnp.tril(jnp.ones((KEY_TILE, KEY_TILE), BF16))
    for lw in layer_weights:
        h2 = _layer(h2, lw, bias, tri, batch, tp, t, topk)
    return h2.reshape(batch, tp, D_MODEL)[:, N_META:t]


def kernel(x, meta_tokens, rel_bias_table, norm_gain, w_in, cq_norm_gain, ckv_norm_gain, w_uq, w_ukv, w_q_idx,
           q_norm_gain, k_norm_gain, conv_w, a_log, dt_bias, gdn_norm_gain, w_out):
    depth = norm_gain.shape[0]
    topk = min(TOPK_MAX, x.shape[1] // 4)
    layers = [_pack_layer(norm_gain[l], w_in[l], cq_norm_gain[l], ckv_norm_gain[l], w_uq[l], w_ukv[l],
                          w_q_idx[l], q_norm_gain[l], k_norm_gain[l], conv_w[l], a_log[l], dt_bias[l],
                          gdn_norm_gain[l], w_out[l]) for l in range(depth)]
    return _forward(x, meta_tokens, rel_bias_table, layers, topk)
```

```python
import functools
import math

import jax
import jax.numpy as jnp
from jax import lax
from jax.experimental import pallas as pl
from jax.experimental.pallas import tpu as pltpu

F32 = jnp.float32
BF16 = jnp.bfloat16
I32 = jnp.int32

D_MODEL = 1024
N_META = 16
EPS = 1e-6
N_ATTN_HEADS = 8
ATTN_HEAD_DIM = 64
D_ATTN = N_ATTN_HEADS * ATTN_HEAD_DIM
Q_RANK = 256
KV_RANK = 128
N_IDX_HEADS = 4
IDX_DIM = 64
TOPK_MAX = 256
N_REL_BUCKETS = 32
REL_MAX_DIST = 128
N_GDN_HEADS = 4
GDN_HEAD_DIM = 128
D_GDN = N_GDN_HEADS * GDN_HEAD_DIM
CONV_WIDTH = 4

LANES = 128
KEY_TILE = 256
COUNT_TILE = 512
SCAN_TILE = 1024
FIRST_CHECK_PASSES = 8
PASSES_PER_CHECK = 2
INTERPOLATED_PASSES = 24
MAX_SEARCH_PASSES = INTERPOLATED_PASSES + 32
GDN_CHUNKS_PER_STEP = 2
ROW_TILES = (5, 4, 3, 2, 1)
HALO_ROWS = 8
D_PACKED = 3 * D_GDN + 512 + D_ATTN + D_GDN
COL_SMALL = 3
COL_Z_ATTN = 4
COL_Z_GDN = 5
MASKED_LOGIT = -1e30
LOG2_E = math.log2(math.e)
KEY_MIN = -2 ** 31
PATTERN_NEG_FLT_MAX = KEY_MIN + (1 << 23)
VMEM_LIMIT = 56 * 1024 * 1024

NT_DIMS = (((1,), (1,)), ((), ()))


def _dot(a, b):
    return jnp.dot(a, b, preferred_element_type=F32)


def _dot_nt(a, b):
    return lax.dot_general(a, b, NT_DIMS, preferred_element_type=F32)


def _split_bf16(x):
    hi = x.astype(BF16)
    return hi, (x - hi.astype(F32)).astype(BF16)


def _dot_split(a_parts, b_parts):
    (a_hi, a_lo), (b_hi, b_lo) = a_parts, b_parts
    return _dot(a_hi, b_hi) + (_dot(a_hi, b_lo) + _dot(a_lo, b_hi))


def _sigmoid(x):
    return 1.0 / (1.0 + jnp.exp(-x))


def _silu(x):
    return x * _sigmoid(x)


def _row_block(tp):
    tiles = tp // LANES
    return LANES * next(d for d in ROW_TILES if tiles % d == 0)


def _params(*sem):
    return pltpu.CompilerParams(dimension_semantics=sem, vmem_limit_bytes=VMEM_LIMIT)


def _bias_kernel(table_ref, out_ref):
    row = lax.broadcasted_iota(I32, (LANES, LANES), 0)
    col = lax.broadcasted_iota(I32, (LANES, LANES), 1)
    max_exact = N_REL_BUCKETS // 2
    for kind in range(3):
        dist = col - row + (2 - kind) * LANES
        n = jnp.maximum(dist, 0)
        nf = jnp.maximum(n, 1).astype(F32)
        large = max_exact + (jnp.log(nf / max_exact) / math.log(REL_MAX_DIST / max_exact)
                             * (N_REL_BUCKETS - max_exact)).astype(I32)
        large = jnp.minimum(large, N_REL_BUCKETS - 1)
        bucket = jnp.where(n < max_exact, n, large)
        for h in range(N_ATTN_HEADS):
            tile = jnp.zeros((LANES, LANES), F32)
            for b in range(N_REL_BUCKETS):
                tile = jnp.where(bucket == b, table_ref[b, h], tile)
            far = table_ref[N_REL_BUCKETS - 1, h]
            out_ref[kind, :, h * LANES:(h + 1) * LANES] = (tile - far) * LOG2_E


def _bias_tiles(rel_table):
    return pl.pallas_call(
        _bias_kernel,
        out_shape=jax.ShapeDtypeStruct((3, LANES, N_ATTN_HEADS * LANES), F32),
        in_specs=[pl.BlockSpec(memory_space=pltpu.SMEM)],
        out_specs=pl.BlockSpec(memory_space=pltpu.VMEM),
        name="rel_bias_tiles",
    )(rel_table)


def _in_proj_kernel(h_ref, gain_ref, w_ref, wrows_ref, proj_ref, rows_ref):
    x = h_ref[...]
    y = x * lax.rsqrt(jnp.mean(x * x, axis=-1, keepdims=True) + EPS)
    hn = (y * gain_ref[...]).astype(BF16)
    proj_ref[...] = _dot(hn, w_ref[...])
    rows_ref[...] = _dot_nt(wrows_ref[...], hn)


def _in_proj(h2, gain, w_packed, w_rows, rb):
    n_rows = h2.shape[0]
    grid = (n_rows // rb,)
    return pl.pallas_call(
        _in_proj_kernel,
        out_shape=(jax.ShapeDtypeStruct((n_rows, D_PACKED), F32),
                   jax.ShapeDtypeStruct((16, n_rows), F32)),
        grid=grid,
        in_specs=[pl.BlockSpec((rb, D_MODEL), lambda i: (i, 0)),
                  pl.BlockSpec((1, D_MODEL), lambda i: (0, 0)),
                  pl.BlockSpec((D_MODEL, D_PACKED), lambda i: (0, 0)),
                  pl.BlockSpec((16, D_MODEL), lambda i: (0, 0))],
        out_specs=(pl.BlockSpec((rb, D_PACKED), lambda i: (i, 0)),
                   pl.BlockSpec((16, rb), lambda i: (0, i))),
        compiler_params=_params("parallel"),
        name="in_proj",
    )(h2, gain, w_packed, w_rows)


def _rms_rows(x, gain):
    return x * lax.rsqrt(jnp.mean(x * x, axis=-1, keepdims=True) + EPS) * gain


def _dsa_prep_kernel(sm_ref, gcq_ref, gckv_ref, wuqt_ref, wqit_ref, wukt_ref, wuvt_ref, gq_ref, gk_ref,
                     qt_ref, k_ref, vt_ref, qit_ref, kidx_ref):
    sm = sm_ref[...]
    rb = sm.shape[0]
    cq = _rms_rows(sm[:, :Q_RANK], gcq_ref[...]).astype(BF16)
    ckv = _rms_rows(sm[:, Q_RANK:Q_RANK + KV_RANK], gckv_ref[...]).astype(BF16)
    kidx_ref[...] = sm[:, Q_RANK + KV_RANK:].astype(BF16)
    q3 = _dot_nt(wuqt_ref[...], cq).reshape(N_ATTN_HEADS, ATTN_HEAD_DIM, rb)
    q3 = q3 * lax.rsqrt(jnp.mean(q3 * q3, axis=1, keepdims=True) + EPS) * gq_ref[...][None]
    qt_ref[...] = (q3 * (ATTN_HEAD_DIM ** -0.5 * LOG2_E)).reshape(D_ATTN, rb).astype(BF16)
    k3 = _dot_nt(wukt_ref[...], ckv).reshape(N_ATTN_HEADS, ATTN_HEAD_DIM, rb)
    k3 = k3 * lax.rsqrt(jnp.mean(k3 * k3, axis=1, keepdims=True) + EPS) * gk_ref[...][None]
    k_ref[...] = k3.reshape(D_ATTN, rb).T.astype(BF16)
    vt_ref[...] = _dot_nt(wuvt_ref[...], ckv).astype(BF16)
    qit_ref[...] = _dot_nt(wqit_ref[...], cq).astype(BF16)


def _dsa_prep(proj, lw, batch, tp, rb):
    n_rows = proj.shape[0]
    nb = tp // rb
    d_idx = N_IDX_HEADS * IDX_DIM
    const = lambda shape: pl.BlockSpec(shape, lambda b, i: (0, 0))
    row_spec = lambda width: pl.BlockSpec((rb, width), lambda b, i: (b * nb + i, 0))
    col_spec = lambda height: pl.BlockSpec((None, height, rb), lambda b, i: (b, 0, i))
    return pl.pallas_call(
        _dsa_prep_kernel,
        out_shape=(jax.ShapeDtypeStruct((batch, D_ATTN, tp), BF16),
                   jax.ShapeDtypeStruct((n_rows, D_ATTN), BF16),
                   jax.ShapeDtypeStruct((batch, D_ATTN, tp), BF16),
                   jax.ShapeDtypeStruct((batch, d_idx, tp), BF16),
                   jax.ShapeDtypeStruct((n_rows, LANES), BF16)),
        grid=(batch, nb),
        in_specs=[pl.BlockSpec((rb, 512), lambda b, i: (b * nb + i, COL_SMALL)),
                  const((1, Q_RANK)), const((1, KV_RANK)),
                  const((D_ATTN, Q_RANK)), const((d_idx, Q_RANK)),
                  const((D_ATTN, KV_RANK)), const((D_ATTN, KV_RANK)),
                  const((ATTN_HEAD_DIM, 1)), const((ATTN_HEAD_DIM, 1))],
        out_specs=(col_spec(D_ATTN), row_spec(D_ATTN), col_spec(D_ATTN), col_spec(d_idx), row_spec(LANES)),
        compiler_params=_params("parallel", "parallel"),
        name="dsa_prep",
    )(proj, lw["g_cq"], lw["g_ckv"], lw["w_uqt"], lw["w_qit"], lw["w_ukt"], lw["w_uvt"], lw["g_q_col"],
      lw["g_k_col"])


def _dsa_block(i, qit_ref, rows_ref, qt_ref, kidx_ref, k_ref, vt_ref, bias_ref, tri_ref, o_ref,
               score_scr, wi_scr, wq_scr, s_scr, p_scr, m_scr, l_scr, acc_scr, alpha_scr, mask_scr, tie_scr, *, topk):
    t0 = i * LANES
    n_kt = i // 2 + 1
    n_st = (n_kt + 3) // 4
    hd = ATTN_HEAD_DIM
    pair_w = 2 * LANES
    n_pairs = N_ATTN_HEADS // 2

    zeros_hd = jnp.zeros((hd, LANES), BF16)
    for h in range(N_IDX_HEADS):
        wi_scr[0:IDX_DIM, h * LANES:(h + 1) * LANES] = qit_ref[h * IDX_DIM:(h + 1) * IDX_DIM, :]
    wi_scr[IDX_DIM:, :] = jnp.zeros((LANES - IDX_DIM, N_IDX_HEADS * LANES), BF16)
    for p in range(n_pairs):
        wq_scr[p, 0:hd, 0:LANES] = qt_ref[2 * p * hd:(2 * p + 1) * hd, :]
        wq_scr[p, 0:hd, LANES:] = zeros_hd
        wq_scr[p, hd:, 0:LANES] = zeros_hd
        wq_scr[p, hd:, LANES:] = qt_ref[(2 * p + 1) * hd:(2 * p + 2) * hd, :]

    row = lax.broadcasted_iota(I32, (KEY_TILE, LANES), 0)
    col = lax.broadcasted_iota(I32, (KEY_TILE, LANES), 1)
    w_idx = rows_ref[0:N_IDX_HEADS, :] * (N_IDX_HEADS ** -0.5 * IDX_DIM ** -0.5)

    def key_tile(j):
        return pl.multiple_of(j * KEY_TILE, KEY_TILE)

    def causal(j):
        return (j * KEY_TILE + row) <= (t0 + col)

    def fold8(x, op):
        return op(x.reshape(KEY_TILE // 8, 8, LANES), axis=0)

    def score_step(jc, carry, masked):
        top, bottom, n_nonneg, n_pos = carry
        subs = [jc * (SCAN_TILE // KEY_TILE) + sub for sub in range(SCAN_TILE // KEY_TILE)]
        logits = [_dot(kidx_ref[pl.ds(key_tile(jnp.minimum(j, n_kt - 1)), KEY_TILE), :], wi_scr[...])
                  for j in subs]
        for j, lg in zip(subs, logits):
            score = jnp.zeros((KEY_TILE, LANES), F32)
            for h in range(N_IDX_HEADS):
                score = score + jnp.maximum(lg[:, h * LANES:(h + 1) * LANES], 0.0) * w_idx[h:h + 1, :]
            if masked:
                visible = causal(j)
                seen = jnp.where(visible, score, -jnp.inf)
                bottom = jnp.minimum(bottom, fold8(jnp.where(visible, score, jnp.inf), jnp.min))
            else:
                seen = score
                bottom = jnp.minimum(bottom, fold8(score, jnp.min))
            top = jnp.maximum(top, fold8(seen, jnp.max))
            n_nonneg = n_nonneg + fold8(jnp.where(seen >= 0.0, 1, 0), jnp.sum)
            n_pos = n_pos + fold8(jnp.where(seen > 0.0, 1, 0), jnp.sum)
            score_scr[pl.ds(key_tile(j), KEY_TILE), :] = seen
        return top, bottom, n_nonneg, n_pos

    zeros8 = jnp.zeros((8, LANES), I32)
    carry = (jnp.full((8, LANES), -jnp.inf, F32), jnp.full((8, LANES), jnp.inf, F32), zeros8, zeros8)
    carry = lax.fori_loop(0, n_st - 1, functools.partial(score_step, masked=False), carry)
    top, bottom, n_nonneg, n_pos = score_step(n_st - 1, carry, masked=True)
    top = jnp.max(top, axis=0, keepdims=True)
    bottom = jnp.min(bottom, axis=0, keepdims=True)
    count0 = jnp.sum(n_nonneg, axis=0, keepdims=True)
    count_pos = jnp.sum(n_pos, axis=0, keepdims=True)

    def count_f32(cand, strict):
        def body(j, acc):
            for part in range(SCAN_TILE // COUNT_TILE):
                start = pl.multiple_of(j * SCAN_TILE + part * COUNT_TILE, COUNT_TILE)
                x = score_scr[pl.ds(start, COUNT_TILE), :]
                ind = jnp.where((x > cand) if strict else (x >= cand), 1, 0)
                acc = acc + jnp.sum(ind.reshape(COUNT_TILE // 8, 8, LANES), axis=0)
            return acc
        acc = lax.fori_loop(0, n_st, body, jnp.zeros((8, LANES), I32))
        return jnp.sum(acc, axis=0, keepdims=True)

    def to_pattern(v):
        bits = lax.bitcast_convert_type(v, I32)
        return bits ^ ((bits >> 31) & 0x7FFFFFFF)

    def to_f32(c):
        return lax.bitcast_convert_type(c ^ ((c >> 31) & 0x7FFFFFFF), F32)

    n_visible = t0 + 1 + lax.broadcasted_iota(I32, (1, LANES), 1)
    nonneg = count0 >= topk
    lo = jnp.where(nonneg, 0, to_pattern(bottom))
    hi = jnp.where(nonneg, to_pattern(top) + 1, 0)
    count_lo = jnp.where(nonneg, count0, n_visible)
    count_hi = jnp.where(nonneg, 0, count0)
    few = n_visible < topk
    lo = jnp.where(few, PATTERN_NEG_FLT_MAX, lo)
    zero_tied = nonneg & (count_pos < topk)
    hi = jnp.where(zero_tied, 1, hi)
    count_hi = jnp.where(zero_tied, count_pos, count_hi)
    open_q = jnp.where(few | zero_tied | (count_lo == topk), 0, 1)

    log_topk = math.log(topk)

    def count_error(count):
        return jnp.log(count.astype(F32) + 0.5) - log_topk

    def probes_left(st):
        n_pass, n_open = st[0], st[-1]
        return (n_pass < MAX_SEARCH_PASSES) & (n_open > 0)

    def probe_group(st):
        n_pass = st[0]
        n_probes = jnp.where(n_pass == 0, FIRST_CHECK_PASSES, PASSES_PER_CHECK)

        def probe(step, carry):
            lo, hi, count_lo, count_hi, err_lo, err_hi, last_side, open_q = carry
            v_lo, v_hi = to_f32(lo), to_f32(hi)
            frac = err_lo / (err_lo - err_hi)
            frac = jnp.where(count_lo - count_hi <= 4, 0.5, frac)
            guess = to_pattern(v_lo + (v_hi - v_lo) * frac)
            middle = lo + lax.shift_right_logical(hi - lo, 1)
            cand = jnp.where(n_pass + step >= INTERPOLATED_PASSES, middle, guess)
            cand = jnp.minimum(jnp.maximum(cand, lo + 1), hi - 1)
            count = count_f32(to_f32(cand), strict=False)
            raise_lo = (open_q == 1) & (count >= topk)
            lower_hi = (open_q == 1) & (count < topk)
            err = count_error(count)
            err_hi = jnp.where(raise_lo & (last_side == 1), err_hi * 0.5, err_hi)
            err_lo = jnp.where(lower_hi & (last_side == -1), err_lo * 0.5, err_lo)
            err_lo = jnp.where(raise_lo, err, err_lo)
            err_hi = jnp.where(lower_hi, err, err_hi)
            lo = jnp.where(raise_lo, cand, lo)
            count_lo = jnp.where(raise_lo, count, count_lo)
            hi = jnp.where(lower_hi, cand, hi)
            count_hi = jnp.where(lower_hi, count, count_hi)
            last_side = jnp.where(raise_lo, 1, jnp.where(lower_hi, -1, last_side))
            closed = (count_lo == topk) | (hi - lo == 1)
            return lo, hi, count_lo, count_hi, err_lo, err_hi, last_side, jnp.where(closed, 0, open_q)

        carry = lax.fori_loop(0, n_probes, probe, st[1:-1])
        return (n_pass + n_probes,) + carry + (jnp.sum(carry[-1]),)

    state = (jnp.int32(0), lo, hi, count_lo, count_hi, count_error(count_lo), count_error(count_hi),
             jnp.zeros((1, LANES), I32), open_q, jnp.sum(open_q))
    state = lax.while_loop(probes_left, probe_group, state)
    lo, count_lo, count_hi = state[1], state[3], state[4]
    tau = to_f32(lo)
    need = jnp.where((count_lo == topk) | few, topk, topk - count_hi).astype(F32)

    m_scr[...] = jnp.full(m_scr.shape, MASKED_LOGIT, F32)
    l_scr[...] = jnp.zeros(l_scr.shape, F32)
    acc_scr[...] = jnp.zeros(acc_scr.shape, F32)
    ones_rows = jnp.ones((16, KEY_TILE), BF16)
    last = n_kt - 1

    def mask_pair(j_first, parity):
        xs = [score_scr[pl.ds(key_tile(jnp.minimum(j_first + slot, last)), KEY_TILE), :] for slot in range(2)]
        ties = [x == tau for x in xs]
        tie_cols = jnp.concatenate([jnp.where(tie, 1.0, 0.0).astype(BF16) for tie in ties], axis=1)
        ranks = _dot(tri_ref[...], tie_cols)
        tie_carry = tie_scr[0:1, :]
        for slot, (x, tie) in enumerate(zip(xs, ties)):
            rank = ranks[:, slot * LANES:(slot + 1) * LANES] + tie_carry
            tie_carry = rank[KEY_TILE - 1:KEY_TILE, :]
            take = (tie & (rank <= need)) | (x > tau)
            mask_scr[parity, slot] = jnp.where(take, 0.0, MASKED_LOGIT)
        tie_scr[0:1, :] = tie_carry

    def qk_pair(slot, j, p):
        s_scr[slot, :, p * pair_w:(p + 1) * pair_w] = _dot(
            k_ref[pl.ds(key_tile(j), KEY_TILE), p * LANES:(p + 1) * LANES], wq_scr[p])

    def softmax_pair(slot, j, p, near, parity):
        mask_add = mask_scr[parity, slot]
        alphas = []
        for h in (2 * p, 2 * p + 1):
            cols = slice(h * LANES, (h + 1) * LANES)
            logits = s_scr[slot, :, cols] + mask_add
            if near:
                kind_top = jnp.clip(2 * j - i + 2, 0, 2)
                kind_bot = jnp.clip(2 * j - i + 3, 0, 2)
                logits = logits + jnp.concatenate(
                    [bias_ref[kind_top, :, cols], bias_ref[kind_bot, :, cols]], axis=0)
            m_old = m_scr[h:h + 1, :]
            m_new = jnp.maximum(m_old, jnp.max(logits, axis=0, keepdims=True))
            m_scr[h:h + 1, :] = m_new
            p_scr[slot, :, cols] = jnp.exp2(logits - m_new).astype(BF16)
            alphas.append(jnp.exp2(m_old - m_new))
        return alphas

    def pv_pair(slot, j, p, alphas):
        lhs = jnp.concatenate([vt_ref[p * 2 * hd:(p + 1) * 2 * hd, pl.ds(key_tile(j), KEY_TILE)], ones_rows],
                              axis=0)
        out = _dot(lhs, p_scr[slot, :, p * pair_w:(p + 1) * pair_w])
        for half in range(2):
            h = 2 * p + half
            rows_h = slice(h * hd, (h + 1) * hd)
            q_cols = slice(half * LANES, (half + 1) * LANES)
            acc_scr[rows_h, :] = acc_scr[rows_h, :] * alphas[half] + out[half * hd:(half + 1) * hd, q_cols]
            l_scr[h:h + 1, :] = l_scr[h:h + 1, :] * alphas[half] + out[2 * hd:2 * hd + 1, q_cols]

    def pending_alphas(p):
        return [alpha_scr[h:h + 1, :] for h in (2 * p, 2 * p + 1)]

    def clear_pending():
        p_scr[1] = jnp.zeros(p_scr.shape[1:], BF16)
        alpha_scr[...] = jnp.ones(alpha_scr.shape, F32)

    def pair_step(ja, j_pending, j_next, near, parity):
        alphas_a = []
        for p in range(n_pairs):
            pv_pair(1, j_pending, p, pending_alphas(p))
            qk_pair(1, ja + 1, p)
            alphas_a.append(softmax_pair(0, ja, p, near, parity))
        for p in range(n_pairs):
            pv_pair(0, ja, p, alphas_a[p])
            qk_pair(0, j_next, p)
            alphas_b = softmax_pair(1, ja + 1, p, near, parity)
            for half in range(2):
                alpha_scr[2 * p + half:2 * p + half + 1, :] = alphas_b[half]
        mask_pair(ja + 2, 1 - parity)

    def single_step(ja, j_pending, near, parity):
        alphas_a = []
        for p in range(n_pairs):
            pv_pair(1, j_pending, p, pending_alphas(p))
            alphas_a.append(softmax_pair(0, ja, p, near, parity))
        for p in range(n_pairs):
            pv_pair(0, ja, p, alphas_a[p])
        clear_pending()

    n_far = 2 * (jnp.maximum(n_kt - 2, 0) // 2)
    n_near = n_kt - n_far
    first_near_parity = (n_far // 2) % 2
    clear_pending()
    tie_scr[...] = jnp.zeros(tie_scr.shape, F32)
    mask_pair(0, 0)
    for p in range(n_pairs):
        qk_pair(0, 0, p)

    def far_body(jp, carry):
        pair_step(2 * jp, jnp.maximum(2 * jp - 1, 0), 2 * jp + 2, near=False, parity=jp % 2)
        return carry

    lax.fori_loop(0, n_far // 2, far_body, 0)

    @pl.when(n_near >= 2)
    def _():
        pair_step(n_far, jnp.maximum(n_far - 1, 0), jnp.minimum(n_far + 2, last), near=True,
                  parity=first_near_parity)

    @pl.when(n_near % 2 == 1)
    def _():
        single_step(last, jnp.where(n_near == 3, n_far + 1, jnp.maximum(n_far - 1, 0)), near=True,
                    parity=jnp.where(n_near == 3, 1 - first_near_parity, first_near_parity))

    for p in range(n_pairs):
        pv_pair(1, last, p, pending_alphas(p))

    for h in range(N_ATTN_HEADS):
        rows_h = slice(h * hd, (h + 1) * hd)
        acc_scr[rows_h, :] = acc_scr[rows_h, :] / l_scr[h:h + 1, :]
    o_ref[...] = acc_scr[...].T


def _dsa_kernel(*refs, topk, t_valid):
    o_ref = refs[8]
    i = pl.program_id(1)
    is_real = i * LANES < t_valid

    @pl.when(is_real)
    def _():
        _dsa_block(i, *refs, topk=topk)

    @pl.when(jnp.logical_not(is_real))
    def _():
        o_ref[...] = jnp.zeros(o_ref.shape, F32)


def _dsa_attention(qit, rows, qt, kidx, k, vt, bias, tri, batch, tp, t_valid, topk):
    n_rows = k.shape[0]
    nqb = tp // LANES
    d_idx = N_IDX_HEADS * IDX_DIM
    n_pairs = N_ATTN_HEADS // 2
    key_rows = -(-tp // SCAN_TILE) * SCAN_TILE
    q_cols = lambda height: pl.BlockSpec((None, height, LANES), lambda b, i: (b, 0, i))
    return pl.pallas_call(
        functools.partial(_dsa_kernel, topk=topk, t_valid=t_valid),
        out_shape=jax.ShapeDtypeStruct((n_rows, D_ATTN), F32),
        grid=(batch, nqb),
        in_specs=[q_cols(d_idx),
                  pl.BlockSpec((16, LANES), lambda b, i: (0, b * nqb + i)),
                  q_cols(D_ATTN),
                  pl.BlockSpec((tp, LANES), lambda b, i: (b, 0)),
                  pl.BlockSpec((tp, D_ATTN), lambda b, i: (b, 0)),
                  pl.BlockSpec((None, D_ATTN, tp), lambda b, i: (b, 0, 0)),
                  pl.BlockSpec((3, LANES, N_ATTN_HEADS * LANES), lambda b, i: (0, 0, 0)),
                  pl.BlockSpec((KEY_TILE, KEY_TILE), lambda b, i: (0, 0))],
        out_specs=pl.BlockSpec((LANES, D_ATTN), lambda b, i: (b * nqb + i, 0)),
        scratch_shapes=[pltpu.VMEM((key_rows, LANES), F32),
                        pltpu.VMEM((LANES, N_IDX_HEADS * LANES), BF16),
                        pltpu.VMEM((n_pairs, LANES, 2 * LANES), BF16),
                        pltpu.VMEM((2, KEY_TILE, N_ATTN_HEADS * LANES), F32),
                        pltpu.VMEM((2, KEY_TILE, N_ATTN_HEADS * LANES), BF16),
                        pltpu.VMEM((N_ATTN_HEADS, LANES), F32),
                        pltpu.VMEM((N_ATTN_HEADS, LANES), F32),
                        pltpu.VMEM((D_ATTN, LANES), F32),
                        pltpu.VMEM((N_ATTN_HEADS, LANES), F32),
                        pltpu.VMEM((2, 2, KEY_TILE, LANES), F32),
                        pltpu.VMEM((8, LANES), F32)],
        compiler_params=_params("parallel", "parallel"),
        name="dsa_attention",
    )(qit, rows, qt, kidx, k, vt, bias, tri)


def _gdn_prep_kernel(x_ref, halo_ref, cw_ref, rows_ref, alog_ref, dtb_ref, q_ref, k_ref, v_ref, bg_ref, buf):
    first = pl.program_id(1) == 0
    rb = x_ref.shape[0]
    buf[0:HALO_ROWS, :] = jnp.where(first, 0.0, halo_ref[...])
    buf[HALO_ROWS:, :] = x_ref[...]
    acc = jnp.zeros((rb, 3 * D_GDN), F32)
    for tap in range(CONV_WIDTH):
        start = HALO_ROWS - (CONV_WIDTH - 1) + tap
        acc = acc + cw_ref[tap:tap + 1, :] * buf[start:start + rb, :]
    y = _silu(acc)
    for h in range(N_GDN_HEADS):
        cols = slice(h * GDN_HEAD_DIM, (h + 1) * GDN_HEAD_DIM)
        qh = y[:, cols]
        kh = y[:, D_GDN + h * GDN_HEAD_DIM:D_GDN + (h + 1) * GDN_HEAD_DIM]
        q_ref[:, cols] = (qh * lax.rsqrt(jnp.sum(qh * qh, axis=-1, keepdims=True) + EPS)
                          * (GDN_HEAD_DIM ** -0.5))
        k_ref[:, cols] = kh * lax.rsqrt(jnp.sum(kh * kh, axis=-1, keepdims=True) + EPS)
    v_ref[...] = y[:, 2 * D_GDN:]
    rows = rows_ref[...]
    beta = _sigmoid(rows[4:8, :])
    a = rows[8:12, :] + dtb_ref[...]
    softplus = jnp.maximum(a, 0.0) + jnp.log1p(jnp.exp(-jnp.abs(a)))
    bg_ref[0:4, :] = beta
    bg_ref[4:8, :] = -jnp.exp(alog_ref[...]) * softplus


def _gdn_prep(proj, rows, lw, batch, tp, rb):
    n_rows = proj.shape[0]
    nb = tp // rb
    halo_per_block = rb // HALO_ROWS
    row_spec = pl.BlockSpec((rb, D_GDN), lambda b, i: (b * nb + i, 0))
    return pl.pallas_call(
        _gdn_prep_kernel,
        out_shape=(jax.ShapeDtypeStruct((n_rows, D_GDN), F32),) * 3
        + (jax.ShapeDtypeStruct((8, n_rows), F32),),
        grid=(batch, nb),
        in_specs=[pl.BlockSpec((rb, 3 * D_GDN), lambda b, i: (b * nb + i, 0)),
                  pl.BlockSpec((HALO_ROWS, 3 * D_GDN),
                               lambda b, i: (jnp.maximum((b * nb + i) * halo_per_block - 1, 0), 0)),
                  pl.BlockSpec((CONV_WIDTH, 3 * D_GDN), lambda b, i: (0, 0)),
                  pl.BlockSpec((16, rb), lambda b, i: (0, b * nb + i)),
                  pl.BlockSpec((N_GDN_HEADS, 1), lambda b, i: (0, 0)),
                  pl.BlockSpec((N_GDN_HEADS, 1), lambda b, i: (0, 0))],
        out_specs=(row_spec, row_spec, row_spec,
                   pl.BlockSpec((8, rb), lambda b, i: (0, b * nb + i))),
        scratch_shapes=[pltpu.VMEM((HALO_ROWS + rb, 3 * D_GDN), F32)],
        compiler_params=_params("parallel", "parallel"),
        name="gdn_prep",
    )(proj, proj, lw["conv_w"], rows, lw["a_log"], lw["dt_bias"])


def _gdn_chunk_kernel(q_ref, k_ref, v_ref, bg_ref, m_ref, n_ref, p_ref, r_ref, cd_ref):
    c = LANES
    n_chunks = q_ref.shape[0] // c
    items = [(ch, h) for ch in range(n_chunks) for h in range(N_GDN_HEADS)]
    idx = range(len(items))
    row = lax.broadcasted_iota(I32, (c, c), 0)
    col = lax.broadcasted_iota(I32, (c, c), 1)
    tri = row >= col
    strict = row > col
    eye = jnp.where(row == col, 1.0, 0.0)
    lane8 = lax.broadcasted_iota(I32, (8, c), 1)
    gates, decays = [], []
    for ch in range(n_chunks):
        bg = bg_ref[:, ch * c:(ch + 1) * c]
        dec = bg
        shift = 1
        while shift < c:
            dec = dec + jnp.where(lane8 >= shift, pltpu.roll(dec, shift, 1), 0.0)
            shift *= 2
        gates.append(bg)
        decays.append(dec)

    def tokens(ref, n):
        ch, h = items[n]
        return ref[ch * c:(ch + 1) * c, h * GDN_HEAD_DIM:(h + 1) * GDN_HEAD_DIM]

    d_row = [jnp.broadcast_to(decays[ch][4 + h:5 + h, :], (c, c)) for ch, h in items]
    d_col = [d.T for d in d_row]
    beta_col = [jnp.broadcast_to(gates[ch][h:h + 1, :], (c, c)).T for ch, h in items]
    d_last = [d[:, c - 1:c] for d in d_row]
    gamma = [jnp.exp(jnp.where(tri, d_col[n] - d_row[n], MASKED_LOGIT)) for n in idx]
    exp_d = [jnp.exp(d_col[n]) for n in idx]
    k16 = [tokens(k_ref, n).astype(BF16) for n in idx]
    kb = [tokens(k_ref, n) * beta_col[n] for n in idx]
    nil = [jnp.where(strict, _dot_nt(kb[n].astype(BF16), k16[n]) * gamma[n], 0.0) for n in idx]
    nil_parts = [_split_bf16(x) for x in nil]
    inv = [eye - x for x in nil]
    power = [_dot_split(x, x) for x in nil_parts]
    steps = int(math.log2(c)) - 1
    for it in range(steps):
        power_parts = [_split_bf16(x) for x in power]
        inv = [inv[n] + _dot_split(_split_bf16(inv[n]), power_parts[n]) for n in idx]
        if it + 1 < steps:
            power = [_dot_split(x, x) for x in power_parts]
    rhs = [jnp.concatenate([kb[n] * exp_d[n], tokens(v_ref, n) * beta_col[n]], axis=1) for n in idx]
    wu = [_dot_split(_split_bf16(inv[n]), _split_bf16(rhs[n])).astype(BF16) for n in idx]
    aqk = [jnp.where(tri, _dot_nt(tokens(q_ref, n).astype(BF16), k16[n]) * gamma[n], 0.0).astype(BF16)
           for n in idx]
    kd_t = [(tokens(k_ref, n) * jnp.exp(d_last[n] - d_col[n])).T.astype(BF16) for n in idx]
    state_wu = [_dot(kd_t[n], wu[n]) for n in idx]
    out_wu = [_dot(aqk[n], wu[n]) for n in idx]
    for n, (ch, h) in enumerate(items):
        m_ref[ch, h] = (-state_wu[n][:, :c]).astype(BF16)
        n_ref[ch, h] = state_wu[n][:, c:]
        p_ref[ch, h] = (tokens(q_ref, n) * exp_d[n] - out_wu[n][:, :c]).astype(BF16)
        r_ref[ch, h] = out_wu[n][:, c:]
        cd_ref[ch, h:h + 1, :] = jnp.exp(d_last[n][0:1, :] + jnp.zeros((1, c), F32))


def _gdn_chunks(qn, kn, vv, bg):
    n_rows = qn.shape[0]
    nc = n_rows // LANES
    per_step = GDN_CHUNKS_PER_STEP
    tok = pl.BlockSpec((per_step * LANES, D_GDN), lambda c: (c, 0))
    mat = pl.BlockSpec((per_step, N_GDN_HEADS, LANES, LANES), lambda c: (c, 0, 0, 0))
    mat_shape = lambda dt: jax.ShapeDtypeStruct((nc, N_GDN_HEADS, LANES, LANES), dt)
    return pl.pallas_call(
        _gdn_chunk_kernel,
        out_shape=(mat_shape(BF16), mat_shape(F32), mat_shape(BF16), mat_shape(F32),
                   jax.ShapeDtypeStruct((nc, N_GDN_HEADS, LANES), F32)),
        grid=(nc // per_step,),
        in_specs=[tok, tok, tok, pl.BlockSpec((8, per_step * LANES), lambda c: (0, c))],
        out_specs=(mat, mat, mat, mat, pl.BlockSpec((per_step, N_GDN_HEADS, LANES), lambda c: (c, 0, 0))),
        compiler_params=_params("parallel"),
        name="gdn_chunks",
    )(qn, kn, vv, bg)


def _gdn_scan_kernel(m_ref, n_ref, p_ref, r_ref, cd_ref, o_ref, s_scr, *, batch):
    @pl.when(pl.program_id(0) == 0)
    def _():
        s_scr[...] = jnp.zeros(s_scr.shape, F32)

    for b in range(batch):
        for h in range(N_GDN_HEADS):
            s = s_scr[b, h]
            s16 = s.astype(BF16)
            o_ref[b, :, h * GDN_HEAD_DIM:(h + 1) * GDN_HEAD_DIM] = _dot(p_ref[b, h], s16) + r_ref[b, h]
            s_scr[b, h] = s * cd_ref[b, h:h + 1, :] + _dot(m_ref[b, h], s16) + n_ref[b, h]


def _gdn_scan(m_mat, n_mat, p_mat, r_mat, cd, batch, tp):
    nc = tp // LANES
    shape5 = lambda a: a.reshape(batch, nc, N_GDN_HEADS, LANES, LANES)
    mat = pl.BlockSpec((batch, None, N_GDN_HEADS, LANES, LANES), lambda c: (0, c, 0, 0, 0))
    return pl.pallas_call(
        functools.partial(_gdn_scan_kernel, batch=batch),
        out_shape=jax.ShapeDtypeStruct((batch, tp, D_GDN), F32),
        grid=(nc,),
        in_specs=[mat, mat, mat, mat,
                  pl.BlockSpec((batch, None, N_GDN_HEADS, LANES), lambda c: (0, c, 0, 0))],
        out_specs=pl.BlockSpec((batch, LANES, D_GDN), lambda c: (0, c, 0)),
        scratch_shapes=[pltpu.VMEM((batch, N_GDN_HEADS, LANES, LANES), F32)],
        compiler_params=_params("arbitrary"),
        name="gdn_scan",
    )(shape5(m_mat), shape5(n_mat), shape5(p_mat), shape5(r_mat), cd.reshape(batch, nc, N_GDN_HEADS, LANES))


def _out_proj_kernel(h_ref, oa_ref, za_ref, og_ref, zg_ref, gain_ref, w_ref, out_ref):
    attn = (oa_ref[...] * _silu(za_ref[...])).astype(BF16)
    y = _dot(attn, w_ref[0:D_ATTN, :])
    og, zg = og_ref[...], zg_ref[...]
    for h in range(N_GDN_HEADS):
        cols = slice(h * GDN_HEAD_DIM, (h + 1) * GDN_HEAD_DIM)
        gated = (_rms_rows(og[:, cols], gain_ref[...]) * _silu(zg[:, cols])).astype(BF16)
        y = y + _dot(gated, w_ref[D_ATTN + h * GDN_HEAD_DIM:D_ATTN + (h + 1) * GDN_HEAD_DIM, :])
    out_ref[...] = h_ref[...] + y


def _out_proj(h2, o_attn, proj, o_gdn, gain, w_out, rb):
    n_rows = h2.shape[0]
    blk = lambda width, col: pl.BlockSpec((rb, width), lambda i: (i, col))
    return pl.pallas_call(
        _out_proj_kernel,
        out_shape=jax.ShapeDtypeStruct((n_rows, D_MODEL), F32),
        grid=(n_rows // rb,),
        in_specs=[blk(D_MODEL, 0), blk(D_ATTN, 0), blk(D_ATTN, COL_Z_ATTN), blk(D_GDN, 0),
                  blk(D_GDN, COL_Z_GDN),
                  pl.BlockSpec((1, GDN_HEAD_DIM), lambda i: (0, 0)),
                  pl.BlockSpec((D_MODEL, D_MODEL), lambda i: (0, 0))],
        out_specs=blk(D_MODEL, 0),
        compiler_params=_params("parallel"),
        name="out_proj",
    )(h2, o_attn, proj, o_gdn, proj, gain, w_out)


def _pack_layer(norm_gain, w_in, cq_gain, ckv_gain, w_uq, w_ukv, w_q_idx, q_gain, k_gain, conv_w, a_log,
                dt_bias, gdn_gain, w_out):
    o = 0
    parts = {}
    for name, size in (("c_q", Q_RANK), ("c_kv", KV_RANK), ("k_idx", IDX_DIM), ("w_idx", N_IDX_HEADS),
                       ("z_attn", D_ATTN), ("qkv_g", 3 * D_GDN), ("z_g", D_GDN), ("b", N_GDN_HEADS),
                       ("a", N_GDN_HEADS)):
        parts[name] = w_in[:, o:o + size]
        o += size
    w_packed = jnp.concatenate([parts["qkv_g"], parts["c_q"], parts["c_kv"], parts["k_idx"], parts["k_idx"],
                                parts["z_attn"], parts["z_g"]], axis=1).astype(BF16)
    w_rows = jnp.concatenate([parts["w_idx"], parts["b"], parts["a"],
                              jnp.zeros((D_MODEL, 4), w_in.dtype)], axis=1).T.astype(BF16)
    return dict(
        gain=norm_gain[None, :], w_packed=w_packed, w_rows=w_rows,
        g_cq=cq_gain[None, :], g_ckv=ckv_gain[None, :],
        w_uqt=w_uq.T.astype(BF16), w_qit=w_q_idx.T.astype(BF16),
        w_ukt=w_ukv[:, :D_ATTN].T.astype(BF16), w_uvt=w_ukv[:, D_ATTN:].T.astype(BF16),
        g_q_col=q_gain[:, None], g_k_col=k_gain[:, None],
        conv_w=conv_w, a_log=a_log[:, None], dt_bias=dt_bias[:, None],
        gdn_gain=gdn_gain[None, :], w_out=w_out.astype(BF16))


def _layer(h2, lw, bias, tri, batch, tp, t_valid, topk):
    rb = _row_block(tp)
    proj, rows = _in_proj(h2, lw["gain"], lw["w_packed"], lw["w_rows"], rb)
    qt, k, vt, qit, kidx = _dsa_prep(proj, lw, batch, tp, rb)
    o_attn = _dsa_attention(qit, rows, qt, kidx, k, vt, bias, tri, batch, tp, t_valid, topk)
    qn, kn, vv, bg = _gdn_prep(proj, rows, lw, batch, tp, rb)
    m_mat, n_mat, p_mat, r_mat, cd = _gdn_chunks(qn, kn, vv, bg)
    o_gdn = _gdn_scan(m_mat, n_mat, p_mat, r_mat, cd, batch, tp).reshape(batch * tp, D_GDN)
    return _out_proj(h2, o_attn, proj, o_gdn, lw["gdn_gain"], lw["w_out"], rb)


def _forward(x, meta_tokens, rel_bias_table, layer_weights, topk):
    batch, seq, _ = x.shape
    t = seq + N_META
    tp = -(-t // KEY_TILE) * KEY_TILE
    meta = jnp.broadcast_to(meta_tokens[None].astype(x.dtype), (batch, N_META, D_MODEL))
    h = jnp.concatenate([meta, x, jnp.zeros((batch, tp - t, D_MODEL), x.dtype)], axis=1)
    h2 = h.reshape(batch * tp, D_MODEL)
    bias = _bias_tiles(rel_bias_table)
    tri = jnp.tril(jnp.ones((KEY_TILE, KEY_TILE), BF16))
    for lw in layer_weights:
        h2 = _layer(h2, lw, bias, tri, batch, tp, t, topk)
    return h2.reshape(batch, tp, D_MODEL)[:, N_META:t]


def kernel(x, meta_tokens, rel_bias_table, norm_gain, w_in, cq_norm_gain, ckv_norm_gain, w_uq, w_ukv, w_q_idx,
           q_norm_gain, k_norm_gain, conv_w, a_log, dt_bias, gdn_norm_gain, w_out):
    depth = norm_gain.shape[0]
    topk = min(TOPK_MAX, x.shape[1] // 4)
    layers = [_pack_layer(norm_gain[l], w_in[l], cq_norm_gain[l], ckv_norm_gain[l], w_uq[l], w_ukv[l],
                          w_q_idx[l], q_norm_gain[l], k_norm_gain[l], conv_w[l], a_log[l], dt_bias[l],
                          gdn_norm_gain[l], w_out[l]) for l in range(depth)]
    return _forward(x, meta_tokens, rel_bias_table, layers, topk)
```

```python
import functools
import math

import jax
import jax.numpy as jnp
from jax import lax
from jax.experimental import pallas as pl
from jax.experimental.pallas import tpu as pltpu

F32 = jnp.float32
BF16 = jnp.bfloat16
I32 = jnp.int32

D_MODEL = 1024
N_META = 16
EPS = 1e-6
N_ATTN_HEADS = 8
ATTN_HEAD_DIM = 64
D_ATTN = N_ATTN_HEADS * ATTN_HEAD_DIM
Q_RANK = 256
KV_RANK = 128
N_IDX_HEADS = 4
IDX_DIM = 64
TOPK_MAX = 256
N_REL_BUCKETS = 32
REL_MAX_DIST = 128
N_GDN_HEADS = 4
GDN_HEAD_DIM = 128
D_GDN = N_GDN_HEADS * GDN_HEAD_DIM
CONV_WIDTH = 4

LANES = 128
KEY_TILE = 256
COUNT_TILE = 512
SCAN_TILE = 1024
FIRST_CHECK_PASSES = 8
INTERPOLATED_PASSES = 24
MAX_SEARCH_PASSES = INTERPOLATED_PASSES + 32
GDN_CHUNKS_PER_STEP = 2
ROW_TILES = (5, 4, 3, 2, 1)
HALO_ROWS = 8
D_PACKED = 3 * D_GDN + 512 + D_ATTN + D_GDN
COL_SMALL = 3
COL_Z_ATTN = 4
COL_Z_GDN = 5
MASKED_LOGIT = -1e30
LOG2_E = math.log2(math.e)
KEY_MIN = -2 ** 31
PATTERN_NEG_FLT_MAX = KEY_MIN + (1 << 23)
VMEM_LIMIT = 56 * 1024 * 1024

NT_DIMS = (((1,), (1,)), ((), ()))


def _dot(a, b):
    return jnp.dot(a, b, preferred_element_type=F32)


def _dot_nt(a, b):
    return lax.dot_general(a, b, NT_DIMS, preferred_element_type=F32)


def _split_bf16(x):
    hi = x.astype(BF16)
    return hi, (x - hi.astype(F32)).astype(BF16)


def _dot_split(a_parts, b_parts):
    (a_hi, a_lo), (b_hi, b_lo) = a_parts, b_parts
    return _dot(a_hi, b_hi) + (_dot(a_hi, b_lo) + _dot(a_lo, b_hi))


def _sigmoid(x):
    return 1.0 / (1.0 + jnp.exp(-x))


def _silu(x):
    return x * _sigmoid(x)


def _row_block(tp):
    tiles = tp // LANES
    return LANES * next(d for d in ROW_TILES if tiles % d == 0)


def _params(*sem):
    return pltpu.CompilerParams(dimension_semantics=sem, vmem_limit_bytes=VMEM_LIMIT)


def _bias_kernel(table_ref, out_ref):
    row = lax.broadcasted_iota(I32, (LANES, LANES), 0)
    col = lax.broadcasted_iota(I32, (LANES, LANES), 1)
    max_exact = N_REL_BUCKETS // 2
    for kind in range(3):
        dist = col - row + (2 - kind) * LANES
        n = jnp.maximum(dist, 0)
        nf = jnp.maximum(n, 1).astype(F32)
        large = max_exact + (jnp.log(nf / max_exact) / math.log(REL_MAX_DIST / max_exact)
                             * (N_REL_BUCKETS - max_exact)).astype(I32)
        large = jnp.minimum(large, N_REL_BUCKETS - 1)
        bucket = jnp.where(n < max_exact, n, large)
        for h in range(N_ATTN_HEADS):
            tile = jnp.zeros((LANES, LANES), F32)
            for b in range(N_REL_BUCKETS):
                tile = jnp.where(bucket == b, table_ref[b, h], tile)
            far = table_ref[N_REL_BUCKETS - 1, h]
            out_ref[kind, :, h * LANES:(h + 1) * LANES] = (tile - far) * LOG2_E


def _bias_tiles(rel_table):
    return pl.pallas_call(
        _bias_kernel,
        out_shape=jax.ShapeDtypeStruct((3, LANES, N_ATTN_HEADS * LANES), F32),
        in_specs=[pl.BlockSpec(memory_space=pltpu.SMEM)],
        out_specs=pl.BlockSpec(memory_space=pltpu.VMEM),
        name="rel_bias_tiles",
    )(rel_table)


def _in_proj_kernel(h_ref, gain_ref, w_ref, wrows_ref, proj_ref, rows_ref):
    x = h_ref[...]
    y = x * lax.rsqrt(jnp.mean(x * x, axis=-1, keepdims=True) + EPS)
    hn = (y * gain_ref[...]).astype(BF16)
    proj_ref[...] = _dot(hn, w_ref[...])
    rows_ref[...] = _dot_nt(wrows_ref[...], hn)


def _in_proj(h2, gain, w_packed, w_rows, rb):
    n_rows = h2.shape[0]
    grid = (n_rows // rb,)
    return pl.pallas_call(
        _in_proj_kernel,
        out_shape=(jax.ShapeDtypeStruct((n_rows, D_PACKED), F32),
                   jax.ShapeDtypeStruct((16, n_rows), F32)),
        grid=grid,
        in_specs=[pl.BlockSpec((rb, D_MODEL), lambda i: (i, 0)),
                  pl.BlockSpec((1, D_MODEL), lambda i: (0, 0)),
                  pl.BlockSpec((D_MODEL, D_PACKED), lambda i: (0, 0)),
                  pl.BlockSpec((16, D_MODEL), lambda i: (0, 0))],
        out_specs=(pl.BlockSpec((rb, D_PACKED), lambda i: (i, 0)),
                   pl.BlockSpec((16, rb), lambda i: (0, i))),
        compiler_params=_params("parallel"),
        name="in_proj",
    )(h2, gain, w_packed, w_rows)


def _rms_rows(x, gain):
    return x * lax.rsqrt(jnp.mean(x * x, axis=-1, keepdims=True) + EPS) * gain


def _dsa_prep_kernel(sm_ref, gcq_ref, gckv_ref, wuqt_ref, wqit_ref, wukt_ref, wuvt_ref, gq_ref, gk_ref,
                     qt_ref, k_ref, vt_ref, qit_ref, kidx_ref):
    sm = sm_ref[...]
    rb = sm.shape[0]
    cq = _rms_rows(sm[:, :Q_RANK], gcq_ref[...]).astype(BF16)
    ckv = _rms_rows(sm[:, Q_RANK:Q_RANK + KV_RANK], gckv_ref[...]).astype(BF16)
    kidx_ref[...] = sm[:, Q_RANK + KV_RANK:].astype(BF16)
    q3 = _dot_nt(wuqt_ref[...], cq).reshape(N_ATTN_HEADS, ATTN_HEAD_DIM, rb)
    q3 = q3 * lax.rsqrt(jnp.mean(q3 * q3, axis=1, keepdims=True) + EPS) * gq_ref[...][None]
    qt_ref[...] = (q3 * (ATTN_HEAD_DIM ** -0.5 * LOG2_E)).reshape(D_ATTN, rb).astype(BF16)
    k3 = _dot_nt(wukt_ref[...], ckv).reshape(N_ATTN_HEADS, ATTN_HEAD_DIM, rb)
    k3 = k3 * lax.rsqrt(jnp.mean(k3 * k3, axis=1, keepdims=True) + EPS) * gk_ref[...][None]
    k_ref[...] = k3.reshape(D_ATTN, rb).T.astype(BF16)
    vt_ref[...] = _dot_nt(wuvt_ref[...], ckv).astype(BF16)
    qit_ref[...] = _dot_nt(wqit_ref[...], cq).astype(BF16)


def _dsa_prep(proj, lw, batch, tp, rb):
    n_rows = proj.shape[0]
    nb = tp // rb
    d_idx = N_IDX_HEADS * IDX_DIM
    const = lambda shape: pl.BlockSpec(shape, lambda b, i: (0, 0))
    row_spec = lambda width: pl.BlockSpec((rb, width), lambda b, i: (b * nb + i, 0))
    col_spec = lambda height: pl.BlockSpec((None, height, rb), lambda b, i: (b, 0, i))
    return pl.pallas_call(
        _dsa_prep_kernel,
        out_shape=(jax.ShapeDtypeStruct((batch, D_ATTN, tp), BF16),
                   jax.ShapeDtypeStruct((n_rows, D_ATTN), BF16),
                   jax.ShapeDtypeStruct((batch, D_ATTN, tp), BF16),
                   jax.ShapeDtypeStruct((batch, d_idx, tp), BF16),
                   jax.ShapeDtypeStruct((n_rows, LANES), BF16)),
        grid=(batch, nb),
        in_specs=[pl.BlockSpec((rb, 512), lambda b, i: (b * nb + i, COL_SMALL)),
                  const((1, Q_RANK)), const((1, KV_RANK)),
                  const((D_ATTN, Q_RANK)), const((d_idx, Q_RANK)),
                  const((D_ATTN, KV_RANK)), const((D_ATTN, KV_RANK)),
                  const((ATTN_HEAD_DIM, 1)), const((ATTN_HEAD_DIM, 1))],
        out_specs=(col_spec(D_ATTN), row_spec(D_ATTN), col_spec(D_ATTN), col_spec(d_idx), row_spec(LANES)),
        compiler_params=_params("parallel", "parallel"),
        name="dsa_prep",
    )(proj, lw["g_cq"], lw["g_ckv"], lw["w_uqt"], lw["w_qit"], lw["w_ukt"], lw["w_uvt"], lw["g_q_col"],
      lw["g_k_col"])


def _dsa_block(i, qit_ref, rows_ref, qt_ref, kidx_ref, k_ref, vt_ref, bias_ref, tri_ref, o_ref,
               score_scr, wi_scr, wq_scr, s_scr, p_scr, m_scr, l_scr, acc_scr, alpha_scr, mask_scr, tie_scr, *, topk):
    t0 = i * LANES
    n_kt = i // 2 + 1
    n_st = (n_kt + 3) // 4
    hd = ATTN_HEAD_DIM
    pair_w = 2 * LANES
    n_pairs = N_ATTN_HEADS // 2

    zeros_hd = jnp.zeros((hd, LANES), BF16)
    for h in range(N_IDX_HEADS):
        wi_scr[0:IDX_DIM, h * LANES:(h + 1) * LANES] = qit_ref[h * IDX_DIM:(h + 1) * IDX_DIM, :]
    wi_scr[IDX_DIM:, :] = jnp.zeros((LANES - IDX_DIM, N_IDX_HEADS * LANES), BF16)
    for p in range(n_pairs):
        wq_scr[p, 0:hd, 0:LANES] = qt_ref[2 * p * hd:(2 * p + 1) * hd, :]
        wq_scr[p, 0:hd, LANES:] = zeros_hd
        wq_scr[p, hd:, 0:LANES] = zeros_hd
        wq_scr[p, hd:, LANES:] = qt_ref[(2 * p + 1) * hd:(2 * p + 2) * hd, :]

    row = lax.broadcasted_iota(I32, (KEY_TILE, LANES), 0)
    col = lax.broadcasted_iota(I32, (KEY_TILE, LANES), 1)
    w_idx = rows_ref[0:N_IDX_HEADS, :] * (N_IDX_HEADS ** -0.5 * IDX_DIM ** -0.5)

    def key_tile(j):
        return pl.multiple_of(j * KEY_TILE, KEY_TILE)

    def causal(j):
        return (j * KEY_TILE + row) <= (t0 + col)

    def fold8(x, op):
        return op(x.reshape(KEY_TILE // 8, 8, LANES), axis=0)

    def score_step(jc, carry, masked):
        top, bottom, n_nonneg, n_pos = carry
        subs = [jc * (SCAN_TILE // KEY_TILE) + sub for sub in range(SCAN_TILE // KEY_TILE)]
        logits = [_dot(kidx_ref[pl.ds(key_tile(jnp.minimum(j, n_kt - 1)), KEY_TILE), :], wi_scr[...])
                  for j in subs]
        for j, lg in zip(subs, logits):
            score = jnp.zeros((KEY_TILE, LANES), F32)
            for h in range(N_IDX_HEADS):
                score = score + jnp.maximum(lg[:, h * LANES:(h + 1) * LANES], 0.0) * w_idx[h:h + 1, :]
            if masked:
                visible = causal(j)
                seen = jnp.where(visible, score, -jnp.inf)
                bottom = jnp.minimum(bottom, fold8(jnp.where(visible, score, jnp.inf), jnp.min))
            else:
                seen = score
                bottom = jnp.minimum(bottom, fold8(score, jnp.min))
            top = jnp.maximum(top, fold8(seen, jnp.max))
            n_nonneg = n_nonneg + fold8(jnp.where(seen >= 0.0, 1, 0), jnp.sum)
            n_pos = n_pos + fold8(jnp.where(seen > 0.0, 1, 0), jnp.sum)
            score_scr[pl.ds(key_tile(j), KEY_TILE), :] = seen
        return top, bottom, n_nonneg, n_pos

    zeros8 = jnp.zeros((8, LANES), I32)
    carry = (jnp.full((8, LANES), -jnp.inf, F32), jnp.full((8, LANES), jnp.inf, F32), zeros8, zeros8)
    carry = lax.fori_loop(0, n_st - 1, functools.partial(score_step, masked=False), carry)
    top, bottom, n_nonneg, n_pos = score_step(n_st - 1, carry, masked=True)
    top = jnp.max(top, axis=0, keepdims=True)
    bottom = jnp.min(bottom, axis=0, keepdims=True)
    count0 = jnp.sum(n_nonneg, axis=0, keepdims=True)
    count_pos = jnp.sum(n_pos, axis=0, keepdims=True)

    def count_f32(cand, below=None):
        def body(j, carry):
            acc, best = carry
            for part in range(SCAN_TILE // COUNT_TILE):
                start = pl.multiple_of(j * SCAN_TILE + part * COUNT_TILE, COUNT_TILE)
                x = score_scr[pl.ds(start, COUNT_TILE), :]
                ind = jnp.where(x >= cand, 1, 0)
                acc = acc + jnp.sum(ind.reshape(COUNT_TILE // 8, 8, LANES), axis=0)
                if below is not None:
                    under = jnp.where(x < below, x, -jnp.inf)
                    best = jnp.maximum(best, jnp.max(under.reshape(COUNT_TILE // 8, 8, LANES), axis=0))
            return acc, best
        acc, best = lax.fori_loop(0, n_st, body, (jnp.zeros((8, LANES), I32), jnp.full((8, LANES), -jnp.inf, F32)))
        count = jnp.sum(acc, axis=0, keepdims=True)
        if below is None:
            return count
        return count, jnp.max(best, axis=0, keepdims=True)

    def to_pattern(v):
        bits = lax.bitcast_convert_type(v, I32)
        return bits ^ ((bits >> 31) & 0x7FFFFFFF)

    def to_f32(c):
        return lax.bitcast_convert_type(c ^ ((c >> 31) & 0x7FFFFFFF), F32)

    n_visible = t0 + 1 + lax.broadcasted_iota(I32, (1, LANES), 1)
    nonneg = count0 >= topk
    lo = jnp.where(nonneg, 0, to_pattern(bottom))
    hi = jnp.where(nonneg, to_pattern(top) + 1, 0)
    count_lo = jnp.where(nonneg, count0, n_visible)
    count_hi = jnp.where(nonneg, 0, count0)
    few = n_visible < topk
    lo = jnp.where(few, PATTERN_NEG_FLT_MAX, lo)
    zero_tied = nonneg & (count_pos < topk)
    hi = jnp.where(zero_tied, 1, hi)
    count_hi = jnp.where(zero_tied, count_pos, count_hi)
    open_q = jnp.where(few | zero_tied | (count_lo == topk), 0, 1)

    log_topk = math.log(topk)

    def count_error(count):
        return jnp.log(count.astype(F32) + 0.5) - log_topk

    def probe(n_pass, carry, extract):
        lo, hi, count_lo, count_hi, err_lo, err_hi, last_side, open_q = carry
        v_lo, v_hi = to_f32(lo), to_f32(hi)
        frac = err_lo / (err_lo - err_hi)
        frac = jnp.where(count_lo - count_hi <= 4, 0.5, frac)
        guess = to_pattern(v_lo + (v_hi - v_lo) * frac)
        middle = lo + lax.shift_right_logical(hi - lo, 1)
        cand = jnp.where(n_pass >= INTERPOLATED_PASSES, middle, guess)
        cand = jnp.minimum(jnp.maximum(cand, lo + 1), hi - 1)
        is_open = open_q == 1
        if extract:
            count, under_hi = count_f32(to_f32(cand), below=v_hi)
            next_below = to_pattern(under_hi)
            found = is_open & (count_hi == topk - 1)
            is_open = is_open & jnp.logical_not(found)
        else:
            count = count_f32(to_f32(cand))
        raise_lo = is_open & (count >= topk)
        lower_hi = is_open & (count < topk)
        err = count_error(count)
        err_hi = jnp.where(raise_lo & (last_side == 1), err_hi * 0.5, err_hi)
        err_lo = jnp.where(lower_hi & (last_side == -1), err_lo * 0.5, err_lo)
        err_lo = jnp.where(raise_lo, err, err_lo)
        err_hi = jnp.where(lower_hi, err, err_hi)
        lo = jnp.where(raise_lo, cand, lo)
        count_lo = jnp.where(raise_lo, count, count_lo)
        hi = jnp.where(lower_hi, cand, hi)
        count_hi = jnp.where(lower_hi, count, count_hi)
        last_side = jnp.where(raise_lo, 1, jnp.where(lower_hi, -1, last_side))
        if extract:
            lo = jnp.where(found, next_below, lo)
            hi = jnp.where(found | raise_lo, next_below + 1, hi)
        closed = (count_lo == topk) | (hi - lo == 1)
        return lo, hi, count_lo, count_hi, err_lo, err_hi, last_side, jnp.where(closed, 0, open_q)

    carry = (lo, hi, count_lo, count_hi, count_error(count_lo), count_error(count_hi),
             jnp.zeros((1, LANES), I32), open_q)
    n_first = jnp.where(jnp.sum(open_q) > 0, FIRST_CHECK_PASSES, 0)
    carry = lax.fori_loop(0, n_first, functools.partial(probe, extract=False), carry)

    def probes_left(st):
        n_pass, n_open = st[0], st[-1]
        return (n_pass < MAX_SEARCH_PASSES) & (n_open > 0)

    def extracting_probe(st):
        carry = probe(st[0], st[1:-1], extract=True)
        return (st[0] + 1,) + carry + (jnp.sum(carry[-1]),)

    state = lax.while_loop(probes_left, extracting_probe,
                           (jnp.int32(FIRST_CHECK_PASSES),) + carry + (jnp.sum(carry[-1]),))
    lo, count_lo, count_hi = state[1], state[3], state[4]
    tau = to_f32(lo)
    need = jnp.where((count_lo == topk) | few, topk, topk - count_hi).astype(F32)

    m_scr[...] = jnp.full(m_scr.shape, MASKED_LOGIT, F32)
    l_scr[...] = jnp.zeros(l_scr.shape, F32)
    acc_scr[...] = jnp.zeros(acc_scr.shape, F32)
    ones_rows = jnp.ones((16, KEY_TILE), BF16)
    last = n_kt - 1

    def mask_pair(j_first, parity):
        xs = [score_scr[pl.ds(key_tile(jnp.minimum(j_first + slot, last)), KEY_TILE), :] for slot in range(2)]
        ties = [x == tau for x in xs]
        tie_cols = jnp.concatenate([jnp.where(tie, 1.0, 0.0).astype(BF16) for tie in ties], axis=1)
        ranks = _dot(tri_ref[...], tie_cols)
        tie_carry = tie_scr[0:1, :]
        for slot, (x, tie) in enumerate(zip(xs, ties)):
            rank = ranks[:, slot * LANES:(slot + 1) * LANES] + tie_carry
            tie_carry = rank[KEY_TILE - 1:KEY_TILE, :]
            take = (tie & (rank <= need)) | (x > tau)
            mask_scr[parity, slot] = jnp.where(take, 0.0, MASKED_LOGIT)
        tie_scr[0:1, :] = tie_carry

    def qk_pair(slot, j, p):
        s_scr[slot, :, p * pair_w:(p + 1) * pair_w] = _dot(
            k_ref[pl.ds(key_tile(j), KEY_TILE), p * LANES:(p + 1) * LANES], wq_scr[p])

    def softmax_pair(slot, j, p, near, parity):
        mask_add = mask_scr[parity, slot]
        alphas = []
        for h in (2 * p, 2 * p + 1):
            cols = slice(h * LANES, (h + 1) * LANES)
            logits = s_scr[slot, :, cols] + mask_add
            if near:
                kind_top = jnp.clip(2 * j - i + 2, 0, 2)
                kind_bot = jnp.clip(2 * j - i + 3, 0, 2)
                logits = logits + jnp.concatenate(
                    [bias_ref[kind_top, :, cols], bias_ref[kind_bot, :, cols]], axis=0)
            m_old = m_scr[h:h + 1, :]
            m_new = jnp.maximum(m_old, jnp.max(logits, axis=0, keepdims=True))
            m_scr[h:h + 1, :] = m_new
            p_scr[slot, :, cols] = jnp.exp2(logits - m_new).astype(BF16)
            alphas.append(jnp.exp2(m_old - m_new))
        return alphas

    def pv_pair(slot, j, p, alphas):
        lhs = jnp.concatenate([vt_ref[p * 2 * hd:(p + 1) * 2 * hd, pl.ds(key_tile(j), KEY_TILE)], ones_rows],
                              axis=0)
        out = _dot(lhs, p_scr[slot, :, p * pair_w:(p + 1) * pair_w])
        for half in range(2):
            h = 2 * p + half
            rows_h = slice(h * hd, (h + 1) * hd)
            q_cols = slice(half * LANES, (half + 1) * LANES)
            acc_scr[rows_h, :] = acc_scr[rows_h, :] * alphas[half] + out[half * hd:(half + 1) * hd, q_cols]
            l_scr[h:h + 1, :] = l_scr[h:h + 1, :] * alphas[half] + out[2 * hd:2 * hd + 1, q_cols]

    def pending_alphas(p):
        return [alpha_scr[h:h + 1, :] for h in (2 * p, 2 * p + 1)]

    def clear_pending():
        p_scr[1] = jnp.zeros(p_scr.shape[1:], BF16)
        alpha_scr[...] = jnp.ones(alpha_scr.shape, F32)

    def pair_step(ja, j_pending, j_next, near, parity):
        alphas_a = []
        for p in range(n_pairs):
            pv_pair(1, j_pending, p, pending_alphas(p))
            qk_pair(1, ja + 1, p)
            alphas_a.append(softmax_pair(0, ja, p, near, parity))
        for p in range(n_pairs):
            pv_pair(0, ja, p, alphas_a[p])
            qk_pair(0, j_next, p)
            alphas_b = softmax_pair(1, ja + 1, p, near, parity)
            for half in range(2):
                alpha_scr[2 * p + half:2 * p + half + 1, :] = alphas_b[half]
        mask_pair(ja + 2, 1 - parity)

    def single_step(ja, j_pending, near, parity):
        alphas_a = []
        for p in range(n_pairs):
            pv_pair(1, j_pending, p, pending_alphas(p))
            alphas_a.append(softmax_pair(0, ja, p, near, parity))
        for p in range(n_pairs):
            pv_pair(0, ja, p, alphas_a[p])
        clear_pending()

    n_far = 2 * (jnp.maximum(n_kt - 2, 0) // 2)
    n_near = n_kt - n_far
    first_near_parity = (n_far // 2) % 2
    clear_pending()
    tie_scr[...] = jnp.zeros(tie_scr.shape, F32)
    mask_pair(0, 0)
    for p in range(n_pairs):
        qk_pair(0, 0, p)

    def far_body(jp, carry):
        pair_step(2 * jp, jnp.maximum(2 * jp - 1, 0), 2 * jp + 2, near=False, parity=jp % 2)
        return carry

    lax.fori_loop(0, n_far // 2, far_body, 0)

    @pl.when(n_near >= 2)
    def _():
        pair_step(n_far, jnp.maximum(n_far - 1, 0), jnp.minimum(n_far + 2, last), near=True,
                  parity=first_near_parity)

    @pl.when(n_near % 2 == 1)
    def _():
        single_step(last, jnp.where(n_near == 3, n_far + 1, jnp.maximum(n_far - 1, 0)), near=True,
                    parity=jnp.where(n_near == 3, 1 - first_near_parity, first_near_parity))

    for p in range(n_pairs):
        pv_pair(1, last, p, pending_alphas(p))

    for h in range(N_ATTN_HEADS):
        rows_h = slice(h * hd, (h + 1) * hd)
        acc_scr[rows_h, :] = acc_scr[rows_h, :] / l_scr[h:h + 1, :]
    o_ref[...] = acc_scr[...].T


def _dsa_kernel(*refs, topk, t_valid):
    o_ref = refs[8]
    i = pl.program_id(1)
    is_real = i * LANES < t_valid

    @pl.when(is_real)
    def _():
        _dsa_block(i, *refs, topk=topk)

    @pl.when(jnp.logical_not(is_real))
    def _():
        o_ref[...] = jnp.zeros(o_ref.shape, F32)


def _dsa_attention(qit, rows, qt, kidx, k, vt, bias, tri, batch, tp, t_valid, topk):
    n_rows = k.shape[0]
    nqb = tp // LANES
    d_idx = N_IDX_HEADS * IDX_DIM
    n_pairs = N_ATTN_HEADS // 2
    key_rows = -(-tp // SCAN_TILE) * SCAN_TILE
    q_cols = lambda height: pl.BlockSpec((None, height, LANES), lambda b, i: (b, 0, i))
    return pl.pallas_call(
        functools.partial(_dsa_kernel, topk=topk, t_valid=t_valid),
        out_shape=jax.ShapeDtypeStruct((n_rows, D_ATTN), F32),
        grid=(batch, nqb),
        in_specs=[q_cols(d_idx),
                  pl.BlockSpec((16, LANES), lambda b, i: (0, b * nqb + i)),
                  q_cols(D_ATTN),
                  pl.BlockSpec((tp, LANES), lambda b, i: (b, 0)),
                  pl.BlockSpec((tp, D_ATTN), lambda b, i: (b, 0)),
                  pl.BlockSpec((None, D_ATTN, tp), lambda b, i: (b, 0, 0)),
                  pl.BlockSpec((3, LANES, N_ATTN_HEADS * LANES), lambda b, i: (0, 0, 0)),
                  pl.BlockSpec((KEY_TILE, KEY_TILE), lambda b, i: (0, 0))],
        out_specs=pl.BlockSpec((LANES, D_ATTN), lambda b, i: (b * nqb + i, 0)),
        scratch_shapes=[pltpu.VMEM((key_rows, LANES), F32),
                        pltpu.VMEM((LANES, N_IDX_HEADS * LANES), BF16),
                        pltpu.VMEM((n_pairs, LANES, 2 * LANES), BF16),
                        pltpu.VMEM((2, KEY_TILE, N_ATTN_HEADS * LANES), F32),
                        pltpu.VMEM((2, KEY_TILE, N_ATTN_HEADS * LANES), BF16),
                        pltpu.VMEM((N_ATTN_HEADS, LANES), F32),
                        pltpu.VMEM((N_ATTN_HEADS, LANES), F32),
                        pltpu.VMEM((D_ATTN, LANES), F32),
                        pltpu.VMEM((N_ATTN_HEADS, LANES), F32),
                        pltpu.VMEM((2, 2, KEY_TILE, LANES), F32),
                        pltpu.VMEM((8, LANES), F32)],
        compiler_params=_params("parallel", "parallel"),
        name="dsa_attention",
    )(qit, rows, qt, kidx, k, vt, bias, tri)


def _gdn_prep_kernel(x_ref, halo_ref, cw_ref, rows_ref, alog_ref, dtb_ref, q_ref, k_ref, v_ref, bg_ref, buf):
    first = pl.program_id(1) == 0
    rb = x_ref.shape[0]
    buf[0:HALO_ROWS, :] = jnp.where(first, 0.0, halo_ref[...])
    buf[HALO_ROWS:, :] = x_ref[...]
    acc = jnp.zeros((rb, 3 * D_GDN), F32)
    for tap in range(CONV_WIDTH):
        start = HALO_ROWS - (CONV_WIDTH - 1) + tap
        acc = acc + cw_ref[tap:tap + 1, :] * buf[start:start + rb, :]
    y = _silu(acc)
    for h in range(N_GDN_HEADS):
        cols = slice(h * GDN_HEAD_DIM, (h + 1) * GDN_HEAD_DIM)
        qh = y[:, cols]
        kh = y[:, D_GDN + h * GDN_HEAD_DIM:D_GDN + (h + 1) * GDN_HEAD_DIM]
        q_ref[:, cols] = (qh * lax.rsqrt(jnp.sum(qh * qh, axis=-1, keepdims=True) + EPS)
                          * (GDN_HEAD_DIM ** -0.5))
        k_ref[:, cols] = kh * lax.rsqrt(jnp.sum(kh * kh, axis=-1, keepdims=True) + EPS)
    v_ref[...] = y[:, 2 * D_GDN:]
    rows = rows_ref[...]
    beta = _sigmoid(rows[4:8, :])
    a = rows[8:12, :] + dtb_ref[...]
    softplus = jnp.maximum(a, 0.0) + jnp.log1p(jnp.exp(-jnp.abs(a)))
    bg_ref[0:4, :] = beta
    bg_ref[4:8, :] = -jnp.exp(alog_ref[...]) * softplus


def _gdn_prep(proj, rows, lw, batch, tp, rb):
    n_rows = proj.shape[0]
    nb = tp // rb
    halo_per_block = rb // HALO_ROWS
    row_spec = pl.BlockSpec((rb, D_GDN), lambda b, i: (b * nb + i, 0))
    return pl.pallas_call(
        _gdn_prep_kernel,
        out_shape=(jax.ShapeDtypeStruct((n_rows, D_GDN), F32),) * 3
        + (jax.ShapeDtypeStruct((8, n_rows), F32),),
        grid=(batch, nb),
        in_specs=[pl.BlockSpec((rb, 3 * D_GDN), lambda b, i: (b * nb + i, 0)),
                  pl.BlockSpec((HALO_ROWS, 3 * D_GDN),
                               lambda b, i: (jnp.maximum((b * nb + i) * halo_per_block - 1, 0), 0)),
                  pl.BlockSpec((CONV_WIDTH, 3 * D_GDN), lambda b, i: (0, 0)),
                  pl.BlockSpec((16, rb), lambda b, i: (0, b * nb + i)),
                  pl.BlockSpec((N_GDN_HEADS, 1), lambda b, i: (0, 0)),
                  pl.BlockSpec((N_GDN_HEADS, 1), lambda b, i: (0, 0))],
        out_specs=(row_spec, row_spec, row_spec,
                   pl.BlockSpec((8, rb), lambda b, i: (0, b * nb + i))),
        scratch_shapes=[pltpu.VMEM((HALO_ROWS + rb, 3 * D_GDN), F32)],
        compiler_params=_params("parallel", "parallel"),
        name="gdn_prep",
    )(proj, proj, lw["conv_w"], rows, lw["a_log"], lw["dt_bias"])


def _gdn_chunk_kernel(q_ref, k_ref, v_ref, bg_ref, m_ref, n_ref, p_ref, r_ref, cd_ref):
    c = LANES
    n_chunks = q_ref.shape[0] // c
    items = [(ch, h) for ch in range(n_chunks) for h in range(N_GDN_HEADS)]
    idx = range(len(items))
    row = lax.broadcasted_iota(I32, (c, c), 0)
    col = lax.broadcasted_iota(I32, (c, c), 1)
    tri = row >= col
    strict = row > col
    eye = jnp.where(row == col, 1.0, 0.0)
    lane8 = lax.broadcasted_iota(I32, (8, c), 1)
    gates, decays = [], []
    for ch in range(n_chunks):
        bg = bg_ref[:, ch * c:(ch + 1) * c]
        dec = bg
        shift = 1
        while shift < c:
            dec = dec + jnp.where(lane8 >= shift, pltpu.roll(dec, shift, 1), 0.0)
            shift *= 2
        gates.append(bg)
        decays.append(dec)

    def tokens(ref, n):
        ch, h = items[n]
        return ref[ch * c:(ch + 1) * c, h * GDN_HEAD_DIM:(h + 1) * GDN_HEAD_DIM]

    d_row = [jnp.broadcast_to(decays[ch][4 + h:5 + h, :], (c, c)) for ch, h in items]
    d_col = [d.T for d in d_row]
    beta_col = [jnp.broadcast_to(gates[ch][h:h + 1, :], (c, c)).T for ch, h in items]
    d_last = [d[:, c - 1:c] for d in d_row]
    gamma = [jnp.exp(jnp.where(tri, d_col[n] - d_row[n], MASKED_LOGIT)) for n in idx]
    exp_d = [jnp.exp(d_col[n]) for n in idx]
    k16 = [tokens(k_ref, n).astype(BF16) for n in idx]
    kb = [tokens(k_ref, n) * beta_col[n] for n in idx]
    nil = [jnp.where(strict, _dot_nt(kb[n].astype(BF16), k16[n]) * gamma[n], 0.0) for n in idx]
    nil_parts = [_split_bf16(x) for x in nil]
    inv = [eye - x for x in nil]
    power = [_dot_split(x, x) for x in nil_parts]
    steps = int(math.log2(c)) - 1
    for it in range(steps):
        power_parts = [_split_bf16(x) for x in power]
        inv = [inv[n] + _dot_split(_split_bf16(inv[n]), power_parts[n]) for n in idx]
        if it + 1 < steps:
            power = [_dot_split(x, x) for x in power_parts]
    rhs = [jnp.concatenate([kb[n] * exp_d[n], tokens(v_ref, n) * beta_col[n]], axis=1) for n in idx]
    wu = [_dot_split(_split_bf16(inv[n]), _split_bf16(rhs[n])).astype(BF16) for n in idx]
    aqk = [jnp.where(tri, _dot_nt(tokens(q_ref, n).astype(BF16), k16[n]) * gamma[n], 0.0).astype(BF16)
           for n in idx]
    kd_t = [(tokens(k_ref, n) * jnp.exp(d_last[n] - d_col[n])).T.astype(BF16) for n in idx]
    state_wu = [_dot(kd_t[n], wu[n]) for n in idx]
    out_wu = [_dot(aqk[n], wu[n]) for n in idx]
    for n, (ch, h) in enumerate(items):
        m_ref[ch, h] = (-state_wu[n][:, :c]).astype(BF16)
        n_ref[ch, h] = state_wu[n][:, c:]
        p_ref[ch, h] = (tokens(q_ref, n) * exp_d[n] - out_wu[n][:, :c]).astype(BF16)
        r_ref[ch, h] = out_wu[n][:, c:]
        cd_ref[ch, h:h + 1, :] = jnp.exp(d_last[n][0:1, :] + jnp.zeros((1, c), F32))


def _gdn_chunks(qn, kn, vv, bg):
    n_rows = qn.shape[0]
    nc = n_rows // LANES
    per_step = GDN_CHUNKS_PER_STEP
    tok = pl.BlockSpec((per_step * LANES, D_GDN), lambda c: (c, 0))
    mat = pl.BlockSpec((per_step, N_GDN_HEADS, LANES, LANES), lambda c: (c, 0, 0, 0))
    mat_shape = lambda dt: jax.ShapeDtypeStruct((nc, N_GDN_HEADS, LANES, LANES), dt)
    return pl.pallas_call(
        _gdn_chunk_kernel,
        out_shape=(mat_shape(BF16), mat_shape(F32), mat_shape(BF16), mat_shape(F32),
                   jax.ShapeDtypeStruct((nc, N_GDN_HEADS, LANES), F32)),
        grid=(nc // per_step,),
        in_specs=[tok, tok, tok, pl.BlockSpec((8, per_step * LANES), lambda c: (0, c))],
        out_specs=(mat, mat, mat, mat, pl.BlockSpec((per_step, N_GDN_HEADS, LANES), lambda c: (c, 0, 0))),
        compiler_params=_params("parallel"),
        name="gdn_chunks",
    )(qn, kn, vv, bg)


def _gdn_scan_kernel(m_ref, n_ref, p_ref, r_ref, cd_ref, o_ref, s_scr, *, batch):
    @pl.when(pl.program_id(0) == 0)
    def _():
        s_scr[...] = jnp.zeros(s_scr.shape, F32)

    for b in range(batch):
        for h in range(N_GDN_HEADS):
            s = s_scr[b, h]
            s16 = s.astype(BF16)
            o_ref[b, :, h * GDN_HEAD_DIM:(h + 1) * GDN_HEAD_DIM] = _dot(p_ref[b, h], s16) + r_ref[b, h]
            s_scr[b, h] = s * cd_ref[b, h:h + 1, :] + _dot(m_ref[b, h], s16) + n_ref[b, h]


def _gdn_scan(m_mat, n_mat, p_mat, r_mat, cd, batch, tp):
    nc = tp // LANES
    shape5 = lambda a: a.reshape(batch, nc, N_GDN_HEADS, LANES, LANES)
    mat = pl.BlockSpec((batch, None, N_GDN_HEADS, LANES, LANES), lambda c: (0, c, 0, 0, 0))
    return pl.pallas_call(
        functools.partial(_gdn_scan_kernel, batch=batch),
        out_shape=jax.ShapeDtypeStruct((batch, tp, D_GDN), F32),
        grid=(nc,),
        in_specs=[mat, mat, mat, mat,
                  pl.BlockSpec((batch, None, N_GDN_HEADS, LANES), lambda c: (0, c, 0, 0))],
        out_specs=pl.BlockSpec((batch, LANES, D_GDN), lambda c: (0, c, 0)),
        scratch_shapes=[pltpu.VMEM((batch, N_GDN_HEADS, LANES, LANES), F32)],
        compiler_params=_params("arbitrary"),
        name="gdn_scan",
    )(shape5(m_mat), shape5(n_mat), shape5(p_mat), shape5(r_mat), cd.reshape(batch, nc, N_GDN_HEADS, LANES))


def _out_proj_kernel(h_ref, oa_ref, za_ref, og_ref, zg_ref, gain_ref, w_ref, out_ref):
    attn = (oa_ref[...] * _silu(za_ref[...])).astype(BF16)
    y = _dot(attn, w_ref[0:D_ATTN, :])
    og, zg = og_ref[...], zg_ref[...]
    for h in range(N_GDN_HEADS):
        cols = slice(h * GDN_HEAD_DIM, (h + 1) * GDN_HEAD_DIM)
        gated = (_rms_rows(og[:, cols], gain_ref[...]) * _silu(zg[:, cols])).astype(BF16)
        y = y + _dot(gated, w_ref[D_ATTN + h * GDN_HEAD_DIM:D_ATTN + (h + 1) * GDN_HEAD_DIM, :])
    out_ref[...] = h_ref[...] + y


def _out_proj(h2, o_attn, proj, o_gdn, gain, w_out, rb):
    n_rows = h2.shape[0]
    blk = lambda width, col: pl.BlockSpec((rb, width), lambda i: (i, col))
    return pl.pallas_call(
        _out_proj_kernel,
        out_shape=jax.ShapeDtypeStruct((n_rows, D_MODEL), F32),
        grid=(n_rows // rb,),
        in_specs=[blk(D_MODEL, 0), blk(D_ATTN, 0), blk(D_ATTN, COL_Z_ATTN), blk(D_GDN, 0),
                  blk(D_GDN, COL_Z_GDN),
                  pl.BlockSpec((1, GDN_HEAD_DIM), lambda i: (0, 0)),
                  pl.BlockSpec((D_MODEL, D_MODEL), lambda i: (0, 0))],
        out_specs=blk(D_MODEL, 0),
        compiler_params=_params("parallel"),
        name="out_proj",
    )(h2, o_attn, proj, o_gdn, proj, gain, w_out)


def _pack_layer(norm_gain, w_in, cq_gain, ckv_gain, w_uq, w_ukv, w_q_idx, q_gain, k_gain, conv_w, a_log,
                dt_bias, gdn_gain, w_out):
    o = 0
    parts = {}
    for name, size in (("c_q", Q_RANK), ("c_kv", KV_RANK), ("k_idx", IDX_DIM), ("w_idx", N_IDX_HEADS),
                       ("z_attn", D_ATTN), ("qkv_g", 3 * D_GDN), ("z_g", D_GDN), ("b", N_GDN_HEADS),
                       ("a", N_GDN_HEADS)):
        parts[name] = w_in[:, o:o + size]
        o += size
    w_packed = jnp.concatenate([parts["qkv_g"], parts["c_q"], parts["c_kv"], parts["k_idx"], parts["k_idx"],
                                parts["z_attn"], parts["z_g"]], axis=1).astype(BF16)
    w_rows = jnp.concatenate([parts["w_idx"], parts["b"], parts["a"],
                              jnp.zeros((D_MODEL, 4), w_in.dtype)], axis=1).T.astype(BF16)
    return dict(
        gain=norm_gain[None, :], w_packed=w_packed, w_rows=w_rows,
        g_cq=cq_gain[None, :], g_ckv=ckv_gain[None, :],
        w_uqt=w_uq.T.astype(BF16), w_qit=w_q_idx.T.astype(BF16),
        w_ukt=w_ukv[:, :D_ATTN].T.astype(BF16), w_uvt=w_ukv[:, D_ATTN:].T.astype(BF16),
        g_q_col=q_gain[:, None], g_k_col=k_gain[:, None],
        conv_w=conv_w, a_log=a_log[:, None], dt_bias=dt_bias[:, None],
        gdn_gain=gdn_gain[None, :], w_out=w_out.astype(BF16))


def _layer(h2, lw, bias, tri, batch, tp, t_valid, topk):
    rb = _row_block(tp)
    proj, rows = _in_proj(h2, lw["gain"], lw["w_packed"], lw["w_rows"], rb)
    qt, k, vt, qit, kidx = _dsa_prep(proj, lw, batch, tp, rb)
    o_attn = _dsa_attention(qit, rows, qt, kidx, k, vt, bias, tri, batch, tp, t_valid, topk)
    qn, kn, vv, bg = _gdn_prep(proj, rows, lw, batch, tp, rb)
    m_mat, n_mat, p_mat, r_mat, cd = _gdn_chunks(qn, kn, vv, bg)
    o_gdn = _gdn_scan(m_mat, n_mat, p_mat, r_mat, cd, batch, tp).reshape(batch * tp, D_GDN)
    return _out_proj(h2, o_attn, proj, o_gdn, lw["gdn_gain"], lw["w_out"], rb)


def _forward(x, meta_tokens, rel_bias_table, layer_weights, topk):
    batch, seq, _ = x.shape
    t = seq + N_META
    tp = -(-t // KEY_TILE) * KEY_TILE
    meta = jnp.broadcast_to(meta_tokens[None].astype(x.dtype), (batch, N_META, D_MODEL))
    h = jnp.concatenate([meta, x, jnp.zeros((batch, tp - t, D_MODEL), x.dtype)], axis=1)
    h2 = h.reshape(batch * tp, D_MODEL)
    bias = _bias_tiles(rel_bias_table)
    tri = jnp.tril(jnp.ones((KEY_TILE, KEY_TILE), BF16))
    for lw in layer_weights:
        h2 = _layer(h2, lw, bias, tri, batch, tp, t, topk)
    return h2.reshape(batch, tp, D_MODEL)[:, N_META:t]


def kernel(x, meta_tokens, rel_bias_table, norm_gain, w_in, cq_norm_gain, ckv_norm_gain, w_uq, w_ukv, w_q_idx,
           q_norm_gain, k_norm_gain, conv_w, a_log, dt_bias, gdn_norm_gain, w_out):
    depth = norm_gain.shape[0]
    topk = min(TOPK_MAX, x.shape[1] // 4)
    layers = [_pack_layer(norm_gain[l], w_in[l], cq_norm_gain[l], ckv_norm_gain[l], w_uq[l], w_ukv[l],
                          w_q_idx[l], q_norm_gain[l], k_norm_gain[l], conv_w[l], a_log[l], dt_bias[l],
                          gdn_norm_gain[l], w_out[l]) for l in range(depth)]
    return _forward(x, meta_tokens, rel_bias_table, layers, topk)
```

```python
import functools
import math

import jax
import jax.numpy as jnp
from jax import lax
from jax.experimental import pallas as pl
from jax.experimental.pallas import tpu as pltpu

F32 = jnp.float32
BF16 = jnp.bfloat16
I32 = jnp.int32

D_MODEL = 1024
N_META = 16
EPS = 1e-6
N_ATTN_HEADS = 8
ATTN_HEAD_DIM = 64
D_ATTN = N_ATTN_HEADS * ATTN_HEAD_DIM
Q_RANK = 256
KV_RANK = 128
N_IDX_HEADS = 4
IDX_DIM = 64
TOPK_MAX = 256
N_REL_BUCKETS = 32
REL_MAX_DIST = 128
N_GDN_HEADS = 4
GDN_HEAD_DIM = 128
D_GDN = N_GDN_HEADS * GDN_HEAD_DIM
CONV_WIDTH = 4

LANES = 128
KEY_TILE = 256
COUNT_TILE = 512
SCAN_TILE = 1024
FIRST_CHECK_PASSES = 10
INTERPOLATED_PASSES = 24
MAX_SEARCH_PASSES = INTERPOLATED_PASSES + 32
GDN_CHUNKS_PER_STEP = 2
ROW_TILES = (5, 4, 3, 2, 1)
HALO_ROWS = 8
D_PACKED = 3 * D_GDN + 512 + D_ATTN + D_GDN
COL_SMALL = 3
COL_Z_ATTN = 4
COL_Z_GDN = 5
MASKED_LOGIT = -1e30
LOG2_E = math.log2(math.e)
KEY_MIN = -2 ** 31
PATTERN_NEG_FLT_MAX = KEY_MIN + (1 << 23)
VMEM_LIMIT = 56 * 1024 * 1024

NT_DIMS = (((1,), (1,)), ((), ()))


def _dot(a, b):
    return jnp.dot(a, b, preferred_element_type=F32)


def _dot_nt(a, b):
    return lax.dot_general(a, b, NT_DIMS, preferred_element_type=F32)


def _split_bf16(x):
    hi = x.astype(BF16)
    return hi, (x - hi.astype(F32)).astype(BF16)


def _dot_split(a_parts, b_parts):
    (a_hi, a_lo), (b_hi, b_lo) = a_parts, b_parts
    return _dot(a_hi, b_hi) + (_dot(a_hi, b_lo) + _dot(a_lo, b_hi))


def _sigmoid(x):
    return 1.0 / (1.0 + jnp.exp(-x))


def _silu(x):
    return x * _sigmoid(x)


def _row_block(tp):
    tiles = tp // LANES
    return LANES * next(d for d in ROW_TILES if tiles % d == 0)


def _params(*sem):
    return pltpu.CompilerParams(dimension_semantics=sem, vmem_limit_bytes=VMEM_LIMIT)


def _bias_kernel(table_ref, out_ref):
    row = lax.broadcasted_iota(I32, (LANES, LANES), 0)
    col = lax.broadcasted_iota(I32, (LANES, LANES), 1)
    max_exact = N_REL_BUCKETS // 2
    for kind in range(3):
        dist = col - row + (2 - kind) * LANES
        n = jnp.maximum(dist, 0)
        nf = jnp.maximum(n, 1).astype(F32)
        large = max_exact + (jnp.log(nf / max_exact) / math.log(REL_MAX_DIST / max_exact)
                             * (N_REL_BUCKETS - max_exact)).astype(I32)
        large = jnp.minimum(large, N_REL_BUCKETS - 1)
        bucket = jnp.where(n < max_exact, n, large)
        for h in range(N_ATTN_HEADS):
            tile = jnp.zeros((LANES, LANES), F32)
            for b in range(N_REL_BUCKETS):
                tile = jnp.where(bucket == b, table_ref[b, h], tile)
            far = table_ref[N_REL_BUCKETS - 1, h]
            out_ref[kind, :, h * LANES:(h + 1) * LANES] = (tile - far) * LOG2_E


def _bias_tiles(rel_table):
    return pl.pallas_call(
        _bias_kernel,
        out_shape=jax.ShapeDtypeStruct((3, LANES, N_ATTN_HEADS * LANES), F32),
        in_specs=[pl.BlockSpec(memory_space=pltpu.SMEM)],
        out_specs=pl.BlockSpec(memory_space=pltpu.VMEM),
        name="rel_bias_tiles",
    )(rel_table)


def _in_proj_kernel(h_ref, gain_ref, w_ref, wrows_ref, proj_ref, rows_ref):
    x = h_ref[...]
    y = x * lax.rsqrt(jnp.mean(x * x, axis=-1, keepdims=True) + EPS)
    hn = (y * gain_ref[...]).astype(BF16)
    proj_ref[...] = _dot(hn, w_ref[...])
    rows_ref[...] = _dot_nt(wrows_ref[...], hn)


def _in_proj(h2, gain, w_packed, w_rows, rb):
    n_rows = h2.shape[0]
    grid = (n_rows // rb,)
    return pl.pallas_call(
        _in_proj_kernel,
        out_shape=(jax.ShapeDtypeStruct((n_rows, D_PACKED), F32),
                   jax.ShapeDtypeStruct((16, n_rows), F32)),
        grid=grid,
        in_specs=[pl.BlockSpec((rb, D_MODEL), lambda i: (i, 0)),
                  pl.BlockSpec((1, D_MODEL), lambda i: (0, 0)),
                  pl.BlockSpec((D_MODEL, D_PACKED), lambda i: (0, 0)),
                  pl.BlockSpec((16, D_MODEL), lambda i: (0, 0))],
        out_specs=(pl.BlockSpec((rb, D_PACKED), lambda i: (i, 0)),
                   pl.BlockSpec((16, rb), lambda i: (0, i))),
        compiler_params=_params("parallel"),
        name="in_proj",
    )(h2, gain, w_packed, w_rows)


def _rms_rows(x, gain):
    return x * lax.rsqrt(jnp.mean(x * x, axis=-1, keepdims=True) + EPS) * gain


def _dsa_prep_kernel(sm_ref, gcq_ref, gckv_ref, wuqt_ref, wqit_ref, wukt_ref, wuvt_ref, gq_ref, gk_ref,
                     qt_ref, k_ref, vt_ref, qit_ref, kidx_ref):
    sm = sm_ref[...]
    rb = sm.shape[0]
    cq = _rms_rows(sm[:, :Q_RANK], gcq_ref[...]).astype(BF16)
    ckv = _rms_rows(sm[:, Q_RANK:Q_RANK + KV_RANK], gckv_ref[...]).astype(BF16)
    kidx_ref[...] = sm[:, Q_RANK + KV_RANK:].astype(BF16)
    q3 = _dot_nt(wuqt_ref[...], cq).reshape(N_ATTN_HEADS, ATTN_HEAD_DIM, rb)
    q3 = q3 * lax.rsqrt(jnp.mean(q3 * q3, axis=1, keepdims=True) + EPS) * gq_ref[...][None]
    qt_ref[...] = (q3 * (ATTN_HEAD_DIM ** -0.5 * LOG2_E)).reshape(D_ATTN, rb).astype(BF16)
    k3 = _dot_nt(wukt_ref[...], ckv).reshape(N_ATTN_HEADS, ATTN_HEAD_DIM, rb)
    k3 = k3 * lax.rsqrt(jnp.mean(k3 * k3, axis=1, keepdims=True) + EPS) * gk_ref[...][None]
    k_ref[...] = k3.reshape(D_ATTN, rb).T.astype(BF16)
    vt_ref[...] = _dot_nt(wuvt_ref[...], ckv).astype(BF16)
    qit_ref[...] = _dot_nt(wqit_ref[...], cq).astype(BF16)


def _dsa_prep(proj, lw, batch, tp, rb):
    n_rows = proj.shape[0]
    nb = tp // rb
    d_idx = N_IDX_HEADS * IDX_DIM
    const = lambda shape: pl.BlockSpec(shape, lambda b, i: (0, 0))
    row_spec = lambda width: pl.BlockSpec((rb, width), lambda b, i: (b * nb + i, 0))
    col_spec = lambda height: pl.BlockSpec((None, height, rb), lambda b, i: (b, 0, i))
    return pl.pallas_call(
        _dsa_prep_kernel,
        out_shape=(jax.ShapeDtypeStruct((batch, D_ATTN, tp), BF16),
                   jax.ShapeDtypeStruct((n_rows, D_ATTN), BF16),
                   jax.ShapeDtypeStruct((batch, D_ATTN, tp), BF16),
                   jax.ShapeDtypeStruct((batch, d_idx, tp), BF16),
                   jax.ShapeDtypeStruct((n_rows, LANES), BF16)),
        grid=(batch, nb),
        in_specs=[pl.BlockSpec((rb, 512), lambda b, i: (b * nb + i, COL_SMALL)),
                  const((1, Q_RANK)), const((1, KV_RANK)),
                  const((D_ATTN, Q_RANK)), const((d_idx, Q_RANK)),
                  const((D_ATTN, KV_RANK)), const((D_ATTN, KV_RANK)),
                  const((ATTN_HEAD_DIM, 1)), const((ATTN_HEAD_DIM, 1))],
        out_specs=(col_spec(D_ATTN), row_spec(D_ATTN), col_spec(D_ATTN), col_spec(d_idx), row_spec(LANES)),
        compiler_params=_params("parallel", "parallel"),
        name="dsa_prep",
    )(proj, lw["g_cq"], lw["g_ckv"], lw["w_uqt"], lw["w_qit"], lw["w_ukt"], lw["w_uvt"], lw["g_q_col"],
      lw["g_k_col"])


def _dsa_block(i, qit_ref, rows_ref, qt_ref, kidx_ref, k_ref, vt_ref, bias_ref, tri_ref, o_ref,
               score_scr, wi_scr, wq_scr, s_scr, p_scr, m_scr, l_scr, acc_scr, alpha_scr, mask_scr, tie_scr, *, topk):
    t0 = i * LANES
    n_kt = i // 2 + 1
    n_st = (n_kt + 3) // 4
    hd = ATTN_HEAD_DIM
    pair_w = 2 * LANES
    n_pairs = N_ATTN_HEADS // 2

    zeros_hd = jnp.zeros((hd, LANES), BF16)
    for h in range(N_IDX_HEADS):
        wi_scr[0:IDX_DIM, h * LANES:(h + 1) * LANES] = qit_ref[h * IDX_DIM:(h + 1) * IDX_DIM, :]
    wi_scr[IDX_DIM:, :] = jnp.zeros((LANES - IDX_DIM, N_IDX_HEADS * LANES), BF16)
    for p in range(n_pairs):
        wq_scr[p, 0:hd, 0:LANES] = qt_ref[2 * p * hd:(2 * p + 1) * hd, :]
        wq_scr[p, 0:hd, LANES:] = zeros_hd
        wq_scr[p, hd:, 0:LANES] = zeros_hd
        wq_scr[p, hd:, LANES:] = qt_ref[(2 * p + 1) * hd:(2 * p + 2) * hd, :]

    row = lax.broadcasted_iota(I32, (KEY_TILE, LANES), 0)
    col = lax.broadcasted_iota(I32, (KEY_TILE, LANES), 1)
    w_idx = rows_ref[0:N_IDX_HEADS, :] * (N_IDX_HEADS ** -0.5 * IDX_DIM ** -0.5)

    def key_tile(j):
        return pl.multiple_of(j * KEY_TILE, KEY_TILE)

    def causal(j):
        return (j * KEY_TILE + row) <= (t0 + col)

    def fold8(x, op):
        return op(x.reshape(KEY_TILE // 8, 8, LANES), axis=0)

    def score_step(jc, carry, masked):
        top, bottom, n_nonneg, n_pos = carry
        subs = [jc * (SCAN_TILE // KEY_TILE) + sub for sub in range(SCAN_TILE // KEY_TILE)]
        logits = [_dot(kidx_ref[pl.ds(key_tile(jnp.minimum(j, n_kt - 1)), KEY_TILE), :], wi_scr[...])
                  for j in subs]
        for j, lg in zip(subs, logits):
            score = jnp.zeros((KEY_TILE, LANES), F32)
            for h in range(N_IDX_HEADS):
                score = score + jnp.maximum(lg[:, h * LANES:(h + 1) * LANES], 0.0) * w_idx[h:h + 1, :]
            if masked:
                visible = causal(j)
                seen = jnp.where(visible, score, -jnp.inf)
                bottom = jnp.minimum(bottom, fold8(jnp.where(visible, score, jnp.inf), jnp.min))
            else:
                seen = score
                bottom = jnp.minimum(bottom, fold8(score, jnp.min))
            top = jnp.maximum(top, fold8(seen, jnp.max))
            n_nonneg = n_nonneg + fold8(jnp.where(seen >= 0.0, 1, 0), jnp.sum)
            n_pos = n_pos + fold8(jnp.where(seen > 0.0, 1, 0), jnp.sum)
            score_scr[pl.ds(key_tile(j), KEY_TILE), :] = seen
        return top, bottom, n_nonneg, n_pos

    zeros8 = jnp.zeros((8, LANES), I32)
    carry = (jnp.full((8, LANES), -jnp.inf, F32), jnp.full((8, LANES), jnp.inf, F32), zeros8, zeros8)
    carry = lax.fori_loop(0, n_st - 1, functools.partial(score_step, masked=False), carry)
    top, bottom, n_nonneg, n_pos = score_step(n_st - 1, carry, masked=True)
    top = jnp.max(top, axis=0, keepdims=True)
    bottom = jnp.min(bottom, axis=0, keepdims=True)
    count0 = jnp.sum(n_nonneg, axis=0, keepdims=True)
    count_pos = jnp.sum(n_pos, axis=0, keepdims=True)

    def count_f32(cand, below=None):
        def body(j, carry):
            acc, best = carry
            for part in range(SCAN_TILE // COUNT_TILE):
                start = pl.multiple_of(j * SCAN_TILE + part * COUNT_TILE, COUNT_TILE)
                x = score_scr[pl.ds(start, COUNT_TILE), :]
                ind = jnp.where(x >= cand, 1, 0)
                acc = acc + jnp.sum(ind.reshape(COUNT_TILE // 8, 8, LANES), axis=0)
                if below is not None:
                    under = jnp.where(x < below, x, -jnp.inf)
                    best = jnp.maximum(best, jnp.max(under.reshape(COUNT_TILE // 8, 8, LANES), axis=0))
            return acc, best
        acc, best = lax.fori_loop(0, n_st, body, (jnp.zeros((8, LANES), I32), jnp.full((8, LANES), -jnp.inf, F32)))
        count = jnp.sum(acc, axis=0, keepdims=True)
        if below is None:
            return count
        return count, jnp.max(best, axis=0, keepdims=True)

    def to_pattern(v):
        bits = lax.bitcast_convert_type(v, I32)
        return bits ^ ((bits >> 31) & 0x7FFFFFFF)

    def to_f32(c):
        return lax.bitcast_convert_type(c ^ ((c >> 31) & 0x7FFFFFFF), F32)

    n_visible = t0 + 1 + lax.broadcasted_iota(I32, (1, LANES), 1)
    nonneg = count0 >= topk
    lo = jnp.where(nonneg, 0, to_pattern(bottom))
    hi = jnp.where(nonneg, to_pattern(top) + 1, 0)
    count_lo = jnp.where(nonneg, count0, n_visible)
    count_hi = jnp.where(nonneg, 0, count0)
    few = n_visible < topk
    lo = jnp.where(few, PATTERN_NEG_FLT_MAX, lo)
    zero_tied = nonneg & (count_pos < topk)
    hi = jnp.where(zero_tied, 1, hi)
    count_hi = jnp.where(zero_tied, count_pos, count_hi)
    open_q = jnp.where(few | zero_tied | (count_lo == topk), 0, 1)

    log_topk = math.log(topk)

    def count_error(count):
        return jnp.log(count.astype(F32) + 0.5) - log_topk

    def probe(n_pass, carry, extract):
        lo, hi, count_lo, count_hi, err_lo, err_hi, last_side, open_q = carry
        v_lo, v_hi = to_f32(lo), to_f32(hi)
        frac = err_lo / (err_lo - err_hi)
        frac = jnp.where(count_lo - count_hi <= 4, 0.5, frac)
        guess = to_pattern(v_lo + (v_hi - v_lo) * frac)
        middle = lo + lax.shift_right_logical(hi - lo, 1)
        cand = jnp.where(n_pass >= INTERPOLATED_PASSES, middle, guess)
        cand = jnp.minimum(jnp.maximum(cand, lo + 1), hi - 1)
        is_open = open_q == 1
        if extract:
            count, under_hi = count_f32(to_f32(cand), below=v_hi)
            next_below = to_pattern(under_hi)
            found = is_open & (count_hi == topk - 1)
            is_open = is_open & jnp.logical_not(found)
        else:
            count = count_f32(to_f32(cand))
        raise_lo = is_open & (count >= topk)
        lower_hi = is_open & (count < topk)
        err = count_error(count)
        err_hi = jnp.where(raise_lo & (last_side == 1), err_hi * 0.5, err_hi)
        err_lo = jnp.where(lower_hi & (last_side == -1), err_lo * 0.5, err_lo)
        err_lo = jnp.where(raise_lo, err, err_lo)
        err_hi = jnp.where(lower_hi, err, err_hi)
        lo = jnp.where(raise_lo, cand, lo)
        count_lo = jnp.where(raise_lo, count, count_lo)
        hi = jnp.where(lower_hi, cand, hi)
        count_hi = jnp.where(lower_hi, count, count_hi)
        last_side = jnp.where(raise_lo, 1, jnp.where(lower_hi, -1, last_side))
        if extract:
            lo = jnp.where(found, next_below, lo)
            hi = jnp.where(found | raise_lo, next_below + 1, hi)
        closed = (count_lo == topk) | (hi - lo == 1)
        return lo, hi, count_lo, count_hi, err_lo, err_hi, last_side, jnp.where(closed, 0, open_q)

    carry = (lo, hi, count_lo, count_hi, count_error(count_lo), count_error(count_hi),
             jnp.zeros((1, LANES), I32), open_q)
    n_first = jnp.where(jnp.sum(open_q) > 0, FIRST_CHECK_PASSES, 0)
    carry = lax.fori_loop(0, n_first, functools.partial(probe, extract=False), carry)

    def probes_left(st):
        n_pass, n_open = st[0], st[-1]
        return (n_pass < MAX_SEARCH_PASSES) & (n_open > 0)

    def extracting_probe(st):
        carry = probe(st[0], st[1:-1], extract=True)
        return (st[0] + 1,) + carry + (jnp.sum(carry[-1]),)

    state = lax.while_loop(probes_left, extracting_probe,
                           (jnp.int32(FIRST_CHECK_PASSES),) + carry + (jnp.sum(carry[-1]),))
    lo, count_lo, count_hi = state[1], state[3], state[4]
    tau = to_f32(lo)
    need = jnp.where((count_lo == topk) | few, topk, topk - count_hi).astype(F32)

    m_scr[...] = jnp.full(m_scr.shape, MASKED_LOGIT, F32)
    l_scr[...] = jnp.zeros(l_scr.shape, F32)
    acc_scr[...] = jnp.zeros(acc_scr.shape, F32)
    ones_rows = jnp.ones((16, KEY_TILE), BF16)
    last = n_kt - 1

    def mask_pair(j_first, parity):
        xs = [score_scr[pl.ds(key_tile(jnp.minimum(j_first + slot, last)), KEY_TILE), :] for slot in range(2)]
        ties = [x == tau for x in xs]
        tie_cols = jnp.concatenate([jnp.where(tie, 1.0, 0.0).astype(BF16) for tie in ties], axis=1)
        ranks = _dot(tri_ref[...], tie_cols)
        tie_carry = tie_scr[0:1, :]
        for slot, (x, tie) in enumerate(zip(xs, ties)):
            rank = ranks[:, slot * LANES:(slot + 1) * LANES] + tie_carry
            tie_carry = rank[KEY_TILE - 1:KEY_TILE, :]
            take = (tie & (rank <= need)) | (x > tau)
            mask_scr[parity, slot] = jnp.where(take, 0.0, MASKED_LOGIT)
        tie_scr[0:1, :] = tie_carry

    def qk_pair(slot, j, p):
        s_scr[slot, :, p * pair_w:(p + 1) * pair_w] = _dot(
            k_ref[pl.ds(key_tile(j), KEY_TILE), p * LANES:(p + 1) * LANES], wq_scr[p])

    def softmax_pair(slot, j, p, near, parity):
        mask_add = mask_scr[parity, slot]
        alphas = []
        for h in (2 * p, 2 * p + 1):
            cols = slice(h * LANES, (h + 1) * LANES)
            logits = s_scr[slot, :, cols] + mask_add
            if near:
                kind_top = jnp.clip(2 * j - i + 2, 0, 2)
                kind_bot = jnp.clip(2 * j - i + 3, 0, 2)
                logits = logits + jnp.concatenate(
                    [bias_ref[kind_top, :, cols], bias_ref[kind_bot, :, cols]], axis=0)
            m_old = m_scr[h:h + 1, :]
            m_new = jnp.maximum(m_old, jnp.max(logits, axis=0, keepdims=True))
            m_scr[h:h + 1, :] = m_new
            p_scr[slot, :, cols] = jnp.exp2(logits - m_new).astype(BF16)
            alphas.append(jnp.exp2(m_old - m_new))
        return alphas

    def pv_pair(slot, j, p, alphas):
        lhs = jnp.concatenate([vt_ref[p * 2 * hd:(p + 1) * 2 * hd, pl.ds(key_tile(j), KEY_TILE)], ones_rows],
                              axis=0)
        out = _dot(lhs, p_scr[slot, :, p * pair_w:(p + 1) * pair_w])
        for half in range(2):
            h = 2 * p + half
            rows_h = slice(h * hd, (h + 1) * hd)
            q_cols = slice(half * LANES, (half + 1) * LANES)
            acc_scr[rows_h, :] = acc_scr[rows_h, :] * alphas[half] + out[half * hd:(half + 1) * hd, q_cols]
            l_scr[h:h + 1, :] = l_scr[h:h + 1, :] * alphas[half] + out[2 * hd:2 * hd + 1, q_cols]

    def pending_alphas(p):
        return [alpha_scr[h:h + 1, :] for h in (2 * p, 2 * p + 1)]

    def clear_pending():
        p_scr[1] = jnp.zeros(p_scr.shape[1:], BF16)
        alpha_scr[...] = jnp.ones(alpha_scr.shape, F32)

    def pair_step(ja, j_pending, j_next, near, parity):
        alphas_a = []
        for p in range(n_pairs):
            pv_pair(1, j_pending, p, pending_alphas(p))
            qk_pair(1, ja + 1, p)
            alphas_a.append(softmax_pair(0, ja, p, near, parity))
        for p in range(n_pairs):
            pv_pair(0, ja, p, alphas_a[p])
            qk_pair(0, j_next, p)
            alphas_b = softmax_pair(1, ja + 1, p, near, parity)
            for half in range(2):
                alpha_scr[2 * p + half:2 * p + half + 1, :] = alphas_b[half]
        mask_pair(ja + 2, 1 - parity)

    def single_step(ja, j_pending, near, parity):
        alphas_a = []
        for p in range(n_pairs):
            pv_pair(1, j_pending, p, pending_alphas(p))
            alphas_a.append(softmax_pair(0, ja, p, near, parity))
        for p in range(n_pairs):
            pv_pair(0, ja, p, alphas_a[p])
        clear_pending()

    n_far = 2 * (jnp.maximum(n_kt - 2, 0) // 2)
    n_near = n_kt - n_far
    first_near_parity = (n_far // 2) % 2
    clear_pending()
    tie_scr[...] = jnp.zeros(tie_scr.shape, F32)
    mask_pair(0, 0)
    for p in range(n_pairs):
        qk_pair(0, 0, p)

    def far_body(jp, carry):
        pair_step(2 * jp, jnp.maximum(2 * jp - 1, 0), 2 * jp + 2, near=False, parity=jp % 2)
        return carry

    lax.fori_loop(0, n_far // 2, far_body, 0)

    @pl.when(n_near >= 2)
    def _():
        pair_step(n_far, jnp.maximum(n_far - 1, 0), jnp.minimum(n_far + 2, last), near=True,
                  parity=first_near_parity)

    @pl.when(n_near % 2 == 1)
    def _():
        single_step(last, jnp.where(n_near == 3, n_far + 1, jnp.maximum(n_far - 1, 0)), near=True,
                    parity=jnp.where(n_near == 3, 1 - first_near_parity, first_near_parity))

    for p in range(n_pairs):
        pv_pair(1, last, p, pending_alphas(p))

    for h in range(N_ATTN_HEADS):
        rows_h = slice(h * hd, (h + 1) * hd)
        acc_scr[rows_h, :] = acc_scr[rows_h, :] / l_scr[h:h + 1, :]
    o_ref[...] = acc_scr[...].T


def _dsa_kernel(*refs, topk, t_valid):
    o_ref = refs[8]
    i = pl.program_id(1)
    is_real = i * LANES < t_valid

    @pl.when(is_real)
    def _():
        _dsa_block(i, *refs, topk=topk)

    @pl.when(jnp.logical_not(is_real))
    def _():
        o_ref[...] = jnp.zeros(o_ref.shape, F32)


def _dsa_attention(qit, rows, qt, kidx, k, vt, bias, tri, batch, tp, t_valid, topk):
    n_rows = k.shape[0]
    nqb = tp // LANES
    d_idx = N_IDX_HEADS * IDX_DIM
    n_pairs = N_ATTN_HEADS // 2
    key_rows = -(-tp // SCAN_TILE) * SCAN_TILE
    q_cols = lambda height: pl.BlockSpec((None, height, LANES), lambda b, i: (b, 0, i))
    return pl.pallas_call(
        functools.partial(_dsa_kernel, topk=topk, t_valid=t_valid),
        out_shape=jax.ShapeDtypeStruct((n_rows, D_ATTN), F32),
        grid=(batch, nqb),
        in_specs=[q_cols(d_idx),
                  pl.BlockSpec((16, LANES), lambda b, i: (0, b * nqb + i)),
                  q_cols(D_ATTN),
                  pl.BlockSpec((tp, LANES), lambda b, i: (b, 0)),
                  pl.BlockSpec((tp, D_ATTN), lambda b, i: (b, 0)),
                  pl.BlockSpec((None, D_ATTN, tp), lambda b, i: (b, 0, 0)),
                  pl.BlockSpec((3, LANES, N_ATTN_HEADS * LANES), lambda b, i: (0, 0, 0)),
                  pl.BlockSpec((KEY_TILE, KEY_TILE), lambda b, i: (0, 0))],
        out_specs=pl.BlockSpec((LANES, D_ATTN), lambda b, i: (b * nqb + i, 0)),
        scratch_shapes=[pltpu.VMEM((key_rows, LANES), F32),
                        pltpu.VMEM((LANES, N_IDX_HEADS * LANES), BF16),
                        pltpu.VMEM((n_pairs, LANES, 2 * LANES), BF16),
                        pltpu.VMEM((2, KEY_TILE, N_ATTN_HEADS * LANES), F32),
                        pltpu.VMEM((2, KEY_TILE, N_ATTN_HEADS * LANES), BF16),
                        pltpu.VMEM((N_ATTN_HEADS, LANES), F32),
                        pltpu.VMEM((N_ATTN_HEADS, LANES), F32),
                        pltpu.VMEM((D_ATTN, LANES), F32),
                        pltpu.VMEM((N_ATTN_HEADS, LANES), F32),
                        pltpu.VMEM((2, 2, KEY_TILE, LANES), F32),
                        pltpu.VMEM((8, LANES), F32)],
        compiler_params=_params("parallel", "parallel"),
        name="dsa_attention",
    )(qit, rows, qt, kidx, k, vt, bias, tri)


def _gdn_prep_kernel(x_ref, halo_ref, cw_ref, rows_ref, alog_ref, dtb_ref, q_ref, k_ref, v_ref, bg_ref, buf):
    first = pl.program_id(1) == 0
    rb = x_ref.shape[0]
    buf[0:HALO_ROWS, :] = jnp.where(first, 0.0, halo_ref[...])
    buf[HALO_ROWS:, :] = x_ref[...]
    acc = jnp.zeros((rb, 3 * D_GDN), F32)
    for tap in range(CONV_WIDTH):
        start = HALO_ROWS - (CONV_WIDTH - 1) + tap
        acc = acc + cw_ref[tap:tap + 1, :] * buf[start:start + rb, :]
    y = _silu(acc)
    for h in range(N_GDN_HEADS):
        cols = slice(h * GDN_HEAD_DIM, (h + 1) * GDN_HEAD_DIM)
        qh = y[:, cols]
        kh = y[:, D_GDN + h * GDN_HEAD_DIM:D_GDN + (h + 1) * GDN_HEAD_DIM]
        q_ref[:, cols] = (qh * lax.rsqrt(jnp.sum(qh * qh, axis=-1, keepdims=True) + EPS)
                          * (GDN_HEAD_DIM ** -0.5))
        k_ref[:, cols] = kh * lax.rsqrt(jnp.sum(kh * kh, axis=-1, keepdims=True) + EPS)
    v_ref[...] = y[:, 2 * D_GDN:]
    rows = rows_ref[...]
    beta = _sigmoid(rows[4:8, :])
    a = rows[8:12, :] + dtb_ref[...]
    softplus = jnp.maximum(a, 0.0) + jnp.log1p(jnp.exp(-jnp.abs(a)))
    bg_ref[0:4, :] = beta
    bg_ref[4:8, :] = -jnp.exp(alog_ref[...]) * softplus


def _gdn_prep(proj, rows, lw, batch, tp, rb):
    n_rows = proj.shape[0]
    nb = tp // rb
    halo_per_block = rb // HALO_ROWS
    row_spec = pl.BlockSpec((rb, D_GDN), lambda b, i: (b * nb + i, 0))
    return pl.pallas_call(
        _gdn_prep_kernel,
        out_shape=(jax.ShapeDtypeStruct((n_rows, D_GDN), F32),) * 3
        + (jax.ShapeDtypeStruct((8, n_rows), F32),),
        grid=(batch, nb),
        in_specs=[pl.BlockSpec((rb, 3 * D_GDN), lambda b, i: (b * nb + i, 0)),
                  pl.BlockSpec((HALO_ROWS, 3 * D_GDN),
                               lambda b, i: (jnp.maximum((b * nb + i) * halo_per_block - 1, 0), 0)),
                  pl.BlockSpec((CONV_WIDTH, 3 * D_GDN), lambda b, i: (0, 0)),
                  pl.BlockSpec((16, rb), lambda b, i: (0, b * nb + i)),
                  pl.BlockSpec((N_GDN_HEADS, 1), lambda b, i: (0, 0)),
                  pl.BlockSpec((N_GDN_HEADS, 1), lambda b, i: (0, 0))],
        out_specs=(row_spec, row_spec, row_spec,
                   pl.BlockSpec((8, rb), lambda b, i: (0, b * nb + i))),
        scratch_shapes=[pltpu.VMEM((HALO_ROWS + rb, 3 * D_GDN), F32)],
        compiler_params=_params("parallel", "parallel"),
        name="gdn_prep",
    )(proj, proj, lw["conv_w"], rows, lw["a_log"], lw["dt_bias"])


def _gdn_chunk_kernel(q_ref, k_ref, v_ref, bg_ref, m_ref, n_ref, p_ref, r_ref, cd_ref):
    c = LANES
    n_chunks = q_ref.shape[0] // c
    items = [(ch, h) for ch in range(n_chunks) for h in range(N_GDN_HEADS)]
    idx = range(len(items))
    row = lax.broadcasted_iota(I32, (c, c), 0)
    col = lax.broadcasted_iota(I32, (c, c), 1)
    tri = row >= col
    strict = row > col
    eye = jnp.where(row == col, 1.0, 0.0)
    lane8 = lax.broadcasted_iota(I32, (8, c), 1)
    gates, decays = [], []
    for ch in range(n_chunks):
        bg = bg_ref[:, ch * c:(ch + 1) * c]
        dec = bg
        shift = 1
        while shift < c:
            dec = dec + jnp.where(lane8 >= shift, pltpu.roll(dec, shift, 1), 0.0)
            shift *= 2
        gates.append(bg)
        decays.append(dec)

    def tokens(ref, n):
        ch, h = items[n]
        return ref[ch * c:(ch + 1) * c, h * GDN_HEAD_DIM:(h + 1) * GDN_HEAD_DIM]

    d_row = [jnp.broadcast_to(decays[ch][4 + h:5 + h, :], (c, c)) for ch, h in items]
    d_col = [d.T for d in d_row]
    beta_col = [jnp.broadcast_to(gates[ch][h:h + 1, :], (c, c)).T for ch, h in items]
    d_last = [d[:, c - 1:c] for d in d_row]
    gamma = [jnp.exp(jnp.where(tri, d_col[n] - d_row[n], MASKED_LOGIT)) for n in idx]
    exp_d = [jnp.exp(d_col[n]) for n in idx]
    k16 = [tokens(k_ref, n).astype(BF16) for n in idx]
    kb = [tokens(k_ref, n) * beta_col[n] for n in idx]
    nil = [jnp.where(strict, _dot_nt(kb[n].astype(BF16), k16[n]) * gamma[n], 0.0) for n in idx]
    nil_parts = [_split_bf16(x) for x in nil]
    inv = [eye - x for x in nil]
    power = [_dot_split(x, x) for x in nil_parts]
    steps = int(math.log2(c)) - 1
    for it in range(steps):
        power_parts = [_split_bf16(x) for x in power]
        inv = [inv[n] + _dot_split(_split_bf16(inv[n]), power_parts[n]) for n in idx]
        if it + 1 < steps:
            power = [_dot_split(x, x) for x in power_parts]
    rhs = [jnp.concatenate([kb[n] * exp_d[n], tokens(v_ref, n) * beta_col[n]], axis=1) for n in idx]
    wu = [_dot_split(_split_bf16(inv[n]), _split_bf16(rhs[n])).astype(BF16) for n in idx]
    aqk = [jnp.where(tri, _dot_nt(tokens(q_ref, n).astype(BF16), k16[n]) * gamma[n], 0.0).astype(BF16)
           for n in idx]
    kd_t = [(tokens(k_ref, n) * jnp.exp(d_last[n] - d_col[n])).T.astype(BF16) for n in idx]
    state_wu = [_dot(kd_t[n], wu[n]) for n in idx]
    out_wu = [_dot(aqk[n], wu[n]) for n in idx]
    for n, (ch, h) in enumerate(items):
        m_ref[ch, h] = (-state_wu[n][:, :c]).astype(BF16)
        n_ref[ch, h] = state_wu[n][:, c:]
        p_ref[ch, h] = (tokens(q_ref, n) * exp_d[n] - out_wu[n][:, :c]).astype(BF16)
        r_ref[ch, h] = out_wu[n][:, c:]
        cd_ref[ch, h:h + 1, :] = jnp.exp(d_last[n][0:1, :] + jnp.zeros((1, c), F32))


def _gdn_chunks(qn, kn, vv, bg):
    n_rows = qn.shape[0]
    nc = n_rows // LANES
    per_step = GDN_CHUNKS_PER_STEP
    tok = pl.BlockSpec((per_step * LANES, D_GDN), lambda c: (c, 0))
    mat = pl.BlockSpec((per_step, N_GDN_HEADS, LANES, LANES), lambda c: (c, 0, 0, 0))
    mat_shape = lambda dt: jax.ShapeDtypeStruct((nc, N_GDN_HEADS, LANES, LANES), dt)
    return pl.pallas_call(
        _gdn_chunk_kernel,
        out_shape=(mat_shape(BF16), mat_shape(F32), mat_shape(BF16), mat_shape(F32),
                   jax.ShapeDtypeStruct((nc, N_GDN_HEADS, LANES), F32)),
        grid=(nc // per_step,),
        in_specs=[tok, tok, tok, pl.BlockSpec((8, per_step * LANES), lambda c: (0, c))],
        out_specs=(mat, mat, mat, mat, pl.BlockSpec((per_step, N_GDN_HEADS, LANES), lambda c: (c, 0, 0))),
        compiler_params=_params("parallel"),
        name="gdn_chunks",
    )(qn, kn, vv, bg)


def _gdn_scan_kernel(m_ref, n_ref, p_ref, r_ref, cd_ref, o_ref, s_scr, *, batch):
    @pl.when(pl.program_id(0) == 0)
    def _():
        s_scr[...] = jnp.zeros(s_scr.shape, F32)

    for ch in range(GDN_CHUNKS_PER_STEP):
        rows = slice(ch * LANES, (ch + 1) * LANES)
        for b in range(batch):
            for h in range(N_GDN_HEADS):
                s = s_scr[b, h]
                s16 = s.astype(BF16)
                o_ref[b, rows, h * GDN_HEAD_DIM:(h + 1) * GDN_HEAD_DIM] = _dot(p_ref[b, ch, h], s16) + r_ref[b, ch, h]
                s_scr[b, h] = s * cd_ref[b, ch, h:h + 1, :] + _dot(m_ref[b, ch, h], s16) + n_ref[b, ch, h]


def _gdn_scan(m_mat, n_mat, p_mat, r_mat, cd, batch, tp):
    nc = tp // LANES
    per_step = GDN_CHUNKS_PER_STEP
    shape5 = lambda a: a.reshape(batch, nc, N_GDN_HEADS, LANES, LANES)
    mat = pl.BlockSpec((batch, per_step, N_GDN_HEADS, LANES, LANES), lambda c: (0, c, 0, 0, 0))
    return pl.pallas_call(
        functools.partial(_gdn_scan_kernel, batch=batch),
        out_shape=jax.ShapeDtypeStruct((batch, tp, D_GDN), F32),
        grid=(nc // per_step,),
        in_specs=[mat, mat, mat, mat,
                  pl.BlockSpec((batch, per_step, N_GDN_HEADS, LANES), lambda c: (0, c, 0, 0))],
        out_specs=pl.BlockSpec((batch, per_step * LANES, D_GDN), lambda c: (0, c, 0)),
        scratch_shapes=[pltpu.VMEM((batch, N_GDN_HEADS, LANES, LANES), F32)],
        compiler_params=_params("arbitrary"),
        name="gdn_scan",
    )(shape5(m_mat), shape5(n_mat), shape5(p_mat), shape5(r_mat), cd.reshape(batch, nc, N_GDN_HEADS, LANES))


def _out_proj_kernel(h_ref, oa_ref, za_ref, og_ref, zg_ref, gain_ref, w_ref, out_ref):
    attn = (oa_ref[...] * _silu(za_ref[...])).astype(BF16)
    y = _dot(attn, w_ref[0:D_ATTN, :])
    og, zg = og_ref[...], zg_ref[...]
    for h in range(N_GDN_HEADS):
        cols = slice(h * GDN_HEAD_DIM, (h + 1) * GDN_HEAD_DIM)
        gated = (_rms_rows(og[:, cols], gain_ref[...]) * _silu(zg[:, cols])).astype(BF16)
        y = y + _dot(gated, w_ref[D_ATTN + h * GDN_HEAD_DIM:D_ATTN + (h + 1) * GDN_HEAD_DIM, :])
    out_ref[...] = h_ref[...] + y


def _out_proj(h2, o_attn, proj, o_gdn, gain, w_out, rb):
    n_rows = h2.shape[0]
    blk = lambda width, col: pl.BlockSpec((rb, width), lambda i: (i, col))
    return pl.pallas_call(
        _out_proj_kernel,
        out_shape=jax.ShapeDtypeStruct((n_rows, D_MODEL), F32),
        grid=(n_rows // rb,),
        in_specs=[blk(D_MODEL, 0), blk(D_ATTN, 0), blk(D_ATTN, COL_Z_ATTN), blk(D_GDN, 0),
                  blk(D_GDN, COL_Z_GDN),
                  pl.BlockSpec((1, GDN_HEAD_DIM), lambda i: (0, 0)),
                  pl.BlockSpec((D_MODEL, D_MODEL), lambda i: (0, 0))],
        out_specs=blk(D_MODEL, 0),
        compiler_params=_params("parallel"),
        name="out_proj",
    )(h2, o_attn, proj, o_gdn, proj, gain, w_out)


def _pack_layer(norm_gain, w_in, cq_gain, ckv_gain, w_uq, w_ukv, w_q_idx, q_gain, k_gain, conv_w, a_log,
                dt_bias, gdn_gain, w_out):
    o = 0
    parts = {}
    for name, size in (("c_q", Q_RANK), ("c_kv", KV_RANK), ("k_idx", IDX_DIM), ("w_idx", N_IDX_HEADS),
                       ("z_attn", D_ATTN), ("qkv_g", 3 * D_GDN), ("z_g", D_GDN), ("b", N_GDN_HEADS),
                       ("a", N_GDN_HEADS)):
        parts[name] = w_in[:, o:o + size]
        o += size
    w_packed = jnp.concatenate([parts["qkv_g"], parts["c_q"], parts["c_kv"], parts["k_idx"], parts["k_idx"],
                                parts["z_attn"], parts["z_g"]], axis=1).astype(BF16)
    w_rows = jnp.concatenate([parts["w_idx"], parts["b"], parts["a"],
                              jnp.zeros((D_MODEL, 4), w_in.dtype)], axis=1).T.astype(BF16)
    return dict(
        gain=norm_gain[None, :], w_packed=w_packed, w_rows=w_rows,
        g_cq=cq_gain[None, :], g_ckv=ckv_gain[None, :],
        w_uqt=w_uq.T.astype(BF16), w_qit=w_q_idx.T.astype(BF16),
        w_ukt=w_ukv[:, :D_ATTN].T.astype(BF16), w_uvt=w_ukv[:, D_ATTN:].T.astype(BF16),
        g_q_col=q_gain[:, None], g_k_col=k_gain[:, None],
        conv_w=conv_w, a_log=a_log[:, None], dt_bias=dt_bias[:, None],
        gdn_gain=gdn_gain[None, :], w_out=w_out.astype(BF16))


def _layer(h2, lw, bias, tri, batch, tp, t_valid, topk):
    rb = _row_block(tp)
    proj, rows = _in_proj(h2, lw["gain"], lw["w_packed"], lw["w_rows"], rb)
    qt, k, vt, qit, kidx = _dsa_prep(proj, lw, batch, tp, rb)
    o_attn = _dsa_attention(qit, rows, qt, kidx, k, vt, bias, tri, batch, tp, t_valid, topk)
    qn, kn, vv, bg = _gdn_prep(proj, rows, lw, batch, tp, rb)
    m_mat, n_mat, p_mat, r_mat, cd = _gdn_chunks(qn, kn, vv, bg)
    o_gdn = _gdn_scan(m_mat, n_mat, p_mat, r_mat, cd, batch, tp).reshape(batch * tp, D_GDN)
    return _out_proj(h2, o_attn, proj, o_gdn, lw["gdn_gain"], lw["w_out"], rb)


def _forward(x, meta_tokens, rel_bias_table, layer_weights, topk):
    batch, seq, _ = x.shape
    t = seq + N_META
    tp = -(-t // KEY_TILE) * KEY_TILE
    meta = jnp.broadcast_to(meta_tokens[None].astype(x.dtype), (batch, N_META, D_MODEL))
    h = jnp.concatenate([meta, x, jnp.zeros((batch, tp - t, D_MODEL), x.dtype)], axis=1)
    h2 = h.reshape(batch * tp, D_MODEL)
    bias = _bias_tiles(rel_bias_table)
    tri = jnp.tril(jnp.ones((KEY_TILE, KEY_TILE), BF16))
    for lw in layer_weights:
        h2 = _layer(h2, lw, bias, tri, batch, tp, t, topk)
    return h2.reshape(batch, tp, D_MODEL)[:, N_META:t]


def kernel(x, meta_tokens, rel_bias_table, norm_gain, w_in, cq_norm_gain, ckv_norm_gain, w_uq, w_ukv, w_q_idx,
           q_norm_gain, k_norm_gain, conv_w, a_log, dt_bias, gdn_norm_gain, w_out):
    depth = norm_gain.shape[0]
    topk = min(TOPK_MAX, x.shape[1] // 4)
    layers = [_pack_layer(norm_gain[l], w_in[l], cq_norm_gain[l], ckv_norm_gain[l], w_uq[l], w_ukv[l],
                          w_q_idx[l], q_norm_gain[l], k_norm_gain[l], conv_w[l], a_log[l], dt_bias[l],
                          gdn_norm_gain[l], w_out[l]) for l in range(depth)]
    return _forward(x, meta_tokens, rel_bias_table, layers, topk)
```

```python
import functools
import math

import jax
import jax.numpy as jnp
from jax import lax
from jax.experimental import pallas as pl
from jax.experimental.pallas import tpu as pltpu

F32 = jnp.float32
BF16 = jnp.bfloat16
I32 = jnp.int32

D_MODEL = 1024
N_META = 16
EPS = 1e-6
N_ATTN_HEADS = 8
ATTN_HEAD_DIM = 64
D_ATTN = N_ATTN_HEADS * ATTN_HEAD_DIM
Q_RANK = 256
KV_RANK = 128
N_IDX_HEADS = 4
IDX_DIM = 64
TOPK_MAX = 256
N_REL_BUCKETS = 32
REL_MAX_DIST = 128
N_GDN_HEADS = 4
GDN_HEAD_DIM = 128
D_GDN = N_GDN_HEADS * GDN_HEAD_DIM
CONV_WIDTH = 4

LANES = 128
KEY_TILE = 256
COUNT_TILE = 512
SCAN_TILE = 1024
FIRST_CHECK_PASSES = 10
INTERPOLATED_PASSES = 24
MAX_SEARCH_PASSES = INTERPOLATED_PASSES + 32
GDN_CHUNKS_PER_STEP = 2
SCAN_CHUNKS = (6, 4, 3, 2, 1)
ROW_TILES = (5, 4, 3, 2, 1)
HALO_ROWS = 8
D_PACKED = 3 * D_GDN + 512 + D_ATTN + D_GDN
COL_SMALL = 3
COL_Z_ATTN = 4
COL_Z_GDN = 5
MASKED_LOGIT = -1e30
LOG2_E = math.log2(math.e)
KEY_MIN = -2 ** 31
PATTERN_NEG_FLT_MAX = KEY_MIN + (1 << 23)
VMEM_LIMIT = 56 * 1024 * 1024

NT_DIMS = (((1,), (1,)), ((), ()))


def _dot(a, b):
    return jnp.dot(a, b, preferred_element_type=F32)


def _dot_nt(a, b):
    return lax.dot_general(a, b, NT_DIMS, preferred_element_type=F32)


def _split_bf16(x):
    hi = x.astype(BF16)
    return hi, (x - hi.astype(F32)).astype(BF16)


def _dot_split(a_parts, b_parts):
    (a_hi, a_lo), (b_hi, b_lo) = a_parts, b_parts
    return _dot(a_hi, b_hi) + (_dot(a_hi, b_lo) + _dot(a_lo, b_hi))


def _sigmoid(x):
    return 1.0 / (1.0 + jnp.exp(-x))


def _silu(x):
    return x * _sigmoid(x)


def _row_block(tp):
    tiles = tp // LANES
    return LANES * next(d for d in ROW_TILES if tiles % d == 0)


def _params(*sem):
    return pltpu.CompilerParams(dimension_semantics=sem, vmem_limit_bytes=VMEM_LIMIT)


def _bias_kernel(table_ref, out_ref):
    row = lax.broadcasted_iota(I32, (LANES, LANES), 0)
    col = lax.broadcasted_iota(I32, (LANES, LANES), 1)
    max_exact = N_REL_BUCKETS // 2
    for kind in range(3):
        dist = col - row + (2 - kind) * LANES
        n = jnp.maximum(dist, 0)
        nf = jnp.maximum(n, 1).astype(F32)
        large = max_exact + (jnp.log(nf / max_exact) / math.log(REL_MAX_DIST / max_exact)
                             * (N_REL_BUCKETS - max_exact)).astype(I32)
        large = jnp.minimum(large, N_REL_BUCKETS - 1)
        bucket = jnp.where(n < max_exact, n, large)
        for h in range(N_ATTN_HEADS):
            tile = jnp.zeros((LANES, LANES), F32)
            for b in range(N_REL_BUCKETS):
                tile = jnp.where(bucket == b, table_ref[b, h], tile)
            far = table_ref[N_REL_BUCKETS - 1, h]
            out_ref[kind, :, h * LANES:(h + 1) * LANES] = (tile - far) * LOG2_E


def _bias_tiles(rel_table):
    return pl.pallas_call(
        _bias_kernel,
        out_shape=jax.ShapeDtypeStruct((3, LANES, N_ATTN_HEADS * LANES), F32),
        in_specs=[pl.BlockSpec(memory_space=pltpu.SMEM)],
        out_specs=pl.BlockSpec(memory_space=pltpu.VMEM),
        name="rel_bias_tiles",
    )(rel_table)


def _in_proj_kernel(h_ref, gain_ref, w_ref, wrows_ref, proj_ref, rows_ref):
    x = h_ref[...]
    y = x * lax.rsqrt(jnp.mean(x * x, axis=-1, keepdims=True) + EPS)
    hn = (y * gain_ref[...]).astype(BF16)
    proj_ref[...] = _dot(hn, w_ref[...])
    rows_ref[...] = _dot_nt(wrows_ref[...], hn)


def _in_proj(h2, gain, w_packed, w_rows, rb):
    n_rows = h2.shape[0]
    grid = (n_rows // rb,)
    return pl.pallas_call(
        _in_proj_kernel,
        out_shape=(jax.ShapeDtypeStruct((n_rows, D_PACKED), F32),
                   jax.ShapeDtypeStruct((16, n_rows), F32)),
        grid=grid,
        in_specs=[pl.BlockSpec((rb, D_MODEL), lambda i: (i, 0)),
                  pl.BlockSpec((1, D_MODEL), lambda i: (0, 0)),
                  pl.BlockSpec((D_MODEL, D_PACKED), lambda i: (0, 0)),
                  pl.BlockSpec((16, D_MODEL), lambda i: (0, 0))],
        out_specs=(pl.BlockSpec((rb, D_PACKED), lambda i: (i, 0)),
                   pl.BlockSpec((16, rb), lambda i: (0, i))),
        compiler_params=_params("parallel"),
        name="in_proj",
    )(h2, gain, w_packed, w_rows)


def _rms_rows(x, gain):
    return x * lax.rsqrt(jnp.mean(x * x, axis=-1, keepdims=True) + EPS) * gain


def _dsa_prep_kernel(sm_ref, gcq_ref, gckv_ref, wuqt_ref, wqit_ref, wukt_ref, wuvt_ref, gq_ref, gk_ref,
                     qt_ref, k_ref, vt_ref, qit_ref, kidx_ref):
    sm = sm_ref[...]
    rb = sm.shape[0]
    cq = _rms_rows(sm[:, :Q_RANK], gcq_ref[...]).astype(BF16)
    ckv = _rms_rows(sm[:, Q_RANK:Q_RANK + KV_RANK], gckv_ref[...]).astype(BF16)
    kidx_ref[...] = sm[:, Q_RANK + KV_RANK:].astype(BF16)
    q3 = _dot_nt(wuqt_ref[...], cq).reshape(N_ATTN_HEADS, ATTN_HEAD_DIM, rb)
    q3 = q3 * lax.rsqrt(jnp.mean(q3 * q3, axis=1, keepdims=True) + EPS) * gq_ref[...][None]
    qt_ref[...] = (q3 * (ATTN_HEAD_DIM ** -0.5 * LOG2_E)).reshape(D_ATTN, rb).astype(BF16)
    k3 = _dot_nt(wukt_ref[...], ckv).reshape(N_ATTN_HEADS, ATTN_HEAD_DIM, rb)
    k3 = k3 * lax.rsqrt(jnp.mean(k3 * k3, axis=1, keepdims=True) + EPS) * gk_ref[...][None]
    k_ref[...] = k3.reshape(D_ATTN, rb).T.astype(BF16)
    vt_ref[...] = _dot_nt(wuvt_ref[...], ckv).astype(BF16)
    qit_ref[...] = _dot_nt(wqit_ref[...], cq).astype(BF16)


def _dsa_prep(proj, lw, batch, tp, rb):
    n_rows = proj.shape[0]
    nb = tp // rb
    d_idx = N_IDX_HEADS * IDX_DIM
    const = lambda shape: pl.BlockSpec(shape, lambda b, i: (0, 0))
    row_spec = lambda width: pl.BlockSpec((rb, width), lambda b, i: (b * nb + i, 0))
    col_spec = lambda height: pl.BlockSpec((None, height, rb), lambda b, i: (b, 0, i))
    return pl.pallas_call(
        _dsa_prep_kernel,
        out_shape=(jax.ShapeDtypeStruct((batch, D_ATTN, tp), BF16),
                   jax.ShapeDtypeStruct((n_rows, D_ATTN), BF16),
                   jax.ShapeDtypeStruct((batch, D_ATTN, tp), BF16),
                   jax.ShapeDtypeStruct((batch, d_idx, tp), BF16),
                   jax.ShapeDtypeStruct((n_rows, LANES), BF16)),
        grid=(batch, nb),
        in_specs=[pl.BlockSpec((rb, 512), lambda b, i: (b * nb + i, COL_SMALL)),
                  const((1, Q_RANK)), const((1, KV_RANK)),
                  const((D_ATTN, Q_RANK)), const((d_idx, Q_RANK)),
                  const((D_ATTN, KV_RANK)), const((D_ATTN, KV_RANK)),
                  const((ATTN_HEAD_DIM, 1)), const((ATTN_HEAD_DIM, 1))],
        out_specs=(col_spec(D_ATTN), row_spec(D_ATTN), col_spec(D_ATTN), col_spec(d_idx), row_spec(LANES)),
        compiler_params=_params("parallel", "parallel"),
        name="dsa_prep",
    )(proj, lw["g_cq"], lw["g_ckv"], lw["w_uqt"], lw["w_qit"], lw["w_ukt"], lw["w_uvt"], lw["g_q_col"],
      lw["g_k_col"])


def _dsa_block(i, qit_ref, rows_ref, qt_ref, kidx_ref, k_ref, vt_ref, bias_ref, tri_ref, o_ref,
               score_scr, wi_scr, wq_scr, s_scr, p_scr, m_scr, l_scr, acc_scr, alpha_scr, mask_scr, tie_scr, *, topk):
    t0 = i * LANES
    n_kt = i // 2 + 1
    n_st = (n_kt + 3) // 4
    hd = ATTN_HEAD_DIM
    pair_w = 2 * LANES
    n_pairs = N_ATTN_HEADS // 2

    zeros_hd = jnp.zeros((hd, LANES), BF16)
    for h in range(N_IDX_HEADS):
        wi_scr[0:IDX_DIM, h * LANES:(h + 1) * LANES] = qit_ref[h * IDX_DIM:(h + 1) * IDX_DIM, :]
    wi_scr[IDX_DIM:, :] = jnp.zeros((LANES - IDX_DIM, N_IDX_HEADS * LANES), BF16)
    for p in range(n_pairs):
        wq_scr[p, 0:hd, 0:LANES] = qt_ref[2 * p * hd:(2 * p + 1) * hd, :]
        wq_scr[p, 0:hd, LANES:] = zeros_hd
        wq_scr[p, hd:, 0:LANES] = zeros_hd
        wq_scr[p, hd:, LANES:] = qt_ref[(2 * p + 1) * hd:(2 * p + 2) * hd, :]

    row = lax.broadcasted_iota(I32, (KEY_TILE, LANES), 0)
    col = lax.broadcasted_iota(I32, (KEY_TILE, LANES), 1)
    w_idx = rows_ref[0:N_IDX_HEADS, :] * (N_IDX_HEADS ** -0.5 * IDX_DIM ** -0.5)

    def key_tile(j):
        return pl.multiple_of(j * KEY_TILE, KEY_TILE)

    def causal(j):
        return (j * KEY_TILE + row) <= (t0 + col)

    def fold8(x, op):
        return op(x.reshape(KEY_TILE // 8, 8, LANES), axis=0)

    def score_step(jc, carry, masked):
        top, bottom, n_nonneg, n_pos = carry
        subs = [jc * (SCAN_TILE // KEY_TILE) + sub for sub in range(SCAN_TILE // KEY_TILE)]
        logits = [_dot(kidx_ref[pl.ds(key_tile(jnp.minimum(j, n_kt - 1)), KEY_TILE), :], wi_scr[...])
                  for j in subs]
        for j, lg in zip(subs, logits):
            score = jnp.zeros((KEY_TILE, LANES), F32)
            for h in range(N_IDX_HEADS):
                score = score + jnp.maximum(lg[:, h * LANES:(h + 1) * LANES], 0.0) * w_idx[h:h + 1, :]
            if masked:
                visible = causal(j)
                seen = jnp.where(visible, score, -jnp.inf)
                bottom = jnp.minimum(bottom, fold8(jnp.where(visible, score, jnp.inf), jnp.min))
            else:
                seen = score
                bottom = jnp.minimum(bottom, fold8(score, jnp.min))
            top = jnp.maximum(top, fold8(seen, jnp.max))
            n_nonneg = n_nonneg + fold8(jnp.where(seen >= 0.0, 1, 0), jnp.sum)
            n_pos = n_pos + fold8(jnp.where(seen > 0.0, 1, 0), jnp.sum)
            score_scr[pl.ds(key_tile(j), KEY_TILE), :] = seen
        return top, bottom, n_nonneg, n_pos

    zeros8 = jnp.zeros((8, LANES), I32)
    carry = (jnp.full((8, LANES), -jnp.inf, F32), jnp.full((8, LANES), jnp.inf, F32), zeros8, zeros8)
    carry = lax.fori_loop(0, n_st - 1, functools.partial(score_step, masked=False), carry)
    top, bottom, n_nonneg, n_pos = score_step(n_st - 1, carry, masked=True)
    top = jnp.max(top, axis=0, keepdims=True)
    bottom = jnp.min(bottom, axis=0, keepdims=True)
    count0 = jnp.sum(n_nonneg, axis=0, keepdims=True)
    count_pos = jnp.sum(n_pos, axis=0, keepdims=True)

    def count_f32(cand, below=None):
        def body(j, carry):
            acc, best = carry
            for part in range(SCAN_TILE // COUNT_TILE):
                start = pl.multiple_of(j * SCAN_TILE + part * COUNT_TILE, COUNT_TILE)
                x = score_scr[pl.ds(start, COUNT_TILE), :]
                ind = jnp.where(x >= cand, 1, 0)
                acc = acc + jnp.sum(ind.reshape(COUNT_TILE // 8, 8, LANES), axis=0)
                if below is not None:
                    under = jnp.where(x < below, x, -jnp.inf)
                    best = jnp.maximum(best, jnp.max(under.reshape(COUNT_TILE // 8, 8, LANES), axis=0))
            return acc, best
        acc, best = lax.fori_loop(0, n_st, body, (jnp.zeros((8, LANES), I32), jnp.full((8, LANES), -jnp.inf, F32)))
        count = jnp.sum(acc, axis=0, keepdims=True)
        if below is None:
            return count
        return count, jnp.max(best, axis=0, keepdims=True)

    def to_pattern(v):
        bits = lax.bitcast_convert_type(v, I32)
        return bits ^ ((bits >> 31) & 0x7FFFFFFF)

    def to_f32(c):
        return lax.bitcast_convert_type(c ^ ((c >> 31) & 0x7FFFFFFF), F32)

    n_visible = t0 + 1 + lax.broadcasted_iota(I32, (1, LANES), 1)
    nonneg = count0 >= topk
    lo = jnp.where(nonneg, 0, to_pattern(bottom))
    hi = jnp.where(nonneg, to_pattern(top) + 1, 0)
    count_lo = jnp.where(nonneg, count0, n_visible)
    count_hi = jnp.where(nonneg, 0, count0)
    few = n_visible < topk
    lo = jnp.where(few, PATTERN_NEG_FLT_MAX, lo)
    zero_tied = nonneg & (count_pos < topk)
    hi = jnp.where(zero_tied, 1, hi)
    count_hi = jnp.where(zero_tied, count_pos, count_hi)
    open_q = jnp.where(few | zero_tied | (count_lo == topk), 0, 1)

    log_topk = math.log(topk)

    def count_error(count):
        return jnp.log(count.astype(F32) + 0.5) - log_topk

    def probe(n_pass, carry, extract):
        lo, hi, count_lo, count_hi, err_lo, err_hi, last_side, open_q = carry
        v_lo, v_hi = to_f32(lo), to_f32(hi)
        frac = err_lo / (err_lo - err_hi)
        frac = jnp.where(count_lo - count_hi <= 4, 0.5, frac)
        guess = to_pattern(v_lo + (v_hi - v_lo) * frac)
        middle = lo + lax.shift_right_logical(hi - lo, 1)
        cand = jnp.where(n_pass >= INTERPOLATED_PASSES, middle, guess)
        cand = jnp.minimum(jnp.maximum(cand, lo + 1), hi - 1)
        is_open = open_q == 1
        if extract:
            count, under_hi = count_f32(to_f32(cand), below=v_hi)
            next_below = to_pattern(under_hi)
            found = is_open & (count_hi == topk - 1)
            is_open = is_open & jnp.logical_not(found)
        else:
            count = count_f32(to_f32(cand))
        raise_lo = is_open & (count >= topk)
        lower_hi = is_open & (count < topk)
        err = count_error(count)
        err_hi = jnp.where(raise_lo & (last_side == 1), err_hi * 0.5, err_hi)
        err_lo = jnp.where(lower_hi & (last_side == -1), err_lo * 0.5, err_lo)
        err_lo = jnp.where(raise_lo, err, err_lo)
        err_hi = jnp.where(lower_hi, err, err_hi)
        lo = jnp.where(raise_lo, cand, lo)
        count_lo = jnp.where(raise_lo, count, count_lo)
        hi = jnp.where(lower_hi, cand, hi)
        count_hi = jnp.where(lower_hi, count, count_hi)
        last_side = jnp.where(raise_lo, 1, jnp.where(lower_hi, -1, last_side))
        if extract:
            lo = jnp.where(found, next_below, lo)
            hi = jnp.where(found | raise_lo, next_below + 1, hi)
        closed = (count_lo == topk) | (hi - lo == 1)
        return lo, hi, count_lo, count_hi, err_lo, err_hi, last_side, jnp.where(closed, 0, open_q)

    carry = (lo, hi, count_lo, count_hi, count_error(count_lo), count_error(count_hi),
             jnp.zeros((1, LANES), I32), open_q)
    n_first = jnp.where(jnp.sum(open_q) > 0, FIRST_CHECK_PASSES, 0)
    carry = lax.fori_loop(0, n_first, functools.partial(probe, extract=False), carry)

    def probes_left(st):
        n_pass, n_open = st[0], st[-1]
        return (n_pass < MAX_SEARCH_PASSES) & (n_open > 0)

    def extracting_probe(st):
        carry = probe(st[0], st[1:-1], extract=True)
        return (st[0] + 1,) + carry + (jnp.sum(carry[-1]),)

    state = lax.while_loop(probes_left, extracting_probe,
                           (jnp.int32(FIRST_CHECK_PASSES),) + carry + (jnp.sum(carry[-1]),))
    lo, count_lo, count_hi = state[1], state[3], state[4]
    tau = to_f32(lo)
    need = jnp.where((count_lo == topk) | few, topk, topk - count_hi).astype(F32)

    m_scr[...] = jnp.full(m_scr.shape, MASKED_LOGIT, F32)
    l_scr[...] = jnp.zeros(l_scr.shape, F32)
    acc_scr[...] = jnp.zeros(acc_scr.shape, F32)
    ones_rows = jnp.ones((16, KEY_TILE), BF16)
    last = n_kt - 1

    def mask_pair(j_first, parity):
        xs = [score_scr[pl.ds(key_tile(jnp.minimum(j_first + slot, last)), KEY_TILE), :] for slot in range(2)]
        ties = [x == tau for x in xs]
        tie_cols = jnp.concatenate([jnp.where(tie, 1.0, 0.0).astype(BF16) for tie in ties], axis=1)
        ranks = _dot(tri_ref[...], tie_cols)
        tie_carry = tie_scr[0:1, :]
        for slot, (x, tie) in enumerate(zip(xs, ties)):
            rank = ranks[:, slot * LANES:(slot + 1) * LANES] + tie_carry
            tie_carry = rank[KEY_TILE - 1:KEY_TILE, :]
            take = (tie & (rank <= need)) | (x > tau)
            mask_scr[parity, slot] = jnp.where(take, 0.0, MASKED_LOGIT)
        tie_scr[0:1, :] = tie_carry

    def qk_pair(slot, j, p):
        s_scr[slot, :, p * pair_w:(p + 1) * pair_w] = _dot(
            k_ref[pl.ds(key_tile(j), KEY_TILE), p * LANES:(p + 1) * LANES], wq_scr[p])

    def softmax_pair(slot, j, p, near, parity):
        mask_add = mask_scr[parity, slot]
        alphas = []
        for h in (2 * p, 2 * p + 1):
            cols = slice(h * LANES, (h + 1) * LANES)
            logits = s_scr[slot, :, cols] + mask_add
            if near:
                kind_top = jnp.clip(2 * j - i + 2, 0, 2)
                kind_bot = jnp.clip(2 * j - i + 3, 0, 2)
                logits = logits + jnp.concatenate(
                    [bias_ref[kind_top, :, cols], bias_ref[kind_bot, :, cols]], axis=0)
            m_old = m_scr[h:h + 1, :]
            m_new = jnp.maximum(m_old, jnp.max(logits, axis=0, keepdims=True))
            m_scr[h:h + 1, :] = m_new
            p_scr[slot, :, cols] = jnp.exp2(logits - m_new).astype(BF16)
            alphas.append(jnp.exp2(m_old - m_new))
        return alphas

    def pv_pair(slot, j, p, alphas):
        lhs = jnp.concatenate([vt_ref[p * 2 * hd:(p + 1) * 2 * hd, pl.ds(key_tile(j), KEY_TILE)], ones_rows],
                              axis=0)
        out = _dot(lhs, p_scr[slot, :, p * pair_w:(p + 1) * pair_w])
        for half in range(2):
            h = 2 * p + half
            rows_h = slice(h * hd, (h + 1) * hd)
            q_cols = slice(half * LANES, (half + 1) * LANES)
            acc_scr[rows_h, :] = acc_scr[rows_h, :] * alphas[half] + out[half * hd:(half + 1) * hd, q_cols]
            l_scr[h:h + 1, :] = l_scr[h:h + 1, :] * alphas[half] + out[2 * hd:2 * hd + 1, q_cols]

    def pending_alphas(p):
        return [alpha_scr[h:h + 1, :] for h in (2 * p, 2 * p + 1)]

    def clear_pending():
        p_scr[1] = jnp.zeros(p_scr.shape[1:], BF16)
        alpha_scr[...] = jnp.ones(alpha_scr.shape, F32)

    def pair_step(ja, j_pending, j_next, near, parity):
        alphas_a = []
        for p in range(n_pairs):
            pv_pair(1, j_pending, p, pending_alphas(p))
            qk_pair(1, ja + 1, p)
            alphas_a.append(softmax_pair(0, ja, p, near, parity))
        for p in range(n_pairs):
            pv_pair(0, ja, p, alphas_a[p])
            qk_pair(0, j_next, p)
            alphas_b = softmax_pair(1, ja + 1, p, near, parity)
            for half in range(2):
                alpha_scr[2 * p + half:2 * p + half + 1, :] = alphas_b[half]
        mask_pair(ja + 2, 1 - parity)

    def single_step(ja, j_pending, near, parity):
        alphas_a = []
        for p in range(n_pairs):
            pv_pair(1, j_pending, p, pending_alphas(p))
            alphas_a.append(softmax_pair(0, ja, p, near, parity))
        for p in range(n_pairs):
            pv_pair(0, ja, p, alphas_a[p])
        clear_pending()

    n_far = 2 * (jnp.maximum(n_kt - 2, 0) // 2)
    n_near = n_kt - n_far
    first_near_parity = (n_far // 2) % 2
    clear_pending()
    tie_scr[...] = jnp.zeros(tie_scr.shape, F32)
    mask_pair(0, 0)
    for p in range(n_pairs):
        qk_pair(0, 0, p)

    def far_body(jp, carry):
        pair_step(2 * jp, jnp.maximum(2 * jp - 1, 0), 2 * jp + 2, near=False, parity=jp % 2)
        return carry

    lax.fori_loop(0, n_far // 2, far_body, 0)

    @pl.when(n_near >= 2)
    def _():
        pair_step(n_far, jnp.maximum(n_far - 1, 0), jnp.minimum(n_far + 2, last), near=True,
                  parity=first_near_parity)

    @pl.when(n_near % 2 == 1)
    def _():
        single_step(last, jnp.where(n_near == 3, n_far + 1, jnp.maximum(n_far - 1, 0)), near=True,
                    parity=jnp.where(n_near == 3, 1 - first_near_parity, first_near_parity))

    for p in range(n_pairs):
        pv_pair(1, last, p, pending_alphas(p))

    for h in range(N_ATTN_HEADS):
        rows_h = slice(h * hd, (h + 1) * hd)
        acc_scr[rows_h, :] = acc_scr[rows_h, :] / l_scr[h:h + 1, :]
    o_ref[...] = acc_scr[...].T


def _dsa_kernel(*refs, topk, t_valid):
    o_ref = refs[8]
    i = pl.program_id(1)
    is_real = i * LANES < t_valid

    @pl.when(is_real)
    def _():
        _dsa_block(i, *refs, topk=topk)

    @pl.when(jnp.logical_not(is_real))
    def _():
        o_ref[...] = jnp.zeros(o_ref.shape, F32)


def _dsa_attention(qit, rows, qt, kidx, k, vt, bias, tri, batch, tp, t_valid, topk):
    n_rows = k.shape[0]
    nqb = tp // LANES
    d_idx = N_IDX_HEADS * IDX_DIM
    n_pairs = N_ATTN_HEADS // 2
    key_rows = -(-tp // SCAN_TILE) * SCAN_TILE
    q_cols = lambda height: pl.BlockSpec((None, height, LANES), lambda b, i: (b, 0, i))
    return pl.pallas_call(
        functools.partial(_dsa_kernel, topk=topk, t_valid=t_valid),
        out_shape=jax.ShapeDtypeStruct((n_rows, D_ATTN), F32),
        grid=(batch, nqb),
        in_specs=[q_cols(d_idx),
                  pl.BlockSpec((16, LANES), lambda b, i: (0, b * nqb + i)),
                  q_cols(D_ATTN),
                  pl.BlockSpec((tp, LANES), lambda b, i: (b, 0)),
                  pl.BlockSpec((tp, D_ATTN), lambda b, i: (b, 0)),
                  pl.BlockSpec((None, D_ATTN, tp), lambda b, i: (b, 0, 0)),
                  pl.BlockSpec((3, LANES, N_ATTN_HEADS * LANES), lambda b, i: (0, 0, 0)),
                  pl.BlockSpec((KEY_TILE, KEY_TILE), lambda b, i: (0, 0))],
        out_specs=pl.BlockSpec((LANES, D_ATTN), lambda b, i: (b * nqb + i, 0)),
        scratch_shapes=[pltpu.VMEM((key_rows, LANES), F32),
                        pltpu.VMEM((LANES, N_IDX_HEADS * LANES), BF16),
                        pltpu.VMEM((n_pairs, LANES, 2 * LANES), BF16),
                        pltpu.VMEM((2, KEY_TILE, N_ATTN_HEADS * LANES), F32),
                        pltpu.VMEM((2, KEY_TILE, N_ATTN_HEADS * LANES), BF16),
                        pltpu.VMEM((N_ATTN_HEADS, LANES), F32),
                        pltpu.VMEM((N_ATTN_HEADS, LANES), F32),
                        pltpu.VMEM((D_ATTN, LANES), F32),
                        pltpu.VMEM((N_ATTN_HEADS, LANES), F32),
                        pltpu.VMEM((2, 2, KEY_TILE, LANES), F32),
                        pltpu.VMEM((8, LANES), F32)],
        compiler_params=_params("parallel", "parallel"),
        name="dsa_attention",
    )(qit, rows, qt, kidx, k, vt, bias, tri)


def _gdn_prep_kernel(x_ref, halo_ref, cw_ref, rows_ref, alog_ref, dtb_ref, q_ref, k_ref, v_ref, bg_ref, buf):
    first = pl.program_id(1) == 0
    rb = x_ref.shape[0]
    buf[0:HALO_ROWS, :] = jnp.where(first, 0.0, halo_ref[...])
    buf[HALO_ROWS:, :] = x_ref[...]
    acc = jnp.zeros((rb, 3 * D_GDN), F32)
    for tap in range(CONV_WIDTH):
        start = HALO_ROWS - (CONV_WIDTH - 1) + tap
        acc = acc + cw_ref[tap:tap + 1, :] * buf[start:start + rb, :]
    y = _silu(acc)
    for h in range(N_GDN_HEADS):
        cols = slice(h * GDN_HEAD_DIM, (h + 1) * GDN_HEAD_DIM)
        qh = y[:, cols]
        kh = y[:, D_GDN + h * GDN_HEAD_DIM:D_GDN + (h + 1) * GDN_HEAD_DIM]
        q_ref[:, cols] = (qh * lax.rsqrt(jnp.sum(qh * qh, axis=-1, keepdims=True) + EPS)
                          * (GDN_HEAD_DIM ** -0.5))
        k_ref[:, cols] = kh * lax.rsqrt(jnp.sum(kh * kh, axis=-1, keepdims=True) + EPS)
    v_ref[...] = y[:, 2 * D_GDN:]
    rows = rows_ref[...]
    beta = _sigmoid(rows[4:8, :])
    a = rows[8:12, :] + dtb_ref[...]
    softplus = jnp.maximum(a, 0.0) + jnp.log1p(jnp.exp(-jnp.abs(a)))
    bg_ref[0:4, :] = beta
    bg_ref[4:8, :] = -jnp.exp(alog_ref[...]) * softplus


def _gdn_prep(proj, rows, lw, batch, tp, rb):
    n_rows = proj.shape[0]
    nb = tp // rb
    halo_per_block = rb // HALO_ROWS
    row_spec = pl.BlockSpec((rb, D_GDN), lambda b, i: (b * nb + i, 0))
    return pl.pallas_call(
        _gdn_prep_kernel,
        out_shape=(jax.ShapeDtypeStruct((n_rows, D_GDN), F32),) * 3
        + (jax.ShapeDtypeStruct((8, n_rows), F32),),
        grid=(batch, nb),
        in_specs=[pl.BlockSpec((rb, 3 * D_GDN), lambda b, i: (b * nb + i, 0)),
                  pl.BlockSpec((HALO_ROWS, 3 * D_GDN),
                               lambda b, i: (jnp.maximum((b * nb + i) * halo_per_block - 1, 0), 0)),
                  pl.BlockSpec((CONV_WIDTH, 3 * D_GDN), lambda b, i: (0, 0)),
                  pl.BlockSpec((16, rb), lambda b, i: (0, b * nb + i)),
                  pl.BlockSpec((N_GDN_HEADS, 1), lambda b, i: (0, 0)),
                  pl.BlockSpec((N_GDN_HEADS, 1), lambda b, i: (0, 0))],
        out_specs=(row_spec, row_spec, row_spec,
                   pl.BlockSpec((8, rb), lambda b, i: (0, b * nb + i))),
        scratch_shapes=[pltpu.VMEM((HALO_ROWS + rb, 3 * D_GDN), F32)],
        compiler_params=_params("parallel", "parallel"),
        name="gdn_prep",
    )(proj, proj, lw["conv_w"], rows, lw["a_log"], lw["dt_bias"])


def _gdn_chunk_kernel(q_ref, k_ref, v_ref, bg_ref, m_ref, n_ref, p_ref, r_ref, cd_ref):
    c = LANES
    n_chunks = q_ref.shape[0] // c
    items = [(ch, h) for ch in range(n_chunks) for h in range(N_GDN_HEADS)]
    idx = range(len(items))
    row = lax.broadcasted_iota(I32, (c, c), 0)
    col = lax.broadcasted_iota(I32, (c, c), 1)
    tri = row >= col
    strict = row > col
    eye = jnp.where(row == col, 1.0, 0.0)
    lane8 = lax.broadcasted_iota(I32, (8, c), 1)
    gates, decays = [], []
    for ch in range(n_chunks):
        bg = bg_ref[:, ch * c:(ch + 1) * c]
        dec = bg
        shift = 1
        while shift < c:
            dec = dec + jnp.where(lane8 >= shift, pltpu.roll(dec, shift, 1), 0.0)
            shift *= 2
        gates.append(bg)
        decays.append(dec)

    def tokens(ref, n):
        ch, h = items[n]
        return ref[ch * c:(ch + 1) * c, h * GDN_HEAD_DIM:(h + 1) * GDN_HEAD_DIM]

    d_row = [jnp.broadcast_to(decays[ch][4 + h:5 + h, :], (c, c)) for ch, h in items]
    d_col = [d.T for d in d_row]
    beta_col = [jnp.broadcast_to(gates[ch][h:h + 1, :], (c, c)).T for ch, h in items]
    d_last = [d[:, c - 1:c] for d in d_row]
    gamma = [jnp.exp(jnp.where(tri, d_col[n] - d_row[n], MASKED_LOGIT)) for n in idx]
    exp_d = [jnp.exp(d_col[n]) for n in idx]
    k16 = [tokens(k_ref, n).astype(BF16) for n in idx]
    kb = [tokens(k_ref, n) * beta_col[n] for n in idx]
    nil = [jnp.where(strict, _dot_nt(kb[n].astype(BF16), k16[n]) * gamma[n], 0.0) for n in idx]
    nil_parts = [_split_bf16(x) for x in nil]
    inv = [eye - x for x in nil]
    power = [_dot_split(x, x) for x in nil_parts]
    steps = int(math.log2(c)) - 1
    for it in range(steps):
        power_parts = [_split_bf16(x) for x in power]
        inv = [inv[n] + _dot_split(_split_bf16(inv[n]), power_parts[n]) for n in idx]
        if it + 1 < steps:
            power = [_dot_split(x, x) for x in power_parts]
    rhs = [jnp.concatenate([kb[n] * exp_d[n], tokens(v_ref, n) * beta_col[n]], axis=1) for n in idx]
    wu = [_dot_split(_split_bf16(inv[n]), _split_bf16(rhs[n])).astype(BF16) for n in idx]
    aqk = [jnp.where(tri, _dot_nt(tokens(q_ref, n).astype(BF16), k16[n]) * gamma[n], 0.0).astype(BF16)
           for n in idx]
    kd_t = [(tokens(k_ref, n) * jnp.exp(d_last[n] - d_col[n])).T.astype(BF16) for n in idx]
    state_wu = [_dot(kd_t[n], wu[n]) for n in idx]
    out_wu = [_dot(aqk[n], wu[n]) for n in idx]
    for n, (ch, h) in enumerate(items):
        m_ref[ch, h] = (-state_wu[n][:, :c]).astype(BF16)
        n_ref[ch, h] = state_wu[n][:, c:]
        p_ref[ch, h] = (tokens(q_ref, n) * exp_d[n] - out_wu[n][:, :c]).astype(BF16)
        r_ref[ch, h] = out_wu[n][:, c:]
        cd_ref[ch, h:h + 1, :] = jnp.exp(d_last[n][0:1, :] + jnp.zeros((1, c), F32))


def _gdn_chunks(qn, kn, vv, bg):
    n_rows = qn.shape[0]
    nc = n_rows // LANES
    per_step = GDN_CHUNKS_PER_STEP
    tok = pl.BlockSpec((per_step * LANES, D_GDN), lambda c: (c, 0))
    mat = pl.BlockSpec((per_step, N_GDN_HEADS, LANES, LANES), lambda c: (c, 0, 0, 0))
    mat_shape = lambda dt: jax.ShapeDtypeStruct((nc, N_GDN_HEADS, LANES, LANES), dt)
    return pl.pallas_call(
        _gdn_chunk_kernel,
        out_shape=(mat_shape(BF16), mat_shape(F32), mat_shape(BF16), mat_shape(F32),
                   jax.ShapeDtypeStruct((nc, N_GDN_HEADS, LANES), F32)),
        grid=(nc // per_step,),
        in_specs=[tok, tok, tok, pl.BlockSpec((8, per_step * LANES), lambda c: (0, c))],
        out_specs=(mat, mat, mat, mat, pl.BlockSpec((per_step, N_GDN_HEADS, LANES), lambda c: (c, 0, 0))),
        compiler_params=_params("parallel"),
        name="gdn_chunks",
    )(qn, kn, vv, bg)


def _gdn_scan_kernel(m_ref, n_ref, p_ref, r_ref, cd_ref, o_ref, s_scr, *, batch):
    @pl.when(pl.program_id(0) == 0)
    def _():
        s_scr[...] = jnp.zeros(s_scr.shape, F32)

    for ch in range(m_ref.shape[1]):
        rows = slice(ch * LANES, (ch + 1) * LANES)
        for b in range(batch):
            for h in range(N_GDN_HEADS):
                s = s_scr[b, h]
                s16 = s.astype(BF16)
                o_ref[b, rows, h * GDN_HEAD_DIM:(h + 1) * GDN_HEAD_DIM] = _dot(p_ref[b, ch, h], s16) + r_ref[b, ch, h]
                s_scr[b, h] = s * cd_ref[b, ch, h:h + 1, :] + _dot(m_ref[b, ch, h], s16) + n_ref[b, ch, h]


def _gdn_scan(m_mat, n_mat, p_mat, r_mat, cd, batch, tp):
    nc = tp // LANES
    per_step = next(d for d in SCAN_CHUNKS if nc % d == 0)
    shape5 = lambda a: a.reshape(batch, nc, N_GDN_HEADS, LANES, LANES)
    mat = pl.BlockSpec((batch, per_step, N_GDN_HEADS, LANES, LANES), lambda c: (0, c, 0, 0, 0))
    return pl.pallas_call(
        functools.partial(_gdn_scan_kernel, batch=batch),
        out_shape=jax.ShapeDtypeStruct((batch, tp, D_GDN), F32),
        grid=(nc // per_step,),
        in_specs=[mat, mat, mat, mat,
                  pl.BlockSpec((batch, per_step, N_GDN_HEADS, LANES), lambda c: (0, c, 0, 0))],
        out_specs=pl.BlockSpec((batch, per_step * LANES, D_GDN), lambda c: (0, c, 0)),
        scratch_shapes=[pltpu.VMEM((batch, N_GDN_HEADS, LANES, LANES), F32)],
        compiler_params=_params("arbitrary"),
        name="gdn_scan",
    )(shape5(m_mat), shape5(n_mat), shape5(p_mat), shape5(r_mat), cd.reshape(batch, nc, N_GDN_HEADS, LANES))


def _out_proj_kernel(h_ref, oa_ref, za_ref, og_ref, zg_ref, gain_ref, w_ref, out_ref):
    attn = (oa_ref[...] * _silu(za_ref[...])).astype(BF16)
    y = _dot(attn, w_ref[0:D_ATTN, :])
    og, zg = og_ref[...], zg_ref[...]
    for h in range(N_GDN_HEADS):
        cols = slice(h * GDN_HEAD_DIM, (h + 1) * GDN_HEAD_DIM)
        gated = (_rms_rows(og[:, cols], gain_ref[...]) * _silu(zg[:, cols])).astype(BF16)
        y = y + _dot(gated, w_ref[D_ATTN + h * GDN_HEAD_DIM:D_ATTN + (h + 1) * GDN_HEAD_DIM, :])
    out_ref[...] = h_ref[...] + y


def _out_proj(h2, o_attn, proj, o_gdn, gain, w_out, rb):
    n_rows = h2.shape[0]
    blk = lambda width, col: pl.BlockSpec((rb, width), lambda i: (i, col))
    return pl.pallas_call(
        _out_proj_kernel,
        out_shape=jax.ShapeDtypeStruct((n_rows, D_MODEL), F32),
        grid=(n_rows // rb,),
        in_specs=[blk(D_MODEL, 0), blk(D_ATTN, 0), blk(D_ATTN, COL_Z_ATTN), blk(D_GDN, 0),
                  blk(D_GDN, COL_Z_GDN),
                  pl.BlockSpec((1, GDN_HEAD_DIM), lambda i: (0, 0)),
                  pl.BlockSpec((D_MODEL, D_MODEL), lambda i: (0, 0))],
        out_specs=blk(D_MODEL, 0),
        compiler_params=_params("parallel"),
        name="out_proj",
    )(h2, o_attn, proj, o_gdn, proj, gain, w_out)


def _pack_layer(norm_gain, w_in, cq_gain, ckv_gain, w_uq, w_ukv, w_q_idx, q_gain, k_gain, conv_w, a_log,
                dt_bias, gdn_gain, w_out):
    o = 0
    parts = {}
    for name, size in (("c_q", Q_RANK), ("c_kv", KV_RANK), ("k_idx", IDX_DIM), ("w_idx", N_IDX_HEADS),
                       ("z_attn", D_ATTN), ("qkv_g", 3 * D_GDN), ("z_g", D_GDN), ("b", N_GDN_HEADS),
                       ("a", N_GDN_HEADS)):
        parts[name] = w_in[:, o:o + size]
        o += size
    w_packed = jnp.concatenate([parts["qkv_g"], parts["c_q"], parts["c_kv"], parts["k_idx"], parts["k_idx"],
                                parts["z_attn"], parts["z_g"]], axis=1).astype(BF16)
    w_rows = jnp.concatenate([parts["w_idx"], parts["b"], parts["a"],
                              jnp.zeros((D_MODEL, 4), w_in.dtype)], axis=1).T.astype(BF16)
    return dict(
        gain=norm_gain[None, :], w_packed=w_packed, w_rows=w_rows,
        g_cq=cq_gain[None, :], g_ckv=ckv_gain[None, :],
        w_uqt=w_uq.T.astype(BF16), w_qit=w_q_idx.T.astype(BF16),
        w_ukt=w_ukv[:, :D_ATTN].T.astype(BF16), w_uvt=w_ukv[:, D_ATTN:].T.astype(BF16),
        g_q_col=q_gain[:, None], g_k_col=k_gain[:, None],
        conv_w=conv_w, a_log=a_log[:, None], dt_bias=dt_bias[:, None],
        gdn_gain=gdn_gain[None, :], w_out=w_out.astype(BF16))


def _layer(h2, lw, bias, tri, batch, tp, t_valid, topk):
    rb = _row_block(tp)
    proj, rows = _in_proj(h2, lw["gain"], lw["w_packed"], lw["w_rows"], rb)
    qt, k, vt, qit, kidx = _dsa_prep(proj, lw, batch, tp, rb)
    o_attn = _dsa_attention(qit, rows, qt, kidx, k, vt, bias, tri, batch, tp, t_valid, topk)
    qn, kn, vv, bg = _gdn_prep(proj, rows, lw, batch, tp, rb)
    m_mat, n_mat, p_mat, r_mat, cd = _gdn_chunks(qn, kn, vv, bg)
    o_gdn = _gdn_scan(m_mat, n_mat, p_mat, r_mat, cd, batch, tp).reshape(batch * tp, D_GDN)
    return _out_proj(h2, o_attn, proj, o_gdn, lw["gdn_gain"], lw["w_out"], rb)


def _forward(x, meta_tokens, rel_bias_table, layer_weights, topk):
    batch, seq, _ = x.shape
    t = seq + N_META
    tp = -(-t // KEY_TILE) * KEY_TILE
    meta = jnp.broadcast_to(meta_tokens[None].astype(x.dtype), (batch, N_META, D_MODEL))
    h = jnp.concatenate([meta, x, jnp.zeros((batch, tp - t, D_MODEL), x.dtype)], axis=1)
    h2 = h.reshape(batch * tp, D_MODEL)
    bias = _bias_tiles(rel_bias_table)
    tri = jnp.tril(jnp.ones((KEY_TILE, KEY_TILE), BF16))
    for lw in layer_weights:
        h2 = _layer(h2, lw, bias, tri, batch, tp, t, topk)
    return h2.reshape(batch, tp, D_MODEL)[:, N_META:t]


def kernel(x, meta_tokens, rel_bias_table, norm_gain, w_in, cq_norm_gain, ckv_norm_gain, w_uq, w_ukv, w_q_idx,
           q_norm_gain, k_norm_gain, conv_w, a_log, dt_bias, gdn_norm_gain, w_out):
    depth = norm_gain.shape[0]
    topk = min(TOPK_MAX, x.shape[1] // 4)
    layers = [_pack_layer(norm_gain[l], w_in[l], cq_norm_gain[l], ckv_norm_gain[l], w_uq[l], w_ukv[l],
                          w_q_idx[l], q_norm_gain[l], k_norm_gain[l], conv_w[l], a_log[l], dt_bias[l],
                          gdn_norm_gain[l], w_out[l]) for l in range(depth)]
    return _forward(x, meta_tokens, rel_bias_table, layers, topk)
```

```python
import functools
import math

import jax
import jax.numpy as jnp
from jax import lax
from jax.experimental import pallas as pl
from jax.experimental.pallas import tpu as pltpu

F32 = jnp.float32
BF16 = jnp.bfloat16
I32 = jnp.int32

D_MODEL = 1024
N_META = 16
EPS = 1e-6
N_ATTN_HEADS = 8
ATTN_HEAD_DIM = 64
D_ATTN = N_ATTN_HEADS * ATTN_HEAD_DIM
Q_RANK = 256
KV_RANK = 128
N_IDX_HEADS = 4
IDX_DIM = 64
TOPK_MAX = 256
N_REL_BUCKETS = 32
REL_MAX_DIST = 128
N_GDN_HEADS = 4
GDN_HEAD_DIM = 128
D_GDN = N_GDN_HEADS * GDN_HEAD_DIM
CONV_WIDTH = 4

LANES = 128
KEY_TILE = 256
COUNT_TILE = 512
SCAN_TILE = 1024
FIRST_CHECK_PASSES = 10
INTERPOLATED_PASSES = 24
MAX_SEARCH_PASSES = INTERPOLATED_PASSES + 32
GDN_CHUNKS_PER_STEP = 2
SCAN_CHUNKS = (6, 4, 3, 2, 1)
ROW_TILES = (5, 4, 3, 2, 1)
HALO_ROWS = 8
D_PACKED = 3 * D_GDN + 512 + D_ATTN + D_GDN
COL_SMALL = 3
COL_Z_ATTN = 4
COL_Z_GDN = 5
MASKED_LOGIT = -1e30
LOG2_E = math.log2(math.e)
KEY_MIN = -2 ** 31
PATTERN_NEG_FLT_MAX = KEY_MIN + (1 << 23)
VMEM_LIMIT = 56 * 1024 * 1024

NT_DIMS = (((1,), (1,)), ((), ()))


def _dot(a, b):
    return jnp.dot(a, b, preferred_element_type=F32)


def _dot_nt(a, b):
    return lax.dot_general(a, b, NT_DIMS, preferred_element_type=F32)


def _split_bf16(x):
    hi = x.astype(BF16)
    return hi, (x - hi.astype(F32)).astype(BF16)


def _dot_split(a_parts, b_parts):
    (a_hi, a_lo), (b_hi, b_lo) = a_parts, b_parts
    return _dot(a_hi, b_hi) + (_dot(a_hi, b_lo) + _dot(a_lo, b_hi))


def _sigmoid(x):
    return 1.0 / (1.0 + jnp.exp(-x))


def _silu(x):
    return x * _sigmoid(x)


def _row_block(tp):
    tiles = tp // LANES
    return LANES * next(d for d in ROW_TILES if tiles % d == 0)


def _params(*sem):
    return pltpu.CompilerParams(dimension_semantics=sem, vmem_limit_bytes=VMEM_LIMIT)


def _bias_kernel(table_ref, out_ref):
    row = lax.broadcasted_iota(I32, (LANES, LANES), 0)
    col = lax.broadcasted_iota(I32, (LANES, LANES), 1)
    max_exact = N_REL_BUCKETS // 2
    for kind in range(3):
        dist = col - row + (2 - kind) * LANES
        n = jnp.maximum(dist, 0)
        nf = jnp.maximum(n, 1).astype(F32)
        large = max_exact + (jnp.log(nf / max_exact) / math.log(REL_MAX_DIST / max_exact)
                             * (N_REL_BUCKETS - max_exact)).astype(I32)
        large = jnp.minimum(large, N_REL_BUCKETS - 1)
        bucket = jnp.where(n < max_exact, n, large)
        for h in range(N_ATTN_HEADS):
            tile = jnp.zeros((LANES, LANES), F32)
            for b in range(N_REL_BUCKETS):
                tile = jnp.where(bucket == b, table_ref[b, h], tile)
            far = table_ref[N_REL_BUCKETS - 1, h]
            out_ref[kind, :, h * LANES:(h + 1) * LANES] = (tile - far) * LOG2_E


def _bias_tiles(rel_table):
    return pl.pallas_call(
        _bias_kernel,
        out_shape=jax.ShapeDtypeStruct((3, LANES, N_ATTN_HEADS * LANES), F32),
        in_specs=[pl.BlockSpec(memory_space=pltpu.SMEM)],
        out_specs=pl.BlockSpec(memory_space=pltpu.VMEM),
        name="rel_bias_tiles",
    )(rel_table)


def _in_proj_kernel(h_ref, gain_ref, w_ref, wrows_ref, proj_ref, rows_ref):
    x = h_ref[...]
    y = x * lax.rsqrt(jnp.mean(x * x, axis=-1, keepdims=True) + EPS)
    hn = (y * gain_ref[...]).astype(BF16)
    proj_ref[...] = _dot(hn, w_ref[...])
    rows_ref[...] = _dot_nt(wrows_ref[...], hn)


def _in_proj(h2, gain, w_packed, w_rows, rb):
    n_rows = h2.shape[0]
    grid = (n_rows // rb,)
    return pl.pallas_call(
        _in_proj_kernel,
        out_shape=(jax.ShapeDtypeStruct((n_rows, D_PACKED), F32),
                   jax.ShapeDtypeStruct((16, n_rows), F32)),
        grid=grid,
        in_specs=[pl.BlockSpec((rb, D_MODEL), lambda i: (i, 0)),
                  pl.BlockSpec((1, D_MODEL), lambda i: (0, 0)),
                  pl.BlockSpec((D_MODEL, D_PACKED), lambda i: (0, 0)),
                  pl.BlockSpec((16, D_MODEL), lambda i: (0, 0))],
        out_specs=(pl.BlockSpec((rb, D_PACKED), lambda i: (i, 0)),
                   pl.BlockSpec((16, rb), lambda i: (0, i))),
        compiler_params=_params("parallel"),
        name="in_proj",
    )(h2, gain, w_packed, w_rows)


def _rms_rows(x, gain):
    return x * lax.rsqrt(jnp.mean(x * x, axis=-1, keepdims=True) + EPS) * gain


def _dsa_prep_kernel(sm_ref, gcq_ref, gckv_ref, wuqt_ref, wqit_ref, wukt_ref, wuvt_ref, gq_ref, gk_ref,
                     qt_ref, k_ref, vt_ref, qit_ref, kidx_ref):
    sm = sm_ref[...]
    rb = sm.shape[0]
    cq = _rms_rows(sm[:, :Q_RANK], gcq_ref[...]).astype(BF16)
    ckv = _rms_rows(sm[:, Q_RANK:Q_RANK + KV_RANK], gckv_ref[...]).astype(BF16)
    kidx_ref[...] = sm[:, Q_RANK + KV_RANK:].astype(BF16)
    q3 = _dot_nt(wuqt_ref[...], cq).reshape(N_ATTN_HEADS, ATTN_HEAD_DIM, rb)
    q3 = q3 * lax.rsqrt(jnp.mean(q3 * q3, axis=1, keepdims=True) + EPS) * gq_ref[...][None]
    qt_ref[...] = (q3 * (ATTN_HEAD_DIM ** -0.5 * LOG2_E)).reshape(D_ATTN, rb).astype(BF16)
    k3 = _dot_nt(wukt_ref[...], ckv).reshape(N_ATTN_HEADS, ATTN_HEAD_DIM, rb)
    k3 = k3 * lax.rsqrt(jnp.mean(k3 * k3, axis=1, keepdims=True) + EPS) * gk_ref[...][None]
    k_ref[...] = k3.reshape(D_ATTN, rb).T.astype(BF16)
    vt_ref[...] = _dot_nt(wuvt_ref[...], ckv).astype(BF16)
    qit_ref[...] = _dot_nt(wqit_ref[...], cq).astype(BF16)


def _dsa_prep(proj, lw, batch, tp, rb):
    n_rows = proj.shape[0]
    nb = tp // rb
    d_idx = N_IDX_HEADS * IDX_DIM
    const = lambda shape: pl.BlockSpec(shape, lambda b, i: (0, 0))
    row_spec = lambda width: pl.BlockSpec((rb, width), lambda b, i: (b * nb + i, 0))
    col_spec = lambda height: pl.BlockSpec((None, height, rb), lambda b, i: (b, 0, i))
    return pl.pallas_call(
        _dsa_prep_kernel,
        out_shape=(jax.ShapeDtypeStruct((batch, D_ATTN, tp), BF16),
                   jax.ShapeDtypeStruct((n_rows, D_ATTN), BF16),
                   jax.ShapeDtypeStruct((batch, D_ATTN, tp), BF16),
                   jax.ShapeDtypeStruct((batch, d_idx, tp), BF16),
                   jax.ShapeDtypeStruct((n_rows, LANES), BF16)),
        grid=(batch, nb),
        in_specs=[pl.BlockSpec((rb, 512), lambda b, i: (b * nb + i, COL_SMALL)),
                  const((1, Q_RANK)), const((1, KV_RANK)),
                  const((D_ATTN, Q_RANK)), const((d_idx, Q_RANK)),
                  const((D_ATTN, KV_RANK)), const((D_ATTN, KV_RANK)),
                  const((ATTN_HEAD_DIM, 1)), const((ATTN_HEAD_DIM, 1))],
        out_specs=(col_spec(D_ATTN), row_spec(D_ATTN), col_spec(D_ATTN), col_spec(d_idx), row_spec(LANES)),
        compiler_params=_params("parallel", "parallel"),
        name="dsa_prep",
    )(proj, lw["g_cq"], lw["g_ckv"], lw["w_uqt"], lw["w_qit"], lw["w_ukt"], lw["w_uvt"], lw["g_q_col"],
      lw["g_k_col"])


def _dsa_block(i, qit_ref, rows_ref, qt_ref, kidx_ref, k_ref, vt_ref, bias_ref, tri_ref, o_ref,
               score_scr, wi_scr, wq_scr, s_scr, p_scr, m_scr, l_scr, acc_scr, alpha_scr, mask_scr, tie_scr, *, topk):
    t0 = i * LANES
    n_kt = i // 2 + 1
    n_st = (n_kt + 3) // 4
    hd = ATTN_HEAD_DIM
    pair_w = 2 * LANES
    n_pairs = N_ATTN_HEADS // 2

    zeros_hd = jnp.zeros((hd, LANES), BF16)
    for h in range(N_IDX_HEADS):
        wi_scr[0:IDX_DIM, h * LANES:(h + 1) * LANES] = qit_ref[h * IDX_DIM:(h + 1) * IDX_DIM, :]
    wi_scr[IDX_DIM:, :] = jnp.zeros((LANES - IDX_DIM, N_IDX_HEADS * LANES), BF16)
    for p in range(n_pairs):
        wq_scr[p, 0:hd, 0:LANES] = qt_ref[2 * p * hd:(2 * p + 1) * hd, :]
        wq_scr[p, 0:hd, LANES:] = zeros_hd
        wq_scr[p, hd:, 0:LANES] = zeros_hd
        wq_scr[p, hd:, LANES:] = qt_ref[(2 * p + 1) * hd:(2 * p + 2) * hd, :]

    row = lax.broadcasted_iota(I32, (KEY_TILE, LANES), 0)
    col = lax.broadcasted_iota(I32, (KEY_TILE, LANES), 1)
    w_idx = rows_ref[0:N_IDX_HEADS, :] * (N_IDX_HEADS ** -0.5 * IDX_DIM ** -0.5)

    def key_tile(j):
        return pl.multiple_of(j * KEY_TILE, KEY_TILE)

    def causal(j):
        return (j * KEY_TILE + row) <= (t0 + col)

    def fold8(x, op):
        return op(x.reshape(KEY_TILE // 8, 8, LANES), axis=0)

    def score_step(jc, carry, masked):
        top, bottom, n_nonneg, n_pos = carry
        subs = [jc * (SCAN_TILE // KEY_TILE) + sub for sub in range(SCAN_TILE // KEY_TILE)]
        logits = [_dot(kidx_ref[pl.ds(key_tile(jnp.minimum(j, n_kt - 1)), KEY_TILE), :], wi_scr[...])
                  for j in subs]
        for j, lg in zip(subs, logits):
            score = jnp.zeros((KEY_TILE, LANES), F32)
            for h in range(N_IDX_HEADS):
                score = score + jnp.maximum(lg[:, h * LANES:(h + 1) * LANES], 0.0) * w_idx[h:h + 1, :]
            if masked:
                visible = causal(j)
                seen = jnp.where(visible, score, -jnp.inf)
                bottom = jnp.minimum(bottom, fold8(jnp.where(visible, score, jnp.inf), jnp.min))
            else:
                seen = score
                bottom = jnp.minimum(bottom, fold8(score, jnp.min))
            top = jnp.maximum(top, fold8(seen, jnp.max))
            n_nonneg = n_nonneg + fold8(jnp.where(seen >= 0.0, 1, 0), jnp.sum)
            n_pos = n_pos + fold8(jnp.where(seen > 0.0, 1, 0), jnp.sum)
            score_scr[pl.ds(key_tile(j), KEY_TILE), :] = seen
        return top, bottom, n_nonneg, n_pos

    zeros8 = jnp.zeros((8, LANES), I32)
    carry = (jnp.full((8, LANES), -jnp.inf, F32), jnp.full((8, LANES), jnp.inf, F32), zeros8, zeros8)
    carry = lax.fori_loop(0, n_st - 1, functools.partial(score_step, masked=False), carry)
    top, bottom, n_nonneg, n_pos = score_step(n_st - 1, carry, masked=True)
    top = jnp.max(top, axis=0, keepdims=True)
    bottom = jnp.min(bottom, axis=0, keepdims=True)
    count0 = jnp.sum(n_nonneg, axis=0, keepdims=True)
    count_pos = jnp.sum(n_pos, axis=0, keepdims=True)

    def count_f32(cand, below=None):
        def body(j, carry):
            acc, best = carry
            for part in range(SCAN_TILE // COUNT_TILE):
                start = pl.multiple_of(j * SCAN_TILE + part * COUNT_TILE, COUNT_TILE)
                x = score_scr[pl.ds(start, COUNT_TILE), :]
                ind = jnp.where(x >= cand, 1, 0)
                acc = acc + jnp.sum(ind.reshape(COUNT_TILE // 8, 8, LANES), axis=0)
                if below is not None:
                    under = jnp.where(x < below, x, -jnp.inf)
                    best = jnp.maximum(best, jnp.max(under.reshape(COUNT_TILE // 8, 8, LANES), axis=0))
            return acc, best
        acc, best = lax.fori_loop(0, n_st, body, (jnp.zeros((8, LANES), I32), jnp.full((8, LANES), -jnp.inf, F32)))
        count = jnp.sum(acc, axis=0, keepdims=True)
        if below is None:
            return count
        return count, jnp.max(best, axis=0, keepdims=True)

    def to_pattern(v):
        bits = lax.bitcast_convert_type(v, I32)
        return bits ^ ((bits >> 31) & 0x7FFFFFFF)

    def to_f32(c):
        return lax.bitcast_convert_type(c ^ ((c >> 31) & 0x7FFFFFFF), F32)

    n_visible = t0 + 1 + lax.broadcasted_iota(I32, (1, LANES), 1)
    nonneg = count0 >= topk
    lo = jnp.where(nonneg, 0, to_pattern(bottom))
    hi = jnp.where(nonneg, to_pattern(top) + 1, 0)
    count_lo = jnp.where(nonneg, count0, n_visible)
    count_hi = jnp.where(nonneg, 0, count0)
    few = n_visible < topk
    lo = jnp.where(few, PATTERN_NEG_FLT_MAX, lo)
    zero_tied = nonneg & (count_pos < topk)
    hi = jnp.where(zero_tied, 1, hi)
    count_hi = jnp.where(zero_tied, count_pos, count_hi)
    open_q = jnp.where(few | zero_tied | (count_lo == topk), 0, 1)

    log_topk = math.log(topk)

    def count_error(count):
        return jnp.log(count.astype(F32) + 0.5) - log_topk

    def probe(n_pass, carry, extract):
        lo, hi, count_lo, count_hi, err_lo, err_hi, last_side, open_q = carry
        v_lo, v_hi = to_f32(lo), to_f32(hi)
        frac = err_lo / (err_lo - err_hi)
        frac = jnp.where(count_lo - count_hi <= 4, 0.5, frac)
        guess = to_pattern(v_lo + (v_hi - v_lo) * frac)
        middle = lo + lax.shift_right_logical(hi - lo, 1)
        cand = jnp.where(n_pass >= INTERPOLATED_PASSES, middle, guess)
        cand = jnp.minimum(jnp.maximum(cand, lo + 1), hi - 1)
        is_open = open_q == 1
        if extract:
            count, under_hi = count_f32(to_f32(cand), below=v_hi)
            next_below = to_pattern(under_hi)
            found = is_open & (count_hi == topk - 1)
            is_open = is_open & jnp.logical_not(found)
        else:
            count = count_f32(to_f32(cand))
        raise_lo = is_open & (count >= topk)
        lower_hi = is_open & (count < topk)
        err = count_error(count)
        err_hi = jnp.where(raise_lo & (last_side == 1), err_hi * 0.5, err_hi)
        err_lo = jnp.where(lower_hi & (last_side == -1), err_lo * 0.5, err_lo)
        err_lo = jnp.where(raise_lo, err, err_lo)
        err_hi = jnp.where(lower_hi, err, err_hi)
        lo = jnp.where(raise_lo, cand, lo)
        count_lo = jnp.where(raise_lo, count, count_lo)
        hi = jnp.where(lower_hi, cand, hi)
        count_hi = jnp.where(lower_hi, count, count_hi)
        last_side = jnp.where(raise_lo, 1, jnp.where(lower_hi, -1, last_side))
        if extract:
            lo = jnp.where(found, next_below, lo)
            hi = jnp.where(found | raise_lo, next_below + 1, hi)
        closed = (count_lo == topk) | (hi - lo == 1)
        return lo, hi, count_lo, count_hi, err_lo, err_hi, last_side, jnp.where(closed, 0, open_q)

    carry = (lo, hi, count_lo, count_hi, count_error(count_lo), count_error(count_hi),
             jnp.zeros((1, LANES), I32), open_q)
    n_first = jnp.where(jnp.sum(open_q) > 0, FIRST_CHECK_PASSES, 0)
    carry = lax.fori_loop(0, n_first, functools.partial(probe, extract=False), carry)

    def probes_left(st):
        n_pass, n_open = st[0], st[-1]
        return (n_pass < MAX_SEARCH_PASSES) & (n_open > 0)

    def extracting_probe(st):
        carry = probe(st[0], st[1:-1], extract=True)
        return (st[0] + 1,) + carry + (jnp.sum(carry[-1]),)

    state = lax.while_loop(probes_left, extracting_probe,
                           (jnp.int32(FIRST_CHECK_PASSES),) + carry + (jnp.sum(carry[-1]),))
    lo, count_lo, count_hi = state[1], state[3], state[4]
    tau = to_f32(lo)
    need = jnp.where((count_lo == topk) | few, topk, topk - count_hi).astype(F32)

    m_scr[...] = jnp.full(m_scr.shape, MASKED_LOGIT, F32)
    l_scr[...] = jnp.zeros(l_scr.shape, F32)
    acc_scr[...] = jnp.zeros(acc_scr.shape, F32)
    ones_rows = jnp.ones((16, KEY_TILE), BF16)
    last = n_kt - 1

    def mask_pair(j_first, parity):
        xs = [score_scr[pl.ds(key_tile(jnp.minimum(j_first + slot, last)), KEY_TILE), :] for slot in range(2)]
        ties = [x == tau for x in xs]
        tie_cols = jnp.concatenate([jnp.where(tie, 1.0, 0.0).astype(BF16) for tie in ties], axis=1)
        ranks = _dot(tri_ref[...], tie_cols)
        tie_carry = tie_scr[0:1, :]
        for slot, (x, tie) in enumerate(zip(xs, ties)):
            rank = ranks[:, slot * LANES:(slot + 1) * LANES] + tie_carry
            tie_carry = rank[KEY_TILE - 1:KEY_TILE, :]
            take = (tie & (rank <= need)) | (x > tau)
            mask_scr[parity, slot] = jnp.where(take, 0.0, MASKED_LOGIT)
        tie_scr[0:1, :] = tie_carry

    def qk_pair(slot, j, p):
        s_scr[slot, :, p * pair_w:(p + 1) * pair_w] = _dot(
            k_ref[pl.ds(key_tile(j), KEY_TILE), p * LANES:(p + 1) * LANES], wq_scr[p])

    def softmax_pair(slot, j, p, near, parity):
        mask_add = mask_scr[parity, slot]
        alphas = []
        for h in (2 * p, 2 * p + 1):
            cols = slice(h * LANES, (h + 1) * LANES)
            logits = s_scr[slot, :, cols] + mask_add
            if near:
                kind_top = jnp.clip(2 * j - i + 2, 0, 2)
                kind_bot = jnp.clip(2 * j - i + 3, 0, 2)
                logits = logits + jnp.concatenate(
                    [bias_ref[kind_top, :, cols], bias_ref[kind_bot, :, cols]], axis=0)
            m_old = m_scr[h:h + 1, :]
            m_new = jnp.maximum(m_old, jnp.max(logits, axis=0, keepdims=True))
            m_scr[h:h + 1, :] = m_new
            p_scr[slot, :, cols] = jnp.exp2(logits - m_new).astype(BF16)
            alphas.append(jnp.exp2(m_old - m_new))
        return alphas

    def pv_pair(slot, j, p, alphas):
        lhs = jnp.concatenate([vt_ref[p * 2 * hd:(p + 1) * 2 * hd, pl.ds(key_tile(j), KEY_TILE)], ones_rows],
                              axis=0)
        out = _dot(lhs, p_scr[slot, :, p * pair_w:(p + 1) * pair_w])
        for half in range(2):
            h = 2 * p + half
            rows_h = slice(h * hd, (h + 1) * hd)
            q_cols = slice(half * LANES, (half + 1) * LANES)
            acc_scr[rows_h, :] = acc_scr[rows_h, :] * alphas[half] + out[half * hd:(half + 1) * hd, q_cols]
            l_scr[h:h + 1, :] = l_scr[h:h + 1, :] * alphas[half] + out[2 * hd:2 * hd + 1, q_cols]

    def pending_alphas(p):
        return [alpha_scr[h:h + 1, :] for h in (2 * p, 2 * p + 1)]

    def clear_pending():
        p_scr[1] = jnp.zeros(p_scr.shape[1:], BF16)
        alpha_scr[...] = jnp.ones(alpha_scr.shape, F32)

    def pair_step(ja, j_pending, j_next, near, parity):
        alphas_a = []
        for p in range(n_pairs):
            pv_pair(1, j_pending, p, pending_alphas(p))
            qk_pair(1, ja + 1, p)
            alphas_a.append(softmax_pair(0, ja, p, near, parity))
        for p in range(n_pairs):
            pv_pair(0, ja, p, alphas_a[p])
            qk_pair(0, j_next, p)
            alphas_b = softmax_pair(1, ja + 1, p, near, parity)
            for half in range(2):
                alpha_scr[2 * p + half:2 * p + half + 1, :] = alphas_b[half]
        mask_pair(ja + 2, 1 - parity)

    def single_step(ja, j_pending, near, parity):
        alphas_a = []
        for p in range(n_pairs):
            pv_pair(1, j_pending, p, pending_alphas(p))
            alphas_a.append(softmax_pair(0, ja, p, near, parity))
        for p in range(n_pairs):
            pv_pair(0, ja, p, alphas_a[p])
        clear_pending()

    n_far = 2 * (jnp.maximum(n_kt - 2, 0) // 2)
    n_near = n_kt - n_far
    first_near_parity = (n_far // 2) % 2
    clear_pending()
    tie_scr[...] = jnp.zeros(tie_scr.shape, F32)
    mask_pair(0, 0)
    for p in range(n_pairs):
        qk_pair(0, 0, p)

    def far_body(jp, carry):
        pair_step(2 * jp, jnp.maximum(2 * jp - 1, 0), 2 * jp + 2, near=False, parity=jp % 2)
        return carry

    lax.fori_loop(0, n_far // 2, far_body, 0)

    @pl.when(n_near >= 2)
    def _():
        pair_step(n_far, jnp.maximum(n_far - 1, 0), jnp.minimum(n_far + 2, last), near=True,
                  parity=first_near_parity)

    @pl.when(n_near % 2 == 1)
    def _():
        single_step(last, jnp.where(n_near == 3, n_far + 1, jnp.maximum(n_far - 1, 0)), near=True,
                    parity=jnp.where(n_near == 3, 1 - first_near_parity, first_near_parity))

    for p in range(n_pairs):
        pv_pair(1, last, p, pending_alphas(p))

    for h in range(N_ATTN_HEADS):
        rows_h = slice(h * hd, (h + 1) * hd)
        acc_scr[rows_h, :] = acc_scr[rows_h, :] / l_scr[h:h + 1, :]
    o_ref[...] = acc_scr[...].T


def _dsa_kernel(*refs, topk, t_valid):
    o_ref = refs[8]
    i = pl.program_id(1)
    is_real = i * LANES < t_valid

    @pl.when(is_real)
    def _():
        _dsa_block(i, *refs, topk=topk)

    @pl.when(jnp.logical_not(is_real))
    def _():
        o_ref[...] = jnp.zeros(o_ref.shape, F32)


def _dsa_attention(qit, rows, qt, kidx, k, vt, bias, tri, batch, tp, t_valid, topk):
    n_rows = k.shape[0]
    nqb = tp // LANES
    d_idx = N_IDX_HEADS * IDX_DIM
    n_pairs = N_ATTN_HEADS // 2
    key_rows = -(-tp // SCAN_TILE) * SCAN_TILE
    q_cols = lambda height: pl.BlockSpec((None, height, LANES), lambda b, i: (b, 0, i))
    return pl.pallas_call(
        functools.partial(_dsa_kernel, topk=topk, t_valid=t_valid),
        out_shape=jax.ShapeDtypeStruct((n_rows, D_ATTN), F32),
        grid=(batch, nqb),
        in_specs=[q_cols(d_idx),
                  pl.BlockSpec((16, LANES), lambda b, i: (0, b * nqb + i)),
                  q_cols(D_ATTN),
                  pl.BlockSpec((tp, LANES), lambda b, i: (b, 0)),
                  pl.BlockSpec((tp, D_ATTN), lambda b, i: (b, 0)),
                  pl.BlockSpec((None, D_ATTN, tp), lambda b, i: (b, 0, 0)),
                  pl.BlockSpec((3, LANES, N_ATTN_HEADS * LANES), lambda b, i: (0, 0, 0)),
                  pl.BlockSpec((KEY_TILE, KEY_TILE), lambda b, i: (0, 0))],
        out_specs=pl.BlockSpec((LANES, D_ATTN), lambda b, i: (b * nqb + i, 0)),
        scratch_shapes=[pltpu.VMEM((key_rows, LANES), F32),
                        pltpu.VMEM((LANES, N_IDX_HEADS * LANES), BF16),
                        pltpu.VMEM((n_pairs, LANES, 2 * LANES), BF16),
                        pltpu.VMEM((2, KEY_TILE, N_ATTN_HEADS * LANES), F32),
                        pltpu.VMEM((2, KEY_TILE, N_ATTN_HEADS * LANES), BF16),
                        pltpu.VMEM((N_ATTN_HEADS, LANES), F32),
                        pltpu.VMEM((N_ATTN_HEADS, LANES), F32),
                        pltpu.VMEM((D_ATTN, LANES), F32),
                        pltpu.VMEM((N_ATTN_HEADS, LANES), F32),
                        pltpu.VMEM((2, 2, KEY_TILE, LANES), F32),
                        pltpu.VMEM((8, LANES), F32)],
        compiler_params=_params("parallel", "parallel"),
        name="dsa_attention",
    )(qit, rows, qt, kidx, k, vt, bias, tri)


def _gdn_prep_kernel(x_ref, halo_ref, cw_ref, rows_ref, alog_ref, dtb_ref, q_ref, k_ref, v_ref, bg_ref, buf):
    first = pl.program_id(1) == 0
    rb = x_ref.shape[0]
    buf[0:HALO_ROWS, :] = jnp.where(first, 0.0, halo_ref[...])
    buf[HALO_ROWS:, :] = x_ref[...]
    acc = jnp.zeros((rb, 3 * D_GDN), F32)
    for tap in range(CONV_WIDTH):
        start = HALO_ROWS - (CONV_WIDTH - 1) + tap
        acc = acc + cw_ref[tap:tap + 1, :] * buf[start:start + rb, :]
    y = _silu(acc)
    for h in range(N_GDN_HEADS):
        cols = slice(h * GDN_HEAD_DIM, (h + 1) * GDN_HEAD_DIM)
        qh = y[:, cols]
        kh = y[:, D_GDN + h * GDN_HEAD_DIM:D_GDN + (h + 1) * GDN_HEAD_DIM]
        q_ref[:, cols] = (qh * lax.rsqrt(jnp.sum(qh * qh, axis=-1, keepdims=True) + EPS)
                          * (GDN_HEAD_DIM ** -0.5))
        k_ref[:, cols] = kh * lax.rsqrt(jnp.sum(kh * kh, axis=-1, keepdims=True) + EPS)
    v_ref[...] = y[:, 2 * D_GDN:]
    rows = rows_ref[...]
    beta = _sigmoid(rows[4:8, :])
    a = rows[8:12, :] + dtb_ref[...]
    softplus = jnp.maximum(a, 0.0) + jnp.log1p(jnp.exp(-jnp.abs(a)))
    bg_ref[0:4, :] = beta
    bg_ref[4:8, :] = -jnp.exp(alog_ref[...]) * softplus


def _gdn_prep(proj, rows, lw, batch, tp, rb):
    n_rows = proj.shape[0]
    nb = tp // rb
    halo_per_block = rb // HALO_ROWS
    row_spec = pl.BlockSpec((rb, D_GDN), lambda b, i: (b * nb + i, 0))
    return pl.pallas_call(
        _gdn_prep_kernel,
        out_shape=(jax.ShapeDtypeStruct((n_rows, D_GDN), F32),) * 3
        + (jax.ShapeDtypeStruct((8, n_rows), F32),),
        grid=(batch, nb),
        in_specs=[pl.BlockSpec((rb, 3 * D_GDN), lambda b, i: (b * nb + i, 0)),
                  pl.BlockSpec((HALO_ROWS, 3 * D_GDN),
                               lambda b, i: (jnp.maximum((b * nb + i) * halo_per_block - 1, 0), 0)),
                  pl.BlockSpec((CONV_WIDTH, 3 * D_GDN), lambda b, i: (0, 0)),
                  pl.BlockSpec((16, rb), lambda b, i: (0, b * nb + i)),
                  pl.BlockSpec((N_GDN_HEADS, 1), lambda b, i: (0, 0)),
                  pl.BlockSpec((N_GDN_HEADS, 1), lambda b, i: (0, 0))],
        out_specs=(row_spec, row_spec, row_spec,
                   pl.BlockSpec((8, rb), lambda b, i: (0, b * nb + i))),
        scratch_shapes=[pltpu.VMEM((HALO_ROWS + rb, 3 * D_GDN), F32)],
        compiler_params=_params("parallel", "parallel"),
        name="gdn_prep",
    )(proj, proj, lw["conv_w"], rows, lw["a_log"], lw["dt_bias"])


def _gdn_chunk_kernel(q_ref, k_ref, v_ref, bg_ref, m_ref, n_ref, p_ref, r_ref, cd_ref):
    c = LANES
    n_chunks = q_ref.shape[0] // c
    items = [(ch, h) for ch in range(n_chunks) for h in range(N_GDN_HEADS)]
    idx = range(len(items))
    row = lax.broadcasted_iota(I32, (c, c), 0)
    col = lax.broadcasted_iota(I32, (c, c), 1)
    tri = row >= col
    strict = row > col
    eye = jnp.where(row == col, 1.0, 0.0)
    lane8 = lax.broadcasted_iota(I32, (8, c), 1)
    gates, decays = [], []
    for ch in range(n_chunks):
        bg = bg_ref[:, ch * c:(ch + 1) * c]
        dec = bg
        shift = 1
        while shift < c:
            dec = dec + jnp.where(lane8 >= shift, pltpu.roll(dec, shift, 1), 0.0)
            shift *= 2
        gates.append(bg)
        decays.append(dec)

    def tokens(ref, n):
        ch, h = items[n]
        return ref[ch * c:(ch + 1) * c, h * GDN_HEAD_DIM:(h + 1) * GDN_HEAD_DIM]

    d_row = [jnp.broadcast_to(decays[ch][4 + h:5 + h, :], (c, c)) for ch, h in items]
    d_col = [d.T for d in d_row]
    beta_col = [jnp.broadcast_to(gates[ch][h:h + 1, :], (c, c)).T for ch, h in items]
    d_last = [d[:, c - 1:c] for d in d_row]
    gamma = [jnp.exp(jnp.where(tri, d_col[n] - d_row[n], MASKED_LOGIT)) for n in idx]
    exp_d = [jnp.exp(d_col[n]) for n in idx]
    k16 = [tokens(k_ref, n).astype(BF16) for n in idx]
    kb = [tokens(k_ref, n) * beta_col[n] for n in idx]
    nil = [jnp.where(strict, _dot_nt(kb[n].astype(BF16), k16[n]) * gamma[n], 0.0) for n in idx]
    block = 8
    same = lambda size: (row // size) == (col // size)
    diag = [jnp.where(same(block), x, 0.0) for x in nil]
    diag_parts = [_split_bf16(x) for x in diag]
    inv = [eye - x for x in diag]
    power = [_dot_split(x, x) for x in diag_parts]
    for it in range(2):
        power_parts = [_split_bf16(x) for x in power]
        inv = [inv[n] + _dot_split(_split_bf16(inv[n]), power_parts[n]) for n in idx]
        if it == 0:
            power = [_dot_split(x, x) for x in power_parts]
    while block < c:
        couples = same(2 * block) & ((row // block) % 2 == 1) & ((col // block) % 2 == 0)
        inv_parts = [_split_bf16(x) for x in inv]
        lower = [_dot_split(_split_bf16(jnp.where(couples, nil[n], 0.0)), inv_parts[n]) for n in idx]
        inv = [inv[n] - _dot_split(inv_parts[n], _split_bf16(lower[n])) for n in idx]
        block *= 2
    rhs = [jnp.concatenate([kb[n] * exp_d[n], tokens(v_ref, n) * beta_col[n]], axis=1) for n in idx]
    wu = [_dot_split(_split_bf16(inv[n]), _split_bf16(rhs[n])).astype(BF16) for n in idx]
    aqk = [jnp.where(tri, _dot_nt(tokens(q_ref, n).astype(BF16), k16[n]) * gamma[n], 0.0).astype(BF16)
           for n in idx]
    kd_t = [(tokens(k_ref, n) * jnp.exp(d_last[n] - d_col[n])).T.astype(BF16) for n in idx]
    state_wu = [_dot(kd_t[n], wu[n]) for n in idx]
    out_wu = [_dot(aqk[n], wu[n]) for n in idx]
    for n, (ch, h) in enumerate(items):
        m_ref[ch, h] = (-state_wu[n][:, :c]).astype(BF16)
        n_ref[ch, h] = state_wu[n][:, c:]
        p_ref[ch, h] = (tokens(q_ref, n) * exp_d[n] - out_wu[n][:, :c]).astype(BF16)
        r_ref[ch, h] = out_wu[n][:, c:]
        cd_ref[ch, h:h + 1, :] = jnp.exp(d_last[n][0:1, :] + jnp.zeros((1, c), F32))


def _gdn_chunks(qn, kn, vv, bg):
    n_rows = qn.shape[0]
    nc = n_rows // LANES
    per_step = GDN_CHUNKS_PER_STEP
    tok = pl.BlockSpec((per_step * LANES, D_GDN), lambda c: (c, 0))
    mat = pl.BlockSpec((per_step, N_GDN_HEADS, LANES, LANES), lambda c: (c, 0, 0, 0))
    mat_shape = lambda dt: jax.ShapeDtypeStruct((nc, N_GDN_HEADS, LANES, LANES), dt)
    return pl.pallas_call(
        _gdn_chunk_kernel,
        out_shape=(mat_shape(BF16), mat_shape(F32), mat_shape(BF16), mat_shape(F32),
                   jax.ShapeDtypeStruct((nc, N_GDN_HEADS, LANES), F32)),
        grid=(nc // per_step,),
        in_specs=[tok, tok, tok, pl.BlockSpec((8, per_step * LANES), lambda c: (0, c))],
        out_specs=(mat, mat, mat, mat, pl.BlockSpec((per_step, N_GDN_HEADS, LANES), lambda c: (c, 0, 0))),
        compiler_params=_params("parallel"),
        name="gdn_chunks",
    )(qn, kn, vv, bg)


def _gdn_scan_kernel(m_ref, n_ref, p_ref, r_ref, cd_ref, o_ref, s_scr, *, batch):
    @pl.when(pl.program_id(0) == 0)
    def _():
        s_scr[...] = jnp.zeros(s_scr.shape, F32)

    for ch in range(m_ref.shape[1]):
        rows = slice(ch * LANES, (ch + 1) * LANES)
        for b in range(batch):
            for h in range(N_GDN_HEADS):
                s = s_scr[b, h]
                s16 = s.astype(BF16)
                o_ref[b, rows, h * GDN_HEAD_DIM:(h + 1) * GDN_HEAD_DIM] = _dot(p_ref[b, ch, h], s16) + r_ref[b, ch, h]
                s_scr[b, h] = s * cd_ref[b, ch, h:h + 1, :] + _dot(m_ref[b, ch, h], s16) + n_ref[b, ch, h]


def _gdn_scan(m_mat, n_mat, p_mat, r_mat, cd, batch, tp):
    nc = tp // LANES
    per_step = next(d for d in SCAN_CHUNKS if nc % d == 0)
    shape5 = lambda a: a.reshape(batch, nc, N_GDN_HEADS, LANES, LANES)
    mat = pl.BlockSpec((batch, per_step, N_GDN_HEADS, LANES, LANES), lambda c: (0, c, 0, 0, 0))
    return pl.pallas_call(
        functools.partial(_gdn_scan_kernel, batch=batch),
        out_shape=jax.ShapeDtypeStruct((batch, tp, D_GDN), F32),
        grid=(nc // per_step,),
        in_specs=[mat, mat, mat, mat,
                  pl.BlockSpec((batch, per_step, N_GDN_HEADS, LANES), lambda c: (0, c, 0, 0))],
        out_specs=pl.BlockSpec((batch, per_step * LANES, D_GDN), lambda c: (0, c, 0)),
        scratch_shapes=[pltpu.VMEM((batch, N_GDN_HEADS, LANES, LANES), F32)],
        compiler_params=_params("arbitrary"),
        name="gdn_scan",
    )(shape5(m_mat), shape5(n_mat), shape5(p_mat), shape5(r_mat), cd.reshape(batch, nc, N_GDN_HEADS, LANES))


def _out_proj_kernel(h_ref, oa_ref, za_ref, og_ref, zg_ref, gain_ref, w_ref, out_ref):
    attn = (oa_ref[...] * _silu(za_ref[...])).astype(BF16)
    y = _dot(attn, w_ref[0:D_ATTN, :])
    og, zg = og_ref[...], zg_ref[...]
    for h in range(N_GDN_HEADS):
        cols = slice(h * GDN_HEAD_DIM, (h + 1) * GDN_HEAD_DIM)
        gated = (_rms_rows(og[:, cols], gain_ref[...]) * _silu(zg[:, cols])).astype(BF16)
        y = y + _dot(gated, w_ref[D_ATTN + h * GDN_HEAD_DIM:D_ATTN + (h + 1) * GDN_HEAD_DIM, :])
    out_ref[...] = h_ref[...] + y


def _out_proj(h2, o_attn, proj, o_gdn, gain, w_out, rb):
    n_rows = h2.shape[0]
    blk = lambda width, col: pl.BlockSpec((rb, width), lambda i: (i, col))
    return pl.pallas_call(
        _out_proj_kernel,
        out_shape=jax.ShapeDtypeStruct((n_rows, D_MODEL), F32),
        grid=(n_rows // rb,),
        in_specs=[blk(D_MODEL, 0), blk(D_ATTN, 0), blk(D_ATTN, COL_Z_ATTN), blk(D_GDN, 0),
                  blk(D_GDN, COL_Z_GDN),
                  pl.BlockSpec((1, GDN_HEAD_DIM), lambda i: (0, 0)),
                  pl.BlockSpec((D_MODEL, D_MODEL), lambda i: (0, 0))],
        out_specs=blk(D_MODEL, 0),
        compiler_params=_params("parallel"),
        name="out_proj",
    )(h2, o_attn, proj, o_gdn, proj, gain, w_out)


def _pack_layer(norm_gain, w_in, cq_gain, ckv_gain, w_uq, w_ukv, w_q_idx, q_gain, k_gain, conv_w, a_log,
                dt_bias, gdn_gain, w_out):
    o = 0
    parts = {}
    for name, size in (("c_q", Q_RANK), ("c_kv", KV_RANK), ("k_idx", IDX_DIM), ("w_idx", N_IDX_HEADS),
                       ("z_attn", D_ATTN), ("qkv_g", 3 * D_GDN), ("z_g", D_GDN), ("b", N_GDN_HEADS),
                       ("a", N_GDN_HEADS)):
        parts[name] = w_in[:, o:o + size]
        o += size
    w_packed = jnp.concatenate([parts["qkv_g"], parts["c_q"], parts["c_kv"], parts["k_idx"], parts["k_idx"],
                                parts["z_attn"], parts["z_g"]], axis=1).astype(BF16)
    w_rows = jnp.concatenate([parts["w_idx"], parts["b"], parts["a"],
                              jnp.zeros((D_MODEL, 4), w_in.dtype)], axis=1).T.astype(BF16)
    return dict(
        gain=norm_gain[None, :], w_packed=w_packed, w_rows=w_rows,
        g_cq=cq_gain[None, :], g_ckv=ckv_gain[None, :],
        w_uqt=w_uq.T.astype(BF16), w_qit=w_q_idx.T.astype(BF16),
        w_ukt=w_ukv[:, :D_ATTN].T.astype(BF16), w_uvt=w_ukv[:, D_ATTN:].T.astype(BF16),
        g_q_col=q_gain[:, None], g_k_col=k_gain[:, None],
        conv_w=conv_w, a_log=a_log[:, None], dt_bias=dt_bias[:, None],
        gdn_gain=gdn_gain[None, :], w_out=w_out.astype(BF16))


def _layer(h2, lw, bias, tri, batch, tp, t_valid, topk):
    rb = _row_block(tp)
    proj, rows = _in_proj(h2, lw["gain"], lw["w_packed"], lw["w_rows"], rb)
    qt, k, vt, qit, kidx = _dsa_prep(proj, lw, batch, tp, rb)
    o_attn = _dsa_attention(qit, rows, qt, kidx, k, vt, bias, tri, batch, tp, t_valid, topk)
    qn, kn, vv, bg = _gdn_prep(proj, rows, lw, batch, tp, rb)
    m_mat, n_mat, p_mat, r_mat, cd = _gdn_chunks(qn, kn, vv, bg)
    o_gdn = _gdn_scan(m_mat, n_mat, p_mat, r_mat, cd, batch, tp).reshape(batch * tp, D_GDN)
    return _out_proj(h2, o_attn, proj, o_gdn, lw["gdn_gain"], lw["w_out"], rb)


def _forward(x, meta_tokens, rel_bias_table, layer_weights, topk):
    batch, seq, _ = x.shape
    t = seq + N_META
    tp = -(-t // KEY_TILE) * KEY_TILE
    meta = jnp.broadcast_to(meta_tokens[None].astype(x.dtype), (batch, N_META, D_MODEL))
    h = jnp.concatenate([meta, x, jnp.zeros((batch, tp - t, D_MODEL), x.dtype)], axis=1)
    h2 = h.reshape(batch * tp, D_MODEL)
    bias = _bias_tiles(rel_bias_table)
    tri = jnp.tril(jnp.ones((KEY_TILE, KEY_TILE), BF16))
    for lw in layer_weights:
        h2 = _layer(h2, lw, bias, tri, batch, tp, t, topk)
    return h2.reshape(batch, tp, D_MODEL)[:, N_META:t]


def kernel(x, meta_tokens, rel_bias_table, norm_gain, w_in, cq_norm_gain, ckv_norm_gain, w_uq, w_ukv, w_q_idx,
           q_norm_gain, k_norm_gain, conv_w, a_log, dt_bias, gdn_norm_gain, w_out):
    depth = norm_gain.shape[0]
    topk = min(TOPK_MAX, x.shape[1] // 4)
    layers = [_pack_layer(norm_gain[l], w_in[l], cq_norm_gain[l], ckv_norm_gain[l], w_uq[l], w_ukv[l],
                          w_q_idx[l], q_norm_gain[l], k_norm_gain[l], conv_w[l], a_log[l], dt_bias[l],
                          gdn_norm_gain[l], w_out[l]) for l in range(depth)]
    return _forward(x, meta_tokens, rel_bias_table, layers, topk)
```

```python
import functools
import math

import jax
import jax.numpy as jnp
from jax import lax
from jax.experimental import pallas as pl
from jax.experimental.pallas import tpu as pltpu

F32 = jnp.float32
BF16 = jnp.bfloat16
I32 = jnp.int32

D_MODEL = 1024
N_META = 16
EPS = 1e-6
N_ATTN_HEADS = 8
ATTN_HEAD_DIM = 64
D_ATTN = N_ATTN_HEADS * ATTN_HEAD_DIM
Q_RANK = 256
KV_RANK = 128
N_IDX_HEADS = 4
IDX_DIM = 64
TOPK_MAX = 256
N_REL_BUCKETS = 32
REL_MAX_DIST = 128
N_GDN_HEADS = 4
GDN_HEAD_DIM = 128
D_GDN = N_GDN_HEADS * GDN_HEAD_DIM
CONV_WIDTH = 4

LANES = 128
KEY_TILE = 256
COUNT_TILE = 512
SCAN_TILE = 1024
FIRST_CHECK_PASSES = 10
INTERPOLATED_PASSES = 24
MAX_SEARCH_PASSES = INTERPOLATED_PASSES + 32
GDN_CHUNKS_PER_STEP = 2
SCAN_CHUNKS = (6, 4, 3, 2, 1)
ROW_TILES = (6, 5, 4, 3, 2, 1)
HALO_ROWS = 8
D_PACKED = 3 * D_GDN + 512 + D_ATTN + D_GDN
COL_SMALL = 3
COL_Z_ATTN = 4
COL_Z_GDN = 5
MASKED_LOGIT = -1e30
LOG2_E = math.log2(math.e)
KEY_MIN = -2 ** 31
PATTERN_NEG_FLT_MAX = KEY_MIN + (1 << 23)
VMEM_LIMIT = 56 * 1024 * 1024

NT_DIMS = (((1,), (1,)), ((), ()))


def _dot(a, b):
    return jnp.dot(a, b, preferred_element_type=F32)


def _dot_nt(a, b):
    return lax.dot_general(a, b, NT_DIMS, preferred_element_type=F32)


def _split_bf16(x):
    hi = x.astype(BF16)
    return hi, (x - hi.astype(F32)).astype(BF16)


def _dot_split(a_parts, b_parts):
    (a_hi, a_lo), (b_hi, b_lo) = a_parts, b_parts
    return _dot(a_hi, b_hi) + (_dot(a_hi, b_lo) + _dot(a_lo, b_hi))


def _sigmoid(x):
    return 1.0 / (1.0 + jnp.exp(-x))


def _silu(x):
    return x * _sigmoid(x)


def _row_block(tp):
    tiles = tp // LANES
    return LANES * next(d for d in ROW_TILES if tiles % d == 0)


def _params(*sem):
    return pltpu.CompilerParams(dimension_semantics=sem, vmem_limit_bytes=VMEM_LIMIT)


def _bias_kernel(table_ref, out_ref):
    row = lax.broadcasted_iota(I32, (LANES, LANES), 0)
    col = lax.broadcasted_iota(I32, (LANES, LANES), 1)
    max_exact = N_REL_BUCKETS // 2
    for kind in range(3):
        dist = col - row + (2 - kind) * LANES
        n = jnp.maximum(dist, 0)
        nf = jnp.maximum(n, 1).astype(F32)
        large = max_exact + (jnp.log(nf / max_exact) / math.log(REL_MAX_DIST / max_exact)
                             * (N_REL_BUCKETS - max_exact)).astype(I32)
        large = jnp.minimum(large, N_REL_BUCKETS - 1)
        bucket = jnp.where(n < max_exact, n, large)
        for h in range(N_ATTN_HEADS):
            tile = jnp.zeros((LANES, LANES), F32)
            for b in range(N_REL_BUCKETS):
                tile = jnp.where(bucket == b, table_ref[b, h], tile)
            far = table_ref[N_REL_BUCKETS - 1, h]
            out_ref[kind, :, h * LANES:(h + 1) * LANES] = (tile - far) * LOG2_E


def _bias_tiles(rel_table):
    return pl.pallas_call(
        _bias_kernel,
        out_shape=jax.ShapeDtypeStruct((3, LANES, N_ATTN_HEADS * LANES), F32),
        in_specs=[pl.BlockSpec(memory_space=pltpu.SMEM)],
        out_specs=pl.BlockSpec(memory_space=pltpu.VMEM),
        name="rel_bias_tiles",
    )(rel_table)


def _in_proj_kernel(h_ref, gain_ref, w_ref, wrows_ref, proj_ref, rows_ref):
    x = h_ref[...]
    y = x * lax.rsqrt(jnp.mean(x * x, axis=-1, keepdims=True) + EPS)
    hn = (y * gain_ref[...]).astype(BF16)
    proj_ref[...] = _dot(hn, w_ref[...])
    rows_ref[...] = _dot_nt(wrows_ref[...], hn)


def _in_proj(h2, gain, w_packed, w_rows, rb):
    n_rows = h2.shape[0]
    grid = (n_rows // rb,)
    return pl.pallas_call(
        _in_proj_kernel,
        out_shape=(jax.ShapeDtypeStruct((n_rows, D_PACKED), F32),
                   jax.ShapeDtypeStruct((16, n_rows), F32)),
        grid=grid,
        in_specs=[pl.BlockSpec((rb, D_MODEL), lambda i: (i, 0)),
                  pl.BlockSpec((1, D_MODEL), lambda i: (0, 0)),
                  pl.BlockSpec((D_MODEL, D_PACKED), lambda i: (0, 0)),
                  pl.BlockSpec((16, D_MODEL), lambda i: (0, 0))],
        out_specs=(pl.BlockSpec((rb, D_PACKED), lambda i: (i, 0)),
                   pl.BlockSpec((16, rb), lambda i: (0, i))),
        compiler_params=_params("parallel"),
        name="in_proj",
    )(h2, gain, w_packed, w_rows)


def _rms_rows(x, gain):
    return x * lax.rsqrt(jnp.mean(x * x, axis=-1, keepdims=True) + EPS) * gain


def _dsa_prep_kernel(sm_ref, gcq_ref, gckv_ref, wuqt_ref, wqit_ref, wukt_ref, wuvt_ref, gq_ref, gk_ref,
                     qt_ref, k_ref, vt_ref, qit_ref, kidx_ref):
    sm = sm_ref[...]
    rb = sm.shape[0]
    cq = _rms_rows(sm[:, :Q_RANK], gcq_ref[...]).astype(BF16)
    ckv = _rms_rows(sm[:, Q_RANK:Q_RANK + KV_RANK], gckv_ref[...]).astype(BF16)
    kidx_ref[...] = sm[:, Q_RANK + KV_RANK:].astype(BF16)
    q3 = _dot_nt(wuqt_ref[...], cq).reshape(N_ATTN_HEADS, ATTN_HEAD_DIM, rb)
    q3 = q3 * lax.rsqrt(jnp.mean(q3 * q3, axis=1, keepdims=True) + EPS) * gq_ref[...][None]
    qt_ref[...] = (q3 * (ATTN_HEAD_DIM ** -0.5 * LOG2_E)).reshape(D_ATTN, rb).astype(BF16)
    k3 = _dot_nt(wukt_ref[...], ckv).reshape(N_ATTN_HEADS, ATTN_HEAD_DIM, rb)
    k3 = k3 * lax.rsqrt(jnp.mean(k3 * k3, axis=1, keepdims=True) + EPS) * gk_ref[...][None]
    k_ref[...] = k3.reshape(D_ATTN, rb).T.astype(BF16)
    vt_ref[...] = _dot_nt(wuvt_ref[...], ckv).astype(BF16)
    qit_ref[...] = _dot_nt(wqit_ref[...], cq).astype(BF16)


def _dsa_prep(proj, lw, batch, tp, rb):
    n_rows = proj.shape[0]
    nb = tp // rb
    d_idx = N_IDX_HEADS * IDX_DIM
    const = lambda shape: pl.BlockSpec(shape, lambda b, i: (0, 0))
    row_spec = lambda width: pl.BlockSpec((rb, width), lambda b, i: (b * nb + i, 0))
    col_spec = lambda height: pl.BlockSpec((None, height, rb), lambda b, i: (b, 0, i))
    return pl.pallas_call(
        _dsa_prep_kernel,
        out_shape=(jax.ShapeDtypeStruct((batch, D_ATTN, tp), BF16),
                   jax.ShapeDtypeStruct((n_rows, D_ATTN), BF16),
                   jax.ShapeDtypeStruct((batch, D_ATTN, tp), BF16),
                   jax.ShapeDtypeStruct((batch, d_idx, tp), BF16),
                   jax.ShapeDtypeStruct((n_rows, LANES), BF16)),
        grid=(batch, nb),
        in_specs=[pl.BlockSpec((rb, 512), lambda b, i: (b * nb + i, COL_SMALL)),
                  const((1, Q_RANK)), const((1, KV_RANK)),
                  const((D_ATTN, Q_RANK)), const((d_idx, Q_RANK)),
                  const((D_ATTN, KV_RANK)), const((D_ATTN, KV_RANK)),
                  const((ATTN_HEAD_DIM, 1)), const((ATTN_HEAD_DIM, 1))],
        out_specs=(col_spec(D_ATTN), row_spec(D_ATTN), col_spec(D_ATTN), col_spec(d_idx), row_spec(LANES)),
        compiler_params=_params("parallel", "parallel"),
        name="dsa_prep",
    )(proj, lw["g_cq"], lw["g_ckv"], lw["w_uqt"], lw["w_qit"], lw["w_ukt"], lw["w_uvt"], lw["g_q_col"],
      lw["g_k_col"])


def _dsa_block(i, qit_ref, rows_ref, qt_ref, kidx_ref, k_ref, vt_ref, bias_ref, tri_ref, o_ref,
               score_scr, wi_scr, wq_scr, s_scr, p_scr, m_scr, l_scr, acc_scr, alpha_scr, mask_scr, tie_scr, *, topk):
    t0 = i * LANES
    n_kt = i // 2 + 1
    n_st = (n_kt + 3) // 4
    hd = ATTN_HEAD_DIM
    pair_w = 2 * LANES
    n_pairs = N_ATTN_HEADS // 2

    zeros_hd = jnp.zeros((hd, LANES), BF16)
    for h in range(N_IDX_HEADS):
        wi_scr[0:IDX_DIM, h * LANES:(h + 1) * LANES] = qit_ref[h * IDX_DIM:(h + 1) * IDX_DIM, :]
    wi_scr[IDX_DIM:, :] = jnp.zeros((LANES - IDX_DIM, N_IDX_HEADS * LANES), BF16)
    for p in range(n_pairs):
        wq_scr[p, 0:hd, 0:LANES] = qt_ref[2 * p * hd:(2 * p + 1) * hd, :]
        wq_scr[p, 0:hd, LANES:] = zeros_hd
        wq_scr[p, hd:, 0:LANES] = zeros_hd
        wq_scr[p, hd:, LANES:] = qt_ref[(2 * p + 1) * hd:(2 * p + 2) * hd, :]

    row = lax.broadcasted_iota(I32, (KEY_TILE, LANES), 0)
    col = lax.broadcasted_iota(I32, (KEY_TILE, LANES), 1)
    w_idx = rows_ref[0:N_IDX_HEADS, :] * (N_IDX_HEADS ** -0.5 * IDX_DIM ** -0.5)

    def key_tile(j):
        return pl.multiple_of(j * KEY_TILE, KEY_TILE)

    def causal(j):
        return (j * KEY_TILE + row) <= (t0 + col)

    def fold8(x, op):
        return op(x.reshape(KEY_TILE // 8, 8, LANES), axis=0)

    def score_step(jc, carry, masked):
        top, bottom, n_nonneg, n_pos = carry
        subs = [jc * (SCAN_TILE // KEY_TILE) + sub for sub in range(SCAN_TILE // KEY_TILE)]
        logits = [_dot(kidx_ref[pl.ds(key_tile(jnp.minimum(j, n_kt - 1)), KEY_TILE), :], wi_scr[...])
                  for j in subs]
        for j, lg in zip(subs, logits):
            score = jnp.zeros((KEY_TILE, LANES), F32)
            for h in range(N_IDX_HEADS):
                score = score + jnp.maximum(lg[:, h * LANES:(h + 1) * LANES], 0.0) * w_idx[h:h + 1, :]
            if masked:
                visible = causal(j)
                seen = jnp.where(visible, score, -jnp.inf)
                bottom = jnp.minimum(bottom, fold8(jnp.where(visible, score, jnp.inf), jnp.min))
            else:
                seen = score
                bottom = jnp.minimum(bottom, fold8(score, jnp.min))
            top = jnp.maximum(top, fold8(seen, jnp.max))
            n_nonneg = n_nonneg + fold8(jnp.where(seen >= 0.0, 1, 0), jnp.sum)
            n_pos = n_pos + fold8(jnp.where(seen > 0.0, 1, 0), jnp.sum)
            score_scr[pl.ds(key_tile(j), KEY_TILE), :] = seen
        return top, bottom, n_nonneg, n_pos

    zeros8 = jnp.zeros((8, LANES), I32)
    carry = (jnp.full((8, LANES), -jnp.inf, F32), jnp.full((8, LANES), jnp.inf, F32), zeros8, zeros8)
    carry = lax.fori_loop(0, n_st - 1, functools.partial(score_step, masked=False), carry)
    top, bottom, n_nonneg, n_pos = score_step(n_st - 1, carry, masked=True)
    top = jnp.max(top, axis=0, keepdims=True)
    bottom = jnp.min(bottom, axis=0, keepdims=True)
    count0 = jnp.sum(n_nonneg, axis=0, keepdims=True)
    count_pos = jnp.sum(n_pos, axis=0, keepdims=True)

    def count_f32(cand, below=None):
        def body(j, carry):
            acc, best = carry
            for part in range(SCAN_TILE // COUNT_TILE):
                start = pl.multiple_of(j * SCAN_TILE + part * COUNT_TILE, COUNT_TILE)
                x = score_scr[pl.ds(start, COUNT_TILE), :]
                ind = jnp.where(x >= cand, 1, 0)
                acc = acc + jnp.sum(ind.reshape(COUNT_TILE // 8, 8, LANES), axis=0)
                if below is not None:
                    under = jnp.where(x < below, x, -jnp.inf)
                    best = jnp.maximum(best, jnp.max(under.reshape(COUNT_TILE // 8, 8, LANES), axis=0))
            return acc, best
        acc, best = lax.fori_loop(0, n_st, body, (jnp.zeros((8, LANES), I32), jnp.full((8, LANES), -jnp.inf, F32)))
        count = jnp.sum(acc, axis=0, keepdims=True)
        if below is None:
            return count
        return count, jnp.max(best, axis=0, keepdims=True)

    def to_pattern(v):
        bits = lax.bitcast_convert_type(v, I32)
        return bits ^ ((bits >> 31) & 0x7FFFFFFF)

    def to_f32(c):
        return lax.bitcast_convert_type(c ^ ((c >> 31) & 0x7FFFFFFF), F32)

    n_visible = t0 + 1 + lax.broadcasted_iota(I32, (1, LANES), 1)
    nonneg = count0 >= topk
    lo = jnp.where(nonneg, 0, to_pattern(bottom))
    hi = jnp.where(nonneg, to_pattern(top) + 1, 0)
    count_lo = jnp.where(nonneg, count0, n_visible)
    count_hi = jnp.where(nonneg, 0, count0)
    few = n_visible < topk
    lo = jnp.where(few, PATTERN_NEG_FLT_MAX, lo)
    zero_tied = nonneg & (count_pos < topk)
    hi = jnp.where(zero_tied, 1, hi)
    count_hi = jnp.where(zero_tied, count_pos, count_hi)
    open_q = jnp.where(few | zero_tied | (count_lo == topk), 0, 1)

    log_topk = math.log(topk)

    def count_error(count):
        return jnp.log(count.astype(F32) + 0.5) - log_topk

    def probe(n_pass, carry, extract):
        lo, hi, count_lo, count_hi, err_lo, err_hi, last_side, open_q = carry
        v_lo, v_hi = to_f32(lo), to_f32(hi)
        frac = err_lo / (err_lo - err_hi)
        frac = jnp.where(count_lo - count_hi <= 4, 0.5, frac)
        guess = to_pattern(v_lo + (v_hi - v_lo) * frac)
        middle = lo + lax.shift_right_logical(hi - lo, 1)
        cand = jnp.where(n_pass >= INTERPOLATED_PASSES, middle, guess)
        cand = jnp.minimum(jnp.maximum(cand, lo + 1), hi - 1)
        is_open = open_q == 1
        if extract:
            count, under_hi = count_f32(to_f32(cand), below=v_hi)
            next_below = to_pattern(under_hi)
            found = is_open & (count_hi == topk - 1)
            is_open = is_open & jnp.logical_not(found)
        else:
            count = count_f32(to_f32(cand))
        raise_lo = is_open & (count >= topk)
        lower_hi = is_open & (count < topk)
        err = count_error(count)
        err_hi = jnp.where(raise_lo & (last_side == 1), err_hi * 0.5, err_hi)
        err_lo = jnp.where(lower_hi & (last_side == -1), err_lo * 0.5, err_lo)
        err_lo = jnp.where(raise_lo, err, err_lo)
        err_hi = jnp.where(lower_hi, err, err_hi)
        lo = jnp.where(raise_lo, cand, lo)
        count_lo = jnp.where(raise_lo, count, count_lo)
        hi = jnp.where(lower_hi, cand, hi)
        count_hi = jnp.where(lower_hi, count, count_hi)
        last_side = jnp.where(raise_lo, 1, jnp.where(lower_hi, -1, last_side))
        if extract:
            lo = jnp.where(found, next_below, lo)
            hi = jnp.where(found | raise_lo, next_below + 1, hi)
        closed = (count_lo == topk) | (hi - lo == 1)
        return lo, hi, count_lo, count_hi, err_lo, err_hi, last_side, jnp.where(closed, 0, open_q)

    carry = (lo, hi, count_lo, count_hi, count_error(count_lo), count_error(count_hi),
             jnp.zeros((1, LANES), I32), open_q)
    n_first = jnp.where(jnp.sum(open_q) > 0, FIRST_CHECK_PASSES, 0)
    carry = lax.fori_loop(0, n_first, functools.partial(probe, extract=False), carry)

    def probes_left(st):
        n_pass, n_open = st[0], st[-1]
        return (n_pass < MAX_SEARCH_PASSES) & (n_open > 0)

    def extracting_probe(st):
        carry = probe(st[0], st[1:-1], extract=True)
        return (st[0] + 1,) + carry + (jnp.sum(carry[-1]),)

    state = lax.while_loop(probes_left, extracting_probe,
                           (jnp.int32(FIRST_CHECK_PASSES),) + carry + (jnp.sum(carry[-1]),))
    lo, count_lo, count_hi = state[1], state[3], state[4]
    tau = to_f32(lo)
    need = jnp.where((count_lo == topk) | few, topk, topk - count_hi).astype(F32)

    m_scr[...] = jnp.full(m_scr.shape, MASKED_LOGIT, F32)
    l_scr[...] = jnp.zeros(l_scr.shape, F32)
    acc_scr[...] = jnp.zeros(acc_scr.shape, F32)
    ones_rows = jnp.ones((16, KEY_TILE), BF16)
    last = n_kt - 1

    def mask_pair(j_first, parity):
        xs = [score_scr[pl.ds(key_tile(jnp.minimum(j_first + slot, last)), KEY_TILE), :] for slot in range(2)]
        ties = [x == tau for x in xs]
        tie_cols = jnp.concatenate([jnp.where(tie, 1.0, 0.0).astype(BF16) for tie in ties], axis=1)
        ranks = _dot(tri_ref[...], tie_cols)
        tie_carry = tie_scr[0:1, :]
        for slot, (x, tie) in enumerate(zip(xs, ties)):
            rank = ranks[:, slot * LANES:(slot + 1) * LANES] + tie_carry
            tie_carry = rank[KEY_TILE - 1:KEY_TILE, :]
            take = (tie & (rank <= need)) | (x > tau)
            mask_scr[parity, slot] = jnp.where(take, 0.0, MASKED_LOGIT)
        tie_scr[0:1, :] = tie_carry

    def qk_pair(slot, j, p):
        s_scr[slot, :, p * pair_w:(p + 1) * pair_w] = _dot(
            k_ref[pl.ds(key_tile(j), KEY_TILE), p * LANES:(p + 1) * LANES], wq_scr[p])

    def softmax_pair(slot, j, p, near, parity):
        mask_add = mask_scr[parity, slot]
        alphas = []
        for h in (2 * p, 2 * p + 1):
            cols = slice(h * LANES, (h + 1) * LANES)
            logits = s_scr[slot, :, cols] + mask_add
            if near:
                kind_top = jnp.clip(2 * j - i + 2, 0, 2)
                kind_bot = jnp.clip(2 * j - i + 3, 0, 2)
                logits = logits + jnp.concatenate(
                    [bias_ref[kind_top, :, cols], bias_ref[kind_bot, :, cols]], axis=0)
            m_old = m_scr[h:h + 1, :]
            m_new = jnp.maximum(m_old, jnp.max(logits, axis=0, keepdims=True))
            m_scr[h:h + 1, :] = m_new
            p_scr[slot, :, cols] = jnp.exp2(logits - m_new).astype(BF16)
            alphas.append(jnp.exp2(m_old - m_new))
        return alphas

    def pv_pair(slot, j, p, alphas):
        lhs = jnp.concatenate([vt_ref[p * 2 * hd:(p + 1) * 2 * hd, pl.ds(key_tile(j), KEY_TILE)], ones_rows],
                              axis=0)
        out = _dot(lhs, p_scr[slot, :, p * pair_w:(p + 1) * pair_w])
        for half in range(2):
            h = 2 * p + half
            rows_h = slice(h * hd, (h + 1) * hd)
            q_cols = slice(half * LANES, (half + 1) * LANES)
            acc_scr[rows_h, :] = acc_scr[rows_h, :] * alphas[half] + out[half * hd:(half + 1) * hd, q_cols]
            l_scr[h:h + 1, :] = l_scr[h:h + 1, :] * alphas[half] + out[2 * hd:2 * hd + 1, q_cols]

    def pending_alphas(p):
        return [alpha_scr[h:h + 1, :] for h in (2 * p, 2 * p + 1)]

    def clear_pending():
        p_scr[1] = jnp.zeros(p_scr.shape[1:], BF16)
        alpha_scr[...] = jnp.ones(alpha_scr.shape, F32)

    def pair_step(ja, j_pending, j_next, near, parity):
        alphas_a = []
        for p in range(n_pairs):
            pv_pair(1, j_pending, p, pending_alphas(p))
            qk_pair(1, ja + 1, p)
            alphas_a.append(softmax_pair(0, ja, p, near, parity))
        for p in range(n_pairs):
            pv_pair(0, ja, p, alphas_a[p])
            qk_pair(0, j_next, p)
            alphas_b = softmax_pair(1, ja + 1, p, near, parity)
            for half in range(2):
                alpha_scr[2 * p + half:2 * p + half + 1, :] = alphas_b[half]
        mask_pair(ja + 2, 1 - parity)

    def single_step(ja, j_pending, near, parity):
        alphas_a = []
        for p in range(n_pairs):
            pv_pair(1, j_pending, p, pending_alphas(p))
            alphas_a.append(softmax_pair(0, ja, p, near, parity))
        for p in range(n_pairs):
            pv_pair(0, ja, p, alphas_a[p])
        clear_pending()

    n_far = 2 * (jnp.maximum(n_kt - 2, 0) // 2)
    n_near = n_kt - n_far
    first_near_parity = (n_far // 2) % 2
    clear_pending()
    tie_scr[...] = jnp.zeros(tie_scr.shape, F32)
    mask_pair(0, 0)
    for p in range(n_pairs):
        qk_pair(0, 0, p)

    def far_body(jp, carry):
        pair_step(2 * jp, jnp.maximum(2 * jp - 1, 0), 2 * jp + 2, near=False, parity=jp % 2)
        return carry

    lax.fori_loop(0, n_far // 2, far_body, 0)

    @pl.when(n_near >= 2)
    def _():
        pair_step(n_far, jnp.maximum(n_far - 1, 0), jnp.minimum(n_far + 2, last), near=True,
                  parity=first_near_parity)

    @pl.when(n_near % 2 == 1)
    def _():
        single_step(last, jnp.where(n_near == 3, n_far + 1, jnp.maximum(n_far - 1, 0)), near=True,
                    parity=jnp.where(n_near == 3, 1 - first_near_parity, first_near_parity))

    for p in range(n_pairs):
        pv_pair(1, last, p, pending_alphas(p))

    for h in range(N_ATTN_HEADS):
        rows_h = slice(h * hd, (h + 1) * hd)
        acc_scr[rows_h, :] = acc_scr[rows_h, :] / l_scr[h:h + 1, :]
    o_ref[...] = acc_scr[...].T


def _dsa_kernel(*refs, topk, t_valid):
    o_ref = refs[8]
    i = pl.program_id(1)
    is_real = i * LANES < t_valid

    @pl.when(is_real)
    def _():
        _dsa_block(i, *refs, topk=topk)

    @pl.when(jnp.logical_not(is_real))
    def _():
        o_ref[...] = jnp.zeros(o_ref.shape, F32)


def _dsa_attention(qit, rows, qt, kidx, k, vt, bias, tri, batch, tp, t_valid, topk):
    n_rows = k.shape[0]
    nqb = tp // LANES
    d_idx = N_IDX_HEADS * IDX_DIM
    n_pairs = N_ATTN_HEADS // 2
    key_rows = -(-tp // SCAN_TILE) * SCAN_TILE
    q_cols = lambda height: pl.BlockSpec((None, height, LANES), lambda b, i: (b, 0, i))
    return pl.pallas_call(
        functools.partial(_dsa_kernel, topk=topk, t_valid=t_valid),
        out_shape=jax.ShapeDtypeStruct((n_rows, D_ATTN), F32),
        grid=(batch, nqb),
        in_specs=[q_cols(d_idx),
                  pl.BlockSpec((16, LANES), lambda b, i: (0, b * nqb + i)),
                  q_cols(D_ATTN),
                  pl.BlockSpec((tp, LANES), lambda b, i: (b, 0)),
                  pl.BlockSpec((tp, D_ATTN), lambda b, i: (b, 0)),
                  pl.BlockSpec((None, D_ATTN, tp), lambda b, i: (b, 0, 0)),
                  pl.BlockSpec((3, LANES, N_ATTN_HEADS * LANES), lambda b, i: (0, 0, 0)),
                  pl.BlockSpec((KEY_TILE, KEY_TILE), lambda b, i: (0, 0))],
        out_specs=pl.BlockSpec((LANES, D_ATTN), lambda b, i: (b * nqb + i, 0)),
        scratch_shapes=[pltpu.VMEM((key_rows, LANES), F32),
                        pltpu.VMEM((LANES, N_IDX_HEADS * LANES), BF16),
                        pltpu.VMEM((n_pairs, LANES, 2 * LANES), BF16),
                        pltpu.VMEM((2, KEY_TILE, N_ATTN_HEADS * LANES), F32),
                        pltpu.VMEM((2, KEY_TILE, N_ATTN_HEADS * LANES), BF16),
                        pltpu.VMEM((N_ATTN_HEADS, LANES), F32),
                        pltpu.VMEM((N_ATTN_HEADS, LANES), F32),
                        pltpu.VMEM((D_ATTN, LANES), F32),
                        pltpu.VMEM((N_ATTN_HEADS, LANES), F32),
                        pltpu.VMEM((2, 2, KEY_TILE, LANES), F32),
                        pltpu.VMEM((8, LANES), F32)],
        compiler_params=_params("parallel", "parallel"),
        name="dsa_attention",
    )(qit, rows, qt, kidx, k, vt, bias, tri)


def _gdn_prep_kernel(x_ref, halo_ref, cw_ref, rows_ref, alog_ref, dtb_ref, q_ref, k_ref, v_ref, bg_ref, buf):
    first = pl.program_id(1) == 0
    rb = x_ref.shape[0]
    buf[0:HALO_ROWS, :] = jnp.where(first, 0.0, halo_ref[...])
    buf[HALO_ROWS:, :] = x_ref[...]
    acc = jnp.zeros((rb, 3 * D_GDN), F32)
    for tap in range(CONV_WIDTH):
        start = HALO_ROWS - (CONV_WIDTH - 1) + tap
        acc = acc + cw_ref[tap:tap + 1, :] * buf[start:start + rb, :]
    y = _silu(acc)
    for h in range(N_GDN_HEADS):
        cols = slice(h * GDN_HEAD_DIM, (h + 1) * GDN_HEAD_DIM)
        qh = y[:, cols]
        kh = y[:, D_GDN + h * GDN_HEAD_DIM:D_GDN + (h + 1) * GDN_HEAD_DIM]
        q_ref[:, cols] = (qh * lax.rsqrt(jnp.sum(qh * qh, axis=-1, keepdims=True) + EPS)
                          * (GDN_HEAD_DIM ** -0.5))
        k_ref[:, cols] = kh * lax.rsqrt(jnp.sum(kh * kh, axis=-1, keepdims=True) + EPS)
    v_ref[...] = y[:, 2 * D_GDN:]
    rows = rows_ref[...]
    beta = _sigmoid(rows[4:8, :])
    a = rows[8:12, :] + dtb_ref[...]
    softplus = jnp.maximum(a, 0.0) + jnp.log1p(jnp.exp(-jnp.abs(a)))
    bg_ref[0:4, :] = beta
    bg_ref[4:8, :] = -jnp.exp(alog_ref[...]) * softplus


def _gdn_prep(proj, rows, lw, batch, tp, rb):
    n_rows = proj.shape[0]
    nb = tp // rb
    halo_per_block = rb // HALO_ROWS
    row_spec = pl.BlockSpec((rb, D_GDN), lambda b, i: (b * nb + i, 0))
    return pl.pallas_call(
        _gdn_prep_kernel,
        out_shape=(jax.ShapeDtypeStruct((n_rows, D_GDN), F32),) * 3
        + (jax.ShapeDtypeStruct((8, n_rows), F32),),
        grid=(batch, nb),
        in_specs=[pl.BlockSpec((rb, 3 * D_GDN), lambda b, i: (b * nb + i, 0)),
                  pl.BlockSpec((HALO_ROWS, 3 * D_GDN),
                               lambda b, i: (jnp.maximum((b * nb + i) * halo_per_block - 1, 0), 0)),
                  pl.BlockSpec((CONV_WIDTH, 3 * D_GDN), lambda b, i: (0, 0)),
                  pl.BlockSpec((16, rb), lambda b, i: (0, b * nb + i)),
                  pl.BlockSpec((N_GDN_HEADS, 1), lambda b, i: (0, 0)),
                  pl.BlockSpec((N_GDN_HEADS, 1), lambda b, i: (0, 0))],
        out_specs=(row_spec, row_spec, row_spec,
                   pl.BlockSpec((8, rb), lambda b, i: (0, b * nb + i))),
        scratch_shapes=[pltpu.VMEM((HALO_ROWS + rb, 3 * D_GDN), F32)],
        compiler_params=_params("parallel", "parallel"),
        name="gdn_prep",
    )(proj, proj, lw["conv_w"], rows, lw["a_log"], lw["dt_bias"])


def _gdn_chunk_kernel(q_ref, k_ref, v_ref, bg_ref, m_ref, n_ref, p_ref, r_ref, cd_ref):
    c = LANES
    n_chunks = q_ref.shape[0] // c
    items = [(ch, h) for ch in range(n_chunks) for h in range(N_GDN_HEADS)]
    idx = range(len(items))
    row = lax.broadcasted_iota(I32, (c, c), 0)
    col = lax.broadcasted_iota(I32, (c, c), 1)
    tri = row >= col
    strict = row > col
    eye = jnp.where(row == col, 1.0, 0.0)
    lane8 = lax.broadcasted_iota(I32, (8, c), 1)
    gates, decays = [], []
    for ch in range(n_chunks):
        bg = bg_ref[:, ch * c:(ch + 1) * c]
        dec = bg
        shift = 1
        while shift < c:
            dec = dec + jnp.where(lane8 >= shift, pltpu.roll(dec, shift, 1), 0.0)
            shift *= 2
        gates.append(bg)
        decays.append(dec)

    def tokens(ref, n):
        ch, h = items[n]
        return ref[ch * c:(ch + 1) * c, h * GDN_HEAD_DIM:(h + 1) * GDN_HEAD_DIM]

    d_row = [jnp.broadcast_to(decays[ch][4 + h:5 + h, :], (c, c)) for ch, h in items]
    d_col = [d.T for d in d_row]
    beta_col = [jnp.broadcast_to(gates[ch][h:h + 1, :], (c, c)).T for ch, h in items]
    d_last = [d[:, c - 1:c] for d in d_row]
    gamma = [jnp.exp(jnp.where(tri, d_col[n] - d_row[n], MASKED_LOGIT)) for n in idx]
    exp_d = [jnp.exp(d_col[n]) for n in idx]
    k16 = [tokens(k_ref, n).astype(BF16) for n in idx]
    kb = [tokens(k_ref, n) * beta_col[n] for n in idx]
    nil = [jnp.where(strict, _dot_nt(kb[n].astype(BF16), k16[n]) * gamma[n], 0.0) for n in idx]
    block = 8
    same = lambda size: (row // size) == (col // size)
    diag = [jnp.where(same(block), x, 0.0) for x in nil]
    diag_parts = [_split_bf16(x) for x in diag]
    inv = [eye - x for x in diag]
    power = [_dot_split(x, x) for x in diag_parts]
    for it in range(2):
        power_parts = [_split_bf16(x) for x in power]
        inv = [inv[n] + _dot_split(_split_bf16(inv[n]), power_parts[n]) for n in idx]
        if it == 0:
            power = [_dot_split(x, x) for x in power_parts]
    while block < c:
        couples = same(2 * block) & ((row // block) % 2 == 1) & ((col // block) % 2 == 0)
        inv_parts = [_split_bf16(x) for x in inv]
        lower = [_dot_split(_split_bf16(jnp.where(couples, nil[n], 0.0)), inv_parts[n]) for n in idx]
        inv = [inv[n] - _dot_split(inv_parts[n], _split_bf16(lower[n])) for n in idx]
        block *= 2
    rhs = [jnp.concatenate([kb[n] * exp_d[n], tokens(v_ref, n) * beta_col[n]], axis=1) for n in idx]
    wu = [_dot_split(_split_bf16(inv[n]), _split_bf16(rhs[n])).astype(BF16) for n in idx]
    aqk = [jnp.where(tri, _dot_nt(tokens(q_ref, n).astype(BF16), k16[n]) * gamma[n], 0.0).astype(BF16)
           for n in idx]
    kd_t = [(tokens(k_ref, n) * jnp.exp(d_last[n] - d_col[n])).T.astype(BF16) for n in idx]
    state_wu = [_dot(kd_t[n], wu[n]) for n in idx]
    out_wu = [_dot(aqk[n], wu[n]) for n in idx]
    for n, (ch, h) in enumerate(items):
        m_ref[ch, h] = (-state_wu[n][:, :c]).astype(BF16)
        n_ref[ch, h] = state_wu[n][:, c:]
        p_ref[ch, h] = (tokens(q_ref, n) * exp_d[n] - out_wu[n][:, :c]).astype(BF16)
        r_ref[ch, h] = out_wu[n][:, c:]
        cd_ref[ch, h:h + 1, :] = jnp.exp(d_last[n][0:1, :] + jnp.zeros((1, c), F32))


def _gdn_chunks(qn, kn, vv, bg):
    n_rows = qn.shape[0]
    nc = n_rows // LANES
    per_step = GDN_CHUNKS_PER_STEP
    tok = pl.BlockSpec((per_step * LANES, D_GDN), lambda c: (c, 0))
    mat = pl.BlockSpec((per_step, N_GDN_HEADS, LANES, LANES), lambda c: (c, 0, 0, 0))
    mat_shape = lambda dt: jax.ShapeDtypeStruct((nc, N_GDN_HEADS, LANES, LANES), dt)
    return pl.pallas_call(
        _gdn_chunk_kernel,
        out_shape=(mat_shape(BF16), mat_shape(F32), mat_shape(BF16), mat_shape(F32),
                   jax.ShapeDtypeStruct((nc, N_GDN_HEADS, LANES), F32)),
        grid=(nc // per_step,),
        in_specs=[tok, tok, tok, pl.BlockSpec((8, per_step * LANES), lambda c: (0, c))],
        out_specs=(mat, mat, mat, mat, pl.BlockSpec((per_step, N_GDN_HEADS, LANES), lambda c: (c, 0, 0))),
        compiler_params=_params("parallel"),
        name="gdn_chunks",
    )(qn, kn, vv, bg)


def _gdn_scan_kernel(m_ref, n_ref, p_ref, r_ref, cd_ref, o_ref, s_scr, *, batch):
    @pl.when(pl.program_id(0) == 0)
    def _():
        s_scr[...] = jnp.zeros(s_scr.shape, F32)

    for ch in range(m_ref.shape[1]):
        rows = slice(ch * LANES, (ch + 1) * LANES)
        for b in range(batch):
            for h in range(N_GDN_HEADS):
                s = s_scr[b, h]
                s16 = s.astype(BF16)
                o_ref[b, rows, h * GDN_HEAD_DIM:(h + 1) * GDN_HEAD_DIM] = _dot(p_ref[b, ch, h], s16) + r_ref[b, ch, h]
                s_scr[b, h] = s * cd_ref[b, ch, h:h + 1, :] + _dot(m_ref[b, ch, h], s16) + n_ref[b, ch, h]


def _gdn_scan(m_mat, n_mat, p_mat, r_mat, cd, batch, tp):
    nc = tp // LANES
    per_step = next(d for d in SCAN_CHUNKS if nc % d == 0)
    shape5 = lambda a: a.reshape(batch, nc, N_GDN_HEADS, LANES, LANES)
    mat = pl.BlockSpec((batch, per_step, N_GDN_HEADS, LANES, LANES), lambda c: (0, c, 0, 0, 0))
    return pl.pallas_call(
        functools.partial(_gdn_scan_kernel, batch=batch),
        out_shape=jax.ShapeDtypeStruct((batch, tp, D_GDN), F32),
        grid=(nc // per_step,),
        in_specs=[mat, mat, mat, mat,
                  pl.BlockSpec((batch, per_step, N_GDN_HEADS, LANES), lambda c: (0, c, 0, 0))],
        out_specs=pl.BlockSpec((batch, per_step * LANES, D_GDN), lambda c: (0, c, 0)),
        scratch_shapes=[pltpu.VMEM((batch, N_GDN_HEADS, LANES, LANES), F32)],
        compiler_params=_params("arbitrary"),
        name="gdn_scan",
    )(shape5(m_mat), shape5(n_mat), shape5(p_mat), shape5(r_mat), cd.reshape(batch, nc, N_GDN_HEADS, LANES))


def _out_proj_kernel(h_ref, oa_ref, za_ref, og_ref, zg_ref, gain_ref, w_ref, out_ref):
    attn = (oa_ref[...] * _silu(za_ref[...])).astype(BF16)
    y = _dot(attn, w_ref[0:D_ATTN, :])
    og, zg = og_ref[...], zg_ref[...]
    for h in range(N_GDN_HEADS):
        cols = slice(h * GDN_HEAD_DIM, (h + 1) * GDN_HEAD_DIM)
        gated = (_rms_rows(og[:, cols], gain_ref[...]) * _silu(zg[:, cols])).astype(BF16)
        y = y + _dot(gated, w_ref[D_ATTN + h * GDN_HEAD_DIM:D_ATTN + (h + 1) * GDN_HEAD_DIM, :])
    out_ref[...] = h_ref[...] + y


def _out_proj(h2, o_attn, proj, o_gdn, gain, w_out, rb):
    n_rows = h2.shape[0]
    blk = lambda width, col: pl.BlockSpec((rb, width), lambda i: (i, col))
    return pl.pallas_call(
        _out_proj_kernel,
        out_shape=jax.ShapeDtypeStruct((n_rows, D_MODEL), F32),
        grid=(n_rows // rb,),
        in_specs=[blk(D_MODEL, 0), blk(D_ATTN, 0), blk(D_ATTN, COL_Z_ATTN), blk(D_GDN, 0),
                  blk(D_GDN, COL_Z_GDN),
                  pl.BlockSpec((1, GDN_HEAD_DIM), lambda i: (0, 0)),
                  pl.BlockSpec((D_MODEL, D_MODEL), lambda i: (0, 0))],
        out_specs=blk(D_MODEL, 0),
        compiler_params=_params("parallel"),
        name="out_proj",
    )(h2, o_attn, proj, o_gdn, proj, gain, w_out)


def _pack_layer(norm_gain, w_in, cq_gain, ckv_gain, w_uq, w_ukv, w_q_idx, q_gain, k_gain, conv_w, a_log,
                dt_bias, gdn_gain, w_out):
    o = 0
    parts = {}
    for name, size in (("c_q", Q_RANK), ("c_kv", KV_RANK), ("k_idx", IDX_DIM), ("w_idx", N_IDX_HEADS),
                       ("z_attn", D_ATTN), ("qkv_g", 3 * D_GDN), ("z_g", D_GDN), ("b", N_GDN_HEADS),
                       ("a", N_GDN_HEADS)):
        parts[name] = w_in[:, o:o + size]
        o += size
    w_packed = jnp.concatenate([parts["qkv_g"], parts["c_q"], parts["c_kv"], parts["k_idx"], parts["k_idx"],
                                parts["z_attn"], parts["z_g"]], axis=1).astype(BF16)
    w_rows = jnp.concatenate([parts["w_idx"], parts["b"], parts["a"],
                              jnp.zeros((D_MODEL, 4), w_in.dtype)], axis=1).T.astype(BF16)
    return dict(
        gain=norm_gain[None, :], w_packed=w_packed, w_rows=w_rows,
        g_cq=cq_gain[None, :], g_ckv=ckv_gain[None, :],
        w_uqt=w_uq.T.astype(BF16), w_qit=w_q_idx.T.astype(BF16),
        w_ukt=w_ukv[:, :D_ATTN].T.astype(BF16), w_uvt=w_ukv[:, D_ATTN:].T.astype(BF16),
        g_q_col=q_gain[:, None], g_k_col=k_gain[:, None],
        conv_w=conv_w, a_log=a_log[:, None], dt_bias=dt_bias[:, None],
        gdn_gain=gdn_gain[None, :], w_out=w_out.astype(BF16))


def _layer(h2, lw, bias, tri, batch, tp, t_valid, topk):
    rb = _row_block(tp)
    proj, rows = _in_proj(h2, lw["gain"], lw["w_packed"], lw["w_rows"], rb)
    qt, k, vt, qit, kidx = _dsa_prep(proj, lw, batch, tp, rb)
    o_attn = _dsa_attention(qit, rows, qt, kidx, k, vt, bias, tri, batch, tp, t_valid, topk)
    qn, kn, vv, bg = _gdn_prep(proj, rows, lw, batch, tp, rb)
    m_mat, n_mat, p_mat, r_mat, cd = _gdn_chunks(qn, kn, vv, bg)
    o_gdn = _gdn_scan(m_mat, n_mat, p_mat, r_mat, cd, batch, tp).reshape(batch * tp, D_GDN)
    return _out_proj(h2, o_attn, proj, o_gdn, lw["gdn_gain"], lw["w_out"], rb)


def _forward(x, meta_tokens, rel_bias_table, layer_weights, topk):
    batch, seq, _ = x.shape
    t = seq + N_META
    tp = -(-t // KEY_TILE) * KEY_TILE
    meta = jnp.broadcast_to(meta_tokens[None].astype(x.dtype), (batch, N_META, D_MODEL))
    h = jnp.concatenate([meta, x, jnp.zeros((batch, tp - t, D_MODEL), x.dtype)], axis=1)
    h2 = h.reshape(batch * tp, D_MODEL)
    bias = _bias_tiles(rel_bias_table)
    tri = jnp.tril(jnp.ones((KEY_TILE, KEY_TILE), BF16))
    for lw in layer_weights:
        h2 = _layer(h2, lw, bias, tri, batch, tp, t, topk)
    return h2.reshape(batch, tp, D_MODEL)[:, N_META:t]


def kernel(x, meta_tokens, rel_bias_table, norm_gain, w_in, cq_norm_gain, ckv_norm_gain, w_uq, w_ukv, w_q_idx,
           q_norm_gain, k_norm_gain, conv_w, a_log, dt_bias, gdn_norm_gain, w_out):
    depth = norm_gain.shape[0]
    topk = min(TOPK_MAX, x.shape[1] // 4)
    layers = [_pack_layer(norm_gain[l], w_in[l], cq_norm_gain[l], ckv_norm_gain[l], w_uq[l], w_ukv[l],
                          w_q_idx[l], q_norm_gain[l], k_norm_gain[l], conv_w[l], a_log[l], dt_bias[l],
                          gdn_norm_gain[l], w_out[l]) for l in range(depth)]
    return _forward(x, meta_tokens, rel_bias_table, layers, topk)
```

```python
import functools
import math

import jax
import jax.numpy as jnp
from jax import lax
from jax.experimental import pallas as pl
from jax.experimental.pallas import tpu as pltpu

F32 = jnp.float32
BF16 = jnp.bfloat16
I32 = jnp.int32

D_MODEL = 1024
N_META = 16
EPS = 1e-6
N_ATTN_HEADS = 8
ATTN_HEAD_DIM = 64
D_ATTN = N_ATTN_HEADS * ATTN_HEAD_DIM
Q_RANK = 256
KV_RANK = 128
N_IDX_HEADS = 4
IDX_DIM = 64
TOPK_MAX = 256
N_REL_BUCKETS = 32
REL_MAX_DIST = 128
N_GDN_HEADS = 4
GDN_HEAD_DIM = 128
D_GDN = N_GDN_HEADS * GDN_HEAD_DIM
CONV_WIDTH = 4

LANES = 128
Q_BLOCK = 256
KEY_TILE = 256
COUNT_TILE = 512
SCAN_TILE = 1024
FIRST_CHECK_PASSES = 10
INTERPOLATED_PASSES = 24
MAX_SEARCH_PASSES = INTERPOLATED_PASSES + 32
GDN_CHUNKS_PER_STEP = 2
SCAN_CHUNKS = (6, 4, 3, 2, 1)
ROW_TILES = (6, 5, 4, 3, 2, 1)
HALO_ROWS = 8
D_PACKED = 3 * D_GDN + 512 + D_ATTN + D_GDN
COL_SMALL = 3
COL_Z_ATTN = 4
COL_Z_GDN = 5
MASKED_LOGIT = -1e30
LOG2_E = math.log2(math.e)
KEY_MIN = -2 ** 31
PATTERN_NEG_FLT_MAX = KEY_MIN + (1 << 23)
VMEM_LIMIT = 56 * 1024 * 1024

NT_DIMS = (((1,), (1,)), ((), ()))


def _dot(a, b):
    return jnp.dot(a, b, preferred_element_type=F32)


def _dot_nt(a, b):
    return lax.dot_general(a, b, NT_DIMS, preferred_element_type=F32)


def _split_bf16(x):
    hi = x.astype(BF16)
    return hi, (x - hi.astype(F32)).astype(BF16)


def _dot_split(a_parts, b_parts):
    (a_hi, a_lo), (b_hi, b_lo) = a_parts, b_parts
    return _dot(a_hi, b_hi) + (_dot(a_hi, b_lo) + _dot(a_lo, b_hi))


def _sigmoid(x):
    return 1.0 / (1.0 + jnp.exp(-x))


def _silu(x):
    return x * _sigmoid(x)


def _row_block(tp):
    tiles = tp // LANES
    return LANES * next(d for d in ROW_TILES if tiles % d == 0)


def _params(*sem):
    return pltpu.CompilerParams(dimension_semantics=sem, vmem_limit_bytes=VMEM_LIMIT)


def _bias_kernel(table_ref, out_ref):
    row = lax.broadcasted_iota(I32, (LANES, LANES), 0)
    col = lax.broadcasted_iota(I32, (LANES, LANES), 1)
    max_exact = N_REL_BUCKETS // 2
    for kind in range(3):
        dist = col - row + (2 - kind) * LANES
        n = jnp.maximum(dist, 0)
        nf = jnp.maximum(n, 1).astype(F32)
        large = max_exact + (jnp.log(nf / max_exact) / math.log(REL_MAX_DIST / max_exact)
                             * (N_REL_BUCKETS - max_exact)).astype(I32)
        large = jnp.minimum(large, N_REL_BUCKETS - 1)
        bucket = jnp.where(n < max_exact, n, large)
        for h in range(N_ATTN_HEADS):
            tile = jnp.zeros((LANES, LANES), F32)
            for b in range(N_REL_BUCKETS):
                tile = jnp.where(bucket == b, table_ref[b, h], tile)
            far = table_ref[N_REL_BUCKETS - 1, h]
            out_ref[kind, :, h * LANES:(h + 1) * LANES] = (tile - far) * LOG2_E


def _bias_tiles(rel_table):
    return pl.pallas_call(
        _bias_kernel,
        out_shape=jax.ShapeDtypeStruct((3, LANES, N_ATTN_HEADS * LANES), F32),
        in_specs=[pl.BlockSpec(memory_space=pltpu.SMEM)],
        out_specs=pl.BlockSpec(memory_space=pltpu.VMEM),
        name="rel_bias_tiles",
    )(rel_table)


def _in_proj_kernel(h_ref, gain_ref, w_ref, wrows_ref, proj_ref, rows_ref):
    x = h_ref[...]
    y = x * lax.rsqrt(jnp.mean(x * x, axis=-1, keepdims=True) + EPS)
    hn = (y * gain_ref[...]).astype(BF16)
    proj_ref[...] = _dot(hn, w_ref[...])
    rows_ref[...] = _dot_nt(wrows_ref[...], hn)


def _in_proj(h2, gain, w_packed, w_rows, rb):
    n_rows = h2.shape[0]
    grid = (n_rows // rb,)
    return pl.pallas_call(
        _in_proj_kernel,
        out_shape=(jax.ShapeDtypeStruct((n_rows, D_PACKED), F32),
                   jax.ShapeDtypeStruct((16, n_rows), F32)),
        grid=grid,
        in_specs=[pl.BlockSpec((rb, D_MODEL), lambda i: (i, 0)),
                  pl.BlockSpec((1, D_MODEL), lambda i: (0, 0)),
                  pl.BlockSpec((D_MODEL, D_PACKED), lambda i: (0, 0)),
                  pl.BlockSpec((16, D_MODEL), lambda i: (0, 0))],
        out_specs=(pl.BlockSpec((rb, D_PACKED), lambda i: (i, 0)),
                   pl.BlockSpec((16, rb), lambda i: (0, i))),
        compiler_params=_params("parallel"),
        name="in_proj",
    )(h2, gain, w_packed, w_rows)


def _rms_rows(x, gain):
    return x * lax.rsqrt(jnp.mean(x * x, axis=-1, keepdims=True) + EPS) * gain


def _dsa_prep_kernel(sm_ref, gcq_ref, gckv_ref, wuqt_ref, wqit_ref, wukt_ref, wuvt_ref, gq_ref, gk_ref,
                     qt_ref, k_ref, vt_ref, qit_ref, kidx_ref):
    sm = sm_ref[...]
    rb = sm.shape[0]
    cq = _rms_rows(sm[:, :Q_RANK], gcq_ref[...]).astype(BF16)
    ckv = _rms_rows(sm[:, Q_RANK:Q_RANK + KV_RANK], gckv_ref[...]).astype(BF16)
    kidx_ref[...] = sm[:, Q_RANK + KV_RANK:].astype(BF16)
    q3 = _dot_nt(wuqt_ref[...], cq).reshape(N_ATTN_HEADS, ATTN_HEAD_DIM, rb)
    q3 = q3 * lax.rsqrt(jnp.mean(q3 * q3, axis=1, keepdims=True) + EPS) * gq_ref[...][None]
    qt_ref[...] = (q3 * (ATTN_HEAD_DIM ** -0.5 * LOG2_E)).reshape(D_ATTN, rb).astype(BF16)
    k3 = _dot_nt(wukt_ref[...], ckv).reshape(N_ATTN_HEADS, ATTN_HEAD_DIM, rb)
    k3 = k3 * lax.rsqrt(jnp.mean(k3 * k3, axis=1, keepdims=True) + EPS) * gk_ref[...][None]
    k_ref[...] = k3.reshape(D_ATTN, rb).T.astype(BF16)
    vt_ref[...] = _dot_nt(wuvt_ref[...], ckv).astype(BF16)
    qit_ref[...] = _dot_nt(wqit_ref[...], cq).astype(BF16)


def _dsa_prep(proj, lw, batch, tp, rb):
    n_rows = proj.shape[0]
    nb = tp // rb
    d_idx = N_IDX_HEADS * IDX_DIM
    const = lambda shape: pl.BlockSpec(shape, lambda b, i: (0, 0))
    row_spec = lambda width: pl.BlockSpec((rb, width), lambda b, i: (b * nb + i, 0))
    col_spec = lambda height: pl.BlockSpec((None, height, rb), lambda b, i: (b, 0, i))
    return pl.pallas_call(
        _dsa_prep_kernel,
        out_shape=(jax.ShapeDtypeStruct((batch, D_ATTN, tp), BF16),
                   jax.ShapeDtypeStruct((n_rows, D_ATTN), BF16),
                   jax.ShapeDtypeStruct((batch, D_ATTN, tp), BF16),
                   jax.ShapeDtypeStruct((batch, d_idx, tp), BF16),
                   jax.ShapeDtypeStruct((n_rows, LANES), BF16)),
        grid=(batch, nb),
        in_specs=[pl.BlockSpec((rb, 512), lambda b, i: (b * nb + i, COL_SMALL)),
                  const((1, Q_RANK)), const((1, KV_RANK)),
                  const((D_ATTN, Q_RANK)), const((d_idx, Q_RANK)),
                  const((D_ATTN, KV_RANK)), const((D_ATTN, KV_RANK)),
                  const((ATTN_HEAD_DIM, 1)), const((ATTN_HEAD_DIM, 1))],
        out_specs=(col_spec(D_ATTN), row_spec(D_ATTN), col_spec(D_ATTN), col_spec(d_idx), row_spec(LANES)),
        compiler_params=_params("parallel", "parallel"),
        name="dsa_prep",
    )(proj, lw["g_cq"], lw["g_ckv"], lw["w_uqt"], lw["w_qit"], lw["w_ukt"], lw["w_uvt"], lw["g_q_col"],
      lw["g_k_col"])


def _dsa_block(i, qit_ref, rows_ref, qt_ref, kidx_ref, k_ref, vt_ref, bias_ref, tri_ref, o_ref,
               score_scr, wi_scr, wq_scr, s_scr, p_scr, m_scr, l_scr, acc_scr, alpha_scr, mask_scr, tie_scr, *, topk):
    ql = Q_BLOCK
    t0 = i * ql
    n_kt = i * (ql // KEY_TILE) + 1
    n_st = (n_kt + 3) // 4
    hd = ATTN_HEAD_DIM
    pair_w = 2 * ql
    n_pairs = N_ATTN_HEADS // 2

    zeros_hd = jnp.zeros((hd, ql), BF16)
    for h in range(N_IDX_HEADS):
        wi_scr[0:IDX_DIM, h * ql:(h + 1) * ql] = qit_ref[h * IDX_DIM:(h + 1) * IDX_DIM, :]
    wi_scr[IDX_DIM:, :] = jnp.zeros((LANES - IDX_DIM, N_IDX_HEADS * ql), BF16)
    for p in range(n_pairs):
        wq_scr[p, 0:hd, 0:ql] = qt_ref[2 * p * hd:(2 * p + 1) * hd, :]
        wq_scr[p, 0:hd, ql:] = zeros_hd
        wq_scr[p, hd:, 0:ql] = zeros_hd
        wq_scr[p, hd:, ql:] = qt_ref[(2 * p + 1) * hd:(2 * p + 2) * hd, :]

    row = lax.broadcasted_iota(I32, (KEY_TILE, ql), 0)
    col = lax.broadcasted_iota(I32, (KEY_TILE, ql), 1)
    w_idx = rows_ref[0:N_IDX_HEADS, :] * (N_IDX_HEADS ** -0.5 * IDX_DIM ** -0.5)

    def key_tile(j):
        return pl.multiple_of(j * KEY_TILE, KEY_TILE)

    def causal(j):
        return (j * KEY_TILE + row) <= (t0 + col)

    def fold8(x, op):
        return op(x.reshape(KEY_TILE // 8, 8, ql), axis=0)

    def score_step(jc, carry, masked):
        top, bottom, n_nonneg, n_pos = carry
        subs = [jc * (SCAN_TILE // KEY_TILE) + sub for sub in range(SCAN_TILE // KEY_TILE)]
        logits = [_dot(kidx_ref[pl.ds(key_tile(jnp.minimum(j, n_kt - 1)), KEY_TILE), :], wi_scr[...])
                  for j in subs]
        for j, lg in zip(subs, logits):
            score = jnp.zeros((KEY_TILE, ql), F32)
            for h in range(N_IDX_HEADS):
                score = score + jnp.maximum(lg[:, h * ql:(h + 1) * ql], 0.0) * w_idx[h:h + 1, :]
            if masked:
                visible = causal(j)
                seen = jnp.where(visible, score, -jnp.inf)
                bottom = jnp.minimum(bottom, fold8(jnp.where(visible, score, jnp.inf), jnp.min))
            else:
                seen = score
                bottom = jnp.minimum(bottom, fold8(score, jnp.min))
            top = jnp.maximum(top, fold8(seen, jnp.max))
            n_nonneg = n_nonneg + fold8(jnp.where(seen >= 0.0, 1, 0), jnp.sum)
            n_pos = n_pos + fold8(jnp.where(seen > 0.0, 1, 0), jnp.sum)
            score_scr[pl.ds(key_tile(j), KEY_TILE), :] = seen
        return top, bottom, n_nonneg, n_pos

    zeros8 = jnp.zeros((8, ql), I32)
    carry = (jnp.full((8, ql), -jnp.inf, F32), jnp.full((8, ql), jnp.inf, F32), zeros8, zeros8)
    carry = lax.fori_loop(0, n_st - 1, functools.partial(score_step, masked=False), carry)
    top, bottom, n_nonneg, n_pos = score_step(n_st - 1, carry, masked=True)
    top = jnp.max(top, axis=0, keepdims=True)
    bottom = jnp.min(bottom, axis=0, keepdims=True)
    count0 = jnp.sum(n_nonneg, axis=0, keepdims=True)
    count_pos = jnp.sum(n_pos, axis=0, keepdims=True)

    def count_f32(cand, below=None):
        def body(j, carry):
            acc, best = carry
            for part in range(SCAN_TILE // COUNT_TILE):
                start = pl.multiple_of(j * SCAN_TILE + part * COUNT_TILE, COUNT_TILE)
                x = score_scr[pl.ds(start, COUNT_TILE), :]
                ind = jnp.where(x >= cand, 1, 0)
                acc = acc + jnp.sum(ind.reshape(COUNT_TILE // 8, 8, ql), axis=0)
                if below is not None:
                    under = jnp.where(x < below, x, -jnp.inf)
                    best = jnp.maximum(best, jnp.max(under.reshape(COUNT_TILE // 8, 8, ql), axis=0))
            return acc, best
        acc, best = lax.fori_loop(0, n_st, body, (jnp.zeros((8, ql), I32), jnp.full((8, ql), -jnp.inf, F32)))
        count = jnp.sum(acc, axis=0, keepdims=True)
        if below is None:
            return count
        return count, jnp.max(best, axis=0, keepdims=True)

    def to_pattern(v):
        bits = lax.bitcast_convert_type(v, I32)
        return bits ^ ((bits >> 31) & 0x7FFFFFFF)

    def to_f32(c):
        return lax.bitcast_convert_type(c ^ ((c >> 31) & 0x7FFFFFFF), F32)

    n_visible = t0 + 1 + lax.broadcasted_iota(I32, (1, ql), 1)
    nonneg = count0 >= topk
    lo = jnp.where(nonneg, 0, to_pattern(bottom))
    hi = jnp.where(nonneg, to_pattern(top) + 1, 0)
    count_lo = jnp.where(nonneg, count0, n_visible)
    count_hi = jnp.where(nonneg, 0, count0)
    few = n_visible < topk
    lo = jnp.where(few, PATTERN_NEG_FLT_MAX, lo)
    zero_tied = nonneg & (count_pos < topk)
    hi = jnp.where(zero_tied, 1, hi)
    count_hi = jnp.where(zero_tied, count_pos, count_hi)
    open_q = jnp.where(few | zero_tied | (count_lo == topk), 0, 1)

    log_topk = math.log(topk)

    def count_error(count):
        return jnp.log(count.astype(F32) + 0.5) - log_topk

    def probe(n_pass, carry, extract):
        lo, hi, count_lo, count_hi, err_lo, err_hi, last_side, open_q = carry
        v_lo, v_hi = to_f32(lo), to_f32(hi)
        frac = err_lo / (err_lo - err_hi)
        frac = jnp.where(count_lo - count_hi <= 4, 0.5, frac)
        guess = to_pattern(v_lo + (v_hi - v_lo) * frac)
        middle = lo + lax.shift_right_logical(hi - lo, 1)
        cand = jnp.where(n_pass >= INTERPOLATED_PASSES, middle, guess)
        cand = jnp.minimum(jnp.maximum(cand, lo + 1), hi - 1)
        is_open = open_q == 1
        if extract:
            count, under_hi = count_f32(to_f32(cand), below=v_hi)
            next_below = to_pattern(under_hi)
            found = is_open & (count_hi == topk - 1)
            is_open = is_open & jnp.logical_not(found)
        else:
            count = count_f32(to_f32(cand))
        raise_lo = is_open & (count >= topk)
        lower_hi = is_open & (count < topk)
        err = count_error(count)
        err_hi = jnp.where(raise_lo & (last_side == 1), err_hi * 0.5, err_hi)
        err_lo = jnp.where(lower_hi & (last_side == -1), err_lo * 0.5, err_lo)
        err_lo = jnp.where(raise_lo, err, err_lo)
        err_hi = jnp.where(lower_hi, err, err_hi)
        lo = jnp.where(raise_lo, cand, lo)
        count_lo = jnp.where(raise_lo, count, count_lo)
        hi = jnp.where(lower_hi, cand, hi)
        count_hi = jnp.where(lower_hi, count, count_hi)
        last_side = jnp.where(raise_lo, 1, jnp.where(lower_hi, -1, last_side))
        if extract:
            lo = jnp.where(found, next_below, lo)
            hi = jnp.where(found | raise_lo, next_below + 1, hi)
        closed = (count_lo == topk) | (hi - lo == 1)
        return lo, hi, count_lo, count_hi, err_lo, err_hi, last_side, jnp.where(closed, 0, open_q)

    carry = (lo, hi, count_lo, count_hi, count_error(count_lo), count_error(count_hi),
             jnp.zeros((1, ql), I32), open_q)
    n_first = jnp.where(jnp.sum(open_q) > 0, FIRST_CHECK_PASSES, 0)
    carry = lax.fori_loop(0, n_first, functools.partial(probe, extract=False), carry)

    def probes_left(st):
        n_pass, n_open = st[0], st[-1]
        return (n_pass < MAX_SEARCH_PASSES) & (n_open > 0)

    def extracting_probe(st):
        carry = probe(st[0], st[1:-1], extract=True)
        return (st[0] + 1,) + carry + (jnp.sum(carry[-1]),)

    state = lax.while_loop(probes_left, extracting_probe,
                           (jnp.int32(FIRST_CHECK_PASSES),) + carry + (jnp.sum(carry[-1]),))
    lo, count_lo, count_hi = state[1], state[3], state[4]
    tau = to_f32(lo)
    need = jnp.where((count_lo == topk) | few, topk, topk - count_hi).astype(F32)

    m_scr[...] = jnp.full(m_scr.shape, MASKED_LOGIT, F32)
    l_scr[...] = jnp.zeros(l_scr.shape, F32)
    acc_scr[...] = jnp.zeros(acc_scr.shape, F32)
    ones_rows = jnp.ones((16, KEY_TILE), BF16)
    last = n_kt - 1

    def mask_pair(j_first, parity):
        xs = [score_scr[pl.ds(key_tile(jnp.minimum(j_first + slot, last)), KEY_TILE), :] for slot in range(2)]
        ties = [x == tau for x in xs]
        tie_cols = jnp.concatenate([jnp.where(tie, 1.0, 0.0).astype(BF16) for tie in ties], axis=1)
        ranks = _dot(tri_ref[...], tie_cols)
        tie_carry = tie_scr[0:1, :]
        for slot, (x, tie) in enumerate(zip(xs, ties)):
            rank = ranks[:, slot * ql:(slot + 1) * ql] + tie_carry
            tie_carry = rank[KEY_TILE - 1:KEY_TILE, :]
            take = (tie & (rank <= need)) | (x > tau)
            mask_scr[parity, slot] = jnp.where(take, 0.0, MASKED_LOGIT)
        tie_scr[0:1, :] = tie_carry

    def qk_pair(slot, j, p):
        s_scr[slot, :, p * pair_w:(p + 1) * pair_w] = _dot(
            k_ref[pl.ds(key_tile(j), KEY_TILE), p * 2 * hd:(p + 1) * 2 * hd], wq_scr[p])

    def softmax_pair(slot, j, p, near, parity):
        alphas = []
        for h in (2 * p, 2 * p + 1):
            alpha_parts = []
            for part in range(ql // LANES):
                q_lanes = slice(part * LANES, (part + 1) * LANES)
                cols = slice(h * ql + part * LANES, h * ql + (part + 1) * LANES)
                logits = s_scr[slot, :, cols] + mask_scr[parity, slot, :, q_lanes]
                if near:
                    key_block = (KEY_TILE // LANES) * j
                    query_block = (ql // LANES) * i + part
                    bias_cols = slice(h * LANES, (h + 1) * LANES)
                    logits = logits + jnp.concatenate(
                        [bias_ref[jnp.clip(key_block + kb - query_block + 2, 0, 2), :, bias_cols]
                         for kb in range(KEY_TILE // LANES)], axis=0)
                m_old = m_scr[h:h + 1, q_lanes]
                m_new = jnp.maximum(m_old, jnp.max(logits, axis=0, keepdims=True))
                m_scr[h:h + 1, q_lanes] = m_new
                p_scr[slot, :, cols] = jnp.exp2(logits - m_new).astype(BF16)
                alpha_parts.append(jnp.exp2(m_old - m_new))
            alphas.append(jnp.concatenate(alpha_parts, axis=1))
        return alphas

    def pv_pair(slot, j, p, alphas):
        lhs = jnp.concatenate([vt_ref[p * 2 * hd:(p + 1) * 2 * hd, pl.ds(key_tile(j), KEY_TILE)], ones_rows],
                              axis=0)
        out = _dot(lhs, p_scr[slot, :, p * pair_w:(p + 1) * pair_w])
        for half in range(2):
            h = 2 * p + half
            rows_h = slice(h * hd, (h + 1) * hd)
            q_cols = slice(half * ql, (half + 1) * ql)
            acc_scr[rows_h, :] = acc_scr[rows_h, :] * alphas[half] + out[half * hd:(half + 1) * hd, q_cols]
            l_scr[h:h + 1, :] = l_scr[h:h + 1, :] * alphas[half] + out[2 * hd:2 * hd + 1, q_cols]

    def pending_alphas(p):
        return [alpha_scr[h:h + 1, :] for h in (2 * p, 2 * p + 1)]

    def clear_pending():
        p_scr[1] = jnp.zeros(p_scr.shape[1:], BF16)
        alpha_scr[...] = jnp.ones(alpha_scr.shape, F32)

    def pair_step(ja, j_pending, j_next, near, parity):
        alphas_a = []
        for p in range(n_pairs):
            pv_pair(1, j_pending, p, pending_alphas(p))
            qk_pair(1, ja + 1, p)
            alphas_a.append(softmax_pair(0, ja, p, near, parity))
        for p in range(n_pairs):
            pv_pair(0, ja, p, alphas_a[p])
            qk_pair(0, j_next, p)
            alphas_b = softmax_pair(1, ja + 1, p, near, parity)
            for half in range(2):
                alpha_scr[2 * p + half:2 * p + half + 1, :] = alphas_b[half]
        mask_pair(ja + 2, 1 - parity)

    def single_step(ja, j_pending, near, parity):
        alphas_a = []
        for p in range(n_pairs):
            pv_pair(1, j_pending, p, pending_alphas(p))
            alphas_a.append(softmax_pair(0, ja, p, near, parity))
        for p in range(n_pairs):
            pv_pair(0, ja, p, alphas_a[p])
        clear_pending()

    n_far = 2 * (jnp.maximum(n_kt - 2, 0) // 2)
    n_near = n_kt - n_far
    first_near_parity = (n_far // 2) % 2
    clear_pending()
    tie_scr[...] = jnp.zeros(tie_scr.shape, F32)
    mask_pair(0, 0)
    for p in range(n_pairs):
        qk_pair(0, 0, p)

    def far_body(jp, carry):
        pair_step(2 * jp, jnp.maximum(2 * jp - 1, 0), 2 * jp + 2, near=False, parity=jp % 2)
        return carry

    lax.fori_loop(0, n_far // 2, far_body, 0)

    @pl.when(n_near >= 2)
    def _():
        pair_step(n_far, jnp.maximum(n_far - 1, 0), jnp.minimum(n_far + 2, last), near=True,
                  parity=first_near_parity)

    @pl.when(n_near % 2 == 1)
    def _():
        single_step(last, jnp.where(n_near == 3, n_far + 1, jnp.maximum(n_far - 1, 0)), near=True,
                    parity=jnp.where(n_near == 3, 1 - first_near_parity, first_near_parity))

    for p in range(n_pairs):
        pv_pair(1, last, p, pending_alphas(p))

    for h in range(N_ATTN_HEADS):
        rows_h = slice(h * hd, (h + 1) * hd)
        acc_scr[rows_h, :] = acc_scr[rows_h, :] / l_scr[h:h + 1, :]
    o_ref[...] = acc_scr[...].T


def _dsa_kernel(*refs, topk, t_valid):
    o_ref = refs[8]
    i = pl.program_id(1)
    is_real = i * Q_BLOCK < t_valid

    @pl.when(is_real)
    def _():
        _dsa_block(i, *refs, topk=topk)

    @pl.when(jnp.logical_not(is_real))
    def _():
        o_ref[...] = jnp.zeros(o_ref.shape, F32)


def _dsa_attention(qit, rows, qt, kidx, k, vt, bias, tri, batch, tp, t_valid, topk):
    n_rows = k.shape[0]
    ql = Q_BLOCK
    nqb = tp // ql
    d_idx = N_IDX_HEADS * IDX_DIM
    n_pairs = N_ATTN_HEADS // 2
    key_rows = -(-tp // SCAN_TILE) * SCAN_TILE
    q_cols = lambda height: pl.BlockSpec((None, height, ql), lambda b, i: (b, 0, i))
    return pl.pallas_call(
        functools.partial(_dsa_kernel, topk=topk, t_valid=t_valid),
        out_shape=jax.ShapeDtypeStruct((n_rows, D_ATTN), F32),
        grid=(batch, nqb),
        in_specs=[q_cols(d_idx),
                  pl.BlockSpec((16, ql), lambda b, i: (0, b * nqb + i)),
                  q_cols(D_ATTN),
                  pl.BlockSpec((tp, LANES), lambda b, i: (b, 0), pipeline_mode=pl.Buffered(1)),
                  pl.BlockSpec((tp, D_ATTN), lambda b, i: (b, 0), pipeline_mode=pl.Buffered(1)),
                  pl.BlockSpec((None, D_ATTN, tp), lambda b, i: (b, 0, 0), pipeline_mode=pl.Buffered(1)),
                  pl.BlockSpec((3, LANES, N_ATTN_HEADS * LANES), lambda b, i: (0, 0, 0)),
                  pl.BlockSpec((KEY_TILE, KEY_TILE), lambda b, i: (0, 0))],
        out_specs=pl.BlockSpec((ql, D_ATTN), lambda b, i: (b * nqb + i, 0)),
        scratch_shapes=[pltpu.VMEM((key_rows, ql), F32),
                        pltpu.VMEM((LANES, N_IDX_HEADS * ql), BF16),
                        pltpu.VMEM((n_pairs, 2 * ATTN_HEAD_DIM, 2 * ql), BF16),
                        pltpu.VMEM((2, KEY_TILE, N_ATTN_HEADS * ql), F32),
                        pltpu.VMEM((2, KEY_TILE, N_ATTN_HEADS * ql), BF16),
                        pltpu.VMEM((N_ATTN_HEADS, ql), F32),
                        pltpu.VMEM((N_ATTN_HEADS, ql), F32),
                        pltpu.VMEM((D_ATTN, ql), F32),
                        pltpu.VMEM((N_ATTN_HEADS, ql), F32),
                        pltpu.VMEM((2, 2, KEY_TILE, ql), F32),
                        pltpu.VMEM((8, ql), F32)],
        compiler_params=_params("parallel", "parallel"),
        name="dsa_attention",
    )(qit, rows, qt, kidx, k, vt, bias, tri)


def _gdn_prep_kernel(x_ref, halo_ref, cw_ref, rows_ref, alog_ref, dtb_ref, q_ref, k_ref, v_ref, bg_ref, buf):
    first = pl.program_id(1) == 0
    rb = x_ref.shape[0]
    buf[0:HALO_ROWS, :] = jnp.where(first, 0.0, halo_ref[...])
    buf[HALO_ROWS:, :] = x_ref[...]
    acc = jnp.zeros((rb, 3 * D_GDN), F32)
    for tap in range(CONV_WIDTH):
        start = HALO_ROWS - (CONV_WIDTH - 1) + tap
        acc = acc + cw_ref[tap:tap + 1, :] * buf[start:start + rb, :]
    y = _silu(acc)
    for h in range(N_GDN_HEADS):
        cols = slice(h * GDN_HEAD_DIM, (h + 1) * GDN_HEAD_DIM)
        qh = y[:, cols]
        kh = y[:, D_GDN + h * GDN_HEAD_DIM:D_GDN + (h + 1) * GDN_HEAD_DIM]
        q_ref[:, cols] = (qh * lax.rsqrt(jnp.sum(qh * qh, axis=-1, keepdims=True) + EPS)
                          * (GDN_HEAD_DIM ** -0.5))
        k_ref[:, cols] = kh * lax.rsqrt(jnp.sum(kh * kh, axis=-1, keepdims=True) + EPS)
    v_ref[...] = y[:, 2 * D_GDN:]
    rows = rows_ref[...]
    beta = _sigmoid(rows[4:8, :])
    a = rows[8:12, :] + dtb_ref[...]
    softplus = jnp.maximum(a, 0.0) + jnp.log1p(jnp.exp(-jnp.abs(a)))
    bg_ref[0:4, :] = beta
    bg_ref[4:8, :] = -jnp.exp(alog_ref[...]) * softplus


def _gdn_prep(proj, rows, lw, batch, tp, rb):
    n_rows = proj.shape[0]
    nb = tp // rb
    halo_per_block = rb // HALO_ROWS
    row_spec = pl.BlockSpec((rb, D_GDN), lambda b, i: (b * nb + i, 0))
    return pl.pallas_call(
        _gdn_prep_kernel,
        out_shape=(jax.ShapeDtypeStruct((n_rows, D_GDN), F32),) * 3
        + (jax.ShapeDtypeStruct((8, n_rows), F32),),
        grid=(batch, nb),
        in_specs=[pl.BlockSpec((rb, 3 * D_GDN), lambda b, i: (b * nb + i, 0)),
                  pl.BlockSpec((HALO_ROWS, 3 * D_GDN),
                               lambda b, i: (jnp.maximum((b * nb + i) * halo_per_block - 1, 0), 0)),
                  pl.BlockSpec((CONV_WIDTH, 3 * D_GDN), lambda b, i: (0, 0)),
                  pl.BlockSpec((16, rb), lambda b, i: (0, b * nb + i)),
                  pl.BlockSpec((N_GDN_HEADS, 1), lambda b, i: (0, 0)),
                  pl.BlockSpec((N_GDN_HEADS, 1), lambda b, i: (0, 0))],
        out_specs=(row_spec, row_spec, row_spec,
                   pl.BlockSpec((8, rb), lambda b, i: (0, b * nb + i))),
        scratch_shapes=[pltpu.VMEM((HALO_ROWS + rb, 3 * D_GDN), F32)],
        compiler_params=_params("parallel", "parallel"),
        name="gdn_prep",
    )(proj, proj, lw["conv_w"], rows, lw["a_log"], lw["dt_bias"])


def _gdn_chunk_kernel(q_ref, k_ref, v_ref, bg_ref, m_ref, n_ref, p_ref, r_ref, cd_ref):
    c = LANES
    n_chunks = q_ref.shape[0] // c
    items = [(ch, h) for ch in range(n_chunks) for h in range(N_GDN_HEADS)]
    idx = range(len(items))
    row = lax.broadcasted_iota(I32, (c, c), 0)
    col = lax.broadcasted_iota(I32, (c, c), 1)
    tri = row >= col
    strict = row > col
    eye = jnp.where(row == col, 1.0, 0.0)
    lane8 = lax.broadcasted_iota(I32, (8, c), 1)
    gates, decays = [], []
    for ch in range(n_chunks):
        bg = bg_ref[:, ch * c:(ch + 1) * c]
        dec = bg
        shift = 1
        while shift < c:
            dec = dec + jnp.where(lane8 >= shift, pltpu.roll(dec, shift, 1), 0.0)
            shift *= 2
        gates.append(bg)
        decays.append(dec)

    def tokens(ref, n):
        ch, h = items[n]
        return ref[ch * c:(ch + 1) * c, h * GDN_HEAD_DIM:(h + 1) * GDN_HEAD_DIM]

    d_row = [jnp.broadcast_to(decays[ch][4 + h:5 + h, :], (c, c)) for ch, h in items]
    d_col = [d.T for d in d_row]
    beta_col = [jnp.broadcast_to(gates[ch][h:h + 1, :], (c, c)).T for ch, h in items]
    d_last = [d[:, c - 1:c] for d in d_row]
    gamma = [jnp.exp(jnp.where(tri, d_col[n] - d_row[n], MASKED_LOGIT)) for n in idx]
    exp_d = [jnp.exp(d_col[n]) for n in idx]
    k16 = [tokens(k_ref, n).astype(BF16) for n in idx]
    kb = [tokens(k_ref, n) * beta_col[n] for n in idx]
    nil = [jnp.where(strict, _dot_nt(kb[n].astype(BF16), k16[n]) * gamma[n], 0.0) for n in idx]
    block = 8
    same = lambda size: (row // size) == (col // size)
    diag = [jnp.where(same(block), x, 0.0) for x in nil]
    diag_parts = [_split_bf16(x) for x in diag]
    inv = [eye - x for x in diag]
    power = [_dot_split(x, x) for x in diag_parts]
    for it in range(2):
        power_parts = [_split_bf16(x) for x in power]
        inv = [inv[n] + _dot_split(_split_bf16(inv[n]), power_parts[n]) for n in idx]
        if it == 0:
            power = [_dot_split(x, x) for x in power_parts]
    while block < c:
        couples = same(2 * block) & ((row // block) % 2 == 1) & ((col // block) % 2 == 0)
        inv_parts = [_split_bf16(x) for x in inv]
        lower = [_dot_split(_split_bf16(jnp.where(couples, nil[n], 0.0)), inv_parts[n]) for n in idx]
        inv = [inv[n] - _dot_split(inv_parts[n], _split_bf16(lower[n])) for n in idx]
        block *= 2
    rhs = [jnp.concatenate([kb[n] * exp_d[n], tokens(v_ref, n) * beta_col[n]], axis=1) for n in idx]
    wu = [_dot_split(_split_bf16(inv[n]), _split_bf16(rhs[n])).astype(BF16) for n in idx]
    aqk = [jnp.where(tri, _dot_nt(tokens(q_ref, n).astype(BF16), k16[n]) * gamma[n], 0.0).astype(BF16)
           for n in idx]
    kd_t = [(tokens(k_ref, n) * jnp.exp(d_last[n] - d_col[n])).T.astype(BF16) for n in idx]
    state_wu = [_dot(kd_t[n], wu[n]) for n in idx]
    out_wu = [_dot(aqk[n], wu[n]) for n in idx]
    for n, (ch, h) in enumerate(items):
        m_ref[ch, h] = (-state_wu[n][:, :c]).astype(BF16)
        n_ref[ch, h] = state_wu[n][:, c:]
        p_ref[ch, h] = (tokens(q_ref, n) * exp_d[n] - out_wu[n][:, :c]).astype(BF16)
        r_ref[ch, h] = out_wu[n][:, c:]
        cd_ref[ch, h:h + 1, :] = jnp.exp(d_last[n][0:1, :] + jnp.zeros((1, c), F32))


def _gdn_chunks(qn, kn, vv, bg):
    n_rows = qn.shape[0]
    nc = n_rows // LANES
    per_step = GDN_CHUNKS_PER_STEP
    tok = pl.BlockSpec((per_step * LANES, D_GDN), lambda c: (c, 0))
    mat = pl.BlockSpec((per_step, N_GDN_HEADS, LANES, LANES), lambda c: (c, 0, 0, 0))
    mat_shape = lambda dt: jax.ShapeDtypeStruct((nc, N_GDN_HEADS, LANES, LANES), dt)
    return pl.pallas_call(
        _gdn_chunk_kernel,
        out_shape=(mat_shape(BF16), mat_shape(F32), mat_shape(BF16), mat_shape(F32),
                   jax.ShapeDtypeStruct((nc, N_GDN_HEADS, LANES), F32)),
        grid=(nc // per_step,),
        in_specs=[tok, tok, tok, pl.BlockSpec((8, per_step * LANES), lambda c: (0, c))],
        out_specs=(mat, mat, mat, mat, pl.BlockSpec((per_step, N_GDN_HEADS, LANES), lambda c: (c, 0, 0))),
        compiler_params=_params("parallel"),
        name="gdn_chunks",
    )(qn, kn, vv, bg)


def _gdn_scan_kernel(m_ref, n_ref, p_ref, r_ref, cd_ref, o_ref, s_scr, *, batch):
    @pl.when(pl.program_id(0) == 0)
    def _():
        s_scr[...] = jnp.zeros(s_scr.shape, F32)

    for ch in range(m_ref.shape[1]):
        rows = slice(ch * LANES, (ch + 1) * LANES)
        for b in range(batch):
            for h in range(N_GDN_HEADS):
                s = s_scr[b, h]
                s16 = s.astype(BF16)
                o_ref[b, rows, h * GDN_HEAD_DIM:(h + 1) * GDN_HEAD_DIM] = _dot(p_ref[b, ch, h], s16) + r_ref[b, ch, h]
                s_scr[b, h] = s * cd_ref[b, ch, h:h + 1, :] + _dot(m_ref[b, ch, h], s16) + n_ref[b, ch, h]


def _gdn_scan(m_mat, n_mat, p_mat, r_mat, cd, batch, tp):
    nc = tp // LANES
    per_step = next(d for d in SCAN_CHUNKS if nc % d == 0)
    shape5 = lambda a: a.reshape(batch, nc, N_GDN_HEADS, LANES, LANES)
    mat = pl.BlockSpec((batch, per_step, N_GDN_HEADS, LANES, LANES), lambda c: (0, c, 0, 0, 0))
    return pl.pallas_call(
        functools.partial(_gdn_scan_kernel, batch=batch),
        out_shape=jax.ShapeDtypeStruct((batch, tp, D_GDN), F32),
        grid=(nc // per_step,),
        in_specs=[mat, mat, mat, mat,
                  pl.BlockSpec((batch, per_step, N_GDN_HEADS, LANES), lambda c: (0, c, 0, 0))],
        out_specs=pl.BlockSpec((batch, per_step * LANES, D_GDN), lambda c: (0, c, 0)),
        scratch_shapes=[pltpu.VMEM((batch, N_GDN_HEADS, LANES, LANES), F32)],
        compiler_params=_params("arbitrary"),
        name="gdn_scan",
    )(shape5(m_mat), shape5(n_mat), shape5(p_mat), shape5(r_mat), cd.reshape(batch, nc, N_GDN_HEADS, LANES))


def _out_proj_kernel(h_ref, oa_ref, za_ref, og_ref, zg_ref, gain_ref, w_ref, out_ref):
    attn = (oa_ref[...] * _silu(za_ref[...])).astype(BF16)
    y = _dot(attn, w_ref[0:D_ATTN, :])
    og, zg = og_ref[...], zg_ref[...]
    for h in range(N_GDN_HEADS):
        cols = slice(h * GDN_HEAD_DIM, (h + 1) * GDN_HEAD_DIM)
        gated = (_rms_rows(og[:, cols], gain_ref[...]) * _silu(zg[:, cols])).astype(BF16)
        y = y + _dot(gated, w_ref[D_ATTN + h * GDN_HEAD_DIM:D_ATTN + (h + 1) * GDN_HEAD_DIM, :])
    out_ref[...] = h_ref[...] + y


def _out_proj(h2, o_attn, proj, o_gdn, gain, w_out, rb):
    n_rows = h2.shape[0]
    blk = lambda width, col: pl.BlockSpec((rb, width), lambda i: (i, col))
    return pl.pallas_call(
        _out_proj_kernel,
        out_shape=jax.ShapeDtypeStruct((n_rows, D_MODEL), F32),
        grid=(n_rows // rb,),
        in_specs=[blk(D_MODEL, 0), blk(D_ATTN, 0), blk(D_ATTN, COL_Z_ATTN), blk(D_GDN, 0),
                  blk(D_GDN, COL_Z_GDN),
                  pl.BlockSpec((1, GDN_HEAD_DIM), lambda i: (0, 0)),
                  pl.BlockSpec((D_MODEL, D_MODEL), lambda i: (0, 0))],
        out_specs=blk(D_MODEL, 0),
        compiler_params=_params("parallel"),
        name="out_proj",
    )(h2, o_attn, proj, o_gdn, proj, gain, w_out)


def _pack_layer(norm_gain, w_in, cq_gain, ckv_gain, w_uq, w_ukv, w_q_idx, q_gain, k_gain, conv_w, a_log,
                dt_bias, gdn_gain, w_out):
    o = 0
    parts = {}
    for name, size in (("c_q", Q_RANK), ("c_kv", KV_RANK), ("k_idx", IDX_DIM), ("w_idx", N_IDX_HEADS),
                       ("z_attn", D_ATTN), ("qkv_g", 3 * D_GDN), ("z_g", D_GDN), ("b", N_GDN_HEADS),
                       ("a", N_GDN_HEADS)):
        parts[name] = w_in[:, o:o + size]
        o += size
    w_packed = jnp.concatenate([parts["qkv_g"], parts["c_q"], parts["c_kv"], parts["k_idx"], parts["k_idx"],
                                parts["z_attn"], parts["z_g"]], axis=1).astype(BF16)
    w_rows = jnp.concatenate([parts["w_idx"], parts["b"], parts["a"],
                              jnp.zeros((D_MODEL, 4), w_in.dtype)], axis=1).T.astype(BF16)
    return dict(
        gain=norm_gain[None, :], w_packed=w_packed, w_rows=w_rows,
        g_cq=cq_gain[None, :], g_ckv=ckv_gain[None, :],
        w_uqt=w_uq.T.astype(BF16), w_qit=w_q_idx.T.astype(BF16),
        w_ukt=w_ukv[:, :D_ATTN].T.astype(BF16), w_uvt=w_ukv[:, D_ATTN:].T.astype(BF16),
        g_q_col=q_gain[:, None], g_k_col=k_gain[:, None],
        conv_w=conv_w, a_log=a_log[:, None], dt_bias=dt_bias[:, None],
        gdn_gain=gdn_gain[None, :], w_out=w_out.astype(BF16))


def _layer(h2, lw, bias, tri, batch, tp, t_valid, topk):
    rb = _row_block(tp)
    proj, rows = _in_proj(h2, lw["gain"], lw["w_packed"], lw["w_rows"], rb)
    qt, k, vt, qit, kidx = _dsa_prep(proj, lw, batch, tp, rb)
    o_attn = _dsa_attention(qit, rows, qt, kidx, k, vt, bias, tri, batch, tp, t_valid, topk)
    qn, kn, vv, bg = _gdn_prep(proj, rows, lw, batch, tp, rb)
    m_mat, n_mat, p_mat, r_mat, cd = _gdn_chunks(qn, kn, vv, bg)
    o_gdn = _gdn_scan(m_mat, n_mat, p_mat, r_mat, cd, batch, tp).reshape(batch * tp, D_GDN)
    return _out_proj(h2, o_attn, proj, o_gdn, lw["gdn_gain"], lw["w_out"], rb)


def _forward(x, meta_tokens, rel_bias_table, layer_weights, topk):
    batch, seq, _ = x.shape
    t = seq + N_META
    tp = -(-t // KEY_TILE) * KEY_TILE
    meta = jnp.broadcast_to(meta_tokens[None].astype(x.dtype), (batch, N_META, D_MODEL))
    h = jnp.concatenate([meta, x, jnp.zeros((batch, tp - t, D_MODEL), x.dtype)], axis=1)
    h2 = h.reshape(batch * tp, D_MODEL)
    bias = _bias_tiles(rel_bias_table)
    tri = jnp.tril(jnp.ones((KEY_TILE, KEY_TILE), BF16))
    for lw in layer_weights:
        h2 = _layer(h2, lw, bias, tri, batch, tp, t, topk)
    return h2.reshape(batch, tp, D_MODEL)[:, N_META:t]


def kernel(x, meta_tokens, rel_bias_table, norm_gain, w_in, cq_norm_gain, ckv_norm_gain, w_uq, w_ukv, w_q_idx,
           q_norm_gain, k_norm_gain, conv_w, a_log, dt_bias, gdn_norm_gain, w_out):
    depth = norm_gain.shape[0]
    topk = min(TOPK_MAX, x.shape[1] // 4)
    layers = [_pack_layer(norm_gain[l], w_in[l], cq_norm_gain[l], ckv_norm_gain[l], w_uq[l], w_ukv[l],
                          w_q_idx[l], q_norm_gain[l], k_norm_gain[l], conv_w[l], a_log[l], dt_bias[l],
                          gdn_norm_gain[l], w_out[l]) for l in range(depth)]
    return _forward(x, meta_tokens, rel_bias_table, layers, topk)
```

```python
import functools
import math

import jax
import jax.numpy as jnp
from jax import lax
from jax.experimental import pallas as pl
from jax.experimental.pallas import tpu as pltpu

F32 = jnp.float32
BF16 = jnp.bfloat16
I32 = jnp.int32

D_MODEL = 1024
N_META = 16
EPS = 1e-6
N_ATTN_HEADS = 8
ATTN_HEAD_DIM = 64
D_ATTN = N_ATTN_HEADS * ATTN_HEAD_DIM
Q_RANK = 256
KV_RANK = 128
N_IDX_HEADS = 4
IDX_DIM = 64
TOPK_MAX = 256
N_REL_BUCKETS = 32
REL_MAX_DIST = 128
N_GDN_HEADS = 4
GDN_HEAD_DIM = 128
D_GDN = N_GDN_HEADS * GDN_HEAD_DIM
CONV_WIDTH = 4

LANES = 128
KEY_TILE = 256
COUNT_TILE = 512
SCAN_TILE = 1024
FIRST_CHECK_PASSES = 10
INTERPOLATED_PASSES = 24
MAX_SEARCH_PASSES = INTERPOLATED_PASSES + 32
GDN_CHUNKS_PER_STEP = 2
SCAN_CHUNKS = (6, 4, 3, 2, 1)
ROW_TILES = (6, 5, 4, 3, 2, 1)
HALO_ROWS = 8
D_PACKED = 3 * D_GDN + 512 + D_ATTN + D_GDN
COL_SMALL = 3
COL_Z_ATTN = 4
COL_Z_GDN = 5
MASKED_LOGIT = -1e30
LOG2_E = math.log2(math.e)
KEY_MIN = -2 ** 31
PATTERN_NEG_FLT_MAX = KEY_MIN + (1 << 23)
VMEM_LIMIT = 56 * 1024 * 1024

NT_DIMS = (((1,), (1,)), ((), ()))


def _dot(a, b):
    return jnp.dot(a, b, preferred_element_type=F32)


def _dot_nt(a, b):
    return lax.dot_general(a, b, NT_DIMS, preferred_element_type=F32)


def _split_bf16(x):
    hi = x.astype(BF16)
    return hi, (x - hi.astype(F32)).astype(BF16)


def _dot_split(a_parts, b_parts):
    (a_hi, a_lo), (b_hi, b_lo) = a_parts, b_parts
    return _dot(a_hi, b_hi) + (_dot(a_hi, b_lo) + _dot(a_lo, b_hi))


def _sigmoid(x):
    return 1.0 / (1.0 + jnp.exp(-x))


def _silu(x):
    return x * _sigmoid(x)


def _row_block(tp):
    tiles = tp // LANES
    return LANES * next(d for d in ROW_TILES if tiles % d == 0)


def _params(*sem):
    return pltpu.CompilerParams(dimension_semantics=sem, vmem_limit_bytes=VMEM_LIMIT)


def _bias_kernel(table_ref, out_ref):
    row = lax.broadcasted_iota(I32, (LANES, LANES), 0)
    col = lax.broadcasted_iota(I32, (LANES, LANES), 1)
    max_exact = N_REL_BUCKETS // 2
    for kind in range(3):
        dist = col - row + (2 - kind) * LANES
        n = jnp.maximum(dist, 0)
        nf = jnp.maximum(n, 1).astype(F32)
        large = max_exact + (jnp.log(nf / max_exact) / math.log(REL_MAX_DIST / max_exact)
                             * (N_REL_BUCKETS - max_exact)).astype(I32)
        large = jnp.minimum(large, N_REL_BUCKETS - 1)
        bucket = jnp.where(n < max_exact, n, large)
        for h in range(N_ATTN_HEADS):
            tile = jnp.zeros((LANES, LANES), F32)
            for b in range(N_REL_BUCKETS):
                tile = jnp.where(bucket == b, table_ref[b, h], tile)
            far = table_ref[N_REL_BUCKETS - 1, h]
            out_ref[kind, :, h * LANES:(h + 1) * LANES] = (tile - far) * LOG2_E


def _bias_tiles(rel_table):
    return pl.pallas_call(
        _bias_kernel,
        out_shape=jax.ShapeDtypeStruct((3, LANES, N_ATTN_HEADS * LANES), F32),
        in_specs=[pl.BlockSpec(memory_space=pltpu.SMEM)],
        out_specs=pl.BlockSpec(memory_space=pltpu.VMEM),
        name="rel_bias_tiles",
    )(rel_table)


def _in_proj_kernel(h_ref, gain_ref, w_ref, wrows_ref, proj_ref, rows_ref):
    x = h_ref[...]
    y = x * lax.rsqrt(jnp.mean(x * x, axis=-1, keepdims=True) + EPS)
    hn = (y * gain_ref[...]).astype(BF16)
    proj_ref[...] = _dot(hn, w_ref[...])
    rows_ref[...] = _dot_nt(wrows_ref[...], hn)


def _in_proj(h2, gain, w_packed, w_rows, rb):
    n_rows = h2.shape[0]
    grid = (n_rows // rb,)
    return pl.pallas_call(
        _in_proj_kernel,
        out_shape=(jax.ShapeDtypeStruct((n_rows, D_PACKED), F32),
                   jax.ShapeDtypeStruct((16, n_rows), F32)),
        grid=grid,
        in_specs=[pl.BlockSpec((rb, D_MODEL), lambda i: (i, 0)),
                  pl.BlockSpec((1, D_MODEL), lambda i: (0, 0)),
                  pl.BlockSpec((D_MODEL, D_PACKED), lambda i: (0, 0)),
                  pl.BlockSpec((16, D_MODEL), lambda i: (0, 0))],
        out_specs=(pl.BlockSpec((rb, D_PACKED), lambda i: (i, 0)),
                   pl.BlockSpec((16, rb), lambda i: (0, i))),
        compiler_params=_params("parallel"),
        name="in_proj",
    )(h2, gain, w_packed, w_rows)


def _rms_rows(x, gain):
    return x * lax.rsqrt(jnp.mean(x * x, axis=-1, keepdims=True) + EPS) * gain


def _dsa_prep_kernel(sm_ref, gcq_ref, gckv_ref, wuqt_ref, wqit_ref, wukt_ref, wuvt_ref, gq_ref, gk_ref,
                     qt_ref, k_ref, vt_ref, qit_ref, kidx_ref):
    sm = sm_ref[...]
    rb = sm.shape[0]
    cq = _rms_rows(sm[:, :Q_RANK], gcq_ref[...]).astype(BF16)
    ckv = _rms_rows(sm[:, Q_RANK:Q_RANK + KV_RANK], gckv_ref[...]).astype(BF16)
    kidx_ref[...] = sm[:, Q_RANK + KV_RANK:].astype(BF16)
    q3 = _dot_nt(wuqt_ref[...], cq).reshape(N_ATTN_HEADS, ATTN_HEAD_DIM, rb)
    q3 = q3 * lax.rsqrt(jnp.mean(q3 * q3, axis=1, keepdims=True) + EPS) * gq_ref[...][None]
    qt_ref[...] = (q3 * (ATTN_HEAD_DIM ** -0.5 * LOG2_E)).reshape(D_ATTN, rb).astype(BF16)
    k3 = _dot_nt(wukt_ref[...], ckv).reshape(N_ATTN_HEADS, ATTN_HEAD_DIM, rb)
    k3 = k3 * lax.rsqrt(jnp.mean(k3 * k3, axis=1, keepdims=True) + EPS) * gk_ref[...][None]
    k_ref[...] = k3.reshape(D_ATTN, rb).T.astype(BF16)
    vt_ref[...] = _dot_nt(wuvt_ref[...], ckv).astype(BF16)
    qit_ref[...] = _dot_nt(wqit_ref[...], cq).astype(BF16)


def _dsa_prep(proj, lw, batch, tp, rb):
    n_rows = proj.shape[0]
    nb = tp // rb
    d_idx = N_IDX_HEADS * IDX_DIM
    const = lambda shape: pl.BlockSpec(shape, lambda b, i: (0, 0))
    row_spec = lambda width: pl.BlockSpec((rb, width), lambda b, i: (b * nb + i, 0))
    col_spec = lambda height: pl.BlockSpec((None, height, rb), lambda b, i: (b, 0, i))
    return pl.pallas_call(
        _dsa_prep_kernel,
        out_shape=(jax.ShapeDtypeStruct((batch, D_ATTN, tp), BF16),
                   jax.ShapeDtypeStruct((n_rows, D_ATTN), BF16),
                   jax.ShapeDtypeStruct((batch, D_ATTN, tp), BF16),
                   jax.ShapeDtypeStruct((batch, d_idx, tp), BF16),
                   jax.ShapeDtypeStruct((n_rows, LANES), BF16)),
        grid=(batch, nb),
        in_specs=[pl.BlockSpec((rb, 512), lambda b, i: (b * nb + i, COL_SMALL)),
                  const((1, Q_RANK)), const((1, KV_RANK)),
                  const((D_ATTN, Q_RANK)), const((d_idx, Q_RANK)),
                  const((D_ATTN, KV_RANK)), const((D_ATTN, KV_RANK)),
                  const((ATTN_HEAD_DIM, 1)), const((ATTN_HEAD_DIM, 1))],
        out_specs=(col_spec(D_ATTN), row_spec(D_ATTN), col_spec(D_ATTN), col_spec(d_idx), row_spec(LANES)),
        compiler_params=_params("parallel", "parallel"),
        name="dsa_prep",
    )(proj, lw["g_cq"], lw["g_ckv"], lw["w_uqt"], lw["w_qit"], lw["w_ukt"], lw["w_uvt"], lw["g_q_col"],
      lw["g_k_col"])


def _dsa_block(i, qit_ref, rows_ref, qt_ref, kidx_ref, k_ref, vt_ref, bias_ref, tri_ref, o_ref,
               score_scr, wi_scr, wq_scr, s_scr, p_scr, m_scr, l_scr, acc_scr, alpha_scr, mask_scr, tie_scr, *, topk):
    t0 = i * LANES
    n_kt = i // 2 + 1
    n_st = (n_kt + 3) // 4
    hd = ATTN_HEAD_DIM
    pair_w = 2 * LANES
    n_pairs = N_ATTN_HEADS // 2

    zeros_hd = jnp.zeros((hd, LANES), BF16)
    for h in range(N_IDX_HEADS):
        wi_scr[0:IDX_DIM, h * LANES:(h + 1) * LANES] = qit_ref[h * IDX_DIM:(h + 1) * IDX_DIM, :]
    wi_scr[IDX_DIM:, :] = jnp.zeros((LANES - IDX_DIM, N_IDX_HEADS * LANES), BF16)
    for p in range(n_pairs):
        wq_scr[p, 0:hd, 0:LANES] = qt_ref[2 * p * hd:(2 * p + 1) * hd, :]
        wq_scr[p, 0:hd, LANES:] = zeros_hd
        wq_scr[p, hd:, 0:LANES] = zeros_hd
        wq_scr[p, hd:, LANES:] = qt_ref[(2 * p + 1) * hd:(2 * p + 2) * hd, :]

    row = lax.broadcasted_iota(I32, (KEY_TILE, LANES), 0)
    col = lax.broadcasted_iota(I32, (KEY_TILE, LANES), 1)
    w_idx = rows_ref[0:N_IDX_HEADS, :] * (N_IDX_HEADS ** -0.5 * IDX_DIM ** -0.5)

    def key_tile(j):
        return pl.multiple_of(j * KEY_TILE, KEY_TILE)

    def causal(j):
        return (j * KEY_TILE + row) <= (t0 + col)

    def fold8(x, op):
        return op(x.reshape(KEY_TILE // 8, 8, LANES), axis=0)

    def score_step(jc, carry, masked):
        top, bottom, n_nonneg, n_pos = carry
        subs = [jc * (SCAN_TILE // KEY_TILE) + sub for sub in range(SCAN_TILE // KEY_TILE)]
        logits = [_dot(kidx_ref[pl.ds(key_tile(jnp.minimum(j, n_kt - 1)), KEY_TILE), :], wi_scr[...])
                  for j in subs]
        for j, lg in zip(subs, logits):
            score = jnp.zeros((KEY_TILE, LANES), F32)
            for h in range(N_IDX_HEADS):
                score = score + jnp.maximum(lg[:, h * LANES:(h + 1) * LANES], 0.0) * w_idx[h:h + 1, :]
            if masked:
                visible = causal(j)
                seen = jnp.where(visible, score, -jnp.inf)
                bottom = jnp.minimum(bottom, fold8(jnp.where(visible, score, jnp.inf), jnp.min))
            else:
                seen = score
                bottom = jnp.minimum(bottom, fold8(score, jnp.min))
            top = jnp.maximum(top, fold8(seen, jnp.max))
            n_nonneg = n_nonneg + fold8(jnp.where(seen >= 0.0, 1, 0), jnp.sum)
            n_pos = n_pos + fold8(jnp.where(seen > 0.0, 1, 0), jnp.sum)
            score_scr[pl.ds(key_tile(j), KEY_TILE), :] = seen
        return top, bottom, n_nonneg, n_pos

    zeros8 = jnp.zeros((8, LANES), I32)
    carry = (jnp.full((8, LANES), -jnp.inf, F32), jnp.full((8, LANES), jnp.inf, F32), zeros8, zeros8)
    carry = lax.fori_loop(0, n_st - 1, functools.partial(score_step, masked=False), carry)
    top, bottom, n_nonneg, n_pos = score_step(n_st - 1, carry, masked=True)
    top = jnp.max(top, axis=0, keepdims=True)
    bottom = jnp.min(bottom, axis=0, keepdims=True)
    count0 = jnp.sum(n_nonneg, axis=0, keepdims=True)
    count_pos = jnp.sum(n_pos, axis=0, keepdims=True)

    def count_f32(cand, below=None):
        def body(j, carry):
            acc, best = carry
            for part in range(SCAN_TILE // COUNT_TILE):
                start = pl.multiple_of(j * SCAN_TILE + part * COUNT_TILE, COUNT_TILE)
                x = score_scr[pl.ds(start, COUNT_TILE), :]
                ind = jnp.where(x >= cand, 1, 0)
                acc = acc + jnp.sum(ind.reshape(COUNT_TILE // 8, 8, LANES), axis=0)
                if below is not None:
                    under = jnp.where(x < below, x, -jnp.inf)
                    best = jnp.maximum(best, jnp.max(under.reshape(COUNT_TILE // 8, 8, LANES), axis=0))
            return acc, best
        acc, best = lax.fori_loop(0, n_st, body, (jnp.zeros((8, LANES), I32), jnp.full((8, LANES), -jnp.inf, F32)))
        count = jnp.sum(acc, axis=0, keepdims=True)
        if below is None:
            return count
        return count, jnp.max(best, axis=0, keepdims=True)

    def to_pattern(v):
        bits = lax.bitcast_convert_type(v, I32)
        return bits ^ ((bits >> 31) & 0x7FFFFFFF)

    def to_f32(c):
        return lax.bitcast_convert_type(c ^ ((c >> 31) & 0x7FFFFFFF), F32)

    n_visible = t0 + 1 + lax.broadcasted_iota(I32, (1, LANES), 1)
    nonneg = count0 >= topk
    lo = jnp.where(nonneg, 0, to_pattern(bottom))
    hi = jnp.where(nonneg, to_pattern(top) + 1, 0)
    count_lo = jnp.where(nonneg, count0, n_visible)
    count_hi = jnp.where(nonneg, 0, count0)
    few = n_visible < topk
    lo = jnp.where(few, PATTERN_NEG_FLT_MAX, lo)
    zero_tied = nonneg & (count_pos < topk)
    hi = jnp.where(zero_tied, 1, hi)
    count_hi = jnp.where(zero_tied, count_pos, count_hi)
    open_q = jnp.where(few | zero_tied | (count_lo == topk), 0, 1)

    log_topk = math.log(topk)

    def count_error(count):
        return jnp.log(count.astype(F32) + 0.5) - log_topk

    def probe(n_pass, carry, extract):
        lo, hi, count_lo, count_hi, err_lo, err_hi, last_side, open_q = carry
        v_lo, v_hi = to_f32(lo), to_f32(hi)
        frac = err_lo / (err_lo - err_hi)
        frac = jnp.where(count_lo - count_hi <= 4, 0.5, frac)
        guess = to_pattern(v_lo + (v_hi - v_lo) * frac)
        middle = lo + lax.shift_right_logical(hi - lo, 1)
        cand = jnp.where(n_pass >= INTERPOLATED_PASSES, middle, guess)
        cand = jnp.minimum(jnp.maximum(cand, lo + 1), hi - 1)
        is_open = open_q == 1
        if extract:
            count, under_hi = count_f32(to_f32(cand), below=v_hi)
            next_below = to_pattern(under_hi)
            found = is_open & (count_hi == topk - 1)
            is_open = is_open & jnp.logical_not(found)
        else:
            count = count_f32(to_f32(cand))
        raise_lo = is_open & (count >= topk)
        lower_hi = is_open & (count < topk)
        err = count_error(count)
        err_hi = jnp.where(raise_lo & (last_side == 1), err_hi * 0.5, err_hi)
        err_lo = jnp.where(lower_hi & (last_side == -1), err_lo * 0.5, err_lo)
        err_lo = jnp.where(raise_lo, err, err_lo)
        err_hi = jnp.where(lower_hi, err, err_hi)
        lo = jnp.where(raise_lo, cand, lo)
        count_lo = jnp.where(raise_lo, count, count_lo)
        hi = jnp.where(lower_hi, cand, hi)
        count_hi = jnp.where(lower_hi, count, count_hi)
        last_side = jnp.where(raise_lo, 1, jnp.where(lower_hi, -1, last_side))
        if extract:
            lo = jnp.where(found, next_below, lo)
            hi = jnp.where(found | raise_lo, next_below + 1, hi)
        closed = (count_lo == topk) | (hi - lo == 1)
        return lo, hi, count_lo, count_hi, err_lo, err_hi, last_side, jnp.where(closed, 0, open_q)

    carry = (lo, hi, count_lo, count_hi, count_error(count_lo), count_error(count_hi),
             jnp.zeros((1, LANES), I32), open_q)
    n_first = jnp.where(jnp.sum(open_q) > 0, FIRST_CHECK_PASSES, 0)
    carry = lax.fori_loop(0, n_first, functools.partial(probe, extract=False), carry)

    def probes_left(st):
        n_pass, n_open = st[0], st[-1]
        return (n_pass < MAX_SEARCH_PASSES) & (n_open > 0)

    def extracting_probe(st):
        carry = probe(st[0], st[1:-1], extract=True)
        return (st[0] + 1,) + carry + (jnp.sum(carry[-1]),)

    state = lax.while_loop(probes_left, extracting_probe,
                           (jnp.int32(FIRST_CHECK_PASSES),) + carry + (jnp.sum(carry[-1]),))
    lo, count_lo, count_hi = state[1], state[3], state[4]
    tau = to_f32(lo)
    need = jnp.where((count_lo == topk) | few, topk, topk - count_hi).astype(F32)

    m_scr[...] = jnp.full(m_scr.shape, MASKED_LOGIT, F32)
    l_scr[...] = jnp.zeros(l_scr.shape, F32)
    acc_scr[...] = jnp.zeros(acc_scr.shape, F32)
    ones_rows = jnp.ones((16, KEY_TILE), BF16)
    last = n_kt - 1

    def mask_pair(j_first, parity):
        xs = [score_scr[pl.ds(key_tile(jnp.minimum(j_first + slot, last)), KEY_TILE), :] for slot in range(2)]
        ties = [x == tau for x in xs]
        tie_cols = jnp.concatenate([jnp.where(tie, 1.0, 0.0).astype(BF16) for tie in ties], axis=1)
        ranks = _dot(tri_ref[...], tie_cols)
        tie_carry = tie_scr[0:1, :]
        for slot, (x, tie) in enumerate(zip(xs, ties)):
            rank = ranks[:, slot * LANES:(slot + 1) * LANES] + tie_carry
            tie_carry = rank[KEY_TILE - 1:KEY_TILE, :]
            take = (tie & (rank <= need)) | (x > tau)
            mask_scr[parity, slot] = jnp.where(take, 0.0, MASKED_LOGIT)
        tie_scr[0:1, :] = tie_carry

    def qk_pair(slot, j, p):
        s_scr[slot, :, p * pair_w:(p + 1) * pair_w] = _dot(
            k_ref[pl.ds(key_tile(j), KEY_TILE), p * LANES:(p + 1) * LANES], wq_scr[p])

    def softmax_pair(slot, j, p, near, parity):
        mask_add = mask_scr[parity, slot]
        alphas = []
        for h in (2 * p, 2 * p + 1):
            cols = slice(h * LANES, (h + 1) * LANES)
            logits = s_scr[slot, :, cols] + mask_add
            if near:
                kind_top = jnp.clip(2 * j - i + 2, 0, 2)
                kind_bot = jnp.clip(2 * j - i + 3, 0, 2)
                logits = logits + jnp.concatenate(
                    [bias_ref[kind_top, :, cols], bias_ref[kind_bot, :, cols]], axis=0)
            m_old = m_scr[h:h + 1, :]
            m_new = jnp.maximum(m_old, jnp.max(logits, axis=0, keepdims=True))
            m_scr[h:h + 1, :] = m_new
            p_scr[slot, :, cols] = jnp.exp2(logits - m_new).astype(BF16)
            alphas.append(jnp.exp2(m_old - m_new))
        return alphas

    def pv_pair(slot, j, p, alphas):
        lhs = jnp.concatenate([vt_ref[p * 2 * hd:(p + 1) * 2 * hd, pl.ds(key_tile(j), KEY_TILE)], ones_rows],
                              axis=0)
        out = _dot(lhs, p_scr[slot, :, p * pair_w:(p + 1) * pair_w])
        for half in range(2):
            h = 2 * p + half
            rows_h = slice(h * hd, (h + 1) * hd)
            q_cols = slice(half * LANES, (half + 1) * LANES)
            acc_scr[rows_h, :] = acc_scr[rows_h, :] * alphas[half] + out[half * hd:(half + 1) * hd, q_cols]
            l_scr[h:h + 1, :] = l_scr[h:h + 1, :] * alphas[half] + out[2 * hd:2 * hd + 1, q_cols]

    def pending_alphas(p):
        return [alpha_scr[h:h + 1, :] for h in (2 * p, 2 * p + 1)]

    def clear_pending():
        p_scr[1] = jnp.zeros(p_scr.shape[1:], BF16)
        alpha_scr[...] = jnp.ones(alpha_scr.shape, F32)

    def pair_step(ja, j_pending, j_next, near, parity):
        alphas_a = []
        for p in range(n_pairs):
            pv_pair(1, j_pending, p, pending_alphas(p))
            qk_pair(1, ja + 1, p)
            alphas_a.append(softmax_pair(0, ja, p, near, parity))
        for p in range(n_pairs):
            pv_pair(0, ja, p, alphas_a[p])
            qk_pair(0, j_next, p)
            alphas_b = softmax_pair(1, ja + 1, p, near, parity)
            for half in range(2):
                alpha_scr[2 * p + half:2 * p + half + 1, :] = alphas_b[half]
        mask_pair(ja + 2, 1 - parity)

    def single_step(ja, j_pending, near, parity):
        alphas_a = []
        for p in range(n_pairs):
            pv_pair(1, j_pending, p, pending_alphas(p))
            alphas_a.append(softmax_pair(0, ja, p, near, parity))
        for p in range(n_pairs):
            pv_pair(0, ja, p, alphas_a[p])
        clear_pending()

    n_far = 2 * (jnp.maximum(n_kt - 2, 0) // 2)
    n_near = n_kt - n_far
    first_near_parity = (n_far // 2) % 2
    clear_pending()
    tie_scr[...] = jnp.zeros(tie_scr.shape, F32)
    mask_pair(0, 0)
    for p in range(n_pairs):
        qk_pair(0, 0, p)

    def far_body(jp, carry):
        pair_step(2 * jp, jnp.maximum(2 * jp - 1, 0), 2 * jp + 2, near=False, parity=jp % 2)
        return carry

    lax.fori_loop(0, n_far // 2, far_body, 0)

    @pl.when(n_near >= 2)
    def _():
        pair_step(n_far, jnp.maximum(n_far - 1, 0), jnp.minimum(n_far + 2, last), near=True,
                  parity=first_near_parity)

    @pl.when(n_near % 2 == 1)
    def _():
        single_step(last, jnp.where(n_near == 3, n_far + 1, jnp.maximum(n_far - 1, 0)), near=True,
                    parity=jnp.where(n_near == 3, 1 - first_near_parity, first_near_parity))

    for p in range(n_pairs):
        pv_pair(1, last, p, pending_alphas(p))

    for h in range(N_ATTN_HEADS):
        rows_h = slice(h * hd, (h + 1) * hd)
        acc_scr[rows_h, :] = acc_scr[rows_h, :] / l_scr[h:h + 1, :]
    o_ref[...] = acc_scr[...].T


def _dsa_kernel(*refs, topk, t_valid):
    o_ref = refs[8]
    i = pl.program_id(1)
    is_real = i * LANES < t_valid

    @pl.when(is_real)
    def _():
        _dsa_block(i, *refs, topk=topk)

    @pl.when(jnp.logical_not(is_real))
    def _():
        o_ref[...] = jnp.zeros(o_ref.shape, F32)


def _dsa_attention(qit, rows, qt, kidx, k, vt, bias, tri, batch, tp, t_valid, topk):
    n_rows = k.shape[0]
    nqb = tp // LANES
    d_idx = N_IDX_HEADS * IDX_DIM
    n_pairs = N_ATTN_HEADS // 2
    key_rows = -(-tp // SCAN_TILE) * SCAN_TILE
    q_cols = lambda height: pl.BlockSpec((None, height, LANES), lambda b, i: (b, 0, i))
    return pl.pallas_call(
        functools.partial(_dsa_kernel, topk=topk, t_valid=t_valid),
        out_shape=jax.ShapeDtypeStruct((n_rows, D_ATTN), F32),
        grid=(batch, nqb),
        in_specs=[q_cols(d_idx),
                  pl.BlockSpec((16, LANES), lambda b, i: (0, b * nqb + i)),
                  q_cols(D_ATTN),
                  pl.BlockSpec((tp, LANES), lambda b, i: (b, 0)),
                  pl.BlockSpec((tp, D_ATTN), lambda b, i: (b, 0)),
                  pl.BlockSpec((None, D_ATTN, tp), lambda b, i: (b, 0, 0)),
                  pl.BlockSpec((3, LANES, N_ATTN_HEADS * LANES), lambda b, i: (0, 0, 0)),
                  pl.BlockSpec((KEY_TILE, KEY_TILE), lambda b, i: (0, 0))],
        out_specs=pl.BlockSpec((LANES, D_ATTN), lambda b, i: (b * nqb + i, 0)),
        scratch_shapes=[pltpu.VMEM((key_rows, LANES), F32),
                        pltpu.VMEM((LANES, N_IDX_HEADS * LANES), BF16),
                        pltpu.VMEM((n_pairs, LANES, 2 * LANES), BF16),
                        pltpu.VMEM((2, KEY_TILE, N_ATTN_HEADS * LANES), F32),
                        pltpu.VMEM((2, KEY_TILE, N_ATTN_HEADS * LANES), BF16),
                        pltpu.VMEM((N_ATTN_HEADS, LANES), F32),
                        pltpu.VMEM((N_ATTN_HEADS, LANES), F32),
                        pltpu.VMEM((D_ATTN, LANES), F32),
                        pltpu.VMEM((N_ATTN_HEADS, LANES), F32),
                        pltpu.VMEM((2, 2, KEY_TILE, LANES), F32),
                        pltpu.VMEM((8, LANES), F32)],
        compiler_params=_params("parallel", "parallel"),
        name="dsa_attention",
    )(qit, rows, qt, kidx, k, vt, bias, tri)


def _gdn_prep_kernel(x_ref, halo_ref, cw_ref, rows_ref, alog_ref, dtb_ref, q_ref, k_ref, v_ref, bg_ref, buf):
    first = pl.program_id(1) == 0
    rb = x_ref.shape[0]
    buf[0:HALO_ROWS, :] = jnp.where(first, 0.0, halo_ref[...])
    buf[HALO_ROWS:, :] = x_ref[...]
    acc = jnp.zeros((rb, 3 * D_GDN), F32)
    for tap in range(CONV_WIDTH):
        start = HALO_ROWS - (CONV_WIDTH - 1) + tap
        acc = acc + cw_ref[tap:tap + 1, :] * buf[start:start + rb, :]
    y = _silu(acc)
    for h in range(N_GDN_HEADS):
        cols = slice(h * GDN_HEAD_DIM, (h + 1) * GDN_HEAD_DIM)
        qh = y[:, cols]
        kh = y[:, D_GDN + h * GDN_HEAD_DIM:D_GDN + (h + 1) * GDN_HEAD_DIM]
        q_ref[:, cols] = (qh * lax.rsqrt(jnp.sum(qh * qh, axis=-1, keepdims=True) + EPS)
                          * (GDN_HEAD_DIM ** -0.5))
        k_ref[:, cols] = kh * lax.rsqrt(jnp.sum(kh * kh, axis=-1, keepdims=True) + EPS)
    v_ref[...] = y[:, 2 * D_GDN:]
    rows = rows_ref[...]
    beta = _sigmoid(rows[4:8, :])
    a = rows[8:12, :] + dtb_ref[...]
    softplus = jnp.maximum(a, 0.0) + jnp.log1p(jnp.exp(-jnp.abs(a)))
    bg_ref[0:4, :] = beta
    bg_ref[4:8, :] = -jnp.exp(alog_ref[...]) * softplus


def _gdn_prep(proj, rows, lw, batch, tp, rb):
    n_rows = proj.shape[0]
    nb = tp // rb
    halo_per_block = rb // HALO_ROWS
    row_spec = pl.BlockSpec((rb, D_GDN), lambda b, i: (b * nb + i, 0))
    return pl.pallas_call(
        _gdn_prep_kernel,
        out_shape=(jax.ShapeDtypeStruct((n_rows, D_GDN), F32),) * 3
        + (jax.ShapeDtypeStruct((8, n_rows), F32),),
        grid=(batch, nb),
        in_specs=[pl.BlockSpec((rb, 3 * D_GDN), lambda b, i: (b * nb + i, 0)),
                  pl.BlockSpec((HALO_ROWS, 3 * D_GDN),
                               lambda b, i: (jnp.maximum((b * nb + i) * halo_per_block - 1, 0), 0)),
                  pl.BlockSpec((CONV_WIDTH, 3 * D_GDN), lambda b, i: (0, 0)),
                  pl.BlockSpec((16, rb), lambda b, i: (0, b * nb + i)),
                  pl.BlockSpec((N_GDN_HEADS, 1), lambda b, i: (0, 0)),
                  pl.BlockSpec((N_GDN_HEADS, 1), lambda b, i: (0, 0))],
        out_specs=(row_spec, row_spec, row_spec,
                   pl.BlockSpec((8, rb), lambda b, i: (0, b * nb + i))),
        scratch_shapes=[pltpu.VMEM((HALO_ROWS + rb, 3 * D_GDN), F32)],
        compiler_params=_params("parallel", "parallel"),
        name="gdn_prep",
    )(proj, proj, lw["conv_w"], rows, lw["a_log"], lw["dt_bias"])


def _gdn_chunk_kernel(q_ref, k_ref, v_ref, bg_ref, m_ref, n_ref, p_ref, r_ref, cd_ref):
    c = LANES
    n_chunks = q_ref.shape[0] // c
    items = [(ch, h) for ch in range(n_chunks) for h in range(N_GDN_HEADS)]
    idx = range(len(items))
    row = lax.broadcasted_iota(I32, (c, c), 0)
    col = lax.broadcasted_iota(I32, (c, c), 1)
    tri = row >= col
    strict = row > col
    eye = jnp.where(row == col, 1.0, 0.0)
    lane8 = lax.broadcasted_iota(I32, (8, c), 1)
    gates, decays = [], []
    for ch in range(n_chunks):
        bg = bg_ref[:, ch * c:(ch + 1) * c]
        dec = bg
        shift = 1
        while shift < c:
            dec = dec + jnp.where(lane8 >= shift, pltpu.roll(dec, shift, 1), 0.0)
            shift *= 2
        gates.append(bg)
        decays.append(dec)

    def tokens(ref, n):
        ch, h = items[n]
        return ref[ch * c:(ch + 1) * c, h * GDN_HEAD_DIM:(h + 1) * GDN_HEAD_DIM]

    d_row = [jnp.broadcast_to(decays[ch][4 + h:5 + h, :], (c, c)) for ch, h in items]
    d_col = [d.T for d in d_row]
    beta_col = [jnp.broadcast_to(gates[ch][h:h + 1, :], (c, c)).T for ch, h in items]
    d_last = [d[:, c - 1:c] for d in d_row]
    gamma = [jnp.exp(jnp.where(tri, d_col[n] - d_row[n], MASKED_LOGIT)) for n in idx]
    exp_d = [jnp.exp(d_col[n]) for n in idx]
    k16 = [tokens(k_ref, n).astype(BF16) for n in idx]
    kb = [tokens(k_ref, n) * beta_col[n] for n in idx]
    nil = [jnp.where(strict, _dot_nt(kb[n].astype(BF16), k16[n]) * gamma[n], 0.0) for n in idx]
    block = 8
    same = lambda size: (row // size) == (col // size)
    diag = [jnp.where(same(block), x, 0.0) for x in nil]
    diag_parts = [_split_bf16(x) for x in diag]
    inv = [eye - x for x in diag]
    power = [_dot_split(x, x) for x in diag_parts]
    for it in range(2):
        power_parts = [_split_bf16(x) for x in power]
        inv = [inv[n] + _dot_split(_split_bf16(inv[n]), power_parts[n]) for n in idx]
        if it == 0:
            power = [_dot_split(x, x) for x in power_parts]
    while block < c:
        couples = same(2 * block) & ((row // block) % 2 == 1) & ((col // block) % 2 == 0)
        inv_parts = [_split_bf16(x) for x in inv]
        lower = [_dot_split(_split_bf16(jnp.where(couples, nil[n], 0.0)), inv_parts[n]) for n in idx]
        inv = [inv[n] - _dot_split(inv_parts[n], _split_bf16(lower[n])) for n in idx]
        block *= 2
    rhs = [jnp.concatenate([kb[n] * exp_d[n], tokens(v_ref, n) * beta_col[n]], axis=1) for n in idx]
    wu = [_dot_split(_split_bf16(inv[n]), _split_bf16(rhs[n])).astype(BF16) for n in idx]
    aqk = [jnp.where(tri, _dot_nt(tokens(q_ref, n).astype(BF16), k16[n]) * gamma[n], 0.0).astype(BF16)
           for n in idx]
    kd_t = [(tokens(k_ref, n) * jnp.exp(d_last[n] - d_col[n])).T.astype(BF16) for n in idx]
    state_wu = [_dot(kd_t[n], wu[n]) for n in idx]
    out_wu = [_dot(aqk[n], wu[n]) for n in idx]
    for n, (ch, h) in enumerate(items):
        m_ref[ch, h] = (-state_wu[n][:, :c]).astype(BF16)
        n_ref[ch, h] = state_wu[n][:, c:]
        p_ref[ch, h] = (tokens(q_ref, n) * exp_d[n] - out_wu[n][:, :c]).astype(BF16)
        r_ref[ch, h] = out_wu[n][:, c:]
        cd_ref[ch, h:h + 1, :] = jnp.exp(d_last[n][0:1, :] + jnp.zeros((1, c), F32))


def _gdn_chunks(qn, kn, vv, bg):
    n_rows = qn.shape[0]
    nc = n_rows // LANES
    per_step = GDN_CHUNKS_PER_STEP
    tok = pl.BlockSpec((per_step * LANES, D_GDN), lambda c: (c, 0))
    mat = pl.BlockSpec((per_step, N_GDN_HEADS, LANES, LANES), lambda c: (c, 0, 0, 0))
    mat_shape = lambda dt: jax.ShapeDtypeStruct((nc, N_GDN_HEADS, LANES, LANES), dt)
    return pl.pallas_call(
        _gdn_chunk_kernel,
        out_shape=(mat_shape(BF16), mat_shape(F32), mat_shape(BF16), mat_shape(F32),
                   jax.ShapeDtypeStruct((nc, N_GDN_HEADS, LANES), F32)),
        grid=(nc // per_step,),
        in_specs=[tok, tok, tok, pl.BlockSpec((8, per_step * LANES), lambda c: (0, c))],
        out_specs=(mat, mat, mat, mat, pl.BlockSpec((per_step, N_GDN_HEADS, LANES), lambda c: (c, 0, 0))),
        compiler_params=_params("parallel"),
        name="gdn_chunks",
    )(qn, kn, vv, bg)


def _gdn_scan_kernel(m_ref, n_ref, p_ref, r_ref, cd_ref, o_ref, s_scr, *, batch):
    @pl.when(pl.program_id(0) == 0)
    def _():
        s_scr[...] = jnp.zeros(s_scr.shape, F32)

    for ch in range(m_ref.shape[1]):
        rows = slice(ch * LANES, (ch + 1) * LANES)
        for b in range(batch):
            for h in range(N_GDN_HEADS):
                s = s_scr[b, h]
                s16 = s.astype(BF16)
                o_ref[b, rows, h * GDN_HEAD_DIM:(h + 1) * GDN_HEAD_DIM] = _dot(p_ref[b, ch, h], s16) + r_ref[b, ch, h]
                s_scr[b, h] = s * cd_ref[b, ch, h:h + 1, :] + _dot(m_ref[b, ch, h], s16) + n_ref[b, ch, h]


def _gdn_scan(m_mat, n_mat, p_mat, r_mat, cd, batch, tp):
    nc = tp // LANES
    per_step = next(d for d in SCAN_CHUNKS if nc % d == 0)
    shape5 = lambda a: a.reshape(batch, nc, N_GDN_HEADS, LANES, LANES)
    mat = pl.BlockSpec((batch, per_step, N_GDN_HEADS, LANES, LANES), lambda c: (0, c, 0, 0, 0))
    return pl.pallas_call(
        functools.partial(_gdn_scan_kernel, batch=batch),
        out_shape=jax.ShapeDtypeStruct((batch, tp, D_GDN), F32),
        grid=(nc // per_step,),
        in_specs=[mat, mat, mat, mat,
                  pl.BlockSpec((batch, per_step, N_GDN_HEADS, LANES), lambda c: (0, c, 0, 0))],
        out_specs=pl.BlockSpec((batch, per_step * LANES, D_GDN), lambda c: (0, c, 0)),
        scratch_shapes=[pltpu.VMEM((batch, N_GDN_HEADS, LANES, LANES), F32)],
        compiler_params=_params("arbitrary"),
        name="gdn_scan",
    )(shape5(m_mat), shape5(n_mat), shape5(p_mat), shape5(r_mat), cd.reshape(batch, nc, N_GDN_HEADS, LANES))


def _out_proj_kernel(h_ref, oa_ref, za_ref, og_ref, zg_ref, gain_ref, w_ref, out_ref):
    attn = (oa_ref[...] * _silu(za_ref[...])).astype(BF16)
    y = _dot(attn, w_ref[0:D_ATTN, :])
    og, zg = og_ref[...], zg_ref[...]
    for h in range(N_GDN_HEADS):
        cols = slice(h * GDN_HEAD_DIM, (h + 1) * GDN_HEAD_DIM)
        gated = (_rms_rows(og[:, cols], gain_ref[...]) * _silu(zg[:, cols])).astype(BF16)
        y = y + _dot(gated, w_ref[D_ATTN + h * GDN_HEAD_DIM:D_ATTN + (h + 1) * GDN_HEAD_DIM, :])
    out_ref[...] = h_ref[...] + y


def _out_proj(h2, o_attn, proj, o_gdn, gain, w_out, batch, tp, first, rows):
    rb = _row_block(rows)
    n_blocks = rows // rb
    blk = lambda width, col: pl.BlockSpec((pl.Element(rb), pl.Element(width)),
                                          lambda b, j: (pl.multiple_of(b * tp + first + j * rb, HALO_ROWS), col * width))
    return pl.pallas_call(
        _out_proj_kernel,
        out_shape=jax.ShapeDtypeStruct((batch * rows, D_MODEL), F32),
        grid=(batch, n_blocks),
        in_specs=[blk(D_MODEL, 0), blk(D_ATTN, 0), blk(D_ATTN, COL_Z_ATTN), blk(D_GDN, 0),
                  blk(D_GDN, COL_Z_GDN),
                  pl.BlockSpec((1, GDN_HEAD_DIM), lambda b, j: (0, 0)),
                  pl.BlockSpec((D_MODEL, D_MODEL), lambda b, j: (0, 0))],
        out_specs=pl.BlockSpec((rb, D_MODEL), lambda b, j: (b * n_blocks + j, 0)),
        compiler_params=_params("parallel", "parallel"),
        name="out_proj",
    )(h2, o_attn, proj, o_gdn, proj, gain, w_out)


def _pack_layer(norm_gain, w_in, cq_gain, ckv_gain, w_uq, w_ukv, w_q_idx, q_gain, k_gain, conv_w, a_log,
                dt_bias, gdn_gain, w_out):
    o = 0
    parts = {}
    for name, size in (("c_q", Q_RANK), ("c_kv", KV_RANK), ("k_idx", IDX_DIM), ("w_idx", N_IDX_HEADS),
                       ("z_attn", D_ATTN), ("qkv_g", 3 * D_GDN), ("z_g", D_GDN), ("b", N_GDN_HEADS),
                       ("a", N_GDN_HEADS)):
        parts[name] = w_in[:, o:o + size]
        o += size
    w_packed = jnp.concatenate([parts["qkv_g"], parts["c_q"], parts["c_kv"], parts["k_idx"], parts["k_idx"],
                                parts["z_attn"], parts["z_g"]], axis=1).astype(BF16)
    w_rows = jnp.concatenate([parts["w_idx"], parts["b"], parts["a"],
                              jnp.zeros((D_MODEL, 4), w_in.dtype)], axis=1).T.astype(BF16)
    return dict(
        gain=norm_gain[None, :], w_packed=w_packed, w_rows=w_rows,
        g_cq=cq_gain[None, :], g_ckv=ckv_gain[None, :],
        w_uqt=w_uq.T.astype(BF16), w_qit=w_q_idx.T.astype(BF16),
        w_ukt=w_ukv[:, :D_ATTN].T.astype(BF16), w_uvt=w_ukv[:, D_ATTN:].T.astype(BF16),
        g_q_col=q_gain[:, None], g_k_col=k_gain[:, None],
        conv_w=conv_w, a_log=a_log[:, None], dt_bias=dt_bias[:, None],
        gdn_gain=gdn_gain[None, :], w_out=w_out.astype(BF16))


def _layer(h2, lw, bias, tri, batch, tp, t_valid, topk, keep):
    rb = _row_block(tp)
    proj, rows = _in_proj(h2, lw["gain"], lw["w_packed"], lw["w_rows"], rb)
    qt, k, vt, qit, kidx = _dsa_prep(proj, lw, batch, tp, rb)
    o_attn = _dsa_attention(qit, rows, qt, kidx, k, vt, bias, tri, batch, tp, t_valid, topk)
    qn, kn, vv, bg = _gdn_prep(proj, rows, lw, batch, tp, rb)
    m_mat, n_mat, p_mat, r_mat, cd = _gdn_chunks(qn, kn, vv, bg)
    o_gdn = _gdn_scan(m_mat, n_mat, p_mat, r_mat, cd, batch, tp).reshape(batch * tp, D_GDN)
    return _out_proj(h2, o_attn, proj, o_gdn, lw["gdn_gain"], lw["w_out"], batch, tp, *keep)


def _forward(x, meta_tokens, rel_bias_table, layer_weights, topk):
    batch, seq, _ = x.shape
    t = seq + N_META
    tp = -(-t // KEY_TILE) * KEY_TILE
    meta = jnp.broadcast_to(meta_tokens[None].astype(x.dtype), (batch, N_META, D_MODEL))
    h = jnp.concatenate([meta, x, jnp.zeros((batch, tp - t, D_MODEL), x.dtype)], axis=1)
    h2 = h.reshape(batch * tp, D_MODEL)
    bias = _bias_tiles(rel_bias_table)
    tri = jnp.tril(jnp.ones((KEY_TILE, KEY_TILE), BF16))
    crop = seq % LANES == 0
    for depth, lw in enumerate(layer_weights):
        last = crop and depth == len(layer_weights) - 1
        h2 = _layer(h2, lw, bias, tri, batch, tp, t, topk, (N_META, seq) if last else (0, tp))
    if crop:
        return h2.reshape(batch, seq, D_MODEL)
    return h2.reshape(batch, tp, D_MODEL)[:, N_META:t]


def kernel(x, meta_tokens, rel_bias_table, norm_gain, w_in, cq_norm_gain, ckv_norm_gain, w_uq, w_ukv, w_q_idx,
           q_norm_gain, k_norm_gain, conv_w, a_log, dt_bias, gdn_norm_gain, w_out):
    depth = norm_gain.shape[0]
    topk = min(TOPK_MAX, x.shape[1] // 4)
    layers = [_pack_layer(norm_gain[l], w_in[l], cq_norm_gain[l], ckv_norm_gain[l], w_uq[l], w_ukv[l],
                          w_q_idx[l], q_norm_gain[l], k_norm_gain[l], conv_w[l], a_log[l], dt_bias[l],
                          gdn_norm_gain[l], w_out[l]) for l in range(depth)]
    return _forward(x, meta_tokens, rel_bias_table, layers, topk)
```

```python
import functools
import math

import jax
import jax.numpy as jnp
from jax import lax
from jax.experimental import pallas as pl
from jax.experimental.pallas import tpu as pltpu

F32 = jnp.float32
BF16 = jnp.bfloat16
I32 = jnp.int32

D_MODEL = 1024
N_META = 16
EPS = 1e-6
N_ATTN_HEADS = 8
ATTN_HEAD_DIM = 64
D_ATTN = N_ATTN_HEADS * ATTN_HEAD_DIM
Q_RANK = 256
KV_RANK = 128
N_IDX_HEADS = 4
IDX_DIM = 64
TOPK_MAX = 256
N_REL_BUCKETS = 32
REL_MAX_DIST = 128
N_GDN_HEADS = 4
GDN_HEAD_DIM = 128
D_GDN = N_GDN_HEADS * GDN_HEAD_DIM
CONV_WIDTH = 4

LANES = 128
KEY_TILE = 256
COUNT_TILE = 512
SCAN_TILE = 1024
FIRST_CHECK_PASSES = 10
INTERPOLATED_PASSES = 24
MAX_SEARCH_PASSES = INTERPOLATED_PASSES + 32
GDN_CHUNKS_PER_STEP = 2
SCAN_CHUNKS = (6, 4, 3, 2, 1)
ROW_TILES = (6, 5, 4, 3, 2, 1)
HALO_ROWS = 8
D_PACKED = 3 * D_GDN + 512 + D_ATTN + D_GDN
COL_SMALL = 3
COL_Z_ATTN = 4
COL_Z_GDN = 5
MASKED_LOGIT = -1e30
LOG2_E = math.log2(math.e)
KEY_MIN = -2 ** 31
PATTERN_NEG_FLT_MAX = KEY_MIN + (1 << 23)
VMEM_LIMIT = 56 * 1024 * 1024

NT_DIMS = (((1,), (1,)), ((), ()))


def _dot(a, b):
    return jnp.dot(a, b, preferred_element_type=F32)


def _dot_nt(a, b):
    return lax.dot_general(a, b, NT_DIMS, preferred_element_type=F32)


def _split_bf16(x):
    hi = x.astype(BF16)
    return hi, (x - hi.astype(F32)).astype(BF16)


def _dot_split(a_parts, b_parts):
    (a_hi, a_lo), (b_hi, b_lo) = a_parts, b_parts
    return _dot(a_hi, b_hi) + (_dot(a_hi, b_lo) + _dot(a_lo, b_hi))


def _sigmoid(x):
    return 1.0 / (1.0 + jnp.exp(-x))


def _silu(x):
    return x * _sigmoid(x)


def _row_block(tp):
    tiles = tp // LANES
    return LANES * next(d for d in ROW_TILES if tiles % d == 0)


def _params(*sem):
    return pltpu.CompilerParams(dimension_semantics=sem, vmem_limit_bytes=VMEM_LIMIT)


def _bias_kernel(table_ref, out_ref):
    row = lax.broadcasted_iota(I32, (LANES, LANES), 0)
    col = lax.broadcasted_iota(I32, (LANES, LANES), 1)
    max_exact = N_REL_BUCKETS // 2
    for kind in range(3):
        dist = col - row + (2 - kind) * LANES
        n = jnp.maximum(dist, 0)
        nf = jnp.maximum(n, 1).astype(F32)
        large = max_exact + (jnp.log(nf / max_exact) / math.log(REL_MAX_DIST / max_exact)
                             * (N_REL_BUCKETS - max_exact)).astype(I32)
        large = jnp.minimum(large, N_REL_BUCKETS - 1)
        bucket = jnp.where(n < max_exact, n, large)
        for h in range(N_ATTN_HEADS):
            tile = jnp.zeros((LANES, LANES), F32)
            for b in range(N_REL_BUCKETS):
                tile = jnp.where(bucket == b, table_ref[b, h], tile)
            far = table_ref[N_REL_BUCKETS - 1, h]
            out_ref[kind, :, h * LANES:(h + 1) * LANES] = (tile - far) * LOG2_E


def _bias_tiles(rel_table):
    return pl.pallas_call(
        _bias_kernel,
        out_shape=jax.ShapeDtypeStruct((3, LANES, N_ATTN_HEADS * LANES), F32),
        in_specs=[pl.BlockSpec(memory_space=pltpu.SMEM)],
        out_specs=pl.BlockSpec(memory_space=pltpu.VMEM),
        name="rel_bias_tiles",
    )(rel_table)


def _in_proj_kernel(h_ref, gain_ref, w_ref, wrows_ref, proj_ref, rows_ref):
    x = h_ref[...]
    y = x * lax.rsqrt(jnp.mean(x * x, axis=-1, keepdims=True) + EPS)
    hn = (y * gain_ref[...]).astype(BF16)
    proj_ref[...] = _dot(hn, w_ref[...])
    rows_ref[...] = _dot_nt(wrows_ref[...], hn)


def _in_proj(h2, gain, w_packed, w_rows, rb):
    n_rows = h2.shape[0]
    grid = (n_rows // rb,)
    return pl.pallas_call(
        _in_proj_kernel,
        out_shape=(jax.ShapeDtypeStruct((n_rows, D_PACKED), F32),
                   jax.ShapeDtypeStruct((16, n_rows), F32)),
        grid=grid,
        in_specs=[pl.BlockSpec((rb, D_MODEL), lambda i: (i, 0)),
                  pl.BlockSpec((1, D_MODEL), lambda i: (0, 0)),
                  pl.BlockSpec((D_MODEL, D_PACKED), lambda i: (0, 0)),
                  pl.BlockSpec((16, D_MODEL), lambda i: (0, 0))],
        out_specs=(pl.BlockSpec((rb, D_PACKED), lambda i: (i, 0)),
                   pl.BlockSpec((16, rb), lambda i: (0, i))),
        compiler_params=_params("parallel"),
        name="in_proj",
    )(h2, gain, w_packed, w_rows)


def _rms_rows(x, gain):
    return x * lax.rsqrt(jnp.mean(x * x, axis=-1, keepdims=True) + EPS) * gain


def _dsa_prep_kernel(sm_ref, gcq_ref, gckv_ref, wuqt_ref, wqit_ref, wukt_ref, wuvt_ref, gq_ref, gk_ref,
                     qt_ref, k_ref, vt_ref, qit_ref, kidx_ref):
    sm = sm_ref[...]
    rb = sm.shape[0]
    cq = _rms_rows(sm[:, :Q_RANK], gcq_ref[...]).astype(BF16)
    ckv = _rms_rows(sm[:, Q_RANK:Q_RANK + KV_RANK], gckv_ref[...]).astype(BF16)
    kidx_ref[...] = sm[:, Q_RANK + KV_RANK:].astype(BF16)
    q3 = _dot_nt(wuqt_ref[...], cq).reshape(N_ATTN_HEADS, ATTN_HEAD_DIM, rb)
    q3 = q3 * lax.rsqrt(jnp.mean(q3 * q3, axis=1, keepdims=True) + EPS) * gq_ref[...][None]
    qt_ref[...] = (q3 * (ATTN_HEAD_DIM ** -0.5 * LOG2_E)).reshape(D_ATTN, rb).astype(BF16)
    k3 = _dot_nt(wukt_ref[...], ckv).reshape(N_ATTN_HEADS, ATTN_HEAD_DIM, rb)
    k3 = k3 * lax.rsqrt(jnp.mean(k3 * k3, axis=1, keepdims=True) + EPS) * gk_ref[...][None]
    k_ref[...] = k3.reshape(D_ATTN, rb).T.astype(BF16)
    vt_ref[...] = _dot_nt(wuvt_ref[...], ckv).astype(BF16)
    qit_ref[...] = _dot_nt(wqit_ref[...], cq).astype(BF16)


def _dsa_prep(proj, lw, batch, tp, rb):
    n_rows = proj.shape[0]
    nb = tp // rb
    d_idx = N_IDX_HEADS * IDX_DIM
    const = lambda shape: pl.BlockSpec(shape, lambda b, i: (0, 0))
    row_spec = lambda width: pl.BlockSpec((rb, width), lambda b, i: (b * nb + i, 0))
    col_spec = lambda height: pl.BlockSpec((None, height, rb), lambda b, i: (b, 0, i))
    return pl.pallas_call(
        _dsa_prep_kernel,
        out_shape=(jax.ShapeDtypeStruct((batch, D_ATTN, tp), BF16),
                   jax.ShapeDtypeStruct((n_rows, D_ATTN), BF16),
                   jax.ShapeDtypeStruct((batch, D_ATTN, tp), BF16),
                   jax.ShapeDtypeStruct((batch, d_idx, tp), BF16),
                   jax.ShapeDtypeStruct((n_rows, LANES), BF16)),
        grid=(batch, nb),
        in_specs=[pl.BlockSpec((rb, 512), lambda b, i: (b * nb + i, COL_SMALL)),
                  const((1, Q_RANK)), const((1, KV_RANK)),
                  const((D_ATTN, Q_RANK)), const((d_idx, Q_RANK)),
                  const((D_ATTN, KV_RANK)), const((D_ATTN, KV_RANK)),
                  const((ATTN_HEAD_DIM, 1)), const((ATTN_HEAD_DIM, 1))],
        out_specs=(col_spec(D_ATTN), row_spec(D_ATTN), col_spec(D_ATTN), col_spec(d_idx), row_spec(LANES)),
        compiler_params=_params("parallel", "parallel"),
        name="dsa_prep",
    )(proj, lw["g_cq"], lw["g_ckv"], lw["w_uqt"], lw["w_qit"], lw["w_ukt"], lw["w_uvt"], lw["g_q_col"],
      lw["g_k_col"])


def _dsa_block(i, qit_ref, rows_ref, qt_ref, kidx_ref, k_ref, vt_ref, bias_ref, tri_ref, o_ref,
               score_scr, wi_scr, wq_scr, s_scr, p_scr, m_scr, l_scr, acc_scr, alpha_scr, mask_scr, tie_scr, *, topk):
    t0 = i * LANES
    n_kt = i // 2 + 1
    n_st = (n_kt + 3) // 4
    hd = ATTN_HEAD_DIM
    pair_w = 2 * LANES
    n_pairs = N_ATTN_HEADS // 2

    zeros_hd = jnp.zeros((hd, LANES), BF16)
    for h in range(N_IDX_HEADS):
        wi_scr[0:IDX_DIM, h * LANES:(h + 1) * LANES] = qit_ref[h * IDX_DIM:(h + 1) * IDX_DIM, :]
    wi_scr[IDX_DIM:, :] = jnp.zeros((LANES - IDX_DIM, N_IDX_HEADS * LANES), BF16)
    for p in range(n_pairs):
        wq_scr[p, 0:hd, 0:LANES] = qt_ref[2 * p * hd:(2 * p + 1) * hd, :]
        wq_scr[p, 0:hd, LANES:] = zeros_hd
        wq_scr[p, hd:, 0:LANES] = zeros_hd
        wq_scr[p, hd:, LANES:] = qt_ref[(2 * p + 1) * hd:(2 * p + 2) * hd, :]

    row = lax.broadcasted_iota(I32, (KEY_TILE, LANES), 0)
    col = lax.broadcasted_iota(I32, (KEY_TILE, LANES), 1)
    w_idx = rows_ref[0:N_IDX_HEADS, :] * (N_IDX_HEADS ** -0.5 * IDX_DIM ** -0.5)

    def key_tile(j):
        return pl.multiple_of(j * KEY_TILE, KEY_TILE)

    def causal(j):
        return (j * KEY_TILE + row) <= (t0 + col)

    def fold8(x, op):
        return op(x.reshape(KEY_TILE // 8, 8, LANES), axis=0)

    def score_step(jc, carry, masked):
        top, bottom, n_nonneg, n_pos = carry
        subs = [jc * (SCAN_TILE // KEY_TILE) + sub for sub in range(SCAN_TILE // KEY_TILE)]
        logits = [_dot(kidx_ref[pl.ds(key_tile(jnp.minimum(j, n_kt - 1)), KEY_TILE), :], wi_scr[...])
                  for j in subs]
        for j, lg in zip(subs, logits):
            score = jnp.zeros((KEY_TILE, LANES), F32)
            for h in range(N_IDX_HEADS):
                score = score + jnp.maximum(lg[:, h * LANES:(h + 1) * LANES], 0.0) * w_idx[h:h + 1, :]
            if masked:
                visible = causal(j)
                seen = jnp.where(visible, score, -jnp.inf)
                bottom = jnp.minimum(bottom, fold8(jnp.where(visible, score, jnp.inf), jnp.min))
            else:
                seen = score
                bottom = jnp.minimum(bottom, fold8(score, jnp.min))
            top = jnp.maximum(top, fold8(seen, jnp.max))
            n_nonneg = n_nonneg + fold8(jnp.where(seen >= 0.0, 1, 0), jnp.sum)
            n_pos = n_pos + fold8(jnp.where(seen > 0.0, 1, 0), jnp.sum)
            score_scr[pl.ds(key_tile(j), KEY_TILE), :] = seen
        return top, bottom, n_nonneg, n_pos

    zeros8 = jnp.zeros((8, LANES), I32)
    carry = (jnp.full((8, LANES), -jnp.inf, F32), jnp.full((8, LANES), jnp.inf, F32), zeros8, zeros8)
    carry = lax.fori_loop(0, n_st - 1, functools.partial(score_step, masked=False), carry)
    top, bottom, n_nonneg, n_pos = score_step(n_st - 1, carry, masked=True)
    top = jnp.max(top, axis=0, keepdims=True)
    bottom = jnp.min(bottom, axis=0, keepdims=True)
    count0 = jnp.sum(n_nonneg, axis=0, keepdims=True)
    count_pos = jnp.sum(n_pos, axis=0, keepdims=True)

    def count_f32(cand, below=None):
        def body(j, carry):
            acc, best = carry
            for part in range(SCAN_TILE // COUNT_TILE):
                start = pl.multiple_of(j * SCAN_TILE + part * COUNT_TILE, COUNT_TILE)
                x = score_scr[pl.ds(start, COUNT_TILE), :]
                ind = jnp.where(x >= cand, 1, 0)
                acc = acc + jnp.sum(ind.reshape(COUNT_TILE // 8, 8, LANES), axis=0)
                if below is not None:
                    under = jnp.where(x < below, x, -jnp.inf)
                    best = jnp.maximum(best, jnp.max(under.reshape(COUNT_TILE // 8, 8, LANES), axis=0))
            return acc, best
        acc, best = lax.fori_loop(0, n_st, body, (jnp.zeros((8, LANES), I32), jnp.full((8, LANES), -jnp.inf, F32)))
        count = jnp.sum(acc, axis=0, keepdims=True)
        if below is None:
            return count
        return count, jnp.max(best, axis=0, keepdims=True)

    def to_pattern(v):
        bits = lax.bitcast_convert_type(v, I32)
        return bits ^ ((bits >> 31) & 0x7FFFFFFF)

    def to_f32(c):
        return lax.bitcast_convert_type(c ^ ((c >> 31) & 0x7FFFFFFF), F32)

    n_visible = t0 + 1 + lax.broadcasted_iota(I32, (1, LANES), 1)
    nonneg = count0 >= topk
    lo = jnp.where(nonneg, 0, to_pattern(bottom))
    hi = jnp.where(nonneg, to_pattern(top) + 1, 0)
    count_lo = jnp.where(nonneg, count0, n_visible)
    count_hi = jnp.where(nonneg, 0, count0)
    few = n_visible < topk
    lo = jnp.where(few, PATTERN_NEG_FLT_MAX, lo)
    zero_tied = nonneg & (count_pos < topk)
    hi = jnp.where(zero_tied, 1, hi)
    count_hi = jnp.where(zero_tied, count_pos, count_hi)
    open_q = jnp.where(few | zero_tied | (count_lo == topk), 0, 1)

    log_topk = math.log(topk)

    def count_error(count):
        return jnp.log(count.astype(F32) + 0.5) - log_topk

    def probe(n_pass, carry, extract):
        lo, hi, count_lo, count_hi, err_lo, err_hi, last_side, open_q = carry
        v_lo, v_hi = to_f32(lo), to_f32(hi)
        frac = err_lo / (err_lo - err_hi)
        frac = jnp.where(count_lo - count_hi <= 4, 0.5, frac)
        guess = to_pattern(v_lo + (v_hi - v_lo) * frac)
        middle = lo + lax.shift_right_logical(hi - lo, 1)
        cand = jnp.where(n_pass >= INTERPOLATED_PASSES, middle, guess)
        cand = jnp.minimum(jnp.maximum(cand, lo + 1), hi - 1)
        is_open = open_q == 1
        if extract:
            count, under_hi = count_f32(to_f32(cand), below=v_hi)
            next_below = to_pattern(under_hi)
            found = is_open & (count_hi == topk - 1)
            is_open = is_open & jnp.logical_not(found)
        else:
            count = count_f32(to_f32(cand))
        raise_lo = is_open & (count >= topk)
        lower_hi = is_open & (count < topk)
        err = count_error(count)
        err_hi = jnp.where(raise_lo & (last_side == 1), err_hi * 0.5, err_hi)
        err_lo = jnp.where(lower_hi & (last_side == -1), err_lo * 0.5, err_lo)
        err_lo = jnp.where(raise_lo, err, err_lo)
        err_hi = jnp.where(lower_hi, err, err_hi)
        lo = jnp.where(raise_lo, cand, lo)
        count_lo = jnp.where(raise_lo, count, count_lo)
        hi = jnp.where(lower_hi, cand, hi)
        count_hi = jnp.where(lower_hi, count, count_hi)
        last_side = jnp.where(raise_lo, 1, jnp.where(lower_hi, -1, last_side))
        if extract:
            lo = jnp.where(found, next_below, lo)
            hi = jnp.where(found | raise_lo, next_below + 1, hi)
        closed = (count_lo == topk) | (hi - lo == 1)
        return lo, hi, count_lo, count_hi, err_lo, err_hi, last_side, jnp.where(closed, 0, open_q)

    carry = (lo, hi, count_lo, count_hi, count_error(count_lo), count_error(count_hi),
             jnp.zeros((1, LANES), I32), open_q)
    n_first = jnp.where(jnp.sum(open_q) > 0, FIRST_CHECK_PASSES, 0)
    carry = lax.fori_loop(0, n_first, functools.partial(probe, extract=False), carry)

    def probes_left(st):
        n_pass, n_open = st[0], st[-1]
        return (n_pass < MAX_SEARCH_PASSES) & (n_open > 0)

    def extracting_probe(st):
        carry = probe(st[0], st[1:-1], extract=True)
        return (st[0] + 1,) + carry + (jnp.sum(carry[-1]),)

    state = lax.while_loop(probes_left, extracting_probe,
                           (jnp.int32(FIRST_CHECK_PASSES),) + carry + (jnp.sum(carry[-1]),))
    lo, count_lo, count_hi = state[1], state[3], state[4]
    tau = to_f32(lo)
    need = jnp.where((count_lo == topk) | few, topk, topk - count_hi).astype(F32)

    m_scr[...] = jnp.full(m_scr.shape, MASKED_LOGIT, F32)
    l_scr[...] = jnp.zeros(l_scr.shape, F32)
    acc_scr[...] = jnp.zeros(acc_scr.shape, F32)
    ones_rows = jnp.ones((16, KEY_TILE), BF16)
    last = n_kt - 1

    def mask_pair(j_first, parity):
        xs = [score_scr[pl.ds(key_tile(jnp.minimum(j_first + slot, last)), KEY_TILE), :] for slot in range(2)]
        ties = [x == tau for x in xs]
        tie_cols = jnp.concatenate([jnp.where(tie, 1.0, 0.0).astype(BF16) for tie in ties], axis=1)
        ranks = _dot(tri_ref[...], tie_cols)
        tie_carry = tie_scr[0:1, :]
        for slot, (x, tie) in enumerate(zip(xs, ties)):
            rank = ranks[:, slot * LANES:(slot + 1) * LANES] + tie_carry
            tie_carry = rank[KEY_TILE - 1:KEY_TILE, :]
            take = (tie & (rank <= need)) | (x > tau)
            mask_scr[parity, slot] = jnp.where(take, 0.0, MASKED_LOGIT)
        tie_scr[0:1, :] = tie_carry

    def qk_pair(slot, j, p):
        s_scr[slot, :, p * pair_w:(p + 1) * pair_w] = _dot(
            k_ref[pl.ds(key_tile(j), KEY_TILE), p * LANES:(p + 1) * LANES], wq_scr[p])

    def softmax_pair(slot, j, p, near, parity):
        mask_add = mask_scr[parity, slot]
        alphas = []
        for h in (2 * p, 2 * p + 1):
            cols = slice(h * LANES, (h + 1) * LANES)
            logits = s_scr[slot, :, cols] + mask_add
            if near:
                kind_top = jnp.clip(2 * j - i + 2, 0, 2)
                kind_bot = jnp.clip(2 * j - i + 3, 0, 2)
                logits = logits + jnp.concatenate(
                    [bias_ref[kind_top, :, cols], bias_ref[kind_bot, :, cols]], axis=0)
            m_old = m_scr[h:h + 1, :]
            m_new = jnp.maximum(m_old, jnp.max(logits, axis=0, keepdims=True))
            m_scr[h:h + 1, :] = m_new
            p_scr[slot, :, cols] = jnp.exp2(logits - m_new).astype(BF16)
            alphas.append(jnp.exp2(m_old - m_new))
        return alphas

    def pv_pair(slot, j, p, alphas):
        lhs = jnp.concatenate([vt_ref[p * 2 * hd:(p + 1) * 2 * hd, pl.ds(key_tile(j), KEY_TILE)], ones_rows],
                              axis=0)
        out = _dot(lhs, p_scr[slot, :, p * pair_w:(p + 1) * pair_w])
        for half in range(2):
            h = 2 * p + half
            rows_h = slice(h * hd, (h + 1) * hd)
            q_cols = slice(half * LANES, (half + 1) * LANES)
            acc_scr[rows_h, :] = acc_scr[rows_h, :] * alphas[half] + out[half * hd:(half + 1) * hd, q_cols]
            l_scr[h:h + 1, :] = l_scr[h:h + 1, :] * alphas[half] + out[2 * hd:2 * hd + 1, q_cols]

    def pending_alphas(p):
        return [alpha_scr[h:h + 1, :] for h in (2 * p, 2 * p + 1)]

    def clear_pending():
        p_scr[1] = jnp.zeros(p_scr.shape[1:], BF16)
        alpha_scr[...] = jnp.ones(alpha_scr.shape, F32)

    def pair_step(ja, j_pending, j_next, near, parity):
        alphas_a = []
        for p in range(n_pairs):
            pv_pair(1, j_pending, p, pending_alphas(p))
            qk_pair(1, ja + 1, p)
            alphas_a.append(softmax_pair(0, ja, p, near, parity))
        for p in range(n_pairs):
            pv_pair(0, ja, p, alphas_a[p])
            qk_pair(0, j_next, p)
            alphas_b = softmax_pair(1, ja + 1, p, near, parity)
            for half in range(2):
                alpha_scr[2 * p + half:2 * p + half + 1, :] = alphas_b[half]
        mask_pair(ja + 2, 1 - parity)

    def single_step(ja, j_pending, near, parity):
        alphas_a = []
        for p in range(n_pairs):
            pv_pair(1, j_pending, p, pending_alphas(p))
            alphas_a.append(softmax_pair(0, ja, p, near, parity))
        for p in range(n_pairs):
            pv_pair(0, ja, p, alphas_a[p])
        clear_pending()

    n_far = 2 * (jnp.maximum(n_kt - 2, 0) // 2)
    n_near = n_kt - n_far
    first_near_parity = (n_far // 2) % 2
    clear_pending()
    tie_scr[...] = jnp.zeros(tie_scr.shape, F32)
    mask_pair(0, 0)
    for p in range(n_pairs):
        qk_pair(0, 0, p)

    def far_body(jp, carry):
        pair_step(2 * jp, jnp.maximum(2 * jp - 1, 0), 2 * jp + 2, near=False, parity=jp % 2)
        return carry

    lax.fori_loop(0, n_far // 2, far_body, 0)

    @pl.when(n_near >= 2)
    def _():
        pair_step(n_far, jnp.maximum(n_far - 1, 0), jnp.minimum(n_far + 2, last), near=True,
                  parity=first_near_parity)

    @pl.when(n_near % 2 == 1)
    def _():
        single_step(last, jnp.where(n_near == 3, n_far + 1, jnp.maximum(n_far - 1, 0)), near=True,
                    parity=jnp.where(n_near == 3, 1 - first_near_parity, first_near_parity))

    for p in range(n_pairs):
        pv_pair(1, last, p, pending_alphas(p))

    for h in range(N_ATTN_HEADS):
        rows_h = slice(h * hd, (h + 1) * hd)
        acc_scr[rows_h, :] = acc_scr[rows_h, :] / l_scr[h:h + 1, :]
    o_ref[...] = acc_scr[...].T


def _dsa_kernel(*refs, topk, t_valid):
    o_ref = refs[8]
    i = pl.program_id(1)
    is_real = i * LANES < t_valid

    @pl.when(is_real)
    def _():
        _dsa_block(i, *refs, topk=topk)

    @pl.when(jnp.logical_not(is_real))
    def _():
        o_ref[...] = jnp.zeros(o_ref.shape, F32)


def _dsa_attention(qit, rows, qt, kidx, k, vt, bias, tri, batch, tp, t_valid, topk):
    n_rows = k.shape[0]
    nqb = tp // LANES
    d_idx = N_IDX_HEADS * IDX_DIM
    n_pairs = N_ATTN_HEADS // 2
    key_rows = -(-tp // SCAN_TILE) * SCAN_TILE
    q_cols = lambda height: pl.BlockSpec((None, height, LANES), lambda b, i: (b, 0, i))
    return pl.pallas_call(
        functools.partial(_dsa_kernel, topk=topk, t_valid=t_valid),
        out_shape=jax.ShapeDtypeStruct((n_rows, D_ATTN), F32),
        grid=(batch, nqb),
        in_specs=[q_cols(d_idx),
                  pl.BlockSpec((16, LANES), lambda b, i: (0, b * nqb + i)),
                  q_cols(D_ATTN),
                  pl.BlockSpec((tp, LANES), lambda b, i: (b, 0)),
                  pl.BlockSpec((tp, D_ATTN), lambda b, i: (b, 0)),
                  pl.BlockSpec((None, D_ATTN, tp), lambda b, i: (b, 0, 0)),
                  pl.BlockSpec((3, LANES, N_ATTN_HEADS * LANES), lambda b, i: (0, 0, 0)),
                  pl.BlockSpec((KEY_TILE, KEY_TILE), lambda b, i: (0, 0))],
        out_specs=pl.BlockSpec((LANES, D_ATTN), lambda b, i: (b * nqb + i, 0)),
        scratch_shapes=[pltpu.VMEM((key_rows, LANES), F32),
                        pltpu.VMEM((LANES, N_IDX_HEADS * LANES), BF16),
                        pltpu.VMEM((n_pairs, LANES, 2 * LANES), BF16),
                        pltpu.VMEM((2, KEY_TILE, N_ATTN_HEADS * LANES), F32),
                        pltpu.VMEM((2, KEY_TILE, N_ATTN_HEADS * LANES), BF16),
                        pltpu.VMEM((N_ATTN_HEADS, LANES), F32),
                        pltpu.VMEM((N_ATTN_HEADS, LANES), F32),
                        pltpu.VMEM((D_ATTN, LANES), F32),
                        pltpu.VMEM((N_ATTN_HEADS, LANES), F32),
                        pltpu.VMEM((2, 2, KEY_TILE, LANES), F32),
                        pltpu.VMEM((8, LANES), F32)],
        compiler_params=_params("parallel", "parallel"),
        name="dsa_attention",
    )(qit, rows, qt, kidx, k, vt, bias, tri)


def _gdn_prep_kernel(x_ref, halo_ref, cw_ref, rows_ref, alog_ref, dtb_ref, q_ref, k_ref, v_ref, bg_ref, buf):
    first = pl.program_id(1) == 0
    rb = x_ref.shape[0]
    buf[0:HALO_ROWS, :] = jnp.where(first, 0.0, halo_ref[...])
    buf[HALO_ROWS:, :] = x_ref[...]
    acc = jnp.zeros((rb, 3 * D_GDN), F32)
    for tap in range(CONV_WIDTH):
        start = HALO_ROWS - (CONV_WIDTH - 1) + tap
        acc = acc + cw_ref[tap:tap + 1, :] * buf[start:start + rb, :]
    y = _silu(acc)
    for h in range(N_GDN_HEADS):
        cols = slice(h * GDN_HEAD_DIM, (h + 1) * GDN_HEAD_DIM)
        qh = y[:, cols]
        kh = y[:, D_GDN + h * GDN_HEAD_DIM:D_GDN + (h + 1) * GDN_HEAD_DIM]
        q_ref[:, cols] = (qh * lax.rsqrt(jnp.sum(qh * qh, axis=-1, keepdims=True) + EPS)
                          * (GDN_HEAD_DIM ** -0.5))
        k_ref[:, cols] = kh * lax.rsqrt(jnp.sum(kh * kh, axis=-1, keepdims=True) + EPS)
    v_ref[...] = y[:, 2 * D_GDN:]
    rows = rows_ref[...]
    beta = _sigmoid(rows[4:8, :])
    a = rows[8:12, :] + dtb_ref[...]
    softplus = jnp.maximum(a, 0.0) + jnp.log1p(jnp.exp(-jnp.abs(a)))
    bg_ref[0:4, :] = beta
    bg_ref[4:8, :] = -jnp.exp(alog_ref[...]) * softplus


def _gdn_prep(proj, rows, lw, batch, tp, rb):
    n_rows = proj.shape[0]
    nb = tp // rb
    halo_per_block = rb // HALO_ROWS
    row_spec = pl.BlockSpec((rb, D_GDN), lambda b, i: (b * nb + i, 0))
    return pl.pallas_call(
        _gdn_prep_kernel,
        out_shape=(jax.ShapeDtypeStruct((n_rows, D_GDN), F32),) * 3
        + (jax.ShapeDtypeStruct((8, n_rows), F32),),
        grid=(batch, nb),
        in_specs=[pl.BlockSpec((rb, 3 * D_GDN), lambda b, i: (b * nb + i, 0)),
                  pl.BlockSpec((HALO_ROWS, 3 * D_GDN),
                               lambda b, i: (jnp.maximum((b * nb + i) * halo_per_block - 1, 0), 0)),
                  pl.BlockSpec((CONV_WIDTH, 3 * D_GDN), lambda b, i: (0, 0)),
                  pl.BlockSpec((16, rb), lambda b, i: (0, b * nb + i)),
                  pl.BlockSpec((N_GDN_HEADS, 1), lambda b, i: (0, 0)),
                  pl.BlockSpec((N_GDN_HEADS, 1), lambda b, i: (0, 0))],
        out_specs=(row_spec, row_spec, row_spec,
                   pl.BlockSpec((8, rb), lambda b, i: (0, b * nb + i))),
        scratch_shapes=[pltpu.VMEM((HALO_ROWS + rb, 3 * D_GDN), F32)],
        compiler_params=_params("parallel", "parallel"),
        name="gdn_prep",
    )(proj, proj, lw["conv_w"], rows, lw["a_log"], lw["dt_bias"])


def _gdn_chunk_kernel(q_ref, k_ref, v_ref, bg_ref, m_ref, n_ref, p_ref, r_ref, cd_ref):
    c = LANES
    n_chunks = q_ref.shape[0] // c
    items = [(ch, h) for ch in range(n_chunks) for h in range(N_GDN_HEADS)]
    idx = range(len(items))
    row = lax.broadcasted_iota(I32, (c, c), 0)
    col = lax.broadcasted_iota(I32, (c, c), 1)
    tri = row >= col
    strict = row > col
    eye = jnp.where(row == col, 1.0, 0.0)
    lane8 = lax.broadcasted_iota(I32, (8, c), 1)
    gates, decays = [], []
    for ch in range(n_chunks):
        bg = bg_ref[:, ch * c:(ch + 1) * c]
        dec = bg
        shift = 1
        while shift < c:
            dec = dec + jnp.where(lane8 >= shift, pltpu.roll(dec, shift, 1), 0.0)
            shift *= 2
        gates.append(bg)
        decays.append(dec)

    def tokens(ref, n):
        ch, h = items[n]
        return ref[ch * c:(ch + 1) * c, h * GDN_HEAD_DIM:(h + 1) * GDN_HEAD_DIM]

    d_row = [jnp.broadcast_to(decays[ch][4 + h:5 + h, :], (c, c)) for ch, h in items]
    d_col = [d.T for d in d_row]
    beta_col = [jnp.broadcast_to(gates[ch][h:h + 1, :], (c, c)).T for ch, h in items]
    d_last = [d[:, c - 1:c] for d in d_row]
    gamma = [jnp.exp(jnp.where(tri, d_col[n] - d_row[n], MASKED_LOGIT)) for n in idx]
    exp_d = [jnp.exp(d_col[n]) for n in idx]
    k16 = [tokens(k_ref, n).astype(BF16) for n in idx]
    kb = [tokens(k_ref, n) * beta_col[n] for n in idx]
    nil = [jnp.where(strict, _dot_nt(kb[n].astype(BF16), k16[n]) * gamma[n], 0.0) for n in idx]
    block = 8
    same = lambda size: (row // size) == (col // size)
    diag = [jnp.where(same(block), x, 0.0) for x in nil]
    diag_parts = [_split_bf16(x) for x in diag]
    inv = [eye - x for x in diag]
    power = [_dot_split(x, x) for x in diag_parts]
    for it in range(2):
        power_parts = [_split_bf16(x) for x in power]
        inv = [inv[n] + _dot_split(_split_bf16(inv[n]), power_parts[n]) for n in idx]
        if it == 0:
            power = [_dot_split(x, x) for x in power_parts]
    while block < c:
        couples = same(2 * block) & ((row // block) % 2 == 1) & ((col // block) % 2 == 0)
        inv_parts = [_split_bf16(x) for x in inv]
        lower = [_dot_split(_split_bf16(jnp.where(couples, nil[n], 0.0)), inv_parts[n]) for n in idx]
        inv = [inv[n] - _dot_split(inv_parts[n], _split_bf16(lower[n])) for n in idx]
        block *= 2
    rhs = [jnp.concatenate([kb[n] * exp_d[n], tokens(v_ref, n) * beta_col[n]], axis=1) for n in idx]
    wu = [_dot_split(_split_bf16(inv[n]), _split_bf16(rhs[n])).astype(BF16) for n in idx]
    aqk = [jnp.where(tri, _dot_nt(tokens(q_ref, n).astype(BF16), k16[n]) * gamma[n], 0.0).astype(BF16)
           for n in idx]
    kd_t = [(tokens(k_ref, n) * jnp.exp(d_last[n] - d_col[n])).T.astype(BF16) for n in idx]
    state_wu = [_dot(kd_t[n], wu[n]) for n in idx]
    out_wu = [_dot(aqk[n], wu[n]) for n in idx]
    for n, (ch, h) in enumerate(items):
        m_ref[ch, h] = (-state_wu[n][:, :c]).astype(BF16)
        n_ref[ch, h] = state_wu[n][:, c:]
        p_ref[ch, h] = (tokens(q_ref, n) * exp_d[n] - out_wu[n][:, :c]).astype(BF16)
        r_ref[ch, h] = out_wu[n][:, c:]
        cd_ref[ch, h:h + 1, :] = jnp.exp(d_last[n][0:1, :] + jnp.zeros((1, c), F32))


def _gdn_chunks(qn, kn, vv, bg):
    n_rows = qn.shape[0]
    nc = n_rows // LANES
    per_step = GDN_CHUNKS_PER_STEP
    tok = pl.BlockSpec((per_step * LANES, D_GDN), lambda c: (c, 0))
    mat = pl.BlockSpec((per_step, N_GDN_HEADS, LANES, LANES), lambda c: (c, 0, 0, 0))
    mat_shape = lambda dt: jax.ShapeDtypeStruct((nc, N_GDN_HEADS, LANES, LANES), dt)
    return pl.pallas_call(
        _gdn_chunk_kernel,
        out_shape=(mat_shape(BF16), mat_shape(F32), mat_shape(BF16), mat_shape(F32),
                   jax.ShapeDtypeStruct((nc, N_GDN_HEADS, LANES), F32)),
        grid=(nc // per_step,),
        in_specs=[tok, tok, tok, pl.BlockSpec((8, per_step * LANES), lambda c: (0, c))],
        out_specs=(mat, mat, mat, mat, pl.BlockSpec((per_step, N_GDN_HEADS, LANES), lambda c: (c, 0, 0))),
        compiler_params=_params("parallel"),
        name="gdn_chunks",
    )(qn, kn, vv, bg)


def _gdn_scan_kernel(m_ref, n_ref, p_ref, r_ref, cd_ref, o_ref, s_scr, *, batch):
    @pl.when(pl.program_id(0) == 0)
    def _():
        s_scr[...] = jnp.zeros(s_scr.shape, F32)

    for ch in range(m_ref.shape[1]):
        rows = slice(ch * LANES, (ch + 1) * LANES)
        for b in range(batch):
            for h in range(N_GDN_HEADS):
                s = s_scr[b, h]
                s16 = s.astype(BF16)
                o_ref[b, rows, h * GDN_HEAD_DIM:(h + 1) * GDN_HEAD_DIM] = _dot(p_ref[b, ch, h], s16) + r_ref[b, ch, h]
                s_scr[b, h] = s * cd_ref[b, ch, h:h + 1, :] + _dot(m_ref[b, ch, h], s16) + n_ref[b, ch, h]


def _gdn_scan(m_mat, n_mat, p_mat, r_mat, cd, batch, tp):
    nc = tp // LANES
    per_step = next(d for d in SCAN_CHUNKS if nc % d == 0)
    shape5 = lambda a: a.reshape(batch, nc, N_GDN_HEADS, LANES, LANES)
    mat = pl.BlockSpec((batch, per_step, N_GDN_HEADS, LANES, LANES), lambda c: (0, c, 0, 0, 0))
    return pl.pallas_call(
        functools.partial(_gdn_scan_kernel, batch=batch),
        out_shape=jax.ShapeDtypeStruct((batch, tp, D_GDN), F32),
        grid=(nc // per_step,),
        in_specs=[mat, mat, mat, mat,
                  pl.BlockSpec((batch, per_step, N_GDN_HEADS, LANES), lambda c: (0, c, 0, 0))],
        out_specs=pl.BlockSpec((batch, per_step * LANES, D_GDN), lambda c: (0, c, 0)),
        scratch_shapes=[pltpu.VMEM((batch, N_GDN_HEADS, LANES, LANES), F32)],
        compiler_params=_params("arbitrary"),
        name="gdn_scan",
    )(shape5(m_mat), shape5(n_mat), shape5(p_mat), shape5(r_mat), cd.reshape(batch, nc, N_GDN_HEADS, LANES))


def _out_proj_kernel(h_ref, oa_ref, za_ref, og_ref, zg_ref, gain_ref, w_ref, out_ref):
    attn = (oa_ref[...] * _silu(za_ref[...])).astype(BF16)
    y = _dot(attn, w_ref[0:D_ATTN, :])
    og, zg = og_ref[...], zg_ref[...]
    for h in range(N_GDN_HEADS):
        cols = slice(h * GDN_HEAD_DIM, (h + 1) * GDN_HEAD_DIM)
        gated = (_rms_rows(og[:, cols], gain_ref[...]) * _silu(zg[:, cols])).astype(BF16)
        y = y + _dot(gated, w_ref[D_ATTN + h * GDN_HEAD_DIM:D_ATTN + (h + 1) * GDN_HEAD_DIM, :])
    out_ref[...] = h_ref[...] + y


def _out_proj(h2, o_attn, proj, o_gdn, gain, w_out, batch, tp, first, rows):
    rb = _row_block(rows)
    n_blocks = rows // rb
    blk = lambda width, col: pl.BlockSpec((pl.Element(rb), pl.Element(width)),
                                          lambda b, j: (pl.multiple_of(b * tp + first + j * rb, HALO_ROWS), col * width))
    return pl.pallas_call(
        _out_proj_kernel,
        out_shape=jax.ShapeDtypeStruct((batch * rows, D_MODEL), F32),
        grid=(batch, n_blocks),
        in_specs=[blk(D_MODEL, 0), blk(D_ATTN, 0), blk(D_ATTN, COL_Z_ATTN), blk(D_GDN, 0),
                  blk(D_GDN, COL_Z_GDN),
                  pl.BlockSpec((1, GDN_HEAD_DIM), lambda b, j: (0, 0)),
                  pl.BlockSpec((D_MODEL, D_MODEL), lambda b, j: (0, 0))],
        out_specs=pl.BlockSpec((rb, D_MODEL), lambda b, j: (b * n_blocks + j, 0)),
        compiler_params=_params("parallel", "parallel"),
        name="out_proj",
    )(h2, o_attn, proj, o_gdn, proj, gain, w_out)


W_IN_PARTS = (("c_q", Q_RANK), ("c_kv", KV_RANK), ("k_idx", IDX_DIM), ("w_idx", N_IDX_HEADS),
              ("z_attn", D_ATTN), ("qkv_g", 3 * D_GDN), ("z_g", D_GDN), ("b", N_GDN_HEADS), ("a", N_GDN_HEADS))
W_PACKED_ORDER = ("qkv_g", "c_q", "c_kv", "k_idx", "k_idx", "z_attn", "z_g")


def _w_in_columns():
    cols, o = {}, 0
    for name, size in W_IN_PARTS:
        cols[name] = slice(o, o + size)
        o += size
    return cols, o


def _pack_w_in_kernel(w_ref, out_ref):
    cols, _ = _w_in_columns()
    o = 0
    for name in W_PACKED_ORDER:
        src = cols[name]
        size = src.stop - src.start
        out_ref[:, o:o + size] = w_ref[:, src].astype(BF16)
        o += size


def _pack_w_in(w_in, layer):
    _, width = _w_in_columns()
    rb = 2 * LANES
    return pl.pallas_call(
        _pack_w_in_kernel,
        out_shape=jax.ShapeDtypeStruct((D_MODEL, D_PACKED), BF16),
        grid=(D_MODEL // rb,),
        in_specs=[pl.BlockSpec((None, rb, width), lambda i: (layer, i, 0))],
        out_specs=pl.BlockSpec((rb, D_PACKED), lambda i: (i, 0)),
        compiler_params=_params("parallel"),
        name="pack_w_in",
    )(w_in)


def _pack_layer(layer, norm_gain, w_in_all, cq_gain, ckv_gain, w_uq, w_ukv, w_q_idx, q_gain, k_gain, conv_w, a_log,
                dt_bias, gdn_gain, w_out):
    cols, _ = _w_in_columns()
    w_in = w_in_all[layer]
    w_packed = _pack_w_in(w_in_all, layer)
    w_rows = jnp.concatenate([w_in[:, cols["w_idx"]], w_in[:, cols["b"]], w_in[:, cols["a"]],
                              jnp.zeros((D_MODEL, 4), w_in.dtype)], axis=1).T.astype(BF16)
    return dict(
        gain=norm_gain[None, :], w_packed=w_packed, w_rows=w_rows,
        g_cq=cq_gain[None, :], g_ckv=ckv_gain[None, :],
        w_uqt=w_uq.T.astype(BF16), w_qit=w_q_idx.T.astype(BF16),
        w_ukt=w_ukv[:, :D_ATTN].T.astype(BF16), w_uvt=w_ukv[:, D_ATTN:].T.astype(BF16),
        g_q_col=q_gain[:, None], g_k_col=k_gain[:, None],
        conv_w=conv_w, a_log=a_log[:, None], dt_bias=dt_bias[:, None],
        gdn_gain=gdn_gain[None, :], w_out=w_out.astype(BF16))


def _layer(h2, lw, bias, tri, batch, tp, t_valid, topk, keep):
    rb = _row_block(tp)
    proj, rows = _in_proj(h2, lw["gain"], lw["w_packed"], lw["w_rows"], rb)
    qt, k, vt, qit, kidx = _dsa_prep(proj, lw, batch, tp, rb)
    o_attn = _dsa_attention(qit, rows, qt, kidx, k, vt, bias, tri, batch, tp, t_valid, topk)
    qn, kn, vv, bg = _gdn_prep(proj, rows, lw, batch, tp, rb)
    m_mat, n_mat, p_mat, r_mat, cd = _gdn_chunks(qn, kn, vv, bg)
    o_gdn = _gdn_scan(m_mat, n_mat, p_mat, r_mat, cd, batch, tp).reshape(batch * tp, D_GDN)
    return _out_proj(h2, o_attn, proj, o_gdn, lw["gdn_gain"], lw["w_out"], batch, tp, *keep)


def _forward(x, meta_tokens, rel_bias_table, layer_weights, topk):
    batch, seq, _ = x.shape
    t = seq + N_META
    tp = -(-t // KEY_TILE) * KEY_TILE
    meta = jnp.broadcast_to(meta_tokens[None].astype(x.dtype), (batch, N_META, D_MODEL))
    h = jnp.concatenate([meta, x, jnp.zeros((batch, tp - t, D_MODEL), x.dtype)], axis=1)
    h2 = h.reshape(batch * tp, D_MODEL)
    bias = _bias_tiles(rel_bias_table)
    tri = jnp.tril(jnp.ones((KEY_TILE, KEY_TILE), BF16))
    crop = seq % LANES == 0
    for depth, lw in enumerate(layer_weights):
        last = crop and depth == len(layer_weights) - 1
        h2 = _layer(h2, lw, bias, tri, batch, tp, t, topk, (N_META, seq) if last else (0, tp))
    if crop:
        return h2.reshape(batch, seq, D_MODEL)
    return h2.reshape(batch, tp, D_MODEL)[:, N_META:t]


def kernel(x, meta_tokens, rel_bias_table, norm_gain, w_in, cq_norm_gain, ckv_norm_gain, w_uq, w_ukv, w_q_idx,
           q_norm_gain, k_norm_gain, conv_w, a_log, dt_bias, gdn_norm_gain, w_out):
    depth = norm_gain.shape[0]
    topk = min(TOPK_MAX, x.shape[1] // 4)
    layers = [_pack_layer(l, norm_gain[l], w_in, cq_norm_gain[l], ckv_norm_gain[l], w_uq[l], w_ukv[l],
                          w_q_idx[l], q_norm_gain[l], k_norm_gain[l], conv_w[l], a_log[l], dt_bias[l],
                          gdn_norm_gain[l], w_out[l]) for l in range(depth)]
    return _forward(x, meta_tokens, rel_bias_table, layers, topk)
```

```python
import functools
import math

import jax
import jax.numpy as jnp
from jax import lax
from jax.experimental import pallas as pl
from jax.experimental.pallas import tpu as pltpu

F32 = jnp.float32
BF16 = jnp.bfloat16
I32 = jnp.int32

D_MODEL = 1024
N_META = 16
EPS = 1e-6
N_ATTN_HEADS = 8
ATTN_HEAD_DIM = 64
D_ATTN = N_ATTN_HEADS * ATTN_HEAD_DIM
Q_RANK = 256
KV_RANK = 128
N_IDX_HEADS = 4
IDX_DIM = 64
TOPK_MAX = 256
N_REL_BUCKETS = 32
REL_MAX_DIST = 128
N_GDN_HEADS = 4
GDN_HEAD_DIM = 128
D_GDN = N_GDN_HEADS * GDN_HEAD_DIM
CONV_WIDTH = 4

LANES = 128
KEY_TILE = 256
COUNT_TILE = 512
SCAN_TILE = 1024
FIRST_CHECK_PASSES = 10
INTERPOLATED_PASSES = 24
MAX_SEARCH_PASSES = INTERPOLATED_PASSES + 32
GDN_CHUNKS_PER_STEP = 2
SCAN_CHUNKS = (6, 4, 3, 2, 1)
ROW_TILES = (6, 5, 4, 3, 2, 1)
HALO_ROWS = 8
D_PACKED = 3 * D_GDN + 512 + D_ATTN + D_GDN
COL_SMALL = 3
COL_Z_ATTN = 4
COL_Z_GDN = 5
MASKED_LOGIT = -1e30
LOG2_E = math.log2(math.e)
KEY_MIN = -2 ** 31
PATTERN_NEG_FLT_MAX = KEY_MIN + (1 << 23)
VMEM_LIMIT = 56 * 1024 * 1024

NT_DIMS = (((1,), (1,)), ((), ()))


def _dot(a, b):
    return jnp.dot(a, b, preferred_element_type=F32)


def _dot_nt(a, b):
    return lax.dot_general(a, b, NT_DIMS, preferred_element_type=F32)


def _split_bf16(x):
    hi = x.astype(BF16)
    return hi, (x - hi.astype(F32)).astype(BF16)


def _dot_split(a_parts, b_parts):
    (a_hi, a_lo), (b_hi, b_lo) = a_parts, b_parts
    return _dot(a_hi, b_hi) + (_dot(a_hi, b_lo) + _dot(a_lo, b_hi))


def _sigmoid(x):
    return 1.0 / (1.0 + jnp.exp(-x))


def _silu(x):
    return x * _sigmoid(x)


def _row_block(tp):
    tiles = tp // LANES
    return LANES * next(d for d in ROW_TILES if tiles % d == 0)


def _params(*sem):
    return pltpu.CompilerParams(dimension_semantics=sem, vmem_limit_bytes=VMEM_LIMIT)


def _bias_kernel(table_ref, out_ref):
    row = lax.broadcasted_iota(I32, (LANES, LANES), 0)
    col = lax.broadcasted_iota(I32, (LANES, LANES), 1)
    max_exact = N_REL_BUCKETS // 2
    for kind in range(3):
        dist = col - row + (2 - kind) * LANES
        n = jnp.maximum(dist, 0)
        nf = jnp.maximum(n, 1).astype(F32)
        large = max_exact + (jnp.log(nf / max_exact) / math.log(REL_MAX_DIST / max_exact)
                             * (N_REL_BUCKETS - max_exact)).astype(I32)
        large = jnp.minimum(large, N_REL_BUCKETS - 1)
        bucket = jnp.where(n < max_exact, n, large)
        for h in range(N_ATTN_HEADS):
            tile = jnp.zeros((LANES, LANES), F32)
            for b in range(N_REL_BUCKETS):
                tile = jnp.where(bucket == b, table_ref[b, h], tile)
            far = table_ref[N_REL_BUCKETS - 1, h]
            out_ref[kind, :, h * LANES:(h + 1) * LANES] = (tile - far) * LOG2_E


def _bias_tiles(rel_table):
    return pl.pallas_call(
        _bias_kernel,
        out_shape=jax.ShapeDtypeStruct((3, LANES, N_ATTN_HEADS * LANES), F32),
        in_specs=[pl.BlockSpec(memory_space=pltpu.SMEM)],
        out_specs=pl.BlockSpec(memory_space=pltpu.VMEM),
        name="rel_bias_tiles",
    )(rel_table)


def _in_proj_kernel(h_ref, gain_ref, w_ref, wrows_ref, proj_ref, rows_ref):
    x = h_ref[...]
    y = x * lax.rsqrt(jnp.mean(x * x, axis=-1, keepdims=True) + EPS)
    hn = (y * gain_ref[...]).astype(BF16)
    proj_ref[...] = _dot(hn, w_ref[...])
    rows_ref[...] = _dot_nt(wrows_ref[...], hn)


def _in_proj(h2, gain, w_packed, w_rows, rb):
    n_rows = h2.shape[0]
    grid = (n_rows // rb,)
    return pl.pallas_call(
        _in_proj_kernel,
        out_shape=(jax.ShapeDtypeStruct((n_rows, D_PACKED), F32),
                   jax.ShapeDtypeStruct((16, n_rows), F32)),
        grid=grid,
        in_specs=[pl.BlockSpec((rb, D_MODEL), lambda i: (i, 0)),
                  pl.BlockSpec((1, D_MODEL), lambda i: (0, 0)),
                  pl.BlockSpec((D_MODEL, D_PACKED), lambda i: (0, 0)),
                  pl.BlockSpec((16, D_MODEL), lambda i: (0, 0))],
        out_specs=(pl.BlockSpec((rb, D_PACKED), lambda i: (i, 0)),
                   pl.BlockSpec((16, rb), lambda i: (0, i))),
        compiler_params=_params("parallel"),
        name="in_proj",
    )(h2, gain, w_packed, w_rows)


def _rms_rows(x, gain):
    return x * lax.rsqrt(jnp.mean(x * x, axis=-1, keepdims=True) + EPS) * gain


def _dsa_prep_kernel(sm_ref, gcq_ref, gckv_ref, wuqt_ref, wqit_ref, wukt_ref, wuvt_ref, gq_ref, gk_ref,
                     qt_ref, k_ref, vt_ref, qit_ref, kidx_ref):
    sm = sm_ref[...]
    rb = sm.shape[0]
    cq = _rms_rows(sm[:, :Q_RANK], gcq_ref[...]).astype(BF16)
    ckv = _rms_rows(sm[:, Q_RANK:Q_RANK + KV_RANK], gckv_ref[...]).astype(BF16)
    kidx_ref[...] = sm[:, Q_RANK + KV_RANK:].astype(BF16)
    q3 = _dot_nt(wuqt_ref[...], cq).reshape(N_ATTN_HEADS, ATTN_HEAD_DIM, rb)
    q3 = q3 * lax.rsqrt(jnp.mean(q3 * q3, axis=1, keepdims=True) + EPS) * gq_ref[...][None]
    qt_ref[...] = (q3 * (ATTN_HEAD_DIM ** -0.5 * LOG2_E)).reshape(D_ATTN, rb).astype(BF16)
    k3 = _dot_nt(wukt_ref[...], ckv).reshape(N_ATTN_HEADS, ATTN_HEAD_DIM, rb)
    k3 = k3 * lax.rsqrt(jnp.mean(k3 * k3, axis=1, keepdims=True) + EPS) * gk_ref[...][None]
    k_ref[...] = k3.reshape(D_ATTN, rb).T.astype(BF16)
    vt_ref[...] = _dot_nt(wuvt_ref[...], ckv).astype(BF16)
    qit_ref[...] = _dot_nt(wqit_ref[...], cq).astype(BF16)


def _dsa_prep(proj, lw, batch, tp, rb):
    n_rows = proj.shape[0]
    nb = tp // rb
    d_idx = N_IDX_HEADS * IDX_DIM
    const = lambda shape: pl.BlockSpec(shape, lambda b, i: (0, 0))
    row_spec = lambda width: pl.BlockSpec((rb, width), lambda b, i: (b * nb + i, 0))
    col_spec = lambda height: pl.BlockSpec((None, height, rb), lambda b, i: (b, 0, i))
    return pl.pallas_call(
        _dsa_prep_kernel,
        out_shape=(jax.ShapeDtypeStruct((batch, D_ATTN, tp), BF16),
                   jax.ShapeDtypeStruct((n_rows, D_ATTN), BF16),
                   jax.ShapeDtypeStruct((batch, D_ATTN, tp), BF16),
                   jax.ShapeDtypeStruct((batch, d_idx, tp), BF16),
                   jax.ShapeDtypeStruct((n_rows, LANES), BF16)),
        grid=(batch, nb),
        in_specs=[pl.BlockSpec((rb, 512), lambda b, i: (b * nb + i, COL_SMALL)),
                  const((1, Q_RANK)), const((1, KV_RANK)),
                  const((D_ATTN, Q_RANK)), const((d_idx, Q_RANK)),
                  const((D_ATTN, KV_RANK)), const((D_ATTN, KV_RANK)),
                  const((ATTN_HEAD_DIM, 1)), const((ATTN_HEAD_DIM, 1))],
        out_specs=(col_spec(D_ATTN), row_spec(D_ATTN), col_spec(D_ATTN), col_spec(d_idx), row_spec(LANES)),
        compiler_params=_params("parallel", "parallel"),
        name="dsa_prep",
    )(proj, lw["g_cq"], lw["g_ckv"], lw["w_uqt"], lw["w_qit"], lw["w_ukt"], lw["w_uvt"], lw["g_q_col"],
      lw["g_k_col"])


def _dsa_block(i, qit_ref, rows_ref, qt_ref, kidx_ref, k_ref, vt_ref, bias_ref, tri_ref, o_ref,
               score_scr, wi_scr, wq_scr, s_scr, p_scr, m_scr, l_scr, acc_scr, alpha_scr, mask_scr, tie_scr, *, topk):
    t0 = i * LANES
    n_kt = i // 2 + 1
    n_st = (n_kt + 3) // 4
    hd = ATTN_HEAD_DIM
    pair_w = 2 * LANES
    n_pairs = N_ATTN_HEADS // 2

    zeros_hd = jnp.zeros((hd, LANES), BF16)
    for h in range(N_IDX_HEADS):
        wi_scr[0:IDX_DIM, h * LANES:(h + 1) * LANES] = qit_ref[h * IDX_DIM:(h + 1) * IDX_DIM, :]
    wi_scr[IDX_DIM:, :] = jnp.zeros((LANES - IDX_DIM, N_IDX_HEADS * LANES), BF16)
    for p in range(n_pairs):
        wq_scr[p, 0:hd, 0:LANES] = qt_ref[2 * p * hd:(2 * p + 1) * hd, :]
        wq_scr[p, 0:hd, LANES:] = zeros_hd
        wq_scr[p, hd:, 0:LANES] = zeros_hd
        wq_scr[p, hd:, LANES:] = qt_ref[(2 * p + 1) * hd:(2 * p + 2) * hd, :]

    row = lax.broadcasted_iota(I32, (KEY_TILE, LANES), 0)
    col = lax.broadcasted_iota(I32, (KEY_TILE, LANES), 1)
    w_idx = rows_ref[0:N_IDX_HEADS, :] * (N_IDX_HEADS ** -0.5 * IDX_DIM ** -0.5)

    def key_tile(j):
        return pl.multiple_of(j * KEY_TILE, KEY_TILE)

    def causal(j):
        return (j * KEY_TILE + row) <= (t0 + col)

    def fold8(x, op):
        return op(x.reshape(KEY_TILE // 8, 8, LANES), axis=0)

    def score_step(jc, carry, masked):
        top, bottom, n_nonneg, n_pos = carry
        subs = [jc * (SCAN_TILE // KEY_TILE) + sub for sub in range(SCAN_TILE // KEY_TILE)]
        logits = [_dot(kidx_ref[pl.ds(key_tile(jnp.minimum(j, n_kt - 1)), KEY_TILE), :], wi_scr[...])
                  for j in subs]
        for j, lg in zip(subs, logits):
            score = jnp.zeros((KEY_TILE, LANES), F32)
            for h in range(N_IDX_HEADS):
                score = score + jnp.maximum(lg[:, h * LANES:(h + 1) * LANES], 0.0) * w_idx[h:h + 1, :]
            if masked:
                visible = causal(j)
                seen = jnp.where(visible, score, -jnp.inf)
                bottom = jnp.minimum(bottom, fold8(jnp.where(visible, score, jnp.inf), jnp.min))
            else:
                seen = score
                bottom = jnp.minimum(bottom, fold8(score, jnp.min))
            top = jnp.maximum(top, fold8(seen, jnp.max))
            n_nonneg = n_nonneg + fold8(jnp.where(seen >= 0.0, 1, 0), jnp.sum)
            n_pos = n_pos + fold8(jnp.where(seen > 0.0, 1, 0), jnp.sum)
            score_scr[pl.ds(key_tile(j), KEY_TILE), :] = seen
        return top, bottom, n_nonneg, n_pos

    zeros8 = jnp.zeros((8, LANES), I32)
    carry = (jnp.full((8, LANES), -jnp.inf, F32), jnp.full((8, LANES), jnp.inf, F32), zeros8, zeros8)
    carry = lax.fori_loop(0, n_st - 1, functools.partial(score_step, masked=False), carry)
    top, bottom, n_nonneg, n_pos = score_step(n_st - 1, carry, masked=True)
    top = jnp.max(top, axis=0, keepdims=True)
    bottom = jnp.min(bottom, axis=0, keepdims=True)
    count0 = jnp.sum(n_nonneg, axis=0, keepdims=True)
    count_pos = jnp.sum(n_pos, axis=0, keepdims=True)

    def count_f32(cand, below=None):
        def body(j, carry):
            acc, best = carry
            for part in range(SCAN_TILE // COUNT_TILE):
                start = pl.multiple_of(j * SCAN_TILE + part * COUNT_TILE, COUNT_TILE)
                x = score_scr[pl.ds(start, COUNT_TILE), :]
                ind = jnp.where(x >= cand, 1, 0)
                acc = acc + jnp.sum(ind.reshape(COUNT_TILE // 8, 8, LANES), axis=0)
                if below is not None:
                    under = jnp.where(x < below, x, -jnp.inf)
                    best = jnp.maximum(best, jnp.max(under.reshape(COUNT_TILE // 8, 8, LANES), axis=0))
            return acc, best
        acc, best = lax.fori_loop(0, n_st, body, (jnp.zeros((8, LANES), I32), jnp.full((8, LANES), -jnp.inf, F32)))
        count = jnp.sum(acc, axis=0, keepdims=True)
        if below is None:
            return count
        return count, jnp.max(best, axis=0, keepdims=True)

    def to_pattern(v):
        bits = lax.bitcast_convert_type(v, I32)
        return bits ^ ((bits >> 31) & 0x7FFFFFFF)

    def to_f32(c):
        return lax.bitcast_convert_type(c ^ ((c >> 31) & 0x7FFFFFFF), F32)

    n_visible = t0 + 1 + lax.broadcasted_iota(I32, (1, LANES), 1)
    nonneg = count0 >= topk
    lo = jnp.where(nonneg, 0, to_pattern(bottom))
    hi = jnp.where(nonneg, to_pattern(top) + 1, 0)
    count_lo = jnp.where(nonneg, count0, n_visible)
    count_hi = jnp.where(nonneg, 0, count0)
    few = n_visible < topk
    lo = jnp.where(few, PATTERN_NEG_FLT_MAX, lo)
    zero_tied = nonneg & (count_pos < topk)
    hi = jnp.where(zero_tied, 1, hi)
    count_hi = jnp.where(zero_tied, count_pos, count_hi)
    open_q = jnp.where(few | zero_tied | (count_lo == topk), 0, 1)

    log_topk = math.log(topk)

    def count_error(count):
        return jnp.log(count.astype(F32) + 0.5) - log_topk

    def probe(n_pass, carry, extract):
        lo, hi, count_lo, count_hi, err_lo, err_hi, last_side, open_q = carry
        v_lo, v_hi = to_f32(lo), to_f32(hi)
        frac = err_lo / (err_lo - err_hi)
        frac = jnp.where(count_lo - count_hi <= 4, 0.5, frac)
        guess = to_pattern(v_lo + (v_hi - v_lo) * frac)
        middle = lo + lax.shift_right_logical(hi - lo, 1)
        cand = jnp.where(n_pass >= INTERPOLATED_PASSES, middle, guess)
        cand = jnp.minimum(jnp.maximum(cand, lo + 1), hi - 1)
        is_open = open_q == 1
        if extract:
            count, under_hi = count_f32(to_f32(cand), below=v_hi)
            next_below = to_pattern(under_hi)
            found = is_open & (count_hi == topk - 1)
            is_open = is_open & jnp.logical_not(found)
        else:
            count = count_f32(to_f32(cand))
        raise_lo = is_open & (count >= topk)
        lower_hi = is_open & (count < topk)
        err = count_error(count)
        err_hi = jnp.where(raise_lo & (last_side == 1), err_hi * 0.5, err_hi)
        err_lo = jnp.where(lower_hi & (last_side == -1), err_lo * 0.5, err_lo)
        err_lo = jnp.where(raise_lo, err, err_lo)
        err_hi = jnp.where(lower_hi, err, err_hi)
        lo = jnp.where(raise_lo, cand, lo)
        count_lo = jnp.where(raise_lo, count, count_lo)
        hi = jnp.where(lower_hi, cand, hi)
        count_hi = jnp.where(lower_hi, count, count_hi)
        last_side = jnp.where(raise_lo, 1, jnp.where(lower_hi, -1, last_side))
        if extract:
            lo = jnp.where(found, next_below, lo)
            hi = jnp.where(found | raise_lo, next_below + 1, hi)
        closed = (count_lo == topk) | (hi - lo == 1)
        return lo, hi, count_lo, count_hi, err_lo, err_hi, last_side, jnp.where(closed, 0, open_q)

    carry = (lo, hi, count_lo, count_hi, count_error(count_lo), count_error(count_hi),
             jnp.zeros((1, LANES), I32), open_q)
    n_first = jnp.where(jnp.sum(open_q) > 0, FIRST_CHECK_PASSES, 0)
    carry = lax.fori_loop(0, n_first, functools.partial(probe, extract=False), carry)

    def probes_left(st):
        n_pass, n_open = st[0], st[-1]
        return (n_pass < MAX_SEARCH_PASSES) & (n_open > 0)

    def extracting_probe(st):
        carry = probe(st[0], st[1:-1], extract=True)
        return (st[0] + 1,) + carry + (jnp.sum(carry[-1]),)

    state = lax.while_loop(probes_left, extracting_probe,
                           (jnp.int32(FIRST_CHECK_PASSES),) + carry + (jnp.sum(carry[-1]),))
    lo, count_lo, count_hi = state[1], state[3], state[4]
    tau = to_f32(lo)
    need = jnp.where((count_lo == topk) | few, topk, topk - count_hi).astype(F32)

    m_scr[...] = jnp.full(m_scr.shape, MASKED_LOGIT, F32)
    l_scr[...] = jnp.zeros(l_scr.shape, F32)
    acc_scr[...] = jnp.zeros(acc_scr.shape, F32)
    ones_rows = jnp.ones((16, KEY_TILE), BF16)
    last = n_kt - 1

    def mask_pair(j_first, parity):
        xs = [score_scr[pl.ds(key_tile(jnp.minimum(j_first + slot, last)), KEY_TILE), :] for slot in range(2)]
        ties = [x == tau for x in xs]
        tie_cols = jnp.concatenate([jnp.where(tie, 1.0, 0.0).astype(BF16) for tie in ties], axis=1)
        ranks = _dot(tri_ref[...], tie_cols)
        tie_carry = tie_scr[0:1, :]
        for slot, (x, tie) in enumerate(zip(xs, ties)):
            rank = ranks[:, slot * LANES:(slot + 1) * LANES] + tie_carry
            tie_carry = rank[KEY_TILE - 1:KEY_TILE, :]
            take = (tie & (rank <= need)) | (x > tau)
            mask_scr[parity, slot] = jnp.where(take, 0.0, MASKED_LOGIT)
        tie_scr[0:1, :] = tie_carry

    def qk_pair(slot, j, p):
        s_scr[slot, :, p * pair_w:(p + 1) * pair_w] = _dot(
            k_ref[pl.ds(key_tile(j), KEY_TILE), p * LANES:(p + 1) * LANES], wq_scr[p])

    def softmax_pair(slot, j, p, near, parity):
        mask_add = mask_scr[parity, slot]
        alphas = []
        for h in (2 * p, 2 * p + 1):
            cols = slice(h * LANES, (h + 1) * LANES)
            logits = s_scr[slot, :, cols] + mask_add
            if near:
                kind_top = jnp.clip(2 * j - i + 2, 0, 2)
                kind_bot = jnp.clip(2 * j - i + 3, 0, 2)
                logits = logits + jnp.concatenate(
                    [bias_ref[kind_top, :, cols], bias_ref[kind_bot, :, cols]], axis=0)
            m_old = m_scr[h:h + 1, :]
            m_new = jnp.maximum(m_old, jnp.max(logits, axis=0, keepdims=True))
            m_scr[h:h + 1, :] = m_new
            p_scr[slot, :, cols] = jnp.exp2(logits - m_new).astype(BF16)
            alphas.append(jnp.exp2(m_old - m_new))
        return alphas

    def pv_pair(slot, j, p, alphas):
        lhs = jnp.concatenate([vt_ref[p * 2 * hd:(p + 1) * 2 * hd, pl.ds(key_tile(j), KEY_TILE)], ones_rows],
                              axis=0)
        out = _dot(lhs, p_scr[slot, :, p * pair_w:(p + 1) * pair_w])
        for half in range(2):
            h = 2 * p + half
            rows_h = slice(h * hd, (h + 1) * hd)
            q_cols = slice(half * LANES, (half + 1) * LANES)
            acc_scr[rows_h, :] = acc_scr[rows_h, :] * alphas[half] + out[half * hd:(half + 1) * hd, q_cols]
            l_scr[h:h + 1, :] = l_scr[h:h + 1, :] * alphas[half] + out[2 * hd:2 * hd + 1, q_cols]

    def pending_alphas(p):
        return [alpha_scr[h:h + 1, :] for h in (2 * p, 2 * p + 1)]

    def clear_pending():
        p_scr[1] = jnp.zeros(p_scr.shape[1:], BF16)
        alpha_scr[...] = jnp.ones(alpha_scr.shape, F32)

    def pair_step(ja, j_pending, j_next, near, parity):
        alphas_a = []
        for p in range(n_pairs):
            pv_pair(1, j_pending, p, pending_alphas(p))
            qk_pair(1, ja + 1, p)
            alphas_a.append(softmax_pair(0, ja, p, near, parity))
        for p in range(n_pairs):
            pv_pair(0, ja, p, alphas_a[p])
            qk_pair(0, j_next, p)
            alphas_b = softmax_pair(1, ja + 1, p, near, parity)
            for half in range(2):
                alpha_scr[2 * p + half:2 * p + half + 1, :] = alphas_b[half]
        mask_pair(ja + 2, 1 - parity)

    def single_step(ja, j_pending, near, parity):
        alphas_a = []
        for p in range(n_pairs):
            pv_pair(1, j_pending, p, pending_alphas(p))
            alphas_a.append(softmax_pair(0, ja, p, near, parity))
        for p in range(n_pairs):
            pv_pair(0, ja, p, alphas_a[p])
        clear_pending()

    n_far = 2 * (jnp.maximum(n_kt - 2, 0) // 2)
    n_near = n_kt - n_far
    first_near_parity = (n_far // 2) % 2
    clear_pending()
    tie_scr[...] = jnp.zeros(tie_scr.shape, F32)
    mask_pair(0, 0)
    for p in range(n_pairs):
        qk_pair(0, 0, p)

    def far_body(jp, carry):
        pair_step(2 * jp, jnp.maximum(2 * jp - 1, 0), 2 * jp + 2, near=False, parity=jp % 2)
        return carry

    lax.fori_loop(0, n_far // 2, far_body, 0)

    @pl.when(n_near >= 2)
    def _():
        pair_step(n_far, jnp.maximum(n_far - 1, 0), jnp.minimum(n_far + 2, last), near=True,
                  parity=first_near_parity)

    @pl.when(n_near % 2 == 1)
    def _():
        single_step(last, jnp.where(n_near == 3, n_far + 1, jnp.maximum(n_far - 1, 0)), near=True,
                    parity=jnp.where(n_near == 3, 1 - first_near_parity, first_near_parity))

    for p in range(n_pairs):
        pv_pair(1, last, p, pending_alphas(p))

    for h in range(N_ATTN_HEADS):
        rows_h = slice(h * hd, (h + 1) * hd)
        acc_scr[rows_h, :] = acc_scr[rows_h, :] / l_scr[h:h + 1, :]
    o_ref[...] = acc_scr[...].T


def _dsa_kernel(*refs, topk, t_valid):
    o_ref = refs[8]
    i = pl.program_id(1)
    is_real = i * LANES < t_valid

    @pl.when(is_real)
    def _():
        _dsa_block(i, *refs, topk=topk)

    @pl.when(jnp.logical_not(is_real))
    def _():
        o_ref[...] = jnp.zeros(o_ref.shape, F32)


def _dsa_attention(qit, rows, qt, kidx, k, vt, bias, tri, batch, tp, t_valid, topk):
    n_rows = k.shape[0]
    nqb = tp // LANES
    d_idx = N_IDX_HEADS * IDX_DIM
    n_pairs = N_ATTN_HEADS // 2
    key_rows = -(-tp // SCAN_TILE) * SCAN_TILE
    q_cols = lambda height: pl.BlockSpec((None, height, LANES), lambda b, i: (b, 0, i))
    return pl.pallas_call(
        functools.partial(_dsa_kernel, topk=topk, t_valid=t_valid),
        out_shape=jax.ShapeDtypeStruct((n_rows, D_ATTN), F32),
        grid=(batch, nqb),
        in_specs=[q_cols(d_idx),
                  pl.BlockSpec((16, LANES), lambda b, i: (0, b * nqb + i)),
                  q_cols(D_ATTN),
                  pl.BlockSpec((tp, LANES), lambda b, i: (b, 0)),
                  pl.BlockSpec((tp, D_ATTN), lambda b, i: (b, 0)),
                  pl.BlockSpec((None, D_ATTN, tp), lambda b, i: (b, 0, 0)),
                  pl.BlockSpec((3, LANES, N_ATTN_HEADS * LANES), lambda b, i: (0, 0, 0)),
                  pl.BlockSpec((KEY_TILE, KEY_TILE), lambda b, i: (0, 0))],
        out_specs=pl.BlockSpec((LANES, D_ATTN), lambda b, i: (b * nqb + i, 0)),
        scratch_shapes=[pltpu.VMEM((key_rows, LANES), F32),
                        pltpu.VMEM((LANES, N_IDX_HEADS * LANES), BF16),
                        pltpu.VMEM((n_pairs, LANES, 2 * LANES), BF16),
                        pltpu.VMEM((2, KEY_TILE, N_ATTN_HEADS * LANES), F32),
                        pltpu.VMEM((2, KEY_TILE, N_ATTN_HEADS * LANES), BF16),
                        pltpu.VMEM((N_ATTN_HEADS, LANES), F32),
                        pltpu.VMEM((N_ATTN_HEADS, LANES), F32),
                        pltpu.VMEM((D_ATTN, LANES), F32),
                        pltpu.VMEM((N_ATTN_HEADS, LANES), F32),
                        pltpu.VMEM((2, 2, KEY_TILE, LANES), F32),
                        pltpu.VMEM((8, LANES), F32)],
        compiler_params=_params("parallel", "parallel"),
        name="dsa_attention",
    )(qit, rows, qt, kidx, k, vt, bias, tri)


def _gdn_prep_kernel(x_ref, halo_ref, cw_ref, rows_ref, alog_ref, dtb_ref, q_ref, k_ref, v_ref, bg_ref, buf):
    first = pl.program_id(1) == 0
    rb = x_ref.shape[0]
    buf[0:HALO_ROWS, :] = jnp.where(first, 0.0, halo_ref[...])
    buf[HALO_ROWS:, :] = x_ref[...]
    acc = jnp.zeros((rb, 3 * D_GDN), F32)
    for tap in range(CONV_WIDTH):
        start = HALO_ROWS - (CONV_WIDTH - 1) + tap
        acc = acc + cw_ref[tap:tap + 1, :] * buf[start:start + rb, :]
    y = _silu(acc)
    for h in range(N_GDN_HEADS):
        cols = slice(h * GDN_HEAD_DIM, (h + 1) * GDN_HEAD_DIM)
        qh = y[:, cols]
        kh = y[:, D_GDN + h * GDN_HEAD_DIM:D_GDN + (h + 1) * GDN_HEAD_DIM]
        q_ref[:, cols] = (qh * lax.rsqrt(jnp.sum(qh * qh, axis=-1, keepdims=True) + EPS)
                          * (GDN_HEAD_DIM ** -0.5))
        k_ref[:, cols] = kh * lax.rsqrt(jnp.sum(kh * kh, axis=-1, keepdims=True) + EPS)
    v_ref[...] = y[:, 2 * D_GDN:]
    rows = rows_ref[...]
    beta = _sigmoid(rows[4:8, :])
    a = rows[8:12, :] + dtb_ref[...]
    softplus = jnp.maximum(a, 0.0) + jnp.log1p(jnp.exp(-jnp.abs(a)))
    bg_ref[0:4, :] = beta
    bg_ref[4:8, :] = -jnp.exp(alog_ref[...]) * softplus


def _gdn_prep(proj, rows, lw, batch, tp, rb):
    n_rows = proj.shape[0]
    nb = tp // rb
    halo_per_block = rb // HALO_ROWS
    row_spec = pl.BlockSpec((rb, D_GDN), lambda b, i: (b * nb + i, 0))
    return pl.pallas_call(
        _gdn_prep_kernel,
        out_shape=(jax.ShapeDtypeStruct((n_rows, D_GDN), F32),) * 3
        + (jax.ShapeDtypeStruct((8, n_rows), F32),),
        grid=(batch, nb),
        in_specs=[pl.BlockSpec((rb, 3 * D_GDN), lambda b, i: (b * nb + i, 0)),
                  pl.BlockSpec((HALO_ROWS, 3 * D_GDN),
                               lambda b, i: (jnp.maximum((b * nb + i) * halo_per_block - 1, 0), 0)),
                  pl.BlockSpec((CONV_WIDTH, 3 * D_GDN), lambda b, i: (0, 0)),
                  pl.BlockSpec((16, rb), lambda b, i: (0, b * nb + i)),
                  pl.BlockSpec((N_GDN_HEADS, 1), lambda b, i: (0, 0)),
                  pl.BlockSpec((N_GDN_HEADS, 1), lambda b, i: (0, 0))],
        out_specs=(row_spec, row_spec, row_spec,
                   pl.BlockSpec((8, rb), lambda b, i: (0, b * nb + i))),
        scratch_shapes=[pltpu.VMEM((HALO_ROWS + rb, 3 * D_GDN), F32)],
        compiler_params=_params("parallel", "parallel"),
        name="gdn_prep",
    )(proj, proj, lw["conv_w"], rows, lw["a_log"], lw["dt_bias"])


def _gdn_chunk_kernel(q_ref, k_ref, v_ref, bg_ref, m_ref, n_ref, p_ref, r_ref, cd_ref):
    c = LANES
    n_chunks = q_ref.shape[0] // c
    items = [(ch, h) for ch in range(n_chunks) for h in range(N_GDN_HEADS)]
    idx = range(len(items))
    row = lax.broadcasted_iota(I32, (c, c), 0)
    col = lax.broadcasted_iota(I32, (c, c), 1)
    tri = row >= col
    strict = row > col
    eye = jnp.where(row == col, 1.0, 0.0)
    lane8 = lax.broadcasted_iota(I32, (8, c), 1)
    gates, decays = [], []
    for ch in range(n_chunks):
        bg = bg_ref[:, ch * c:(ch + 1) * c]
        dec = bg
        shift = 1
        while shift < c:
            dec = dec + jnp.where(lane8 >= shift, pltpu.roll(dec, shift, 1), 0.0)
            shift *= 2
        gates.append(bg)
        decays.append(dec)

    def tokens(ref, n):
        ch, h = items[n]
        return ref[ch * c:(ch + 1) * c, h * GDN_HEAD_DIM:(h + 1) * GDN_HEAD_DIM]

    d_row = [jnp.broadcast_to(decays[ch][4 + h:5 + h, :], (c, c)) for ch, h in items]
    d_col = [d.T for d in d_row]
    beta_col = [jnp.broadcast_to(gates[ch][h:h + 1, :], (c, c)).T for ch, h in items]
    d_last = [d[:, c - 1:c] for d in d_row]
    gamma = [jnp.exp(jnp.where(tri, d_col[n] - d_row[n], MASKED_LOGIT)) for n in idx]
    exp_d = [jnp.exp(d_col[n]) for n in idx]
    k16 = [tokens(k_ref, n).astype(BF16) for n in idx]
    kb = [tokens(k_ref, n) * beta_col[n] for n in idx]
    nil = [jnp.where(strict, _dot_nt(kb[n].astype(BF16), k16[n]) * gamma[n], 0.0) for n in idx]
    block = 8
    same = lambda size: (row // size) == (col // size)
    diag = [jnp.where(same(block), x, 0.0) for x in nil]
    diag_parts = [_split_bf16(x) for x in diag]
    inv = [eye - x for x in diag]
    power = [_dot_split(x, x) for x in diag_parts]
    for it in range(2):
        power_parts = [_split_bf16(x) for x in power]
        inv = [inv[n] + _dot_split(_split_bf16(inv[n]), power_parts[n]) for n in idx]
        if it == 0:
            power = [_dot_split(x, x) for x in power_parts]
    while block < c:
        couples = same(2 * block) & ((row // block) % 2 == 1) & ((col // block) % 2 == 0)
        inv_parts = [_split_bf16(x) for x in inv]
        lower = [_dot_split(_split_bf16(jnp.where(couples, nil[n], 0.0)), inv_parts[n]) for n in idx]
        inv = [inv[n] - _dot_split(inv_parts[n], _split_bf16(lower[n])) for n in idx]
        block *= 2
    rhs = [jnp.concatenate([kb[n] * exp_d[n], tokens(v_ref, n) * beta_col[n]], axis=1) for n in idx]
    wu = [_dot_split(_split_bf16(inv[n]), _split_bf16(rhs[n])).astype(BF16) for n in idx]
    aqk = [jnp.where(tri, _dot_nt(tokens(q_ref, n).astype(BF16), k16[n]) * gamma[n], 0.0).astype(BF16)
           for n in idx]
    kd_t = [(tokens(k_ref, n) * jnp.exp(d_last[n] - d_col[n])).T.astype(BF16) for n in idx]
    state_wu = [_dot(kd_t[n], wu[n]) for n in idx]
    out_wu = [_dot(aqk[n], wu[n]) for n in idx]
    for n, (ch, h) in enumerate(items):
        m_ref[ch, h] = (-state_wu[n][:, :c]).astype(BF16)
        n_ref[ch, h] = state_wu[n][:, c:]
        p_ref[ch, h] = (tokens(q_ref, n) * exp_d[n] - out_wu[n][:, :c]).astype(BF16)
        r_ref[ch, h] = out_wu[n][:, c:]
        cd_ref[ch, h:h + 1, :] = jnp.exp(d_last[n][0:1, :] + jnp.zeros((1, c), F32))


def _gdn_chunks(qn, kn, vv, bg):
    n_rows = qn.shape[0]
    nc = n_rows // LANES
    per_step = GDN_CHUNKS_PER_STEP
    tok = pl.BlockSpec((per_step * LANES, D_GDN), lambda c: (c, 0))
    mat = pl.BlockSpec((per_step, N_GDN_HEADS, LANES, LANES), lambda c: (c, 0, 0, 0))
    mat_shape = lambda dt: jax.ShapeDtypeStruct((nc, N_GDN_HEADS, LANES, LANES), dt)
    return pl.pallas_call(
        _gdn_chunk_kernel,
        out_shape=(mat_shape(BF16), mat_shape(F32), mat_shape(BF16), mat_shape(F32),
                   jax.ShapeDtypeStruct((nc, N_GDN_HEADS, LANES), F32)),
        grid=(nc // per_step,),
        in_specs=[tok, tok, tok, pl.BlockSpec((8, per_step * LANES), lambda c: (0, c))],
        out_specs=(mat, mat, mat, mat, pl.BlockSpec((per_step, N_GDN_HEADS, LANES), lambda c: (c, 0, 0))),
        compiler_params=_params("parallel"),
        name="gdn_chunks",
    )(qn, kn, vv, bg)


def _gdn_scan_kernel(m_ref, n_ref, p_ref, r_ref, cd_ref, o_ref, s_scr, *, batch):
    @pl.when(pl.program_id(0) == 0)
    def _():
        s_scr[...] = jnp.zeros(s_scr.shape, F32)

    for ch in range(m_ref.shape[1]):
        rows = slice(ch * LANES, (ch + 1) * LANES)
        for b in range(batch):
            for h in range(N_GDN_HEADS):
                s = s_scr[b, h]
                s16 = s.astype(BF16)
                o_ref[b, rows, h * GDN_HEAD_DIM:(h + 1) * GDN_HEAD_DIM] = _dot(p_ref[b, ch, h], s16) + r_ref[b, ch, h]
                s_scr[b, h] = s * cd_ref[b, ch, h:h + 1, :] + _dot(m_ref[b, ch, h], s16) + n_ref[b, ch, h]


def _gdn_scan(m_mat, n_mat, p_mat, r_mat, cd, batch, tp):
    nc = tp // LANES
    per_step = next(d for d in SCAN_CHUNKS if nc % d == 0)
    shape5 = lambda a: a.reshape(batch, nc, N_GDN_HEADS, LANES, LANES)
    mat = pl.BlockSpec((batch, per_step, N_GDN_HEADS, LANES, LANES), lambda c: (0, c, 0, 0, 0))
    return pl.pallas_call(
        functools.partial(_gdn_scan_kernel, batch=batch),
        out_shape=jax.ShapeDtypeStruct((batch, tp, D_GDN), F32),
        grid=(nc // per_step,),
        in_specs=[mat, mat, mat, mat,
                  pl.BlockSpec((batch, per_step, N_GDN_HEADS, LANES), lambda c: (0, c, 0, 0))],
        out_specs=pl.BlockSpec((batch, per_step * LANES, D_GDN), lambda c: (0, c, 0)),
        scratch_shapes=[pltpu.VMEM((batch, N_GDN_HEADS, LANES, LANES), F32)],
        compiler_params=_params("arbitrary"),
        name="gdn_scan",
    )(shape5(m_mat), shape5(n_mat), shape5(p_mat), shape5(r_mat), cd.reshape(batch, nc, N_GDN_HEADS, LANES))


def _out_proj_kernel(h_ref, oa_ref, za_ref, og_ref, zg_ref, gain_ref, w_ref, out_ref):
    attn = (oa_ref[...] * _silu(za_ref[...])).astype(BF16)
    y = _dot(attn, w_ref[0:D_ATTN, :])
    og, zg = og_ref[...], zg_ref[...]
    for h in range(N_GDN_HEADS):
        cols = slice(h * GDN_HEAD_DIM, (h + 1) * GDN_HEAD_DIM)
        gated = (_rms_rows(og[:, cols], gain_ref[...]) * _silu(zg[:, cols])).astype(BF16)
        y = y + _dot(gated, w_ref[D_ATTN + h * GDN_HEAD_DIM:D_ATTN + (h + 1) * GDN_HEAD_DIM, :])
    out_ref[...] = h_ref[...] + y


def _out_proj(h2, o_attn, proj, o_gdn, gain, w_out, batch, tp, first, rows):
    rb = _row_block(rows)
    n_blocks = rows // rb
    blk = lambda width, col: pl.BlockSpec((pl.Element(rb), pl.Element(width)),
                                          lambda b, j: (pl.multiple_of(b * tp + first + j * rb, HALO_ROWS), col * width))
    return pl.pallas_call(
        _out_proj_kernel,
        out_shape=jax.ShapeDtypeStruct((batch * rows, D_MODEL), F32),
        grid=(batch, n_blocks),
        in_specs=[blk(D_MODEL, 0), blk(D_ATTN, 0), blk(D_ATTN, COL_Z_ATTN), blk(D_GDN, 0),
                  blk(D_GDN, COL_Z_GDN),
                  pl.BlockSpec((1, GDN_HEAD_DIM), lambda b, j: (0, 0)),
                  pl.BlockSpec((D_MODEL, D_MODEL), lambda b, j: (0, 0))],
        out_specs=pl.BlockSpec((rb, D_MODEL), lambda b, j: (b * n_blocks + j, 0)),
        compiler_params=_params("parallel", "parallel"),
        name="out_proj",
    )(h2, o_attn, proj, o_gdn, proj, gain, w_out)


W_IN_PARTS = (("c_q", Q_RANK), ("c_kv", KV_RANK), ("k_idx", IDX_DIM), ("w_idx", N_IDX_HEADS),
              ("z_attn", D_ATTN), ("qkv_g", 3 * D_GDN), ("z_g", D_GDN), ("b", N_GDN_HEADS), ("a", N_GDN_HEADS))
W_PACKED_ORDER = ("qkv_g", "c_q", "c_kv", "k_idx", "k_idx", "z_attn", "z_g")


def _w_in_columns():
    cols, o = {}, 0
    for name, size in W_IN_PARTS:
        cols[name] = slice(o, o + size)
        o += size
    return cols, o


W_ROWS_ORDER = ("w_idx", "b", "a")


def _pack_w_in_kernel(w_ref, out_ref, narrow_ref):
    cols, _ = _w_in_columns()
    narrow_ref[...] = jnp.zeros(narrow_ref.shape, F32)
    o = 0
    for name in W_ROWS_ORDER:
        src = cols[name]
        size = src.stop - src.start
        narrow_ref[:, o:o + size] = w_ref[:, src]
        o += size
    o = 0
    for name in W_PACKED_ORDER:
        src = cols[name]
        size = src.stop - src.start
        out_ref[:, o:o + size] = w_ref[:, src].astype(BF16)
        o += size


def _pack_w_in(w_in, layer):
    _, width = _w_in_columns()
    rb = 2 * LANES
    return pl.pallas_call(
        _pack_w_in_kernel,
        out_shape=(jax.ShapeDtypeStruct((D_MODEL, D_PACKED), BF16),
                   jax.ShapeDtypeStruct((D_MODEL, LANES), F32)),
        grid=(D_MODEL // rb,),
        in_specs=[pl.BlockSpec((None, rb, width), lambda i: (layer, i, 0))],
        out_specs=(pl.BlockSpec((rb, D_PACKED), lambda i: (i, 0)),
                   pl.BlockSpec((rb, LANES), lambda i: (i, 0))),
        compiler_params=_params("parallel"),
        name="pack_w_in",
    )(w_in)


def _pack_layer(layer, norm_gain, w_in, cq_gain, ckv_gain, w_uq, w_ukv, w_q_idx, q_gain, k_gain, conv_w, a_log,
                dt_bias, gdn_gain, w_out):
    w_packed, narrow = _pack_w_in(w_in, layer)
    w_rows = narrow[:, :16].T.astype(BF16)
    return dict(
        gain=norm_gain[None, :], w_packed=w_packed, w_rows=w_rows,
        g_cq=cq_gain[None, :], g_ckv=ckv_gain[None, :],
        w_uqt=w_uq.T.astype(BF16), w_qit=w_q_idx.T.astype(BF16),
        w_ukt=w_ukv[:, :D_ATTN].T.astype(BF16), w_uvt=w_ukv[:, D_ATTN:].T.astype(BF16),
        g_q_col=q_gain[:, None], g_k_col=k_gain[:, None],
        conv_w=conv_w, a_log=a_log[:, None], dt_bias=dt_bias[:, None],
        gdn_gain=gdn_gain[None, :], w_out=w_out.astype(BF16))


def _layer(h2, lw, bias, tri, batch, tp, t_valid, topk, keep):
    rb = _row_block(tp)
    proj, rows = _in_proj(h2, lw["gain"], lw["w_packed"], lw["w_rows"], rb)
    qt, k, vt, qit, kidx = _dsa_prep(proj, lw, batch, tp, rb)
    o_attn = _dsa_attention(qit, rows, qt, kidx, k, vt, bias, tri, batch, tp, t_valid, topk)
    qn, kn, vv, bg = _gdn_prep(proj, rows, lw, batch, tp, rb)
    m_mat, n_mat, p_mat, r_mat, cd = _gdn_chunks(qn, kn, vv, bg)
    o_gdn = _gdn_scan(m_mat, n_mat, p_mat, r_mat, cd, batch, tp).reshape(batch * tp, D_GDN)
    return _out_proj(h2, o_attn, proj, o_gdn, lw["gdn_gain"], lw["w_out"], batch, tp, *keep)


def _forward(x, meta_tokens, rel_bias_table, layer_weights, topk):
    batch, seq, _ = x.shape
    t = seq + N_META
    tp = -(-t // KEY_TILE) * KEY_TILE
    meta = jnp.broadcast_to(meta_tokens[None].astype(x.dtype), (batch, N_META, D_MODEL))
    h = jnp.concatenate([meta, x, jnp.zeros((batch, tp - t, D_MODEL), x.dtype)], axis=1)
    h2 = h.reshape(batch * tp, D_MODEL)
    bias = _bias_tiles(rel_bias_table)
    tri = jnp.tril(jnp.ones((KEY_TILE, KEY_TILE), BF16))
    crop = seq % LANES == 0
    for depth, lw in enumerate(layer_weights):
        last = crop and depth == len(layer_weights) - 1
        h2 = _layer(h2, lw, bias, tri, batch, tp, t, topk, (N_META, seq) if last else (0, tp))
    if crop:
        return h2.reshape(batch, seq, D_MODEL)
    return h2.reshape(batch, tp, D_MODEL)[:, N_META:t]


def kernel(x, meta_tokens, rel_bias_table, norm_gain, w_in, cq_norm_gain, ckv_norm_gain, w_uq, w_ukv, w_q_idx,
           q_norm_gain, k_norm_gain, conv_w, a_log, dt_bias, gdn_norm_gain, w_out):
    depth = norm_gain.shape[0]
    topk = min(TOPK_MAX, x.shape[1] // 4)
    layers = [_pack_layer(l, norm_gain[l], w_in, cq_norm_gain[l], ckv_norm_gain[l], w_uq[l], w_ukv[l],
                          w_q_idx[l], q_norm_gain[l], k_norm_gain[l], conv_w[l], a_log[l], dt_bias[l],
                          gdn_norm_gain[l], w_out[l]) for l in range(depth)]
    return _forward(x, meta_tokens, rel_bias_table, layers, topk)
```

```python
import functools
import math

import jax
import jax.numpy as jnp
from jax import lax
from jax.experimental import pallas as pl
from jax.experimental.pallas import tpu as pltpu

F32 = jnp.float32
BF16 = jnp.bfloat16
I32 = jnp.int32

D_MODEL = 1024
N_META = 16
EPS = 1e-6
N_ATTN_HEADS = 8
ATTN_HEAD_DIM = 64
D_ATTN = N_ATTN_HEADS * ATTN_HEAD_DIM
Q_RANK = 256
KV_RANK = 128
N_IDX_HEADS = 4
IDX_DIM = 64
TOPK_MAX = 256
N_REL_BUCKETS = 32
REL_MAX_DIST = 128
N_GDN_HEADS = 4
GDN_HEAD_DIM = 128
D_GDN = N_GDN_HEADS * GDN_HEAD_DIM
CONV_WIDTH = 4

LANES = 128
KEY_TILE = 256
COUNT_TILE = 512
SCAN_TILE = 1024
FIRST_CHECK_PASSES = 10
INTERPOLATED_PASSES = 24
MAX_SEARCH_PASSES = INTERPOLATED_PASSES + 32
GDN_CHUNKS_PER_STEP = (3, 2)
SCAN_CHUNKS = (6, 4, 3, 2, 1)
ROW_TILES = (6, 5, 4, 3, 2, 1)
HALO_ROWS = 8
D_PACKED = 3 * D_GDN + 512 + D_ATTN + D_GDN
COL_SMALL = 3
COL_Z_ATTN = 4
COL_Z_GDN = 5
MASKED_LOGIT = -1e30
LOG2_E = math.log2(math.e)
KEY_MIN = -2 ** 31
PATTERN_NEG_FLT_MAX = KEY_MIN + (1 << 23)
VMEM_LIMIT = 56 * 1024 * 1024

NT_DIMS = (((1,), (1,)), ((), ()))


def _dot(a, b):
    return jnp.dot(a, b, preferred_element_type=F32)


def _dot_nt(a, b):
    return lax.dot_general(a, b, NT_DIMS, preferred_element_type=F32)


def _split_bf16(x):
    hi = x.astype(BF16)
    return hi, (x - hi.astype(F32)).astype(BF16)


def _dot_split(a_parts, b_parts):
    (a_hi, a_lo), (b_hi, b_lo) = a_parts, b_parts
    return _dot(a_hi, b_hi) + (_dot(a_hi, b_lo) + _dot(a_lo, b_hi))


def _sigmoid(x):
    return 1.0 / (1.0 + jnp.exp(-x))


def _silu(x):
    return x * _sigmoid(x)


def _row_block(tp):
    tiles = tp // LANES
    return LANES * next(d for d in ROW_TILES if tiles % d == 0)


def _params(*sem):
    return pltpu.CompilerParams(dimension_semantics=sem, vmem_limit_bytes=VMEM_LIMIT)


def _bias_kernel(table_ref, out_ref):
    row = lax.broadcasted_iota(I32, (LANES, LANES), 0)
    col = lax.broadcasted_iota(I32, (LANES, LANES), 1)
    max_exact = N_REL_BUCKETS // 2
    for kind in range(3):
        dist = col - row + (2 - kind) * LANES
        n = jnp.maximum(dist, 0)
        nf = jnp.maximum(n, 1).astype(F32)
        large = max_exact + (jnp.log(nf / max_exact) / math.log(REL_MAX_DIST / max_exact)
                             * (N_REL_BUCKETS - max_exact)).astype(I32)
        large = jnp.minimum(large, N_REL_BUCKETS - 1)
        bucket = jnp.where(n < max_exact, n, large)
        for h in range(N_ATTN_HEADS):
            tile = jnp.zeros((LANES, LANES), F32)
            for b in range(N_REL_BUCKETS):
                tile = jnp.where(bucket == b, table_ref[b, h], tile)
            far = table_ref[N_REL_BUCKETS - 1, h]
            out_ref[kind, :, h * LANES:(h + 1) * LANES] = (tile - far) * LOG2_E


def _bias_tiles(rel_table):
    return pl.pallas_call(
        _bias_kernel,
        out_shape=jax.ShapeDtypeStruct((3, LANES, N_ATTN_HEADS * LANES), F32),
        in_specs=[pl.BlockSpec(memory_space=pltpu.SMEM)],
        out_specs=pl.BlockSpec(memory_space=pltpu.VMEM),
        name="rel_bias_tiles",
    )(rel_table)


def _in_proj_kernel(h_ref, gain_ref, w_ref, wrows_ref, proj_ref, rows_ref):
    x = h_ref[...]
    y = x * lax.rsqrt(jnp.mean(x * x, axis=-1, keepdims=True) + EPS)
    hn = (y * gain_ref[...]).astype(BF16)
    proj_ref[...] = _dot(hn, w_ref[...])
    rows_ref[...] = _dot_nt(wrows_ref[...], hn)


def _in_proj(h2, gain, w_packed, w_rows, rb):
    n_rows = h2.shape[0]
    grid = (n_rows // rb,)
    return pl.pallas_call(
        _in_proj_kernel,
        out_shape=(jax.ShapeDtypeStruct((n_rows, D_PACKED), F32),
                   jax.ShapeDtypeStruct((16, n_rows), F32)),
        grid=grid,
        in_specs=[pl.BlockSpec((rb, D_MODEL), lambda i: (i, 0)),
                  pl.BlockSpec((1, D_MODEL), lambda i: (0, 0)),
                  pl.BlockSpec((D_MODEL, D_PACKED), lambda i: (0, 0)),
                  pl.BlockSpec((16, D_MODEL), lambda i: (0, 0))],
        out_specs=(pl.BlockSpec((rb, D_PACKED), lambda i: (i, 0)),
                   pl.BlockSpec((16, rb), lambda i: (0, i))),
        compiler_params=_params("parallel"),
        name="in_proj",
    )(h2, gain, w_packed, w_rows)


def _rms_rows(x, gain):
    return x * lax.rsqrt(jnp.mean(x * x, axis=-1, keepdims=True) + EPS) * gain


def _dsa_prep_kernel(sm_ref, gcq_ref, gckv_ref, wuqt_ref, wqit_ref, wukt_ref, wuvt_ref, gq_ref, gk_ref,
                     qt_ref, k_ref, vt_ref, qit_ref, kidx_ref):
    sm = sm_ref[...]
    rb = sm.shape[0]
    cq = _rms_rows(sm[:, :Q_RANK], gcq_ref[...]).astype(BF16)
    ckv = _rms_rows(sm[:, Q_RANK:Q_RANK + KV_RANK], gckv_ref[...]).astype(BF16)
    kidx_ref[...] = sm[:, Q_RANK + KV_RANK:].astype(BF16)
    q3 = _dot_nt(wuqt_ref[...], cq).reshape(N_ATTN_HEADS, ATTN_HEAD_DIM, rb)
    q3 = q3 * lax.rsqrt(jnp.mean(q3 * q3, axis=1, keepdims=True) + EPS) * gq_ref[...][None]
    qt_ref[...] = (q3 * (ATTN_HEAD_DIM ** -0.5 * LOG2_E)).reshape(D_ATTN, rb).astype(BF16)
    k3 = _dot_nt(wukt_ref[...], ckv).reshape(N_ATTN_HEADS, ATTN_HEAD_DIM, rb)
    k3 = k3 * lax.rsqrt(jnp.mean(k3 * k3, axis=1, keepdims=True) + EPS) * gk_ref[...][None]
    k_ref[...] = k3.reshape(D_ATTN, rb).T.astype(BF16)
    vt_ref[...] = _dot_nt(wuvt_ref[...], ckv).astype(BF16)
    qit_ref[...] = _dot_nt(wqit_ref[...], cq).astype(BF16)


def _dsa_prep(proj, lw, batch, tp, rb):
    n_rows = proj.shape[0]
    nb = tp // rb
    d_idx = N_IDX_HEADS * IDX_DIM
    const = lambda shape: pl.BlockSpec(shape, lambda b, i: (0, 0))
    row_spec = lambda width: pl.BlockSpec((rb, width), lambda b, i: (b * nb + i, 0))
    col_spec = lambda height: pl.BlockSpec((None, height, rb), lambda b, i: (b, 0, i))
    return pl.pallas_call(
        _dsa_prep_kernel,
        out_shape=(jax.ShapeDtypeStruct((batch, D_ATTN, tp), BF16),
                   jax.ShapeDtypeStruct((n_rows, D_ATTN), BF16),
                   jax.ShapeDtypeStruct((batch, D_ATTN, tp), BF16),
                   jax.ShapeDtypeStruct((batch, d_idx, tp), BF16),
                   jax.ShapeDtypeStruct((n_rows, LANES), BF16)),
        grid=(batch, nb),
        in_specs=[pl.BlockSpec((rb, 512), lambda b, i: (b * nb + i, COL_SMALL)),
                  const((1, Q_RANK)), const((1, KV_RANK)),
                  const((D_ATTN, Q_RANK)), const((d_idx, Q_RANK)),
                  const((D_ATTN, KV_RANK)), const((D_ATTN, KV_RANK)),
                  const((ATTN_HEAD_DIM, 1)), const((ATTN_HEAD_DIM, 1))],
        out_specs=(col_spec(D_ATTN), row_spec(D_ATTN), col_spec(D_ATTN), col_spec(d_idx), row_spec(LANES)),
        compiler_params=_params("parallel", "parallel"),
        name="dsa_prep",
    )(proj, lw["g_cq"], lw["g_ckv"], lw["w_uqt"], lw["w_qit"], lw["w_ukt"], lw["w_uvt"], lw["g_q_col"],
      lw["g_k_col"])


def _dsa_block(i, qit_ref, rows_ref, qt_ref, kidx_ref, k_ref, vt_ref, bias_ref, tri_ref, o_ref,
               score_scr, wi_scr, wq_scr, s_scr, p_scr, m_scr, l_scr, acc_scr, alpha_scr, mask_scr, tie_scr, *, topk):
    t0 = i * LANES
    n_kt = i // 2 + 1
    n_st = (n_kt + 3) // 4
    hd = ATTN_HEAD_DIM
    pair_w = 2 * LANES
    n_pairs = N_ATTN_HEADS // 2

    zeros_hd = jnp.zeros((hd, LANES), BF16)
    for h in range(N_IDX_HEADS):
        wi_scr[0:IDX_DIM, h * LANES:(h + 1) * LANES] = qit_ref[h * IDX_DIM:(h + 1) * IDX_DIM, :]
    wi_scr[IDX_DIM:, :] = jnp.zeros((LANES - IDX_DIM, N_IDX_HEADS * LANES), BF16)
    for p in range(n_pairs):
        wq_scr[p, 0:hd, 0:LANES] = qt_ref[2 * p * hd:(2 * p + 1) * hd, :]
        wq_scr[p, 0:hd, LANES:] = zeros_hd
        wq_scr[p, hd:, 0:LANES] = zeros_hd
        wq_scr[p, hd:, LANES:] = qt_ref[(2 * p + 1) * hd:(2 * p + 2) * hd, :]

    row = lax.broadcasted_iota(I32, (KEY_TILE, LANES), 0)
    col = lax.broadcasted_iota(I32, (KEY_TILE, LANES), 1)
    w_idx = rows_ref[0:N_IDX_HEADS, :] * (N_IDX_HEADS ** -0.5 * IDX_DIM ** -0.5)

    def key_tile(j):
        return pl.multiple_of(j * KEY_TILE, KEY_TILE)

    def causal(j):
        return (j * KEY_TILE + row) <= (t0 + col)

    def fold8(x, op):
        return op(x.reshape(KEY_TILE // 8, 8, LANES), axis=0)

    def score_step(jc, carry, masked):
        top, bottom, n_nonneg, n_pos = carry
        subs = [jc * (SCAN_TILE // KEY_TILE) + sub for sub in range(SCAN_TILE // KEY_TILE)]
        logits = [_dot(kidx_ref[pl.ds(key_tile(jnp.minimum(j, n_kt - 1)), KEY_TILE), :], wi_scr[...])
                  for j in subs]
        for j, lg in zip(subs, logits):
            score = jnp.zeros((KEY_TILE, LANES), F32)
            for h in range(N_IDX_HEADS):
                score = score + jnp.maximum(lg[:, h * LANES:(h + 1) * LANES], 0.0) * w_idx[h:h + 1, :]
            if masked:
                visible = causal(j)
                seen = jnp.where(visible, score, -jnp.inf)
                bottom = jnp.minimum(bottom, fold8(jnp.where(visible, score, jnp.inf), jnp.min))
            else:
                seen = score
                bottom = jnp.minimum(bottom, fold8(score, jnp.min))
            top = jnp.maximum(top, fold8(seen, jnp.max))
            n_nonneg = n_nonneg + fold8(jnp.where(seen >= 0.0, 1, 0), jnp.sum)
            n_pos = n_pos + fold8(jnp.where(seen > 0.0, 1, 0), jnp.sum)
            score_scr[pl.ds(key_tile(j), KEY_TILE), :] = seen
        return top, bottom, n_nonneg, n_pos

    zeros8 = jnp.zeros((8, LANES), I32)
    carry = (jnp.full((8, LANES), -jnp.inf, F32), jnp.full((8, LANES), jnp.inf, F32), zeros8, zeros8)
    carry = lax.fori_loop(0, n_st - 1, functools.partial(score_step, masked=False), carry)
    top, bottom, n_nonneg, n_pos = score_step(n_st - 1, carry, masked=True)
    top = jnp.max(top, axis=0, keepdims=True)
    bottom = jnp.min(bottom, axis=0, keepdims=True)
    count0 = jnp.sum(n_nonneg, axis=0, keepdims=True)
    count_pos = jnp.sum(n_pos, axis=0, keepdims=True)

    def count_f32(cand, below=None):
        def body(j, carry):
            acc, best = carry
            for part in range(SCAN_TILE // COUNT_TILE):
                start = pl.multiple_of(j * SCAN_TILE + part * COUNT_TILE, COUNT_TILE)
                x = score_scr[pl.ds(start, COUNT_TILE), :]
                ind = jnp.where(x >= cand, 1, 0)
                acc = acc + jnp.sum(ind.reshape(COUNT_TILE // 8, 8, LANES), axis=0)
                if below is not None:
                    under = jnp.where(x < below, x, -jnp.inf)
                    best = jnp.maximum(best, jnp.max(under.reshape(COUNT_TILE // 8, 8, LANES), axis=0))
            return acc, best
        acc, best = lax.fori_loop(0, n_st, body, (jnp.zeros((8, LANES), I32), jnp.full((8, LANES), -jnp.inf, F32)))
        count = jnp.sum(acc, axis=0, keepdims=True)
        if below is None:
            return count
        return count, jnp.max(best, axis=0, keepdims=True)

    def to_pattern(v):
        bits = lax.bitcast_convert_type(v, I32)
        return bits ^ ((bits >> 31) & 0x7FFFFFFF)

    def to_f32(c):
        return lax.bitcast_convert_type(c ^ ((c >> 31) & 0x7FFFFFFF), F32)

    n_visible = t0 + 1 + lax.broadcasted_iota(I32, (1, LANES), 1)
    nonneg = count0 >= topk
    lo = jnp.where(nonneg, 0, to_pattern(bottom))
    hi = jnp.where(nonneg, to_pattern(top) + 1, 0)
    count_lo = jnp.where(nonneg, count0, n_visible)
    count_hi = jnp.where(nonneg, 0, count0)
    few = n_visible < topk
    lo = jnp.where(few, PATTERN_NEG_FLT_MAX, lo)
    zero_tied = nonneg & (count_pos < topk)
    hi = jnp.where(zero_tied, 1, hi)
    count_hi = jnp.where(zero_tied, count_pos, count_hi)
    open_q = jnp.where(few | zero_tied | (count_lo == topk), 0, 1)

    log_topk = math.log(topk)

    def count_error(count):
        return jnp.log(count.astype(F32) + 0.5) - log_topk

    def probe(n_pass, carry, extract):
        lo, hi, count_lo, count_hi, err_lo, err_hi, last_side, open_q = carry
        v_lo, v_hi = to_f32(lo), to_f32(hi)
        frac = err_lo / (err_lo - err_hi)
        frac = jnp.where(count_lo - count_hi <= 4, 0.5, frac)
        guess = to_pattern(v_lo + (v_hi - v_lo) * frac)
        middle = lo + lax.shift_right_logical(hi - lo, 1)
        cand = jnp.where(n_pass >= INTERPOLATED_PASSES, middle, guess)
        cand = jnp.minimum(jnp.maximum(cand, lo + 1), hi - 1)
        is_open = open_q == 1
        if extract:
            count, under_hi = count_f32(to_f32(cand), below=v_hi)
            next_below = to_pattern(under_hi)
            found = is_open & (count_hi == topk - 1)
            is_open = is_open & jnp.logical_not(found)
        else:
            count = count_f32(to_f32(cand))
        raise_lo = is_open & (count >= topk)
        lower_hi = is_open & (count < topk)
        err = count_error(count)
        err_hi = jnp.where(raise_lo & (last_side == 1), err_hi * 0.5, err_hi)
        err_lo = jnp.where(lower_hi & (last_side == -1), err_lo * 0.5, err_lo)
        err_lo = jnp.where(raise_lo, err, err_lo)
        err_hi = jnp.where(lower_hi, err, err_hi)
        lo = jnp.where(raise_lo, cand, lo)
        count_lo = jnp.where(raise_lo, count, count_lo)
        hi = jnp.where(lower_hi, cand, hi)
        count_hi = jnp.where(lower_hi, count, count_hi)
        last_side = jnp.where(raise_lo, 1, jnp.where(lower_hi, -1, last_side))
        if extract:
            lo = jnp.where(found, next_below, lo)
            hi = jnp.where(found | raise_lo, next_below + 1, hi)
        closed = (count_lo == topk) | (hi - lo == 1)
        return lo, hi, count_lo, count_hi, err_lo, err_hi, last_side, jnp.where(closed, 0, open_q)

    carry = (lo, hi, count_lo, count_hi, count_error(count_lo), count_error(count_hi),
             jnp.zeros((1, LANES), I32), open_q)
    n_first = jnp.where(jnp.sum(open_q) > 0, FIRST_CHECK_PASSES, 0)
    carry = lax.fori_loop(0, n_first, functools.partial(probe, extract=False), carry)

    def probes_left(st):
        n_pass, n_open = st[0], st[-1]
        return (n_pass < MAX_SEARCH_PASSES) & (n_open > 0)

    def extracting_probe(st):
        carry = probe(st[0], st[1:-1], extract=True)
        return (st[0] + 1,) + carry + (jnp.sum(carry[-1]),)

    state = lax.while_loop(probes_left, extracting_probe,
                           (jnp.int32(FIRST_CHECK_PASSES),) + carry + (jnp.sum(carry[-1]),))
    lo, count_lo, count_hi = state[1], state[3], state[4]
    tau = to_f32(lo)
    need = jnp.where((count_lo == topk) | few, topk, topk - count_hi).astype(F32)

    m_scr[...] = jnp.full(m_scr.shape, MASKED_LOGIT, F32)
    l_scr[...] = jnp.zeros(l_scr.shape, F32)
    acc_scr[...] = jnp.zeros(acc_scr.shape, F32)
    ones_rows = jnp.ones((16, KEY_TILE), BF16)
    last = n_kt - 1

    def mask_pair(j_first, parity):
        xs = [score_scr[pl.ds(key_tile(jnp.minimum(j_first + slot, last)), KEY_TILE), :] for slot in range(2)]
        ties = [x == tau for x in xs]
        tie_cols = jnp.concatenate([jnp.where(tie, 1.0, 0.0).astype(BF16) for tie in ties], axis=1)
        ranks = _dot(tri_ref[...], tie_cols)
        tie_carry = tie_scr[0:1, :]
        for slot, (x, tie) in enumerate(zip(xs, ties)):
            rank = ranks[:, slot * LANES:(slot + 1) * LANES] + tie_carry
            tie_carry = rank[KEY_TILE - 1:KEY_TILE, :]
            take = (tie & (rank <= need)) | (x > tau)
            mask_scr[parity, slot] = jnp.where(take, 0.0, MASKED_LOGIT)
        tie_scr[0:1, :] = tie_carry

    def qk_pair(slot, j, p):
        s_scr[slot, :, p * pair_w:(p + 1) * pair_w] = _dot(
            k_ref[pl.ds(key_tile(j), KEY_TILE), p * LANES:(p + 1) * LANES], wq_scr[p])

    def softmax_pair(slot, j, p, near, parity):
        mask_add = mask_scr[parity, slot]
        alphas = []
        for h in (2 * p, 2 * p + 1):
            cols = slice(h * LANES, (h + 1) * LANES)
            logits = s_scr[slot, :, cols] + mask_add
            if near:
                kind_top = jnp.clip(2 * j - i + 2, 0, 2)
                kind_bot = jnp.clip(2 * j - i + 3, 0, 2)
                logits = logits + jnp.concatenate(
                    [bias_ref[kind_top, :, cols], bias_ref[kind_bot, :, cols]], axis=0)
            m_old = m_scr[h:h + 1, :]
            m_new = jnp.maximum(m_old, jnp.max(logits, axis=0, keepdims=True))
            m_scr[h:h + 1, :] = m_new
            p_scr[slot, :, cols] = jnp.exp2(logits - m_new).astype(BF16)
            alphas.append(jnp.exp2(m_old - m_new))
        return alphas

    def pv_pair(slot, j, p, alphas):
        lhs = jnp.concatenate([vt_ref[p * 2 * hd:(p + 1) * 2 * hd, pl.ds(key_tile(j), KEY_TILE)], ones_rows],
                              axis=0)
        out = _dot(lhs, p_scr[slot, :, p * pair_w:(p + 1) * pair_w])
        for half in range(2):
            h = 2 * p + half
            rows_h = slice(h * hd, (h + 1) * hd)
            q_cols = slice(half * LANES, (half + 1) * LANES)
            acc_scr[rows_h, :] = acc_scr[rows_h, :] * alphas[half] + out[half * hd:(half + 1) * hd, q_cols]
            l_scr[h:h + 1, :] = l_scr[h:h + 1, :] * alphas[half] + out[2 * hd:2 * hd + 1, q_cols]

    def pending_alphas(p):
        return [alpha_scr[h:h + 1, :] for h in (2 * p, 2 * p + 1)]

    def clear_pending():
        p_scr[1] = jnp.zeros(p_scr.shape[1:], BF16)
        alpha_scr[...] = jnp.ones(alpha_scr.shape, F32)

    def pair_step(ja, j_pending, j_next, near, parity):
        alphas_a = []
        for p in range(n_pairs):
            pv_pair(1, j_pending, p, pending_alphas(p))
            qk_pair(1, ja + 1, p)
            alphas_a.append(softmax_pair(0, ja, p, near, parity))
        for p in range(n_pairs):
            pv_pair(0, ja, p, alphas_a[p])
            qk_pair(0, j_next, p)
            alphas_b = softmax_pair(1, ja + 1, p, near, parity)
            for half in range(2):
                alpha_scr[2 * p + half:2 * p + half + 1, :] = alphas_b[half]
        mask_pair(ja + 2, 1 - parity)

    def single_step(ja, j_pending, near, parity):
        alphas_a = []
        for p in range(n_pairs):
            pv_pair(1, j_pending, p, pending_alphas(p))
            alphas_a.append(softmax_pair(0, ja, p, near, parity))
        for p in range(n_pairs):
            pv_pair(0, ja, p, alphas_a[p])
        clear_pending()

    n_far = 2 * (jnp.maximum(n_kt - 2, 0) // 2)
    n_near = n_kt - n_far
    first_near_parity = (n_far // 2) % 2
    clear_pending()
    tie_scr[...] = jnp.zeros(tie_scr.shape, F32)
    mask_pair(0, 0)
    for p in range(n_pairs):
        qk_pair(0, 0, p)

    def far_body(jp, carry):
        pair_step(2 * jp, jnp.maximum(2 * jp - 1, 0), 2 * jp + 2, near=False, parity=jp % 2)
        return carry

    lax.fori_loop(0, n_far // 2, far_body, 0)

    @pl.when(n_near >= 2)
    def _():
        pair_step(n_far, jnp.maximum(n_far - 1, 0), jnp.minimum(n_far + 2, last), near=True,
                  parity=first_near_parity)

    @pl.when(n_near % 2 == 1)
    def _():
        single_step(last, jnp.where(n_near == 3, n_far + 1, jnp.maximum(n_far - 1, 0)), near=True,
                    parity=jnp.where(n_near == 3, 1 - first_near_parity, first_near_parity))

    for p in range(n_pairs):
        pv_pair(1, last, p, pending_alphas(p))

    for h in range(N_ATTN_HEADS):
        rows_h = slice(h * hd, (h + 1) * hd)
        acc_scr[rows_h, :] = acc_scr[rows_h, :] / l_scr[h:h + 1, :]
    o_ref[...] = acc_scr[...].T


def _dsa_kernel(*refs, topk, t_valid):
    o_ref = refs[8]
    i = pl.program_id(1)
    is_real = i * LANES < t_valid

    @pl.when(is_real)
    def _():
        _dsa_block(i, *refs, topk=topk)

    @pl.when(jnp.logical_not(is_real))
    def _():
        o_ref[...] = jnp.zeros(o_ref.shape, F32)


def _dsa_attention(qit, rows, qt, kidx, k, vt, bias, tri, batch, tp, t_valid, topk):
    n_rows = k.shape[0]
    nqb = tp // LANES
    d_idx = N_IDX_HEADS * IDX_DIM
    n_pairs = N_ATTN_HEADS // 2
    key_rows = -(-tp // SCAN_TILE) * SCAN_TILE
    q_cols = lambda height: pl.BlockSpec((None, height, LANES), lambda b, i: (b, 0, i))
    return pl.pallas_call(
        functools.partial(_dsa_kernel, topk=topk, t_valid=t_valid),
        out_shape=jax.ShapeDtypeStruct((n_rows, D_ATTN), F32),
        grid=(batch, nqb),
        in_specs=[q_cols(d_idx),
                  pl.BlockSpec((16, LANES), lambda b, i: (0, b * nqb + i)),
                  q_cols(D_ATTN),
                  pl.BlockSpec((tp, LANES), lambda b, i: (b, 0)),
                  pl.BlockSpec((tp, D_ATTN), lambda b, i: (b, 0)),
                  pl.BlockSpec((None, D_ATTN, tp), lambda b, i: (b, 0, 0)),
                  pl.BlockSpec((3, LANES, N_ATTN_HEADS * LANES), lambda b, i: (0, 0, 0)),
                  pl.BlockSpec((KEY_TILE, KEY_TILE), lambda b, i: (0, 0))],
        out_specs=pl.BlockSpec((LANES, D_ATTN), lambda b, i: (b * nqb + i, 0)),
        scratch_shapes=[pltpu.VMEM((key_rows, LANES), F32),
                        pltpu.VMEM((LANES, N_IDX_HEADS * LANES), BF16),
                        pltpu.VMEM((n_pairs, LANES, 2 * LANES), BF16),
                        pltpu.VMEM((2, KEY_TILE, N_ATTN_HEADS * LANES), F32),
                        pltpu.VMEM((2, KEY_TILE, N_ATTN_HEADS * LANES), BF16),
                        pltpu.VMEM((N_ATTN_HEADS, LANES), F32),
                        pltpu.VMEM((N_ATTN_HEADS, LANES), F32),
                        pltpu.VMEM((D_ATTN, LANES), F32),
                        pltpu.VMEM((N_ATTN_HEADS, LANES), F32),
                        pltpu.VMEM((2, 2, KEY_TILE, LANES), F32),
                        pltpu.VMEM((8, LANES), F32)],
        compiler_params=_params("parallel", "parallel"),
        name="dsa_attention",
    )(qit, rows, qt, kidx, k, vt, bias, tri)


def _gdn_prep_kernel(x_ref, halo_ref, cw_ref, rows_ref, alog_ref, dtb_ref, q_ref, k_ref, v_ref, bg_ref, buf):
    first = pl.program_id(1) == 0
    rb = x_ref.shape[0]
    buf[0:HALO_ROWS, :] = jnp.where(first, 0.0, halo_ref[...])
    buf[HALO_ROWS:, :] = x_ref[...]
    acc = jnp.zeros((rb, 3 * D_GDN), F32)
    for tap in range(CONV_WIDTH):
        start = HALO_ROWS - (CONV_WIDTH - 1) + tap
        acc = acc + cw_ref[tap:tap + 1, :] * buf[start:start + rb, :]
    y = _silu(acc)
    for h in range(N_GDN_HEADS):
        cols = slice(h * GDN_HEAD_DIM, (h + 1) * GDN_HEAD_DIM)
        qh = y[:, cols]
        kh = y[:, D_GDN + h * GDN_HEAD_DIM:D_GDN + (h + 1) * GDN_HEAD_DIM]
        q_ref[:, cols] = (qh * lax.rsqrt(jnp.sum(qh * qh, axis=-1, keepdims=True) + EPS)
                          * (GDN_HEAD_DIM ** -0.5))
        k_ref[:, cols] = kh * lax.rsqrt(jnp.sum(kh * kh, axis=-1, keepdims=True) + EPS)
    v_ref[...] = y[:, 2 * D_GDN:]
    rows = rows_ref[...]
    beta = _sigmoid(rows[4:8, :])
    a = rows[8:12, :] + dtb_ref[...]
    softplus = jnp.maximum(a, 0.0) + jnp.log1p(jnp.exp(-jnp.abs(a)))
    bg_ref[0:4, :] = beta
    bg_ref[4:8, :] = -jnp.exp(alog_ref[...]) * softplus


def _gdn_prep(proj, rows, lw, batch, tp, rb):
    n_rows = proj.shape[0]
    nb = tp // rb
    halo_per_block = rb // HALO_ROWS
    row_spec = pl.BlockSpec((rb, D_GDN), lambda b, i: (b * nb + i, 0))
    return pl.pallas_call(
        _gdn_prep_kernel,
        out_shape=(jax.ShapeDtypeStruct((n_rows, D_GDN), F32),) * 3
        + (jax.ShapeDtypeStruct((8, n_rows), F32),),
        grid=(batch, nb),
        in_specs=[pl.BlockSpec((rb, 3 * D_GDN), lambda b, i: (b * nb + i, 0)),
                  pl.BlockSpec((HALO_ROWS, 3 * D_GDN),
                               lambda b, i: (jnp.maximum((b * nb + i) * halo_per_block - 1, 0), 0)),
                  pl.BlockSpec((CONV_WIDTH, 3 * D_GDN), lambda b, i: (0, 0)),
                  pl.BlockSpec((16, rb), lambda b, i: (0, b * nb + i)),
                  pl.BlockSpec((N_GDN_HEADS, 1), lambda b, i: (0, 0)),
                  pl.BlockSpec((N_GDN_HEADS, 1), lambda b, i: (0, 0))],
        out_specs=(row_spec, row_spec, row_spec,
                   pl.BlockSpec((8, rb), lambda b, i: (0, b * nb + i))),
        scratch_shapes=[pltpu.VMEM((HALO_ROWS + rb, 3 * D_GDN), F32)],
        compiler_params=_params("parallel", "parallel"),
        name="gdn_prep",
    )(proj, proj, lw["conv_w"], rows, lw["a_log"], lw["dt_bias"])


def _gdn_chunk_kernel(q_ref, k_ref, v_ref, bg_ref, m_ref, n_ref, p_ref, r_ref, cd_ref):
    c = LANES
    n_chunks = q_ref.shape[0] // c
    items = [(ch, h) for ch in range(n_chunks) for h in range(N_GDN_HEADS)]
    idx = range(len(items))
    row = lax.broadcasted_iota(I32, (c, c), 0)
    col = lax.broadcasted_iota(I32, (c, c), 1)
    tri = row >= col
    strict = row > col
    eye = jnp.where(row == col, 1.0, 0.0)
    lane8 = lax.broadcasted_iota(I32, (8, c), 1)
    gates, decays = [], []
    for ch in range(n_chunks):
        bg = bg_ref[:, ch * c:(ch + 1) * c]
        dec = bg
        shift = 1
        while shift < c:
            dec = dec + jnp.where(lane8 >= shift, pltpu.roll(dec, shift, 1), 0.0)
            shift *= 2
        gates.append(bg)
        decays.append(dec)

    def tokens(ref, n):
        ch, h = items[n]
        return ref[ch * c:(ch + 1) * c, h * GDN_HEAD_DIM:(h + 1) * GDN_HEAD_DIM]

    d_row = [jnp.broadcast_to(decays[ch][4 + h:5 + h, :], (c, c)) for ch, h in items]
    d_col = [d.T for d in d_row]
    beta_col = [jnp.broadcast_to(gates[ch][h:h + 1, :], (c, c)).T for ch, h in items]
    d_last = [d[:, c - 1:c] for d in d_row]
    gamma = [jnp.exp(jnp.where(tri, d_col[n] - d_row[n], MASKED_LOGIT)) for n in idx]
    exp_d = [jnp.exp(d_col[n]) for n in idx]
    k16 = [tokens(k_ref, n).astype(BF16) for n in idx]
    kb = [tokens(k_ref, n) * beta_col[n] for n in idx]
    nil = [jnp.where(strict, _dot_nt(kb[n].astype(BF16), k16[n]) * gamma[n], 0.0) for n in idx]
    block = 8
    same = lambda size: (row // size) == (col // size)
    diag = [jnp.where(same(block), x, 0.0) for x in nil]
    diag_parts = [_split_bf16(x) for x in diag]
    inv = [eye - x for x in diag]
    power = [_dot_split(x, x) for x in diag_parts]
    for it in range(2):
        power_parts = [_split_bf16(x) for x in power]
        inv = [inv[n] + _dot_split(_split_bf16(inv[n]), power_parts[n]) for n in idx]
        if it == 0:
            power = [_dot_split(x, x) for x in power_parts]
    while block < c:
        couples = same(2 * block) & ((row // block) % 2 == 1) & ((col // block) % 2 == 0)
        inv_parts = [_split_bf16(x) for x in inv]
        lower = [_dot_split(_split_bf16(jnp.where(couples, nil[n], 0.0)), inv_parts[n]) for n in idx]
        inv = [inv[n] - _dot_split(inv_parts[n], _split_bf16(lower[n])) for n in idx]
        block *= 2
    rhs = [jnp.concatenate([kb[n] * exp_d[n], tokens(v_ref, n) * beta_col[n]], axis=1) for n in idx]
    wu = [_dot_split(_split_bf16(inv[n]), _split_bf16(rhs[n])).astype(BF16) for n in idx]
    aqk = [jnp.where(tri, _dot_nt(tokens(q_ref, n).astype(BF16), k16[n]) * gamma[n], 0.0).astype(BF16)
           for n in idx]
    kd_t = [(tokens(k_ref, n) * jnp.exp(d_last[n] - d_col[n])).T.astype(BF16) for n in idx]
    state_wu = [_dot(kd_t[n], wu[n]) for n in idx]
    out_wu = [_dot(aqk[n], wu[n]) for n in idx]
    for n, (ch, h) in enumerate(items):
        m_ref[ch, h] = (-state_wu[n][:, :c]).astype(BF16)
        n_ref[ch, h] = state_wu[n][:, c:]
        p_ref[ch, h] = (tokens(q_ref, n) * exp_d[n] - out_wu[n][:, :c]).astype(BF16)
        r_ref[ch, h] = out_wu[n][:, c:]
        cd_ref[ch, h:h + 1, :] = jnp.exp(d_last[n][0:1, :] + jnp.zeros((1, c), F32))


def _gdn_chunks(qn, kn, vv, bg):
    n_rows = qn.shape[0]
    nc = n_rows // LANES
    per_step = next(d for d in GDN_CHUNKS_PER_STEP if nc % d == 0)
    tok = pl.BlockSpec((per_step * LANES, D_GDN), lambda c: (c, 0))
    mat = pl.BlockSpec((per_step, N_GDN_HEADS, LANES, LANES), lambda c: (c, 0, 0, 0))
    mat_shape = lambda dt: jax.ShapeDtypeStruct((nc, N_GDN_HEADS, LANES, LANES), dt)
    return pl.pallas_call(
        _gdn_chunk_kernel,
        out_shape=(mat_shape(BF16), mat_shape(F32), mat_shape(BF16), mat_shape(F32),
                   jax.ShapeDtypeStruct((nc, N_GDN_HEADS, LANES), F32)),
        grid=(nc // per_step,),
        in_specs=[tok, tok, tok, pl.BlockSpec((8, per_step * LANES), lambda c: (0, c))],
        out_specs=(mat, mat, mat, mat, pl.BlockSpec((per_step, N_GDN_HEADS, LANES), lambda c: (c, 0, 0))),
        compiler_params=_params("parallel"),
        name="gdn_chunks",
    )(qn, kn, vv, bg)


def _gdn_scan_kernel(m_ref, n_ref, p_ref, r_ref, cd_ref, o_ref, s_scr, *, batch):
    @pl.when(pl.program_id(0) == 0)
    def _():
        s_scr[...] = jnp.zeros(s_scr.shape, F32)

    for ch in range(m_ref.shape[1]):
        rows = slice(ch * LANES, (ch + 1) * LANES)
        for b in range(batch):
            for h in range(N_GDN_HEADS):
                s = s_scr[b, h]
                s16 = s.astype(BF16)
                o_ref[b, rows, h * GDN_HEAD_DIM:(h + 1) * GDN_HEAD_DIM] = _dot(p_ref[b, ch, h], s16) + r_ref[b, ch, h]
                s_scr[b, h] = s * cd_ref[b, ch, h:h + 1, :] + _dot(m_ref[b, ch, h], s16) + n_ref[b, ch, h]


def _gdn_scan(m_mat, n_mat, p_mat, r_mat, cd, batch, tp):
    nc = tp // LANES
    per_step = next(d for d in SCAN_CHUNKS if nc % d == 0)
    shape5 = lambda a: a.reshape(batch, nc, N_GDN_HEADS, LANES, LANES)
    mat = pl.BlockSpec((batch, per_step, N_GDN_HEADS, LANES, LANES), lambda c: (0, c, 0, 0, 0))
    return pl.pallas_call(
        functools.partial(_gdn_scan_kernel, batch=batch),
        out_shape=jax.ShapeDtypeStruct((batch, tp, D_GDN), F32),
        grid=(nc // per_step,),
        in_specs=[mat, mat, mat, mat,
                  pl.BlockSpec((batch, per_step, N_GDN_HEADS, LANES), lambda c: (0, c, 0, 0))],
        out_specs=pl.BlockSpec((batch, per_step * LANES, D_GDN), lambda c: (0, c, 0)),
        scratch_shapes=[pltpu.VMEM((batch, N_GDN_HEADS, LANES, LANES), F32)],
        compiler_params=_params("arbitrary"),
        name="gdn_scan",
    )(shape5(m_mat), shape5(n_mat), shape5(p_mat), shape5(r_mat), cd.reshape(batch, nc, N_GDN_HEADS, LANES))


def _out_proj_kernel(h_ref, oa_ref, za_ref, og_ref, zg_ref, gain_ref, w_ref, out_ref):
    attn = (oa_ref[...] * _silu(za_ref[...])).astype(BF16)
    y = _dot(attn, w_ref[0:D_ATTN, :])
    og, zg = og_ref[...], zg_ref[...]
    for h in range(N_GDN_HEADS):
        cols = slice(h * GDN_HEAD_DIM, (h + 1) * GDN_HEAD_DIM)
        gated = (_rms_rows(og[:, cols], gain_ref[...]) * _silu(zg[:, cols])).astype(BF16)
        y = y + _dot(gated, w_ref[D_ATTN + h * GDN_HEAD_DIM:D_ATTN + (h + 1) * GDN_HEAD_DIM, :])
    out_ref[...] = h_ref[...] + y


def _out_proj(h2, o_attn, proj, o_gdn, gain, w_out, batch, tp, first, rows):
    rb = _row_block(rows)
    n_blocks = rows // rb
    blk = lambda width, col: pl.BlockSpec((pl.Element(rb), pl.Element(width)),
                                          lambda b, j: (pl.multiple_of(b * tp + first + j * rb, HALO_ROWS), col * width))
    return pl.pallas_call(
        _out_proj_kernel,
        out_shape=jax.ShapeDtypeStruct((batch * rows, D_MODEL), F32),
        grid=(batch, n_blocks),
        in_specs=[blk(D_MODEL, 0), blk(D_ATTN, 0), blk(D_ATTN, COL_Z_ATTN), blk(D_GDN, 0),
                  blk(D_GDN, COL_Z_GDN),
                  pl.BlockSpec((1, GDN_HEAD_DIM), lambda b, j: (0, 0)),
                  pl.BlockSpec((D_MODEL, D_MODEL), lambda b, j: (0, 0))],
        out_specs=pl.BlockSpec((rb, D_MODEL), lambda b, j: (b * n_blocks + j, 0)),
        compiler_params=_params("parallel", "parallel"),
        name="out_proj",
    )(h2, o_attn, proj, o_gdn, proj, gain, w_out)


W_IN_PARTS = (("c_q", Q_RANK), ("c_kv", KV_RANK), ("k_idx", IDX_DIM), ("w_idx", N_IDX_HEADS),
              ("z_attn", D_ATTN), ("qkv_g", 3 * D_GDN), ("z_g", D_GDN), ("b", N_GDN_HEADS), ("a", N_GDN_HEADS))
W_PACKED_ORDER = ("qkv_g", "c_q", "c_kv", "k_idx", "k_idx", "z_attn", "z_g")


def _w_in_columns():
    cols, o = {}, 0
    for name, size in W_IN_PARTS:
        cols[name] = slice(o, o + size)
        o += size
    return cols, o


W_ROWS_ORDER = ("w_idx", "b", "a")


def _pack_w_in_kernel(w_ref, out_ref, narrow_ref):
    cols, _ = _w_in_columns()
    narrow_ref[...] = jnp.zeros(narrow_ref.shape, F32)
    o = 0
    for name in W_ROWS_ORDER:
        src = cols[name]
        size = src.stop - src.start
        narrow_ref[:, o:o + size] = w_ref[:, src]
        o += size
    o = 0
    for name in W_PACKED_ORDER:
        src = cols[name]
        size = src.stop - src.start
        out_ref[:, o:o + size] = w_ref[:, src].astype(BF16)
        o += size


def _pack_w_in(w_in, layer):
    _, width = _w_in_columns()
    rb = 2 * LANES
    return pl.pallas_call(
        _pack_w_in_kernel,
        out_shape=(jax.ShapeDtypeStruct((D_MODEL, D_PACKED), BF16),
                   jax.ShapeDtypeStruct((D_MODEL, LANES), F32)),
        grid=(D_MODEL // rb,),
        in_specs=[pl.BlockSpec((None, rb, width), lambda i: (layer, i, 0))],
        out_specs=(pl.BlockSpec((rb, D_PACKED), lambda i: (i, 0)),
                   pl.BlockSpec((rb, LANES), lambda i: (i, 0))),
        compiler_params=_params("parallel"),
        name="pack_w_in",
    )(w_in)


def _pack_layer(layer, norm_gain, w_in, cq_gain, ckv_gain, w_uq, w_ukv, w_q_idx, q_gain, k_gain, conv_w, a_log,
                dt_bias, gdn_gain, w_out):
    w_packed, narrow = _pack_w_in(w_in, layer)
    w_rows = narrow[:, :16].T.astype(BF16)
    return dict(
        gain=norm_gain[None, :], w_packed=w_packed, w_rows=w_rows,
        g_cq=cq_gain[None, :], g_ckv=ckv_gain[None, :],
        w_uqt=w_uq.T.astype(BF16), w_qit=w_q_idx.T.astype(BF16),
        w_ukt=w_ukv[:, :D_ATTN].T.astype(BF16), w_uvt=w_ukv[:, D_ATTN:].T.astype(BF16),
        g_q_col=q_gain[:, None], g_k_col=k_gain[:, None],
        conv_w=conv_w, a_log=a_log[:, None], dt_bias=dt_bias[:, None],
        gdn_gain=gdn_gain[None, :], w_out=w_out.astype(BF16))


def _layer(h2, lw, bias, tri, batch, tp, t_valid, topk, keep):
    rb = _row_block(tp)
    proj, rows = _in_proj(h2, lw["gain"], lw["w_packed"], lw["w_rows"], rb)
    qt, k, vt, qit, kidx = _dsa_prep(proj, lw, batch, tp, rb)
    o_attn = _dsa_attention(qit, rows, qt, kidx, k, vt, bias, tri, batch, tp, t_valid, topk)
    qn, kn, vv, bg = _gdn_prep(proj, rows, lw, batch, tp, rb)
    m_mat, n_mat, p_mat, r_mat, cd = _gdn_chunks(qn, kn, vv, bg)
    o_gdn = _gdn_scan(m_mat, n_mat, p_mat, r_mat, cd, batch, tp).reshape(batch * tp, D_GDN)
    return _out_proj(h2, o_attn, proj, o_gdn, lw["gdn_gain"], lw["w_out"], batch, tp, *keep)


def _forward(x, meta_tokens, rel_bias_table, layer_weights, topk):
    batch, seq, _ = x.shape
    t = seq + N_META
    tp = -(-t // KEY_TILE) * KEY_TILE
    meta = jnp.broadcast_to(meta_tokens[None].astype(x.dtype), (batch, N_META, D_MODEL))
    h = jnp.concatenate([meta, x, jnp.zeros((batch, tp - t, D_MODEL), x.dtype)], axis=1)
    h2 = h.reshape(batch * tp, D_MODEL)
    bias = _bias_tiles(rel_bias_table)
    tri = jnp.tril(jnp.ones((KEY_TILE, KEY_TILE), BF16))
    crop = seq % LANES == 0
    for depth, lw in enumerate(layer_weights):
        last = crop and depth == len(layer_weights) - 1
        h2 = _layer(h2, lw, bias, tri, batch, tp, t, topk, (N_META, seq) if last else (0, tp))
    if crop:
        return h2.reshape(batch, seq, D_MODEL)
    return h2.reshape(batch, tp, D_MODEL)[:, N_META:t]


def kernel(x, meta_tokens, rel_bias_table, norm_gain, w_in, cq_norm_gain, ckv_norm_gain, w_uq, w_ukv, w_q_idx,
           q_norm_gain, k_norm_gain, conv_w, a_log, dt_bias, gdn_norm_gain, w_out):
    depth = norm_gain.shape[0]
    topk = min(TOPK_MAX, x.shape[1] // 4)
    layers = [_pack_layer(l, norm_gain[l], w_in, cq_norm_gain[l], ckv_norm_gain[l], w_uq[l], w_ukv[l],
                          w_q_idx[l], q_norm_gain[l], k_norm_gain[l], conv_w[l], a_log[l], dt_bias[l],
                          gdn_norm_gain[l], w_out[l]) for l in range(depth)]
    return _forward(x, meta_tokens, rel_bias_table, layers, topk)
```

```python
import functools
import math

import jax
import jax.numpy as jnp
from jax import lax
from jax.experimental import pallas as pl
from jax.experimental.pallas import tpu as pltpu

F32 = jnp.float32
BF16 = jnp.bfloat16
I32 = jnp.int32

D_MODEL = 1024
N_META = 16
EPS = 1e-6
N_ATTN_HEADS = 8
ATTN_HEAD_DIM = 64
D_ATTN = N_ATTN_HEADS * ATTN_HEAD_DIM
Q_RANK = 256
KV_RANK = 128
N_IDX_HEADS = 4
IDX_DIM = 64
TOPK_MAX = 256
N_REL_BUCKETS = 32
REL_MAX_DIST = 128
N_GDN_HEADS = 4
GDN_HEAD_DIM = 128
D_GDN = N_GDN_HEADS * GDN_HEAD_DIM
CONV_WIDTH = 4

LANES = 128
KEY_TILE = 256
COUNT_TILE = 512
SCAN_TILE = 1024
FIRST_CHECK_PASSES = 10
INTERPOLATED_PASSES = 24
MAX_SEARCH_PASSES = INTERPOLATED_PASSES + 32
GDN_CHUNKS_PER_STEP = (3, 2)
SCAN_CHUNKS = (6, 4, 3, 2, 1)
ROW_TILES = (6, 5, 4, 3, 2, 1)
HALO_ROWS = 8
D_PACKED = 3 * D_GDN + 512 + D_ATTN + D_GDN
COL_SMALL = 3
COL_Z_ATTN = 4
COL_Z_GDN = 5
MASKED_LOGIT = -1e30
LOG2_E = math.log2(math.e)
KEY_MIN = -2 ** 31
PATTERN_NEG_FLT_MAX = KEY_MIN + (1 << 23)
VMEM_LIMIT = 56 * 1024 * 1024

NT_DIMS = (((1,), (1,)), ((), ()))


def _dot(a, b):
    return jnp.dot(a, b, preferred_element_type=F32)


def _dot_nt(a, b):
    return lax.dot_general(a, b, NT_DIMS, preferred_element_type=F32)


def _split_bf16(x):
    hi = x.astype(BF16)
    return hi, (x - hi.astype(F32)).astype(BF16)


def _dot_split(a_parts, b_parts):
    (a_hi, a_lo), (b_hi, b_lo) = a_parts, b_parts
    return _dot(a_hi, b_hi) + (_dot(a_hi, b_lo) + _dot(a_lo, b_hi))


def _sigmoid(x):
    return 1.0 / (1.0 + jnp.exp(-x))


def _silu(x):
    return x * _sigmoid(x)


def _row_block(tp):
    tiles = tp // LANES
    return LANES * next(d for d in ROW_TILES if tiles % d == 0)


def _params(*sem):
    return pltpu.CompilerParams(dimension_semantics=sem, vmem_limit_bytes=VMEM_LIMIT)


def _bias_kernel(table_ref, out_ref):
    row = lax.broadcasted_iota(I32, (LANES, LANES), 0)
    col = lax.broadcasted_iota(I32, (LANES, LANES), 1)
    max_exact = N_REL_BUCKETS // 2
    for kind in range(3):
        dist = col - row + (2 - kind) * LANES
        n = jnp.maximum(dist, 0)
        nf = jnp.maximum(n, 1).astype(F32)
        large = max_exact + (jnp.log(nf / max_exact) / math.log(REL_MAX_DIST / max_exact)
                             * (N_REL_BUCKETS - max_exact)).astype(I32)
        large = jnp.minimum(large, N_REL_BUCKETS - 1)
        bucket = jnp.where(n < max_exact, n, large)
        for h in range(N_ATTN_HEADS):
            tile = jnp.zeros((LANES, LANES), F32)
            for b in range(N_REL_BUCKETS):
                tile = jnp.where(bucket == b, table_ref[b, h], tile)
            far = table_ref[N_REL_BUCKETS - 1, h]
            out_ref[kind, :, h * LANES:(h + 1) * LANES] = (tile - far) * LOG2_E


def _bias_tiles(rel_table):
    return pl.pallas_call(
        _bias_kernel,
        out_shape=jax.ShapeDtypeStruct((3, LANES, N_ATTN_HEADS * LANES), F32),
        in_specs=[pl.BlockSpec(memory_space=pltpu.SMEM)],
        out_specs=pl.BlockSpec(memory_space=pltpu.VMEM),
        name="rel_bias_tiles",
    )(rel_table)


def _in_proj_kernel(h_ref, gain_ref, w_ref, wrows_ref, proj_ref, rows_ref):
    x = h_ref[...]
    y = x * lax.rsqrt(jnp.mean(x * x, axis=-1, keepdims=True) + EPS)
    hn = (y * gain_ref[...]).astype(BF16)
    proj_ref[...] = _dot(hn, w_ref[...])
    rows_ref[...] = _dot_nt(wrows_ref[...], hn)


def _in_proj(h2, gain, w_packed, w_rows, rb):
    n_rows = h2.shape[0]
    grid = (n_rows // rb,)
    return pl.pallas_call(
        _in_proj_kernel,
        out_shape=(jax.ShapeDtypeStruct((n_rows, D_PACKED), F32),
                   jax.ShapeDtypeStruct((16, n_rows), F32)),
        grid=grid,
        in_specs=[pl.BlockSpec((rb, D_MODEL), lambda i: (i, 0)),
                  pl.BlockSpec((1, D_MODEL), lambda i: (0, 0)),
                  pl.BlockSpec((D_MODEL, D_PACKED), lambda i: (0, 0)),
                  pl.BlockSpec((16, D_MODEL), lambda i: (0, 0))],
        out_specs=(pl.BlockSpec((rb, D_PACKED), lambda i: (i, 0)),
                   pl.BlockSpec((16, rb), lambda i: (0, i))),
        compiler_params=_params("parallel"),
        name="in_proj",
    )(h2, gain, w_packed, w_rows)


def _rms_rows(x, gain):
    return x * lax.rsqrt(jnp.mean(x * x, axis=-1, keepdims=True) + EPS) * gain


def _dsa_prep_kernel(sm_ref, gcq_ref, gckv_ref, wuqt_ref, wqit_ref, wukt_ref, wuvt_ref, gq_ref, gk_ref,
                     qt_ref, k_ref, vt_ref, qit_ref, kidx_ref):
    sm = sm_ref[...]
    rb = sm.shape[0]
    cq = _rms_rows(sm[:, :Q_RANK], gcq_ref[...]).astype(BF16)
    ckv = _rms_rows(sm[:, Q_RANK:Q_RANK + KV_RANK], gckv_ref[...]).astype(BF16)
    kidx_ref[...] = sm[:, Q_RANK + KV_RANK:].astype(BF16)
    q3 = _dot_nt(wuqt_ref[...], cq).reshape(N_ATTN_HEADS, ATTN_HEAD_DIM, rb)
    q3 = q3 * lax.rsqrt(jnp.mean(q3 * q3, axis=1, keepdims=True) + EPS) * gq_ref[...][None]
    qt_ref[...] = (q3 * (ATTN_HEAD_DIM ** -0.5 * LOG2_E)).reshape(D_ATTN, rb).astype(BF16)
    k3 = _dot_nt(wukt_ref[...], ckv).reshape(N_ATTN_HEADS, ATTN_HEAD_DIM, rb)
    k3 = k3 * lax.rsqrt(jnp.mean(k3 * k3, axis=1, keepdims=True) + EPS) * gk_ref[...][None]
    k_ref[...] = k3.reshape(D_ATTN, rb).T.astype(BF16)
    vt_ref[...] = _dot_nt(wuvt_ref[...], ckv).astype(BF16)
    qit_ref[...] = _dot_nt(wqit_ref[...], cq).astype(BF16)


def _dsa_prep(proj, lw, batch, tp, rb):
    n_rows = proj.shape[0]
    nb = tp // rb
    d_idx = N_IDX_HEADS * IDX_DIM
    const = lambda shape: pl.BlockSpec(shape, lambda b, i: (0, 0))
    row_spec = lambda width: pl.BlockSpec((rb, width), lambda b, i: (b * nb + i, 0))
    col_spec = lambda height: pl.BlockSpec((None, height, rb), lambda b, i: (b, 0, i))
    return pl.pallas_call(
        _dsa_prep_kernel,
        out_shape=(jax.ShapeDtypeStruct((batch, D_ATTN, tp), BF16),
                   jax.ShapeDtypeStruct((n_rows, D_ATTN), BF16),
                   jax.ShapeDtypeStruct((batch, D_ATTN, tp), BF16),
                   jax.ShapeDtypeStruct((batch, d_idx, tp), BF16),
                   jax.ShapeDtypeStruct((n_rows, LANES), BF16)),
        grid=(batch, nb),
        in_specs=[pl.BlockSpec((rb, 512), lambda b, i: (b * nb + i, COL_SMALL)),
                  const((1, Q_RANK)), const((1, KV_RANK)),
                  const((D_ATTN, Q_RANK)), const((d_idx, Q_RANK)),
                  const((D_ATTN, KV_RANK)), const((D_ATTN, KV_RANK)),
                  const((ATTN_HEAD_DIM, 1)), const((ATTN_HEAD_DIM, 1))],
        out_specs=(col_spec(D_ATTN), row_spec(D_ATTN), col_spec(D_ATTN), col_spec(d_idx), row_spec(LANES)),
        compiler_params=_params("parallel", "parallel"),
        name="dsa_prep",
    )(proj, lw["g_cq"], lw["g_ckv"], lw["w_uqt"], lw["w_qit"], lw["w_ukt"], lw["w_uvt"], lw["g_q_col"],
      lw["g_k_col"])


def _dsa_block(i, qit_ref, rows_ref, qt_ref, kidx_ref, k_ref, vt_ref, bias_ref, tri_ref, o_ref,
               score_scr, wi_scr, wq_scr, s_scr, p_scr, m_scr, l_scr, acc_scr, alpha_scr, mask_scr, tie_scr, *, topk):
    t0 = i * LANES
    n_kt = i // 2 + 1
    n_st = (n_kt + 3) // 4
    last = n_kt - 1
    hd = ATTN_HEAD_DIM
    pair_w = 2 * LANES
    n_pairs = N_ATTN_HEADS // 2

    zeros_hd = jnp.zeros((hd, LANES), BF16)
    for h in range(N_IDX_HEADS):
        wi_scr[0:IDX_DIM, h * LANES:(h + 1) * LANES] = qit_ref[h * IDX_DIM:(h + 1) * IDX_DIM, :]
    wi_scr[IDX_DIM:, :] = jnp.zeros((LANES - IDX_DIM, N_IDX_HEADS * LANES), BF16)
    for p in range(n_pairs):
        wq_scr[p, 0:hd, 0:LANES] = qt_ref[2 * p * hd:(2 * p + 1) * hd, :]
        wq_scr[p, 0:hd, LANES:] = zeros_hd
        wq_scr[p, hd:, 0:LANES] = zeros_hd
        wq_scr[p, hd:, LANES:] = qt_ref[(2 * p + 1) * hd:(2 * p + 2) * hd, :]

    row = lax.broadcasted_iota(I32, (KEY_TILE, LANES), 0)
    col = lax.broadcasted_iota(I32, (KEY_TILE, LANES), 1)
    w_idx = rows_ref[0:N_IDX_HEADS, :] * (N_IDX_HEADS ** -0.5 * IDX_DIM ** -0.5)

    def key_tile(j):
        return pl.multiple_of(j * KEY_TILE, KEY_TILE)

    def causal(j):
        return (j * KEY_TILE + row) <= (t0 + col)

    def fold8(x, op):
        return op(x.reshape(KEY_TILE // 8, 8, LANES), axis=0)

    def score_step(tiles, carry, fresh=None, diagonal=None):
        top, bottom, n_nonneg, n_pos = carry
        logits = [_dot(kidx_ref[pl.ds(key_tile(j), KEY_TILE), :], wi_scr[...]) for j in tiles]
        for n, (j, lg) in enumerate(zip(tiles, logits)):
            score = jnp.zeros((KEY_TILE, LANES), F32)
            for h in range(N_IDX_HEADS):
                score = score + jnp.maximum(lg[:, h * LANES:(h + 1) * LANES], 0.0) * w_idx[h:h + 1, :]
            if n == diagonal:
                visible = causal(j)
                seen = jnp.where(visible, score, -jnp.inf)
                low = fold8(jnp.where(visible, score, jnp.inf), jnp.min)
            else:
                seen = score
                low = fold8(score, jnp.min)
            high = fold8(seen, jnp.max)
            nonneg = fold8(jnp.where(seen >= 0.0, 1, 0), jnp.sum)
            pos = fold8(jnp.where(seen > 0.0, 1, 0), jnp.sum)
            if fresh is not None and fresh[n] is not True:
                low = jnp.where(fresh[n], low, jnp.inf)
                high = jnp.where(fresh[n], high, -jnp.inf)
                nonneg = jnp.where(fresh[n], nonneg, 0)
                pos = jnp.where(fresh[n], pos, 0)
            top, bottom = jnp.maximum(top, high), jnp.minimum(bottom, low)
            n_nonneg, n_pos = n_nonneg + nonneg, n_pos + pos
            score_scr[pl.ds(key_tile(j), KEY_TILE), :] = seen
        return top, bottom, n_nonneg, n_pos

    tiles_per_step = SCAN_TILE // KEY_TILE
    first_uncounted = (n_st - 1) * tiles_per_step
    zeros8 = jnp.zeros((8, LANES), I32)
    carry = (jnp.full((8, LANES), -jnp.inf, F32), jnp.full((8, LANES), jnp.inf, F32), zeros8, zeros8)
    carry = lax.fori_loop(
        0, n_st - 1,
        lambda jc, c: score_step([jc * tiles_per_step + sub for sub in range(tiles_per_step)], c), carry)
    tail = [last - back for back in range(tiles_per_step - 1, 0, -1)]
    top, bottom, n_nonneg, n_pos = score_step(
        [jnp.maximum(j, 0) for j in tail] + [last], carry,
        fresh=[j >= first_uncounted for j in tail] + [True], diagonal=tiles_per_step - 1)

    def fill_unseen(j, _):
        score_scr[pl.ds(key_tile(j), KEY_TILE), :] = jnp.full((KEY_TILE, LANES), -jnp.inf, F32)
        return 0

    lax.fori_loop(n_kt, n_st * tiles_per_step, fill_unseen, 0)
    top = jnp.max(top, axis=0, keepdims=True)
    bottom = jnp.min(bottom, axis=0, keepdims=True)
    count0 = jnp.sum(n_nonneg, axis=0, keepdims=True)
    count_pos = jnp.sum(n_pos, axis=0, keepdims=True)

    def count_f32(cand, below=None):
        def body(j, carry):
            acc, best = carry
            for part in range(SCAN_TILE // COUNT_TILE):
                start = pl.multiple_of(j * SCAN_TILE + part * COUNT_TILE, COUNT_TILE)
                x = score_scr[pl.ds(start, COUNT_TILE), :]
                ind = jnp.where(x >= cand, 1, 0)
                acc = acc + jnp.sum(ind.reshape(COUNT_TILE // 8, 8, LANES), axis=0)
                if below is not None:
                    under = jnp.where(x < below, x, -jnp.inf)
                    best = jnp.maximum(best, jnp.max(under.reshape(COUNT_TILE // 8, 8, LANES), axis=0))
            return acc, best
        acc, best = lax.fori_loop(0, n_st, body, (jnp.zeros((8, LANES), I32), jnp.full((8, LANES), -jnp.inf, F32)))
        count = jnp.sum(acc, axis=0, keepdims=True)
        if below is None:
            return count
        return count, jnp.max(best, axis=0, keepdims=True)

    def to_pattern(v):
        bits = lax.bitcast_convert_type(v, I32)
        return bits ^ ((bits >> 31) & 0x7FFFFFFF)

    def to_f32(c):
        return lax.bitcast_convert_type(c ^ ((c >> 31) & 0x7FFFFFFF), F32)

    n_visible = t0 + 1 + lax.broadcasted_iota(I32, (1, LANES), 1)
    nonneg = count0 >= topk
    lo = jnp.where(nonneg, 0, to_pattern(bottom))
    hi = jnp.where(nonneg, to_pattern(top) + 1, 0)
    count_lo = jnp.where(nonneg, count0, n_visible)
    count_hi = jnp.where(nonneg, 0, count0)
    few = n_visible < topk
    lo = jnp.where(few, PATTERN_NEG_FLT_MAX, lo)
    zero_tied = nonneg & (count_pos < topk)
    hi = jnp.where(zero_tied, 1, hi)
    count_hi = jnp.where(zero_tied, count_pos, count_hi)
    open_q = jnp.where(few | zero_tied | (count_lo == topk), 0, 1)

    log_topk = math.log(topk)

    def count_error(count):
        return jnp.log(count.astype(F32) + 0.5) - log_topk

    def probe(n_pass, carry, extract):
        lo, hi, count_lo, count_hi, err_lo, err_hi, last_side, open_q = carry
        v_lo, v_hi = to_f32(lo), to_f32(hi)
        frac = err_lo / (err_lo - err_hi)
        frac = jnp.where(count_lo - count_hi <= 4, 0.5, frac)
        guess = to_pattern(v_lo + (v_hi - v_lo) * frac)
        middle = lo + lax.shift_right_logical(hi - lo, 1)
        cand = jnp.where(n_pass >= INTERPOLATED_PASSES, middle, guess)
        cand = jnp.minimum(jnp.maximum(cand, lo + 1), hi - 1)
        is_open = open_q == 1
        if extract:
            count, under_hi = count_f32(to_f32(cand), below=v_hi)
            next_below = to_pattern(under_hi)
            found = is_open & (count_hi == topk - 1)
            is_open = is_open & jnp.logical_not(found)
        else:
            count = count_f32(to_f32(cand))
        raise_lo = is_open & (count >= topk)
        lower_hi = is_open & (count < topk)
        err = count_error(count)
        err_hi = jnp.where(raise_lo & (last_side == 1), err_hi * 0.5, err_hi)
        err_lo = jnp.where(lower_hi & (last_side == -1), err_lo * 0.5, err_lo)
        err_lo = jnp.where(raise_lo, err, err_lo)
        err_hi = jnp.where(lower_hi, err, err_hi)
        lo = jnp.where(raise_lo, cand, lo)
        count_lo = jnp.where(raise_lo, count, count_lo)
        hi = jnp.where(lower_hi, cand, hi)
        count_hi = jnp.where(lower_hi, count, count_hi)
        last_side = jnp.where(raise_lo, 1, jnp.where(lower_hi, -1, last_side))
        if extract:
            lo = jnp.where(found, next_below, lo)
            hi = jnp.where(found | raise_lo, next_below + 1, hi)
        closed = (count_lo == topk) | (hi - lo == 1)
        return lo, hi, count_lo, count_hi, err_lo, err_hi, last_side, jnp.where(closed, 0, open_q)

    carry = (lo, hi, count_lo, count_hi, count_error(count_lo), count_error(count_hi),
             jnp.zeros((1, LANES), I32), open_q)
    n_first = jnp.where(jnp.sum(open_q) > 0, FIRST_CHECK_PASSES, 0)
    carry = lax.fori_loop(0, n_first, functools.partial(probe, extract=False), carry)

    def probes_left(st):
        n_pass, n_open = st[0], st[-1]
        return (n_pass < MAX_SEARCH_PASSES) & (n_open > 0)

    def extracting_probe(st):
        carry = probe(st[0], st[1:-1], extract=True)
        return (st[0] + 1,) + carry + (jnp.sum(carry[-1]),)

    state = lax.while_loop(probes_left, extracting_probe,
                           (jnp.int32(FIRST_CHECK_PASSES),) + carry + (jnp.sum(carry[-1]),))
    lo, count_lo, count_hi = state[1], state[3], state[4]
    tau = to_f32(lo)
    need = jnp.where((count_lo == topk) | few, topk, topk - count_hi).astype(F32)

    m_scr[...] = jnp.full(m_scr.shape, MASKED_LOGIT, F32)
    l_scr[...] = jnp.zeros(l_scr.shape, F32)
    acc_scr[...] = jnp.zeros(acc_scr.shape, F32)
    ones_rows = jnp.ones((16, KEY_TILE), BF16)

    def mask_pair(j_first, parity):
        xs = [score_scr[pl.ds(key_tile(jnp.minimum(j_first + slot, last)), KEY_TILE), :] for slot in range(2)]
        ties = [x == tau for x in xs]
        tie_cols = jnp.concatenate([jnp.where(tie, 1.0, 0.0).astype(BF16) for tie in ties], axis=1)
        ranks = _dot(tri_ref[...], tie_cols)
        tie_carry = tie_scr[0:1, :]
        for slot, (x, tie) in enumerate(zip(xs, ties)):
            rank = ranks[:, slot * LANES:(slot + 1) * LANES] + tie_carry
            tie_carry = rank[KEY_TILE - 1:KEY_TILE, :]
            take = (tie & (rank <= need)) | (x > tau)
            mask_scr[parity, slot] = jnp.where(take, 0.0, MASKED_LOGIT)
        tie_scr[0:1, :] = tie_carry

    def qk_pair(slot, j, p):
        s_scr[slot, :, p * pair_w:(p + 1) * pair_w] = _dot(
            k_ref[pl.ds(key_tile(j), KEY_TILE), p * LANES:(p + 1) * LANES], wq_scr[p])

    def softmax_pair(slot, j, p, near, parity):
        mask_add = mask_scr[parity, slot]
        alphas = []
        for h in (2 * p, 2 * p + 1):
            cols = slice(h * LANES, (h + 1) * LANES)
            logits = s_scr[slot, :, cols] + mask_add
            if near:
                kind_top = jnp.clip(2 * j - i + 2, 0, 2)
                kind_bot = jnp.clip(2 * j - i + 3, 0, 2)
                logits = logits + jnp.concatenate(
                    [bias_ref[kind_top, :, cols], bias_ref[kind_bot, :, cols]], axis=0)
            m_old = m_scr[h:h + 1, :]
            m_new = jnp.maximum(m_old, jnp.max(logits, axis=0, keepdims=True))
            m_scr[h:h + 1, :] = m_new
            p_scr[slot, :, cols] = jnp.exp2(logits - m_new).astype(BF16)
            alphas.append(jnp.exp2(m_old - m_new))
        return alphas

    def pv_pair(slot, j, p, alphas):
        lhs = jnp.concatenate([vt_ref[p * 2 * hd:(p + 1) * 2 * hd, pl.ds(key_tile(j), KEY_TILE)], ones_rows],
                              axis=0)
        out = _dot(lhs, p_scr[slot, :, p * pair_w:(p + 1) * pair_w])
        for half in range(2):
            h = 2 * p + half
            rows_h = slice(h * hd, (h + 1) * hd)
            q_cols = slice(half * LANES, (half + 1) * LANES)
            acc_scr[rows_h, :] = acc_scr[rows_h, :] * alphas[half] + out[half * hd:(half + 1) * hd, q_cols]
            l_scr[h:h + 1, :] = l_scr[h:h + 1, :] * alphas[half] + out[2 * hd:2 * hd + 1, q_cols]

    def pending_alphas(p):
        return [alpha_scr[h:h + 1, :] for h in (2 * p, 2 * p + 1)]

    def clear_pending():
        p_scr[1] = jnp.zeros(p_scr.shape[1:], BF16)
        alpha_scr[...] = jnp.ones(alpha_scr.shape, F32)

    def pair_step(ja, j_pending, j_next, near, parity):
        alphas_a = []
        for p in range(n_pairs):
            pv_pair(1, j_pending, p, pending_alphas(p))
            qk_pair(1, ja + 1, p)
            alphas_a.append(softmax_pair(0, ja, p, near, parity))
        for p in range(n_pairs):
            pv_pair(0, ja, p, alphas_a[p])
            qk_pair(0, j_next, p)
            alphas_b = softmax_pair(1, ja + 1, p, near, parity)
            for half in range(2):
                alpha_scr[2 * p + half:2 * p + half + 1, :] = alphas_b[half]
        mask_pair(ja + 2, 1 - parity)

    def single_step(ja, j_pending, near, parity):
        alphas_a = []
        for p in range(n_pairs):
            pv_pair(1, j_pending, p, pending_alphas(p))
            alphas_a.append(softmax_pair(0, ja, p, near, parity))
        for p in range(n_pairs):
            pv_pair(0, ja, p, alphas_a[p])
        clear_pending()

    n_far = 2 * (jnp.maximum(n_kt - 2, 0) // 2)
    n_near = n_kt - n_far
    first_near_parity = (n_far // 2) % 2
    clear_pending()
    tie_scr[...] = jnp.zeros(tie_scr.shape, F32)
    mask_pair(0, 0)
    for p in range(n_pairs):
        qk_pair(0, 0, p)

    def far_body(jp, carry):
        pair_step(2 * jp, jnp.maximum(2 * jp - 1, 0), 2 * jp + 2, near=False, parity=jp % 2)
        return carry

    lax.fori_loop(0, n_far // 2, far_body, 0)

    @pl.when(n_near >= 2)
    def _():
        pair_step(n_far, jnp.maximum(n_far - 1, 0), jnp.minimum(n_far + 2, last), near=True,
                  parity=first_near_parity)

    @pl.when(n_near % 2 == 1)
    def _():
        single_step(last, jnp.where(n_near == 3, n_far + 1, jnp.maximum(n_far - 1, 0)), near=True,
                    parity=jnp.where(n_near == 3, 1 - first_near_parity, first_near_parity))

    for p in range(n_pairs):
        pv_pair(1, last, p, pending_alphas(p))

    for h in range(N_ATTN_HEADS):
        rows_h = slice(h * hd, (h + 1) * hd)
        acc_scr[rows_h, :] = acc_scr[rows_h, :] / l_scr[h:h + 1, :]
    o_ref[...] = acc_scr[...].T


def _dsa_kernel(*refs, topk, t_valid):
    o_ref = refs[8]
    i = pl.program_id(1)
    is_real = i * LANES < t_valid

    @pl.when(is_real)
    def _():
        _dsa_block(i, *refs, topk=topk)

    @pl.when(jnp.logical_not(is_real))
    def _():
        o_ref[...] = jnp.zeros(o_ref.shape, F32)


def _dsa_attention(qit, rows, qt, kidx, k, vt, bias, tri, batch, tp, t_valid, topk):
    n_rows = k.shape[0]
    nqb = tp // LANES
    d_idx = N_IDX_HEADS * IDX_DIM
    n_pairs = N_ATTN_HEADS // 2
    key_rows = -(-tp // SCAN_TILE) * SCAN_TILE
    q_cols = lambda height: pl.BlockSpec((None, height, LANES), lambda b, i: (b, 0, i))
    return pl.pallas_call(
        functools.partial(_dsa_kernel, topk=topk, t_valid=t_valid),
        out_shape=jax.ShapeDtypeStruct((n_rows, D_ATTN), F32),
        grid=(batch, nqb),
        in_specs=[q_cols(d_idx),
                  pl.BlockSpec((16, LANES), lambda b, i: (0, b * nqb + i)),
                  q_cols(D_ATTN),
                  pl.BlockSpec((tp, LANES), lambda b, i: (b, 0)),
                  pl.BlockSpec((tp, D_ATTN), lambda b, i: (b, 0)),
                  pl.BlockSpec((None, D_ATTN, tp), lambda b, i: (b, 0, 0)),
                  pl.BlockSpec((3, LANES, N_ATTN_HEADS * LANES), lambda b, i: (0, 0, 0)),
                  pl.BlockSpec((KEY_TILE, KEY_TILE), lambda b, i: (0, 0))],
        out_specs=pl.BlockSpec((LANES, D_ATTN), lambda b, i: (b * nqb + i, 0)),
        scratch_shapes=[pltpu.VMEM((key_rows, LANES), F32),
                        pltpu.VMEM((LANES, N_IDX_HEADS * LANES), BF16),
                        pltpu.VMEM((n_pairs, LANES, 2 * LANES), BF16),
                        pltpu.VMEM((2, KEY_TILE, N_ATTN_HEADS * LANES), F32),
                        pltpu.VMEM((2, KEY_TILE, N_ATTN_HEADS * LANES), BF16),
                        pltpu.VMEM((N_ATTN_HEADS, LANES), F32),
                        pltpu.VMEM((N_ATTN_HEADS, LANES), F32),
                        pltpu.VMEM((D_ATTN, LANES), F32),
                        pltpu.VMEM((N_ATTN_HEADS, LANES), F32),
                        pltpu.VMEM((2, 2, KEY_TILE, LANES), F32),
                        pltpu.VMEM((8, LANES), F32)],
        compiler_params=_params("parallel", "parallel"),
        name="dsa_attention",
    )(qit, rows, qt, kidx, k, vt, bias, tri)


def _gdn_prep_kernel(x_ref, halo_ref, cw_ref, rows_ref, alog_ref, dtb_ref, q_ref, k_ref, v_ref, bg_ref, buf):
    first = pl.program_id(1) == 0
    rb = x_ref.shape[0]
    buf[0:HALO_ROWS, :] = jnp.where(first, 0.0, halo_ref[...])
    buf[HALO_ROWS:, :] = x_ref[...]
    acc = jnp.zeros((rb, 3 * D_GDN), F32)
    for tap in range(CONV_WIDTH):
        start = HALO_ROWS - (CONV_WIDTH - 1) + tap
        acc = acc + cw_ref[tap:tap + 1, :] * buf[start:start + rb, :]
    y = _silu(acc)
    for h in range(N_GDN_HEADS):
        cols = slice(h * GDN_HEAD_DIM, (h + 1) * GDN_HEAD_DIM)
        qh = y[:, cols]
        kh = y[:, D_GDN + h * GDN_HEAD_DIM:D_GDN + (h + 1) * GDN_HEAD_DIM]
        q_ref[:, cols] = (qh * lax.rsqrt(jnp.sum(qh * qh, axis=-1, keepdims=True) + EPS)
                          * (GDN_HEAD_DIM ** -0.5))
        k_ref[:, cols] = kh * lax.rsqrt(jnp.sum(kh * kh, axis=-1, keepdims=True) + EPS)
    v_ref[...] = y[:, 2 * D_GDN:]
    rows = rows_ref[...]
    beta = _sigmoid(rows[4:8, :])
    a = rows[8:12, :] + dtb_ref[...]
    softplus = jnp.maximum(a, 0.0) + jnp.log1p(jnp.exp(-jnp.abs(a)))
    bg_ref[0:4, :] = beta
    bg_ref[4:8, :] = -jnp.exp(alog_ref[...]) * softplus


def _gdn_prep(proj, rows, lw, batch, tp, rb):
    n_rows = proj.shape[0]
    nb = tp // rb
    halo_per_block = rb // HALO_ROWS
    row_spec = pl.BlockSpec((rb, D_GDN), lambda b, i: (b * nb + i, 0))
    return pl.pallas_call(
        _gdn_prep_kernel,
        out_shape=(jax.ShapeDtypeStruct((n_rows, D_GDN), F32),) * 3
        + (jax.ShapeDtypeStruct((8, n_rows), F32),),
        grid=(batch, nb),
        in_specs=[pl.BlockSpec((rb, 3 * D_GDN), lambda b, i: (b * nb + i, 0)),
                  pl.BlockSpec((HALO_ROWS, 3 * D_GDN),
                               lambda b, i: (jnp.maximum((b * nb + i) * halo_per_block - 1, 0), 0)),
                  pl.BlockSpec((CONV_WIDTH, 3 * D_GDN), lambda b, i: (0, 0)),
                  pl.BlockSpec((16, rb), lambda b, i: (0, b * nb + i)),
                  pl.BlockSpec((N_GDN_HEADS, 1), lambda b, i: (0, 0)),
                  pl.BlockSpec((N_GDN_HEADS, 1), lambda b, i: (0, 0))],
        out_specs=(row_spec, row_spec, row_spec,
                   pl.BlockSpec((8, rb), lambda b, i: (0, b * nb + i))),
        scratch_shapes=[pltpu.VMEM((HALO_ROWS + rb, 3 * D_GDN), F32)],
        compiler_params=_params("parallel", "parallel"),
        name="gdn_prep",
    )(proj, proj, lw["conv_w"], rows, lw["a_log"], lw["dt_bias"])


def _gdn_chunk_kernel(q_ref, k_ref, v_ref, bg_ref, m_ref, n_ref, p_ref, r_ref, cd_ref):
    c = LANES
    n_chunks = q_ref.shape[0] // c
    items = [(ch, h) for ch in range(n_chunks) for h in range(N_GDN_HEADS)]
    idx = range(len(items))
    row = lax.broadcasted_iota(I32, (c, c), 0)
    col = lax.broadcasted_iota(I32, (c, c), 1)
    tri = row >= col
    strict = row > col
    eye = jnp.where(row == col, 1.0, 0.0)
    lane8 = lax.broadcasted_iota(I32, (8, c), 1)
    gates, decays = [], []
    for ch in range(n_chunks):
        bg = bg_ref[:, ch * c:(ch + 1) * c]
        dec = bg
        shift = 1
        while shift < c:
            dec = dec + jnp.where(lane8 >= shift, pltpu.roll(dec, shift, 1), 0.0)
            shift *= 2
        gates.append(bg)
        decays.append(dec)

    def tokens(ref, n):
        ch, h = items[n]
        return ref[ch * c:(ch + 1) * c, h * GDN_HEAD_DIM:(h + 1) * GDN_HEAD_DIM]

    d_row = [jnp.broadcast_to(decays[ch][4 + h:5 + h, :], (c, c)) for ch, h in items]
    d_col = [d.T for d in d_row]
    beta_col = [jnp.broadcast_to(gates[ch][h:h + 1, :], (c, c)).T for ch, h in items]
    d_last = [d[:, c - 1:c] for d in d_row]
    gamma = [jnp.exp(jnp.where(tri, d_col[n] - d_row[n], MASKED_LOGIT)) for n in idx]
    exp_d = [jnp.exp(d_col[n]) for n in idx]
    k16 = [tokens(k_ref, n).astype(BF16) for n in idx]
    kb = [tokens(k_ref, n) * beta_col[n] for n in idx]
    nil = [jnp.where(strict, _dot_nt(kb[n].astype(BF16), k16[n]) * gamma[n], 0.0) for n in idx]
    block = 8
    same = lambda size: (row // size) == (col // size)
    diag = [jnp.where(same(block), x, 0.0) for x in nil]
    diag_parts = [_split_bf16(x) for x in diag]
    inv = [eye - x for x in diag]
    power = [_dot_split(x, x) for x in diag_parts]
    for it in range(2):
        power_parts = [_split_bf16(x) for x in power]
        inv = [inv[n] + _dot_split(_split_bf16(inv[n]), power_parts[n]) for n in idx]
        if it == 0:
            power = [_dot_split(x, x) for x in power_parts]
    while block < c:
        couples = same(2 * block) & ((row // block) % 2 == 1) & ((col // block) % 2 == 0)
        inv_parts = [_split_bf16(x) for x in inv]
        lower = [_dot_split(_split_bf16(jnp.where(couples, nil[n], 0.0)), inv_parts[n]) for n in idx]
        inv = [inv[n] - _dot_split(inv_parts[n], _split_bf16(lower[n])) for n in idx]
        block *= 2
    rhs = [jnp.concatenate([kb[n] * exp_d[n], tokens(v_ref, n) * beta_col[n]], axis=1) for n in idx]
    wu = [_dot_split(_split_bf16(inv[n]), _split_bf16(rhs[n])).astype(BF16) for n in idx]
    aqk = [jnp.where(tri, _dot_nt(tokens(q_ref, n).astype(BF16), k16[n]) * gamma[n], 0.0).astype(BF16)
           for n in idx]
    kd_t = [(tokens(k_ref, n) * jnp.exp(d_last[n] - d_col[n])).T.astype(BF16) for n in idx]
    state_wu = [_dot(kd_t[n], wu[n]) for n in idx]
    out_wu = [_dot(aqk[n], wu[n]) for n in idx]
    for n, (ch, h) in enumerate(items):
        m_ref[ch, h] = (-state_wu[n][:, :c]).astype(BF16)
        n_ref[ch, h] = state_wu[n][:, c:]
        p_ref[ch, h] = (tokens(q_ref, n) * exp_d[n] - out_wu[n][:, :c]).astype(BF16)
        r_ref[ch, h] = out_wu[n][:, c:]
        cd_ref[ch, h:h + 1, :] = jnp.exp(d_last[n][0:1, :] + jnp.zeros((1, c), F32))


def _gdn_chunks(qn, kn, vv, bg):
    n_rows = qn.shape[0]
    nc = n_rows // LANES
    per_step = next(d for d in GDN_CHUNKS_PER_STEP if nc % d == 0)
    tok = pl.BlockSpec((per_step * LANES, D_GDN), lambda c: (c, 0))
    mat = pl.BlockSpec((per_step, N_GDN_HEADS, LANES, LANES), lambda c: (c, 0, 0, 0))
    mat_shape = lambda dt: jax.ShapeDtypeStruct((nc, N_GDN_HEADS, LANES, LANES), dt)
    return pl.pallas_call(
        _gdn_chunk_kernel,
        out_shape=(mat_shape(BF16), mat_shape(F32), mat_shape(BF16), mat_shape(F32),
                   jax.ShapeDtypeStruct((nc, N_GDN_HEADS, LANES), F32)),
        grid=(nc // per_step,),
        in_specs=[tok, tok, tok, pl.BlockSpec((8, per_step * LANES), lambda c: (0, c))],
        out_specs=(mat, mat, mat, mat, pl.BlockSpec((per_step, N_GDN_HEADS, LANES), lambda c: (c, 0, 0))),
        compiler_params=_params("parallel"),
        name="gdn_chunks",
    )(qn, kn, vv, bg)


def _gdn_scan_kernel(m_ref, n_ref, p_ref, r_ref, cd_ref, o_ref, s_scr, *, batch):
    @pl.when(pl.program_id(0) == 0)
    def _():
        s_scr[...] = jnp.zeros(s_scr.shape, F32)

    for ch in range(m_ref.shape[1]):
        rows = slice(ch * LANES, (ch + 1) * LANES)
        for b in range(batch):
            for h in range(N_GDN_HEADS):
                s = s_scr[b, h]
                s16 = s.astype(BF16)
                o_ref[b, rows, h * GDN_HEAD_DIM:(h + 1) * GDN_HEAD_DIM] = _dot(p_ref[b, ch, h], s16) + r_ref[b, ch, h]
                s_scr[b, h] = s * cd_ref[b, ch, h:h + 1, :] + _dot(m_ref[b, ch, h], s16) + n_ref[b, ch, h]


def _gdn_scan(m_mat, n_mat, p_mat, r_mat, cd, batch, tp):
    nc = tp // LANES
    per_step = next(d for d in SCAN_CHUNKS if nc % d == 0)
    shape5 = lambda a: a.reshape(batch, nc, N_GDN_HEADS, LANES, LANES)
    mat = pl.BlockSpec((batch, per_step, N_GDN_HEADS, LANES, LANES), lambda c: (0, c, 0, 0, 0))
    return pl.pallas_call(
        functools.partial(_gdn_scan_kernel, batch=batch),
        out_shape=jax.ShapeDtypeStruct((batch, tp, D_GDN), F32),
        grid=(nc // per_step,),
        in_specs=[mat, mat, mat, mat,
                  pl.BlockSpec((batch, per_step, N_GDN_HEADS, LANES), lambda c: (0, c, 0, 0))],
        out_specs=pl.BlockSpec((batch, per_step * LANES, D_GDN), lambda c: (0, c, 0)),
        scratch_shapes=[pltpu.VMEM((batch, N_GDN_HEADS, LANES, LANES), F32)],
        compiler_params=_params("arbitrary"),
        name="gdn_scan",
    )(shape5(m_mat), shape5(n_mat), shape5(p_mat), shape5(r_mat), cd.reshape(batch, nc, N_GDN_HEADS, LANES))


def _out_proj_kernel(h_ref, oa_ref, za_ref, og_ref, zg_ref, gain_ref, w_ref, out_ref):
    attn = (oa_ref[...] * _silu(za_ref[...])).astype(BF16)
    y = _dot(attn, w_ref[0:D_ATTN, :])
    og, zg = og_ref[...], zg_ref[...]
    for h in range(N_GDN_HEADS):
        cols = slice(h * GDN_HEAD_DIM, (h + 1) * GDN_HEAD_DIM)
        gated = (_rms_rows(og[:, cols], gain_ref[...]) * _silu(zg[:, cols])).astype(BF16)
        y = y + _dot(gated, w_ref[D_ATTN + h * GDN_HEAD_DIM:D_ATTN + (h + 1) * GDN_HEAD_DIM, :])
    out_ref[...] = h_ref[...] + y


def _out_proj(h2, o_attn, proj, o_gdn, gain, w_out, batch, tp, first, rows):
    rb = _row_block(rows)
    n_blocks = rows // rb
    blk = lambda width, col: pl.BlockSpec((pl.Element(rb), pl.Element(width)),
                                          lambda b, j: (pl.multiple_of(b * tp + first + j * rb, HALO_ROWS), col * width))
    return pl.pallas_call(
        _out_proj_kernel,
        out_shape=jax.ShapeDtypeStruct((batch * rows, D_MODEL), F32),
        grid=(batch, n_blocks),
        in_specs=[blk(D_MODEL, 0), blk(D_ATTN, 0), blk(D_ATTN, COL_Z_ATTN), blk(D_GDN, 0),
                  blk(D_GDN, COL_Z_GDN),
                  pl.BlockSpec((1, GDN_HEAD_DIM), lambda b, j: (0, 0)),
                  pl.BlockSpec((D_MODEL, D_MODEL), lambda b, j: (0, 0))],
        out_specs=pl.BlockSpec((rb, D_MODEL), lambda b, j: (b * n_blocks + j, 0)),
        compiler_params=_params("parallel", "parallel"),
        name="out_proj",
    )(h2, o_attn, proj, o_gdn, proj, gain, w_out)


W_IN_PARTS = (("c_q", Q_RANK), ("c_kv", KV_RANK), ("k_idx", IDX_DIM), ("w_idx", N_IDX_HEADS),
              ("z_attn", D_ATTN), ("qkv_g", 3 * D_GDN), ("z_g", D_GDN), ("b", N_GDN_HEADS), ("a", N_GDN_HEADS))
W_PACKED_ORDER = ("qkv_g", "c_q", "c_kv", "k_idx", "k_idx", "z_attn", "z_g")


def _w_in_columns():
    cols, o = {}, 0
    for name, size in W_IN_PARTS:
        cols[name] = slice(o, o + size)
        o += size
    return cols, o


W_ROWS_ORDER = ("w_idx", "b", "a")


def _pack_w_in_kernel(w_ref, out_ref, narrow_ref):
    cols, _ = _w_in_columns()
    narrow_ref[...] = jnp.zeros(narrow_ref.shape, F32)
    o = 0
    for name in W_ROWS_ORDER:
        src = cols[name]
        size = src.stop - src.start
        narrow_ref[:, o:o + size] = w_ref[:, src]
        o += size
    o = 0
    for name in W_PACKED_ORDER:
        src = cols[name]
        size = src.stop - src.start
        out_ref[:, o:o + size] = w_ref[:, src].astype(BF16)
        o += size


def _pack_w_in(w_in, layer):
    _, width = _w_in_columns()
    rb = 2 * LANES
    return pl.pallas_call(
        _pack_w_in_kernel,
        out_shape=(jax.ShapeDtypeStruct((D_MODEL, D_PACKED), BF16),
                   jax.ShapeDtypeStruct((D_MODEL, LANES), F32)),
        grid=(D_MODEL // rb,),
        in_specs=[pl.BlockSpec((None, rb, width), lambda i: (layer, i, 0))],
        out_specs=(pl.BlockSpec((rb, D_PACKED), lambda i: (i, 0)),
                   pl.BlockSpec((rb, LANES), lambda i: (i, 0))),
        compiler_params=_params("parallel"),
        name="pack_w_in",
    )(w_in)


def _pack_layer(layer, norm_gain, w_in, cq_gain, ckv_gain, w_uq, w_ukv, w_q_idx, q_gain, k_gain, conv_w, a_log,
                dt_bias, gdn_gain, w_out):
    w_packed, narrow = _pack_w_in(w_in, layer)
    w_rows = narrow[:, :16].T.astype(BF16)
    return dict(
        gain=norm_gain[None, :], w_packed=w_packed, w_rows=w_rows,
        g_cq=cq_gain[None, :], g_ckv=ckv_gain[None, :],
        w_uqt=w_uq.T.astype(BF16), w_qit=w_q_idx.T.astype(BF16),
        w_ukt=w_ukv[:, :D_ATTN].T.astype(BF16), w_uvt=w_ukv[:, D_ATTN:].T.astype(BF16),
        g_q_col=q_gain[:, None], g_k_col=k_gain[:, None],
        conv_w=conv_w, a_log=a_log[:, None], dt_bias=dt_bias[:, None],
        gdn_gain=gdn_gain[None, :], w_out=w_out.astype(BF16))


def _layer(h2, lw, bias, tri, batch, tp, t_valid, topk, keep):
    rb = _row_block(tp)
    proj, rows = _in_proj(h2, lw["gain"], lw["w_packed"], lw["w_rows"], rb)
    qt, k, vt, qit, kidx = _dsa_prep(proj, lw, batch, tp, rb)
    o_attn = _dsa_attention(qit, rows, qt, kidx, k, vt, bias, tri, batch, tp, t_valid, topk)
    qn, kn, vv, bg = _gdn_prep(proj, rows, lw, batch, tp, rb)
    m_mat, n_mat, p_mat, r_mat, cd = _gdn_chunks(qn, kn, vv, bg)
    o_gdn = _gdn_scan(m_mat, n_mat, p_mat, r_mat, cd, batch, tp).reshape(batch * tp, D_GDN)
    return _out_proj(h2, o_attn, proj, o_gdn, lw["gdn_gain"], lw["w_out"], batch, tp, *keep)


def _forward(x, meta_tokens, rel_bias_table, layer_weights, topk):
    batch, seq, _ = x.shape
    t = seq + N_META
    tp = -(-t // KEY_TILE) * KEY_TILE
    meta = jnp.broadcast_to(meta_tokens[None].astype(x.dtype), (batch, N_META, D_MODEL))
    h = jnp.concatenate([meta, x, jnp.zeros((batch, tp - t, D_MODEL), x.dtype)], axis=1)
    h2 = h.reshape(batch * tp, D_MODEL)
    bias = _bias_tiles(rel_bias_table)
    tri = jnp.tril(jnp.ones((KEY_TILE, KEY_TILE), BF16))
    crop = seq % LANES == 0
    for depth, lw in enumerate(layer_weights):
        last = crop and depth == len(layer_weights) - 1
        h2 = _layer(h2, lw, bias, tri, batch, tp, t, topk, (N_META, seq) if last else (0, tp))
    if crop:
        return h2.reshape(batch, seq, D_MODEL)
    return h2.reshape(batch, tp, D_MODEL)[:, N_META:t]


def kernel(x, meta_tokens, rel_bias_table, norm_gain, w_in, cq_norm_gain, ckv_norm_gain, w_uq, w_ukv, w_q_idx,
           q_norm_gain, k_norm_gain, conv_w, a_log, dt_bias, gdn_norm_gain, w_out):
    depth = norm_gain.shape[0]
    topk = min(TOPK_MAX, x.shape[1] // 4)
    layers = [_pack_layer(l, norm_gain[l], w_in, cq_norm_gain[l], ckv_norm_gain[l], w_uq[l], w_ukv[l],
                          w_q_idx[l], q_norm_gain[l], k_norm_gain[l], conv_w[l], a_log[l], dt_bias[l],
                          gdn_norm_gain[l], w_out[l]) for l in range(depth)]
    return _forward(x, meta_tokens, rel_bias_table, layers, topk)
```

```python
import functools
import math

import jax
import jax.numpy as jnp
from jax import lax
from jax.experimental import pallas as pl
from jax.experimental.pallas import tpu as pltpu

F32 = jnp.float32
BF16 = jnp.bfloat16
I32 = jnp.int32

D_MODEL = 1024
N_META = 16
EPS = 1e-6
N_ATTN_HEADS = 8
ATTN_HEAD_DIM = 64
D_ATTN = N_ATTN_HEADS * ATTN_HEAD_DIM
Q_RANK = 256
KV_RANK = 128
N_IDX_HEADS = 4
IDX_DIM = 64
TOPK_MAX = 256
N_REL_BUCKETS = 32
REL_MAX_DIST = 128
N_GDN_HEADS = 4
GDN_HEAD_DIM = 128
D_GDN = N_GDN_HEADS * GDN_HEAD_DIM
CONV_WIDTH = 4

LANES = 128
KEY_TILE = 256
COUNT_TILE = 512
SCAN_TILE = 1024
FIRST_CHECK_PASSES = 10
INTERPOLATED_PASSES = 24
MAX_SEARCH_PASSES = INTERPOLATED_PASSES + 32
GDN_CHUNKS_PER_STEP = (3, 2)
SCAN_CHUNKS = (6, 4, 3, 2, 1)
ROW_TILES = (6, 5, 4, 3, 2, 1)
HALO_ROWS = 8
D_PACKED = 3 * D_GDN + 512 + D_ATTN + D_GDN
COL_SMALL = 3
COL_Z_ATTN = 4
COL_Z_GDN = 5
MASKED_LOGIT = -1e30
LOG2_E = math.log2(math.e)
KEY_MIN = -2 ** 31
PATTERN_NEG_FLT_MAX = KEY_MIN + (1 << 23)
VMEM_LIMIT = 56 * 1024 * 1024

NT_DIMS = (((1,), (1,)), ((), ()))


def _dot(a, b):
    return jnp.dot(a, b, preferred_element_type=F32)


def _dot_nt(a, b):
    return lax.dot_general(a, b, NT_DIMS, preferred_element_type=F32)


def _split_bf16(x):
    hi = x.astype(BF16)
    return hi, (x - hi.astype(F32)).astype(BF16)


def _dot_split(a_parts, b_parts):
    (a_hi, a_lo), (b_hi, b_lo) = a_parts, b_parts
    return _dot(a_hi, b_hi) + (_dot(a_hi, b_lo) + _dot(a_lo, b_hi))


def _sigmoid(x):
    return 1.0 / (1.0 + jnp.exp(-x))


def _silu(x):
    return x * _sigmoid(x)


def _row_block(tp):
    tiles = tp // LANES
    return LANES * next(d for d in ROW_TILES if tiles % d == 0)


def _params(*sem):
    return pltpu.CompilerParams(dimension_semantics=sem, vmem_limit_bytes=VMEM_LIMIT)


def _bias_kernel(table_ref, out_ref):
    row = lax.broadcasted_iota(I32, (LANES, LANES), 0)
    col = lax.broadcasted_iota(I32, (LANES, LANES), 1)
    max_exact = N_REL_BUCKETS // 2
    for kind in range(3):
        dist = col - row + (2 - kind) * LANES
        n = jnp.maximum(dist, 0)
        nf = jnp.maximum(n, 1).astype(F32)
        large = max_exact + (jnp.log(nf / max_exact) / math.log(REL_MAX_DIST / max_exact)
                             * (N_REL_BUCKETS - max_exact)).astype(I32)
        large = jnp.minimum(large, N_REL_BUCKETS - 1)
        bucket = jnp.where(n < max_exact, n, large)
        for h in range(N_ATTN_HEADS):
            tile = jnp.zeros((LANES, LANES), F32)
            for b in range(N_REL_BUCKETS):
                tile = jnp.where(bucket == b, table_ref[b, h], tile)
            far = table_ref[N_REL_BUCKETS - 1, h]
            out_ref[kind, :, h * LANES:(h + 1) * LANES] = (tile - far) * LOG2_E


def _bias_tiles(rel_table):
    return pl.pallas_call(
        _bias_kernel,
        out_shape=jax.ShapeDtypeStruct((3, LANES, N_ATTN_HEADS * LANES), F32),
        in_specs=[pl.BlockSpec(memory_space=pltpu.SMEM)],
        out_specs=pl.BlockSpec(memory_space=pltpu.VMEM),
        name="rel_bias_tiles",
    )(rel_table)


def _in_proj_kernel(h_ref, gain_ref, w_ref, wrows_ref, proj_ref, rows_ref):
    x = h_ref[...]
    y = x * lax.rsqrt(jnp.mean(x * x, axis=-1, keepdims=True) + EPS)
    hn = (y * gain_ref[...]).astype(BF16)
    proj_ref[...] = _dot(hn, w_ref[...])
    rows_ref[...] = _dot_nt(wrows_ref[...], hn)


def _in_proj(h2, gain, w_packed, w_rows, rb):
    n_rows = h2.shape[0]
    grid = (n_rows // rb,)
    return pl.pallas_call(
        _in_proj_kernel,
        out_shape=(jax.ShapeDtypeStruct((n_rows, D_PACKED), F32),
                   jax.ShapeDtypeStruct((16, n_rows), F32)),
        grid=grid,
        in_specs=[pl.BlockSpec((rb, D_MODEL), lambda i: (i, 0)),
                  pl.BlockSpec((1, D_MODEL), lambda i: (0, 0)),
                  pl.BlockSpec((D_MODEL, D_PACKED), lambda i: (0, 0)),
                  pl.BlockSpec((16, D_MODEL), lambda i: (0, 0))],
        out_specs=(pl.BlockSpec((rb, D_PACKED), lambda i: (i, 0)),
                   pl.BlockSpec((16, rb), lambda i: (0, i))),
        compiler_params=_params("parallel"),
        name="in_proj",
    )(h2, gain, w_packed, w_rows)


def _rms_rows(x, gain):
    return x * lax.rsqrt(jnp.mean(x * x, axis=-1, keepdims=True) + EPS) * gain


def _dsa_prep_kernel(sm_ref, gcq_ref, gckv_ref, wuqt_ref, wqit_ref, wukt_ref, wuvt_ref, gq_ref, gk_ref,
                     qt_ref, k_ref, vt_ref, qit_ref, kidx_ref):
    sm = sm_ref[...]
    rb = sm.shape[0]
    cq = _rms_rows(sm[:, :Q_RANK], gcq_ref[...]).astype(BF16)
    ckv = _rms_rows(sm[:, Q_RANK:Q_RANK + KV_RANK], gckv_ref[...]).astype(BF16)
    kidx_ref[...] = sm[:, Q_RANK + KV_RANK:].astype(BF16)
    q3 = _dot_nt(wuqt_ref[...], cq).reshape(N_ATTN_HEADS, ATTN_HEAD_DIM, rb)
    q3 = q3 * lax.rsqrt(jnp.mean(q3 * q3, axis=1, keepdims=True) + EPS) * gq_ref[...][None]
    qt_ref[...] = (q3 * (ATTN_HEAD_DIM ** -0.5 * LOG2_E)).reshape(D_ATTN, rb).astype(BF16)
    k3 = _dot_nt(wukt_ref[...], ckv).reshape(N_ATTN_HEADS, ATTN_HEAD_DIM, rb)
    k3 = k3 * lax.rsqrt(jnp.mean(k3 * k3, axis=1, keepdims=True) + EPS) * gk_ref[...][None]
    k_ref[...] = k3.reshape(D_ATTN, rb).T.astype(BF16)
    vt_ref[...] = _dot_nt(wuvt_ref[...], ckv).astype(BF16)
    qit_ref[...] = _dot_nt(wqit_ref[...], cq).astype(BF16)


def _dsa_prep(proj, lw, batch, tp, rb):
    n_rows = proj.shape[0]
    nb = tp // rb
    d_idx = N_IDX_HEADS * IDX_DIM
    const = lambda shape: pl.BlockSpec(shape, lambda b, i: (0, 0))
    row_spec = lambda width: pl.BlockSpec((rb, width), lambda b, i: (b * nb + i, 0))
    col_spec = lambda height: pl.BlockSpec((None, height, rb), lambda b, i: (b, 0, i))
    return pl.pallas_call(
        _dsa_prep_kernel,
        out_shape=(jax.ShapeDtypeStruct((batch, D_ATTN, tp), BF16),
                   jax.ShapeDtypeStruct((n_rows, D_ATTN), BF16),
                   jax.ShapeDtypeStruct((batch, D_ATTN, tp), BF16),
                   jax.ShapeDtypeStruct((batch, d_idx, tp), BF16),
                   jax.ShapeDtypeStruct((n_rows, LANES), BF16)),
        grid=(batch, nb),
        in_specs=[pl.BlockSpec((rb, 512), lambda b, i: (b * nb + i, COL_SMALL)),
                  const((1, Q_RANK)), const((1, KV_RANK)),
                  const((D_ATTN, Q_RANK)), const((d_idx, Q_RANK)),
                  const((D_ATTN, KV_RANK)), const((D_ATTN, KV_RANK)),
                  const((ATTN_HEAD_DIM, 1)), const((ATTN_HEAD_DIM, 1))],
        out_specs=(col_spec(D_ATTN), row_spec(D_ATTN), col_spec(D_ATTN), col_spec(d_idx), row_spec(LANES)),
        compiler_params=_params("parallel", "parallel"),
        name="dsa_prep",
    )(proj, lw["g_cq"], lw["g_ckv"], lw["w_uqt"], lw["w_qit"], lw["w_ukt"], lw["w_uvt"], lw["g_q_col"],
      lw["g_k_col"])


def _dsa_block(i, qit_ref, rows_ref, qt_ref, kidx_ref, k_ref, vt_ref, bias_ref, tri_ref, o_ref,
               score_scr, wi_scr, wq_scr, s_scr, p_scr, m_scr, l_scr, acc_scr, alpha_scr, mask_scr, tie_scr, *, topk):
    t0 = i * LANES
    n_kt = i // 2 + 1
    n_st = (n_kt + 3) // 4
    last = n_kt - 1
    hd = ATTN_HEAD_DIM
    pair_w = 2 * LANES
    n_pairs = N_ATTN_HEADS // 2

    zeros_hd = jnp.zeros((hd, LANES), BF16)
    for h in range(N_IDX_HEADS):
        wi_scr[0:IDX_DIM, h * LANES:(h + 1) * LANES] = qit_ref[h * IDX_DIM:(h + 1) * IDX_DIM, :]
    wi_scr[IDX_DIM:, :] = jnp.zeros((LANES - IDX_DIM, N_IDX_HEADS * LANES), BF16)
    for p in range(n_pairs):
        wq_scr[p, 0:hd, 0:LANES] = qt_ref[2 * p * hd:(2 * p + 1) * hd, :]
        wq_scr[p, 0:hd, LANES:] = zeros_hd
        wq_scr[p, hd:, 0:LANES] = zeros_hd
        wq_scr[p, hd:, LANES:] = qt_ref[(2 * p + 1) * hd:(2 * p + 2) * hd, :]

    row = lax.broadcasted_iota(I32, (KEY_TILE, LANES), 0)
    col = lax.broadcasted_iota(I32, (KEY_TILE, LANES), 1)
    w_idx = rows_ref[0:N_IDX_HEADS, :] * (N_IDX_HEADS ** -0.5 * IDX_DIM ** -0.5)

    def key_tile(j):
        return pl.multiple_of(j * KEY_TILE, KEY_TILE)

    def causal(j):
        return (j * KEY_TILE + row) <= (t0 + col)

    def fold8(x, op):
        return op(x.reshape(KEY_TILE // 8, 8, LANES), axis=0)

    def score_step(tiles, carry, fresh=None, diagonal=None):
        top, bottom, n_nonneg, n_pos = carry
        logits = [_dot(kidx_ref[pl.ds(key_tile(j), KEY_TILE), :], wi_scr[...]) for j in tiles]
        for n, (j, lg) in enumerate(zip(tiles, logits)):
            score = jnp.zeros((KEY_TILE, LANES), F32)
            for h in range(N_IDX_HEADS):
                score = score + jnp.maximum(lg[:, h * LANES:(h + 1) * LANES], 0.0) * w_idx[h:h + 1, :]
            if n == diagonal:
                visible = causal(j)
                seen = jnp.where(visible, score, -jnp.inf)
                low = fold8(jnp.where(visible, score, jnp.inf), jnp.min)
            else:
                seen = score
                low = fold8(score, jnp.min)
            high = fold8(seen, jnp.max)
            nonneg = fold8(jnp.where(seen >= 0.0, 1, 0), jnp.sum)
            pos = fold8(jnp.where(seen > 0.0, 1, 0), jnp.sum)
            if fresh is not None and fresh[n] is not True:
                low = jnp.where(fresh[n], low, jnp.inf)
                high = jnp.where(fresh[n], high, -jnp.inf)
                nonneg = jnp.where(fresh[n], nonneg, 0)
                pos = jnp.where(fresh[n], pos, 0)
            top, bottom = jnp.maximum(top, high), jnp.minimum(bottom, low)
            n_nonneg, n_pos = n_nonneg + nonneg, n_pos + pos
            score_scr[pl.ds(key_tile(j), KEY_TILE), :] = seen
        return top, bottom, n_nonneg, n_pos

    tiles_per_step = SCAN_TILE // KEY_TILE
    first_uncounted = (n_st - 1) * tiles_per_step
    zeros8 = jnp.zeros((8, LANES), I32)
    carry = (jnp.full((8, LANES), -jnp.inf, F32), jnp.full((8, LANES), jnp.inf, F32), zeros8, zeros8)
    carry = lax.fori_loop(
        0, n_st - 1,
        lambda jc, c: score_step([jc * tiles_per_step + sub for sub in range(tiles_per_step)], c), carry)
    tail = [last - back for back in range(tiles_per_step - 1, 0, -1)]
    top, bottom, n_nonneg, n_pos = score_step(
        [jnp.maximum(j, 0) for j in tail] + [last], carry,
        fresh=[j >= first_uncounted for j in tail] + [True], diagonal=tiles_per_step - 1)

    def fill_unseen(j, _):
        score_scr[pl.ds(key_tile(j), KEY_TILE), :] = jnp.full((KEY_TILE, LANES), -jnp.inf, F32)
        return 0

    lax.fori_loop(n_kt, n_st * tiles_per_step, fill_unseen, 0)
    top = jnp.max(top, axis=0, keepdims=True)
    bottom = jnp.min(bottom, axis=0, keepdims=True)
    count0 = jnp.sum(n_nonneg, axis=0, keepdims=True)
    count_pos = jnp.sum(n_pos, axis=0, keepdims=True)

    def count_f32(cand, below=None):
        def body(j, carry):
            acc, best = carry
            for part in range(SCAN_TILE // COUNT_TILE):
                start = pl.multiple_of(j * SCAN_TILE + part * COUNT_TILE, COUNT_TILE)
                x = score_scr[pl.ds(start, COUNT_TILE), :]
                ind = jnp.where(x >= cand, 1, 0)
                acc = acc + jnp.sum(ind.reshape(COUNT_TILE // 8, 8, LANES), axis=0)
                if below is not None:
                    under = jnp.where(x < below, x, -jnp.inf)
                    best = jnp.maximum(best, jnp.max(under.reshape(COUNT_TILE // 8, 8, LANES), axis=0))
            return acc, best
        acc, best = lax.fori_loop(0, n_st, body, (jnp.zeros((8, LANES), I32), jnp.full((8, LANES), -jnp.inf, F32)))
        count = jnp.sum(acc, axis=0, keepdims=True)
        if below is None:
            return count
        return count, jnp.max(best, axis=0, keepdims=True)

    def to_pattern(v):
        bits = lax.bitcast_convert_type(v, I32)
        return bits ^ ((bits >> 31) & 0x7FFFFFFF)

    def to_f32(c):
        return lax.bitcast_convert_type(c ^ ((c >> 31) & 0x7FFFFFFF), F32)

    n_visible = t0 + 1 + lax.broadcasted_iota(I32, (1, LANES), 1)
    nonneg = count0 >= topk
    lo = jnp.where(nonneg, 0, to_pattern(bottom))
    hi = jnp.where(nonneg, to_pattern(top) + 1, 0)
    count_lo = jnp.where(nonneg, count0, n_visible)
    count_hi = jnp.where(nonneg, 0, count0)
    few = n_visible < topk
    lo = jnp.where(few, PATTERN_NEG_FLT_MAX, lo)
    zero_tied = nonneg & (count_pos < topk)
    hi = jnp.where(zero_tied, 1, hi)
    count_hi = jnp.where(zero_tied, count_pos, count_hi)
    open_q = jnp.where(few | zero_tied | (count_lo == topk), 0, 1)

    log_topk = math.log(topk)

    def count_error(count):
        return jnp.log(count.astype(F32) + 0.5) - log_topk

    def probe(n_pass, carry, extract):
        lo, hi, count_lo, count_hi, err_lo, err_hi, last_side, open_q = carry
        v_lo, v_hi = to_f32(lo), to_f32(hi)
        frac = err_lo / (err_lo - err_hi)
        frac = jnp.where(count_lo - count_hi <= 4, 0.5, frac)
        guess = to_pattern(v_lo + (v_hi - v_lo) * frac)
        middle = lo + lax.shift_right_logical(hi - lo, 1)
        cand = jnp.where(n_pass >= INTERPOLATED_PASSES, middle, guess)
        cand = jnp.minimum(jnp.maximum(cand, lo + 1), hi - 1)
        is_open = open_q == 1
        if extract:
            count, under_hi = count_f32(to_f32(cand), below=v_hi)
            next_below = to_pattern(under_hi)
            found = is_open & (count_hi == topk - 1)
            is_open = is_open & jnp.logical_not(found)
        else:
            count = count_f32(to_f32(cand))
        raise_lo = is_open & (count >= topk)
        lower_hi = is_open & (count < topk)
        err = count_error(count)
        err_hi = jnp.where(raise_lo & (last_side == 1), err_hi * 0.5, err_hi)
        err_lo = jnp.where(lower_hi & (last_side == -1), err_lo * 0.5, err_lo)
        err_lo = jnp.where(raise_lo, err, err_lo)
        err_hi = jnp.where(lower_hi, err, err_hi)
        lo = jnp.where(raise_lo, cand, lo)
        count_lo = jnp.where(raise_lo, count, count_lo)
        hi = jnp.where(lower_hi, cand, hi)
        count_hi = jnp.where(lower_hi, count, count_hi)
        last_side = jnp.where(raise_lo, 1, jnp.where(lower_hi, -1, last_side))
        if extract:
            lo = jnp.where(found, next_below, lo)
            hi = jnp.where(found | raise_lo, next_below + 1, hi)
        closed = (count_lo == topk) | (hi - lo == 1)
        return lo, hi, count_lo, count_hi, err_lo, err_hi, last_side, jnp.where(closed, 0, open_q)

    carry = (lo, hi, count_lo, count_hi, count_error(count_lo), count_error(count_hi),
             jnp.zeros((1, LANES), I32), open_q)
    n_first = jnp.where(jnp.sum(open_q) > 0, FIRST_CHECK_PASSES, 0)
    carry = lax.fori_loop(0, n_first, functools.partial(probe, extract=False), carry)

    def probes_left(st):
        n_pass, n_open = st[0], st[-1]
        return (n_pass < MAX_SEARCH_PASSES) & (n_open > 0)

    def extracting_probe(st):
        carry = probe(st[0], st[1:-1], extract=True)
        return (st[0] + 1,) + carry + (jnp.sum(carry[-1]),)

    state = lax.while_loop(probes_left, extracting_probe,
                           (jnp.int32(FIRST_CHECK_PASSES),) + carry + (jnp.sum(carry[-1]),))
    lo, count_lo, count_hi = state[1], state[3], state[4]
    tau = to_f32(lo)
    need = jnp.where((count_lo == topk) | few, topk, topk - count_hi).astype(F32)

    m_scr[...] = jnp.full(m_scr.shape, MASKED_LOGIT, F32)
    l_scr[...] = jnp.zeros(l_scr.shape, F32)
    acc_scr[...] = jnp.zeros(acc_scr.shape, F32)
    ones_rows = jnp.ones((16, KEY_TILE), BF16)

    def mask_pair(j_first, parity):
        xs = [score_scr[pl.ds(key_tile(jnp.minimum(j_first + slot, last)), KEY_TILE), :] for slot in range(2)]
        ties = [x == tau for x in xs]
        tie_cols = jnp.concatenate([jnp.where(tie, 1.0, 0.0).astype(BF16) for tie in ties], axis=1)
        ranks = _dot(tri_ref[...], tie_cols)
        tie_carry = tie_scr[0:1, :]
        for slot, (x, tie) in enumerate(zip(xs, ties)):
            rank = ranks[:, slot * LANES:(slot + 1) * LANES] + tie_carry
            tie_carry = rank[KEY_TILE - 1:KEY_TILE, :]
            take = (tie & (rank <= need)) | (x > tau)
            mask_scr[parity, slot] = jnp.where(take, 0.0, MASKED_LOGIT)
        tie_scr[0:1, :] = tie_carry

    def qk_pair(slot, j, p):
        s_scr[slot, :, p * pair_w:(p + 1) * pair_w] = _dot(
            k_ref[pl.ds(key_tile(j), KEY_TILE), p * LANES:(p + 1) * LANES], wq_scr[p])

    def softmax_pair(slot, j, p, near, parity):
        mask_add = mask_scr[parity, slot]
        alphas = []
        for h in (2 * p, 2 * p + 1):
            cols = slice(h * LANES, (h + 1) * LANES)
            logits = s_scr[slot, :, cols] + mask_add
            if near:
                kind_top = jnp.clip(2 * j - i + 2, 0, 2)
                kind_bot = jnp.clip(2 * j - i + 3, 0, 2)
                logits = logits + jnp.concatenate(
                    [bias_ref[kind_top, :, cols], bias_ref[kind_bot, :, cols]], axis=0)
            m_old = m_scr[h:h + 1, :]
            m_new = jnp.maximum(m_old, jnp.max(logits, axis=0, keepdims=True))
            m_scr[h:h + 1, :] = m_new
            p_scr[slot, :, cols] = jnp.exp2(logits - m_new).astype(BF16)
            alphas.append(jnp.exp2(m_old - m_new))
        return alphas

    def pv_pair(slot, j, p, alphas):
        lhs = jnp.concatenate([vt_ref[p * 2 * hd:(p + 1) * 2 * hd, pl.ds(key_tile(j), KEY_TILE)], ones_rows],
                              axis=0)
        out = _dot(lhs, p_scr[slot, :, p * pair_w:(p + 1) * pair_w])
        for half in range(2):
            h = 2 * p + half
            rows_h = slice(h * hd, (h + 1) * hd)
            q_cols = slice(half * LANES, (half + 1) * LANES)
            acc_scr[rows_h, :] = acc_scr[rows_h, :] * alphas[half] + out[half * hd:(half + 1) * hd, q_cols]
            l_scr[h:h + 1, :] = l_scr[h:h + 1, :] * alphas[half] + out[2 * hd:2 * hd + 1, q_cols]

    def pending_alphas(p):
        return [alpha_scr[h:h + 1, :] for h in (2 * p, 2 * p + 1)]

    def clear_pending():
        p_scr[1] = jnp.zeros(p_scr.shape[1:], BF16)
        alpha_scr[...] = jnp.ones(alpha_scr.shape, F32)

    def pair_step(ja, j_pending, j_next, near, parity):
        alphas_a = []
        for p in range(n_pairs):
            pv_pair(1, j_pending, p, pending_alphas(p))
            qk_pair(1, ja + 1, p)
            alphas_a.append(softmax_pair(0, ja, p, near, parity))
        for p in range(n_pairs):
            pv_pair(0, ja, p, alphas_a[p])
            qk_pair(0, j_next, p)
            alphas_b = softmax_pair(1, ja + 1, p, near, parity)
            for half in range(2):
                alpha_scr[2 * p + half:2 * p + half + 1, :] = alphas_b[half]
        mask_pair(ja + 2, 1 - parity)

    def single_step(ja, j_pending, near, parity):
        alphas_a = []
        for p in range(n_pairs):
            pv_pair(1, j_pending, p, pending_alphas(p))
            alphas_a.append(softmax_pair(0, ja, p, near, parity))
        for p in range(n_pairs):
            pv_pair(0, ja, p, alphas_a[p])
        clear_pending()

    n_far = 2 * (jnp.maximum(n_kt - 2, 0) // 2)
    n_near = n_kt - n_far
    first_near_parity = (n_far // 2) % 2
    clear_pending()
    tie_scr[...] = jnp.zeros(tie_scr.shape, F32)
    mask_pair(0, 0)
    for p in range(n_pairs):
        qk_pair(0, 0, p)

    def far_body(jp, carry):
        pair_step(2 * jp, jnp.maximum(2 * jp - 1, 0), 2 * jp + 2, near=False, parity=jp % 2)
        return carry

    lax.fori_loop(0, n_far // 2, far_body, 0)

    @pl.when(n_near >= 2)
    def _():
        pair_step(n_far, jnp.maximum(n_far - 1, 0), jnp.minimum(n_far + 2, last), near=True,
                  parity=first_near_parity)

    @pl.when(n_near % 2 == 1)
    def _():
        single_step(last, jnp.where(n_near == 3, n_far + 1, jnp.maximum(n_far - 1, 0)), near=True,
                    parity=jnp.where(n_near == 3, 1 - first_near_parity, first_near_parity))

    for p in range(n_pairs):
        pv_pair(1, last, p, pending_alphas(p))

    for h in range(N_ATTN_HEADS):
        rows_h = slice(h * hd, (h + 1) * hd)
        acc_scr[rows_h, :] = acc_scr[rows_h, :] / l_scr[h:h + 1, :]
    o_ref[...] = acc_scr[...].T


def _dsa_kernel(*refs, topk, t_valid):
    o_ref = refs[8]
    i = pl.program_id(1)
    is_real = i * LANES < t_valid

    @pl.when(is_real)
    def _():
        _dsa_block(i, *refs, topk=topk)

    @pl.when(jnp.logical_not(is_real))
    def _():
        o_ref[...] = jnp.zeros(o_ref.shape, F32)


def _dsa_attention(qit, rows, qt, kidx, k, vt, bias, tri, batch, tp, t_valid, topk):
    n_rows = k.shape[0]
    nqb = tp // LANES
    d_idx = N_IDX_HEADS * IDX_DIM
    n_pairs = N_ATTN_HEADS // 2
    key_rows = -(-tp // SCAN_TILE) * SCAN_TILE
    q_cols = lambda height: pl.BlockSpec((None, height, LANES), lambda b, i: (b, 0, i))
    return pl.pallas_call(
        functools.partial(_dsa_kernel, topk=topk, t_valid=t_valid),
        out_shape=jax.ShapeDtypeStruct((n_rows, D_ATTN), F32),
        grid=(batch, nqb),
        in_specs=[q_cols(d_idx),
                  pl.BlockSpec((16, LANES), lambda b, i: (0, b * nqb + i)),
                  q_cols(D_ATTN),
                  pl.BlockSpec((tp, LANES), lambda b, i: (b, 0)),
                  pl.BlockSpec((tp, D_ATTN), lambda b, i: (b, 0)),
                  pl.BlockSpec((None, D_ATTN, tp), lambda b, i: (b, 0, 0)),
                  pl.BlockSpec((3, LANES, N_ATTN_HEADS * LANES), lambda b, i: (0, 0, 0)),
                  pl.BlockSpec((KEY_TILE, KEY_TILE), lambda b, i: (0, 0))],
        out_specs=pl.BlockSpec((LANES, D_ATTN), lambda b, i: (b * nqb + i, 0)),
        scratch_shapes=[pltpu.VMEM((key_rows, LANES), F32),
                        pltpu.VMEM((LANES, N_IDX_HEADS * LANES), BF16),
                        pltpu.VMEM((n_pairs, LANES, 2 * LANES), BF16),
                        pltpu.VMEM((2, KEY_TILE, N_ATTN_HEADS * LANES), F32),
                        pltpu.VMEM((2, KEY_TILE, N_ATTN_HEADS * LANES), BF16),
                        pltpu.VMEM((N_ATTN_HEADS, LANES), F32),
                        pltpu.VMEM((N_ATTN_HEADS, LANES), F32),
                        pltpu.VMEM((D_ATTN, LANES), F32),
                        pltpu.VMEM((N_ATTN_HEADS, LANES), F32),
                        pltpu.VMEM((2, 2, KEY_TILE, LANES), F32),
                        pltpu.VMEM((8, LANES), F32)],
        compiler_params=_params("parallel", "parallel"),
        name="dsa_attention",
    )(qit, rows, qt, kidx, k, vt, bias, tri)


def _gdn_prep_kernel(x_ref, halo_ref, cw_ref, rows_ref, alog_ref, dtb_ref, q_ref, k_ref, v_ref, bg_ref, buf):
    first = pl.program_id(1) == 0
    rb = x_ref.shape[0]
    buf[0:HALO_ROWS, :] = jnp.where(first, 0.0, halo_ref[...])
    buf[HALO_ROWS:, :] = x_ref[...]
    acc = jnp.zeros((rb, 3 * D_GDN), F32)
    for tap in range(CONV_WIDTH):
        start = HALO_ROWS - (CONV_WIDTH - 1) + tap
        acc = acc + cw_ref[tap:tap + 1, :] * buf[start:start + rb, :]
    y = _silu(acc)
    for h in range(N_GDN_HEADS):
        cols = slice(h * GDN_HEAD_DIM, (h + 1) * GDN_HEAD_DIM)
        qh = y[:, cols]
        kh = y[:, D_GDN + h * GDN_HEAD_DIM:D_GDN + (h + 1) * GDN_HEAD_DIM]
        q_ref[:, cols] = (qh * lax.rsqrt(jnp.sum(qh * qh, axis=-1, keepdims=True) + EPS)
                          * (GDN_HEAD_DIM ** -0.5))
        k_ref[:, cols] = kh * lax.rsqrt(jnp.sum(kh * kh, axis=-1, keepdims=True) + EPS)
    v_ref[...] = y[:, 2 * D_GDN:]
    rows = rows_ref[...]
    beta = _sigmoid(rows[4:8, :])
    a = rows[8:12, :] + dtb_ref[...]
    softplus = jnp.maximum(a, 0.0) + jnp.log1p(jnp.exp(-jnp.abs(a)))
    bg_ref[0:4, :] = beta
    bg_ref[4:8, :] = -jnp.exp(alog_ref[...]) * softplus


def _gdn_prep(proj, rows, lw, batch, tp, rb):
    n_rows = proj.shape[0]
    nb = tp // rb
    halo_per_block = rb // HALO_ROWS
    row_spec = pl.BlockSpec((rb, D_GDN), lambda b, i: (b * nb + i, 0))
    return pl.pallas_call(
        _gdn_prep_kernel,
        out_shape=(jax.ShapeDtypeStruct((n_rows, D_GDN), F32),) * 3
        + (jax.ShapeDtypeStruct((8, n_rows), F32),),
        grid=(batch, nb),
        in_specs=[pl.BlockSpec((rb, 3 * D_GDN), lambda b, i: (b * nb + i, 0)),
                  pl.BlockSpec((HALO_ROWS, 3 * D_GDN),
                               lambda b, i: (jnp.maximum((b * nb + i) * halo_per_block - 1, 0), 0)),
                  pl.BlockSpec((CONV_WIDTH, 3 * D_GDN), lambda b, i: (0, 0)),
                  pl.BlockSpec((16, rb), lambda b, i: (0, b * nb + i)),
                  pl.BlockSpec((N_GDN_HEADS, 1), lambda b, i: (0, 0)),
                  pl.BlockSpec((N_GDN_HEADS, 1), lambda b, i: (0, 0))],
        out_specs=(row_spec, row_spec, row_spec,
                   pl.BlockSpec((8, rb), lambda b, i: (0, b * nb + i))),
        scratch_shapes=[pltpu.VMEM((HALO_ROWS + rb, 3 * D_GDN), F32)],
        compiler_params=_params("parallel", "parallel"),
        name="gdn_prep",
    )(proj, proj, lw["conv_w"], rows, lw["a_log"], lw["dt_bias"])


def _gdn_chunk_kernel(q_ref, k_ref, v_ref, bg_ref, m_ref, n_ref, p_ref, r_ref, cd_ref):
    c = LANES
    n_chunks = q_ref.shape[0] // c
    items = [(ch, h) for ch in range(n_chunks) for h in range(N_GDN_HEADS)]
    idx = range(len(items))
    row = lax.broadcasted_iota(I32, (c, c), 0)
    col = lax.broadcasted_iota(I32, (c, c), 1)
    tri = row >= col
    strict = row > col
    eye = jnp.where(row == col, 1.0, 0.0)
    lane8 = lax.broadcasted_iota(I32, (8, c), 1)
    gates, decays = [], []
    for ch in range(n_chunks):
        bg = bg_ref[:, ch * c:(ch + 1) * c]
        dec = bg
        shift = 1
        while shift < c:
            dec = dec + jnp.where(lane8 >= shift, pltpu.roll(dec, shift, 1), 0.0)
            shift *= 2
        gates.append(bg)
        decays.append(dec)

    def tokens(ref, n):
        ch, h = items[n]
        return ref[ch * c:(ch + 1) * c, h * GDN_HEAD_DIM:(h + 1) * GDN_HEAD_DIM]

    d_row = [jnp.broadcast_to(decays[ch][4 + h:5 + h, :], (c, c)) for ch, h in items]
    d_col = [d.T for d in d_row]
    beta_col = [jnp.broadcast_to(gates[ch][h:h + 1, :], (c, c)).T for ch, h in items]
    d_last = [d[:, c - 1:c] for d in d_row]
    gamma = [jnp.exp(jnp.where(tri, d_col[n] - d_row[n], MASKED_LOGIT)) for n in idx]
    exp_d = [jnp.exp(d_col[n]) for n in idx]
    k16 = [tokens(k_ref, n).astype(BF16) for n in idx]
    kb = [tokens(k_ref, n) * beta_col[n] for n in idx]
    nil = [jnp.where(strict, _dot_nt(kb[n].astype(BF16), k16[n]) * gamma[n], 0.0) for n in idx]
    block = 8
    same = lambda size: (row // size) == (col // size)
    diag = [jnp.where(same(block), x, 0.0) for x in nil]
    diag_parts = [_split_bf16(x) for x in diag]
    inv = [eye - x for x in diag]
    power = [_dot_split(x, x) for x in diag_parts]
    for it in range(2):
        power_parts = [_split_bf16(x) for x in power]
        inv = [inv[n] + _dot_split(_split_bf16(inv[n]), power_parts[n]) for n in idx]
        if it == 0:
            power = [_dot_split(x, x) for x in power_parts]
    while block < c:
        couples = same(2 * block) & ((row // block) % 2 == 1) & ((col // block) % 2 == 0)
        inv_parts = [_split_bf16(x) for x in inv]
        lower = [_dot_split(_split_bf16(jnp.where(couples, nil[n], 0.0)), inv_parts[n]) for n in idx]
        inv = [inv[n] - _dot_split(inv_parts[n], _split_bf16(lower[n])) for n in idx]
        block *= 2
    rhs = [jnp.concatenate([kb[n] * exp_d[n], tokens(v_ref, n) * beta_col[n]], axis=1) for n in idx]
    wu = [_dot_split(_split_bf16(inv[n]), _split_bf16(rhs[n])).astype(BF16) for n in idx]
    aqk = [jnp.where(tri, _dot_nt(tokens(q_ref, n).astype(BF16), k16[n]) * gamma[n], 0.0).astype(BF16)
           for n in idx]
    kd_t = [(tokens(k_ref, n) * jnp.exp(d_last[n] - d_col[n])).T.astype(BF16) for n in idx]
    state_wu = [_dot(kd_t[n], wu[n]) for n in idx]
    out_wu = [_dot(aqk[n], wu[n]) for n in idx]
    for n, (ch, h) in enumerate(items):
        m_ref[ch, h] = (-state_wu[n][:, :c]).astype(BF16)
        n_ref[ch, h] = state_wu[n][:, c:]
        p_ref[ch, h] = (tokens(q_ref, n) * exp_d[n] - out_wu[n][:, :c]).astype(BF16)
        r_ref[ch, h] = out_wu[n][:, c:]
        cd_ref[ch, h:h + 1, :] = jnp.exp(d_last[n][0:1, :] + jnp.zeros((1, c), F32))


def _gdn_chunks(qn, kn, vv, bg):
    n_rows = qn.shape[0]
    nc = n_rows // LANES
    per_step = next(d for d in GDN_CHUNKS_PER_STEP if nc % d == 0)
    tok = pl.BlockSpec((per_step * LANES, D_GDN), lambda c: (c, 0))
    mat = pl.BlockSpec((per_step, N_GDN_HEADS, LANES, LANES), lambda c: (c, 0, 0, 0))
    mat_shape = lambda dt: jax.ShapeDtypeStruct((nc, N_GDN_HEADS, LANES, LANES), dt)
    return pl.pallas_call(
        _gdn_chunk_kernel,
        out_shape=(mat_shape(BF16), mat_shape(F32), mat_shape(BF16), mat_shape(F32),
                   jax.ShapeDtypeStruct((nc, N_GDN_HEADS, LANES), F32)),
        grid=(nc // per_step,),
        in_specs=[tok, tok, tok, pl.BlockSpec((8, per_step * LANES), lambda c: (0, c))],
        out_specs=(mat, mat, mat, mat, pl.BlockSpec((per_step, N_GDN_HEADS, LANES), lambda c: (c, 0, 0))),
        compiler_params=_params("parallel"),
        name="gdn_chunks",
    )(qn, kn, vv, bg)


def _gdn_scan_kernel(m_ref, n_ref, p_ref, r_ref, cd_ref, o_ref, s_scr, *, batch):
    @pl.when(pl.program_id(0) == 0)
    def _():
        s_scr[...] = jnp.zeros(s_scr.shape, F32)

    for ch in range(m_ref.shape[1]):
        rows = slice(ch * LANES, (ch + 1) * LANES)
        for b in range(batch):
            for h in range(N_GDN_HEADS):
                s = s_scr[b, h]
                s16 = s.astype(BF16)
                o_ref[b, rows, h * GDN_HEAD_DIM:(h + 1) * GDN_HEAD_DIM] = _dot(p_ref[b, ch, h], s16) + r_ref[b, ch, h]
                s_scr[b, h] = s * cd_ref[b, ch, h:h + 1, :] + _dot(m_ref[b, ch, h], s16) + n_ref[b, ch, h]


def _gdn_scan(m_mat, n_mat, p_mat, r_mat, cd, batch, tp):
    nc = tp // LANES
    per_step = next(d for d in SCAN_CHUNKS if nc % d == 0)
    shape5 = lambda a: a.reshape(batch, nc, N_GDN_HEADS, LANES, LANES)
    mat = pl.BlockSpec((batch, per_step, N_GDN_HEADS, LANES, LANES), lambda c: (0, c, 0, 0, 0))
    return pl.pallas_call(
        functools.partial(_gdn_scan_kernel, batch=batch),
        out_shape=jax.ShapeDtypeStruct((batch, tp, D_GDN), F32),
        grid=(nc // per_step,),
        in_specs=[mat, mat, mat, mat,
                  pl.BlockSpec((batch, per_step, N_GDN_HEADS, LANES), lambda c: (0, c, 0, 0))],
        out_specs=pl.BlockSpec((batch, per_step * LANES, D_GDN), lambda c: (0, c, 0)),
        scratch_shapes=[pltpu.VMEM((batch, N_GDN_HEADS, LANES, LANES), F32)],
        compiler_params=_params("arbitrary"),
        name="gdn_scan",
    )(shape5(m_mat), shape5(n_mat), shape5(p_mat), shape5(r_mat), cd.reshape(batch, nc, N_GDN_HEADS, LANES))


def _out_proj_kernel(h_ref, oa_ref, za_ref, og_ref, zg_ref, gain_ref, w_ref, out_ref):
    attn = (oa_ref[...] * _silu(za_ref[...])).astype(BF16)
    og, zg = og_ref[...], zg_ref[...]
    heads = [slice(h * GDN_HEAD_DIM, (h + 1) * GDN_HEAD_DIM) for h in range(N_GDN_HEADS)]
    gated = jnp.concatenate([(_rms_rows(og[:, cols], gain_ref[...]) * _silu(zg[:, cols])).astype(BF16)
                             for cols in heads], axis=1)
    out_ref[...] = h_ref[...] + _dot(jnp.concatenate([attn, gated], axis=1), w_ref[...])


def _out_proj(h2, o_attn, proj, o_gdn, gain, w_out, batch, tp, first, rows):
    rb = _row_block(rows)
    n_blocks = rows // rb
    blk = lambda width, col: pl.BlockSpec((pl.Element(rb), pl.Element(width)),
                                          lambda b, j: (pl.multiple_of(b * tp + first + j * rb, HALO_ROWS), col * width))
    return pl.pallas_call(
        _out_proj_kernel,
        out_shape=jax.ShapeDtypeStruct((batch * rows, D_MODEL), F32),
        grid=(batch, n_blocks),
        in_specs=[blk(D_MODEL, 0), blk(D_ATTN, 0), blk(D_ATTN, COL_Z_ATTN), blk(D_GDN, 0),
                  blk(D_GDN, COL_Z_GDN),
                  pl.BlockSpec((1, GDN_HEAD_DIM), lambda b, j: (0, 0)),
                  pl.BlockSpec((D_MODEL, D_MODEL), lambda b, j: (0, 0))],
        out_specs=pl.BlockSpec((rb, D_MODEL), lambda b, j: (b * n_blocks + j, 0)),
        compiler_params=_params("parallel", "parallel"),
        name="out_proj",
    )(h2, o_attn, proj, o_gdn, proj, gain, w_out)


W_IN_PARTS = (("c_q", Q_RANK), ("c_kv", KV_RANK), ("k_idx", IDX_DIM), ("w_idx", N_IDX_HEADS),
              ("z_attn", D_ATTN), ("qkv_g", 3 * D_GDN), ("z_g", D_GDN), ("b", N_GDN_HEADS), ("a", N_GDN_HEADS))
W_PACKED_ORDER = ("qkv_g", "c_q", "c_kv", "k_idx", "k_idx", "z_attn", "z_g")


def _w_in_columns():
    cols, o = {}, 0
    for name, size in W_IN_PARTS:
        cols[name] = slice(o, o + size)
        o += size
    return cols, o


W_ROWS_ORDER = ("w_idx", "b", "a")


def _pack_w_in_kernel(w_ref, out_ref, narrow_ref):
    cols, _ = _w_in_columns()
    narrow_ref[...] = jnp.zeros(narrow_ref.shape, F32)
    o = 0
    for name in W_ROWS_ORDER:
        src = cols[name]
        size = src.stop - src.start
        narrow_ref[:, o:o + size] = w_ref[:, src]
        o += size
    o = 0
    for name in W_PACKED_ORDER:
        src = cols[name]
        size = src.stop - src.start
        out_ref[:, o:o + size] = w_ref[:, src].astype(BF16)
        o += size


def _pack_w_in(w_in, layer):
    _, width = _w_in_columns()
    rb = 2 * LANES
    return pl.pallas_call(
        _pack_w_in_kernel,
        out_shape=(jax.ShapeDtypeStruct((D_MODEL, D_PACKED), BF16),
                   jax.ShapeDtypeStruct((D_MODEL, LANES), F32)),
        grid=(D_MODEL // rb,),
        in_specs=[pl.BlockSpec((None, rb, width), lambda i: (layer, i, 0))],
        out_specs=(pl.BlockSpec((rb, D_PACKED), lambda i: (i, 0)),
                   pl.BlockSpec((rb, LANES), lambda i: (i, 0))),
        compiler_params=_params("parallel"),
        name="pack_w_in",
    )(w_in)


def _pack_layer(layer, norm_gain, w_in, cq_gain, ckv_gain, w_uq, w_ukv, w_q_idx, q_gain, k_gain, conv_w, a_log,
                dt_bias, gdn_gain, w_out):
    w_packed, narrow = _pack_w_in(w_in, layer)
    w_rows = narrow[:, :16].T.astype(BF16)
    return dict(
        gain=norm_gain[None, :], w_packed=w_packed, w_rows=w_rows,
        g_cq=cq_gain[None, :], g_ckv=ckv_gain[None, :],
        w_uqt=w_uq.T.astype(BF16), w_qit=w_q_idx.T.astype(BF16),
        w_ukt=w_ukv[:, :D_ATTN].T.astype(BF16), w_uvt=w_ukv[:, D_ATTN:].T.astype(BF16),
        g_q_col=q_gain[:, None], g_k_col=k_gain[:, None],
        conv_w=conv_w, a_log=a_log[:, None], dt_bias=dt_bias[:, None],
        gdn_gain=gdn_gain[None, :], w_out=w_out.astype(BF16))


def _layer(h2, lw, bias, tri, batch, tp, t_valid, topk, keep):
    rb = _row_block(tp)
    proj, rows = _in_proj(h2, lw["gain"], lw["w_packed"], lw["w_rows"], rb)
    qt, k, vt, qit, kidx = _dsa_prep(proj, lw, batch, tp, rb)
    o_attn = _dsa_attention(qit, rows, qt, kidx, k, vt, bias, tri, batch, tp, t_valid, topk)
    qn, kn, vv, bg = _gdn_prep(proj, rows, lw, batch, tp, rb)
    m_mat, n_mat, p_mat, r_mat, cd = _gdn_chunks(qn, kn, vv, bg)
    o_gdn = _gdn_scan(m_mat, n_mat, p_mat, r_mat, cd, batch, tp).reshape(batch * tp, D_GDN)
    return _out_proj(h2, o_attn, proj, o_gdn, lw["gdn_gain"], lw["w_out"], batch, tp, *keep)


def _forward(x, meta_tokens, rel_bias_table, layer_weights, topk):
    batch, seq, _ = x.shape
    t = seq + N_META
    tp = -(-t // KEY_TILE) * KEY_TILE
    meta = jnp.broadcast_to(meta_tokens[None].astype(x.dtype), (batch, N_META, D_MODEL))
    h = jnp.concatenate([meta, x, jnp.zeros((batch, tp - t, D_MODEL), x.dtype)], axis=1)
    h2 = h.reshape(batch * tp, D_MODEL)
    bias = _bias_tiles(rel_bias_table)
    tri = jnp.tril(jnp.ones((KEY_TILE, KEY_TILE), BF16))
    crop = seq % LANES == 0
    for depth, lw in enumerate(layer_weights):
        last = crop and depth == len(layer_weights) - 1
        h2 = _layer(h2, lw, bias, tri, batch, tp, t, topk, (N_META, seq) if last else (0, tp))
    if crop:
        return h2.reshape(batch, seq, D_MODEL)
    return h2.reshape(batch, tp, D_MODEL)[:, N_META:t]


def kernel(x, meta_tokens, rel_bias_table, norm_gain, w_in, cq_norm_gain, ckv_norm_gain, w_uq, w_ukv, w_q_idx,
           q_norm_gain, k_norm_gain, conv_w, a_log, dt_bias, gdn_norm_gain, w_out):
    depth = norm_gain.shape[0]
    topk = min(TOPK_MAX, x.shape[1] // 4)
    layers = [_pack_layer(l, norm_gain[l], w_in, cq_norm_gain[l], ckv_norm_gain[l], w_uq[l], w_ukv[l],
                          w_q_idx[l], q_norm_gain[l], k_norm_gain[l], conv_w[l], a_log[l], dt_bias[l],
                          gdn_norm_gain[l], w_out[l]) for l in range(depth)]
    return _forward(x, meta_tokens, rel_bias_table, layers, topk)
```
